```python
import jax, jax.numpy as jnp
from jax import lax
import numpy as np

D_MODEL = 1024
BATCH = 8
SEQ = 4096
DEPTH = 2

N_A_LAYERS = (DEPTH + 1) // 2
N_B_LAYERS = DEPTH - N_A_LAYERS
HGRN_DK = 128
HGRN_DV = 128
HGRN_HEADS = D_MODEL // HGRN_DK
HGRN_WIDTH = HGRN_HEADS * HGRN_DK
HGRN_CHUNK = 64
FOX_HEAD_DIM = 64
FOX_HEADS = D_MODEL // FOX_HEAD_DIM
FOX_WIDTH = FOX_HEADS * FOX_HEAD_DIM
FOX_QBLOCK = 128
FOX_GATE_BIAS_OFFSET = 3.0
D_FF = ((8 * D_MODEL // 3 + 255) // 256) * 256
PLE_DIM = 256
NORM_EPS = 1e-6

kernel_name = "yoco_hgrn2_fox_macaron_hybrid"


def rmsnorm(x, g):
    x32 = x.astype(jnp.float32)
    y = x32 * lax.rsqrt(jnp.mean(jnp.square(x32), axis=-1, keepdims=True) + NORM_EPS)
    return (y * g.astype(jnp.float32)).astype(x.dtype)


def swiglu(x, w_in, w_out):
    gate, up = jnp.split(x @ w_in, 2, axis=-1)
    return (jax.nn.silu(gate) * up) @ w_out


def hgrn2_chunked_scan(q, log_f, k, v):
    b_, s_, h_, dk = q.shape
    dv = v.shape[-1]
    n_chunks = s_ // HGRN_CHUNK

    def to_chunks(t):
        return t.astype(jnp.float32).reshape(b_, n_chunks, HGRN_CHUNK, h_, t.shape[-1]).transpose(1, 0, 3, 2, 4)

    qc, gc, kc, vc = to_chunks(q), to_chunks(log_f), to_chunks(k), to_chunks(v)
    causal = jnp.asarray(np.tril(np.ones((HGRN_CHUNK, HGRN_CHUNK), dtype=bool)))

    def step(state, inp):
        qb, gb, kb, vb = inp
        cum = jnp.cumsum(gb, axis=2)
        o_inter = jnp.einsum('bhtk,bhkv->bhtv', qb * jnp.exp(cum), state)
        rel = cum[:, :, :, None, :] - cum[:, :, None, :, :]
        decay = jnp.exp(jnp.where(causal[:, :, None], rel, -jnp.inf))
        scores = jnp.einsum('bhtk,bhtsk,bhsk->bhts', qb, decay, kb)
        o_intra = jnp.einsum('bhts,bhsv->bhtv', scores, vb)
        last = cum[:, :, -1, :]
        new_state = jnp.exp(last)[..., None] * state + jnp.einsum(
            'bhsk,bhsv->bhkv', kb * jnp.exp(last[:, :, None, :] - cum), vb)
        return new_state, o_inter + o_intra

    state0 = jnp.zeros((b_, h_, dk, dv), jnp.float32)
    _, out = lax.scan(step, state0, (qc, gc, kc, vc))
    return out.transpose(1, 0, 3, 2, 4).reshape(b_, s_, h_, dv)


def hgrn2_mixer(xn, w_in, lb, out_norm, w_out):
    b_, s_, _ = xn.shape
    q, fz, inp, g = jnp.split(xn @ w_in, 4, axis=-1)
    fz32 = fz.astype(jnp.float32)
    log_f = jnp.logaddexp(jnp.log(lb), jnp.log1p(-lb) + jax.nn.log_sigmoid(fz32))
    k = (1.0 - lb) * jax.nn.sigmoid(-fz32)
    heads = lambda t: t.reshape(b_, s_, HGRN_HEADS, -1)
    o = hgrn2_chunked_scan(heads(q), heads(log_f), heads(k), heads(inp))
    o = rmsnorm(o, out_norm).astype(xn.dtype).reshape(b_, s_, HGRN_WIDTH)
    return (o * jax.nn.silu(g)) @ w_out


def shared_kv(h, kv_norm, w_kvf, b_f):
    b_, s_, _ = h.shape
    kvf = rmsnorm(h, kv_norm) @ w_kvf
    k = kvf[..., :FOX_WIDTH].reshape(b_, s_, FOX_HEADS, FOX_HEAD_DIM)
    v = kvf[..., FOX_WIDTH:2 * FOX_WIDTH].reshape(b_, s_, FOX_HEADS, FOX_HEAD_DIM)
    log_f = jax.nn.log_sigmoid((kvf[..., 2 * FOX_WIDTH:] + b_f).astype(jnp.float32))
    c = jnp.cumsum(log_f, axis=1).transpose(0, 2, 1)
    return k, v, c


def forgetting_attention(q, k, v, c):
    s_ = q.shape[1]
    scale = FOX_HEAD_DIM ** -0.5
    outs = []
    for blk in range(s_ // FOX_QBLOCK):
        t0 = blk * FOX_QBLOCK
        t1 = t0 + FOX_QBLOCK
        logits = jnp.einsum('bthd,bshd->bhts', q[:, t0:t1], k[:, :t1]).astype(jnp.float32) * scale
        logits = logits + c[:, :, t0:t1, None] - c[:, :, None, :t1]
        mask = jnp.asarray((t0 + np.arange(FOX_QBLOCK))[:, None] >= np.arange(t1)[None, :])
        logits = jnp.where(mask, logits, -jnp.inf)
        probs = jax.nn.softmax(logits, axis=-1).astype(v.dtype)
        outs.append(jnp.einsum('bhts,bshd->bthd', probs, v[:, :t1]))
    return jnp.concatenate(outs, axis=1)


def fox_mixer(xn, w_qg, w_out, k, v, c):
    b_, s_, _ = xn.shape
    q, g = jnp.split(xn @ w_qg, 2, axis=-1)
    q = q.reshape(b_, s_, FOX_HEADS, FOX_HEAD_DIM)
    o = forgetting_attention(q, k, v, c).reshape(b_, s_, FOX_WIDTH)
    return (o * jax.nn.sigmoid(g)) @ w_out


def _fwd_setup_inputs(seed: int = 0) -> dict:
    key = jax.random.key(seed)
    ks = iter(jax.random.split(key, 32))
    nrm = lambda shape, scale: jax.random.normal(next(ks), shape, jnp.float32) * scale
    gain = lambda shape: 1.0 + nrm(shape, 0.1)
    return {
        "x": nrm((BATCH, SEQ, D_MODEL), 1.0),
        "p": nrm((DEPTH, BATCH, SEQ, PLE_DIM), 1.0),
        "ffn1_norm_pre": gain((DEPTH, D_MODEL)),
        "ffn1_w_in": nrm((DEPTH, D_MODEL, 2 * D_FF), D_MODEL ** -0.5),
        "ffn1_w_out": nrm((DEPTH, D_FF, D_MODEL), D_FF ** -0.5),
        "ffn1_norm_post": gain((DEPTH, D_MODEL)),
        "mix_norm_pre": gain((DEPTH, D_MODEL)),
        "mix_norm_post": gain((DEPTH, D_MODEL)),
        "ffn2_norm_pre": gain((DEPTH, D_MODEL)),
        "ffn2_w_in": nrm((DEPTH, D_MODEL, 2 * D_FF), D_MODEL ** -0.5),
        "ffn2_w_out": nrm((DEPTH, D_FF, D_MODEL), D_FF ** -0.5),
        "ffn2_norm_post": gain((DEPTH, D_MODEL)),
        "hgrn_w_in": nrm((N_A_LAYERS, D_MODEL, 4 * HGRN_WIDTH), D_MODEL ** -0.5),
        "hgrn_lb_logits": nrm((N_A_LAYERS + 1, HGRN_WIDTH), 0.5),
        "hgrn_out_norm": gain((N_A_LAYERS, HGRN_DV)),
        "hgrn_w_out": nrm((N_A_LAYERS, HGRN_WIDTH, D_MODEL), HGRN_WIDTH ** -0.5),
        "kv_norm": gain((D_MODEL,)),
        "fox_w_kvf": nrm((D_MODEL, 2 * FOX_WIDTH + FOX_HEADS), D_MODEL ** -0.5),
        "fox_b_f": FOX_GATE_BIAS_OFFSET + nrm((FOX_HEADS,), 0.1),
        "fox_w_qg": nrm((N_B_LAYERS, D_MODEL, 2 * FOX_WIDTH), D_MODEL ** -0.5),
        "fox_w_out": nrm((N_B_LAYERS, FOX_WIDTH, D_MODEL), FOX_WIDTH ** -0.5),
        "ple_norm_pre": gain((DEPTH, D_MODEL)),
        "ple_w_gate": nrm((DEPTH, D_MODEL, D_MODEL), D_MODEL ** -0.5),
        "ple_w_proj": nrm((DEPTH, PLE_DIM, D_MODEL), PLE_DIM ** -0.5),
        "ple_norm_post": gain((DEPTH, D_MODEL)),
    }


def _fwd_reference(x, p, ffn1_norm_pre, ffn1_w_in, ffn1_w_out, ffn1_norm_post,
              mix_norm_pre, mix_norm_post,
              ffn2_norm_pre, ffn2_w_in, ffn2_w_out, ffn2_norm_post,
              hgrn_w_in, hgrn_lb_logits, hgrn_out_norm, hgrn_w_out,
              kv_norm, fox_w_kvf, fox_b_f, fox_w_qg, fox_w_out,
              ple_norm_pre, ple_w_gate, ple_w_proj, ple_norm_post):
    lb_all = jnp.cumsum(jax.nn.softmax(hgrn_lb_logits.astype(jnp.float32), axis=0), axis=0)
    h = x
    k_sh = v_sh = c_sh = None
    for i in range(DEPTH):
        h = h + 0.5 * rmsnorm(swiglu(rmsnorm(h, ffn1_norm_pre[i]), ffn1_w_in[i], ffn1_w_out[i]), ffn1_norm_post[i])
        hn = rmsnorm(h, mix_norm_pre[i])
        if i < N_A_LAYERS:
            mix = hgrn2_mixer(hn, hgrn_w_in[i], lb_all[i], hgrn_out_norm[i], hgrn_w_out[i])
        else:
            j = i - N_A_LAYERS
            mix = fox_mixer(hn, fox_w_qg[j], fox_w_out[j], k_sh, v_sh, c_sh)
        h = h + rmsnorm(mix, mix_norm_post[i])
        h = h + 0.5 * rmsnorm(swiglu(rmsnorm(h, ffn2_norm_pre[i]), ffn2_w_in[i], ffn2_w_out[i]), ffn2_norm_post[i])
        gate = jax.nn.sigmoid(rmsnorm(h, ple_norm_pre[i]) @ ple_w_gate[i])
        h = h + rmsnorm(gate * (p[i] @ ple_w_proj[i]), ple_norm_post[i])
        if i == N_A_LAYERS - 1:
            k_sh, v_sh, c_sh = shared_kv(h, kv_norm, fox_w_kvf, fox_b_f)
    return h


import jax as _jax
import jax.numpy as _jnp

TWIN_FORMAT = 'train_step'
FWD_PARAMS = ['x', 'p', 'ffn1_norm_pre', 'ffn1_w_in', 'ffn1_w_out', 'ffn1_norm_post', 'mix_norm_pre', 'mix_norm_post', 'ffn2_norm_pre', 'ffn2_w_in', 'ffn2_w_out', 'ffn2_norm_post', 'hgrn_w_in', 'hgrn_lb_logits', 'hgrn_out_norm', 'hgrn_w_out', 'kv_norm', 'fox_w_kvf', 'fox_b_f', 'fox_w_qg', 'fox_w_out', 'ple_norm_pre', 'ple_w_gate', 'ple_w_proj', 'ple_norm_post']
TWIN_WEIGHTS = ['ffn1_norm_pre', 'ffn1_w_in', 'ffn1_w_out', 'ffn1_norm_post', 'mix_norm_pre', 'mix_norm_post', 'ffn2_norm_pre', 'ffn2_w_in', 'ffn2_w_out', 'ffn2_norm_post', 'hgrn_w_in', 'hgrn_lb_logits', 'hgrn_out_norm', 'hgrn_w_out', 'kv_norm', 'fox_w_kvf', 'fox_b_f', 'fox_w_qg', 'fox_w_out', 'ple_norm_pre', 'ple_w_gate', 'ple_w_proj', 'ple_norm_post']
TWIN_DIFF_INPUT = 'x'
TWIN_INPUTS = ['x', 'p', 'ffn1_norm_pre', 'ffn1_w_in', 'ffn1_w_out', 'ffn1_norm_post', 'mix_norm_pre', 'mix_norm_post', 'ffn2_norm_pre', 'ffn2_w_in', 'ffn2_w_out', 'ffn2_norm_post', 'hgrn_w_in', 'hgrn_lb_logits', 'hgrn_out_norm', 'hgrn_w_out', 'kv_norm', 'fox_w_kvf', 'fox_b_f', 'fox_w_qg', 'fox_w_out', 'ple_norm_pre', 'ple_w_gate', 'ple_w_proj', 'ple_norm_post', 'loss_target', 'm_ffn1_norm_pre', 'm_ffn1_w_in', 'm_ffn1_w_out', 'm_ffn1_norm_post', 'm_mix_norm_pre', 'm_mix_norm_post', 'm_ffn2_norm_pre', 'm_ffn2_w_in', 'm_ffn2_w_out', 'm_ffn2_norm_post', 'm_hgrn_w_in', 'm_hgrn_lb_logits', 'm_hgrn_out_norm', 'm_hgrn_w_out', 'm_kv_norm', 'm_fox_w_kvf', 'm_fox_b_f', 'm_fox_w_qg', 'm_fox_w_out', 'm_ple_norm_pre', 'm_ple_w_gate', 'm_ple_w_proj', 'm_ple_norm_post', 'v_ffn1_norm_pre', 'v_ffn1_w_in', 'v_ffn1_w_out', 'v_ffn1_norm_post', 'v_mix_norm_pre', 'v_mix_norm_post', 'v_ffn2_norm_pre', 'v_ffn2_w_in', 'v_ffn2_w_out', 'v_ffn2_norm_post', 'v_hgrn_w_in', 'v_hgrn_lb_logits', 'v_hgrn_out_norm', 'v_hgrn_w_out', 'v_kv_norm', 'v_fox_w_kvf', 'v_fox_b_f', 'v_fox_w_qg', 'v_fox_w_out', 'v_ple_norm_pre', 'v_ple_w_gate', 'v_ple_w_proj', 'v_ple_norm_post']
TWIN_OUTPUTS = ['loss', 'grad_x', 'grad_ffn1_norm_pre', 'grad_ffn1_w_in', 'grad_ffn1_w_out', 'grad_ffn1_norm_post', 'grad_mix_norm_pre', 'grad_mix_norm_post', 'grad_ffn2_norm_pre', 'grad_ffn2_w_in', 'grad_ffn2_w_out', 'grad_ffn2_norm_post', 'grad_hgrn_w_in', 'grad_hgrn_lb_logits', 'grad_hgrn_out_norm', 'grad_hgrn_w_out', 'grad_kv_norm', 'grad_fox_w_kvf', 'grad_fox_b_f', 'grad_fox_w_qg', 'grad_fox_w_out', 'grad_ple_norm_pre', 'grad_ple_w_gate', 'grad_ple_w_proj', 'grad_ple_norm_post', 'delta_ffn1_norm_pre', 'delta_ffn1_w_in', 'delta_ffn1_w_out', 'delta_ffn1_norm_post', 'delta_mix_norm_pre', 'delta_mix_norm_post', 'delta_ffn2_norm_pre', 'delta_ffn2_w_in', 'delta_ffn2_w_out', 'delta_ffn2_norm_post', 'delta_hgrn_w_in', 'delta_hgrn_lb_logits', 'delta_hgrn_out_norm', 'delta_hgrn_w_out', 'delta_kv_norm', 'delta_fox_w_kvf', 'delta_fox_b_f', 'delta_fox_w_qg', 'delta_fox_w_out', 'delta_ple_norm_pre', 'delta_ple_w_gate', 'delta_ple_w_proj', 'delta_ple_norm_post', 'new_m_ffn1_norm_pre', 'new_m_ffn1_w_in', 'new_m_ffn1_w_out', 'new_m_ffn1_norm_post', 'new_m_mix_norm_pre', 'new_m_mix_norm_post', 'new_m_ffn2_norm_pre', 'new_m_ffn2_w_in', 'new_m_ffn2_w_out', 'new_m_ffn2_norm_post', 'new_m_hgrn_w_in', 'new_m_hgrn_lb_logits', 'new_m_hgrn_out_norm', 'new_m_hgrn_w_out', 'new_m_kv_norm', 'new_m_fox_w_kvf', 'new_m_fox_b_f', 'new_m_fox_w_qg', 'new_m_fox_w_out', 'new_m_ple_norm_pre', 'new_m_ple_w_gate', 'new_m_ple_w_proj', 'new_m_ple_norm_post', 'new_v_ffn1_norm_pre', 'new_v_ffn1_w_in', 'new_v_ffn1_w_out', 'new_v_ffn1_norm_post', 'new_v_mix_norm_pre', 'new_v_mix_norm_post', 'new_v_ffn2_norm_pre', 'new_v_ffn2_w_in', 'new_v_ffn2_w_out', 'new_v_ffn2_norm_post', 'new_v_hgrn_w_in', 'new_v_hgrn_lb_logits', 'new_v_hgrn_out_norm', 'new_v_hgrn_w_out', 'new_v_kv_norm', 'new_v_fox_w_kvf', 'new_v_fox_b_f', 'new_v_fox_w_qg', 'new_v_fox_w_out', 'new_v_ple_norm_pre', 'new_v_ple_w_gate', 'new_v_ple_w_proj', 'new_v_ple_norm_post']
TWIN_LEAF_KINDS = {'loss': 'loss', 'grad_x': 'grad_x', 'grad_ffn1_norm_pre': 'grad_w', 'grad_ffn1_w_in': 'grad_w', 'grad_ffn1_w_out': 'grad_w', 'grad_ffn1_norm_post': 'grad_w', 'grad_mix_norm_pre': 'grad_w', 'grad_mix_norm_post': 'grad_w', 'grad_ffn2_norm_pre': 'grad_w', 'grad_ffn2_w_in': 'grad_w', 'grad_ffn2_w_out': 'grad_w', 'grad_ffn2_norm_post': 'grad_w', 'grad_hgrn_w_in': 'grad_w', 'grad_hgrn_lb_logits': 'grad_w', 'grad_hgrn_out_norm': 'grad_w', 'grad_hgrn_w_out': 'grad_w', 'grad_kv_norm': 'grad_w', 'grad_fox_w_kvf': 'grad_w', 'grad_fox_b_f': 'grad_w', 'grad_fox_w_qg': 'grad_w', 'grad_fox_w_out': 'grad_w', 'grad_ple_norm_pre': 'grad_w', 'grad_ple_w_gate': 'grad_w', 'grad_ple_w_proj': 'grad_w', 'grad_ple_norm_post': 'grad_w', 'delta_ffn1_norm_pre': 'delta_w', 'delta_ffn1_w_in': 'delta_w', 'delta_ffn1_w_out': 'delta_w', 'delta_ffn1_norm_post': 'delta_w', 'delta_mix_norm_pre': 'delta_w', 'delta_mix_norm_post': 'delta_w', 'delta_ffn2_norm_pre': 'delta_w', 'delta_ffn2_w_in': 'delta_w', 'delta_ffn2_w_out': 'delta_w', 'delta_ffn2_norm_post': 'delta_w', 'delta_hgrn_w_in': 'delta_w', 'delta_hgrn_lb_logits': 'delta_w', 'delta_hgrn_out_norm': 'delta_w', 'delta_hgrn_w_out': 'delta_w', 'delta_kv_norm': 'delta_w', 'delta_fox_w_kvf': 'delta_w', 'delta_fox_b_f': 'delta_w', 'delta_fox_w_qg': 'delta_w', 'delta_fox_w_out': 'delta_w', 'delta_ple_norm_pre': 'delta_w', 'delta_ple_w_gate': 'delta_w', 'delta_ple_w_proj': 'delta_w', 'delta_ple_norm_post': 'delta_w', 'new_m_ffn1_norm_pre': 'new_m', 'new_m_ffn1_w_in': 'new_m', 'new_m_ffn1_w_out': 'new_m', 'new_m_ffn1_norm_post': 'new_m', 'new_m_mix_norm_pre': 'new_m', 'new_m_mix_norm_post': 'new_m', 'new_m_ffn2_norm_pre': 'new_m', 'new_m_ffn2_w_in': 'new_m', 'new_m_ffn2_w_out': 'new_m', 'new_m_ffn2_norm_post': 'new_m', 'new_m_hgrn_w_in': 'new_m', 'new_m_hgrn_lb_logits': 'new_m', 'new_m_hgrn_out_norm': 'new_m', 'new_m_hgrn_w_out': 'new_m', 'new_m_kv_norm': 'new_m', 'new_m_fox_w_kvf': 'new_m', 'new_m_fox_b_f': 'new_m', 'new_m_fox_w_qg': 'new_m', 'new_m_fox_w_out': 'new_m', 'new_m_ple_norm_pre': 'new_m', 'new_m_ple_w_gate': 'new_m', 'new_m_ple_w_proj': 'new_m', 'new_m_ple_norm_post': 'new_m', 'new_v_ffn1_norm_pre': 'new_v', 'new_v_ffn1_w_in': 'new_v', 'new_v_ffn1_w_out': 'new_v', 'new_v_ffn1_norm_post': 'new_v', 'new_v_mix_norm_pre': 'new_v', 'new_v_mix_norm_post': 'new_v', 'new_v_ffn2_norm_pre': 'new_v', 'new_v_ffn2_w_in': 'new_v', 'new_v_ffn2_w_out': 'new_v', 'new_v_ffn2_norm_post': 'new_v', 'new_v_hgrn_w_in': 'new_v', 'new_v_hgrn_lb_logits': 'new_v', 'new_v_hgrn_out_norm': 'new_v', 'new_v_hgrn_w_out': 'new_v', 'new_v_kv_norm': 'new_v', 'new_v_fox_w_kvf': 'new_v', 'new_v_fox_b_f': 'new_v', 'new_v_fox_w_qg': 'new_v', 'new_v_fox_w_out': 'new_v', 'new_v_ple_norm_pre': 'new_v', 'new_v_ple_w_gate': 'new_v', 'new_v_ple_w_proj': 'new_v', 'new_v_ple_norm_post': 'new_v'}


def _forward(args):
    return _fwd_reference(*[args[k] for k in FWD_PARAMS])


def _output_shape():
    out = _jax.eval_shape(lambda: _forward(_fwd_setup_inputs(0)))
    return out.shape, out.dtype

N_MICROBATCH = 1
ADAM_LR = 0.001
ADAM_B1 = 0.9
ADAM_B2 = 0.999
ADAM_EPS = 1e-08
ADAM_WD = 0.01
ADAM_STEP = 10
PER_EXAMPLE_BATCH_AXIS = {'x': 0, 'p': 1, 'loss_target': 0}
SHARED_INPUTS = []
_WEIGHT_DTYPES = {'ffn1_norm_pre': _jnp.float32, 'ffn1_w_in': _jnp.float32, 'ffn1_w_out': _jnp.float32, 'ffn1_norm_post': _jnp.float32, 'mix_norm_pre': _jnp.float32, 'mix_norm_post': _jnp.float32, 'ffn2_norm_pre': _jnp.float32, 'ffn2_w_in': _jnp.float32, 'ffn2_w_out': _jnp.float32, 'ffn2_norm_post': _jnp.float32, 'hgrn_w_in': _jnp.float32, 'hgrn_lb_logits': _jnp.float32, 'hgrn_out_norm': _jnp.float32, 'hgrn_w_out': _jnp.float32, 'kv_norm': _jnp.float32, 'fox_w_kvf': _jnp.float32, 'fox_b_f': _jnp.float32, 'fox_w_qg': _jnp.float32, 'fox_w_out': _jnp.float32, 'ple_norm_pre': _jnp.float32, 'ple_w_gate': _jnp.float32, 'ple_w_proj': _jnp.float32, 'ple_norm_post': _jnp.float32}
MOMENT_SCALE = {'ffn1_norm_pre': 8.786457e-01, 'ffn1_w_in': 3.850924e-01, 'ffn1_w_out': 6.366045e-01, 'ffn1_norm_post': 7.520194e+00, 'mix_norm_pre': 1.325470e+00, 'mix_norm_post': 3.206208e+01, 'ffn2_norm_pre': 4.489275e-01, 'ffn2_w_in': 2.034919e-01, 'ffn2_w_out': 3.642102e-01, 'ffn2_norm_post': 7.958937e+00, 'hgrn_w_in': 8.128262e-01, 'hgrn_lb_logits': 4.381194e-01, 'hgrn_out_norm': 2.228670e+00, 'hgrn_w_out': 7.893491e-01, 'kv_norm': 7.756568e-01, 'fox_w_kvf': 5.155389e-01, 'fox_b_f': 6.499353e+00, 'fox_w_qg': 3.408586e-01, 'fox_w_out': 5.825749e-01, 'ple_norm_pre': 2.373164e-01, 'ple_w_gate': 2.247658e-01, 'ple_w_proj': 6.281056e-01, 'ple_norm_post': 3.228490e+01}


def _to_microbatches(a, axis):
    t = _jnp.moveaxis(a, axis, 0)
    t = t.reshape((N_MICROBATCH, t.shape[0] // N_MICROBATCH) + t.shape[1:])
    return _jnp.moveaxis(t, 1, axis + 1)


def setup_inputs(seed: int = 0) -> dict:
    inp = _fwd_setup_inputs(seed)
    key = _jax.random.fold_in(_jax.random.key(seed), 7919)
    shape, _ = _output_shape()
    out = dict(inp)
    out["loss_target"] = _jax.random.normal(_jax.random.fold_in(key, 0), shape, _jnp.float32)
    for i, name in enumerate(TWIN_WEIGHTS):
        w = inp[name].astype(_jnp.float32)
        if MOMENT_SCALE is None:
            s = _jnp.sqrt(_jnp.mean(_jnp.square(w)) + 1e-30)
        else:
            s = MOMENT_SCALE[name]
        km, kv = _jax.random.split(_jax.random.fold_in(key, i + 1))
        out[name] = w
        out["m_" + name] = s * _jax.random.normal(km, w.shape, _jnp.float32)
        out["v_" + name] = (s * s) * _jax.random.uniform(kv, w.shape, _jnp.float32, 0.5, 1.5)
    if N_MICROBATCH > 1:
        for name, axis in PER_EXAMPLE_BATCH_AXIS.items():
            out[name] = _to_microbatches(out[name], axis)
    return {'x': out['x'], 'p': out['p'], 'ffn1_norm_pre': out['ffn1_norm_pre'], 'ffn1_w_in': out['ffn1_w_in'], 'ffn1_w_out': out['ffn1_w_out'], 'ffn1_norm_post': out['ffn1_norm_post'], 'mix_norm_pre': out['mix_norm_pre'], 'mix_norm_post': out['mix_norm_post'], 'ffn2_norm_pre': out['ffn2_norm_pre'], 'ffn2_w_in': out['ffn2_w_in'], 'ffn2_w_out': out['ffn2_w_out'], 'ffn2_norm_post': out['ffn2_norm_post'], 'hgrn_w_in': out['hgrn_w_in'], 'hgrn_lb_logits': out['hgrn_lb_logits'], 'hgrn_out_norm': out['hgrn_out_norm'], 'hgrn_w_out': out['hgrn_w_out'], 'kv_norm': out['kv_norm'], 'fox_w_kvf': out['fox_w_kvf'], 'fox_b_f': out['fox_b_f'], 'fox_w_qg': out['fox_w_qg'], 'fox_w_out': out['fox_w_out'], 'ple_norm_pre': out['ple_norm_pre'], 'ple_w_gate': out['ple_w_gate'], 'ple_w_proj': out['ple_w_proj'], 'ple_norm_post': out['ple_norm_post'], 'loss_target': out['loss_target'], 'm_ffn1_norm_pre': out['m_ffn1_norm_pre'], 'm_ffn1_w_in': out['m_ffn1_w_in'], 'm_ffn1_w_out': out['m_ffn1_w_out'], 'm_ffn1_norm_post': out['m_ffn1_norm_post'], 'm_mix_norm_pre': out['m_mix_norm_pre'], 'm_mix_norm_post': out['m_mix_norm_post'], 'm_ffn2_norm_pre': out['m_ffn2_norm_pre'], 'm_ffn2_w_in': out['m_ffn2_w_in'], 'm_ffn2_w_out': out['m_ffn2_w_out'], 'm_ffn2_norm_post': out['m_ffn2_norm_post'], 'm_hgrn_w_in': out['m_hgrn_w_in'], 'm_hgrn_lb_logits': out['m_hgrn_lb_logits'], 'm_hgrn_out_norm': out['m_hgrn_out_norm'], 'm_hgrn_w_out': out['m_hgrn_w_out'], 'm_kv_norm': out['m_kv_norm'], 'm_fox_w_kvf': out['m_fox_w_kvf'], 'm_fox_b_f': out['m_fox_b_f'], 'm_fox_w_qg': out['m_fox_w_qg'], 'm_fox_w_out': out['m_fox_w_out'], 'm_ple_norm_pre': out['m_ple_norm_pre'], 'm_ple_w_gate': out['m_ple_w_gate'], 'm_ple_w_proj': out['m_ple_w_proj'], 'm_ple_norm_post': out['m_ple_norm_post'], 'v_ffn1_norm_pre': out['v_ffn1_norm_pre'], 'v_ffn1_w_in': out['v_ffn1_w_in'], 'v_ffn1_w_out': out['v_ffn1_w_out'], 'v_ffn1_norm_post': out['v_ffn1_norm_post'], 'v_mix_norm_pre': out['v_mix_norm_pre'], 'v_mix_norm_post': out['v_mix_norm_post'], 'v_ffn2_norm_pre': out['v_ffn2_norm_pre'], 'v_ffn2_w_in': out['v_ffn2_w_in'], 'v_ffn2_w_out': out['v_ffn2_w_out'], 'v_ffn2_norm_post': out['v_ffn2_norm_post'], 'v_hgrn_w_in': out['v_hgrn_w_in'], 'v_hgrn_lb_logits': out['v_hgrn_lb_logits'], 'v_hgrn_out_norm': out['v_hgrn_out_norm'], 'v_hgrn_w_out': out['v_hgrn_w_out'], 'v_kv_norm': out['v_kv_norm'], 'v_fox_w_kvf': out['v_fox_w_kvf'], 'v_fox_b_f': out['v_fox_b_f'], 'v_fox_w_qg': out['v_fox_w_qg'], 'v_fox_w_out': out['v_fox_w_out'], 'v_ple_norm_pre': out['v_ple_norm_pre'], 'v_ple_w_gate': out['v_ple_w_gate'], 'v_ple_w_proj': out['v_ple_w_proj'], 'v_ple_norm_post': out['v_ple_norm_post']}


def _loss(weights, diff, rest, loss_target):
    with _jax.named_scope("forward"):
        args = {**rest, TWIN_DIFF_INPUT: diff, **{k: w.astype(_WEIGHT_DTYPES[k]) for k, w in weights.items()}}
        y = _forward(args)
    with _jax.named_scope("loss_head"):
        err = _jnp.square(y.astype(_jnp.float32) - loss_target)
        return 0.5 * _jnp.sum(_jnp.mean(err, axis=-1)) if err.ndim else 0.5 * err


def _adamw(w, g, m, v):
    m = ADAM_B1 * m + (1.0 - ADAM_B1) * g
    v = ADAM_B2 * v + (1.0 - ADAM_B2) * _jnp.square(g)
    m_hat = m / (1.0 - ADAM_B1 ** ADAM_STEP)
    v_hat = v / (1.0 - ADAM_B2 ** ADAM_STEP)
    delta = -ADAM_LR * (m_hat / (_jnp.sqrt(v_hat) + ADAM_EPS) + ADAM_WD * w)
    return delta, m, v


def reference(x, p, ffn1_norm_pre, ffn1_w_in, ffn1_w_out, ffn1_norm_post, mix_norm_pre, mix_norm_post, ffn2_norm_pre, ffn2_w_in, ffn2_w_out, ffn2_norm_post, hgrn_w_in, hgrn_lb_logits, hgrn_out_norm, hgrn_w_out, kv_norm, fox_w_kvf, fox_b_f, fox_w_qg, fox_w_out, ple_norm_pre, ple_w_gate, ple_w_proj, ple_norm_post, loss_target, m_ffn1_norm_pre, m_ffn1_w_in, m_ffn1_w_out, m_ffn1_norm_post, m_mix_norm_pre, m_mix_norm_post, m_ffn2_norm_pre, m_ffn2_w_in, m_ffn2_w_out, m_ffn2_norm_post, m_hgrn_w_in, m_hgrn_lb_logits, m_hgrn_out_norm, m_hgrn_w_out, m_kv_norm, m_fox_w_kvf, m_fox_b_f, m_fox_w_qg, m_fox_w_out, m_ple_norm_pre, m_ple_w_gate, m_ple_w_proj, m_ple_norm_post, v_ffn1_norm_pre, v_ffn1_w_in, v_ffn1_w_out, v_ffn1_norm_post, v_mix_norm_pre, v_mix_norm_post, v_ffn2_norm_pre, v_ffn2_w_in, v_ffn2_w_out, v_ffn2_norm_post, v_hgrn_w_in, v_hgrn_lb_logits, v_hgrn_out_norm, v_hgrn_w_out, v_kv_norm, v_fox_w_kvf, v_fox_b_f, v_fox_w_qg, v_fox_w_out, v_ple_norm_pre, v_ple_w_gate, v_ple_w_proj, v_ple_norm_post):
    given = dict(x=x, p=p, ffn1_norm_pre=ffn1_norm_pre, ffn1_w_in=ffn1_w_in, ffn1_w_out=ffn1_w_out, ffn1_norm_post=ffn1_norm_post, mix_norm_pre=mix_norm_pre, mix_norm_post=mix_norm_post, ffn2_norm_pre=ffn2_norm_pre, ffn2_w_in=ffn2_w_in, ffn2_w_out=ffn2_w_out, ffn2_norm_post=ffn2_norm_post, hgrn_w_in=hgrn_w_in, hgrn_lb_logits=hgrn_lb_logits, hgrn_out_norm=hgrn_out_norm, hgrn_w_out=hgrn_w_out, kv_norm=kv_norm, fox_w_kvf=fox_w_kvf, fox_b_f=fox_b_f, fox_w_qg=fox_w_qg, fox_w_out=fox_w_out, ple_norm_pre=ple_norm_pre, ple_w_gate=ple_w_gate, ple_w_proj=ple_w_proj, ple_norm_post=ple_norm_post, loss_target=loss_target, m_ffn1_norm_pre=m_ffn1_norm_pre, m_ffn1_w_in=m_ffn1_w_in, m_ffn1_w_out=m_ffn1_w_out, m_ffn1_norm_post=m_ffn1_norm_post, m_mix_norm_pre=m_mix_norm_pre, m_mix_norm_post=m_mix_norm_post, m_ffn2_norm_pre=m_ffn2_norm_pre, m_ffn2_w_in=m_ffn2_w_in, m_ffn2_w_out=m_ffn2_w_out, m_ffn2_norm_post=m_ffn2_norm_post, m_hgrn_w_in=m_hgrn_w_in, m_hgrn_lb_logits=m_hgrn_lb_logits, m_hgrn_out_norm=m_hgrn_out_norm, m_hgrn_w_out=m_hgrn_w_out, m_kv_norm=m_kv_norm, m_fox_w_kvf=m_fox_w_kvf, m_fox_b_f=m_fox_b_f, m_fox_w_qg=m_fox_w_qg, m_fox_w_out=m_fox_w_out, m_ple_norm_pre=m_ple_norm_pre, m_ple_w_gate=m_ple_w_gate, m_ple_w_proj=m_ple_w_proj, m_ple_norm_post=m_ple_norm_post, v_ffn1_norm_pre=v_ffn1_norm_pre, v_ffn1_w_in=v_ffn1_w_in, v_ffn1_w_out=v_ffn1_w_out, v_ffn1_norm_post=v_ffn1_norm_post, v_mix_norm_pre=v_mix_norm_pre, v_mix_norm_post=v_mix_norm_post, v_ffn2_norm_pre=v_ffn2_norm_pre, v_ffn2_w_in=v_ffn2_w_in, v_ffn2_w_out=v_ffn2_w_out, v_ffn2_norm_post=v_ffn2_norm_post, v_hgrn_w_in=v_hgrn_w_in, v_hgrn_lb_logits=v_hgrn_lb_logits, v_hgrn_out_norm=v_hgrn_out_norm, v_hgrn_w_out=v_hgrn_w_out, v_kv_norm=v_kv_norm, v_fox_w_kvf=v_fox_w_kvf, v_fox_b_f=v_fox_b_f, v_fox_w_qg=v_fox_w_qg, v_fox_w_out=v_fox_w_out, v_ple_norm_pre=v_ple_norm_pre, v_ple_w_gate=v_ple_w_gate, v_ple_w_proj=v_ple_w_proj, v_ple_norm_post=v_ple_norm_post)
    weights = {n: given[n] for n in TWIN_WEIGHTS}
    shared = {n: given[n] for n in SHARED_INPUTS}
    per_example = {n: given[n] for n in ['x', 'p']}
    grad_fn = _jax.value_and_grad(_loss, argnums=(0, 1))

    def one_microbatch(ex, loss_target):
        ex = dict(ex)
        diff = ex.pop(TWIN_DIFF_INPUT)
        return grad_fn(weights, diff, {**shared, **ex}, loss_target)

    if N_MICROBATCH == 1:
        loss, (grad_w, grad_x) = one_microbatch(per_example, given["loss_target"])
    else:
        def body(carry, xs):
            loss_sum, grad_sum = carry
            l_k, (gw_k, gx_k) = one_microbatch(xs[0], xs[1])
            with _jax.named_scope("update"):
                return (loss_sum + l_k, _jax.tree.map(_jnp.add, grad_sum, gw_k)), gx_k

        init = (_jnp.zeros((), _jnp.float32), _jax.tree.map(_jnp.zeros_like, weights))
        (loss, grad_w), grad_x = _jax.lax.scan(body, init, (per_example, given["loss_target"]))
    with _jax.named_scope("update"):
        delta_w, new_m, new_v = {}, {}, {}
        for n in TWIN_WEIGHTS:
            delta_w[n], new_m[n], new_v[n] = _adamw(weights[n], grad_w[n], given["m_" + n], given["v_" + n])
    return (loss, grad_x, *[grad_w[n] for n in TWIN_WEIGHTS], *[delta_w[n] for n in TWIN_WEIGHTS],
            *[new_m[n] for n in TWIN_WEIGHTS], *[new_v[n] for n in TWIN_WEIGHTS])
```

```python
import functools

import jax
import jax.numpy as jnp
from jax import lax
from jax.experimental import pallas as pl
from jax.experimental.pallas import tpu as pltpu

F32 = jnp.float32
BF = jnp.bfloat16
NORM_EPS = 1e-6
N_DEV = 8
HGRN_DK = 128
HGRN_CHUNK = 16
FOX_HEAD_DIM = 64
LANES = 128
ADAM_LR, ADAM_B1, ADAM_B2, ADAM_EPS, ADAM_WD, ADAM_STEP = 0.001, 0.9, 0.999, 1e-08, 0.01, 10
VMEM_LIMIT = 56 * 1024 * 1024
HI = lax.Precision.HIGHEST
NT = (((1,), (1,)), ((), ()))
TN = (((0,), (0,)), ((), ()))


def _params(n_axes):
    return pltpu.CompilerParams(dimension_semantics=("arbitrary",) * n_axes, vmem_limit_bytes=VMEM_LIMIT)


def _tile(n, want):
    t = min(n, want)
    while n % t:
        t //= 2
    return t


def _sigmoid(x):
    return 1.0 / (1.0 + jnp.exp(-x))


def _rms(x):
    r = lax.rsqrt(jnp.mean(x * x, axis=-1, keepdims=True) + NORM_EPS)
    return x * r, r


def _norm_bwd(dy, xhat, r, g):
    dxh = dy * g
    return r * (dxh - xhat * jnp.mean(dxh * xhat, axis=-1, keepdims=True))


def _colsum(x):
    return jnp.sum(x, axis=0, keepdims=True)


def _w_spec(w, layer, blk):
    if w.ndim == 4:
        return lambda off: pl.BlockSpec((None, None, w.shape[2], w.shape[3]), lambda i, j: (layer, j + off, 0, 0))
    return lambda off: pl.BlockSpec((w.shape[0], blk), lambda i, j: (0, j + off))


def norm_mm_swiglu(h, g, w4, layer, name):
    T, D = h.shape
    _, nb, _, cs = w4.shape
    nh = nb // 2
    tm = _tile(T, 1024)

    def body(h_ref, g_ref, wg_ref, wu_ref, gate_ref, up_ref, a_ref, xn_ref):
        @pl.when(pl.program_id(1) == 0)
        def _():
            xh, _ = _rms(h_ref[...])
            xn_ref[...] = (xh * g_ref[...]).astype(BF)

        xn = xn_ref[...]
        gt = jnp.dot(xn, wg_ref[...], preferred_element_type=F32)
        up = jnp.dot(xn, wu_ref[...], preferred_element_type=F32)
        gate_ref[...] = gt.astype(BF)
        up_ref[...] = up.astype(BF)
        a_ref[...] = (gt * _sigmoid(gt) * up).astype(BF)

    ws = _w_spec(w4, layer, cs)
    blk = pl.BlockSpec((None, tm, cs), lambda i, j: (j, i, 0))
    shp = jax.ShapeDtypeStruct((nh, T, cs), BF)
    return pl.pallas_call(
        body, name=name, grid=(T // tm, nh),
        in_specs=[pl.BlockSpec((tm, D), lambda i, j: (i, 0)), pl.BlockSpec((1, D), lambda i, j: (0, 0)), ws(0), ws(nh)],
        out_specs=[blk, blk, blk], out_shape=[shp, shp, shp],
        scratch_shapes=[pltpu.VMEM((tm, D), BF)], compiler_params=_params(2),
    )(h, g, w4, w4)


def norm_mm(h, g, w, layer, name, tn=None):
    T, D = h.shape
    if w.ndim == 4:
        nb, cs = w.shape[1], w.shape[3]
    else:
        cs = tn
        nb = w.shape[1] // cs
    tm = _tile(T, 1024)

    def body(h_ref, g_ref, w_ref, z_ref, xn_ref):
        @pl.when(pl.program_id(1) == 0)
        def _():
            xh, _ = _rms(h_ref[...])
            xn_ref[...] = (xh * g_ref[...]).astype(BF)

        z_ref[...] = jnp.dot(xn_ref[...], w_ref[...], preferred_element_type=F32)

    return pl.pallas_call(
        body, name=name, grid=(T // tm, nb),
        in_specs=[pl.BlockSpec((tm, D), lambda i, j: (i, 0)), pl.BlockSpec((1, D), lambda i, j: (0, 0)),
                  _w_spec(w, layer, cs)(0)],
        out_specs=pl.BlockSpec((tm, cs), lambda i, j: (i, j)), out_shape=jax.ShapeDtypeStruct((T, nb * cs), F32),
        scratch_shapes=[pltpu.VMEM((tm, D), BF)], compiler_params=_params(2),
    )(h, g, w)


def _x_spec(x, tm, kb):
    if x.ndim == 3:
        return pl.BlockSpec((None, tm, x.shape[2]), lambda i, j: (j, i, 0))
    return pl.BlockSpec((tm, kb), lambda i, j: (i, j))


def mm_norm_res(x, w3, layer, g, h, coef, name, kb=None):
    T, D = h.shape
    if x.ndim == 3:
        nb, kb = x.shape[0], x.shape[2]
    else:
        nb = x.shape[1] // kb
    tm = _tile(T, 512)

    def body(x_ref, w_ref, h_ref, g_ref, hn_ref, y_ref, acc_ref):
        b = pl.program_id(1)

        @pl.when(b == 0)
        def _():
            acc_ref[...] = jnp.zeros_like(acc_ref)

        acc_ref[...] += jnp.dot(x_ref[...], w_ref[...], preferred_element_type=F32)

        @pl.when(b == nb - 1)
        def _():
            y = acc_ref[...]
            y_ref[...] = y
            yh, _ = _rms(y)
            hn_ref[...] = h_ref[...] + coef * (yh * g_ref[...])

    tok = pl.BlockSpec((tm, D), lambda i, j: (i, 0))
    shp = jax.ShapeDtypeStruct((T, D), F32)
    return pl.pallas_call(
        body, name=name, grid=(T // tm, nb),
        in_specs=[_x_spec(x, tm, kb), pl.BlockSpec((None, kb, D), lambda i, j: (layer, j, 0)), tok,
                  pl.BlockSpec((1, D), lambda i, j: (0, 0))],
        out_specs=[tok, tok], out_shape=[shp, shp],
        scratch_shapes=[pltpu.VMEM((tm, D), F32)], compiler_params=_params(2),
    )(x, w3, h, g)


def nbwd_mm_nt(dout, y, g, w3, layer, coef, kb, name, gate=None, up=None):
    T, D = dout.shape
    nb = w3.shape[1] // kb
    swiglu = gate is not None
    tm = _tile(T, 512)

    def body(*refs):
        if swiglu:
            dout_ref, y_ref, g_ref, w_ref, gate_ref, up_ref, dy_ref, dg_ref, da_ref, dys_ref = refs
        else:
            dout_ref, y_ref, g_ref, w_ref, dy_ref, dg_ref, da_ref, dys_ref = refs
        i, b = pl.program_id(0), pl.program_id(1)

        @pl.when((i == 0) & (b == 0))
        def _():
            dg_ref[...] = jnp.zeros_like(dg_ref)

        @pl.when(b == 0)
        def _():
            yh, r = _rms(y_ref[...])
            dyn = coef * dout_ref[...]
            dg_ref[...] += _colsum(dyn * yh)
            dy = _norm_bwd(dyn, yh, r, g_ref[...]).astype(BF)
            dys_ref[...] = dy
            dy_ref[...] = dy

        da = lax.dot_general(dys_ref[...], w_ref[...], NT, preferred_element_type=F32)
        if swiglu:
            gt = gate_ref[...].astype(F32)
            u = up_ref[...].astype(F32)
            sg = _sigmoid(gt)
            da_ref[0] = (da * u * (sg * (1.0 + gt * (1.0 - sg)))).astype(BF)
            da_ref[1] = (da * (gt * sg)).astype(BF)
        else:
            da_ref[...] = da.astype(BF)

    tok = pl.BlockSpec((tm, D), lambda i, j: (i, 0))
    vec = pl.BlockSpec((1, D), lambda i, j: (0, 0))
    in_specs = [tok, tok, vec, pl.BlockSpec((None, kb, D), lambda i, j: (layer, j, 0))]
    args = [dout, y, g, w3]
    if swiglu:
        blk = pl.BlockSpec((None, tm, kb), lambda i, j: (j, i, 0))
        in_specs += [blk, blk]
        args += [gate, up]
        da_spec = pl.BlockSpec((2, None, tm, kb), lambda i, j: (0, j, i, 0))
        da_shape = jax.ShapeDtypeStruct((2, nb, T, kb), BF)
    else:
        da_spec = pl.BlockSpec((tm, kb), lambda i, j: (i, j))
        da_shape = jax.ShapeDtypeStruct((T, nb * kb), BF)
    return pl.pallas_call(
        body, name=name, grid=(T // tm, nb), in_specs=in_specs,
        out_specs=[tok, vec, da_spec],
        out_shape=[jax.ShapeDtypeStruct((T, D), BF), jax.ShapeDtypeStruct((1, D), F32), da_shape],
        scratch_shapes=[pltpu.VMEM((tm, D), BF)], compiler_params=_params(2),
    )(*args)


def mm_nt_nbwd(dz, w, layer, h, g, dout, name, tn=None):
    T, D = h.shape
    if w.ndim == 4:
        nb, cs = w.shape[1], w.shape[3]
    else:
        cs = tn
        nb = w.shape[1] // cs
    tm = _tile(T, 512)

    def body(dz_ref, w_ref, h_ref, g_ref, dout_ref, dh_ref, xn_ref, dg_ref, acc_ref):
        i, b = pl.program_id(0), pl.program_id(1)

        @pl.when((i == 0) & (b == 0))
        def _():
            dg_ref[...] = jnp.zeros_like(dg_ref)

        @pl.when(b == 0)
        def _():
            acc_ref[...] = jnp.zeros_like(acc_ref)

        acc_ref[...] += lax.dot_general(dz_ref[...], w_ref[...], NT, preferred_element_type=F32)

        @pl.when(b == nb - 1)
        def _():
            xh, r = _rms(h_ref[...])
            gg = g_ref[...]
            dxn = acc_ref[...]
            dg_ref[...] += _colsum(dxn * xh)
            dh_ref[...] = dout_ref[...] + _norm_bwd(dxn, xh, r, gg)
            xn_ref[...] = (xh * gg).astype(BF)

    tok = pl.BlockSpec((tm, D), lambda i, j: (i, 0))
    vec = pl.BlockSpec((1, D), lambda i, j: (0, 0))
    return pl.pallas_call(
        body, name=name, grid=(T // tm, nb),
        in_specs=[_x_spec(dz, tm, cs), _w_spec(w, layer, cs)(0), tok, vec, tok],
        out_specs=[tok, tok, vec],
        out_shape=[jax.ShapeDtypeStruct((T, D), F32), jax.ShapeDtypeStruct((T, D), BF), jax.ShapeDtypeStruct((1, D), F32)],
        scratch_shapes=[pltpu.VMEM((tm, D), F32)], compiler_params=_params(2),
    )(dz, w, h, g, dout)


def mm_tn(x, y, name, xb=None, yb=None, x_layer=None):
    T = y.shape[-2]
    tt = _tile(T, 512)
    x_split = (x.ndim == 3 and x_layer is None) or xb is not None
    if x_layer is not None:
        xs = pl.BlockSpec((None, tt, x.shape[2]), lambda b, t: (x_layer, t, 0))
        kdim = x.shape[2]
    elif x.ndim == 3:
        xs = pl.BlockSpec((None, tt, x.shape[2]), lambda b, t: (b, t, 0))
        nb, kdim = x.shape[0], x.shape[2]
    elif xb is not None:
        xs = pl.BlockSpec((tt, xb), lambda b, t: (t, b))
        nb, kdim = x.shape[1] // xb, xb
    else:
        xs = pl.BlockSpec((tt, x.shape[1]), lambda b, t: (t, 0))
        kdim = x.shape[1]
    if x_split:
        ys = pl.BlockSpec((tt, y.shape[1]), lambda b, t: (t, 0))
        ndim = y.shape[1]
        out_spec = pl.BlockSpec((kdim, ndim), lambda b, t: (b, 0))
        out_shape = jax.ShapeDtypeStruct((nb * kdim, ndim), BF)
    else:
        if y.ndim == 3:
            ys = pl.BlockSpec((None, tt, y.shape[2]), lambda b, t: (b, t, 0))
            nb, ndim = y.shape[0], y.shape[2]
        else:
            ys = pl.BlockSpec((tt, yb), lambda b, t: (t, b))
            nb, ndim = y.shape[1] // yb, yb
        out_spec = pl.BlockSpec((None, kdim, ndim), lambda b, t: (b, 0, 0))
        out_shape = jax.ShapeDtypeStruct((nb, kdim, ndim), BF)
    nt = T // tt

    def body(x_ref, y_ref, o_ref, acc_ref):
        t = pl.program_id(1)

        @pl.when(t == 0)
        def _():
            acc_ref[...] = jnp.zeros_like(acc_ref)

        acc_ref[...] += lax.dot_general(x_ref[...].astype(BF), y_ref[...].astype(BF), TN, preferred_element_type=F32)

        @pl.when(t == nt - 1)
        def _():
            o_ref[...] = acc_ref[...].astype(BF)

    return pl.pallas_call(
        body, name=name, grid=(nb, nt), in_specs=[xs, ys], out_specs=out_spec, out_shape=out_shape,
        scratch_shapes=[pltpu.VMEM((kdim, ndim), F32)], compiler_params=_params(2),
    )(x, y)


def ple_fwd(h, gpre, wg3, p3, wp3, gpost, layer, name):
    T, D = h.shape
    pd = p3.shape[2]
    tm = _tile(T, 512)

    def body(h_ref, gpre_ref, wg_ref, p_ref, wp_ref, gpost_ref, hn_ref, gate_ref, pp_ref):
        x = h_ref[...]
        xh, _ = _rms(x)
        u = jnp.dot((xh * gpre_ref[...]).astype(BF), wg_ref[...], preferred_element_type=F32)
        gate = _sigmoid(u)
        pp = jnp.dot(p_ref[...].astype(BF), wp_ref[...], preferred_element_type=F32)
        yh, _ = _rms(gate * pp)
        hn_ref[...] = x + yh * gpost_ref[...]
        gate_ref[...] = gate.astype(BF)
        pp_ref[...] = pp.astype(BF)

    tok = pl.BlockSpec((tm, D), lambda i: (i, 0))
    vec = pl.BlockSpec((1, D), lambda i: (0, 0))
    return pl.pallas_call(
        body, name=name, grid=(T // tm,),
        in_specs=[tok, vec, pl.BlockSpec((None, D, D), lambda i: (layer, 0, 0)),
                  pl.BlockSpec((None, tm, pd), lambda i: (layer, i, 0)),
                  pl.BlockSpec((None, pd, D), lambda i: (layer, 0, 0)), vec],
        out_specs=[tok, tok, tok],
        out_shape=[jax.ShapeDtypeStruct((T, D), F32), jax.ShapeDtypeStruct((T, D), BF), jax.ShapeDtypeStruct((T, D), BF)],
        compiler_params=_params(1),
    )(h, gpre, wg3, p3, wp3, gpost)


def ple_bwd(dout, h, gate, pp, gpre, wg3, gpost, layer, name):
    T, D = h.shape
    tm = _tile(T, 512)

    def body(dout_ref, h_ref, gate_ref, pp_ref, gpre_ref, wg_ref, gpost_ref, dh_ref, du_ref, dpp_ref, xn_ref, dgpre_ref, dgpost_ref):
        @pl.when(pl.program_id(0) == 0)
        def _():
            dgpre_ref[...] = jnp.zeros_like(dgpre_ref)
            dgpost_ref[...] = jnp.zeros_like(dgpost_ref)

        dout = dout_ref[...]
        gate = gate_ref[...].astype(F32)
        pp = pp_ref[...].astype(F32)
        yh, ry = _rms(gate * pp)
        dgpost_ref[...] += _colsum(dout * yh)
        dy = _norm_bwd(dout, yh, ry, gpost_ref[...])
        dpp_ref[...] = (dy * gate).astype(BF)
        du = (dy * pp * gate * (1.0 - gate)).astype(BF)
        du_ref[...] = du
        dxn = lax.dot_general(du, wg_ref[...], NT, preferred_element_type=F32)
        xh, r = _rms(h_ref[...])
        gp = gpre_ref[...]
        dgpre_ref[...] += _colsum(dxn * xh)
        dh_ref[...] = dout + _norm_bwd(dxn, xh, r, gp)
        xn_ref[...] = (xh * gp).astype(BF)

    tok = pl.BlockSpec((tm, D), lambda i: (i, 0))
    vec = pl.BlockSpec((1, D), lambda i: (0, 0))
    bft = jax.ShapeDtypeStruct((T, D), BF)
    v32 = jax.ShapeDtypeStruct((1, D), F32)
    return pl.pallas_call(
        body, name=name, grid=(T // tm,),
        in_specs=[tok, tok, tok, tok, vec, pl.BlockSpec((None, D, D), lambda i: (layer, 0, 0)), vec],
        out_specs=[tok, tok, tok, tok, vec, vec],
        out_shape=[jax.ShapeDtypeStruct((T, D), F32), bft, bft, bft, v32, v32],
        compiler_params=_params(1),
    )(dout, h, gate, pp, gpre, wg3, gpost)


def _chunk_tri(tb, upper):
    r = lax.broadcasted_iota(jnp.int32, (tb, tb), 0)
    c = lax.broadcasted_iota(jnp.int32, (tb, tb), 1)
    shift = HGRN_CHUNK.bit_length() - 1
    same = jnp.right_shift(r, shift) == jnp.right_shift(c, shift)
    return (same & ((c >= r) if upper else (c <= r))).astype(F32)


def _hgrn_gates(z, logits):
    lb = 1.0 / (1.0 + jnp.exp(logits[1:2, :] - logits[0:1, :]))
    e = jnp.exp(-jnp.abs(z))
    inv = 1.0 / (1.0 + e)
    sig = jnp.where(z >= 0, inv, e * inv)
    nsig = jnp.where(z >= 0, e * inv, inv)
    return lb, sig, nsig, lb + (1.0 - lb) * sig


def hgrn_fwd(z, lb_logits, out_norm, name):
    T = z.shape[0]
    W = z.shape[1] // 4
    H = W // HGRN_DK
    C = HGRN_CHUNK
    tb = _tile(T, 256)
    nch = tb // C

    def body(zq_ref, zf_ref, zv_ref, zg_ref, lbl_ref, on_ref, x_ref, o_ref, st_ref, s_scr, cum_scr, k_scr, o_scr):
        @pl.when(pl.program_id(1) == 0)
        def _():
            s_scr[...] = jnp.zeros_like(s_scr)

        lb, sig, nsig, f = _hgrn_gates(zf_ref[...], lbl_ref[...])
        cum_scr[...] = jnp.dot(_chunk_tri(tb, False), jnp.log(f), precision=HI, preferred_element_type=F32)
        k_scr[...] = (1.0 - lb) * nsig
        row = lax.broadcasted_iota(jnp.int32, (C, HGRN_DK), 0)

        def chunk(c, carry):
            r0 = pl.multiple_of(c * C, C)
            rows = pl.ds(r0, C)
            q, k, v, cu = zq_ref[rows, :], k_scr[rows, :], zv_ref[rows, :], cum_scr[rows, :]
            st = s_scr[...]
            st_ref[c] = st
            o = lax.dot_general((q * jnp.exp(cu)).astype(BF), st.astype(BF), NT, preferred_element_type=F32)
            qr = q.astype(BF).astype(F32)
            for s in range(C):
                one = pl.ds(r0 + s, 1)
                e = jnp.exp(jnp.minimum(cu - cum_scr[one, :], 0.0))
                col = jnp.sum(qr * (e * k_scr[one, :]).astype(BF).astype(F32), axis=-1, keepdims=True)
                col = jnp.where(row >= s, col, 0.0).astype(BF).astype(F32)
                o = o + col * zv_ref[one, :].astype(BF).astype(F32)
            o_scr[rows, :] = o
            last = cum_scr[pl.ds(r0 + C - 1, 1), :]
            kg = (k * jnp.exp(last - cu)).astype(BF)
            s_scr[...] = st * jnp.exp(last) + lax.dot_general(v.astype(BF), kg, TN, preferred_element_type=F32)
            return carry

        lax.fori_loop(0, nch, chunk, 0)
        o = o_scr[...]
        o_ref[...] = o
        oh, _ = _rms(o)
        g = zg_ref[...]
        x_ref[...] = (oh * on_ref[...] * (g * _sigmoid(g))).astype(BF)

    def zs(part):
        return pl.BlockSpec((tb, HGRN_DK), lambda hd, i: (i, part * H + hd))

    blk = pl.BlockSpec((tb, HGRN_DK), lambda hd, i: (i, hd))
    return pl.pallas_call(
        body, name=name, grid=(H, T // tb),
        in_specs=[zs(0), zs(1), zs(2), zs(3), pl.BlockSpec((2, HGRN_DK), lambda hd, i: (0, hd)),
                  pl.BlockSpec((1, HGRN_DK), lambda hd, i: (0, 0))],
        out_specs=[blk, blk, pl.BlockSpec((nch, None, HGRN_DK, HGRN_DK), lambda hd, i: (i, hd, 0, 0))],
        out_shape=[jax.ShapeDtypeStruct((T, W), BF), jax.ShapeDtypeStruct((T, W), F32),
                   jax.ShapeDtypeStruct((T // C, H, HGRN_DK, HGRN_DK), F32)],
        scratch_shapes=[pltpu.VMEM((HGRN_DK, HGRN_DK), F32), pltpu.VMEM((tb, HGRN_DK), F32),
                        pltpu.VMEM((tb, HGRN_DK), F32), pltpu.VMEM((tb, HGRN_DK), F32)],
        compiler_params=_params(2),
    )(z, z, z, z, lb_logits, out_norm)


def hgrn_bwd(dx, z, o, states, lb_logits, out_norm, name):
    T = z.shape[0]
    W = z.shape[1] // 4
    H = W // HGRN_DK
    C = HGRN_CHUNK
    tb = _tile(T, 256)
    nch = tb // C
    nblk = T // tb

    def body(dx_ref, zq_ref, zf_ref, zv_ref, zg_ref, o_ref, st_ref, lbl_ref, on_ref,
             dq_ref, df_ref, dv_ref, dg_ref, dl_ref, don_ref,
             ds_scr, cum_scr, k_scr, do_scr, dq_scr, dk_scr, dv_scr, dcum_scr):
        hd, i = pl.program_id(0), pl.program_id(1)

        @pl.when(i == 0)
        def _():
            ds_scr[...] = jnp.zeros_like(ds_scr)
            dl_ref[...] = jnp.zeros_like(dl_ref)

        @pl.when((i == 0) & (hd == 0))
        def _():
            don_ref[...] = jnp.zeros_like(don_ref)

        lb, sig, nsig, f = _hgrn_gates(zf_ref[...], lbl_ref[...])
        cum_scr[...] = jnp.dot(_chunk_tri(tb, False), jnp.log(f), precision=HI, preferred_element_type=F32)
        k_scr[...] = (1.0 - lb) * nsig
        oh, r = _rms(o_ref[...])
        w = on_ref[...]
        g = zg_ref[...]
        sg = _sigmoid(g)
        dxv = dx_ref[...].astype(F32)
        dg_ref[...] = (dxv * (oh * w) * (sg * (1.0 + g * (1.0 - sg)))).astype(BF)
        don = dxv * (g * sg)
        don_ref[...] += _colsum(don * oh)
        do_scr[...] = _norm_bwd(don, oh, r, w)
        row = lax.broadcasted_iota(jnp.int32, (C, HGRN_DK), 0)

        def chunk(cc, carry):
            c = nch - 1 - cc
            r0 = pl.multiple_of(c * C, C)
            rows = pl.ds(r0, C)
            q, k, v, cu, do = zq_ref[rows, :], k_scr[rows, :], zv_ref[rows, :], cum_scr[rows, :], do_scr[rows, :]
            st = st_ref[c]
            dst = ds_scr[...]
            last = cum_scr[pl.ds(r0 + C - 1, 1), :]
            lam, gam, elast = jnp.exp(cu), jnp.exp(last - cu), jnp.exp(last)
            dob, dstb = do.astype(BF), dst.astype(BF)
            dq = jnp.dot(dob, st.astype(BF), preferred_element_type=F32) * lam
            dv = lax.dot_general((k * gam).astype(BF), dstb, NT, preferred_element_type=F32)
            dk = jnp.dot(v.astype(BF), dstb, preferred_element_type=F32) * gam
            dlast = elast * _colsum(dst * st) + _colsum(dk * k)
            ds_scr[...] = dst * elast + lax.dot_general(dob, (q * lam).astype(BF), TN, preferred_element_type=F32)
            for s in range(C):
                one = pl.ds(r0 + s, 1)
                e = jnp.where(row >= s, jnp.exp(jnp.minimum(cu - cum_scr[one, :], 0.0)), 0.0)
                ks = k_scr[one, :]
                da = jnp.sum(do * zv_ref[one, :], axis=-1, keepdims=True)
                pq = q * e
                a = jnp.sum(pq * ks, axis=-1, keepdims=True)
                dq = dq + da * e * ks
                dk = jnp.where(row == s, dk + _colsum(da * pq), dk)
                dv = jnp.where(row == s, dv + _colsum(a * do), dv)
            dq_scr[rows, :] = dq
            dk_scr[rows, :] = dk
            dv_scr[rows, :] = dv
            dcum_scr[rows, :] = q * dq - k * dk + jnp.where(row == C - 1, dlast, 0.0)
            return carry

        lax.fori_loop(0, nch, chunk, 0)
        dlf = jnp.dot(_chunk_tri(tb, True), dcum_scr[...], precision=HI, preferred_element_type=F32)
        dk = dk_scr[...]
        common = (1.0 - lb) * sig * nsig
        df_ref[...] = (dlf * common / f - dk * common).astype(BF)
        dq_ref[...] = dq_scr[...].astype(BF)
        dv_ref[...] = dv_scr[...].astype(BF)
        dl0 = _colsum(dlf * nsig / f - dk * nsig) * lb * (1.0 - lb)
        dl_ref[...] += jnp.where(lax.broadcasted_iota(jnp.int32, (2, HGRN_DK), 0) == 0, dl0, -dl0)

    def zs(part):
        return pl.BlockSpec((tb, HGRN_DK), lambda hd, i: (nblk - 1 - i, part * H + hd))

    blk = pl.BlockSpec((tb, HGRN_DK), lambda hd, i: (nblk - 1 - i, hd))
    bft = jax.ShapeDtypeStruct((T, W), BF)
    scr = pltpu.VMEM((tb, HGRN_DK), F32)
    return pl.pallas_call(
        body, name=name, grid=(H, nblk),
        in_specs=[blk, zs(0), zs(1), zs(2), zs(3), blk,
                  pl.BlockSpec((nch, None, HGRN_DK, HGRN_DK), lambda hd, i: (nblk - 1 - i, hd, 0, 0)),
                  pl.BlockSpec((2, HGRN_DK), lambda hd, i: (0, hd)), pl.BlockSpec((1, HGRN_DK), lambda hd, i: (0, 0))],
        out_specs=[blk, blk, blk, blk, pl.BlockSpec((2, HGRN_DK), lambda hd, i: (0, hd)),
                   pl.BlockSpec((1, HGRN_DK), lambda hd, i: (0, 0))],
        out_shape=[bft, bft, bft, bft, jax.ShapeDtypeStruct((2, W), F32), jax.ShapeDtypeStruct((1, HGRN_DK), F32)],
        scratch_shapes=[pltpu.VMEM((HGRN_DK, HGRN_DK), F32), scr, scr, scr, scr, scr, scr, scr],
        compiler_params=_params(2),
    )(dx, z, z, z, z, o, states, lb_logits, out_norm)


def _log_sigmoid(x):
    return jnp.minimum(x, 0.0) - jnp.log(1.0 + jnp.exp(-jnp.abs(x)))


def _tri(n, upper):
    r = lax.broadcasted_iota(jnp.int32, (n, n), 0)
    c = lax.broadcasted_iota(jnp.int32, (n, n), 1)
    return ((r >= c) if upper else (c <= r)).astype(F32)


def fox_cum_fwd(kvf, fcol, b_row, name):
    T = kvf.shape[0]
    tb = _tile(T, 512)

    def body(x_ref, b_ref, ct_ref, carry_ref):
        @pl.when(pl.program_id(0) == 0)
        def _():
            carry_ref[...] = jnp.zeros_like(carry_ref)

        lf = _log_sigmoid(x_ref[...] + b_ref[...])
        cum = jnp.dot(_tri(tb, False), lf, precision=HI, preferred_element_type=F32) + carry_ref[...]
        carry_ref[...] += _colsum(lf)
        ct_ref[...] = cum.T

    return pl.pallas_call(
        body, name=name, grid=(T // tb,),
        in_specs=[pl.BlockSpec((tb, LANES), lambda i: (i, fcol)), pl.BlockSpec((1, LANES), lambda i: (0, 0))],
        out_specs=pl.BlockSpec((LANES, tb), lambda i: (0, i)), out_shape=jax.ShapeDtypeStruct((LANES, T), F32),
        scratch_shapes=[pltpu.VMEM((1, LANES), F32)], compiler_params=_params(1),
    )(kvf, b_row)


def fox_cum_bwd(dct, kvf, fcol, b_row, name):
    T = kvf.shape[0]
    tb = _tile(T, 512)
    nblk = T // tb

    def body(dc_ref, x_ref, b_ref, df_ref, db_ref, carry_ref):
        @pl.when(pl.program_id(0) == 0)
        def _():
            carry_ref[...] = jnp.zeros_like(carry_ref)
            db_ref[...] = jnp.zeros_like(db_ref)

        dc = dc_ref[...]
        dlf_t = jnp.dot(dc, _tri(tb, True), precision=HI, preferred_element_type=F32) + carry_ref[...]
        carry_ref[...] += jnp.sum(dc, axis=1, keepdims=True)
        x = x_ref[...] + b_ref[...]
        df = dlf_t.T * _sigmoid(-x)
        df_ref[...] = df.astype(BF)
        db_ref[...] += _colsum(df)

    return pl.pallas_call(
        body, name=name, grid=(nblk,),
        in_specs=[pl.BlockSpec((LANES, tb), lambda i: (0, nblk - 1 - i)),
                  pl.BlockSpec((tb, LANES), lambda i: (nblk - 1 - i, fcol)), pl.BlockSpec((1, LANES), lambda i: (0, 0))],
        out_specs=[pl.BlockSpec((tb, LANES), lambda i: (nblk - 1 - i, 0)), pl.BlockSpec((1, LANES), lambda i: (0, 0))],
        out_shape=[jax.ShapeDtypeStruct((T, LANES), BF), jax.ShapeDtypeStruct((1, LANES), F32)],
        scratch_shapes=[pltpu.VMEM((LANES, 1), F32)], compiler_params=_params(1),
    )(dct, kvf, b_row)


def fox_fwd(qg, kvf, c4, name):
    T = qg.shape[0]
    W = qg.shape[1] // 2
    NP = W // LANES
    tq = _tile(T, 256)
    scale = FOX_HEAD_DIM ** -0.5

    def body(q_ref, g_ref, k_ref, v_ref, c_ref, x_ref, o_ref, lse_ref):
        i = pl.program_id(1)
        half = lax.broadcasted_iota(jnp.int32, (tq, LANES), 1) < FOX_HEAD_DIM
        q2 = q_ref[...] * scale
        qs = (jnp.where(half, q2, 0.0).astype(BF), jnp.where(half, 0.0, q2).astype(BF))
        causal = lax.broadcasted_iota(jnp.int32, (tq, tq), 1) <= lax.broadcasted_iota(jnp.int32, (tq, tq), 0)

        def step(j, carry, diag):
            rows = pl.ds(pl.multiple_of(j * tq, tq), tq)
            k = k_ref[rows, :].astype(BF)
            v = v_ref[rows, :].astype(BF)
            out = []
            for a in range(2):
                m, l, acc = carry[a]
                s = lax.dot_general(qs[a], k, NT, preferred_element_type=F32) - c_ref[j, a:a + 1, :]
                if diag:
                    s = jnp.where(causal, s, -1e30)
                mn = jnp.maximum(m, jnp.max(s, axis=-1, keepdims=True))
                alpha = jnp.exp(m - mn)
                p = jnp.exp(s - mn)
                l = l * alpha + jnp.sum(p, axis=-1, keepdims=True)
                acc = acc * alpha + jnp.dot(p.astype(BF), v, preferred_element_type=F32)
                out.append((mn, l, acc))
            return tuple(out)

        init = (jnp.full((tq, 1), -1e30, F32), jnp.zeros((tq, 1), F32), jnp.zeros((tq, LANES), F32))
        carry = lax.fori_loop(0, i, lambda j, c: step(j, c, False), (init, init))
        (ma, la, acca), (mb, lb, accb) = step(i, carry, True)
        o = jnp.where(half, acca / la, accb / lb)
        o_ref[...] = o
        lse_ref[...] = jnp.where(half, ma + jnp.log(la), mb + jnp.log(lb))
        x_ref[...] = (o * _sigmoid(g_ref[...])).astype(BF)

    blk = pl.BlockSpec((tq, LANES), lambda hp, i: (i, hp))
    f32t = jax.ShapeDtypeStruct((T, W), F32)
    return pl.pallas_call(
        body, name=name, grid=(NP, T // tq),
        in_specs=[blk, pl.BlockSpec((tq, LANES), lambda hp, i: (i, NP + hp)),
                  pl.BlockSpec((T, LANES), lambda hp, i: (0, hp)), pl.BlockSpec((T, LANES), lambda hp, i: (0, NP + hp)),
                  pl.BlockSpec((None, T // tq, 2, tq), lambda hp, i: (hp, 0, 0, 0))],
        out_specs=[blk, blk, blk], out_shape=[jax.ShapeDtypeStruct((T, W), BF), f32t, f32t],
        compiler_params=_params(2),
    )(qg, qg, kvf, kvf, c4)


def fox_gate_bwd(dx, o, qg, name):
    T, W = o.shape
    tm = _tile(T, 512)

    def body(dx_ref, o_ref, g_ref, do_ref, dg_ref):
        dxv = dx_ref[...].astype(F32)
        sg = _sigmoid(g_ref[...])
        do_ref[...] = dxv * sg
        dg_ref[...] = (dxv * o_ref[...] * sg * (1.0 - sg)).astype(BF)

    tok = pl.BlockSpec((tm, W), lambda i: (i, 0))
    return pl.pallas_call(
        body, name=name, grid=(T // tm,),
        in_specs=[tok, tok, pl.BlockSpec((tm, W), lambda i: (i, 1))], out_specs=[tok, tok],
        out_shape=[jax.ShapeDtypeStruct((T, W), F32), jax.ShapeDtypeStruct((T, W), BF)], compiler_params=_params(1),
    )(dx, o, qg)


def fox_bwd(qg, kvf, c4, o, lse, do, name):
    T = qg.shape[0]
    W = qg.shape[1] // 2
    NP = W // LANES
    tq = _tile(T, 256)
    nq = T // tq
    scale = FOX_HEAD_DIM ** -0.5

    def body(q_ref, do_ref, o_ref, lse_ref, k_ref, v_ref, c_ref, dq_ref, dk_ref, dv_ref, dc_ref, dcq_ref):
        j = pl.program_id(1)

        @pl.when(j == 0)
        def _():
            dq_ref[...] = jnp.zeros_like(dq_ref)
            dcq_ref[...] = jnp.zeros_like(dcq_ref)

        ones = jnp.full((8, tq), 0.125, F32)

        half = lax.broadcasted_iota(jnp.int32, (tq, LANES), 1) < FOX_HEAD_DIM
        halves = (half, jnp.logical_not(half))
        k2, v2 = k_ref[...], v_ref[...]
        ks = [jnp.where(hm, k2, 0.0).astype(BF) for hm in halves]
        vs = [jnp.where(hm, v2, 0.0).astype(BF) for hm in halves]
        cs = [c_ref[j, a:a + 1, :] for a in range(2)]
        causal = lax.broadcasted_iota(jnp.int32, (tq, tq), 1) <= lax.broadcasted_iota(jnp.int32, (tq, tq), 0)

        def step(i, carry, diag):
            dk, dv, dca, dcb = carry
            dcs = [dca, dcb]
            rows = pl.ds(pl.multiple_of(i * tq, tq), tq)
            q2 = q_ref[rows, :] * scale
            do2 = do_ref[rows, :]
            dd = do2 * o_ref[rows, :]
            lse2 = lse_ref[rows, :]
            dq = jnp.zeros((tq, LANES), F32)
            for a in range(2):
                hm = halves[a]
                dsum = jnp.sum(jnp.where(hm, dd, 0.0), axis=-1, keepdims=True)
                lse_a = jnp.max(jnp.where(hm, lse2, -jnp.inf), axis=-1, keepdims=True)
                qa = jnp.where(hm, q2, 0.0).astype(BF)
                doa = jnp.where(hm, do2, 0.0).astype(BF)
                s = lax.dot_general(qa, ks[a], NT, preferred_element_type=F32) - cs[a]
                p = jnp.exp(s - lse_a)
                if diag:
                    p = jnp.where(causal, p, 0.0)
                dp = lax.dot_general(doa, vs[a], NT, preferred_element_type=F32)
                ds = p * (dp - dsum)
                dsb = ds.astype(BF)
                dv = dv + lax.dot_general(p.astype(BF), doa, TN, preferred_element_type=F32)
                dk = dk + lax.dot_general(dsb, qa, TN, preferred_element_type=F32)
                dq = dq + jnp.dot(dsb, ks[a], preferred_element_type=F32)
                dcs[a] = dcs[a] - _colsum(ds)
                dcq_ref[i, a:a + 1, :] += _colsum(lax.dot_general(ones, ds, NT, precision=HI, preferred_element_type=F32))
            dq_ref[rows, :] += dq * scale
            return dk, dv, dcs[0], dcs[1]

        zero = jnp.zeros((tq, LANES), F32)
        zc = jnp.zeros((1, tq), F32)
        carry = step(j, (zero, zero, zc, zc), True)
        dk, dv, dca, dcb = lax.fori_loop(j + 1, nq, lambda i, c: step(i, c, False), carry)
        dk_ref[...] = dk.astype(BF)
        dv_ref[...] = dv.astype(BF)
        dc_ref[...] = jnp.where(lax.broadcasted_iota(jnp.int32, (2, tq), 0) == 0, dca, dcb)

    full = lambda col: pl.BlockSpec((T, LANES), lambda hp, j: (0, col(hp)))
    kv = lambda off: pl.BlockSpec((tq, LANES), lambda hp, j: (j, off + hp))
    tile = pl.BlockSpec((tq, LANES), lambda hp, j: (j, hp))
    bft = jax.ShapeDtypeStruct((T, W), BF)
    return pl.pallas_call(
        body, name=name, grid=(NP, nq),
        in_specs=[full(lambda hp: hp), full(lambda hp: hp), full(lambda hp: hp), full(lambda hp: hp), kv(0), kv(NP),
                  pl.BlockSpec((None, nq, 2, tq), lambda hp, j: (hp, 0, 0, 0))],
        out_specs=[full(lambda hp: hp), tile, tile, pl.BlockSpec((None, None, 2, tq), lambda hp, j: (hp, j, 0, 0)),
                   pl.BlockSpec((None, nq, 2, tq), lambda hp, j: (hp, 0, 0, 0))],
        out_shape=[jax.ShapeDtypeStruct((T, W), F32), bft, bft, jax.ShapeDtypeStruct((NP, nq, 2, tq), F32),
                   jax.ShapeDtypeStruct((NP, nq, 2, tq), F32)],
        compiler_params=_params(2),
    )(qg, do, o, lse, kvf, kvf, c4)


def loss_fwd_bwd(y, target, name):
    T, D = y.shape
    tm = _tile(T, 512)

    def body(y_ref, t_ref, dy_ref, l_ref):
        @pl.when(pl.program_id(0) == 0)
        def _():
            l_ref[...] = jnp.zeros_like(l_ref)

        d = y_ref[...] - t_ref[...]
        dy_ref[...] = d * (1.0 / D)
        l_ref[...] += 0.5 * jnp.sum(jnp.mean(d * d, axis=-1, keepdims=True), axis=0, keepdims=True)

    tok = pl.BlockSpec((tm, D), lambda i: (i, 0))
    return pl.pallas_call(
        body, name=name, grid=(T // tm,), in_specs=[tok, tok],
        out_specs=[tok, pl.BlockSpec((1, 1), lambda i: (0, 0))],
        out_shape=[jax.ShapeDtypeStruct((T, D), F32), jax.ShapeDtypeStruct((1, 1), F32)], compiler_params=_params(1),
    )(y, target)


def adamw(parts, w, m, v, name):
    shape = w.shape
    if w.ndim == 2:
        w, m, v = w[None], m[None], v[None]
        parts = parts[:, None]
    L, R, C = w.shape
    tr = _tile(R, 256)
    c1 = 1.0 / (1.0 - ADAM_B1 ** ADAM_STEP)
    c2 = 1.0 / (1.0 - ADAM_B2 ** ADAM_STEP)

    def body(p_ref, w_ref, m_ref, v_ref, g_ref, d_ref, mo_ref, vo_ref):
        g = p_ref[0].astype(F32)
        for d in range(1, N_DEV):
            g = g + p_ref[d].astype(F32)
        mn = ADAM_B1 * m_ref[...] + (1.0 - ADAM_B1) * g
        vn = ADAM_B2 * v_ref[...] + (1.0 - ADAM_B2) * (g * g)
        g_ref[...] = g
        mo_ref[...] = mn
        vo_ref[...] = vn
        d_ref[...] = -ADAM_LR * ((mn * c1) / (jnp.sqrt(vn * c2) + ADAM_EPS) + ADAM_WD * w_ref[...])

    blk = pl.BlockSpec((None, tr, C), lambda l, i: (l, i, 0))
    shp = jax.ShapeDtypeStruct((L, R, C), F32)
    outs = pl.pallas_call(
        body, name=name, grid=(L, R // tr),
        in_specs=[pl.BlockSpec((N_DEV, None, tr, C), lambda l, i: (0, l, i, 0)), blk, blk, blk],
        out_specs=[blk, blk, blk, blk], out_shape=[shp, shp, shp, shp], compiler_params=_params(2),
    )(parts, w, m, v)
    return [a.reshape(shape) for a in outs]


def _mesh_pos():
    return lax.axis_index("x"), lax.axis_index("y"), lax.axis_index("c")


def _flip(v, bit):
    return v + bit - 2 * v * bit


def _peer(pos, delta):
    x, y, c = pos
    px, py, pc = _flip(x, (delta >> 2) & 1), _flip(y, (delta >> 1) & 1), _flip(c, delta & 1)
    return (px, py, pc), 4 * px + 2 * py + pc


def all_gather(shards, name):
    n = len(shards)

    def body(*refs):
        srcs, dsts = refs[:n], refs[n:2 * n]
        send, recv, loc = refs[2 * n:]
        pos = _mesh_pos()
        me = 4 * pos[0] + 2 * pos[1] + pos[2]

        def slot(k, dev):
            return dsts[k].at[:, dev] if srcs[k].ndim == 3 else dsts[k].at[dev]

        started = []
        for k in range(n):
            cp = pltpu.make_async_copy(srcs[k], slot(k, me), loc.at[k])
            cp.start()
            started.append(cp)
        sends = []
        for k in range(n):
            for d in range(1, N_DEV):
                dev, _ = _peer(pos, d)
                rc = pltpu.make_async_remote_copy(srcs[k], slot(k, me), send.at[k, d - 1], recv.at[k, d - 1],
                                                  device_id=dev, device_id_type=pl.DeviceIdType.MESH)
                rc.start()
                sends.append(rc)
        for k in range(n):
            for d in range(1, N_DEV):
                dev, idx = _peer(pos, d)
                pltpu.make_async_remote_copy(srcs[k], slot(k, idx), send.at[k, d - 1], recv.at[k, d - 1],
                                             device_id=dev, device_id_type=pl.DeviceIdType.MESH).wait_recv()
        for rc in sends:
            rc.wait_send()
        for cp in started:
            cp.wait()

    out_shape = [jax.ShapeDtypeStruct((s.shape[0], N_DEV) + s.shape[1:] if s.ndim == 3 else (N_DEV,) + s.shape, s.dtype)
                 for s in shards]
    anyspec = pl.BlockSpec(memory_space=pl.ANY)
    return pl.pallas_call(
        body, name=name, in_specs=[anyspec] * n, out_specs=[anyspec] * n, out_shape=out_shape,
        scratch_shapes=[pltpu.SemaphoreType.DMA((n, N_DEV - 1)), pltpu.SemaphoreType.DMA((n, N_DEV - 1)),
                        pltpu.SemaphoreType.DMA((n,))],
        compiler_params=pltpu.CompilerParams(has_side_effects=True),
    )(*shards)


def scatter_grads(items, small, name):
    flat = [(k, l, g) for k, (gs) in enumerate(items) for l, g in enumerate(gs)]
    n = len(flat) + 1
    nk = len(items)

    def body(*refs):
        srcs = refs[:n]
        dsts = refs[n:n + nk + 1]
        send, recv, loc = refs[n + nk + 1:]
        pos = _mesh_pos()
        me = 4 * pos[0] + 2 * pos[1] + pos[2]

        def pair(f, src_dev, slot_dev):
            if f == n - 1:
                return srcs[f], dsts[nk].at[slot_dev]
            k, l, _ = flat[f]
            return srcs[f].at[src_dev], dsts[k].at[slot_dev, l]

        started = []
        for f in range(n):
            s, d = pair(f, me, me)
            cp = pltpu.make_async_copy(s, d, loc.at[f])
            cp.start()
            started.append(cp)
        sends = []
        for f in range(n):
            for dl in range(1, N_DEV):
                dev, idx = _peer(pos, dl)
                s, d = pair(f, idx, me)
                rc = pltpu.make_async_remote_copy(s, d, send.at[f, dl - 1], recv.at[f, dl - 1],
                                                  device_id=dev, device_id_type=pl.DeviceIdType.MESH)
                rc.start()
                sends.append(rc)
        for f in range(n):
            for dl in range(1, N_DEV):
                dev, idx = _peer(pos, dl)
                s, d = pair(f, me, idx)
                pltpu.make_async_remote_copy(s, d, send.at[f, dl - 1], recv.at[f, dl - 1],
                                             device_id=dev, device_id_type=pl.DeviceIdType.MESH).wait_recv()
        for rc in sends:
            rc.wait_send()
        for cp in started:
            cp.wait()

    out_shape = [jax.ShapeDtypeStruct((N_DEV, len(gs)) + gs[0].shape[1:], gs[0].dtype) for gs in items]
    out_shape.append(jax.ShapeDtypeStruct((N_DEV,) + small.shape, small.dtype))
    anyspec = pl.BlockSpec(memory_space=pl.ANY)
    return pl.pallas_call(
        body, name=name, in_specs=[anyspec] * n, out_specs=[anyspec] * (nk + 1), out_shape=out_shape,
        scratch_shapes=[pltpu.SemaphoreType.DMA((n, N_DEV - 1)), pltpu.SemaphoreType.DMA((n, N_DEV - 1)),
                        pltpu.SemaphoreType.DMA((n,))],
        compiler_params=pltpu.CompilerParams(has_side_effects=True),
    )(*[g for _, _, g in flat], small)


def _row(v):
    return v.reshape(1, -1)


def _pad_lanes(v, n):
    return jnp.pad(v, ((0, 0), (0, n - v.shape[1])))


def kernel(x, p, ffn1_norm_pre, ffn1_w_in, ffn1_w_out, ffn1_norm_post, mix_norm_pre, mix_norm_post, ffn2_norm_pre, ffn2_w_in, ffn2_w_out, ffn2_norm_post, hgrn_w_in, hgrn_lb_logits, hgrn_out_norm, hgrn_w_out, kv_norm, fox_w_kvf, fox_b_f, fox_w_qg, fox_w_out, ple_norm_pre, ple_w_gate, ple_w_proj, ple_norm_post, loss_target, m_ffn1_norm_pre, m_ffn1_w_in, m_ffn1_w_out, m_ffn1_norm_post, m_mix_norm_pre, m_mix_norm_post, m_ffn2_norm_pre, m_ffn2_w_in, m_ffn2_w_out, m_ffn2_norm_post, m_hgrn_w_in, m_hgrn_lb_logits, m_hgrn_out_norm, m_hgrn_w_out, m_kv_norm, m_fox_w_kvf, m_fox_b_f, m_fox_w_qg, m_fox_w_out, m_ple_norm_pre, m_ple_w_gate, m_ple_w_proj, m_ple_norm_post, v_ffn1_norm_pre, v_ffn1_w_in, v_ffn1_w_out, v_ffn1_norm_post, v_mix_norm_pre, v_mix_norm_post, v_ffn2_norm_pre, v_ffn2_w_in, v_ffn2_w_out, v_ffn2_norm_post, v_hgrn_w_in, v_hgrn_lb_logits, v_hgrn_out_norm, v_hgrn_w_out, v_kv_norm, v_fox_w_kvf, v_fox_b_f, v_fox_w_qg, v_fox_w_out, v_ple_norm_pre, v_ple_w_gate, v_ple_w_proj, v_ple_norm_post):
    weights = dict(ffn1_norm_pre=ffn1_norm_pre, ffn1_w_in=ffn1_w_in, ffn1_w_out=ffn1_w_out, ffn1_norm_post=ffn1_norm_post, mix_norm_pre=mix_norm_pre, mix_norm_post=mix_norm_post, ffn2_norm_pre=ffn2_norm_pre, ffn2_w_in=ffn2_w_in, ffn2_w_out=ffn2_w_out, ffn2_norm_post=ffn2_norm_post, hgrn_w_in=hgrn_w_in, hgrn_lb_logits=hgrn_lb_logits, hgrn_out_norm=hgrn_out_norm, hgrn_w_out=hgrn_w_out, kv_norm=kv_norm, fox_w_kvf=fox_w_kvf, fox_b_f=fox_b_f, fox_w_qg=fox_w_qg, fox_w_out=fox_w_out, ple_norm_pre=ple_norm_pre, ple_w_gate=ple_w_gate, ple_w_proj=ple_w_proj, ple_norm_post=ple_norm_post)
    mom1 = dict(ffn1_norm_pre=m_ffn1_norm_pre, ffn1_w_in=m_ffn1_w_in, ffn1_w_out=m_ffn1_w_out, ffn1_norm_post=m_ffn1_norm_post, mix_norm_pre=m_mix_norm_pre, mix_norm_post=m_mix_norm_post, ffn2_norm_pre=m_ffn2_norm_pre, ffn2_w_in=m_ffn2_w_in, ffn2_w_out=m_ffn2_w_out, ffn2_norm_post=m_ffn2_norm_post, hgrn_w_in=m_hgrn_w_in, hgrn_lb_logits=m_hgrn_lb_logits, hgrn_out_norm=m_hgrn_out_norm, hgrn_w_out=m_hgrn_w_out, kv_norm=m_kv_norm, fox_w_kvf=m_fox_w_kvf, fox_b_f=m_fox_b_f, fox_w_qg=m_fox_w_qg, fox_w_out=m_fox_w_out, ple_norm_pre=m_ple_norm_pre, ple_w_gate=m_ple_w_gate, ple_w_proj=m_ple_w_proj, ple_norm_post=m_ple_norm_post)
    mom2 = dict(ffn1_norm_pre=v_ffn1_norm_pre, ffn1_w_in=v_ffn1_w_in, ffn1_w_out=v_ffn1_w_out, ffn1_norm_post=v_ffn1_norm_post, mix_norm_pre=v_mix_norm_pre, mix_norm_post=v_mix_norm_post, ffn2_norm_pre=v_ffn2_norm_pre, ffn2_w_in=v_ffn2_w_in, ffn2_w_out=v_ffn2_w_out, ffn2_norm_post=v_ffn2_norm_post, hgrn_w_in=v_hgrn_w_in, hgrn_lb_logits=v_hgrn_lb_logits, hgrn_out_norm=v_hgrn_out_norm, hgrn_w_out=v_hgrn_w_out, kv_norm=v_kv_norm, fox_w_kvf=v_fox_w_kvf, fox_b_f=v_fox_b_f, fox_w_qg=v_fox_w_qg, fox_w_out=v_fox_w_out, ple_norm_pre=v_ple_norm_pre, ple_w_gate=v_ple_w_gate, ple_w_proj=v_ple_w_proj, ple_norm_post=v_ple_norm_post)
    names = list(weights)
    big = ["ffn1_w_in", "ffn1_w_out", "ffn2_w_in", "ffn2_w_out", "hgrn_w_in", "hgrn_w_out", "fox_w_kvf", "fox_w_qg",
           "fox_w_out", "ple_w_gate", "ple_w_proj"]
    small_names = [n for n in names if n not in big]

    T, D = x.shape[1], x.shape[2]
    depth = p.shape[0]
    h0 = x.reshape(T, D)
    target = loss_target.reshape(T, D)
    p3 = p.reshape(depth, T, p.shape[3])
    n_fox = fox_b_f.shape[0]
    fox_w = n_fox * FOX_HEAD_DIM
    fcol = 2 * fox_w // LANES

    gathered = dict(zip(big, all_gather([weights[n].astype(BF) for n in big], "gather_weights")))
    w_in = {1: gathered["ffn1_w_in"], 2: gathered["ffn2_w_in"]}
    w_out = {k: gathered[f"ffn{k}_w_out"] for k in (1, 2)}
    w_out = {k: w.reshape(w.shape[0], -1, D) for k, w in w_out.items()}
    ffn_cs = w_in[1].shape[3]
    hgrn_in = gathered["hgrn_w_in"]
    hgrn_out = gathered["hgrn_w_out"].reshape(1, -1, D)
    qg_w = gathered["fox_w_qg"]
    fox_out = gathered["fox_w_out"].reshape(1, -1, D)
    ple_gate = gathered["ple_w_gate"].reshape(depth, -1, D)
    ple_proj = gathered["ple_w_proj"].transpose(0, 2, 1, 3).reshape(depth, p.shape[3], D)
    kvf_nat = gathered["fox_w_kvf"].transpose(1, 0, 2).reshape(D, -1)
    kvf_cols = kvf_nat.shape[1]
    kvf_w = _pad_lanes(kvf_nat, 2 * fox_w + LANES)
    b_row = _pad_lanes(_row(fox_b_f), LANES)

    norm = lambda name, i: weights[name][i:i + 1]

    saved = []
    h = h0
    kvf = c4 = None
    tq = _tile(T, 256)
    for i in range(depth):
        s = dict(h_in=h)
        for k in (1, 2):
            if k == 2:
                s["h_a"] = h
                if i == 0:
                    z = norm_mm(h, norm("mix_norm_pre", i), hgrn_in, 0, "hgrn_in")
                    xm, o, states = hgrn_fwd(z, hgrn_lb_logits, hgrn_out_norm, "hgrn_scan")
                    s.update(z=z, o=o, states=states)
                    h, ym = mm_norm_res(xm, hgrn_out, 0, norm("mix_norm_post", i), h, 1.0, "hgrn_out", kb=_tile(xm.shape[1], 512))
                else:
                    qg = norm_mm(h, norm("mix_norm_pre", i), qg_w, 0, "fox_qg")
                    xm, o, lse = fox_fwd(qg, kvf, c4, "fox_attn")
                    s.update(qg=qg, o=o, lse=lse)
                    h, ym = mm_norm_res(xm, fox_out, 0, norm("mix_norm_post", i), h, 1.0, "fox_out", kb=_tile(xm.shape[1], 512))
                s.update(xm=xm, ym=ym, h_b=h)
            gate, up, a = norm_mm_swiglu(h, norm(f"ffn{k}_norm_pre", i), w_in[k], i, f"ffn{k}_in_{i}")
            hn, y = mm_norm_res(a, w_out[k], i, norm(f"ffn{k}_norm_post", i), h, 0.5, f"ffn{k}_out_{i}")
            s[f"ffn{k}"] = (h, gate, up, a, y)
            h = hn
        s["h_c"] = h
        h, pgate, pp = ple_fwd(h, norm("ple_norm_pre", i), ple_gate, p3, ple_proj, norm("ple_norm_post", i), i, f"ple_{i}")
        s.update(pgate=pgate, pp=pp)
        saved.append(s)
        if i == 0:
            kvf = norm_mm(h, _row(kv_norm), kvf_w, 0, "fox_kvf", tn=LANES)
            ct = fox_cum_fwd(kvf, fcol, b_row, "fox_cum")
            c4 = ct[:n_fox].reshape(n_fox // 2, 2, T // tq, tq).transpose(0, 2, 1, 3)
            h_kv = h

    dh, loss_part = loss_fwd_bwd(h, target, "loss")
    loss = lax.psum(loss_part[0, 0], ("x", "y", "c"))

    gbig = {}
    gsmall = {n: [None] * weights[n].shape[0] if weights[n].ndim == 2 else None for n in small_names}
    for i in reversed(range(depth)):
        s = saved[i]
        dh, du, dpp, xn, dgpre, dgpost = ple_bwd(dh, s["h_c"], s["pgate"], s["pp"], norm("ple_norm_pre", i), ple_gate,
                                                 norm("ple_norm_post", i), i, f"ple_bwd_{i}")
        gsmall["ple_norm_pre"][i], gsmall["ple_norm_post"][i] = dgpre, dgpost
        gbig.setdefault("ple_w_gate", {})[i] = mm_tn(xn, du, f"ple_dgate_{i}", xb=D // N_DEV).reshape(N_DEV, -1, D)
        dproj = mm_tn(p3, dpp, f"ple_dproj_{i}", x_layer=i, yb=D // N_DEV)
        gbig.setdefault("ple_w_proj", {})[i] = dproj
        for k in (2, 1):
            hin, gate, up, a, y = s[f"ffn{k}"]
            dy, dgpost, dz = nbwd_mm_nt(dh, y, norm(f"ffn{k}_norm_post", i), w_out[k], i, 0.5, ffn_cs, f"ffn{k}_bwd_out_{i}",
                                        gate=gate, up=up)
            dz = dz.reshape(-1, T, ffn_cs)
            gbig.setdefault(f"ffn{k}_w_out", {})[i] = mm_tn(a, dy, f"ffn{k}_dwout_{i}").reshape(N_DEV, -1, D)
            dh, xn, dgpre = mm_nt_nbwd(dz, w_in[k], i, hin, norm(f"ffn{k}_norm_pre", i), dh, f"ffn{k}_bwd_in_{i}")
            gbig.setdefault(f"ffn{k}_w_in", {})[i] = mm_tn(xn, dz, f"ffn{k}_dwin_{i}")
            gsmall[f"ffn{k}_norm_pre"][i], gsmall[f"ffn{k}_norm_post"][i] = dgpre, dgpost
            if k == 2:
                wmix, nm = (hgrn_out, "hgrn") if i == 0 else (fox_out, "fox")
                kb = _tile(wmix.shape[1], 512)
                dy, dgpost, dxm = nbwd_mm_nt(dh, s["ym"], norm("mix_norm_post", i), wmix, 0, 1.0, kb, f"{nm}_bwd_out")
                gbig[f"{nm}_w_out"] = {0: mm_tn(s["xm"], dy, f"{nm}_dwout", xb=wmix.shape[1] // N_DEV).reshape(N_DEV, -1, D)}
                gsmall["mix_norm_post"][i] = dgpost
                if i == 0:
                    dq, df, dv, dg, dlog, don = hgrn_bwd(dxm, s["z"], s["o"], s["states"], hgrn_lb_logits, hgrn_out_norm,
                                                         "hgrn_scan_bwd")
                    gsmall["hgrn_lb_logits"] = [dlog[0:1], dlog[1:2]]
                    gsmall["hgrn_out_norm"] = [don]
                    dzm = jnp.concatenate([dq, df, dv, dg], axis=1)
                    wmin, nmin = hgrn_in, "hgrn_w_in"
                else:
                    do, dg = fox_gate_bwd(dxm, s["o"], s["qg"], "fox_gate_bwd")
                    dq, dk_sh, dv_sh, dc4, dcq4 = fox_bwd(s["qg"], kvf, c4, s["o"], s["lse"], do, "fox_attn_bwd")
                    dzm = jnp.concatenate([dq.astype(BF), dg], axis=1)
                    wmin, nmin = qg_w, "fox_w_qg"
                dh, xn, dgpre = mm_nt_nbwd(dzm, wmin, 0, s["h_a"], norm("mix_norm_pre", i), dh, f"{nm}_bwd_in", tn=wmin.shape[3])
                gbig[nmin] = {0: mm_tn(xn, dzm, f"{nm}_dwin", yb=wmin.shape[3])}
                gsmall["mix_norm_pre"][i] = dgpre
        if i == 1:
            dct = (dc4 + dcq4).transpose(0, 2, 1, 3).reshape(n_fox, T)
            dct = jnp.pad(dct, ((0, LANES - n_fox), (0, 0)))
            dflog, db = fox_cum_bwd(dct, kvf, fcol, b_row, "fox_cum_bwd")
            gsmall["fox_b_f"] = db[:, :n_fox]
            dkvf = jnp.concatenate([dk_sh, dv_sh, dflog], axis=1)
            dh, xn, dgkv = mm_nt_nbwd(dkvf, kvf_w, 0, h_kv, _row(kv_norm), dh, "fox_kvf_bwd", tn=LANES)
            gsmall["kv_norm"] = dgkv
            dwk = mm_tn(xn, dkvf, "fox_dwkvf", yb=LANES)
            dwk = dwk.transpose(1, 0, 2).reshape(D, -1)[:, :kvf_cols]
            gbig["fox_w_kvf"] = {0: dwk.reshape(D, N_DEV, -1).transpose(1, 0, 2)}
    grad_x = dh.reshape(x.shape)

    def small_rows(n):
        g = gsmall[n]
        rows = g if isinstance(g, list) else [g]
        return [_pad_lanes(r, D) for r in rows]

    counts = {n: len(small_rows(n)) for n in small_names}
    packed = jnp.concatenate([r for n in small_names for r in small_rows(n)], axis=0)
    n_rows = packed.shape[0]
    packed = jnp.pad(packed, ((0, -n_rows % 8), (0, 0)))
    items = []
    for n in big:
        per_layer = gbig[n]
        gs = [per_layer[l] for l in sorted(per_layer)]
        items.append(tuple(gs))
    recv = scatter_grads(items, packed, "scatter_grads")

    res = {}
    for n, parts in zip(big, recv[:-1]):
        w = weights[n]
        if w.ndim == 2:
            parts = parts[:, 0]
        res[n] = adamw(parts, w, mom1[n], mom2[n], f"adamw_{n}")

    def pack(d):
        rows = []
        for n in small_names:
            a = d[n]
            rows.append(_pad_lanes(a.reshape(-1, a.shape[-1]), D))
        a = jnp.concatenate(rows, axis=0)
        return jnp.pad(a, ((0, -n_rows % 8), (0, 0)))

    sm = adamw(recv[-1], pack(weights), pack(mom1), pack(mom2), "adamw_small")
    off = 0
    for n in small_names:
        w = weights[n]
        res[n] = [a[off:off + counts[n], :w.shape[-1]].reshape(w.shape) for a in sm]
        off += counts[n]

    out = [loss, grad_x]
    for j in range(4):
        out += [res[n][j] for n in names]
    return tuple(out)
```

```python
import functools

import jax
import jax.numpy as jnp
from jax import lax
from jax.experimental import pallas as pl
from jax.experimental.pallas import tpu as pltpu

F32 = jnp.float32
BF = jnp.bfloat16
NORM_EPS = 1e-6
N_DEV = 8
HGRN_DK = 128
HGRN_CHUNK = 16
FOX_HEAD_DIM = 64
LANES = 128
ADAM_LR, ADAM_B1, ADAM_B2, ADAM_EPS, ADAM_WD, ADAM_STEP = 0.001, 0.9, 0.999, 1e-08, 0.01, 10
VMEM_LIMIT = 56 * 1024 * 1024
HI = lax.Precision.HIGHEST
NT = (((1,), (1,)), ((), ()))
TN = (((0,), (0,)), ((), ()))


def _params(n_axes):
    return pltpu.CompilerParams(dimension_semantics=("arbitrary",) * n_axes, vmem_limit_bytes=VMEM_LIMIT)


def _tile(n, want):
    t = min(n, want)
    while n % t:
        t //= 2
    return t


def _sigmoid(x):
    return 1.0 / (1.0 + jnp.exp(-x))


def _rms(x):
    r = lax.rsqrt(jnp.mean(x * x, axis=-1, keepdims=True) + NORM_EPS)
    return x * r, r


def _norm_bwd(dy, xhat, r, g):
    dxh = dy * g
    return r * (dxh - xhat * jnp.mean(dxh * xhat, axis=-1, keepdims=True))


def _colsum(x):
    return jnp.sum(x, axis=0, keepdims=True)


def _w_spec(w, blk):
    if w.ndim == 3:
        return lambda off: pl.BlockSpec((None, w.shape[1], w.shape[2]), lambda i, j: (j + off, 0, 0))
    return lambda off: pl.BlockSpec((w.shape[0], blk), lambda i, j: (0, j + off))


def norm_mm_swiglu(h, g, w3, name):
    T, D = h.shape
    nb, _, cs = w3.shape
    nh = nb // 2
    tm = _tile(T, 1024)

    def body(h_ref, g_ref, wg_ref, wu_ref, gate_ref, up_ref, a_ref, xn_ref):
        @pl.when(pl.program_id(1) == 0)
        def _():
            xh, _ = _rms(h_ref[...])
            xn_ref[...] = (xh * g_ref[...]).astype(BF)

        xn = xn_ref[...]
        gt = jnp.dot(xn, wg_ref[...], preferred_element_type=F32)
        up = jnp.dot(xn, wu_ref[...], preferred_element_type=F32)
        gate_ref[...] = gt.astype(BF)
        up_ref[...] = up.astype(BF)
        a_ref[...] = (gt * _sigmoid(gt) * up).astype(BF)

    ws = _w_spec(w3, cs)
    blk = pl.BlockSpec((None, tm, cs), lambda i, j: (j, i, 0))
    shp = jax.ShapeDtypeStruct((nh, T, cs), BF)
    return pl.pallas_call(
        body, name=name, grid=(T // tm, nh),
        in_specs=[pl.BlockSpec((tm, D), lambda i, j: (i, 0)), pl.BlockSpec((1, D), lambda i, j: (0, 0)), ws(0), ws(nh)],
        out_specs=[blk, blk, blk, pl.BlockSpec((tm, D), lambda i, j: (i, 0))],
        out_shape=[shp, shp, shp, jax.ShapeDtypeStruct((T, D), BF)], compiler_params=_params(2),
    )(h, g, w3, w3)


def norm_mm(h, g, w, name, tn=None):
    T, D = h.shape
    if w.ndim == 3:
        nb, cs = w.shape[0], w.shape[2]
    else:
        cs = tn
        nb = w.shape[1] // cs
    tm = _tile(T, 1024)

    def body(h_ref, g_ref, w_ref, z_ref, xn_ref):
        @pl.when(pl.program_id(1) == 0)
        def _():
            xh, _ = _rms(h_ref[...])
            xn_ref[...] = (xh * g_ref[...]).astype(BF)

        z_ref[...] = jnp.dot(xn_ref[...], w_ref[...], preferred_element_type=F32)

    return pl.pallas_call(
        body, name=name, grid=(T // tm, nb),
        in_specs=[pl.BlockSpec((tm, D), lambda i, j: (i, 0)), pl.BlockSpec((1, D), lambda i, j: (0, 0)),
                  _w_spec(w, cs)(0)],
        out_specs=[pl.BlockSpec((tm, cs), lambda i, j: (i, j)), pl.BlockSpec((tm, D), lambda i, j: (i, 0))],
        out_shape=[jax.ShapeDtypeStruct((T, nb * cs), F32), jax.ShapeDtypeStruct((T, D), BF)], compiler_params=_params(2),
    )(h, g, w)


def _x_spec(x, tm, kb):
    if x.ndim == 3:
        return pl.BlockSpec((None, tm, x.shape[2]), lambda i, j: (j, i, 0))
    return pl.BlockSpec((tm, kb), lambda i, j: (i, j))


def mm_norm_res(x, w2, g, h, coef, name, kb=None):
    T, D = h.shape
    if x.ndim == 3:
        nb, kb = x.shape[0], x.shape[2]
    else:
        nb = x.shape[1] // kb
    tm = _tile(T, 512)

    def body(x_ref, w_ref, h_ref, g_ref, hn_ref, y_ref, acc_ref):
        b = pl.program_id(1)

        @pl.when(b == 0)
        def _():
            acc_ref[...] = jnp.zeros_like(acc_ref)

        acc_ref[...] += jnp.dot(x_ref[...], w_ref[...], preferred_element_type=F32)

        @pl.when(b == nb - 1)
        def _():
            y = acc_ref[...]
            y_ref[...] = y
            yh, _ = _rms(y)
            hn_ref[...] = h_ref[...] + coef * (yh * g_ref[...])

    tok = pl.BlockSpec((tm, D), lambda i, j: (i, 0))
    shp = jax.ShapeDtypeStruct((T, D), F32)
    return pl.pallas_call(
        body, name=name, grid=(T // tm, nb),
        in_specs=[_x_spec(x, tm, kb), pl.BlockSpec((kb, D), lambda i, j: (j, 0)), tok,
                  pl.BlockSpec((1, D), lambda i, j: (0, 0))],
        out_specs=[tok, tok], out_shape=[shp, shp],
        scratch_shapes=[pltpu.VMEM((tm, D), F32)], compiler_params=_params(2),
    )(x, w2, h, g)


def nbwd_mm_nt(dout, y, g, w2, coef, kb, name, gate=None, up=None):
    T, D = dout.shape
    nb = w2.shape[0] // kb
    swiglu = gate is not None
    tm = _tile(T, 512)

    def body(*refs):
        if swiglu:
            dout_ref, y_ref, g_ref, w_ref, gate_ref, up_ref, dy_ref, dg_ref, da_ref, dys_ref = refs
        else:
            dout_ref, y_ref, g_ref, w_ref, dy_ref, dg_ref, da_ref, dys_ref = refs
        i, b = pl.program_id(0), pl.program_id(1)

        @pl.when((i == 0) & (b == 0))
        def _():
            dg_ref[...] = jnp.zeros_like(dg_ref)

        @pl.when(b == 0)
        def _():
            yh, r = _rms(y_ref[...])
            dyn = coef * dout_ref[...]
            dg_ref[...] += _colsum(dyn * yh)
            dy = _norm_bwd(dyn, yh, r, g_ref[...]).astype(BF)
            dys_ref[...] = dy
            dy_ref[...] = dy

        da = lax.dot_general(dys_ref[...], w_ref[...], NT, preferred_element_type=F32)
        if swiglu:
            gt = gate_ref[...].astype(F32)
            u = up_ref[...].astype(F32)
            sg = _sigmoid(gt)
            da_ref[0] = (da * u * (sg * (1.0 + gt * (1.0 - sg)))).astype(BF)
            da_ref[1] = (da * (gt * sg)).astype(BF)
        else:
            da_ref[...] = da.astype(BF)

    tok = pl.BlockSpec((tm, D), lambda i, j: (i, 0))
    vec = pl.BlockSpec((1, D), lambda i, j: (0, 0))
    in_specs = [tok, tok, vec, pl.BlockSpec((kb, D), lambda i, j: (j, 0))]
    args = [dout, y, g, w2]
    if swiglu:
        blk = pl.BlockSpec((None, tm, kb), lambda i, j: (j, i, 0))
        in_specs += [blk, blk]
        args += [gate, up]
        da_spec = pl.BlockSpec((2, None, tm, kb), lambda i, j: (0, j, i, 0))
        da_shape = jax.ShapeDtypeStruct((2, nb, T, kb), BF)
    else:
        da_spec = pl.BlockSpec((tm, kb), lambda i, j: (i, j))
        da_shape = jax.ShapeDtypeStruct((T, nb * kb), BF)
    return pl.pallas_call(
        body, name=name, grid=(T // tm, nb), in_specs=in_specs,
        out_specs=[tok, vec, da_spec],
        out_shape=[jax.ShapeDtypeStruct((T, D), BF), jax.ShapeDtypeStruct((1, D), F32), da_shape],
        scratch_shapes=[pltpu.VMEM((tm, D), BF)], compiler_params=_params(2),
    )(*args)


def mm_nt_nbwd(dz, w, h, g, dout, name, tn=None):
    T, D = h.shape
    if w.ndim == 3:
        nb, cs = w.shape[0], w.shape[2]
    else:
        cs = tn
        nb = w.shape[1] // cs
    tm = _tile(T, 512)

    def body(dz_ref, w_ref, h_ref, g_ref, dout_ref, dh_ref, dg_ref, acc_ref):
        i, b = pl.program_id(0), pl.program_id(1)

        @pl.when((i == 0) & (b == 0))
        def _():
            dg_ref[...] = jnp.zeros_like(dg_ref)

        @pl.when(b == 0)
        def _():
            acc_ref[...] = jnp.zeros_like(acc_ref)

        acc_ref[...] += lax.dot_general(dz_ref[...], w_ref[...], NT, preferred_element_type=F32)

        @pl.when(b == nb - 1)
        def _():
            xh, r = _rms(h_ref[...])
            gg = g_ref[...]
            dxn = acc_ref[...]
            dg_ref[...] += _colsum(dxn * xh)
            dh_ref[...] = dout_ref[...] + _norm_bwd(dxn, xh, r, gg)

    tok = pl.BlockSpec((tm, D), lambda i, j: (i, 0))
    vec = pl.BlockSpec((1, D), lambda i, j: (0, 0))
    return pl.pallas_call(
        body, name=name, grid=(T // tm, nb),
        in_specs=[_x_spec(dz, tm, cs), _w_spec(w, cs)(0), tok, vec, tok],
        out_specs=[tok, vec],
        out_shape=[jax.ShapeDtypeStruct((T, D), F32), jax.ShapeDtypeStruct((1, D), F32)],
        scratch_shapes=[pltpu.VMEM((tm, D), F32)], compiler_params=_params(2),
    )(dz, w, h, g, dout)


def mm_tn(x, y, name, xb=None, yb=None, x_layer=None):
    T = y.shape[-2]
    tt = _tile(T, 512)
    x_split = (x.ndim == 3 and x_layer is None) or xb is not None
    if x_layer is not None:
        xs = pl.BlockSpec((None, tt, x.shape[2]), lambda b, t: (x_layer, t, 0))
        kdim = x.shape[2]
    elif x.ndim == 3:
        xs = pl.BlockSpec((None, tt, x.shape[2]), lambda b, t: (b, t, 0))
        nb, kdim = x.shape[0], x.shape[2]
    elif xb is not None:
        xs = pl.BlockSpec((tt, xb), lambda b, t: (t, b))
        nb, kdim = x.shape[1] // xb, xb
    else:
        xs = pl.BlockSpec((tt, x.shape[1]), lambda b, t: (t, 0))
        kdim = x.shape[1]
    if x_split:
        ys = pl.BlockSpec((tt, y.shape[1]), lambda b, t: (t, 0))
        ndim = y.shape[1]
        out_spec = pl.BlockSpec((kdim, ndim), lambda b, t: (b, 0))
        out_shape = jax.ShapeDtypeStruct((nb * kdim, ndim), BF)
    else:
        if y.ndim == 3:
            ys = pl.BlockSpec((None, tt, y.shape[2]), lambda b, t: (b, t, 0))
            nb, ndim = y.shape[0], y.shape[2]
        else:
            ys = pl.BlockSpec((tt, yb), lambda b, t: (t, b))
            nb, ndim = y.shape[1] // yb, yb
        out_spec = pl.BlockSpec((None, kdim, ndim), lambda b, t: (b, 0, 0))
        out_shape = jax.ShapeDtypeStruct((nb, kdim, ndim), BF)
    nt = T // tt

    def body(x_ref, y_ref, o_ref, acc_ref):
        t = pl.program_id(1)

        @pl.when(t == 0)
        def _():
            acc_ref[...] = jnp.zeros_like(acc_ref)

        acc_ref[...] += lax.dot_general(x_ref[...].astype(BF), y_ref[...].astype(BF), TN, preferred_element_type=F32)

        @pl.when(t == nt - 1)
        def _():
            o_ref[...] = acc_ref[...].astype(BF)

    return pl.pallas_call(
        body, name=name, grid=(nb, nt), in_specs=[xs, ys], out_specs=out_spec, out_shape=out_shape,
        scratch_shapes=[pltpu.VMEM((kdim, ndim), F32)], compiler_params=_params(2),
    )(x, y)


def ple_fwd(h, gpre, wg, p3, wp, gpost, layer, name):
    T, D = h.shape
    pd = p3.shape[2]
    tm = _tile(T, 512)

    def body(h_ref, gpre_ref, wg_ref, p_ref, wp_ref, gpost_ref, hn_ref, gate_ref, pp_ref):
        x = h_ref[...]
        xh, _ = _rms(x)
        u = jnp.dot((xh * gpre_ref[...]).astype(BF), wg_ref[...], preferred_element_type=F32)
        gate = _sigmoid(u)
        pp = jnp.dot(p_ref[...].astype(BF), wp_ref[...], preferred_element_type=F32)
        yh, _ = _rms(gate * pp)
        hn_ref[...] = x + yh * gpost_ref[...]
        gate_ref[...] = gate.astype(BF)
        pp_ref[...] = pp.astype(BF)

    tok = pl.BlockSpec((tm, D), lambda i: (i, 0))
    vec = pl.BlockSpec((1, D), lambda i: (0, 0))
    return pl.pallas_call(
        body, name=name, grid=(T // tm,),
        in_specs=[tok, vec, pl.BlockSpec((D, D), lambda i: (0, 0)),
                  pl.BlockSpec((None, tm, pd), lambda i: (layer, i, 0)),
                  pl.BlockSpec((pd, D), lambda i: (0, 0)), vec],
        out_specs=[tok, tok, tok],
        out_shape=[jax.ShapeDtypeStruct((T, D), F32), jax.ShapeDtypeStruct((T, D), BF), jax.ShapeDtypeStruct((T, D), BF)],
        compiler_params=_params(1),
    )(h, gpre, wg, p3, wp, gpost)


def ple_bwd(dout, h, gate, pp, gpre, wg, gpost, name):
    T, D = h.shape
    tm = _tile(T, 512)

    def body(dout_ref, h_ref, gate_ref, pp_ref, gpre_ref, wg_ref, gpost_ref, dh_ref, du_ref, dpp_ref, xn_ref, dgpre_ref, dgpost_ref):
        @pl.when(pl.program_id(0) == 0)
        def _():
            dgpre_ref[...] = jnp.zeros_like(dgpre_ref)
            dgpost_ref[...] = jnp.zeros_like(dgpost_ref)

        dout = dout_ref[...]
        gate = gate_ref[...].astype(F32)
        pp = pp_ref[...].astype(F32)
        yh, ry = _rms(gate * pp)
        dgpost_ref[...] += _colsum(dout * yh)
        dy = _norm_bwd(dout, yh, ry, gpost_ref[...])
        dpp_ref[...] = (dy * gate).astype(BF)
        du = (dy * pp * gate * (1.0 - gate)).astype(BF)
        du_ref[...] = du
        dxn = lax.dot_general(du, wg_ref[...], NT, preferred_element_type=F32)
        xh, r = _rms(h_ref[...])
        gp = gpre_ref[...]
        dgpre_ref[...] += _colsum(dxn * xh)
        dh_ref[...] = dout + _norm_bwd(dxn, xh, r, gp)
        xn_ref[...] = (xh * gp).astype(BF)

    tok = pl.BlockSpec((tm, D), lambda i: (i, 0))
    vec = pl.BlockSpec((1, D), lambda i: (0, 0))
    bft = jax.ShapeDtypeStruct((T, D), BF)
    v32 = jax.ShapeDtypeStruct((1, D), F32)
    return pl.pallas_call(
        body, name=name, grid=(T // tm,),
        in_specs=[tok, tok, tok, tok, vec, pl.BlockSpec((D, D), lambda i: (0, 0)), vec],
        out_specs=[tok, tok, tok, tok, vec, vec],
        out_shape=[jax.ShapeDtypeStruct((T, D), F32), bft, bft, bft, v32, v32],
        compiler_params=_params(1),
    )(dout, h, gate, pp, gpre, wg, gpost)


def _chunk_tri(tb, upper):
    r = lax.broadcasted_iota(jnp.int32, (tb, tb), 0)
    c = lax.broadcasted_iota(jnp.int32, (tb, tb), 1)
    shift = HGRN_CHUNK.bit_length() - 1
    same = jnp.right_shift(r, shift) == jnp.right_shift(c, shift)
    return (same & ((c >= r) if upper else (c <= r))).astype(F32)


def _hgrn_gates(z, logits):
    lb = 1.0 / (1.0 + jnp.exp(logits[1:2, :] - logits[0:1, :]))
    e = jnp.exp(-jnp.abs(z))
    inv = 1.0 / (1.0 + e)
    sig = jnp.where(z >= 0, inv, e * inv)
    nsig = jnp.where(z >= 0, e * inv, inv)
    return lb, sig, nsig, lb + (1.0 - lb) * sig


def hgrn_fwd(z, lb_logits, out_norm, name):
    T = z.shape[0]
    W = z.shape[1] // 4
    H = W // HGRN_DK
    C = HGRN_CHUNK
    tb = _tile(T, 256)
    nch = tb // C

    def body(zq_ref, zf_ref, zv_ref, zg_ref, lbl_ref, on_ref, x_ref, o_ref, st_ref, s_scr, cum_scr, k_scr, o_scr):
        @pl.when(pl.program_id(1) == 0)
        def _():
            s_scr[...] = jnp.zeros_like(s_scr)

        lb, sig, nsig, f = _hgrn_gates(zf_ref[...], lbl_ref[...])
        cum_scr[...] = jnp.dot(_chunk_tri(tb, False), jnp.log(f), precision=HI, preferred_element_type=F32)
        k_scr[...] = (1.0 - lb) * nsig
        row = lax.broadcasted_iota(jnp.int32, (C, HGRN_DK), 0)

        def chunk(c, carry):
            r0 = pl.multiple_of(c * C, C)
            rows = pl.ds(r0, C)
            q, k, v, cu = zq_ref[rows, :], k_scr[rows, :], zv_ref[rows, :], cum_scr[rows, :]
            st = s_scr[...]
            st_ref[c] = st
            o = lax.dot_general((q * jnp.exp(cu)).astype(BF), st.astype(BF), NT, preferred_element_type=F32)
            qr = q.astype(BF).astype(F32)
            for s in range(C):
                one = pl.ds(r0 + s, 1)
                e = jnp.exp(jnp.minimum(cu - cum_scr[one, :], 0.0))
                col = jnp.sum(qr * (e * k_scr[one, :]).astype(BF).astype(F32), axis=-1, keepdims=True)
                col = jnp.where(row >= s, col, 0.0).astype(BF).astype(F32)
                o = o + col * zv_ref[one, :].astype(BF).astype(F32)
            o_scr[rows, :] = o
            last = cum_scr[pl.ds(r0 + C - 1, 1), :]
            kg = (k * jnp.exp(last - cu)).astype(BF)
            s_scr[...] = st * jnp.exp(last) + lax.dot_general(v.astype(BF), kg, TN, preferred_element_type=F32)
            return carry

        lax.fori_loop(0, nch, chunk, 0)
        o = o_scr[...]
        o_ref[...] = o
        oh, _ = _rms(o)
        g = zg_ref[...]
        x_ref[...] = (oh * on_ref[...] * (g * _sigmoid(g))).astype(BF)

    def zs(part):
        return pl.BlockSpec((tb, HGRN_DK), lambda hd, i: (i, part * H + hd))

    blk = pl.BlockSpec((tb, HGRN_DK), lambda hd, i: (i, hd))
    return pl.pallas_call(
        body, name=name, grid=(H, T // tb),
        in_specs=[zs(0), zs(1), zs(2), zs(3), pl.BlockSpec((2, HGRN_DK), lambda hd, i: (0, hd)),
                  pl.BlockSpec((1, HGRN_DK), lambda hd, i: (0, 0))],
        out_specs=[blk, blk, pl.BlockSpec((nch, None, HGRN_DK, HGRN_DK), lambda hd, i: (i, hd, 0, 0))],
        out_shape=[jax.ShapeDtypeStruct((T, W), BF), jax.ShapeDtypeStruct((T, W), F32),
                   jax.ShapeDtypeStruct((T // C, H, HGRN_DK, HGRN_DK), F32)],
        scratch_shapes=[pltpu.VMEM((HGRN_DK, HGRN_DK), F32), pltpu.VMEM((tb, HGRN_DK), F32),
                        pltpu.VMEM((tb, HGRN_DK), F32), pltpu.VMEM((tb, HGRN_DK), F32)],
        compiler_params=_params(2),
    )(z, z, z, z, lb_logits, out_norm)


def hgrn_bwd(dx, z, o, states, lb_logits, out_norm, name):
    T = z.shape[0]
    W = z.shape[1] // 4
    H = W // HGRN_DK
    C = HGRN_CHUNK
    tb = _tile(T, 256)
    nch = tb // C
    nblk = T // tb

    def body(dx_ref, zq_ref, zf_ref, zv_ref, zg_ref, o_ref, st_ref, lbl_ref, on_ref,
             dq_ref, df_ref, dv_ref, dg_ref, dl_ref, don_ref,
             ds_scr, cum_scr, k_scr, do_scr, dq_scr, dk_scr, dv_scr, dcum_scr):
        hd, i = pl.program_id(0), pl.program_id(1)

        @pl.when(i == 0)
        def _():
            ds_scr[...] = jnp.zeros_like(ds_scr)
            dl_ref[...] = jnp.zeros_like(dl_ref)

        @pl.when((i == 0) & (hd == 0))
        def _():
            don_ref[...] = jnp.zeros_like(don_ref)

        lb, sig, nsig, f = _hgrn_gates(zf_ref[...], lbl_ref[...])
        cum_scr[...] = jnp.dot(_chunk_tri(tb, False), jnp.log(f), precision=HI, preferred_element_type=F32)
        k_scr[...] = (1.0 - lb) * nsig
        oh, r = _rms(o_ref[...])
        w = on_ref[...]
        g = zg_ref[...]
        sg = _sigmoid(g)
        dxv = dx_ref[...].astype(F32)
        dg_ref[...] = (dxv * (oh * w) * (sg * (1.0 + g * (1.0 - sg)))).astype(BF)
        don = dxv * (g * sg)
        don_ref[...] += _colsum(don * oh)
        do_scr[...] = _norm_bwd(don, oh, r, w)
        row = lax.broadcasted_iota(jnp.int32, (C, HGRN_DK), 0)

        def chunk(cc, carry):
            c = nch - 1 - cc
            r0 = pl.multiple_of(c * C, C)
            rows = pl.ds(r0, C)
            q, k, v, cu, do = zq_ref[rows, :], k_scr[rows, :], zv_ref[rows, :], cum_scr[rows, :], do_scr[rows, :]
            st = st_ref[c]
            dst = ds_scr[...]
            last = cum_scr[pl.ds(r0 + C - 1, 1), :]
            lam, gam, elast = jnp.exp(cu), jnp.exp(last - cu), jnp.exp(last)
            dob, dstb = do.astype(BF), dst.astype(BF)
            dq = jnp.dot(dob, st.astype(BF), preferred_element_type=F32) * lam
            dv = lax.dot_general((k * gam).astype(BF), dstb, NT, preferred_element_type=F32)
            dk = jnp.dot(v.astype(BF), dstb, preferred_element_type=F32) * gam
            dlast = elast * _colsum(dst * st) + _colsum(dk * k)
            ds_scr[...] = dst * elast + lax.dot_general(dob, (q * lam).astype(BF), TN, preferred_element_type=F32)
            for s in range(C):
                one = pl.ds(r0 + s, 1)
                e = jnp.where(row >= s, jnp.exp(jnp.minimum(cu - cum_scr[one, :], 0.0)), 0.0)
                ks = k_scr[one, :]
                da = jnp.sum(do * zv_ref[one, :], axis=-1, keepdims=True)
                pq = q * e
                a = jnp.sum(pq * ks, axis=-1, keepdims=True)
                dq = dq + da * e * ks
                dk = jnp.where(row == s, dk + _colsum(da * pq), dk)
                dv = jnp.where(row == s, dv + _colsum(a * do), dv)
            dq_scr[rows, :] = dq
            dk_scr[rows, :] = dk
            dv_scr[rows, :] = dv
            dcum_scr[rows, :] = q * dq - k * dk + jnp.where(row == C - 1, dlast, 0.0)
            return carry

        lax.fori_loop(0, nch, chunk, 0)
        dlf = jnp.dot(_chunk_tri(tb, True), dcum_scr[...], precision=HI, preferred_element_type=F32)
        dk = dk_scr[...]
        common = (1.0 - lb) * sig * nsig
        df_ref[...] = (dlf * common / f - dk * common).astype(BF)
        dq_ref[...] = dq_scr[...].astype(BF)
        dv_ref[...] = dv_scr[...].astype(BF)
        dl0 = _colsum(dlf * nsig / f - dk * nsig) * lb * (1.0 - lb)
        dl_ref[...] += jnp.where(lax.broadcasted_iota(jnp.int32, (2, HGRN_DK), 0) == 0, dl0, -dl0)

    def zs(part):
        return pl.BlockSpec((tb, HGRN_DK), lambda hd, i: (nblk - 1 - i, part * H + hd))

    blk = pl.BlockSpec((tb, HGRN_DK), lambda hd, i: (nblk - 1 - i, hd))
    bft = jax.ShapeDtypeStruct((T, W), BF)
    scr = pltpu.VMEM((tb, HGRN_DK), F32)
    return pl.pallas_call(
        body, name=name, grid=(H, nblk),
        in_specs=[blk, zs(0), zs(1), zs(2), zs(3), blk,
                  pl.BlockSpec((nch, None, HGRN_DK, HGRN_DK), lambda hd, i: (nblk - 1 - i, hd, 0, 0)),
                  pl.BlockSpec((2, HGRN_DK), lambda hd, i: (0, hd)), pl.BlockSpec((1, HGRN_DK), lambda hd, i: (0, 0))],
        out_specs=[blk, blk, blk, blk, pl.BlockSpec((2, HGRN_DK), lambda hd, i: (0, hd)),
                   pl.BlockSpec((1, HGRN_DK), lambda hd, i: (0, 0))],
        out_shape=[bft, bft, bft, bft, jax.ShapeDtypeStruct((2, W), F32), jax.ShapeDtypeStruct((1, HGRN_DK), F32)],
        scratch_shapes=[pltpu.VMEM((HGRN_DK, HGRN_DK), F32), scr, scr, scr, scr, scr, scr, scr],
        compiler_params=_params(2),
    )(dx, z, z, z, z, o, states, lb_logits, out_norm)


def _log_sigmoid(x):
    return jnp.minimum(x, 0.0) - jnp.log(1.0 + jnp.exp(-jnp.abs(x)))


def _tri(n, upper):
    r = lax.broadcasted_iota(jnp.int32, (n, n), 0)
    c = lax.broadcasted_iota(jnp.int32, (n, n), 1)
    return ((r >= c) if upper else (c <= r)).astype(F32)


def fox_cum_fwd(kvf, fcol, b_row, name):
    T = kvf.shape[0]
    tb = _tile(T, 512)

    def body(x_ref, b_ref, ct_ref, carry_ref):
        @pl.when(pl.program_id(0) == 0)
        def _():
            carry_ref[...] = jnp.zeros_like(carry_ref)

        lf = _log_sigmoid(x_ref[...] + b_ref[...])
        cum = jnp.dot(_tri(tb, False), lf, precision=HI, preferred_element_type=F32) + carry_ref[...]
        carry_ref[...] += _colsum(lf)
        ct_ref[...] = cum.T

    return pl.pallas_call(
        body, name=name, grid=(T // tb,),
        in_specs=[pl.BlockSpec((tb, LANES), lambda i: (i, fcol)), pl.BlockSpec((1, LANES), lambda i: (0, 0))],
        out_specs=pl.BlockSpec((LANES, tb), lambda i: (0, i)), out_shape=jax.ShapeDtypeStruct((LANES, T), F32),
        scratch_shapes=[pltpu.VMEM((1, LANES), F32)], compiler_params=_params(1),
    )(kvf, b_row)


def fox_cum_bwd(dct, kvf, fcol, b_row, name):
    T = kvf.shape[0]
    tb = _tile(T, 512)
    nblk = T // tb

    def body(dc_ref, x_ref, b_ref, df_ref, db_ref, carry_ref):
        @pl.when(pl.program_id(0) == 0)
        def _():
            carry_ref[...] = jnp.zeros_like(carry_ref)
            db_ref[...] = jnp.zeros_like(db_ref)

        dc = dc_ref[...]
        dlf_t = jnp.dot(dc, _tri(tb, True), precision=HI, preferred_element_type=F32) + carry_ref[...]
        carry_ref[...] += jnp.sum(dc, axis=1, keepdims=True)
        x = x_ref[...] + b_ref[...]
        df = dlf_t.T * _sigmoid(-x)
        df_ref[...] = df.astype(BF)
        db_ref[...] += _colsum(df)

    return pl.pallas_call(
        body, name=name, grid=(nblk,),
        in_specs=[pl.BlockSpec((LANES, tb), lambda i: (0, nblk - 1 - i)),
                  pl.BlockSpec((tb, LANES), lambda i: (nblk - 1 - i, fcol)), pl.BlockSpec((1, LANES), lambda i: (0, 0))],
        out_specs=[pl.BlockSpec((tb, LANES), lambda i: (nblk - 1 - i, 0)), pl.BlockSpec((1, LANES), lambda i: (0, 0))],
        out_shape=[jax.ShapeDtypeStruct((T, LANES), BF), jax.ShapeDtypeStruct((1, LANES), F32)],
        scratch_shapes=[pltpu.VMEM((LANES, 1), F32)], compiler_params=_params(1),
    )(dct, kvf, b_row)


def fox_fwd(qg, kvf, c4, name):
    T = qg.shape[0]
    W = qg.shape[1] // 2
    NP = W // LANES
    tq = _tile(T, 256)
    scale = FOX_HEAD_DIM ** -0.5

    def body(q_ref, g_ref, k_ref, v_ref, c_ref, x_ref, o_ref, lse_ref):
        i = pl.program_id(1)
        half = lax.broadcasted_iota(jnp.int32, (tq, LANES), 1) < FOX_HEAD_DIM
        q2 = q_ref[...] * scale
        qs = (jnp.where(half, q2, 0.0).astype(BF), jnp.where(half, 0.0, q2).astype(BF))
        causal = lax.broadcasted_iota(jnp.int32, (tq, tq), 1) <= lax.broadcasted_iota(jnp.int32, (tq, tq), 0)

        def step(j, carry, diag):
            rows = pl.ds(pl.multiple_of(j * tq, tq), tq)
            k = k_ref[rows, :].astype(BF)
            v = v_ref[rows, :].astype(BF)
            out = []
            for a in range(2):
                m, l, acc = carry[a]
                s = lax.dot_general(qs[a], k, NT, preferred_element_type=F32) - c_ref[j, a:a + 1, :]
                if diag:
                    s = jnp.where(causal, s, -1e30)
                mn = jnp.maximum(m, jnp.max(s, axis=-1, keepdims=True))
                alpha = jnp.exp(m - mn)
                p = jnp.exp(s - mn)
                l = l * alpha + jnp.sum(p, axis=-1, keepdims=True)
                acc = acc * alpha + jnp.dot(p.astype(BF), v, preferred_element_type=F32)
                out.append((mn, l, acc))
            return tuple(out)

        init = (jnp.full((tq, 1), -1e30, F32), jnp.zeros((tq, 1), F32), jnp.zeros((tq, LANES), F32))
        carry = lax.fori_loop(0, i, lambda j, c: step(j, c, False), (init, init))
        (ma, la, acca), (mb, lb, accb) = step(i, carry, True)
        o = jnp.where(half, acca / la, accb / lb)
        o_ref[...] = o
        lse_ref[...] = jnp.where(half, ma + jnp.log(la), mb + jnp.log(lb))
        x_ref[...] = (o * _sigmoid(g_ref[...])).astype(BF)

    blk = pl.BlockSpec((tq, LANES), lambda hp, i: (i, hp))
    f32t = jax.ShapeDtypeStruct((T, W), F32)
    return pl.pallas_call(
        body, name=name, grid=(NP, T // tq),
        in_specs=[blk, pl.BlockSpec((tq, LANES), lambda hp, i: (i, NP + hp)),
                  pl.BlockSpec((T, LANES), lambda hp, i: (0, hp)), pl.BlockSpec((T, LANES), lambda hp, i: (0, NP + hp)),
                  pl.BlockSpec((None, T // tq, 2, tq), lambda hp, i: (hp, 0, 0, 0))],
        out_specs=[blk, blk, blk], out_shape=[jax.ShapeDtypeStruct((T, W), BF), f32t, f32t],
        compiler_params=_params(2),
    )(qg, qg, kvf, kvf, c4)


def fox_gate_bwd(dx, o, qg, name):
    T, W = o.shape
    tm = _tile(T, 512)

    def body(dx_ref, o_ref, g_ref, do_ref, dg_ref):
        dxv = dx_ref[...].astype(F32)
        sg = _sigmoid(g_ref[...])
        do_ref[...] = dxv * sg
        dg_ref[...] = (dxv * o_ref[...] * sg * (1.0 - sg)).astype(BF)

    tok = pl.BlockSpec((tm, W), lambda i: (i, 0))
    return pl.pallas_call(
        body, name=name, grid=(T // tm,),
        in_specs=[tok, tok, pl.BlockSpec((tm, W), lambda i: (i, 1))], out_specs=[tok, tok],
        out_shape=[jax.ShapeDtypeStruct((T, W), F32), jax.ShapeDtypeStruct((T, W), BF)], compiler_params=_params(1),
    )(dx, o, qg)


def fox_bwd(qg, kvf, c4, o, lse, do, name):
    T = qg.shape[0]
    W = qg.shape[1] // 2
    NP = W // LANES
    tq = _tile(T, 256)
    nq = T // tq
    scale = FOX_HEAD_DIM ** -0.5

    def body(q_ref, do_ref, o_ref, lse_ref, k_ref, v_ref, c_ref, dq_ref, dk_ref, dv_ref, dc_ref, dcq_ref):
        j = pl.program_id(1)

        @pl.when(j == 0)
        def _():
            dq_ref[...] = jnp.zeros_like(dq_ref)
            dcq_ref[...] = jnp.zeros_like(dcq_ref)

        ones = jnp.full((8, tq), 0.125, F32)

        half = lax.broadcasted_iota(jnp.int32, (tq, LANES), 1) < FOX_HEAD_DIM
        halves = (half, jnp.logical_not(half))
        k2, v2 = k_ref[...], v_ref[...]
        ks = [jnp.where(hm, k2, 0.0).astype(BF) for hm in halves]
        vs = [jnp.where(hm, v2, 0.0).astype(BF) for hm in halves]
        cs = [c_ref[j, a:a + 1, :] for a in range(2)]
        causal = lax.broadcasted_iota(jnp.int32, (tq, tq), 1) <= lax.broadcasted_iota(jnp.int32, (tq, tq), 0)

        def step(i, carry, diag):
            dk, dv, dca, dcb = carry
            dcs = [dca, dcb]
            rows = pl.ds(pl.multiple_of(i * tq, tq), tq)
            q2 = q_ref[rows, :] * scale
            do2 = do_ref[rows, :]
            dd = do2 * o_ref[rows, :]
            lse2 = lse_ref[rows, :]
            dq = jnp.zeros((tq, LANES), F32)
            for a in range(2):
                hm = halves[a]
                dsum = jnp.sum(jnp.where(hm, dd, 0.0), axis=-1, keepdims=True)
                lse_a = jnp.max(jnp.where(hm, lse2, -jnp.inf), axis=-1, keepdims=True)
                qa = jnp.where(hm, q2, 0.0).astype(BF)
                doa = jnp.where(hm, do2, 0.0).astype(BF)
                s = lax.dot_general(qa, ks[a], NT, preferred_element_type=F32) - cs[a]
                p = jnp.exp(s - lse_a)
                if diag:
                    p = jnp.where(causal, p, 0.0)
                dp = lax.dot_general(doa, vs[a], NT, preferred_element_type=F32)
                ds = p * (dp - dsum)
                dsb = ds.astype(BF)
                dv = dv + lax.dot_general(p.astype(BF), doa, TN, preferred_element_type=F32)
                dk = dk + lax.dot_general(dsb, qa, TN, preferred_element_type=F32)
                dq = dq + jnp.dot(dsb, ks[a], preferred_element_type=F32)
                dcs[a] = dcs[a] - _colsum(ds)
                dcq_ref[i, a:a + 1, :] += _colsum(lax.dot_general(ones, ds, NT, precision=HI, preferred_element_type=F32))
            dq_ref[rows, :] += dq * scale
            return dk, dv, dcs[0], dcs[1]

        zero = jnp.zeros((tq, LANES), F32)
        zc = jnp.zeros((1, tq), F32)
        carry = step(j, (zero, zero, zc, zc), True)
        dk, dv, dca, dcb = lax.fori_loop(j + 1, nq, lambda i, c: step(i, c, False), carry)
        dk_ref[...] = dk.astype(BF)
        dv_ref[...] = dv.astype(BF)
        dc_ref[...] = jnp.where(lax.broadcasted_iota(jnp.int32, (2, tq), 0) == 0, dca, dcb)

    full = lambda col: pl.BlockSpec((T, LANES), lambda hp, j: (0, col(hp)))
    kv = lambda off: pl.BlockSpec((tq, LANES), lambda hp, j: (j, off + hp))
    tile = pl.BlockSpec((tq, LANES), lambda hp, j: (j, hp))
    bft = jax.ShapeDtypeStruct((T, W), BF)
    return pl.pallas_call(
        body, name=name, grid=(NP, nq),
        in_specs=[full(lambda hp: hp), full(lambda hp: hp), full(lambda hp: hp), full(lambda hp: hp), kv(0), kv(NP),
                  pl.BlockSpec((None, nq, 2, tq), lambda hp, j: (hp, 0, 0, 0))],
        out_specs=[full(lambda hp: hp), tile, tile, pl.BlockSpec((None, None, 2, tq), lambda hp, j: (hp, j, 0, 0)),
                   pl.BlockSpec((None, nq, 2, tq), lambda hp, j: (hp, 0, 0, 0))],
        out_shape=[jax.ShapeDtypeStruct((T, W), F32), bft, bft, jax.ShapeDtypeStruct((NP, nq, 2, tq), F32),
                   jax.ShapeDtypeStruct((NP, nq, 2, tq), F32)],
        compiler_params=_params(2),
    )(qg, do, o, lse, kvf, kvf, c4)


def loss_fwd_bwd(y, target, name):
    T, D = y.shape
    tm = _tile(T, 512)

    def body(y_ref, t_ref, dy_ref, l_ref):
        @pl.when(pl.program_id(0) == 0)
        def _():
            l_ref[...] = jnp.zeros_like(l_ref)

        d = y_ref[...] - t_ref[...]
        dy_ref[...] = d * (1.0 / D)
        l_ref[...] += 0.5 * jnp.sum(jnp.mean(d * d, axis=-1, keepdims=True), axis=0, keepdims=True)

    tok = pl.BlockSpec((tm, D), lambda i: (i, 0))
    return pl.pallas_call(
        body, name=name, grid=(T // tm,), in_specs=[tok, tok],
        out_specs=[tok, pl.BlockSpec((1, 1), lambda i: (0, 0))],
        out_shape=[jax.ShapeDtypeStruct((T, D), F32), jax.ShapeDtypeStruct((1, 1), F32)], compiler_params=_params(1),
    )(y, target)


def adamw(parts, w, m, v, layer, prev, name):
    L, R, C = w.shape
    tr = _tile(R, 256)
    c1 = 1.0 / (1.0 - ADAM_B1 ** ADAM_STEP)
    c2 = 1.0 / (1.0 - ADAM_B2 ** ADAM_STEP)

    def body(p_ref, w_ref, m_ref, v_ref, *rest):
        g_ref, d_ref, mo_ref, vo_ref = rest[-4:]
        g = p_ref[0].astype(F32)
        for d in range(1, N_DEV):
            g = g + p_ref[d].astype(F32)
        mn = ADAM_B1 * m_ref[...] + (1.0 - ADAM_B1) * g
        vn = ADAM_B2 * v_ref[...] + (1.0 - ADAM_B2) * (g * g)
        g_ref[...] = g
        mo_ref[...] = mn
        vo_ref[...] = vn
        d_ref[...] = -ADAM_LR * ((mn * c1) / (jnp.sqrt(vn * c2) + ADAM_EPS) + ADAM_WD * w_ref[...])

    blk = pl.BlockSpec((None, tr, C), lambda i: (layer, i, 0))
    shp = jax.ShapeDtypeStruct((L, R, C), F32)
    in_specs = [pl.BlockSpec((N_DEV, tr, C), lambda i: (0, i, 0)), blk, blk, blk]
    args = [parts, w, m, v]
    aliases = {}
    if prev is not None:
        in_specs += [pl.BlockSpec(memory_space=pl.ANY)] * 4
        args += list(prev)
        aliases = {4 + j: j for j in range(4)}
    return pl.pallas_call(
        body, name=name, grid=(R // tr,), in_specs=in_specs, out_specs=[blk, blk, blk, blk],
        out_shape=[shp, shp, shp, shp], input_output_aliases=aliases, compiler_params=_params(1),
    )(*args)


HBM_SPEC = pl.BlockSpec(memory_space=pltpu.HBM)
SEM_SPEC = pl.BlockSpec(memory_space=pltpu.SEMAPHORE)
EFFECT = pltpu.SideEffectType.DATAFLOW_SIDE_EFFECTING


def _mesh_pos():
    return lax.axis_index("x"), lax.axis_index("y"), lax.axis_index("c")


def _flip(v, bit):
    return v + bit - 2 * v * bit


def _peer(pos, delta):
    x, y, c = pos
    px, py, pc = _flip(x, (delta >> 2) & 1), _flip(y, (delta >> 1) & 1), _flip(c, delta & 1)
    return (px, py, pc), 4 * px + 2 * py + pc


def _me():
    x, y, c = _mesh_pos()
    return 4 * x + 2 * y + c


def _copies(src_refs, land_refs, whole, send, recv, incoming):
    pos = _mesh_pos()
    me = 4 * pos[0] + 2 * pos[1] + pos[2]
    out = []
    for k in range(len(src_refs)):
        for d in range(1, N_DEV):
            dev, idx = _peer(pos, d)
            j = k * (N_DEV - 1) + d - 1
            src = src_refs[k] if whole[k] else src_refs[k].at[idx]
            out.append(pltpu.make_async_remote_copy(
                src_ref=src, dst_ref=land_refs[k].at[idx if incoming else me], send_sem=send.at[j], recv_sem=recv.at[j],
                device_id=dev, device_id_type=pl.DeviceIdType.MESH))
    return out


def exchange_start(srcs, lands, whole, name):
    n = len(srcs)

    def body(*refs):
        for copy in _copies(refs[:n], refs[n:2 * n], whole, refs[2 * n], refs[2 * n + 1], False):
            copy.start()
        refs[-1][...] = jnp.zeros_like(refs[-1])

    sems = pltpu.SemaphoreType.DMA((n * (N_DEV - 1),))
    thru = [pltpu.HBM(a.shape, a.dtype) for a in list(srcs) + list(lands)]
    res = pl.pallas_call(
        body, name=name, in_specs=[HBM_SPEC] * (2 * n),
        out_specs=[SEM_SPEC, SEM_SPEC] + [HBM_SPEC] * (2 * n) + [pl.BlockSpec(memory_space=pltpu.VMEM)],
        out_shape=[sems, sems] + thru + [jax.ShapeDtypeStruct((8, LANES), F32)],
        input_output_aliases={j: 2 + j for j in range(2 * n)},
        compiler_params=pltpu.CompilerParams(has_side_effects=EFFECT),
    )(*[pltpu.with_memory_space_constraint(a, pltpu.HBM) for a in list(srcs) + list(lands)])
    return dict(send=res[0], recv=res[1], srcs=res[2:2 + n], lands=res[2 + n:2 + 2 * n], whole=whole, token=res[-1])


def exchange_wait(handle, after, name):
    n = len(handle["srcs"])
    whole = handle["whole"]

    def body(*refs):
        for copy in _copies(refs[:n], refs[n:2 * n], whole, refs[2 * n], refs[2 * n + 1], False):
            copy.wait_send()
        for copy in _copies(refs[:n], refs[n:2 * n], whole, refs[2 * n], refs[2 * n + 1], True):
            copy.wait_recv()

    bufs = list(handle["srcs"]) + list(handle["lands"])
    res = pl.pallas_call(
        body, name=name, in_specs=[HBM_SPEC] * (2 * n) + [SEM_SPEC, SEM_SPEC] + [pl.BlockSpec(memory_space=pl.ANY)] * len(after),
        out_specs=[HBM_SPEC] * (2 * n), out_shape=[pltpu.HBM(a.shape, a.dtype) for a in bufs],
        input_output_aliases={j: j for j in range(2 * n)},
        compiler_params=pltpu.CompilerParams(has_side_effects=EFFECT),
    )(*bufs, handle["send"], handle["recv"], *after)
    return list(res[n:])


def _landing(own, whole):
    me = _me()
    if not whole:
        own = lax.dynamic_index_in_dim(own, me, 0, keepdims=False)
    buf = lax.empty((N_DEV,) + own.shape, own.dtype)
    return lax.dynamic_update_slice(buf, own[None], (me,) + (0,) * own.ndim)


def _row(v):
    return v.reshape(1, -1)


def _pad_lanes(v, n):
    return jnp.pad(v, ((0, 0), (0, n - v.shape[1])))


GATHER_GROUPS = (
    ("ffn1_0", (("ffn1_w_in", 0), ("ffn1_w_out", 0))),
    ("hgrn", (("hgrn_w_in", 0), ("hgrn_w_out", 0))),
    ("rest_0", (("ffn2_w_in", 0), ("ffn2_w_out", 0), ("ple_w_gate", 0), ("ple_w_proj", 0), ("fox_w_kvf", 0))),
    ("layer_1", (("ffn1_w_in", 1), ("ffn1_w_out", 1), ("fox_w_qg", 0), ("fox_w_out", 0), ("ffn2_w_in", 1),
                 ("ffn2_w_out", 1), ("ple_w_gate", 1), ("ple_w_proj", 1))),
)


def kernel(x, p, ffn1_norm_pre, ffn1_w_in, ffn1_w_out, ffn1_norm_post, mix_norm_pre, mix_norm_post, ffn2_norm_pre, ffn2_w_in, ffn2_w_out, ffn2_norm_post, hgrn_w_in, hgrn_lb_logits, hgrn_out_norm, hgrn_w_out, kv_norm, fox_w_kvf, fox_b_f, fox_w_qg, fox_w_out, ple_norm_pre, ple_w_gate, ple_w_proj, ple_norm_post, loss_target, m_ffn1_norm_pre, m_ffn1_w_in, m_ffn1_w_out, m_ffn1_norm_post, m_mix_norm_pre, m_mix_norm_post, m_ffn2_norm_pre, m_ffn2_w_in, m_ffn2_w_out, m_ffn2_norm_post, m_hgrn_w_in, m_hgrn_lb_logits, m_hgrn_out_norm, m_hgrn_w_out, m_kv_norm, m_fox_w_kvf, m_fox_b_f, m_fox_w_qg, m_fox_w_out, m_ple_norm_pre, m_ple_w_gate, m_ple_w_proj, m_ple_norm_post, v_ffn1_norm_pre, v_ffn1_w_in, v_ffn1_w_out, v_ffn1_norm_post, v_mix_norm_pre, v_mix_norm_post, v_ffn2_norm_pre, v_ffn2_w_in, v_ffn2_w_out, v_ffn2_norm_post, v_hgrn_w_in, v_hgrn_lb_logits, v_hgrn_out_norm, v_hgrn_w_out, v_kv_norm, v_fox_w_kvf, v_fox_b_f, v_fox_w_qg, v_fox_w_out, v_ple_norm_pre, v_ple_w_gate, v_ple_w_proj, v_ple_norm_post):
    weights = dict(ffn1_norm_pre=ffn1_norm_pre, ffn1_w_in=ffn1_w_in, ffn1_w_out=ffn1_w_out, ffn1_norm_post=ffn1_norm_post, mix_norm_pre=mix_norm_pre, mix_norm_post=mix_norm_post, ffn2_norm_pre=ffn2_norm_pre, ffn2_w_in=ffn2_w_in, ffn2_w_out=ffn2_w_out, ffn2_norm_post=ffn2_norm_post, hgrn_w_in=hgrn_w_in, hgrn_lb_logits=hgrn_lb_logits, hgrn_out_norm=hgrn_out_norm, hgrn_w_out=hgrn_w_out, kv_norm=kv_norm, fox_w_kvf=fox_w_kvf, fox_b_f=fox_b_f, fox_w_qg=fox_w_qg, fox_w_out=fox_w_out, ple_norm_pre=ple_norm_pre, ple_w_gate=ple_w_gate, ple_w_proj=ple_w_proj, ple_norm_post=ple_norm_post)
    mom1 = dict(ffn1_norm_pre=m_ffn1_norm_pre, ffn1_w_in=m_ffn1_w_in, ffn1_w_out=m_ffn1_w_out, ffn1_norm_post=m_ffn1_norm_post, mix_norm_pre=m_mix_norm_pre, mix_norm_post=m_mix_norm_post, ffn2_norm_pre=m_ffn2_norm_pre, ffn2_w_in=m_ffn2_w_in, ffn2_w_out=m_ffn2_w_out, ffn2_norm_post=m_ffn2_norm_post, hgrn_w_in=m_hgrn_w_in, hgrn_lb_logits=m_hgrn_lb_logits, hgrn_out_norm=m_hgrn_out_norm, hgrn_w_out=m_hgrn_w_out, kv_norm=m_kv_norm, fox_w_kvf=m_fox_w_kvf, fox_b_f=m_fox_b_f, fox_w_qg=m_fox_w_qg, fox_w_out=m_fox_w_out, ple_norm_pre=m_ple_norm_pre, ple_w_gate=m_ple_w_gate, ple_w_proj=m_ple_w_proj, ple_norm_post=m_ple_norm_post)
    mom2 = dict(ffn1_norm_pre=v_ffn1_norm_pre, ffn1_w_in=v_ffn1_w_in, ffn1_w_out=v_ffn1_w_out, ffn1_norm_post=v_ffn1_norm_post, mix_norm_pre=v_mix_norm_pre, mix_norm_post=v_mix_norm_post, ffn2_norm_pre=v_ffn2_norm_pre, ffn2_w_in=v_ffn2_w_in, ffn2_w_out=v_ffn2_w_out, ffn2_norm_post=v_ffn2_norm_post, hgrn_w_in=v_hgrn_w_in, hgrn_lb_logits=v_hgrn_lb_logits, hgrn_out_norm=v_hgrn_out_norm, hgrn_w_out=v_hgrn_w_out, kv_norm=v_kv_norm, fox_w_kvf=v_fox_w_kvf, fox_b_f=v_fox_b_f, fox_w_qg=v_fox_w_qg, fox_w_out=v_fox_w_out, ple_norm_pre=v_ple_norm_pre, ple_w_gate=v_ple_w_gate, ple_w_proj=v_ple_w_proj, ple_norm_post=v_ple_norm_post)
    names = list(weights)
    big = ["ffn1_w_in", "ffn1_w_out", "ffn2_w_in", "ffn2_w_out", "hgrn_w_in", "hgrn_w_out", "fox_w_kvf", "fox_w_qg",
           "fox_w_out", "ple_w_gate", "ple_w_proj"]
    small_names = [n for n in names if n not in big]

    T, D = x.shape[1], x.shape[2]
    depth = p.shape[0]
    h0 = x.reshape(T, D)
    target = loss_target.reshape(T, D)
    p3 = p.reshape(depth, T, p.shape[3])
    n_fox = fox_b_f.shape[0]
    fox_w = n_fox * FOX_HEAD_DIM
    fcol = 2 * fox_w // LANES
    b_row = _pad_lanes(_row(fox_b_f), LANES)

    tok = jnp.zeros((), F32)
    handles = {}
    for gname, keys in GATHER_GROUPS:
        shards = []
        for n, l in keys:
            w = weights[n]
            shards.append(((w[l] if w.ndim == 3 else w) + tok).astype(BF))
        handles[gname] = exchange_start(shards, [_landing(s, True) for s in shards], [True] * len(keys), f"gather_start_{gname}")
        tok = handles[gname]["token"][0, 0]
    W = {}

    def arrive(gname, after):
        lands = exchange_wait(handles[gname], after, f"gather_wait_{gname}")
        W.update(dict(zip(dict(GATHER_GROUPS)[gname], lands)))

    def w_rows(n, l):
        return W[n, l].reshape(-1, D)

    norm = lambda name, i: weights[name][i:i + 1]

    saved = []
    h = h0
    kvf = c4 = kvf_w = None
    tq = _tile(T, 256)
    for i in range(depth):
        s = {}
        if i == 0:
            arrive("ffn1_0", [handles["layer_1"]["token"]])
        for k in (1, 2):
            if k == 2:
                s["h_a"] = h
                if i == 0:
                    arrive("hgrn", [h])
                    z, xn = norm_mm(h, norm("mix_norm_pre", i), W["hgrn_w_in", 0], "hgrn_in")
                    xm, o, states = hgrn_fwd(z, hgrn_lb_logits, hgrn_out_norm, "hgrn_scan")
                    s.update(z=z, o=o, states=states)
                    wmix = w_rows("hgrn_w_out", 0)
                else:
                    qg, xn = norm_mm(h, norm("mix_norm_pre", i), W["fox_w_qg", 0], "fox_qg")
                    xm, o, lse = fox_fwd(qg, kvf, c4, "fox_attn")
                    s.update(qg=qg, o=o, lse=lse)
                    wmix = w_rows("fox_w_out", 0)
                h, ym = mm_norm_res(xm, wmix, norm("mix_norm_post", i), h, 1.0, f"mix_out_{i}", kb=_tile(xm.shape[1], 512))
                s.update(xm=xm, ym=ym, xn_mix=xn)
                if i == 0:
                    arrive("rest_0", [h])
            gate, up, a, xn = norm_mm_swiglu(h, norm(f"ffn{k}_norm_pre", i), W[f"ffn{k}_w_in", i], f"ffn{k}_in_{i}")
            hn, y = mm_norm_res(a, w_rows(f"ffn{k}_w_out", i), norm(f"ffn{k}_norm_post", i), h, 0.5, f"ffn{k}_out_{i}")
            s[f"ffn{k}"] = (h, gate, up, a, y, xn)
            h = hn
        s["h_c"] = h
        ple_proj = W["ple_w_proj", i].transpose(1, 0, 2).reshape(p.shape[3], D)
        h, pgate, pp = ple_fwd(h, norm("ple_norm_pre", i), w_rows("ple_w_gate", i), p3, ple_proj, norm("ple_norm_post", i),
                               i, f"ple_{i}")
        s.update(pgate=pgate, pp=pp)
        saved.append(s)
        if i == 0:
            kvf_nat = W["fox_w_kvf", 0].transpose(1, 0, 2).reshape(D, -1)
            kvf_cols = kvf_nat.shape[1]
            kvf_w = _pad_lanes(kvf_nat, 2 * fox_w + LANES)
            kvf, xn_kv = norm_mm(h, _row(kv_norm), kvf_w, "fox_kvf", tn=LANES)
            ct = fox_cum_fwd(kvf, fcol, b_row, "fox_cum")
            c4 = ct[:n_fox].reshape(n_fox // 2, 2, T // tq, tq).transpose(0, 2, 1, 3)
            h_kv = h
            arrive("layer_1", [h])

    dh, loss_part = loss_fwd_bwd(h, target, "loss")
    loss = lax.psum(loss_part[0, 0], ("x", "y", "c"))

    gsmall = {n: [None] * weights[n].shape[0] if weights[n].ndim == 2 else None for n in small_names}
    sent = []

    def send(gname, keys, srcs, whole):
        lands = [_landing(a, w) for a, w in zip(srcs, whole)]
        hd = exchange_start(srcs, lands, whole, f"scatter_start_{gname}")
        sent.append((gname, keys, hd))
        return hd["token"][0:1, 0:1]

    def send_grads(gname, grads):
        return send(gname, list(grads), list(grads.values()), [False] * len(grads))

    for i in reversed(range(depth)):
        s = saved[i]
        grads = {}
        dh, du, dpp, xn, dgpre, dgpost = ple_bwd(dh, s["h_c"], s["pgate"], s["pp"], norm("ple_norm_pre", i),
                                                 w_rows("ple_w_gate", i), norm("ple_norm_post", i), f"ple_bwd_{i}")
        gsmall["ple_norm_pre"][i], gsmall["ple_norm_post"][i] = dgpre, dgpost
        grads["ple_w_gate", i] = mm_tn(xn, du, f"ple_dgate_{i}", xb=D // N_DEV).reshape(N_DEV, -1, D)
        grads["ple_w_proj", i] = mm_tn(p3, dpp, f"ple_dproj_{i}", x_layer=i, yb=D // N_DEV)
        for k in (2, 1):
            hin, gate, up, a, y, xn = s[f"ffn{k}"]
            ffn_cs = gate.shape[2]
            dy, dgpost, dz = nbwd_mm_nt(dh, y, norm(f"ffn{k}_norm_post", i), w_rows(f"ffn{k}_w_out", i), 0.5, ffn_cs,
                                        f"ffn{k}_bwd_out_{i}", gate=gate, up=up)
            dz = dz.reshape(-1, T, ffn_cs)
            grads[f"ffn{k}_w_out", i] = mm_tn(a, dy, f"ffn{k}_dwout_{i}").reshape(N_DEV, -1, D)
            grads[f"ffn{k}_w_in", i] = mm_tn(xn, dz, f"ffn{k}_dwin_{i}")
            tokv = send_grads(f"ffn{k}_{i}", grads)
            grads = {}
            dh, dgpre = mm_nt_nbwd(dz, W[f"ffn{k}_w_in", i], hin, norm(f"ffn{k}_norm_pre", i) + tokv, dh, f"ffn{k}_bwd_in_{i}")
            gsmall[f"ffn{k}_norm_pre"][i], gsmall[f"ffn{k}_norm_post"][i] = dgpre, dgpost
            if k == 2:
                nm = "hgrn" if i == 0 else "fox"
                wmix = w_rows(f"{nm}_w_out", 0)
                dy, dgpost, dxm = nbwd_mm_nt(dh, s["ym"], norm("mix_norm_post", i), wmix, 1.0, _tile(wmix.shape[0], 512),
                                             f"{nm}_bwd_out")
                grads[f"{nm}_w_out", 0] = mm_tn(s["xm"], dy, f"{nm}_dwout", xb=wmix.shape[0] // N_DEV).reshape(N_DEV, -1, D)
                gsmall["mix_norm_post"][i] = dgpost
                if i == 0:
                    dq, df, dv, dg, dlog, don = hgrn_bwd(dxm, s["z"], s["o"], s["states"], hgrn_lb_logits, hgrn_out_norm,
                                                         "hgrn_scan_bwd")
                    gsmall["hgrn_lb_logits"] = [dlog[0:1], dlog[1:2]]
                    gsmall["hgrn_out_norm"] = [don]
                    dzm = jnp.concatenate([dq, df, dv, dg], axis=1)
                    nmin = "hgrn_w_in"
                else:
                    do, dg = fox_gate_bwd(dxm, s["o"], s["qg"], "fox_gate_bwd")
                    dq, dk_sh, dv_sh, dc4, dcq4 = fox_bwd(s["qg"], kvf, c4, s["o"], s["lse"], do, "fox_attn_bwd")
                    dzm = jnp.concatenate([dq.astype(BF), dg], axis=1)
                    nmin = "fox_w_qg"
                wmin = W[nmin, 0]
                grads[nmin, 0] = mm_tn(s["xn_mix"], dzm, f"{nm}_dwin", yb=wmin.shape[2])
                tokv = send_grads(f"mix_{i}", grads)
                grads = {}
                dh, dgpre = mm_nt_nbwd(dzm, wmin, s["h_a"], norm("mix_norm_pre", i) + tokv, dh, f"{nm}_bwd_in", tn=wmin.shape[2])
                gsmall["mix_norm_pre"][i] = dgpre
        if i == 1:
            dct = (dc4 + dcq4).transpose(0, 2, 1, 3).reshape(n_fox, T)
            dct = jnp.pad(dct, ((0, LANES - n_fox), (0, 0)))
            dflog, db = fox_cum_bwd(dct, kvf, fcol, b_row, "fox_cum_bwd")
            gsmall["fox_b_f"] = db[:, :n_fox]
            dkvf = jnp.concatenate([dk_sh, dv_sh, dflog], axis=1)
            dwk = mm_tn(xn_kv, dkvf, "fox_dwkvf", yb=LANES)
            dwk = dwk.transpose(1, 0, 2).reshape(D, -1)[:, :kvf_cols]
            tokv = send_grads("kvf", {("fox_w_kvf", 0): dwk.reshape(D, N_DEV, -1).transpose(1, 0, 2)})
            dh, dgkv = mm_nt_nbwd(dkvf, kvf_w, h_kv, _row(kv_norm) + tokv, dh, "fox_kvf_bwd", tn=LANES)
            gsmall["kv_norm"] = dgkv
    grad_x = dh.reshape(x.shape)

    def small_rows(n):
        g = gsmall[n]
        rows = g if isinstance(g, list) else [g]
        return [_pad_lanes(r, D) for r in rows]

    counts = {n: len(small_rows(n)) for n in small_names}
    packed = jnp.concatenate([r for n in small_names for r in small_rows(n)], axis=0)
    n_rows = packed.shape[0]
    packed = jnp.pad(packed, ((0, -n_rows % 8), (0, 0)))
    send("small", ["small"], [packed], [True])

    res = {}
    after = [dh]
    small_parts = None
    for gname, keys, hd in sent:
        lands = exchange_wait(hd, after, f"scatter_wait_{gname}")
        after = []
        for key, parts in zip(keys, lands):
            if key == "small":
                small_parts = parts
                continue
            n, l = key
            w = weights[n]
            as3 = (lambda a: a.reshape((1,) + a.shape)) if w.ndim == 2 else (lambda a: a)
            res[n] = adamw(parts, as3(w), as3(mom1[n]), as3(mom2[n]), l, res.get(n), f"adamw_{n}_{l}")
            after.append(res[n][1])

    def pack(d):
        rows = []
        for n in small_names:
            a = d[n]
            rows.append(_pad_lanes(a.reshape(-1, a.shape[-1]), D))
        a = jnp.concatenate(rows, axis=0)
        return jnp.pad(a, ((0, -n_rows % 8), (0, 0)))[None]

    sm = adamw(small_parts, pack(weights), pack(mom1), pack(mom2), 0, None, "adamw_small")
    off = 0
    for n in small_names:
        w = weights[n]
        res[n] = [a[0, off:off + counts[n], :w.shape[-1]].reshape(w.shape) for a in sm]
        off += counts[n]

    out = [loss, grad_x]
    for j in range(4):
        out += [res[n][j].reshape(weights[n].shape) for n in names]
    return tuple(out)
```

```python
import functools

import jax
import jax.numpy as jnp
from jax import lax
from jax.experimental import pallas as pl
from jax.experimental.pallas import tpu as pltpu

F32 = jnp.float32
BF = jnp.bfloat16
NORM_EPS = 1e-6
N_DEV = 8
HGRN_DK = 128
HGRN_CHUNK = 16
FOX_HEAD_DIM = 64
LANES = 128
ADAM_LR, ADAM_B1, ADAM_B2, ADAM_EPS, ADAM_WD, ADAM_STEP = 0.001, 0.9, 0.999, 1e-08, 0.01, 10
VMEM_LIMIT = 56 * 1024 * 1024
HI = lax.Precision.HIGHEST
NT = (((1,), (1,)), ((), ()))
TN = (((0,), (0,)), ((), ()))


def _params(n_axes):
    return pltpu.CompilerParams(dimension_semantics=("arbitrary",) * n_axes, vmem_limit_bytes=VMEM_LIMIT)


def _tile(n, want):
    t = min(n, want)
    while n % t:
        t //= 2
    return t


def _sigmoid(x):
    return 1.0 / (1.0 + jnp.exp(-x))


def _rms(x):
    r = lax.rsqrt(jnp.mean(x * x, axis=-1, keepdims=True) + NORM_EPS)
    return x * r, r


def _norm_bwd(dy, xhat, r, g):
    dxh = dy * g
    return r * (dxh - xhat * jnp.mean(dxh * xhat, axis=-1, keepdims=True))


def _colsum(x):
    return jnp.sum(x, axis=0, keepdims=True)


def _w_spec(w, blk):
    if w.ndim == 3:
        return lambda off: pl.BlockSpec((None, w.shape[1], w.shape[2]), lambda i, j: (j + off, 0, 0))
    return lambda off: pl.BlockSpec((w.shape[0], blk), lambda i, j: (0, j + off))


def norm_mm_swiglu(h, g, w3, name):
    T, D = h.shape
    nb, _, cs = w3.shape
    nh = nb // 2
    tm = _tile(T, 1024)

    def body(h_ref, g_ref, wg_ref, wu_ref, gate_ref, up_ref, a_ref, xn_ref):
        @pl.when(pl.program_id(1) == 0)
        def _():
            xh, _ = _rms(h_ref[...])
            xn_ref[...] = (xh * g_ref[...]).astype(BF)

        xn = xn_ref[...]
        gt = jnp.dot(xn, wg_ref[...], preferred_element_type=F32)
        up = jnp.dot(xn, wu_ref[...], preferred_element_type=F32)
        gate_ref[...] = gt.astype(BF)
        up_ref[...] = up.astype(BF)
        a_ref[...] = (gt * _sigmoid(gt) * up).astype(BF)

    ws = _w_spec(w3, cs)
    blk = pl.BlockSpec((None, tm, cs), lambda i, j: (j, i, 0))
    shp = jax.ShapeDtypeStruct((nh, T, cs), BF)
    return pl.pallas_call(
        body, name=name, grid=(T // tm, nh),
        in_specs=[pl.BlockSpec((tm, D), lambda i, j: (i, 0)), pl.BlockSpec((1, D), lambda i, j: (0, 0)), ws(0), ws(nh)],
        out_specs=[blk, blk, blk, pl.BlockSpec((tm, D), lambda i, j: (i, 0))],
        out_shape=[shp, shp, shp, jax.ShapeDtypeStruct((T, D), BF)], compiler_params=_params(2),
    )(h, g, w3, w3)


def norm_mm(h, g, w, name, tn=None):
    T, D = h.shape
    if w.ndim == 3:
        nb, cs = w.shape[0], w.shape[2]
    else:
        cs = tn
        nb = w.shape[1] // cs
    tm = _tile(T, 1024)

    def body(h_ref, g_ref, w_ref, z_ref, xn_ref):
        @pl.when(pl.program_id(1) == 0)
        def _():
            xh, _ = _rms(h_ref[...])
            xn_ref[...] = (xh * g_ref[...]).astype(BF)

        z_ref[...] = jnp.dot(xn_ref[...], w_ref[...], preferred_element_type=F32)

    return pl.pallas_call(
        body, name=name, grid=(T // tm, nb),
        in_specs=[pl.BlockSpec((tm, D), lambda i, j: (i, 0)), pl.BlockSpec((1, D), lambda i, j: (0, 0)),
                  _w_spec(w, cs)(0)],
        out_specs=[pl.BlockSpec((tm, cs), lambda i, j: (i, j)), pl.BlockSpec((tm, D), lambda i, j: (i, 0))],
        out_shape=[jax.ShapeDtypeStruct((T, nb * cs), F32), jax.ShapeDtypeStruct((T, D), BF)], compiler_params=_params(2),
    )(h, g, w)


def _x_spec(x, tm, kb):
    if x.ndim == 3:
        return pl.BlockSpec((None, tm, x.shape[2]), lambda i, j: (j, i, 0))
    return pl.BlockSpec((tm, kb), lambda i, j: (i, j))


def mm_norm_res(x, w2, g, h, coef, name, kb=None):
    T, D = h.shape
    if x.ndim == 3:
        nb, kb = x.shape[0], x.shape[2]
    else:
        nb = x.shape[1] // kb
    tm = _tile(T, 512)

    def body(x_ref, w_ref, h_ref, g_ref, hn_ref, y_ref, acc_ref):
        b = pl.program_id(1)

        @pl.when(b == 0)
        def _():
            acc_ref[...] = jnp.zeros_like(acc_ref)

        acc_ref[...] += jnp.dot(x_ref[...], w_ref[...], preferred_element_type=F32)

        @pl.when(b == nb - 1)
        def _():
            y = acc_ref[...]
            y_ref[...] = y
            yh, _ = _rms(y)
            hn_ref[...] = h_ref[...] + coef * (yh * g_ref[...])

    tok = pl.BlockSpec((tm, D), lambda i, j: (i, 0))
    shp = jax.ShapeDtypeStruct((T, D), F32)
    return pl.pallas_call(
        body, name=name, grid=(T // tm, nb),
        in_specs=[_x_spec(x, tm, kb), pl.BlockSpec((kb, D), lambda i, j: (j, 0)), tok,
                  pl.BlockSpec((1, D), lambda i, j: (0, 0))],
        out_specs=[tok, tok], out_shape=[shp, shp],
        scratch_shapes=[pltpu.VMEM((tm, D), F32)], compiler_params=_params(2),
    )(x, w2, h, g)


def nbwd_mm_nt(dout, y, g, w2, coef, kb, name, gate=None, up=None):
    T, D = dout.shape
    nb = w2.shape[0] // kb
    swiglu = gate is not None
    tm = _tile(T, 512)

    def body(*refs):
        if swiglu:
            dout_ref, y_ref, g_ref, w_ref, gate_ref, up_ref, dy_ref, dg_ref, da_ref, dys_ref = refs
        else:
            dout_ref, y_ref, g_ref, w_ref, dy_ref, dg_ref, da_ref, dys_ref = refs
        i, b = pl.program_id(0), pl.program_id(1)

        @pl.when((i == 0) & (b == 0))
        def _():
            dg_ref[...] = jnp.zeros_like(dg_ref)

        @pl.when(b == 0)
        def _():
            yh, r = _rms(y_ref[...])
            dyn = coef * dout_ref[...]
            dg_ref[...] += _colsum(dyn * yh)
            dy = _norm_bwd(dyn, yh, r, g_ref[...]).astype(BF)
            dys_ref[...] = dy
            dy_ref[...] = dy

        da = lax.dot_general(dys_ref[...], w_ref[...], NT, preferred_element_type=F32)
        if swiglu:
            gt = gate_ref[...].astype(F32)
            u = up_ref[...].astype(F32)
            sg = _sigmoid(gt)
            da_ref[0] = (da * u * (sg * (1.0 + gt * (1.0 - sg)))).astype(BF)
            da_ref[1] = (da * (gt * sg)).astype(BF)
        else:
            da_ref[...] = da.astype(BF)

    tok = pl.BlockSpec((tm, D), lambda i, j: (i, 0))
    vec = pl.BlockSpec((1, D), lambda i, j: (0, 0))
    in_specs = [tok, tok, vec, pl.BlockSpec((kb, D), lambda i, j: (j, 0))]
    args = [dout, y, g, w2]
    if swiglu:
        blk = pl.BlockSpec((None, tm, kb), lambda i, j: (j, i, 0))
        in_specs += [blk, blk]
        args += [gate, up]
        da_spec = pl.BlockSpec((2, None, tm, kb), lambda i, j: (0, j, i, 0))
        da_shape = jax.ShapeDtypeStruct((2, nb, T, kb), BF)
    else:
        da_spec = pl.BlockSpec((tm, kb), lambda i, j: (i, j))
        da_shape = jax.ShapeDtypeStruct((T, nb * kb), BF)
    return pl.pallas_call(
        body, name=name, grid=(T // tm, nb), in_specs=in_specs,
        out_specs=[tok, vec, da_spec],
        out_shape=[jax.ShapeDtypeStruct((T, D), BF), jax.ShapeDtypeStruct((1, D), F32), da_shape],
        scratch_shapes=[pltpu.VMEM((tm, D), BF)], compiler_params=_params(2),
    )(*args)


def mm_nt_nbwd(dz, w, h, g, dout, name, tn=None):
    T, D = h.shape
    if w.ndim == 3:
        nb, cs = w.shape[0], w.shape[2]
    else:
        cs = tn
        nb = w.shape[1] // cs
    tm = _tile(T, 512)

    def body(dz_ref, w_ref, h_ref, g_ref, dout_ref, dh_ref, dg_ref, acc_ref):
        i, b = pl.program_id(0), pl.program_id(1)

        @pl.when((i == 0) & (b == 0))
        def _():
            dg_ref[...] = jnp.zeros_like(dg_ref)

        @pl.when(b == 0)
        def _():
            acc_ref[...] = jnp.zeros_like(acc_ref)

        acc_ref[...] += lax.dot_general(dz_ref[...], w_ref[...], NT, preferred_element_type=F32)

        @pl.when(b == nb - 1)
        def _():
            xh, r = _rms(h_ref[...])
            gg = g_ref[...]
            dxn = acc_ref[...]
            dg_ref[...] += _colsum(dxn * xh)
            dh_ref[...] = dout_ref[...] + _norm_bwd(dxn, xh, r, gg)

    tok = pl.BlockSpec((tm, D), lambda i, j: (i, 0))
    vec = pl.BlockSpec((1, D), lambda i, j: (0, 0))
    return pl.pallas_call(
        body, name=name, grid=(T // tm, nb),
        in_specs=[_x_spec(dz, tm, cs), _w_spec(w, cs)(0), tok, vec, tok],
        out_specs=[tok, vec],
        out_shape=[jax.ShapeDtypeStruct((T, D), F32), jax.ShapeDtypeStruct((1, D), F32)],
        scratch_shapes=[pltpu.VMEM((tm, D), F32)], compiler_params=_params(2),
    )(dz, w, h, g, dout)


def mm_tn(x, y, name, xb=None, yb=None, x_layer=None):
    T = y.shape[-2]
    tt = _tile(T, 512)
    x_split = (x.ndim == 3 and x_layer is None) or xb is not None
    if x_layer is not None:
        xs = pl.BlockSpec((None, tt, x.shape[2]), lambda b, t: (x_layer, t, 0))
        kdim = x.shape[2]
    elif x.ndim == 3:
        xs = pl.BlockSpec((None, tt, x.shape[2]), lambda b, t: (b, t, 0))
        nb, kdim = x.shape[0], x.shape[2]
    elif xb is not None:
        xs = pl.BlockSpec((tt, xb), lambda b, t: (t, b))
        nb, kdim = x.shape[1] // xb, xb
    else:
        xs = pl.BlockSpec((tt, x.shape[1]), lambda b, t: (t, 0))
        kdim = x.shape[1]
    if x_split:
        ys = pl.BlockSpec((tt, y.shape[1]), lambda b, t: (t, 0))
        ndim = y.shape[1]
        out_spec = pl.BlockSpec((kdim, ndim), lambda b, t: (b, 0))
        out_shape = jax.ShapeDtypeStruct((nb * kdim, ndim), BF)
    else:
        if y.ndim == 3:
            ys = pl.BlockSpec((None, tt, y.shape[2]), lambda b, t: (b, t, 0))
            nb, ndim = y.shape[0], y.shape[2]
        else:
            ys = pl.BlockSpec((tt, yb), lambda b, t: (t, b))
            nb, ndim = y.shape[1] // yb, yb
        out_spec = pl.BlockSpec((None, kdim, ndim), lambda b, t: (b, 0, 0))
        out_shape = jax.ShapeDtypeStruct((nb, kdim, ndim), BF)
    nt = T // tt

    def body(x_ref, y_ref, o_ref, acc_ref):
        t = pl.program_id(1)

        @pl.when(t == 0)
        def _():
            acc_ref[...] = jnp.zeros_like(acc_ref)

        acc_ref[...] += lax.dot_general(x_ref[...].astype(BF), y_ref[...].astype(BF), TN, preferred_element_type=F32)

        @pl.when(t == nt - 1)
        def _():
            o_ref[...] = acc_ref[...].astype(BF)

    return pl.pallas_call(
        body, name=name, grid=(nb, nt), in_specs=[xs, ys], out_specs=out_spec, out_shape=out_shape,
        scratch_shapes=[pltpu.VMEM((kdim, ndim), F32)], compiler_params=_params(2),
    )(x, y)


def ple_fwd(h, gpre, wg, p3, wp, gpost, layer, name):
    T, D = h.shape
    pd = p3.shape[2]
    tm = _tile(T, 512)

    def body(h_ref, gpre_ref, wg_ref, p_ref, wp_ref, gpost_ref, hn_ref, gate_ref, pp_ref):
        x = h_ref[...]
        xh, _ = _rms(x)
        u = jnp.dot((xh * gpre_ref[...]).astype(BF), wg_ref[...], preferred_element_type=F32)
        gate = _sigmoid(u)
        pp = jnp.dot(p_ref[...].astype(BF), wp_ref[...], preferred_element_type=F32)
        yh, _ = _rms(gate * pp)
        hn_ref[...] = x + yh * gpost_ref[...]
        gate_ref[...] = gate.astype(BF)
        pp_ref[...] = pp.astype(BF)

    tok = pl.BlockSpec((tm, D), lambda i: (i, 0))
    vec = pl.BlockSpec((1, D), lambda i: (0, 0))
    return pl.pallas_call(
        body, name=name, grid=(T // tm,),
        in_specs=[tok, vec, pl.BlockSpec((D, D), lambda i: (0, 0)),
                  pl.BlockSpec((None, tm, pd), lambda i: (layer, i, 0)),
                  pl.BlockSpec((pd, D), lambda i: (0, 0)), vec],
        out_specs=[tok, tok, tok],
        out_shape=[jax.ShapeDtypeStruct((T, D), F32), jax.ShapeDtypeStruct((T, D), BF), jax.ShapeDtypeStruct((T, D), BF)],
        compiler_params=_params(1),
    )(h, gpre, wg, p3, wp, gpost)


def ple_bwd(dout, h, gate, pp, gpre, wg, gpost, name):
    T, D = h.shape
    tm = _tile(T, 512)

    def body(dout_ref, h_ref, gate_ref, pp_ref, gpre_ref, wg_ref, gpost_ref, dh_ref, du_ref, dpp_ref, xn_ref, dgpre_ref, dgpost_ref):
        @pl.when(pl.program_id(0) == 0)
        def _():
            dgpre_ref[...] = jnp.zeros_like(dgpre_ref)
            dgpost_ref[...] = jnp.zeros_like(dgpost_ref)

        dout = dout_ref[...]
        gate = gate_ref[...].astype(F32)
        pp = pp_ref[...].astype(F32)
        yh, ry = _rms(gate * pp)
        dgpost_ref[...] += _colsum(dout * yh)
        dy = _norm_bwd(dout, yh, ry, gpost_ref[...])
        dpp_ref[...] = (dy * gate).astype(BF)
        du = (dy * pp * gate * (1.0 - gate)).astype(BF)
        du_ref[...] = du
        dxn = lax.dot_general(du, wg_ref[...], NT, preferred_element_type=F32)
        xh, r = _rms(h_ref[...])
        gp = gpre_ref[...]
        dgpre_ref[...] += _colsum(dxn * xh)
        dh_ref[...] = dout + _norm_bwd(dxn, xh, r, gp)
        xn_ref[...] = (xh * gp).astype(BF)

    tok = pl.BlockSpec((tm, D), lambda i: (i, 0))
    vec = pl.BlockSpec((1, D), lambda i: (0, 0))
    bft = jax.ShapeDtypeStruct((T, D), BF)
    v32 = jax.ShapeDtypeStruct((1, D), F32)
    return pl.pallas_call(
        body, name=name, grid=(T // tm,),
        in_specs=[tok, tok, tok, tok, vec, pl.BlockSpec((D, D), lambda i: (0, 0)), vec],
        out_specs=[tok, tok, tok, tok, vec, vec],
        out_shape=[jax.ShapeDtypeStruct((T, D), F32), bft, bft, bft, v32, v32],
        compiler_params=_params(1),
    )(dout, h, gate, pp, gpre, wg, gpost)


def _chunk_tri(tb, upper):
    r = lax.broadcasted_iota(jnp.int32, (tb, tb), 0)
    c = lax.broadcasted_iota(jnp.int32, (tb, tb), 1)
    shift = HGRN_CHUNK.bit_length() - 1
    same = jnp.right_shift(r, shift) == jnp.right_shift(c, shift)
    return (same & ((c >= r) if upper else (c <= r))).astype(F32)


def _hgrn_gates(z, logits):
    lb = 1.0 / (1.0 + jnp.exp(logits[1:2, :] - logits[0:1, :]))
    e = jnp.exp(-jnp.abs(z))
    inv = 1.0 / (1.0 + e)
    sig = jnp.where(z >= 0, inv, e * inv)
    nsig = jnp.where(z >= 0, e * inv, inv)
    return lb, sig, nsig, lb + (1.0 - lb) * sig


def hgrn_fwd(z, lb_logits, out_norm, name):
    T = z.shape[0]
    W = z.shape[1] // 4
    H = W // HGRN_DK
    C = HGRN_CHUNK
    tb = _tile(T, 256)
    nch = tb // C

    def body(zq_ref, zf_ref, zv_ref, zg_ref, lbl_ref, on_ref, x_ref, o_ref, st_ref, s_scr, cum_scr, k_scr, o_scr):
        @pl.when(pl.program_id(1) == 0)
        def _():
            s_scr[...] = jnp.zeros_like(s_scr)

        lb, sig, nsig, f = _hgrn_gates(zf_ref[...], lbl_ref[...])
        cum_scr[...] = jnp.dot(_chunk_tri(tb, False), jnp.log(f), precision=HI, preferred_element_type=F32)
        k_scr[...] = (1.0 - lb) * nsig
        row = lax.broadcasted_iota(jnp.int32, (C, HGRN_DK), 0)

        def chunk(c, carry):
            r0 = pl.multiple_of(c * C, C)
            rows = pl.ds(r0, C)
            q, k, v, cu = zq_ref[rows, :], k_scr[rows, :], zv_ref[rows, :], cum_scr[rows, :]
            st = s_scr[...]
            st_ref[c] = st
            o = lax.dot_general((q * jnp.exp(cu)).astype(BF), st.astype(BF), NT, preferred_element_type=F32)
            qr = q.astype(BF).astype(F32)
            for s in range(C):
                one = pl.ds(r0 + s, 1)
                e = jnp.exp(jnp.minimum(cu - cum_scr[one, :], 0.0))
                col = jnp.sum(qr * (e * k_scr[one, :]).astype(BF).astype(F32), axis=-1, keepdims=True)
                col = jnp.where(row >= s, col, 0.0).astype(BF).astype(F32)
                o = o + col * zv_ref[one, :].astype(BF).astype(F32)
            o_scr[rows, :] = o
            last = cum_scr[pl.ds(r0 + C - 1, 1), :]
            kg = (k * jnp.exp(last - cu)).astype(BF)
            s_scr[...] = st * jnp.exp(last) + lax.dot_general(v.astype(BF), kg, TN, preferred_element_type=F32)
            return carry

        lax.fori_loop(0, nch, chunk, 0)
        o = o_scr[...]
        o_ref[...] = o
        oh, _ = _rms(o)
        g = zg_ref[...]
        x_ref[...] = (oh * on_ref[...] * (g * _sigmoid(g))).astype(BF)

    def zs(part):
        return pl.BlockSpec((tb, HGRN_DK), lambda hd, i: (i, part * H + hd))

    blk = pl.BlockSpec((tb, HGRN_DK), lambda hd, i: (i, hd))
    return pl.pallas_call(
        body, name=name, grid=(H, T // tb),
        in_specs=[zs(0), zs(1), zs(2), zs(3), pl.BlockSpec((2, HGRN_DK), lambda hd, i: (0, hd)),
                  pl.BlockSpec((1, HGRN_DK), lambda hd, i: (0, 0))],
        out_specs=[blk, blk, pl.BlockSpec((nch, None, HGRN_DK, HGRN_DK), lambda hd, i: (i, hd, 0, 0))],
        out_shape=[jax.ShapeDtypeStruct((T, W), BF), jax.ShapeDtypeStruct((T, W), F32),
                   jax.ShapeDtypeStruct((T // C, H, HGRN_DK, HGRN_DK), F32)],
        scratch_shapes=[pltpu.VMEM((HGRN_DK, HGRN_DK), F32), pltpu.VMEM((tb, HGRN_DK), F32),
                        pltpu.VMEM((tb, HGRN_DK), F32), pltpu.VMEM((tb, HGRN_DK), F32)],
        compiler_params=_params(2),
    )(z, z, z, z, lb_logits, out_norm)


def hgrn_bwd(dx, z, o, states, lb_logits, out_norm, name):
    T = z.shape[0]
    W = z.shape[1] // 4
    H = W // HGRN_DK
    C = HGRN_CHUNK
    tb = _tile(T, 256)
    nch = tb // C
    nblk = T // tb

    def body(dx_ref, zq_ref, zf_ref, zv_ref, zg_ref, o_ref, st_ref, lbl_ref, on_ref,
             dq_ref, df_ref, dv_ref, dg_ref, dl_ref, don_ref,
             ds_scr, cum_scr, k_scr, do_scr, dq_scr, dk_scr, dv_scr, dcum_scr):
        hd, i = pl.program_id(0), pl.program_id(1)

        @pl.when(i == 0)
        def _():
            ds_scr[...] = jnp.zeros_like(ds_scr)
            dl_ref[...] = jnp.zeros_like(dl_ref)

        @pl.when((i == 0) & (hd == 0))
        def _():
            don_ref[...] = jnp.zeros_like(don_ref)

        lb, sig, nsig, f = _hgrn_gates(zf_ref[...], lbl_ref[...])
        cum_scr[...] = jnp.dot(_chunk_tri(tb, False), jnp.log(f), precision=HI, preferred_element_type=F32)
        k_scr[...] = (1.0 - lb) * nsig
        oh, r = _rms(o_ref[...])
        w = on_ref[...]
        g = zg_ref[...]
        sg = _sigmoid(g)
        dxv = dx_ref[...].astype(F32)
        dg_ref[...] = (dxv * (oh * w) * (sg * (1.0 + g * (1.0 - sg)))).astype(BF)
        don = dxv * (g * sg)
        don_ref[...] += _colsum(don * oh)
        do_scr[...] = _norm_bwd(don, oh, r, w)
        row = lax.broadcasted_iota(jnp.int32, (C, HGRN_DK), 0)

        def chunk(cc, carry):
            c = nch - 1 - cc
            r0 = pl.multiple_of(c * C, C)
            rows = pl.ds(r0, C)
            q, k, v, cu, do = zq_ref[rows, :], k_scr[rows, :], zv_ref[rows, :], cum_scr[rows, :], do_scr[rows, :]
            st = st_ref[c]
            dst = ds_scr[...]
            last = cum_scr[pl.ds(r0 + C - 1, 1), :]
            lam, gam, elast = jnp.exp(cu), jnp.exp(last - cu), jnp.exp(last)
            dob, dstb = do.astype(BF), dst.astype(BF)
            dq = jnp.dot(dob, st.astype(BF), preferred_element_type=F32) * lam
            dv = lax.dot_general((k * gam).astype(BF), dstb, NT, preferred_element_type=F32)
            dk = jnp.dot(v.astype(BF), dstb, preferred_element_type=F32) * gam
            dlast = elast * _colsum(dst * st) + _colsum(dk * k)
            ds_scr[...] = dst * elast + lax.dot_general(dob, (q * lam).astype(BF), TN, preferred_element_type=F32)
            for s in range(C):
                one = pl.ds(r0 + s, 1)
                e = jnp.where(row >= s, jnp.exp(jnp.minimum(cu - cum_scr[one, :], 0.0)), 0.0)
                ks = k_scr[one, :]
                da = jnp.sum(do * zv_ref[one, :], axis=-1, keepdims=True)
                pq = q * e
                a = jnp.sum(pq * ks, axis=-1, keepdims=True)
                dq = dq + da * e * ks
                dk = jnp.where(row == s, dk + _colsum(da * pq), dk)
                dv = jnp.where(row == s, dv + _colsum(a * do), dv)
            dq_scr[rows, :] = dq
            dk_scr[rows, :] = dk
            dv_scr[rows, :] = dv
            dcum_scr[rows, :] = q * dq - k * dk + jnp.where(row == C - 1, dlast, 0.0)
            return carry

        lax.fori_loop(0, nch, chunk, 0)
        dlf = jnp.dot(_chunk_tri(tb, True), dcum_scr[...], precision=HI, preferred_element_type=F32)
        dk = dk_scr[...]
        common = (1.0 - lb) * sig * nsig
        df_ref[...] = (dlf * common / f - dk * common).astype(BF)
        dq_ref[...] = dq_scr[...].astype(BF)
        dv_ref[...] = dv_scr[...].astype(BF)
        dl0 = _colsum(dlf * nsig / f - dk * nsig) * lb * (1.0 - lb)
        dl_ref[...] += jnp.where(lax.broadcasted_iota(jnp.int32, (2, HGRN_DK), 0) == 0, dl0, -dl0)

    def zs(part):
        return pl.BlockSpec((tb, HGRN_DK), lambda hd, i: (nblk - 1 - i, part * H + hd))

    blk = pl.BlockSpec((tb, HGRN_DK), lambda hd, i: (nblk - 1 - i, hd))
    bft = jax.ShapeDtypeStruct((T, W), BF)
    scr = pltpu.VMEM((tb, HGRN_DK), F32)
    return pl.pallas_call(
        body, name=name, grid=(H, nblk),
        in_specs=[blk, zs(0), zs(1), zs(2), zs(3), blk,
                  pl.BlockSpec((nch, None, HGRN_DK, HGRN_DK), lambda hd, i: (nblk - 1 - i, hd, 0, 0)),
                  pl.BlockSpec((2, HGRN_DK), lambda hd, i: (0, hd)), pl.BlockSpec((1, HGRN_DK), lambda hd, i: (0, 0))],
        out_specs=[blk, blk, blk, blk, pl.BlockSpec((2, HGRN_DK), lambda hd, i: (0, hd)),
                   pl.BlockSpec((1, HGRN_DK), lambda hd, i: (0, 0))],
        out_shape=[bft, bft, bft, bft, jax.ShapeDtypeStruct((2, W), F32), jax.ShapeDtypeStruct((1, HGRN_DK), F32)],
        scratch_shapes=[pltpu.VMEM((HGRN_DK, HGRN_DK), F32), scr, scr, scr, scr, scr, scr, scr],
        compiler_params=_params(2),
    )(dx, z, z, z, z, o, states, lb_logits, out_norm)


def _log_sigmoid(x):
    return jnp.minimum(x, 0.0) - jnp.log(1.0 + jnp.exp(-jnp.abs(x)))


def _tri(n, upper):
    r = lax.broadcasted_iota(jnp.int32, (n, n), 0)
    c = lax.broadcasted_iota(jnp.int32, (n, n), 1)
    return ((r >= c) if upper else (c <= r)).astype(F32)


def fox_cum_fwd(kvf, fcol, b_row, name):
    T = kvf.shape[0]
    tb = _tile(T, 512)

    def body(x_ref, b_ref, ct_ref, carry_ref):
        @pl.when(pl.program_id(0) == 0)
        def _():
            carry_ref[...] = jnp.zeros_like(carry_ref)

        lf = _log_sigmoid(x_ref[...] + b_ref[...])
        cum = jnp.dot(_tri(tb, False), lf, precision=HI, preferred_element_type=F32) + carry_ref[...]
        carry_ref[...] += _colsum(lf)
        ct_ref[...] = cum.T

    return pl.pallas_call(
        body, name=name, grid=(T // tb,),
        in_specs=[pl.BlockSpec((tb, LANES), lambda i: (i, fcol)), pl.BlockSpec((1, LANES), lambda i: (0, 0))],
        out_specs=pl.BlockSpec((LANES, tb), lambda i: (0, i)), out_shape=jax.ShapeDtypeStruct((LANES, T), F32),
        scratch_shapes=[pltpu.VMEM((1, LANES), F32)], compiler_params=_params(1),
    )(kvf, b_row)


def fox_cum_bwd(dct, kvf, fcol, b_row, name):
    T = kvf.shape[0]
    tb = _tile(T, 512)
    nblk = T // tb

    def body(dc_ref, x_ref, b_ref, df_ref, db_ref, carry_ref):
        @pl.when(pl.program_id(0) == 0)
        def _():
            carry_ref[...] = jnp.zeros_like(carry_ref)
            db_ref[...] = jnp.zeros_like(db_ref)

        dc = dc_ref[...]
        dlf_t = jnp.dot(dc, _tri(tb, True), precision=HI, preferred_element_type=F32) + carry_ref[...]
        carry_ref[...] += jnp.sum(dc, axis=1, keepdims=True)
        x = x_ref[...] + b_ref[...]
        df = dlf_t.T * _sigmoid(-x)
        df_ref[...] = df.astype(BF)
        db_ref[...] += _colsum(df)

    return pl.pallas_call(
        body, name=name, grid=(nblk,),
        in_specs=[pl.BlockSpec((LANES, tb), lambda i: (0, nblk - 1 - i)),
                  pl.BlockSpec((tb, LANES), lambda i: (nblk - 1 - i, fcol)), pl.BlockSpec((1, LANES), lambda i: (0, 0))],
        out_specs=[pl.BlockSpec((tb, LANES), lambda i: (nblk - 1 - i, 0)), pl.BlockSpec((1, LANES), lambda i: (0, 0))],
        out_shape=[jax.ShapeDtypeStruct((T, LANES), BF), jax.ShapeDtypeStruct((1, LANES), F32)],
        scratch_shapes=[pltpu.VMEM((LANES, 1), F32)], compiler_params=_params(1),
    )(dct, kvf, b_row)


NAUG = 3


def fox_prep(kvf, ct, n_fox, tk):
    T = kvf.shape[0]
    W = n_fox * FOX_HEAD_DIM
    NP = n_fox // 2
    k = kvf[:, :W].astype(BF).reshape(T, NP, 2, FOX_HEAD_DIM)
    c = ct[:n_fox].T.reshape(T, NP, 2)
    hi = lax.reduce_precision(c, 8, 7)
    mid = lax.reduce_precision(c - hi, 8, 7)
    lo = c - hi - mid
    aug = jnp.stack([hi, mid, lo], axis=-1).astype(BF)
    pad = jnp.zeros((T, NP, FOX_HEAD_DIM - NAUG), BF)
    ka = jnp.concatenate([k[:, :, 0], aug[:, :, 0], pad], axis=-1).reshape(T, W)
    kb = jnp.concatenate([aug[:, :, 1], pad, k[:, :, 1]], axis=-1).reshape(T, W)
    v = kvf[:, W:2 * W].astype(BF)
    vt3 = v.reshape(T // tk, tk, NP, LANES).transpose(2, 0, 3, 1)
    return ka, kb, vt3


def fox_fwd(qg, ka, kb, vt3, name):
    T = qg.shape[0]
    W = qg.shape[1] // 2
    NP = W // LANES
    tq = tk = vt3.shape[3]
    scale = FOX_HEAD_DIM ** -0.5
    nk = T // tk
    HD = FOX_HEAD_DIM

    def body(q_ref, g_ref, ka_ref, kb_ref, vt_ref, x_ref, o_ref, lse_ref):
        i = pl.program_id(1)
        lane = lax.broadcasted_iota(jnp.int32, (tq, LANES), 1)
        q2 = q_ref[...] * scale
        qa = jnp.where(lane < HD, q2, jnp.where(lane < HD + NAUG, -1.0, 0.0))
        qb = jnp.where(lane >= HD, q2, jnp.where(lane < NAUG, -1.0, 0.0))
        qts = (qa.T.astype(BF), qb.T.astype(BF))
        krow = lax.broadcasted_iota(jnp.int32, (tk, tq), 0)
        qcol = lax.broadcasted_iota(jnp.int32, (tk, tq), 1)

        def step(j, carry, diag):
            rows = pl.ds(pl.multiple_of(j * tk, tk), tk)
            ks = (ka_ref[rows, :], kb_ref[rows, :])
            vt = vt_ref[j]
            sts = [jnp.dot(ks[a], qts[a], preferred_element_type=F32) for a in range(2)]
            pts, mls = [], []
            for a in range(2):
                m, l, _ = carry[a]
                st = sts[a]
                if diag:
                    st = jnp.where(krow + (j * tk - i * tq) <= qcol, st, -1e30)
                mn = jnp.maximum(m, jnp.max(st, axis=0, keepdims=True))
                alpha = jnp.exp(m - mn)
                pt = jnp.exp(st - mn)
                mls.append((mn, l * alpha + jnp.sum(pt, axis=0, keepdims=True), alpha))
                pts.append(pt.astype(BF))
            out = []
            for a in range(2):
                mn, l, alpha = mls[a]
                acc = carry[a][2] * alpha + jnp.dot(vt[a * HD:(a + 1) * HD, :], pts[a], preferred_element_type=F32)
                out.append((mn, l, acc))
            return tuple(out)

        init = (jnp.full((1, tq), -1e30, F32), jnp.zeros((1, tq), F32), jnp.zeros((HD, tq), F32))
        r = tq // tk
        carry = lax.fori_loop(0, i * r, lambda j, c: step(j, c, False), (init, init))
        for u in range(r):
            carry = step(i * r + u, carry, True)
        (ma, la, acca), (mb, lb, accb) = carry
        ot = jnp.concatenate([acca / la, accb / lb], axis=0)
        o = ot.T
        o_ref[...] = o
        lse_ref[0:1, :] = ma + jnp.log(la)
        lse_ref[1:2, :] = mb + jnp.log(lb)
        x_ref[...] = (o * _sigmoid(g_ref[...])).astype(BF)

    blk = pl.BlockSpec((tq, LANES), lambda hp, i: (i, hp))
    full = pl.BlockSpec((T, LANES), lambda hp, i: (0, hp))
    return pl.pallas_call(
        body, name=name, grid=(NP, T // tq),
        in_specs=[blk, pl.BlockSpec((tq, LANES), lambda hp, i: (i, NP + hp)), full, full,
                  pl.BlockSpec((None, nk, LANES, tk), lambda hp, i: (hp, 0, 0, 0))],
        out_specs=[blk, blk, pl.BlockSpec((None, None, 2, tq), lambda hp, i: (hp, i, 0, 0))],
        out_shape=[jax.ShapeDtypeStruct((T, W), BF), jax.ShapeDtypeStruct((T, W), F32),
                   jax.ShapeDtypeStruct((NP, T // tq, 2, tq), F32)],
        compiler_params=_params(2),
    )(qg, qg, ka, kb, vt3)


def fox_gate_bwd(dx, o, qg, name):
    T, W = o.shape
    tm = _tile(T, 512)

    def body(dx_ref, o_ref, g_ref, do_ref, dg_ref, ds_ref):
        dxv = dx_ref[...].astype(F32)
        sg = _sigmoid(g_ref[...])
        do = dxv * sg
        o = o_ref[...]
        do_ref[...] = do
        dg_ref[...] = (dxv * o * sg * (1.0 - sg)).astype(BF)
        head = jnp.right_shift(lax.broadcasted_iota(jnp.int32, (W, LANES), 0), FOX_HEAD_DIM.bit_length() - 1)
        sel = (head == lax.broadcasted_iota(jnp.int32, (W, LANES), 1)).astype(F32)
        ds_ref[...] = jnp.dot(do * o, sel, precision=HI, preferred_element_type=F32)

    tok = pl.BlockSpec((tm, W), lambda i: (i, 0))
    return pl.pallas_call(
        body, name=name, grid=(T // tm,),
        in_specs=[tok, tok, pl.BlockSpec((tm, W), lambda i: (i, 1))],
        out_specs=[tok, tok, pl.BlockSpec((tm, LANES), lambda i: (i, 0))],
        out_shape=[jax.ShapeDtypeStruct((T, W), F32), jax.ShapeDtypeStruct((T, W), BF), jax.ShapeDtypeStruct((T, LANES), F32)],
        compiler_params=_params(1),
    )(dx, o, qg)


def fox_bwd_prep(qg, kvf, do, tq):
    T = qg.shape[0]
    W = qg.shape[1] // 2
    NP = W // LANES
    scale = FOX_HEAD_DIM ** -0.5
    tr3 = lambda a: a.reshape(T // tq, tq, NP, LANES).transpose(2, 0, 3, 1)
    q = (qg[:, :W] * scale).astype(BF)
    dob = do.astype(BF)
    k = kvf[:, :W]
    return q, tr3(q), dob, tr3(dob), tr3((k * scale).astype(BF)), kvf[:, W:2 * W].astype(BF)


def fox_bwd(q, qt3, dob, dot3, kt3, v, ka, kb, lse4, dsum4, name):
    T, W = q.shape
    NP = W // LANES
    tq = tk = qt3.shape[3]
    nq = T // tq
    HD = FOX_HEAD_DIM

    def body(q_ref, qt_ref, do_ref, dot_ref, kt_ref, v_ref, ka_ref, kb_ref, lse_ref, dsum_ref,
             dqt_ref, dk_ref, dv_ref, dc_ref, dcq_ref, dk_scr, dv_scr, dcl_scr):
        j = pl.program_id(1)

        @pl.when(j == 0)
        def _():
            dqt_ref[...] = jnp.zeros_like(dqt_ref)
            dcq_ref[...] = jnp.zeros_like(dcq_ref)

        dk_scr[...] = jnp.zeros_like(dk_scr)
        dv_scr[...] = jnp.zeros_like(dv_scr)
        dcl_scr[...] = jnp.zeros_like(dcl_scr)
        lane = lax.broadcasted_iota(jnp.int32, (tk, LANES), 1)
        srow = lax.broadcasted_iota(jnp.int32, (LANES, tq), 0)
        lanes_of = (lane < HD, lane >= HD)
        rows_of = (srow < HD, srow >= HD)
        v2 = v_ref[...]
        kt2 = kt_ref[...]
        zero = jnp.zeros((), BF)
        vs = [jnp.where(lanes_of[a], v2, zero) for a in range(2)]
        kts = [jnp.where(rows_of[a], kt2, zero) for a in range(2)]
        kaug = (ka_ref[...], kb_ref[...])
        krow = lax.broadcasted_iota(jnp.int32, (tk, tq), 0)
        qcol = lax.broadcasted_iota(jnp.int32, (tk, tq), 1)
        neg1 = jnp.full((), -1.0, BF)

        def step(i, carry, diag):
            rows = pl.ds(pl.multiple_of(i * tq, tq), tq)
            qt2 = qt_ref[i]
            dot2 = dot_ref[i]
            q2 = q_ref[rows, :]
            do2 = do_ref[rows, :]
            qts = [jnp.where(srow < HD, qt2, jnp.where(srow < HD + NAUG, neg1, zero)),
                   jnp.where(srow >= HD, qt2, jnp.where(srow < NAUG, neg1, zero))]
            sts = [jnp.dot(kaug[a], qts[a], preferred_element_type=F32) for a in range(2)]
            dps = [jnp.dot(vs[a], dot2, preferred_element_type=F32) for a in range(2)]
            pbs, dsbs = [], []
            for a in range(2):
                pt = jnp.exp(sts[a] - lse_ref[i, a:a + 1, :])
                if diag:
                    pt = jnp.where(krow <= qcol, pt, 0.0)
                ds = pt * (dps[a] - dsum_ref[i, a:a + 1, :])
                dcq_ref[i, a:a + 1, :] += _colsum(ds)
                part = ds[:, 0:LANES]
                for u in range(1, tq // LANES):
                    part = part + ds[:, u * LANES:(u + 1) * LANES]
                dcl_scr[a] += part
                pbs.append(pt.astype(BF))
                dsbs.append(ds.astype(BF))
            qn = [jnp.where(lanes_of[a], q2, zero) for a in range(2)]
            don = [jnp.where(lanes_of[a], do2, zero) for a in range(2)]
            dv_scr[...] += (jnp.dot(pbs[0], don[0], preferred_element_type=F32) +
                            jnp.dot(pbs[1], don[1], preferred_element_type=F32))
            dk_scr[...] += (jnp.dot(dsbs[0], qn[0], preferred_element_type=F32) +
                            jnp.dot(dsbs[1], qn[1], preferred_element_type=F32))
            dqt_ref[i] += (jnp.dot(kts[0], dsbs[0], preferred_element_type=F32) +
                           jnp.dot(kts[1], dsbs[1], preferred_element_type=F32))
            return carry

        step(j, 0, True)
        lax.fori_loop(j + 1, nq, lambda i, c: step(i, c, False), 0)
        dk_ref[...] = dk_scr[...].astype(BF)
        dv_ref[...] = dv_scr[...].astype(BF)
        for a in range(2):
            dc_ref[a:a + 1, :] = -_colsum(dcl_scr[a].T)

    tile = pl.BlockSpec((tk, LANES), lambda hp, j: (j, hp))
    full = pl.BlockSpec((T, LANES), lambda hp, j: (0, hp))
    full3 = pl.BlockSpec((None, nq, LANES, tq), lambda hp, j: (hp, 0, 0, 0))
    rows4 = pl.BlockSpec((None, nq, 2, tq), lambda hp, j: (hp, 0, 0, 0))
    bft = jax.ShapeDtypeStruct((T, W), BF)
    r4 = jax.ShapeDtypeStruct((NP, nq, 2, tq), F32)
    return pl.pallas_call(
        body, name=name, grid=(NP, nq),
        in_specs=[full, full3, full, full3, pl.BlockSpec((None, None, LANES, tk), lambda hp, j: (hp, j, 0, 0)),
                  tile, tile, tile, rows4, rows4],
        out_specs=[full3, tile, tile, pl.BlockSpec((None, None, 2, tk), lambda hp, j: (hp, j, 0, 0)), rows4],
        out_shape=[jax.ShapeDtypeStruct((NP, nq, LANES, tq), F32), bft, bft, r4, r4],
        scratch_shapes=[pltpu.VMEM((tk, LANES), F32), pltpu.VMEM((tk, LANES), F32), pltpu.VMEM((2, tk, LANES), F32)],
        compiler_params=_params(2),
    )(q, qt3, dob, dot3, kt3, v, ka, kb, lse4, dsum4)


def loss_fwd_bwd(y, target, name):
    T, D = y.shape
    tm = _tile(T, 512)

    def body(y_ref, t_ref, dy_ref, l_ref):
        @pl.when(pl.program_id(0) == 0)
        def _():
            l_ref[...] = jnp.zeros_like(l_ref)

        d = y_ref[...] - t_ref[...]
        dy_ref[...] = d * (1.0 / D)
        l_ref[...] += 0.5 * jnp.sum(jnp.mean(d * d, axis=-1, keepdims=True), axis=0, keepdims=True)

    tok = pl.BlockSpec((tm, D), lambda i: (i, 0))
    return pl.pallas_call(
        body, name=name, grid=(T // tm,), in_specs=[tok, tok],
        out_specs=[tok, pl.BlockSpec((1, 1), lambda i: (0, 0))],
        out_shape=[jax.ShapeDtypeStruct((T, D), F32), jax.ShapeDtypeStruct((1, 1), F32)], compiler_params=_params(1),
    )(y, target)


def adamw(parts, w, m, v, layer, prev, name):
    L, R, C = w.shape
    tr = _tile(R, 256)
    c1 = 1.0 / (1.0 - ADAM_B1 ** ADAM_STEP)
    c2 = 1.0 / (1.0 - ADAM_B2 ** ADAM_STEP)

    def body(p_ref, w_ref, m_ref, v_ref, *rest):
        g_ref, d_ref, mo_ref, vo_ref = rest[-4:]
        g = p_ref[0].astype(F32)
        for d in range(1, N_DEV):
            g = g + p_ref[d].astype(F32)
        mn = ADAM_B1 * m_ref[...] + (1.0 - ADAM_B1) * g
        vn = ADAM_B2 * v_ref[...] + (1.0 - ADAM_B2) * (g * g)
        g_ref[...] = g
        mo_ref[...] = mn
        vo_ref[...] = vn
        d_ref[...] = -ADAM_LR * ((mn * c1) / (jnp.sqrt(vn * c2) + ADAM_EPS) + ADAM_WD * w_ref[...])

    blk = pl.BlockSpec((None, tr, C), lambda i: (layer, i, 0))
    shp = jax.ShapeDtypeStruct((L, R, C), F32)
    in_specs = [pl.BlockSpec((N_DEV, tr, C), lambda i: (0, i, 0)), blk, blk, blk]
    args = [parts, w, m, v]
    aliases = {}
    if prev is not None:
        in_specs += [pl.BlockSpec(memory_space=pl.ANY)] * 4
        args += list(prev)
        aliases = {4 + j: j for j in range(4)}
    return pl.pallas_call(
        body, name=name, grid=(R // tr,), in_specs=in_specs, out_specs=[blk, blk, blk, blk],
        out_shape=[shp, shp, shp, shp], input_output_aliases=aliases, compiler_params=_params(1),
    )(*args)


HBM_SPEC = pl.BlockSpec(memory_space=pltpu.HBM)
SEM_SPEC = pl.BlockSpec(memory_space=pltpu.SEMAPHORE)
EFFECT = pltpu.SideEffectType.DATAFLOW_SIDE_EFFECTING


def _mesh_pos():
    return lax.axis_index("x"), lax.axis_index("y"), lax.axis_index("c")


def _flip(v, bit):
    return v + bit - 2 * v * bit


def _peer(pos, delta):
    x, y, c = pos
    px, py, pc = _flip(x, (delta >> 2) & 1), _flip(y, (delta >> 1) & 1), _flip(c, delta & 1)
    return (px, py, pc), 4 * px + 2 * py + pc


def _me():
    x, y, c = _mesh_pos()
    return 4 * x + 2 * y + c


def _copies(src_refs, land_refs, whole, send, recv, incoming):
    pos = _mesh_pos()
    me = 4 * pos[0] + 2 * pos[1] + pos[2]
    out = []
    for k in range(len(src_refs)):
        for d in range(1, N_DEV):
            dev, idx = _peer(pos, d)
            j = k * (N_DEV - 1) + d - 1
            src = src_refs[k] if whole[k] else src_refs[k].at[idx]
            out.append(pltpu.make_async_remote_copy(
                src_ref=src, dst_ref=land_refs[k].at[idx if incoming else me], send_sem=send.at[j], recv_sem=recv.at[j],
                device_id=dev, device_id_type=pl.DeviceIdType.MESH))
    return out


def exchange_start(srcs, lands, whole, name):
    n = len(srcs)

    def body(*refs):
        for copy in _copies(refs[:n], refs[n:2 * n], whole, refs[2 * n], refs[2 * n + 1], False):
            copy.start()
        refs[-1][...] = jnp.zeros_like(refs[-1])

    sems = pltpu.SemaphoreType.DMA((n * (N_DEV - 1),))
    thru = [pltpu.HBM(a.shape, a.dtype) for a in list(srcs) + list(lands)]
    res = pl.pallas_call(
        body, name=name, in_specs=[HBM_SPEC] * (2 * n),
        out_specs=[SEM_SPEC, SEM_SPEC] + [HBM_SPEC] * (2 * n) + [pl.BlockSpec(memory_space=pltpu.VMEM)],
        out_shape=[sems, sems] + thru + [jax.ShapeDtypeStruct((8, LANES), F32)],
        input_output_aliases={j: 2 + j for j in range(2 * n)},
        compiler_params=pltpu.CompilerParams(has_side_effects=EFFECT),
    )(*[pltpu.with_memory_space_constraint(a, pltpu.HBM) for a in list(srcs) + list(lands)])
    return dict(send=res[0], recv=res[1], srcs=res[2:2 + n], lands=res[2 + n:2 + 2 * n], whole=whole, token=res[-1])


def exchange_wait(handle, after, name):
    n = len(handle["srcs"])
    whole = handle["whole"]

    def body(*refs):
        for copy in _copies(refs[:n], refs[n:2 * n], whole, refs[2 * n], refs[2 * n + 1], False):
            copy.wait_send()
        for copy in _copies(refs[:n], refs[n:2 * n], whole, refs[2 * n], refs[2 * n + 1], True):
            copy.wait_recv()

    bufs = list(handle["srcs"]) + list(handle["lands"])
    res = pl.pallas_call(
        body, name=name, in_specs=[HBM_SPEC] * (2 * n) + [SEM_SPEC, SEM_SPEC] + [pl.BlockSpec(memory_space=pl.ANY)] * len(after),
        out_specs=[HBM_SPEC] * (2 * n), out_shape=[pltpu.HBM(a.shape, a.dtype) for a in bufs],
        input_output_aliases={j: j for j in range(2 * n)},
        compiler_params=pltpu.CompilerParams(has_side_effects=EFFECT),
    )(*bufs, handle["send"], handle["recv"], *after)
    return list(res[n:])


def _landing(own, whole):
    me = _me()
    if not whole:
        own = lax.dynamic_index_in_dim(own, me, 0, keepdims=False)
    buf = lax.empty((N_DEV,) + own.shape, own.dtype)
    return lax.dynamic_update_slice(buf, own[None], (me,) + (0,) * own.ndim)


def _row(v):
    return v.reshape(1, -1)


def _pad_lanes(v, n):
    return jnp.pad(v, ((0, 0), (0, n - v.shape[1])))


GATHER_GROUPS = (
    ("ffn1_0", (("ffn1_w_in", 0), ("ffn1_w_out", 0))),
    ("hgrn", (("hgrn_w_in", 0), ("hgrn_w_out", 0))),
    ("rest_0", (("ffn2_w_in", 0), ("ffn2_w_out", 0), ("ple_w_gate", 0), ("ple_w_proj", 0), ("fox_w_kvf", 0))),
    ("layer_1", (("ffn1_w_in", 1), ("ffn1_w_out", 1), ("fox_w_qg", 0), ("fox_w_out", 0), ("ffn2_w_in", 1),
                 ("ffn2_w_out", 1), ("ple_w_gate", 1), ("ple_w_proj", 1))),
)


def kernel(x, p, ffn1_norm_pre, ffn1_w_in, ffn1_w_out, ffn1_norm_post, mix_norm_pre, mix_norm_post, ffn2_norm_pre, ffn2_w_in, ffn2_w_out, ffn2_norm_post, hgrn_w_in, hgrn_lb_logits, hgrn_out_norm, hgrn_w_out, kv_norm, fox_w_kvf, fox_b_f, fox_w_qg, fox_w_out, ple_norm_pre, ple_w_gate, ple_w_proj, ple_norm_post, loss_target, m_ffn1_norm_pre, m_ffn1_w_in, m_ffn1_w_out, m_ffn1_norm_post, m_mix_norm_pre, m_mix_norm_post, m_ffn2_norm_pre, m_ffn2_w_in, m_ffn2_w_out, m_ffn2_norm_post, m_hgrn_w_in, m_hgrn_lb_logits, m_hgrn_out_norm, m_hgrn_w_out, m_kv_norm, m_fox_w_kvf, m_fox_b_f, m_fox_w_qg, m_fox_w_out, m_ple_norm_pre, m_ple_w_gate, m_ple_w_proj, m_ple_norm_post, v_ffn1_norm_pre, v_ffn1_w_in, v_ffn1_w_out, v_ffn1_norm_post, v_mix_norm_pre, v_mix_norm_post, v_ffn2_norm_pre, v_ffn2_w_in, v_ffn2_w_out, v_ffn2_norm_post, v_hgrn_w_in, v_hgrn_lb_logits, v_hgrn_out_norm, v_hgrn_w_out, v_kv_norm, v_fox_w_kvf, v_fox_b_f, v_fox_w_qg, v_fox_w_out, v_ple_norm_pre, v_ple_w_gate, v_ple_w_proj, v_ple_norm_post):
    weights = dict(ffn1_norm_pre=ffn1_norm_pre, ffn1_w_in=ffn1_w_in, ffn1_w_out=ffn1_w_out, ffn1_norm_post=ffn1_norm_post, mix_norm_pre=mix_norm_pre, mix_norm_post=mix_norm_post, ffn2_norm_pre=ffn2_norm_pre, ffn2_w_in=ffn2_w_in, ffn2_w_out=ffn2_w_out, ffn2_norm_post=ffn2_norm_post, hgrn_w_in=hgrn_w_in, hgrn_lb_logits=hgrn_lb_logits, hgrn_out_norm=hgrn_out_norm, hgrn_w_out=hgrn_w_out, kv_norm=kv_norm, fox_w_kvf=fox_w_kvf, fox_b_f=fox_b_f, fox_w_qg=fox_w_qg, fox_w_out=fox_w_out, ple_norm_pre=ple_norm_pre, ple_w_gate=ple_w_gate, ple_w_proj=ple_w_proj, ple_norm_post=ple_norm_post)
    mom1 = dict(ffn1_norm_pre=m_ffn1_norm_pre, ffn1_w_in=m_ffn1_w_in, ffn1_w_out=m_ffn1_w_out, ffn1_norm_post=m_ffn1_norm_post, mix_norm_pre=m_mix_norm_pre, mix_norm_post=m_mix_norm_post, ffn2_norm_pre=m_ffn2_norm_pre, ffn2_w_in=m_ffn2_w_in, ffn2_w_out=m_ffn2_w_out, ffn2_norm_post=m_ffn2_norm_post, hgrn_w_in=m_hgrn_w_in, hgrn_lb_logits=m_hgrn_lb_logits, hgrn_out_norm=m_hgrn_out_norm, hgrn_w_out=m_hgrn_w_out, kv_norm=m_kv_norm, fox_w_kvf=m_fox_w_kvf, fox_b_f=m_fox_b_f, fox_w_qg=m_fox_w_qg, fox_w_out=m_fox_w_out, ple_norm_pre=m_ple_norm_pre, ple_w_gate=m_ple_w_gate, ple_w_proj=m_ple_w_proj, ple_norm_post=m_ple_norm_post)
    mom2 = dict(ffn1_norm_pre=v_ffn1_norm_pre, ffn1_w_in=v_ffn1_w_in, ffn1_w_out=v_ffn1_w_out, ffn1_norm_post=v_ffn1_norm_post, mix_norm_pre=v_mix_norm_pre, mix_norm_post=v_mix_norm_post, ffn2_norm_pre=v_ffn2_norm_pre, ffn2_w_in=v_ffn2_w_in, ffn2_w_out=v_ffn2_w_out, ffn2_norm_post=v_ffn2_norm_post, hgrn_w_in=v_hgrn_w_in, hgrn_lb_logits=v_hgrn_lb_logits, hgrn_out_norm=v_hgrn_out_norm, hgrn_w_out=v_hgrn_w_out, kv_norm=v_kv_norm, fox_w_kvf=v_fox_w_kvf, fox_b_f=v_fox_b_f, fox_w_qg=v_fox_w_qg, fox_w_out=v_fox_w_out, ple_norm_pre=v_ple_norm_pre, ple_w_gate=v_ple_w_gate, ple_w_proj=v_ple_w_proj, ple_norm_post=v_ple_norm_post)
    names = list(weights)
    big = ["ffn1_w_in", "ffn1_w_out", "ffn2_w_in", "ffn2_w_out", "hgrn_w_in", "hgrn_w_out", "fox_w_kvf", "fox_w_qg",
           "fox_w_out", "ple_w_gate", "ple_w_proj"]
    small_names = [n for n in names if n not in big]

    T, D = x.shape[1], x.shape[2]
    depth = p.shape[0]
    h0 = x.reshape(T, D)
    target = loss_target.reshape(T, D)
    p3 = p.reshape(depth, T, p.shape[3])
    n_fox = fox_b_f.shape[0]
    fox_w = n_fox * FOX_HEAD_DIM
    fcol = 2 * fox_w // LANES
    b_row = _pad_lanes(_row(fox_b_f), LANES)

    tok = jnp.zeros((), F32)
    handles = {}
    for gname, keys in GATHER_GROUPS:
        shards = []
        for n, l in keys:
            w = weights[n]
            shards.append(((w[l] if w.ndim == 3 else w) + tok).astype(BF))
        handles[gname] = exchange_start(shards, [_landing(s, True) for s in shards], [True] * len(keys), f"gather_start_{gname}")
        tok = handles[gname]["token"][0, 0]
    W = {}

    def arrive(gname, after):
        lands = exchange_wait(handles[gname], after, f"gather_wait_{gname}")
        W.update(dict(zip(dict(GATHER_GROUPS)[gname], lands)))

    def w_rows(n, l):
        return W[n, l].reshape(-1, D)

    norm = lambda name, i: weights[name][i:i + 1]

    saved = []
    h = h0
    kvf = ct = kvf_w = None
    tq = _tile(T, 512)
    for i in range(depth):
        s = {}
        if i == 0:
            arrive("ffn1_0", [handles["layer_1"]["token"]])
        for k in (1, 2):
            if k == 2:
                s["h_a"] = h
                if i == 0:
                    arrive("hgrn", [h])
                    z, xn = norm_mm(h, norm("mix_norm_pre", i), W["hgrn_w_in", 0], "hgrn_in")
                    xm, o, states = hgrn_fwd(z, hgrn_lb_logits, hgrn_out_norm, "hgrn_scan")
                    s.update(z=z, o=o, states=states)
                    wmix = w_rows("hgrn_w_out", 0)
                else:
                    qg, xn = norm_mm(h, norm("mix_norm_pre", i), W["fox_w_qg", 0], "fox_qg")
                    ka, kb, vt3 = fox_prep(kvf, ct, n_fox, tq)
                    xm, o, lse = fox_fwd(qg, ka, kb, vt3, "fox_attn")
                    s.update(qg=qg, o=o, lse=lse, ka=ka, kb=kb)
                    wmix = w_rows("fox_w_out", 0)
                h, ym = mm_norm_res(xm, wmix, norm("mix_norm_post", i), h, 1.0, f"mix_out_{i}", kb=_tile(xm.shape[1], 512))
                s.update(xm=xm, ym=ym, xn_mix=xn)
                if i == 0:
                    arrive("rest_0", [h])
            gate, up, a, xn = norm_mm_swiglu(h, norm(f"ffn{k}_norm_pre", i), W[f"ffn{k}_w_in", i], f"ffn{k}_in_{i}")
            hn, y = mm_norm_res(a, w_rows(f"ffn{k}_w_out", i), norm(f"ffn{k}_norm_post", i), h, 0.5, f"ffn{k}_out_{i}")
            s[f"ffn{k}"] = (h, gate, up, a, y, xn)
            h = hn
        s["h_c"] = h
        ple_proj = W["ple_w_proj", i].transpose(1, 0, 2).reshape(p.shape[3], D)
        h, pgate, pp = ple_fwd(h, norm("ple_norm_pre", i), w_rows("ple_w_gate", i), p3, ple_proj, norm("ple_norm_post", i),
                               i, f"ple_{i}")
        s.update(pgate=pgate, pp=pp)
        saved.append(s)
        if i == 0:
            kvf_nat = W["fox_w_kvf", 0].transpose(1, 0, 2).reshape(D, -1)
            kvf_cols = kvf_nat.shape[1]
            kvf_w = _pad_lanes(kvf_nat, 2 * fox_w + LANES)
            kvf, xn_kv = norm_mm(h, _row(kv_norm), kvf_w, "fox_kvf", tn=LANES)
            ct = fox_cum_fwd(kvf, fcol, b_row, "fox_cum")
            h_kv = h
            arrive("layer_1", [h])

    dh, loss_part = loss_fwd_bwd(h, target, "loss")
    loss = lax.psum(loss_part[0, 0], ("x", "y", "c"))

    gsmall = {n: [None] * weights[n].shape[0] if weights[n].ndim == 2 else None for n in small_names}
    sent = []

    def send(gname, keys, srcs, whole):
        lands = [_landing(a, w) for a, w in zip(srcs, whole)]
        hd = exchange_start(srcs, lands, whole, f"scatter_start_{gname}")
        sent.append((gname, keys, hd))
        return hd["token"][0:1, 0:1]

    def send_grads(gname, grads):
        return send(gname, list(grads), list(grads.values()), [False] * len(grads))

    for i in reversed(range(depth)):
        s = saved[i]
        grads = {}
        dh, du, dpp, xn, dgpre, dgpost = ple_bwd(dh, s["h_c"], s["pgate"], s["pp"], norm("ple_norm_pre", i),
                                                 w_rows("ple_w_gate", i), norm("ple_norm_post", i), f"ple_bwd_{i}")
        gsmall["ple_norm_pre"][i], gsmall["ple_norm_post"][i] = dgpre, dgpost
        grads["ple_w_gate", i] = mm_tn(xn, du, f"ple_dgate_{i}", xb=D // N_DEV).reshape(N_DEV, -1, D)
        grads["ple_w_proj", i] = mm_tn(p3, dpp, f"ple_dproj_{i}", x_layer=i, yb=D // N_DEV)
        for k in (2, 1):
            hin, gate, up, a, y, xn = s[f"ffn{k}"]
            ffn_cs = gate.shape[2]
            dy, dgpost, dz = nbwd_mm_nt(dh, y, norm(f"ffn{k}_norm_post", i), w_rows(f"ffn{k}_w_out", i), 0.5, ffn_cs,
                                        f"ffn{k}_bwd_out_{i}", gate=gate, up=up)
            dz = dz.reshape(-1, T, ffn_cs)
            grads[f"ffn{k}_w_out", i] = mm_tn(a, dy, f"ffn{k}_dwout_{i}").reshape(N_DEV, -1, D)
            grads[f"ffn{k}_w_in", i] = mm_tn(xn, dz, f"ffn{k}_dwin_{i}")
            tokv = send_grads(f"ffn{k}_{i}", grads)
            grads = {}
            dh, dgpre = mm_nt_nbwd(dz, W[f"ffn{k}_w_in", i], hin, norm(f"ffn{k}_norm_pre", i) + tokv, dh, f"ffn{k}_bwd_in_{i}")
            gsmall[f"ffn{k}_norm_pre"][i], gsmall[f"ffn{k}_norm_post"][i] = dgpre, dgpost
            if k == 2:
                nm = "hgrn" if i == 0 else "fox"
                wmix = w_rows(f"{nm}_w_out", 0)
                dy, dgpost, dxm = nbwd_mm_nt(dh, s["ym"], norm("mix_norm_post", i), wmix, 1.0, _tile(wmix.shape[0], 512),
                                             f"{nm}_bwd_out")
                grads[f"{nm}_w_out", 0] = mm_tn(s["xm"], dy, f"{nm}_dwout", xb=wmix.shape[0] // N_DEV).reshape(N_DEV, -1, D)
                gsmall["mix_norm_post"][i] = dgpost
                if i == 0:
                    dq, df, dv, dg, dlog, don = hgrn_bwd(dxm, s["z"], s["o"], s["states"], hgrn_lb_logits, hgrn_out_norm,
                                                         "hgrn_scan_bwd")
                    gsmall["hgrn_lb_logits"] = [dlog[0:1], dlog[1:2]]
                    gsmall["hgrn_out_norm"] = [don]
                    dzm = jnp.concatenate([dq, df, dv, dg], axis=1)
                    nmin = "hgrn_w_in"
                else:
                    do, dg, dsum = fox_gate_bwd(dxm, s["o"], s["qg"], "fox_gate_bwd")
                    dsum4 = dsum[:, :n_fox].T.reshape(n_fox // 2, 2, T // tq, tq).transpose(0, 2, 1, 3)
                    dqt, dk_sh, dv_sh, dc4, dcq4 = fox_bwd(*fox_bwd_prep(s["qg"], kvf, do, tq), s["ka"], s["kb"], s["lse"], dsum4,
                                                           "fox_attn_bwd")
                    dq = dqt.transpose(1, 3, 0, 2).reshape(T, fox_w)
                    dzm = jnp.concatenate([dq.astype(BF), dg], axis=1)
                    nmin = "fox_w_qg"
                wmin = W[nmin, 0]
                grads[nmin, 0] = mm_tn(s["xn_mix"], dzm, f"{nm}_dwin", yb=wmin.shape[2])
                tokv = send_grads(f"mix_{i}", grads)
                grads = {}
                dh, dgpre = mm_nt_nbwd(dzm, wmin, s["h_a"], norm("mix_norm_pre", i) + tokv, dh, f"{nm}_bwd_in", tn=wmin.shape[2])
                gsmall["mix_norm_pre"][i] = dgpre
        if i == 1:
            dct = (dc4 + dcq4).transpose(0, 2, 1, 3).reshape(n_fox, T)
            dct = jnp.pad(dct, ((0, LANES - n_fox), (0, 0)))
            dflog, db = fox_cum_bwd(dct, kvf, fcol, b_row, "fox_cum_bwd")
            gsmall["fox_b_f"] = db[:, :n_fox]
            dkvf = jnp.concatenate([dk_sh, dv_sh, dflog], axis=1)
            dwk = mm_tn(xn_kv, dkvf, "fox_dwkvf", yb=LANES)
            dwk = dwk.transpose(1, 0, 2).reshape(D, -1)[:, :kvf_cols]
            tokv = send_grads("kvf", {("fox_w_kvf", 0): dwk.reshape(D, N_DEV, -1).transpose(1, 0, 2)})
            dh, dgkv = mm_nt_nbwd(dkvf, kvf_w, h_kv, _row(kv_norm) + tokv, dh, "fox_kvf_bwd", tn=LANES)
            gsmall["kv_norm"] = dgkv
    grad_x = dh.reshape(x.shape)

    def small_rows(n):
        g = gsmall[n]
        rows = g if isinstance(g, list) else [g]
        return [_pad_lanes(r, D) for r in rows]

    counts = {n: len(small_rows(n)) for n in small_names}
    packed = jnp.concatenate([r for n in small_names for r in small_rows(n)], axis=0)
    n_rows = packed.shape[0]
    packed = jnp.pad(packed, ((0, -n_rows % 8), (0, 0)))
    send("small", ["small"], [packed], [True])

    res = {}
    after = [dh]
    small_parts = None
    for gname, keys, hd in sent:
        lands = exchange_wait(hd, after, f"scatter_wait_{gname}")
        after = []
        for key, parts in zip(keys, lands):
            if key == "small":
                small_parts = parts
                continue
            n, l = key
            w = weights[n]
            as3 = (lambda a: a.reshape((1,) + a.shape)) if w.ndim == 2 else (lambda a: a)
            res[n] = adamw(parts, as3(w), as3(mom1[n]), as3(mom2[n]), l, res.get(n), f"adamw_{n}_{l}")
            after.append(res[n][1])

    def pack(d):
        rows = []
        for n in small_names:
            a = d[n]
            rows.append(_pad_lanes(a.reshape(-1, a.shape[-1]), D))
        a = jnp.concatenate(rows, axis=0)
        return jnp.pad(a, ((0, -n_rows % 8), (0, 0)))[None]

    sm = adamw(small_parts, pack(weights), pack(mom1), pack(mom2), 0, None, "adamw_small")
    off = 0
    for n in small_names:
        w = weights[n]
        res[n] = [a[0, off:off + counts[n], :w.shape[-1]].reshape(w.shape) for a in sm]
        off += counts[n]

    out = [loss, grad_x]
    for j in range(4):
        out += [res[n][j].reshape(weights[n].shape) for n in names]
    return tuple(out)
```

```python
import functools

import jax
import jax.numpy as jnp
from jax import lax
from jax.experimental import pallas as pl
from jax.experimental.pallas import tpu as pltpu

F32 = jnp.float32
BF = jnp.bfloat16
NORM_EPS = 1e-6
N_DEV = 8
HGRN_DK = 128
HGRN_CHUNK = 16
HGRN_HEADS_PER_STEP = 4
FOX_HEAD_DIM = 64
LANES = 128
ADAM_LR, ADAM_B1, ADAM_B2, ADAM_EPS, ADAM_WD, ADAM_STEP = 0.001, 0.9, 0.999, 1e-08, 0.01, 10
VMEM_LIMIT = 56 * 1024 * 1024
HI = lax.Precision.HIGHEST
NT = (((1,), (1,)), ((), ()))
TN = (((0,), (0,)), ((), ()))


def _params(n_axes):
    return pltpu.CompilerParams(dimension_semantics=("arbitrary",) * n_axes, vmem_limit_bytes=VMEM_LIMIT)


def _tile(n, want):
    t = min(n, want)
    while n % t:
        t //= 2
    return t


def _sigmoid(x):
    return 1.0 / (1.0 + jnp.exp(-x))


def _rms(x):
    r = lax.rsqrt(jnp.mean(x * x, axis=-1, keepdims=True) + NORM_EPS)
    return x * r, r


def _norm_bwd(dy, xhat, r, g):
    dxh = dy * g
    return r * (dxh - xhat * jnp.mean(dxh * xhat, axis=-1, keepdims=True))


def _colsum(x):
    return jnp.sum(x, axis=0, keepdims=True)


def _w_spec(w, blk):
    if w.ndim == 3:
        return lambda off: pl.BlockSpec((None, w.shape[1], w.shape[2]), lambda i, j: (j + off, 0, 0))
    return lambda off: pl.BlockSpec((w.shape[0], blk), lambda i, j: (0, j + off))


def norm_mm_swiglu(h, g, w3, name):
    T, D = h.shape
    nb, _, cs = w3.shape
    nh = nb // 2
    tm = _tile(T, 1024)

    def body(h_ref, g_ref, wg_ref, wu_ref, gate_ref, up_ref, a_ref, xn_ref):
        @pl.when(pl.program_id(1) == 0)
        def _():
            xh, _ = _rms(h_ref[...])
            xn_ref[...] = (xh * g_ref[...]).astype(BF)

        xn = xn_ref[...]
        gt = jnp.dot(xn, wg_ref[...], preferred_element_type=F32)
        up = jnp.dot(xn, wu_ref[...], preferred_element_type=F32)
        gate_ref[...] = gt.astype(BF)
        up_ref[...] = up.astype(BF)
        a_ref[...] = (gt * _sigmoid(gt) * up).astype(BF)

    ws = _w_spec(w3, cs)
    blk = pl.BlockSpec((None, tm, cs), lambda i, j: (j, i, 0))
    shp = jax.ShapeDtypeStruct((nh, T, cs), BF)
    return pl.pallas_call(
        body, name=name, grid=(T // tm, nh),
        in_specs=[pl.BlockSpec((tm, D), lambda i, j: (i, 0)), pl.BlockSpec((1, D), lambda i, j: (0, 0)), ws(0), ws(nh)],
        out_specs=[blk, blk, blk, pl.BlockSpec((tm, D), lambda i, j: (i, 0))],
        out_shape=[shp, shp, shp, jax.ShapeDtypeStruct((T, D), BF)], compiler_params=_params(2),
    )(h, g, w3, w3)


def norm_mm(h, g, w, name, tn=None):
    T, D = h.shape
    if w.ndim == 3:
        nb, cs = w.shape[0], w.shape[2]
    else:
        cs = tn
        nb = w.shape[1] // cs
    tm = _tile(T, 1024)

    def body(h_ref, g_ref, w_ref, z_ref, xn_ref):
        @pl.when(pl.program_id(1) == 0)
        def _():
            xh, _ = _rms(h_ref[...])
            xn_ref[...] = (xh * g_ref[...]).astype(BF)

        z_ref[...] = jnp.dot(xn_ref[...], w_ref[...], preferred_element_type=F32)

    return pl.pallas_call(
        body, name=name, grid=(T // tm, nb),
        in_specs=[pl.BlockSpec((tm, D), lambda i, j: (i, 0)), pl.BlockSpec((1, D), lambda i, j: (0, 0)),
                  _w_spec(w, cs)(0)],
        out_specs=[pl.BlockSpec((tm, cs), lambda i, j: (i, j)), pl.BlockSpec((tm, D), lambda i, j: (i, 0))],
        out_shape=[jax.ShapeDtypeStruct((T, nb * cs), F32), jax.ShapeDtypeStruct((T, D), BF)], compiler_params=_params(2),
    )(h, g, w)


def _x_spec(x, tm, kb):
    if x.ndim == 3:
        return pl.BlockSpec((None, tm, x.shape[2]), lambda i, j: (j, i, 0))
    return pl.BlockSpec((tm, kb), lambda i, j: (i, j))


def mm_norm_res(x, w2, g, h, coef, name, kb=None):
    T, D = h.shape
    if x.ndim == 3:
        nb, kb = x.shape[0], x.shape[2]
    else:
        nb = x.shape[1] // kb
    tm = _tile(T, 512)

    def body(x_ref, w_ref, h_ref, g_ref, hn_ref, y_ref, acc_ref):
        b = pl.program_id(1)

        @pl.when(b == 0)
        def _():
            acc_ref[...] = jnp.zeros_like(acc_ref)

        acc_ref[...] += jnp.dot(x_ref[...], w_ref[...], preferred_element_type=F32)

        @pl.when(b == nb - 1)
        def _():
            y = acc_ref[...]
            y_ref[...] = y
            yh, _ = _rms(y)
            hn_ref[...] = h_ref[...] + coef * (yh * g_ref[...])

    tok = pl.BlockSpec((tm, D), lambda i, j: (i, 0))
    shp = jax.ShapeDtypeStruct((T, D), F32)
    return pl.pallas_call(
        body, name=name, grid=(T // tm, nb),
        in_specs=[_x_spec(x, tm, kb), pl.BlockSpec((kb, D), lambda i, j: (j, 0)), tok,
                  pl.BlockSpec((1, D), lambda i, j: (0, 0))],
        out_specs=[tok, tok], out_shape=[shp, shp],
        scratch_shapes=[pltpu.VMEM((tm, D), F32)], compiler_params=_params(2),
    )(x, w2, h, g)


def nbwd_mm_nt(dout, y, g, w2, coef, kb, name, gate=None, up=None):
    T, D = dout.shape
    nb = w2.shape[0] // kb
    swiglu = gate is not None
    tm = _tile(T, 512)

    def body(*refs):
        if swiglu:
            dout_ref, y_ref, g_ref, w_ref, gate_ref, up_ref, dy_ref, dg_ref, da_ref, dys_ref = refs
        else:
            dout_ref, y_ref, g_ref, w_ref, dy_ref, dg_ref, da_ref, dys_ref = refs
        i, b = pl.program_id(0), pl.program_id(1)

        @pl.when((i == 0) & (b == 0))
        def _():
            dg_ref[...] = jnp.zeros_like(dg_ref)

        @pl.when(b == 0)
        def _():
            yh, r = _rms(y_ref[...])
            dyn = coef * dout_ref[...]
            dg_ref[...] += _colsum(dyn * yh)
            dy = _norm_bwd(dyn, yh, r, g_ref[...]).astype(BF)
            dys_ref[...] = dy
            dy_ref[...] = dy

        da = lax.dot_general(dys_ref[...], w_ref[...], NT, preferred_element_type=F32)
        if swiglu:
            gt = gate_ref[...].astype(F32)
            u = up_ref[...].astype(F32)
            sg = _sigmoid(gt)
            da_ref[0] = (da * u * (sg * (1.0 + gt * (1.0 - sg)))).astype(BF)
            da_ref[1] = (da * (gt * sg)).astype(BF)
        else:
            da_ref[...] = da.astype(BF)

    tok = pl.BlockSpec((tm, D), lambda i, j: (i, 0))
    vec = pl.BlockSpec((1, D), lambda i, j: (0, 0))
    in_specs = [tok, tok, vec, pl.BlockSpec((kb, D), lambda i, j: (j, 0))]
    args = [dout, y, g, w2]
    if swiglu:
        blk = pl.BlockSpec((None, tm, kb), lambda i, j: (j, i, 0))
        in_specs += [blk, blk]
        args += [gate, up]
        da_spec = pl.BlockSpec((2, None, tm, kb), lambda i, j: (0, j, i, 0))
        da_shape = jax.ShapeDtypeStruct((2, nb, T, kb), BF)
    else:
        da_spec = pl.BlockSpec((tm, kb), lambda i, j: (i, j))
        da_shape = jax.ShapeDtypeStruct((T, nb * kb), BF)
    return pl.pallas_call(
        body, name=name, grid=(T // tm, nb), in_specs=in_specs,
        out_specs=[tok, vec, da_spec],
        out_shape=[jax.ShapeDtypeStruct((T, D), BF), jax.ShapeDtypeStruct((1, D), F32), da_shape],
        scratch_shapes=[pltpu.VMEM((tm, D), BF)], compiler_params=_params(2),
    )(*args)


def mm_nt_nbwd(dz, w, h, g, dout, name, tn=None):
    T, D = h.shape
    if w.ndim == 3:
        nb, cs = w.shape[0], w.shape[2]
    else:
        cs = tn
        nb = w.shape[1] // cs
    tm = _tile(T, 512)

    def body(dz_ref, w_ref, h_ref, g_ref, dout_ref, dh_ref, dg_ref, acc_ref):
        i, b = pl.program_id(0), pl.program_id(1)

        @pl.when((i == 0) & (b == 0))
        def _():
            dg_ref[...] = jnp.zeros_like(dg_ref)

        @pl.when(b == 0)
        def _():
            acc_ref[...] = jnp.zeros_like(acc_ref)

        acc_ref[...] += lax.dot_general(dz_ref[...], w_ref[...], NT, preferred_element_type=F32)

        @pl.when(b == nb - 1)
        def _():
            xh, r = _rms(h_ref[...])
            gg = g_ref[...]
            dxn = acc_ref[...]
            dg_ref[...] += _colsum(dxn * xh)
            dh_ref[...] = dout_ref[...] + _norm_bwd(dxn, xh, r, gg)

    tok = pl.BlockSpec((tm, D), lambda i, j: (i, 0))
    vec = pl.BlockSpec((1, D), lambda i, j: (0, 0))
    return pl.pallas_call(
        body, name=name, grid=(T // tm, nb),
        in_specs=[_x_spec(dz, tm, cs), _w_spec(w, cs)(0), tok, vec, tok],
        out_specs=[tok, vec],
        out_shape=[jax.ShapeDtypeStruct((T, D), F32), jax.ShapeDtypeStruct((1, D), F32)],
        scratch_shapes=[pltpu.VMEM((tm, D), F32)], compiler_params=_params(2),
    )(dz, w, h, g, dout)


def mm_tn(x, y, name, xb=None, yb=None, x_layer=None):
    T = y.shape[-2]
    tt = _tile(T, 512)
    x_split = (x.ndim == 3 and x_layer is None) or xb is not None
    if x_layer is not None:
        xs = pl.BlockSpec((None, tt, x.shape[2]), lambda b, t: (x_layer, t, 0))
        kdim = x.shape[2]
    elif x.ndim == 3:
        xs = pl.BlockSpec((None, tt, x.shape[2]), lambda b, t: (b, t, 0))
        nb, kdim = x.shape[0], x.shape[2]
    elif xb is not None:
        xs = pl.BlockSpec((tt, xb), lambda b, t: (t, b))
        nb, kdim = x.shape[1] // xb, xb
    else:
        xs = pl.BlockSpec((tt, x.shape[1]), lambda b, t: (t, 0))
        kdim = x.shape[1]
    if x_split:
        ys = pl.BlockSpec((tt, y.shape[1]), lambda b, t: (t, 0))
        ndim = y.shape[1]
        out_spec = pl.BlockSpec((kdim, ndim), lambda b, t: (b, 0))
        out_shape = jax.ShapeDtypeStruct((nb * kdim, ndim), BF)
    else:
        if y.ndim == 3:
            ys = pl.BlockSpec((None, tt, y.shape[2]), lambda b, t: (b, t, 0))
            nb, ndim = y.shape[0], y.shape[2]
        else:
            ys = pl.BlockSpec((tt, yb), lambda b, t: (t, b))
            nb, ndim = y.shape[1] // yb, yb
        out_spec = pl.BlockSpec((None, kdim, ndim), lambda b, t: (b, 0, 0))
        out_shape = jax.ShapeDtypeStruct((nb, kdim, ndim), BF)
    nt = T // tt

    def body(x_ref, y_ref, o_ref, acc_ref):
        t = pl.program_id(1)

        @pl.when(t == 0)
        def _():
            acc_ref[...] = jnp.zeros_like(acc_ref)

        acc_ref[...] += lax.dot_general(x_ref[...].astype(BF), y_ref[...].astype(BF), TN, preferred_element_type=F32)

        @pl.when(t == nt - 1)
        def _():
            o_ref[...] = acc_ref[...].astype(BF)

    return pl.pallas_call(
        body, name=name, grid=(nb, nt), in_specs=[xs, ys], out_specs=out_spec, out_shape=out_shape,
        scratch_shapes=[pltpu.VMEM((kdim, ndim), F32)], compiler_params=_params(2),
    )(x, y)


def ple_fwd(h, gpre, wg, p3, wp, gpost, layer, name):
    T, D = h.shape
    pd = p3.shape[2]
    tm = _tile(T, 512)

    def body(h_ref, gpre_ref, wg_ref, p_ref, wp_ref, gpost_ref, hn_ref, gate_ref, pp_ref):
        x = h_ref[...]
        xh, _ = _rms(x)
        u = jnp.dot((xh * gpre_ref[...]).astype(BF), wg_ref[...], preferred_element_type=F32)
        gate = _sigmoid(u)
        pp = jnp.dot(p_ref[...].astype(BF), wp_ref[...], preferred_element_type=F32)
        yh, _ = _rms(gate * pp)
        hn_ref[...] = x + yh * gpost_ref[...]
        gate_ref[...] = gate.astype(BF)
        pp_ref[...] = pp.astype(BF)

    tok = pl.BlockSpec((tm, D), lambda i: (i, 0))
    vec = pl.BlockSpec((1, D), lambda i: (0, 0))
    return pl.pallas_call(
        body, name=name, grid=(T // tm,),
        in_specs=[tok, vec, pl.BlockSpec((D, D), lambda i: (0, 0)),
                  pl.BlockSpec((None, tm, pd), lambda i: (layer, i, 0)),
                  pl.BlockSpec((pd, D), lambda i: (0, 0)), vec],
        out_specs=[tok, tok, tok],
        out_shape=[jax.ShapeDtypeStruct((T, D), F32), jax.ShapeDtypeStruct((T, D), BF), jax.ShapeDtypeStruct((T, D), BF)],
        compiler_params=_params(1),
    )(h, gpre, wg, p3, wp, gpost)


def ple_bwd(dout, h, gate, pp, gpre, wg, gpost, name):
    T, D = h.shape
    tm = _tile(T, 512)

    def body(dout_ref, h_ref, gate_ref, pp_ref, gpre_ref, wg_ref, gpost_ref, dh_ref, du_ref, dpp_ref, xn_ref, dgpre_ref, dgpost_ref):
        @pl.when(pl.program_id(0) == 0)
        def _():
            dgpre_ref[...] = jnp.zeros_like(dgpre_ref)
            dgpost_ref[...] = jnp.zeros_like(dgpost_ref)

        dout = dout_ref[...]
        gate = gate_ref[...].astype(F32)
        pp = pp_ref[...].astype(F32)
        yh, ry = _rms(gate * pp)
        dgpost_ref[...] += _colsum(dout * yh)
        dy = _norm_bwd(dout, yh, ry, gpost_ref[...])
        dpp_ref[...] = (dy * gate).astype(BF)
        du = (dy * pp * gate * (1.0 - gate)).astype(BF)
        du_ref[...] = du
        dxn = lax.dot_general(du, wg_ref[...], NT, preferred_element_type=F32)
        xh, r = _rms(h_ref[...])
        gp = gpre_ref[...]
        dgpre_ref[...] += _colsum(dxn * xh)
        dh_ref[...] = dout + _norm_bwd(dxn, xh, r, gp)
        xn_ref[...] = (xh * gp).astype(BF)

    tok = pl.BlockSpec((tm, D), lambda i: (i, 0))
    vec = pl.BlockSpec((1, D), lambda i: (0, 0))
    bft = jax.ShapeDtypeStruct((T, D), BF)
    v32 = jax.ShapeDtypeStruct((1, D), F32)
    return pl.pallas_call(
        body, name=name, grid=(T // tm,),
        in_specs=[tok, tok, tok, tok, vec, pl.BlockSpec((D, D), lambda i: (0, 0)), vec],
        out_specs=[tok, tok, tok, tok, vec, vec],
        out_shape=[jax.ShapeDtypeStruct((T, D), F32), bft, bft, bft, v32, v32],
        compiler_params=_params(1),
    )(dout, h, gate, pp, gpre, wg, gpost)


def _chunk_tri(tb, upper):
    r = lax.broadcasted_iota(jnp.int32, (tb, tb), 0)
    c = lax.broadcasted_iota(jnp.int32, (tb, tb), 1)
    shift = HGRN_CHUNK.bit_length() - 1
    same = jnp.right_shift(r, shift) == jnp.right_shift(c, shift)
    return (same & ((c >= r) if upper else (c <= r))).astype(F32)


def _hgrn_gates(z, logits):
    lb = 1.0 / (1.0 + jnp.exp(logits[1:2, :] - logits[0:1, :]))
    e = jnp.exp(-jnp.abs(z))
    inv = 1.0 / (1.0 + e)
    sig = jnp.where(z >= 0, inv, e * inv)
    nsig = jnp.where(z >= 0, e * inv, inv)
    return lb, sig, nsig, lb + (1.0 - lb) * sig


def hgrn_fwd(z, lb_logits, out_norm, name):
    T = z.shape[0]
    W = z.shape[1] // 4
    H = W // HGRN_DK
    C = HGRN_CHUNK
    HB = _tile(H, HGRN_HEADS_PER_STEP)
    tb = _tile(T, 256)
    nch = tb // C

    def body(zq_ref, zf_ref, zv_ref, zg_ref, lbl_ref, on_ref, x_ref, o_ref, st_ref, s_scr, cum_scr, k_scr, v_scr, o_scr):
        @pl.when(pl.program_id(1) == 0)
        def _():
            s_scr[...] = jnp.zeros_like(s_scr)

        lb, sig, nsig, f = _hgrn_gates(zf_ref[...], lbl_ref[...])
        cum = jnp.dot(_chunk_tri(tb, False), jnp.log(f), precision=HI, preferred_element_type=F32)
        kk = (1.0 - lb) * nsig
        for hh in range(HB):
            cols = slice(hh * HGRN_DK, (hh + 1) * HGRN_DK)
            cum_scr[hh] = cum[:, cols]
            k_scr[hh] = kk[:, cols]
            v_scr[hh] = zv_ref[:, cols]
        row = lax.broadcasted_iota(jnp.int32, (C, HGRN_DK), 0)

        def chunk(c, carry):
            r0 = pl.multiple_of(c * C, C)
            rows = pl.ds(r0, C)
            last_row = pl.ds(r0 + C - 1, 1)
            heads = []
            for hh in range(HB):
                cols = slice(hh * HGRN_DK, (hh + 1) * HGRN_DK)
                q, cu = zq_ref[rows, cols], cum_scr[hh, rows, :]
                st = s_scr[hh]
                st_ref[c, hh] = st
                o = lax.dot_general((q * jnp.exp(cu)).astype(BF), st.astype(BF), NT, preferred_element_type=F32)
                last = cum_scr[hh, last_row, :]
                kg = (k_scr[hh, rows, :] * jnp.exp(last - cu)).astype(BF)
                s_scr[hh] = st * jnp.exp(last) + lax.dot_general(v_scr[hh, rows, :].astype(BF), kg, TN, preferred_element_type=F32)
                heads.append((hh, q, cu, o))
            for hh, q, cu, o in heads:
                qr = q.astype(BF).astype(F32)
                for s in range(C):
                    one = pl.ds(r0 + s, 1)
                    e = jnp.exp(jnp.minimum(cu - cum_scr[hh, one, :], 0.0))
                    col = jnp.sum(qr * (e * k_scr[hh, one, :]).astype(BF).astype(F32), axis=-1, keepdims=True)
                    col = jnp.where(row >= s, col, 0.0).astype(BF).astype(F32)
                    o = o + col * v_scr[hh, one, :].astype(BF).astype(F32)
                o_scr[hh, rows, :] = o
            return carry

        lax.fori_loop(0, nch, chunk, 0)
        for hh in range(HB):
            cols = slice(hh * HGRN_DK, (hh + 1) * HGRN_DK)
            o = o_scr[hh]
            o_ref[:, cols] = o
            oh, _ = _rms(o)
            g = zg_ref[:, cols]
            x_ref[:, cols] = (oh * on_ref[...] * (g * _sigmoid(g))).astype(BF)

    def zs(part):
        return pl.BlockSpec((tb, HB * HGRN_DK), lambda hd, i: (i, part * (H // HB) + hd))

    blk = pl.BlockSpec((tb, HB * HGRN_DK), lambda hd, i: (i, hd))
    wide = pltpu.VMEM((HB, tb, HGRN_DK), F32)
    return pl.pallas_call(
        body, name=name, grid=(H // HB, T // tb),
        in_specs=[zs(0), zs(1), zs(2), zs(3), pl.BlockSpec((2, HB * HGRN_DK), lambda hd, i: (0, hd)),
                  pl.BlockSpec((1, HGRN_DK), lambda hd, i: (0, 0))],
        out_specs=[blk, blk, pl.BlockSpec((nch, HB, HGRN_DK, HGRN_DK), lambda hd, i: (i, hd, 0, 0))],
        out_shape=[jax.ShapeDtypeStruct((T, W), BF), jax.ShapeDtypeStruct((T, W), F32),
                   jax.ShapeDtypeStruct((T // C, H, HGRN_DK, HGRN_DK), F32)],
        scratch_shapes=[pltpu.VMEM((HB, HGRN_DK, HGRN_DK), F32), wide, wide, wide, wide],
        compiler_params=_params(2),
    )(z, z, z, z, lb_logits, out_norm)


def hgrn_bwd(dx, z, o, states, lb_logits, out_norm, name):
    T = z.shape[0]
    W = z.shape[1] // 4
    H = W // HGRN_DK
    C = HGRN_CHUNK
    HB = _tile(H, HGRN_HEADS_PER_STEP)
    tb = _tile(T, 256)
    nch = tb // C
    nblk = T // tb

    def body(dx_ref, zq_ref, zf_ref, zv_ref, zg_ref, o_ref, st_ref, lbl_ref, on_ref,
             dq_ref, df_ref, dv_ref, dg_ref, dl_ref, don_ref,
             ds_scr, cum_scr, k_scr, v_scr, q_scr, do_scr, dq_scr, dk_scr, dv_scr, dcum_scr):
        hd, i = pl.program_id(0), pl.program_id(1)

        @pl.when(i == 0)
        def _():
            ds_scr[...] = jnp.zeros_like(ds_scr)
            dl_ref[...] = jnp.zeros_like(dl_ref)

        @pl.when((i == 0) & (hd == 0))
        def _():
            don_ref[...] = jnp.zeros_like(don_ref)

        lb, sig, nsig, f = _hgrn_gates(zf_ref[...], lbl_ref[...])
        cum = jnp.dot(_chunk_tri(tb, False), jnp.log(f), precision=HI, preferred_element_type=F32)
        kk = (1.0 - lb) * nsig
        w = on_ref[...]
        for hh in range(HB):
            cols = slice(hh * HGRN_DK, (hh + 1) * HGRN_DK)
            cum_scr[hh] = cum[:, cols]
            k_scr[hh] = kk[:, cols]
            v_scr[hh] = zv_ref[:, cols]
            q_scr[hh] = zq_ref[:, cols]
            oh, r = _rms(o_ref[:, cols])
            g = zg_ref[:, cols]
            sg = _sigmoid(g)
            dxv = dx_ref[:, cols].astype(F32)
            dg_ref[:, cols] = (dxv * (oh * w) * (sg * (1.0 + g * (1.0 - sg)))).astype(BF)
            don = dxv * (g * sg)
            don_ref[...] += _colsum(don * oh)
            do_scr[hh] = _norm_bwd(don, oh, r, w)
        row = lax.broadcasted_iota(jnp.int32, (C, HGRN_DK), 0)

        def chunk(cc, carry):
            c = nch - 1 - cc
            r0 = pl.multiple_of(c * C, C)
            rows = pl.ds(r0, C)
            last_row = pl.ds(r0 + C - 1, 1)
            heads = []
            for hh in range(HB):
                q, k, v, cu, do = q_scr[hh, rows, :], k_scr[hh, rows, :], v_scr[hh, rows, :], cum_scr[hh, rows, :], do_scr[hh, rows, :]
                st = st_ref[c, hh]
                dst = ds_scr[hh]
                last = cum_scr[hh, last_row, :]
                lam, gam, elast = jnp.exp(cu), jnp.exp(last - cu), jnp.exp(last)
                dob, dstb = do.astype(BF), dst.astype(BF)
                dq = jnp.dot(dob, st.astype(BF), preferred_element_type=F32) * lam
                dv = lax.dot_general((k * gam).astype(BF), dstb, NT, preferred_element_type=F32)
                dk = jnp.dot(v.astype(BF), dstb, preferred_element_type=F32) * gam
                dlast = elast * _colsum(dst * st) + _colsum(dk * k)
                ds_scr[hh] = dst * elast + lax.dot_general(dob, (q * lam).astype(BF), TN, preferred_element_type=F32)
                heads.append((hh, q, k, cu, do, dq, dk, dv, dlast))
            for hh, q, k, cu, do, dq, dk, dv, dlast in heads:
                for s in range(C):
                    one = pl.ds(r0 + s, 1)
                    e = jnp.where(row >= s, jnp.exp(jnp.minimum(cu - cum_scr[hh, one, :], 0.0)), 0.0)
                    ks = k_scr[hh, one, :]
                    da = jnp.sum(do * v_scr[hh, one, :], axis=-1, keepdims=True)
                    pq = q * e
                    a = jnp.sum(pq * ks, axis=-1, keepdims=True)
                    dq = dq + da * e * ks
                    dk = jnp.where(row == s, dk + _colsum(da * pq), dk)
                    dv = jnp.where(row == s, dv + _colsum(a * do), dv)
                dq_scr[hh, rows, :] = dq
                dk_scr[hh, rows, :] = dk
                dv_scr[hh, rows, :] = dv
                dcum_scr[hh, rows, :] = q * dq - k * dk + jnp.where(row == C - 1, dlast, 0.0)
            return carry

        lax.fori_loop(0, nch, chunk, 0)
        tri = _chunk_tri(tb, True)
        for hh in range(HB):
            cols = slice(hh * HGRN_DK, (hh + 1) * HGRN_DK)
            dlf = jnp.dot(tri, dcum_scr[hh], precision=HI, preferred_element_type=F32)
            dk = dk_scr[hh]
            lbh, sigh, nsigh, fh = lb[:, cols], sig[:, cols], nsig[:, cols], f[:, cols]
            common = (1.0 - lbh) * sigh * nsigh
            df_ref[:, cols] = (dlf * common / fh - dk * common).astype(BF)
            dq_ref[:, cols] = dq_scr[hh].astype(BF)
            dv_ref[:, cols] = dv_scr[hh].astype(BF)
            dl0 = _colsum(dlf * nsigh / fh - dk * nsigh) * lbh * (1.0 - lbh)
            dl_ref[:, cols] += jnp.where(lax.broadcasted_iota(jnp.int32, (2, HGRN_DK), 0) == 0, dl0, -dl0)

    def zs(part):
        return pl.BlockSpec((tb, HB * HGRN_DK), lambda hd, i: (nblk - 1 - i, part * (H // HB) + hd))

    blk = pl.BlockSpec((tb, HB * HGRN_DK), lambda hd, i: (nblk - 1 - i, hd))
    bft = jax.ShapeDtypeStruct((T, W), BF)
    scr = pltpu.VMEM((HB, tb, HGRN_DK), F32)
    return pl.pallas_call(
        body, name=name, grid=(H // HB, nblk),
        in_specs=[blk, zs(0), zs(1), zs(2), zs(3), blk,
                  pl.BlockSpec((nch, HB, HGRN_DK, HGRN_DK), lambda hd, i: (nblk - 1 - i, hd, 0, 0)),
                  pl.BlockSpec((2, HB * HGRN_DK), lambda hd, i: (0, hd)), pl.BlockSpec((1, HGRN_DK), lambda hd, i: (0, 0))],
        out_specs=[blk, blk, blk, blk, pl.BlockSpec((2, HB * HGRN_DK), lambda hd, i: (0, hd)),
                   pl.BlockSpec((1, HGRN_DK), lambda hd, i: (0, 0))],
        out_shape=[bft, bft, bft, bft, jax.ShapeDtypeStruct((2, W), F32), jax.ShapeDtypeStruct((1, HGRN_DK), F32)],
        scratch_shapes=[pltpu.VMEM((HB, HGRN_DK, HGRN_DK), F32), scr, scr, scr, scr, scr, scr, scr, scr, scr],
        compiler_params=_params(2),
    )(dx, z, z, z, z, o, states, lb_logits, out_norm)


def _log_sigmoid(x):
    return jnp.minimum(x, 0.0) - jnp.log(1.0 + jnp.exp(-jnp.abs(x)))


def _tri(n, upper):
    r = lax.broadcasted_iota(jnp.int32, (n, n), 0)
    c = lax.broadcasted_iota(jnp.int32, (n, n), 1)
    return ((r >= c) if upper else (c <= r)).astype(F32)


def fox_cum_fwd(kvf, fcol, b_row, name):
    T = kvf.shape[0]
    tb = _tile(T, 512)

    def body(x_ref, b_ref, ct_ref, carry_ref):
        @pl.when(pl.program_id(0) == 0)
        def _():
            carry_ref[...] = jnp.zeros_like(carry_ref)

        lf = _log_sigmoid(x_ref[...] + b_ref[...])
        cum = jnp.dot(_tri(tb, False), lf, precision=HI, preferred_element_type=F32) + carry_ref[...]
        carry_ref[...] += _colsum(lf)
        ct_ref[...] = cum.T

    return pl.pallas_call(
        body, name=name, grid=(T // tb,),
        in_specs=[pl.BlockSpec((tb, LANES), lambda i: (i, fcol)), pl.BlockSpec((1, LANES), lambda i: (0, 0))],
        out_specs=pl.BlockSpec((LANES, tb), lambda i: (0, i)), out_shape=jax.ShapeDtypeStruct((LANES, T), F32),
        scratch_shapes=[pltpu.VMEM((1, LANES), F32)], compiler_params=_params(1),
    )(kvf, b_row)


def fox_cum_bwd(dct, kvf, fcol, b_row, name):
    T = kvf.shape[0]
    tb = _tile(T, 512)
    nblk = T // tb

    def body(dc_ref, x_ref, b_ref, df_ref, db_ref, carry_ref):
        @pl.when(pl.program_id(0) == 0)
        def _():
            carry_ref[...] = jnp.zeros_like(carry_ref)
            db_ref[...] = jnp.zeros_like(db_ref)

        dc = dc_ref[...]
        dlf_t = jnp.dot(dc, _tri(tb, True), precision=HI, preferred_element_type=F32) + carry_ref[...]
        carry_ref[...] += jnp.sum(dc, axis=1, keepdims=True)
        x = x_ref[...] + b_ref[...]
        df = dlf_t.T * _sigmoid(-x)
        df_ref[...] = df.astype(BF)
        db_ref[...] += _colsum(df)

    return pl.pallas_call(
        body, name=name, grid=(nblk,),
        in_specs=[pl.BlockSpec((LANES, tb), lambda i: (0, nblk - 1 - i)),
                  pl.BlockSpec((tb, LANES), lambda i: (nblk - 1 - i, fcol)), pl.BlockSpec((1, LANES), lambda i: (0, 0))],
        out_specs=[pl.BlockSpec((tb, LANES), lambda i: (nblk - 1 - i, 0)), pl.BlockSpec((1, LANES), lambda i: (0, 0))],
        out_shape=[jax.ShapeDtypeStruct((T, LANES), BF), jax.ShapeDtypeStruct((1, LANES), F32)],
        scratch_shapes=[pltpu.VMEM((LANES, 1), F32)], compiler_params=_params(1),
    )(dct, kvf, b_row)


NAUG = 3


def fox_prep(kvf, ct, n_fox, tk):
    T = kvf.shape[0]
    W = n_fox * FOX_HEAD_DIM
    NP = n_fox // 2
    k = kvf[:, :W].astype(BF).reshape(T, NP, 2, FOX_HEAD_DIM)
    c = ct[:n_fox].T.reshape(T, NP, 2)
    hi = lax.reduce_precision(c, 8, 7)
    mid = lax.reduce_precision(c - hi, 8, 7)
    lo = c - hi - mid
    aug = jnp.stack([hi, mid, lo], axis=-1).astype(BF)
    pad = jnp.zeros((T, NP, FOX_HEAD_DIM - NAUG), BF)
    ka = jnp.concatenate([k[:, :, 0], aug[:, :, 0], pad], axis=-1).reshape(T, W)
    kb = jnp.concatenate([aug[:, :, 1], pad, k[:, :, 1]], axis=-1).reshape(T, W)
    v = kvf[:, W:2 * W].astype(BF)
    vt3 = v.reshape(T // tk, tk, NP, LANES).transpose(2, 0, 3, 1)
    return ka, kb, vt3


def fox_fwd(qg, ka, kb, vt3, name):
    T = qg.shape[0]
    W = qg.shape[1] // 2
    NP = W // LANES
    tq = tk = vt3.shape[3]
    scale = FOX_HEAD_DIM ** -0.5
    nk = T // tk
    HD = FOX_HEAD_DIM

    def body(q_ref, g_ref, ka_ref, kb_ref, vt_ref, x_ref, o_ref, lse_ref):
        i = pl.program_id(1)
        lane = lax.broadcasted_iota(jnp.int32, (tq, LANES), 1)
        q2 = q_ref[...] * scale
        qa = jnp.where(lane < HD, q2, jnp.where(lane < HD + NAUG, -1.0, 0.0))
        qb = jnp.where(lane >= HD, q2, jnp.where(lane < NAUG, -1.0, 0.0))
        qts = (qa.T.astype(BF), qb.T.astype(BF))
        krow = lax.broadcasted_iota(jnp.int32, (tk, tq), 0)
        qcol = lax.broadcasted_iota(jnp.int32, (tk, tq), 1)

        def step(j, carry, diag):
            rows = pl.ds(pl.multiple_of(j * tk, tk), tk)
            ks = (ka_ref[rows, :], kb_ref[rows, :])
            vt = vt_ref[j]
            sts = [jnp.dot(ks[a], qts[a], preferred_element_type=F32) for a in range(2)]
            pts, mls = [], []
            for a in range(2):
                m, l, _ = carry[a]
                st = sts[a]
                if diag:
                    st = jnp.where(krow + (j * tk - i * tq) <= qcol, st, -1e30)
                mn = jnp.maximum(m, jnp.max(st, axis=0, keepdims=True))
                alpha = jnp.exp(m - mn)
                pt = jnp.exp(st - mn)
                mls.append((mn, l * alpha + jnp.sum(pt, axis=0, keepdims=True), alpha))
                pts.append(pt.astype(BF))
            out = []
            for a in range(2):
                mn, l, alpha = mls[a]
                acc = carry[a][2] * alpha + jnp.dot(vt[a * HD:(a + 1) * HD, :], pts[a], preferred_element_type=F32)
                out.append((mn, l, acc))
            return tuple(out)

        init = (jnp.full((1, tq), -1e30, F32), jnp.zeros((1, tq), F32), jnp.zeros((HD, tq), F32))
        r = tq // tk
        carry = lax.fori_loop(0, i * r, lambda j, c: step(j, c, False), (init, init))
        for u in range(r):
            carry = step(i * r + u, carry, True)
        (ma, la, acca), (mb, lb, accb) = carry
        ot = jnp.concatenate([acca / la, accb / lb], axis=0)
        o = ot.T
        o_ref[...] = o
        lse_ref[0:1, :] = ma + jnp.log(la)
        lse_ref[1:2, :] = mb + jnp.log(lb)
        x_ref[...] = (o * _sigmoid(g_ref[...])).astype(BF)

    blk = pl.BlockSpec((tq, LANES), lambda hp, i: (i, hp))
    full = pl.BlockSpec((T, LANES), lambda hp, i: (0, hp))
    return pl.pallas_call(
        body, name=name, grid=(NP, T // tq),
        in_specs=[blk, pl.BlockSpec((tq, LANES), lambda hp, i: (i, NP + hp)), full, full,
                  pl.BlockSpec((None, nk, LANES, tk), lambda hp, i: (hp, 0, 0, 0))],
        out_specs=[blk, blk, pl.BlockSpec((None, None, 2, tq), lambda hp, i: (hp, i, 0, 0))],
        out_shape=[jax.ShapeDtypeStruct((T, W), BF), jax.ShapeDtypeStruct((T, W), F32),
                   jax.ShapeDtypeStruct((NP, T // tq, 2, tq), F32)],
        compiler_params=_params(2),
    )(qg, qg, ka, kb, vt3)


def fox_gate_bwd(dx, o, qg, name):
    T, W = o.shape
    tm = _tile(T, 512)

    def body(dx_ref, o_ref, g_ref, do_ref, dg_ref, ds_ref):
        dxv = dx_ref[...].astype(F32)
        sg = _sigmoid(g_ref[...])
        do = dxv * sg
        o = o_ref[...]
        do_ref[...] = do
        dg_ref[...] = (dxv * o * sg * (1.0 - sg)).astype(BF)
        head = jnp.right_shift(lax.broadcasted_iota(jnp.int32, (W, LANES), 0), FOX_HEAD_DIM.bit_length() - 1)
        sel = (head == lax.broadcasted_iota(jnp.int32, (W, LANES), 1)).astype(F32)
        ds_ref[...] = jnp.dot(do * o, sel, precision=HI, preferred_element_type=F32)

    tok = pl.BlockSpec((tm, W), lambda i: (i, 0))
    return pl.pallas_call(
        body, name=name, grid=(T // tm,),
        in_specs=[tok, tok, pl.BlockSpec((tm, W), lambda i: (i, 1))],
        out_specs=[tok, tok, pl.BlockSpec((tm, LANES), lambda i: (i, 0))],
        out_shape=[jax.ShapeDtypeStruct((T, W), F32), jax.ShapeDtypeStruct((T, W), BF), jax.ShapeDtypeStruct((T, LANES), F32)],
        compiler_params=_params(1),
    )(dx, o, qg)


def fox_bwd_prep(qg, kvf, do, tq):
    T = qg.shape[0]
    W = qg.shape[1] // 2
    NP = W // LANES
    scale = FOX_HEAD_DIM ** -0.5
    tr3 = lambda a: a.reshape(T // tq, tq, NP, LANES).transpose(2, 0, 3, 1)
    q = (qg[:, :W] * scale).astype(BF)
    dob = do.astype(BF)
    k = kvf[:, :W]
    return q, tr3(q), dob, tr3(dob), tr3((k * scale).astype(BF)), kvf[:, W:2 * W].astype(BF)


def fox_bwd(q, qt3, dob, dot3, kt3, v, ka, kb, lse4, dsum4, name):
    T, W = q.shape
    NP = W // LANES
    tq = tk = qt3.shape[3]
    nq = T // tq
    HD = FOX_HEAD_DIM

    def body(q_ref, qt_ref, do_ref, dot_ref, kt_ref, v_ref, ka_ref, kb_ref, lse_ref, dsum_ref,
             dqt_ref, dk_ref, dv_ref, dc_ref, dcq_ref, dk_scr, dv_scr, dcl_scr):
        j = pl.program_id(1)

        @pl.when(j == 0)
        def _():
            dqt_ref[...] = jnp.zeros_like(dqt_ref)
            dcq_ref[...] = jnp.zeros_like(dcq_ref)

        dk_scr[...] = jnp.zeros_like(dk_scr)
        dv_scr[...] = jnp.zeros_like(dv_scr)
        dcl_scr[...] = jnp.zeros_like(dcl_scr)
        lane = lax.broadcasted_iota(jnp.int32, (tk, LANES), 1)
        srow = lax.broadcasted_iota(jnp.int32, (LANES, tq), 0)
        lanes_of = (lane < HD, lane >= HD)
        rows_of = (srow < HD, srow >= HD)
        v2 = v_ref[...]
        kt2 = kt_ref[...]
        zero = jnp.zeros((), BF)
        vs = [jnp.where(lanes_of[a], v2, zero) for a in range(2)]
        kts = [jnp.where(rows_of[a], kt2, zero) for a in range(2)]
        kaug = (ka_ref[...], kb_ref[...])
        krow = lax.broadcasted_iota(jnp.int32, (tk, tq), 0)
        qcol = lax.broadcasted_iota(jnp.int32, (tk, tq), 1)
        neg1 = jnp.full((), -1.0, BF)

        def step(i, carry, diag):
            rows = pl.ds(pl.multiple_of(i * tq, tq), tq)
            qt2 = qt_ref[i]
            dot2 = dot_ref[i]
            q2 = q_ref[rows, :]
            do2 = do_ref[rows, :]
            qts = [jnp.where(srow < HD, qt2, jnp.where(srow < HD + NAUG, neg1, zero)),
                   jnp.where(srow >= HD, qt2, jnp.where(srow < NAUG, neg1, zero))]
            sts = [jnp.dot(kaug[a], qts[a], preferred_element_type=F32) for a in range(2)]
            dps = [jnp.dot(vs[a], dot2, preferred_element_type=F32) for a in range(2)]
            pbs, dsbs = [], []
            for a in range(2):
                pt = jnp.exp(sts[a] - lse_ref[i, a:a + 1, :])
                if diag:
                    pt = jnp.where(krow <= qcol, pt, 0.0)
                ds = pt * (dps[a] - dsum_ref[i, a:a + 1, :])
                dcq_ref[i, a:a + 1, :] += _colsum(ds)
                part = ds[:, 0:LANES]
                for u in range(1, tq // LANES):
                    part = part + ds[:, u * LANES:(u + 1) * LANES]
                dcl_scr[a] += part
                pbs.append(pt.astype(BF))
                dsbs.append(ds.astype(BF))
            qn = [jnp.where(lanes_of[a], q2, zero) for a in range(2)]
            don = [jnp.where(lanes_of[a], do2, zero) for a in range(2)]
            dv_scr[...] += (jnp.dot(pbs[0], don[0], preferred_element_type=F32) +
                            jnp.dot(pbs[1], don[1], preferred_element_type=F32))
            dk_scr[...] += (jnp.dot(dsbs[0], qn[0], preferred_element_type=F32) +
                            jnp.dot(dsbs[1], qn[1], preferred_element_type=F32))
            dqt_ref[i] += (jnp.dot(kts[0], dsbs[0], preferred_element_type=F32) +
                           jnp.dot(kts[1], dsbs[1], preferred_element_type=F32))
            return carry

        step(j, 0, True)
        lax.fori_loop(j + 1, nq, lambda i, c: step(i, c, False), 0)
        dk_ref[...] = dk_scr[...].astype(BF)
        dv_ref[...] = dv_scr[...].astype(BF)
        for a in range(2):
            dc_ref[a:a + 1, :] = -_colsum(dcl_scr[a].T)

    tile = pl.BlockSpec((tk, LANES), lambda hp, j: (j, hp))
    full = pl.BlockSpec((T, LANES), lambda hp, j: (0, hp))
    full3 = pl.BlockSpec((None, nq, LANES, tq), lambda hp, j: (hp, 0, 0, 0))
    rows4 = pl.BlockSpec((None, nq, 2, tq), lambda hp, j: (hp, 0, 0, 0))
    bft = jax.ShapeDtypeStruct((T, W), BF)
    r4 = jax.ShapeDtypeStruct((NP, nq, 2, tq), F32)
    return pl.pallas_call(
        body, name=name, grid=(NP, nq),
        in_specs=[full, full3, full, full3, pl.BlockSpec((None, None, LANES, tk), lambda hp, j: (hp, j, 0, 0)),
                  tile, tile, tile, rows4, rows4],
        out_specs=[full3, tile, tile, pl.BlockSpec((None, None, 2, tk), lambda hp, j: (hp, j, 0, 0)), rows4],
        out_shape=[jax.ShapeDtypeStruct((NP, nq, LANES, tq), F32), bft, bft, r4, r4],
        scratch_shapes=[pltpu.VMEM((tk, LANES), F32), pltpu.VMEM((tk, LANES), F32), pltpu.VMEM((2, tk, LANES), F32)],
        compiler_params=_params(2),
    )(q, qt3, dob, dot3, kt3, v, ka, kb, lse4, dsum4)


def loss_fwd_bwd(y, target, name):
    T, D = y.shape
    tm = _tile(T, 512)

    def body(y_ref, t_ref, dy_ref, l_ref):
        @pl.when(pl.program_id(0) == 0)
        def _():
            l_ref[...] = jnp.zeros_like(l_ref)

        d = y_ref[...] - t_ref[...]
        dy_ref[...] = d * (1.0 / D)
        l_ref[...] += 0.5 * jnp.sum(jnp.mean(d * d, axis=-1, keepdims=True), axis=0, keepdims=True)

    tok = pl.BlockSpec((tm, D), lambda i: (i, 0))
    return pl.pallas_call(
        body, name=name, grid=(T // tm,), in_specs=[tok, tok],
        out_specs=[tok, pl.BlockSpec((1, 1), lambda i: (0, 0))],
        out_shape=[jax.ShapeDtypeStruct((T, D), F32), jax.ShapeDtypeStruct((1, 1), F32)], compiler_params=_params(1),
    )(y, target)


def adamw(parts, w, m, v, layer, prev, name):
    L, R, C = w.shape
    tr = _tile(R, 256)
    c1 = 1.0 / (1.0 - ADAM_B1 ** ADAM_STEP)
    c2 = 1.0 / (1.0 - ADAM_B2 ** ADAM_STEP)

    def body(p_ref, w_ref, m_ref, v_ref, *rest):
        g_ref, d_ref, mo_ref, vo_ref = rest[-4:]
        g = p_ref[0].astype(F32)
        for d in range(1, N_DEV):
            g = g + p_ref[d].astype(F32)
        mn = ADAM_B1 * m_ref[...] + (1.0 - ADAM_B1) * g
        vn = ADAM_B2 * v_ref[...] + (1.0 - ADAM_B2) * (g * g)
        g_ref[...] = g
        mo_ref[...] = mn
        vo_ref[...] = vn
        d_ref[...] = -ADAM_LR * ((mn * c1) / (jnp.sqrt(vn * c2) + ADAM_EPS) + ADAM_WD * w_ref[...])

    blk = pl.BlockSpec((None, tr, C), lambda i: (layer, i, 0))
    shp = jax.ShapeDtypeStruct((L, R, C), F32)
    in_specs = [pl.BlockSpec((N_DEV, tr, C), lambda i: (0, i, 0)), blk, blk, blk]
    args = [parts, w, m, v]
    aliases = {}
    if prev is not None:
        in_specs += [pl.BlockSpec(memory_space=pl.ANY)] * 4
        args += list(prev)
        aliases = {4 + j: j for j in range(4)}
    return pl.pallas_call(
        body, name=name, grid=(R // tr,), in_specs=in_specs, out_specs=[blk, blk, blk, blk],
        out_shape=[shp, shp, shp, shp], input_output_aliases=aliases, compiler_params=_params(1),
    )(*args)


HBM_SPEC = pl.BlockSpec(memory_space=pltpu.HBM)
SEM_SPEC = pl.BlockSpec(memory_space=pltpu.SEMAPHORE)
EFFECT = pltpu.SideEffectType.DATAFLOW_SIDE_EFFECTING


def _mesh_pos():
    return lax.axis_index("x"), lax.axis_index("y"), lax.axis_index("c")


def _flip(v, bit):
    return v + bit - 2 * v * bit


def _peer(pos, delta):
    x, y, c = pos
    px, py, pc = _flip(x, (delta >> 2) & 1), _flip(y, (delta >> 1) & 1), _flip(c, delta & 1)
    return (px, py, pc), 4 * px + 2 * py + pc


def _me():
    x, y, c = _mesh_pos()
    return 4 * x + 2 * y + c


def _copies(src_refs, land_refs, whole, send, recv, incoming):
    pos = _mesh_pos()
    me = 4 * pos[0] + 2 * pos[1] + pos[2]
    out = []
    for k in range(len(src_refs)):
        for d in range(1, N_DEV):
            dev, idx = _peer(pos, d)
            j = k * (N_DEV - 1) + d - 1
            src = src_refs[k] if whole[k] else src_refs[k].at[idx]
            out.append(pltpu.make_async_remote_copy(
                src_ref=src, dst_ref=land_refs[k].at[idx if incoming else me], send_sem=send.at[j], recv_sem=recv.at[j],
                device_id=dev, device_id_type=pl.DeviceIdType.MESH))
    return out


def exchange_start(srcs, lands, whole, name):
    n = len(srcs)

    def body(*refs):
        for copy in _copies(refs[:n], refs[n:2 * n], whole, refs[2 * n], refs[2 * n + 1], False):
            copy.start()
        refs[-1][...] = jnp.zeros_like(refs[-1])

    sems = pltpu.SemaphoreType.DMA((n * (N_DEV - 1),))
    thru = [pltpu.HBM(a.shape, a.dtype) for a in list(srcs) + list(lands)]
    res = pl.pallas_call(
        body, name=name, in_specs=[HBM_SPEC] * (2 * n),
        out_specs=[SEM_SPEC, SEM_SPEC] + [HBM_SPEC] * (2 * n) + [pl.BlockSpec(memory_space=pltpu.VMEM)],
        out_shape=[sems, sems] + thru + [jax.ShapeDtypeStruct((8, LANES), F32)],
        input_output_aliases={j: 2 + j for j in range(2 * n)},
        compiler_params=pltpu.CompilerParams(has_side_effects=EFFECT),
    )(*[pltpu.with_memory_space_constraint(a, pltpu.HBM) for a in list(srcs) + list(lands)])
    return dict(send=res[0], recv=res[1], srcs=res[2:2 + n], lands=res[2 + n:2 + 2 * n], whole=whole, token=res[-1])


def exchange_wait(handle, after, name):
    n = len(handle["srcs"])
    whole = handle["whole"]

    def body(*refs):
        for copy in _copies(refs[:n], refs[n:2 * n], whole, refs[2 * n], refs[2 * n + 1], False):
            copy.wait_send()
        for copy in _copies(refs[:n], refs[n:2 * n], whole, refs[2 * n], refs[2 * n + 1], True):
            copy.wait_recv()

    bufs = list(handle["srcs"]) + list(handle["lands"])
    res = pl.pallas_call(
        body, name=name, in_specs=[HBM_SPEC] * (2 * n) + [SEM_SPEC, SEM_SPEC] + [pl.BlockSpec(memory_space=pl.ANY)] * len(after),
        out_specs=[HBM_SPEC] * (2 * n), out_shape=[pltpu.HBM(a.shape, a.dtype) for a in bufs],
        input_output_aliases={j: j for j in range(2 * n)},
        compiler_params=pltpu.CompilerParams(has_side_effects=EFFECT),
    )(*bufs, handle["send"], handle["recv"], *after)
    return list(res[n:])


def _landing(own, whole):
    me = _me()
    if not whole:
        own = lax.dynamic_index_in_dim(own, me, 0, keepdims=False)
    buf = lax.empty((N_DEV,) + own.shape, own.dtype)
    return lax.dynamic_update_slice(buf, own[None], (me,) + (0,) * own.ndim)


def _row(v):
    return v.reshape(1, -1)


def _pad_lanes(v, n):
    return jnp.pad(v, ((0, 0), (0, n - v.shape[1])))


GATHER_GROUPS = (
    ("ffn1_0", (("ffn1_w_in", 0), ("ffn1_w_out", 0))),
    ("hgrn", (("hgrn_w_in", 0), ("hgrn_w_out", 0))),
    ("rest_0", (("ffn2_w_in", 0), ("ffn2_w_out", 0), ("ple_w_gate", 0), ("ple_w_proj", 0), ("fox_w_kvf", 0))),
    ("layer_1", (("ffn1_w_in", 1), ("ffn1_w_out", 1), ("fox_w_qg", 0), ("fox_w_out", 0), ("ffn2_w_in", 1),
                 ("ffn2_w_out", 1), ("ple_w_gate", 1), ("ple_w_proj", 1))),
)


def kernel(x, p, ffn1_norm_pre, ffn1_w_in, ffn1_w_out, ffn1_norm_post, mix_norm_pre, mix_norm_post, ffn2_norm_pre, ffn2_w_in, ffn2_w_out, ffn2_norm_post, hgrn_w_in, hgrn_lb_logits, hgrn_out_norm, hgrn_w_out, kv_norm, fox_w_kvf, fox_b_f, fox_w_qg, fox_w_out, ple_norm_pre, ple_w_gate, ple_w_proj, ple_norm_post, loss_target, m_ffn1_norm_pre, m_ffn1_w_in, m_ffn1_w_out, m_ffn1_norm_post, m_mix_norm_pre, m_mix_norm_post, m_ffn2_norm_pre, m_ffn2_w_in, m_ffn2_w_out, m_ffn2_norm_post, m_hgrn_w_in, m_hgrn_lb_logits, m_hgrn_out_norm, m_hgrn_w_out, m_kv_norm, m_fox_w_kvf, m_fox_b_f, m_fox_w_qg, m_fox_w_out, m_ple_norm_pre, m_ple_w_gate, m_ple_w_proj, m_ple_norm_post, v_ffn1_norm_pre, v_ffn1_w_in, v_ffn1_w_out, v_ffn1_norm_post, v_mix_norm_pre, v_mix_norm_post, v_ffn2_norm_pre, v_ffn2_w_in, v_ffn2_w_out, v_ffn2_norm_post, v_hgrn_w_in, v_hgrn_lb_logits, v_hgrn_out_norm, v_hgrn_w_out, v_kv_norm, v_fox_w_kvf, v_fox_b_f, v_fox_w_qg, v_fox_w_out, v_ple_norm_pre, v_ple_w_gate, v_ple_w_proj, v_ple_norm_post):
    weights = dict(ffn1_norm_pre=ffn1_norm_pre, ffn1_w_in=ffn1_w_in, ffn1_w_out=ffn1_w_out, ffn1_norm_post=ffn1_norm_post, mix_norm_pre=mix_norm_pre, mix_norm_post=mix_norm_post, ffn2_norm_pre=ffn2_norm_pre, ffn2_w_in=ffn2_w_in, ffn2_w_out=ffn2_w_out, ffn2_norm_post=ffn2_norm_post, hgrn_w_in=hgrn_w_in, hgrn_lb_logits=hgrn_lb_logits, hgrn_out_norm=hgrn_out_norm, hgrn_w_out=hgrn_w_out, kv_norm=kv_norm, fox_w_kvf=fox_w_kvf, fox_b_f=fox_b_f, fox_w_qg=fox_w_qg, fox_w_out=fox_w_out, ple_norm_pre=ple_norm_pre, ple_w_gate=ple_w_gate, ple_w_proj=ple_w_proj, ple_norm_post=ple_norm_post)
    mom1 = dict(ffn1_norm_pre=m_ffn1_norm_pre, ffn1_w_in=m_ffn1_w_in, ffn1_w_out=m_ffn1_w_out, ffn1_norm_post=m_ffn1_norm_post, mix_norm_pre=m_mix_norm_pre, mix_norm_post=m_mix_norm_post, ffn2_norm_pre=m_ffn2_norm_pre, ffn2_w_in=m_ffn2_w_in, ffn2_w_out=m_ffn2_w_out, ffn2_norm_post=m_ffn2_norm_post, hgrn_w_in=m_hgrn_w_in, hgrn_lb_logits=m_hgrn_lb_logits, hgrn_out_norm=m_hgrn_out_norm, hgrn_w_out=m_hgrn_w_out, kv_norm=m_kv_norm, fox_w_kvf=m_fox_w_kvf, fox_b_f=m_fox_b_f, fox_w_qg=m_fox_w_qg, fox_w_out=m_fox_w_out, ple_norm_pre=m_ple_norm_pre, ple_w_gate=m_ple_w_gate, ple_w_proj=m_ple_w_proj, ple_norm_post=m_ple_norm_post)
    mom2 = dict(ffn1_norm_pre=v_ffn1_norm_pre, ffn1_w_in=v_ffn1_w_in, ffn1_w_out=v_ffn1_w_out, ffn1_norm_post=v_ffn1_norm_post, mix_norm_pre=v_mix_norm_pre, mix_norm_post=v_mix_norm_post, ffn2_norm_pre=v_ffn2_norm_pre, ffn2_w_in=v_ffn2_w_in, ffn2_w_out=v_ffn2_w_out, ffn2_norm_post=v_ffn2_norm_post, hgrn_w_in=v_hgrn_w_in, hgrn_lb_logits=v_hgrn_lb_logits, hgrn_out_norm=v_hgrn_out_norm, hgrn_w_out=v_hgrn_w_out, kv_norm=v_kv_norm, fox_w_kvf=v_fox_w_kvf, fox_b_f=v_fox_b_f, fox_w_qg=v_fox_w_qg, fox_w_out=v_fox_w_out, ple_norm_pre=v_ple_norm_pre, ple_w_gate=v_ple_w_gate, ple_w_proj=v_ple_w_proj, ple_norm_post=v_ple_norm_post)
    names = list(weights)
    big = ["ffn1_w_in", "ffn1_w_out", "ffn2_w_in", "ffn2_w_out", "hgrn_w_in", "hgrn_w_out", "fox_w_kvf", "fox_w_qg",
           "fox_w_out", "ple_w_gate", "ple_w_proj"]
    small_names = [n for n in names if n not in big]

    T, D = x.shape[1], x.shape[2]
    depth = p.shape[0]
    h0 = x.reshape(T, D)
    target = loss_target.reshape(T, D)
    p3 = p.reshape(depth, T, p.shape[3])
    n_fox = fox_b_f.shape[0]
    fox_w = n_fox * FOX_HEAD_DIM
    fcol = 2 * fox_w // LANES
    b_row = _pad_lanes(_row(fox_b_f), LANES)

    tok = jnp.zeros((), F32)
    handles = {}
    for gname, keys in GATHER_GROUPS:
        shards = []
        for n, l in keys:
            w = weights[n]
            shards.append(((w[l] if w.ndim == 3 else w) + tok).astype(BF))
        handles[gname] = exchange_start(shards, [_landing(s, True) for s in shards], [True] * len(keys), f"gather_start_{gname}")
        tok = handles[gname]["token"][0, 0]
    W = {}

    def arrive(gname, after):
        lands = exchange_wait(handles[gname], after, f"gather_wait_{gname}")
        W.update(dict(zip(dict(GATHER_GROUPS)[gname], lands)))

    def w_rows(n, l):
        return W[n, l].reshape(-1, D)

    norm = lambda name, i: weights[name][i:i + 1]

    saved = []
    h = h0
    kvf = ct = kvf_w = None
    tq = _tile(T, 512)
    for i in range(depth):
        s = {}
        if i == 0:
            arrive("ffn1_0", [handles["layer_1"]["token"]])
        for k in (1, 2):
            if k == 2:
                s["h_a"] = h
                if i == 0:
                    arrive("hgrn", [h])
                    z, xn = norm_mm(h, norm("mix_norm_pre", i), W["hgrn_w_in", 0], "hgrn_in")
                    xm, o, states = hgrn_fwd(z, hgrn_lb_logits, hgrn_out_norm, "hgrn_scan")
                    s.update(z=z, o=o, states=states)
                    wmix = w_rows("hgrn_w_out", 0)
                else:
                    qg, xn = norm_mm(h, norm("mix_norm_pre", i), W["fox_w_qg", 0], "fox_qg")
                    ka, kb, vt3 = fox_prep(kvf, ct, n_fox, tq)
                    xm, o, lse = fox_fwd(qg, ka, kb, vt3, "fox_attn")
                    s.update(qg=qg, o=o, lse=lse, ka=ka, kb=kb)
                    wmix = w_rows("fox_w_out", 0)
                h, ym = mm_norm_res(xm, wmix, norm("mix_norm_post", i), h, 1.0, f"mix_out_{i}", kb=_tile(xm.shape[1], 512))
                s.update(xm=xm, ym=ym, xn_mix=xn)
                if i == 0:
                    arrive("rest_0", [h])
            gate, up, a, xn = norm_mm_swiglu(h, norm(f"ffn{k}_norm_pre", i), W[f"ffn{k}_w_in", i], f"ffn{k}_in_{i}")
            hn, y = mm_norm_res(a, w_rows(f"ffn{k}_w_out", i), norm(f"ffn{k}_norm_post", i), h, 0.5, f"ffn{k}_out_{i}")
            s[f"ffn{k}"] = (h, gate, up, a, y, xn)
            h = hn
        s["h_c"] = h
        ple_proj = W["ple_w_proj", i].transpose(1, 0, 2).reshape(p.shape[3], D)
        h, pgate, pp = ple_fwd(h, norm("ple_norm_pre", i), w_rows("ple_w_gate", i), p3, ple_proj, norm("ple_norm_post", i),
                               i, f"ple_{i}")
        s.update(pgate=pgate, pp=pp)
        saved.append(s)
        if i == 0:
            kvf_nat = W["fox_w_kvf", 0].transpose(1, 0, 2).reshape(D, -1)
            kvf_cols = kvf_nat.shape[1]
            kvf_w = _pad_lanes(kvf_nat, 2 * fox_w + LANES)
            kvf, xn_kv = norm_mm(h, _row(kv_norm), kvf_w, "fox_kvf", tn=LANES)
            ct = fox_cum_fwd(kvf, fcol, b_row, "fox_cum")
            h_kv = h
            arrive("layer_1", [h])

    dh, loss_part = loss_fwd_bwd(h, target, "loss")
    loss = lax.psum(loss_part[0, 0], ("x", "y", "c"))

    gsmall = {n: [None] * weights[n].shape[0] if weights[n].ndim == 2 else None for n in small_names}
    sent = []

    def send(gname, keys, srcs, whole):
        lands = [_landing(a, w) for a, w in zip(srcs, whole)]
        hd = exchange_start(srcs, lands, whole, f"scatter_start_{gname}")
        sent.append((gname, keys, hd))
        return hd["token"][0:1, 0:1]

    def send_grads(gname, grads):
        return send(gname, list(grads), list(grads.values()), [False] * len(grads))

    for i in reversed(range(depth)):
        s = saved[i]
        grads = {}
        dh, du, dpp, xn, dgpre, dgpost = ple_bwd(dh, s["h_c"], s["pgate"], s["pp"], norm("ple_norm_pre", i),
                                                 w_rows("ple_w_gate", i), norm("ple_norm_post", i), f"ple_bwd_{i}")
        gsmall["ple_norm_pre"][i], gsmall["ple_norm_post"][i] = dgpre, dgpost
        grads["ple_w_gate", i] = mm_tn(xn, du, f"ple_dgate_{i}", xb=D // N_DEV).reshape(N_DEV, -1, D)
        grads["ple_w_proj", i] = mm_tn(p3, dpp, f"ple_dproj_{i}", x_layer=i, yb=D // N_DEV)
        for k in (2, 1):
            hin, gate, up, a, y, xn = s[f"ffn{k}"]
            ffn_cs = gate.shape[2]
            dy, dgpost, dz = nbwd_mm_nt(dh, y, norm(f"ffn{k}_norm_post", i), w_rows(f"ffn{k}_w_out", i), 0.5, ffn_cs,
                                        f"ffn{k}_bwd_out_{i}", gate=gate, up=up)
            dz = dz.reshape(-1, T, ffn_cs)
            grads[f"ffn{k}_w_out", i] = mm_tn(a, dy, f"ffn{k}_dwout_{i}").reshape(N_DEV, -1, D)
            grads[f"ffn{k}_w_in", i] = mm_tn(xn, dz, f"ffn{k}_dwin_{i}")
            tokv = send_grads(f"ffn{k}_{i}", grads)
            grads = {}
            dh, dgpre = mm_nt_nbwd(dz, W[f"ffn{k}_w_in", i], hin, norm(f"ffn{k}_norm_pre", i) + tokv, dh, f"ffn{k}_bwd_in_{i}")
            gsmall[f"ffn{k}_norm_pre"][i], gsmall[f"ffn{k}_norm_post"][i] = dgpre, dgpost
            if k == 2:
                nm = "hgrn" if i == 0 else "fox"
                wmix = w_rows(f"{nm}_w_out", 0)
                dy, dgpost, dxm = nbwd_mm_nt(dh, s["ym"], norm("mix_norm_post", i), wmix, 1.0, _tile(wmix.shape[0], 512),
                                             f"{nm}_bwd_out")
                grads[f"{nm}_w_out", 0] = mm_tn(s["xm"], dy, f"{nm}_dwout", xb=wmix.shape[0] // N_DEV).reshape(N_DEV, -1, D)
                gsmall["mix_norm_post"][i] = dgpost
                if i == 0:
                    dq, df, dv, dg, dlog, don = hgrn_bwd(dxm, s["z"], s["o"], s["states"], hgrn_lb_logits, hgrn_out_norm,
                                                         "hgrn_scan_bwd")
                    gsmall["hgrn_lb_logits"] = [dlog[0:1], dlog[1:2]]
                    gsmall["hgrn_out_norm"] = [don]
                    dzm = jnp.concatenate([dq, df, dv, dg], axis=1)
                    nmin = "hgrn_w_in"
                else:
                    do, dg, dsum = fox_gate_bwd(dxm, s["o"], s["qg"], "fox_gate_bwd")
                    dsum4 = dsum[:, :n_fox].T.reshape(n_fox // 2, 2, T // tq, tq).transpose(0, 2, 1, 3)
                    dqt, dk_sh, dv_sh, dc4, dcq4 = fox_bwd(*fox_bwd_prep(s["qg"], kvf, do, tq), s["ka"], s["kb"], s["lse"], dsum4,
                                                           "fox_attn_bwd")
                    dq = dqt.transpose(1, 3, 0, 2).reshape(T, fox_w)
                    dzm = jnp.concatenate([dq.astype(BF), dg], axis=1)
                    nmin = "fox_w_qg"
                wmin = W[nmin, 0]
                grads[nmin, 0] = mm_tn(s["xn_mix"], dzm, f"{nm}_dwin", yb=wmin.shape[2])
                tokv = send_grads(f"mix_{i}", grads)
                grads = {}
                dh, dgpre = mm_nt_nbwd(dzm, wmin, s["h_a"], norm("mix_norm_pre", i) + tokv, dh, f"{nm}_bwd_in", tn=wmin.shape[2])
                gsmall["mix_norm_pre"][i] = dgpre
        if i == 1:
            dct = (dc4 + dcq4).transpose(0, 2, 1, 3).reshape(n_fox, T)
            dct = jnp.pad(dct, ((0, LANES - n_fox), (0, 0)))
            dflog, db = fox_cum_bwd(dct, kvf, fcol, b_row, "fox_cum_bwd")
            gsmall["fox_b_f"] = db[:, :n_fox]
            dkvf = jnp.concatenate([dk_sh, dv_sh, dflog], axis=1)
            dwk = mm_tn(xn_kv, dkvf, "fox_dwkvf", yb=LANES)
            dwk = dwk.transpose(1, 0, 2).reshape(D, -1)[:, :kvf_cols]
            tokv = send_grads("kvf", {("fox_w_kvf", 0): dwk.reshape(D, N_DEV, -1).transpose(1, 0, 2)})
            dh, dgkv = mm_nt_nbwd(dkvf, kvf_w, h_kv, _row(kv_norm) + tokv, dh, "fox_kvf_bwd", tn=LANES)
            gsmall["kv_norm"] = dgkv
    grad_x = dh.reshape(x.shape)

    def small_rows(n):
        g = gsmall[n]
        rows = g if isinstance(g, list) else [g]
        return [_pad_lanes(r, D) for r in rows]

    counts = {n: len(small_rows(n)) for n in small_names}
    packed = jnp.concatenate([r for n in small_names for r in small_rows(n)], axis=0)
    n_rows = packed.shape[0]
    packed = jnp.pad(packed, ((0, -n_rows % 8), (0, 0)))
    send("small", ["small"], [packed], [True])

    res = {}
    after = [dh]
    small_parts = None
    for gname, keys, hd in sent:
        lands = exchange_wait(hd, after, f"scatter_wait_{gname}")
        after = []
        for key, parts in zip(keys, lands):
            if key == "small":
                small_parts = parts
                continue
            n, l = key
            w = weights[n]
            as3 = (lambda a: a.reshape((1,) + a.shape)) if w.ndim == 2 else (lambda a: a)
            res[n] = adamw(parts, as3(w), as3(mom1[n]), as3(mom2[n]), l, res.get(n), f"adamw_{n}_{l}")
            after.append(res[n][1])

    def pack(d):
        rows = []
        for n in small_names:
            a = d[n]
            rows.append(_pad_lanes(a.reshape(-1, a.shape[-1]), D))
        a = jnp.concatenate(rows, axis=0)
        return jnp.pad(a, ((0, -n_rows % 8), (0, 0)))[None]

    sm = adamw(small_parts, pack(weights), pack(mom1), pack(mom2), 0, None, "adamw_small")
    off = 0
    for n in small_names:
        w = weights[n]
        res[n] = [a[0, off:off + counts[n], :w.shape[-1]].reshape(w.shape) for a in sm]
        off += counts[n]

    out = [loss, grad_x]
    for j in range(4):
        out += [res[n][j].reshape(weights[n].shape) for n in names]
    return tuple(out)
```

```python
import functools

import jax
import jax.numpy as jnp
from jax import lax
from jax.experimental import pallas as pl
from jax.experimental.pallas import tpu as pltpu

F32 = jnp.float32
BF = jnp.bfloat16
NORM_EPS = 1e-6
N_DEV = 8
HGRN_DK = 128
HGRN_CHUNK = 16
HGRN_HEADS_PER_STEP = 4
FOX_HEAD_DIM = 64
LANES = 128
ADAM_LR, ADAM_B1, ADAM_B2, ADAM_EPS, ADAM_WD, ADAM_STEP = 0.001, 0.9, 0.999, 1e-08, 0.01, 10
VMEM_LIMIT = 56 * 1024 * 1024
HI = lax.Precision.HIGHEST
NT = (((1,), (1,)), ((), ()))
TN = (((0,), (0,)), ((), ()))


def _params(n_axes):
    return pltpu.CompilerParams(dimension_semantics=("arbitrary",) * n_axes, vmem_limit_bytes=VMEM_LIMIT)


def _tile(n, want):
    t = min(n, want)
    while n % t:
        t //= 2
    return t


def _sigmoid(x):
    return 1.0 / (1.0 + jnp.exp(-x))


def _rms(x):
    r = lax.rsqrt(jnp.mean(x * x, axis=-1, keepdims=True) + NORM_EPS)
    return x * r, r


def _norm_bwd(dy, xhat, r, g):
    dxh = dy * g
    return r * (dxh - xhat * jnp.mean(dxh * xhat, axis=-1, keepdims=True))


def _colsum(x):
    return jnp.sum(x, axis=0, keepdims=True)


def _w_spec(w, blk):
    if w.ndim == 3:
        return lambda off: pl.BlockSpec((None, w.shape[1], w.shape[2]), lambda i, j: (j + off, 0, 0))
    return lambda off: pl.BlockSpec((w.shape[0], blk), lambda i, j: (0, j + off))


def norm_mm_swiglu(h, g, w3, name):
    T, D = h.shape
    nb, _, cs = w3.shape
    nh = nb // 2
    tm = _tile(T, 1024)

    def body(h_ref, g_ref, wg_ref, wu_ref, gate_ref, up_ref, a_ref, xn_ref):
        @pl.when(pl.program_id(1) == 0)
        def _():
            xh, _ = _rms(h_ref[...])
            xn_ref[...] = (xh * g_ref[...]).astype(BF)

        xn = xn_ref[...]
        gt = jnp.dot(xn, wg_ref[...], preferred_element_type=F32)
        up = jnp.dot(xn, wu_ref[...], preferred_element_type=F32)
        gate_ref[...] = gt.astype(BF)
        up_ref[...] = up.astype(BF)
        a_ref[...] = (gt * _sigmoid(gt) * up).astype(BF)

    ws = _w_spec(w3, cs)
    blk = pl.BlockSpec((None, tm, cs), lambda i, j: (j, i, 0))
    shp = jax.ShapeDtypeStruct((nh, T, cs), BF)
    return pl.pallas_call(
        body, name=name, grid=(T // tm, nh),
        in_specs=[pl.BlockSpec((tm, D), lambda i, j: (i, 0)), pl.BlockSpec((1, D), lambda i, j: (0, 0)), ws(0), ws(nh)],
        out_specs=[blk, blk, blk, pl.BlockSpec((tm, D), lambda i, j: (i, 0))],
        out_shape=[shp, shp, shp, jax.ShapeDtypeStruct((T, D), BF)], compiler_params=_params(2),
    )(h, g, w3, w3)


def norm_mm(h, g, w, name, tn=None):
    T, D = h.shape
    if w.ndim == 3:
        nb, cs = w.shape[0], w.shape[2]
    else:
        cs = tn
        nb = w.shape[1] // cs
    tm = _tile(T, 1024)

    def body(h_ref, g_ref, w_ref, z_ref, xn_ref):
        @pl.when(pl.program_id(1) == 0)
        def _():
            xh, _ = _rms(h_ref[...])
            xn_ref[...] = (xh * g_ref[...]).astype(BF)

        z_ref[...] = jnp.dot(xn_ref[...], w_ref[...], preferred_element_type=F32)

    return pl.pallas_call(
        body, name=name, grid=(T // tm, nb),
        in_specs=[pl.BlockSpec((tm, D), lambda i, j: (i, 0)), pl.BlockSpec((1, D), lambda i, j: (0, 0)),
                  _w_spec(w, cs)(0)],
        out_specs=[pl.BlockSpec((tm, cs), lambda i, j: (i, j)), pl.BlockSpec((tm, D), lambda i, j: (i, 0))],
        out_shape=[jax.ShapeDtypeStruct((T, nb * cs), F32), jax.ShapeDtypeStruct((T, D), BF)], compiler_params=_params(2),
    )(h, g, w)


def _x_spec(x, tm, kb):
    if x.ndim == 3:
        return pl.BlockSpec((None, tm, x.shape[2]), lambda i, j: (j, i, 0))
    return pl.BlockSpec((tm, kb), lambda i, j: (i, j))


def mm_norm_res(x, w2, g, h, coef, name, kb=None):
    T, D = h.shape
    if x.ndim == 3:
        nb, kb = x.shape[0], x.shape[2]
    else:
        nb = x.shape[1] // kb
    tm = _tile(T, 512)

    def body(x_ref, w_ref, h_ref, g_ref, hn_ref, y_ref, acc_ref):
        b = pl.program_id(1)

        @pl.when(b == 0)
        def _():
            acc_ref[...] = jnp.zeros_like(acc_ref)

        acc_ref[...] += jnp.dot(x_ref[...], w_ref[...], preferred_element_type=F32)

        @pl.when(b == nb - 1)
        def _():
            y = acc_ref[...]
            y_ref[...] = y
            yh, _ = _rms(y)
            hn_ref[...] = h_ref[...] + coef * (yh * g_ref[...])

    tok = pl.BlockSpec((tm, D), lambda i, j: (i, 0))
    shp = jax.ShapeDtypeStruct((T, D), F32)
    return pl.pallas_call(
        body, name=name, grid=(T // tm, nb),
        in_specs=[_x_spec(x, tm, kb), pl.BlockSpec((kb, D), lambda i, j: (j, 0)), tok,
                  pl.BlockSpec((1, D), lambda i, j: (0, 0))],
        out_specs=[tok, tok], out_shape=[shp, shp],
        scratch_shapes=[pltpu.VMEM((tm, D), F32)], compiler_params=_params(2),
    )(x, w2, h, g)


def nbwd_mm_nt(dout, y, g, w2, coef, kb, name, gate=None, up=None):
    T, D = dout.shape
    nb = w2.shape[0] // kb
    swiglu = gate is not None
    tm = _tile(T, 512)

    def body(*refs):
        if swiglu:
            dout_ref, y_ref, g_ref, w_ref, gate_ref, up_ref, dy_ref, dg_ref, da_ref, dys_ref = refs
        else:
            dout_ref, y_ref, g_ref, w_ref, dy_ref, dg_ref, da_ref, dys_ref = refs
        i, b = pl.program_id(0), pl.program_id(1)

        @pl.when((i == 0) & (b == 0))
        def _():
            dg_ref[...] = jnp.zeros_like(dg_ref)

        @pl.when(b == 0)
        def _():
            yh, r = _rms(y_ref[...])
            dyn = coef * dout_ref[...]
            dg_ref[...] += _colsum(dyn * yh)
            dy = _norm_bwd(dyn, yh, r, g_ref[...]).astype(BF)
            dys_ref[...] = dy
            dy_ref[...] = dy

        da = lax.dot_general(dys_ref[...], w_ref[...], NT, preferred_element_type=F32)
        if swiglu:
            gt = gate_ref[...].astype(F32)
            u = up_ref[...].astype(F32)
            sg = _sigmoid(gt)
            da_ref[0] = (da * u * (sg * (1.0 + gt * (1.0 - sg)))).astype(BF)
            da_ref[1] = (da * (gt * sg)).astype(BF)
        else:
            da_ref[...] = da.astype(BF)

    tok = pl.BlockSpec((tm, D), lambda i, j: (i, 0))
    vec = pl.BlockSpec((1, D), lambda i, j: (0, 0))
    in_specs = [tok, tok, vec, pl.BlockSpec((kb, D), lambda i, j: (j, 0))]
    args = [dout, y, g, w2]
    if swiglu:
        blk = pl.BlockSpec((None, tm, kb), lambda i, j: (j, i, 0))
        in_specs += [blk, blk]
        args += [gate, up]
        da_spec = pl.BlockSpec((2, None, tm, kb), lambda i, j: (0, j, i, 0))
        da_shape = jax.ShapeDtypeStruct((2, nb, T, kb), BF)
    else:
        da_spec = pl.BlockSpec((tm, kb), lambda i, j: (i, j))
        da_shape = jax.ShapeDtypeStruct((T, nb * kb), BF)
    return pl.pallas_call(
        body, name=name, grid=(T // tm, nb), in_specs=in_specs,
        out_specs=[tok, vec, da_spec],
        out_shape=[jax.ShapeDtypeStruct((T, D), BF), jax.ShapeDtypeStruct((1, D), F32), da_shape],
        scratch_shapes=[pltpu.VMEM((tm, D), BF)], compiler_params=_params(2),
    )(*args)


def mm_nt_nbwd(dz, w, h, g, dout, name, tn=None):
    T, D = h.shape
    if w.ndim == 3:
        nb, cs = w.shape[0], w.shape[2]
    else:
        cs = tn
        nb = w.shape[1] // cs
    tm = _tile(T, 512)

    def body(dz_ref, w_ref, h_ref, g_ref, dout_ref, dh_ref, dg_ref, acc_ref):
        i, b = pl.program_id(0), pl.program_id(1)

        @pl.when((i == 0) & (b == 0))
        def _():
            dg_ref[...] = jnp.zeros_like(dg_ref)

        @pl.when(b == 0)
        def _():
            acc_ref[...] = jnp.zeros_like(acc_ref)

        acc_ref[...] += lax.dot_general(dz_ref[...], w_ref[...], NT, preferred_element_type=F32)

        @pl.when(b == nb - 1)
        def _():
            xh, r = _rms(h_ref[...])
            gg = g_ref[...]
            dxn = acc_ref[...]
            dg_ref[...] += _colsum(dxn * xh)
            dh_ref[...] = dout_ref[...] + _norm_bwd(dxn, xh, r, gg)

    tok = pl.BlockSpec((tm, D), lambda i, j: (i, 0))
    vec = pl.BlockSpec((1, D), lambda i, j: (0, 0))
    return pl.pallas_call(
        body, name=name, grid=(T // tm, nb),
        in_specs=[_x_spec(dz, tm, cs), _w_spec(w, cs)(0), tok, vec, tok],
        out_specs=[tok, vec],
        out_shape=[jax.ShapeDtypeStruct((T, D), F32), jax.ShapeDtypeStruct((1, D), F32)],
        scratch_shapes=[pltpu.VMEM((tm, D), F32)], compiler_params=_params(2),
    )(dz, w, h, g, dout)


def mm_tn(x, y, name, xb=None, yb=None, x_layer=None):
    T = y.shape[-2]
    tt = _tile(T, 512)
    x_split = (x.ndim == 3 and x_layer is None) or xb is not None
    if x_layer is not None:
        xs = pl.BlockSpec((None, tt, x.shape[2]), lambda b, t: (x_layer, t, 0))
        kdim = x.shape[2]
    elif x.ndim == 3:
        xs = pl.BlockSpec((None, tt, x.shape[2]), lambda b, t: (b, t, 0))
        nb, kdim = x.shape[0], x.shape[2]
    elif xb is not None:
        xs = pl.BlockSpec((tt, xb), lambda b, t: (t, b))
        nb, kdim = x.shape[1] // xb, xb
    else:
        xs = pl.BlockSpec((tt, x.shape[1]), lambda b, t: (t, 0))
        kdim = x.shape[1]
    if x_split:
        ys = pl.BlockSpec((tt, y.shape[1]), lambda b, t: (t, 0))
        ndim = y.shape[1]
        out_spec = pl.BlockSpec((kdim, ndim), lambda b, t: (b, 0))
        out_shape = jax.ShapeDtypeStruct((nb * kdim, ndim), BF)
    else:
        if y.ndim == 3:
            ys = pl.BlockSpec((None, tt, y.shape[2]), lambda b, t: (b, t, 0))
            nb, ndim = y.shape[0], y.shape[2]
        else:
            ys = pl.BlockSpec((tt, yb), lambda b, t: (t, b))
            nb, ndim = y.shape[1] // yb, yb
        out_spec = pl.BlockSpec((None, kdim, ndim), lambda b, t: (b, 0, 0))
        out_shape = jax.ShapeDtypeStruct((nb, kdim, ndim), BF)
    nt = T // tt

    def body(x_ref, y_ref, o_ref, acc_ref):
        t = pl.program_id(1)

        @pl.when(t == 0)
        def _():
            acc_ref[...] = jnp.zeros_like(acc_ref)

        acc_ref[...] += lax.dot_general(x_ref[...].astype(BF), y_ref[...].astype(BF), TN, preferred_element_type=F32)

        @pl.when(t == nt - 1)
        def _():
            o_ref[...] = acc_ref[...].astype(BF)

    return pl.pallas_call(
        body, name=name, grid=(nb, nt), in_specs=[xs, ys], out_specs=out_spec, out_shape=out_shape,
        scratch_shapes=[pltpu.VMEM((kdim, ndim), F32)], compiler_params=_params(2),
    )(x, y)


def ple_fwd(h, gpre, wg, p3, wp, gpost, layer, name):
    T, D = h.shape
    pd = p3.shape[2]
    tm = _tile(T, 512)

    def body(h_ref, gpre_ref, wg_ref, p_ref, wp_ref, gpost_ref, hn_ref, gate_ref, pp_ref):
        x = h_ref[...]
        xh, _ = _rms(x)
        u = jnp.dot((xh * gpre_ref[...]).astype(BF), wg_ref[...], preferred_element_type=F32)
        gate = _sigmoid(u)
        pp = jnp.dot(p_ref[...].astype(BF), wp_ref[...], preferred_element_type=F32)
        yh, _ = _rms(gate * pp)
        hn_ref[...] = x + yh * gpost_ref[...]
        gate_ref[...] = gate.astype(BF)
        pp_ref[...] = pp.astype(BF)

    tok = pl.BlockSpec((tm, D), lambda i: (i, 0))
    vec = pl.BlockSpec((1, D), lambda i: (0, 0))
    return pl.pallas_call(
        body, name=name, grid=(T // tm,),
        in_specs=[tok, vec, pl.BlockSpec((D, D), lambda i: (0, 0)),
                  pl.BlockSpec((None, tm, pd), lambda i: (layer, i, 0)),
                  pl.BlockSpec((pd, D), lambda i: (0, 0)), vec],
        out_specs=[tok, tok, tok],
        out_shape=[jax.ShapeDtypeStruct((T, D), F32), jax.ShapeDtypeStruct((T, D), BF), jax.ShapeDtypeStruct((T, D), BF)],
        compiler_params=_params(1),
    )(h, gpre, wg, p3, wp, gpost)


def ple_bwd(dout, h, gate, pp, gpre, wg, gpost, name):
    T, D = h.shape
    tm = _tile(T, 512)

    def body(dout_ref, h_ref, gate_ref, pp_ref, gpre_ref, wg_ref, gpost_ref, dh_ref, du_ref, dpp_ref, xn_ref, dgpre_ref, dgpost_ref):
        @pl.when(pl.program_id(0) == 0)
        def _():
            dgpre_ref[...] = jnp.zeros_like(dgpre_ref)
            dgpost_ref[...] = jnp.zeros_like(dgpost_ref)

        dout = dout_ref[...]
        gate = gate_ref[...].astype(F32)
        pp = pp_ref[...].astype(F32)
        yh, ry = _rms(gate * pp)
        dgpost_ref[...] += _colsum(dout * yh)
        dy = _norm_bwd(dout, yh, ry, gpost_ref[...])
        dpp_ref[...] = (dy * gate).astype(BF)
        du = (dy * pp * gate * (1.0 - gate)).astype(BF)
        du_ref[...] = du
        dxn = lax.dot_general(du, wg_ref[...], NT, preferred_element_type=F32)
        xh, r = _rms(h_ref[...])
        gp = gpre_ref[...]
        dgpre_ref[...] += _colsum(dxn * xh)
        dh_ref[...] = dout + _norm_bwd(dxn, xh, r, gp)
        xn_ref[...] = (xh * gp).astype(BF)

    tok = pl.BlockSpec((tm, D), lambda i: (i, 0))
    vec = pl.BlockSpec((1, D), lambda i: (0, 0))
    bft = jax.ShapeDtypeStruct((T, D), BF)
    v32 = jax.ShapeDtypeStruct((1, D), F32)
    return pl.pallas_call(
        body, name=name, grid=(T // tm,),
        in_specs=[tok, tok, tok, tok, vec, pl.BlockSpec((D, D), lambda i: (0, 0)), vec],
        out_specs=[tok, tok, tok, tok, vec, vec],
        out_shape=[jax.ShapeDtypeStruct((T, D), F32), bft, bft, bft, v32, v32],
        compiler_params=_params(1),
    )(dout, h, gate, pp, gpre, wg, gpost)


def _chunk_tri(tb, upper):
    r = lax.broadcasted_iota(jnp.int32, (tb, tb), 0)
    c = lax.broadcasted_iota(jnp.int32, (tb, tb), 1)
    shift = HGRN_CHUNK.bit_length() - 1
    same = jnp.right_shift(r, shift) == jnp.right_shift(c, shift)
    return (same & ((c >= r) if upper else (c <= r))).astype(F32)


def _hgrn_gates(z, logits):
    lb = 1.0 / (1.0 + jnp.exp(logits[1:2, :] - logits[0:1, :]))
    e = jnp.exp(-jnp.abs(z))
    inv = 1.0 / (1.0 + e)
    sig = jnp.where(z >= 0, inv, e * inv)
    nsig = jnp.where(z >= 0, e * inv, inv)
    return lb, sig, nsig, lb + (1.0 - lb) * sig


def hgrn_fwd(z, lb_logits, out_norm, name):
    T = z.shape[0]
    W = z.shape[1] // 4
    H = W // HGRN_DK
    C = HGRN_CHUNK
    HB = _tile(H, HGRN_HEADS_PER_STEP)
    tb = _tile(T, 256)
    nch = tb // C

    def body(zq_ref, zf_ref, zv_ref, zg_ref, lbl_ref, on_ref, x_ref, o_ref, st_ref, s_scr, cum_scr, k_scr, v_scr, o_scr):
        @pl.when(pl.program_id(1) == 0)
        def _():
            s_scr[...] = jnp.zeros_like(s_scr)

        lb, sig, nsig, f = _hgrn_gates(zf_ref[...], lbl_ref[...])
        cum = jnp.dot(_chunk_tri(tb, False), jnp.log(f), precision=HI, preferred_element_type=F32)
        kk = (1.0 - lb) * nsig
        for hh in range(HB):
            cols = slice(hh * HGRN_DK, (hh + 1) * HGRN_DK)
            cum_scr[hh] = cum[:, cols]
            k_scr[hh] = kk[:, cols]
            v_scr[hh] = zv_ref[:, cols]
        row = lax.broadcasted_iota(jnp.int32, (C, HGRN_DK), 0)

        def chunk(c, carry):
            r0 = pl.multiple_of(c * C, C)
            rows = pl.ds(r0, C)
            last_row = pl.ds(r0 + C - 1, 1)
            heads = []
            for hh in range(HB):
                cols = slice(hh * HGRN_DK, (hh + 1) * HGRN_DK)
                q, cu = zq_ref[rows, cols], cum_scr[hh, rows, :]
                st = s_scr[hh]
                st_ref[c, hh] = st
                o = lax.dot_general((q * jnp.exp(cu)).astype(BF), st.astype(BF), NT, preferred_element_type=F32)
                last = cum_scr[hh, last_row, :]
                kg = (k_scr[hh, rows, :] * jnp.exp(last - cu)).astype(BF)
                s_scr[hh] = st * jnp.exp(last) + lax.dot_general(v_scr[hh, rows, :].astype(BF), kg, TN, preferred_element_type=F32)
                heads.append((hh, q, cu, o))
            for hh, q, cu, o in heads:
                qr = q.astype(BF).astype(F32)
                for s in range(C):
                    one = pl.ds(r0 + s, 1)
                    e = jnp.exp(jnp.minimum(cu - cum_scr[hh, one, :], 0.0))
                    col = jnp.sum(qr * (e * k_scr[hh, one, :]).astype(BF).astype(F32), axis=-1, keepdims=True)
                    col = jnp.where(row >= s, col, 0.0).astype(BF).astype(F32)
                    o = o + col * v_scr[hh, one, :].astype(BF).astype(F32)
                o_scr[hh, rows, :] = o
            return carry

        lax.fori_loop(0, nch, chunk, 0)
        for hh in range(HB):
            cols = slice(hh * HGRN_DK, (hh + 1) * HGRN_DK)
            o = o_scr[hh]
            o_ref[:, cols] = o
            oh, _ = _rms(o)
            g = zg_ref[:, cols]
            x_ref[:, cols] = (oh * on_ref[...] * (g * _sigmoid(g))).astype(BF)

    def zs(part):
        return pl.BlockSpec((tb, HB * HGRN_DK), lambda hd, i: (i, part * (H // HB) + hd))

    blk = pl.BlockSpec((tb, HB * HGRN_DK), lambda hd, i: (i, hd))
    wide = pltpu.VMEM((HB, tb, HGRN_DK), F32)
    return pl.pallas_call(
        body, name=name, grid=(H // HB, T // tb),
        in_specs=[zs(0), zs(1), zs(2), zs(3), pl.BlockSpec((2, HB * HGRN_DK), lambda hd, i: (0, hd)),
                  pl.BlockSpec((1, HGRN_DK), lambda hd, i: (0, 0))],
        out_specs=[blk, blk, pl.BlockSpec((nch, HB, HGRN_DK, HGRN_DK), lambda hd, i: (i, hd, 0, 0))],
        out_shape=[jax.ShapeDtypeStruct((T, W), BF), jax.ShapeDtypeStruct((T, W), F32),
                   jax.ShapeDtypeStruct((T // C, H, HGRN_DK, HGRN_DK), F32)],
        scratch_shapes=[pltpu.VMEM((HB, HGRN_DK, HGRN_DK), F32), wide, wide, wide, wide],
        compiler_params=_params(2),
    )(z, z, z, z, lb_logits, out_norm)


def hgrn_bwd(dx, z, o, states, lb_logits, out_norm, name):
    T = z.shape[0]
    W = z.shape[1] // 4
    H = W // HGRN_DK
    C = HGRN_CHUNK
    HB = _tile(H, HGRN_HEADS_PER_STEP)
    tb = _tile(T, 256)
    nch = tb // C
    nblk = T // tb

    def body(dx_ref, zq_ref, zf_ref, zv_ref, zg_ref, o_ref, st_ref, lbl_ref, on_ref,
             dq_ref, df_ref, dv_ref, dg_ref, dl_ref, don_ref,
             ds_scr, cum_scr, k_scr, v_scr, q_scr, do_scr, dq_scr, dk_scr, dv_scr, dcum_scr):
        hd, i = pl.program_id(0), pl.program_id(1)

        @pl.when(i == 0)
        def _():
            ds_scr[...] = jnp.zeros_like(ds_scr)
            dl_ref[...] = jnp.zeros_like(dl_ref)

        @pl.when((i == 0) & (hd == 0))
        def _():
            don_ref[...] = jnp.zeros_like(don_ref)

        lb, sig, nsig, f = _hgrn_gates(zf_ref[...], lbl_ref[...])
        cum = jnp.dot(_chunk_tri(tb, False), jnp.log(f), precision=HI, preferred_element_type=F32)
        kk = (1.0 - lb) * nsig
        w = on_ref[...]
        for hh in range(HB):
            cols = slice(hh * HGRN_DK, (hh + 1) * HGRN_DK)
            cum_scr[hh] = cum[:, cols]
            k_scr[hh] = kk[:, cols]
            v_scr[hh] = zv_ref[:, cols]
            q_scr[hh] = zq_ref[:, cols]
            oh, r = _rms(o_ref[:, cols])
            g = zg_ref[:, cols]
            sg = _sigmoid(g)
            dxv = dx_ref[:, cols].astype(F32)
            dg_ref[:, cols] = (dxv * (oh * w) * (sg * (1.0 + g * (1.0 - sg)))).astype(BF)
            don = dxv * (g * sg)
            don_ref[...] += _colsum(don * oh)
            do_scr[hh] = _norm_bwd(don, oh, r, w)
        row = lax.broadcasted_iota(jnp.int32, (C, HGRN_DK), 0)

        def chunk(cc, carry):
            c = nch - 1 - cc
            r0 = pl.multiple_of(c * C, C)
            rows = pl.ds(r0, C)
            last_row = pl.ds(r0 + C - 1, 1)
            heads = []
            for hh in range(HB):
                q, k, v, cu, do = q_scr[hh, rows, :], k_scr[hh, rows, :], v_scr[hh, rows, :], cum_scr[hh, rows, :], do_scr[hh, rows, :]
                st = st_ref[c, hh]
                dst = ds_scr[hh]
                last = cum_scr[hh, last_row, :]
                lam, gam, elast = jnp.exp(cu), jnp.exp(last - cu), jnp.exp(last)
                dob, dstb = do.astype(BF), dst.astype(BF)
                dq = jnp.dot(dob, st.astype(BF), preferred_element_type=F32) * lam
                dv = lax.dot_general((k * gam).astype(BF), dstb, NT, preferred_element_type=F32)
                dk = jnp.dot(v.astype(BF), dstb, preferred_element_type=F32) * gam
                dlast = elast * _colsum(dst * st) + _colsum(dk * k)
                ds_scr[hh] = dst * elast + lax.dot_general(dob, (q * lam).astype(BF), TN, preferred_element_type=F32)
                heads.append((hh, q, k, cu, do, dq, dk, dv, dlast))
            for hh, q, k, cu, do, dq, dk, dv, dlast in heads:
                for s in range(C):
                    one = pl.ds(r0 + s, 1)
                    e = jnp.where(row >= s, jnp.exp(jnp.minimum(cu - cum_scr[hh, one, :], 0.0)), 0.0)
                    ks = k_scr[hh, one, :]
                    da = jnp.sum(do * v_scr[hh, one, :], axis=-1, keepdims=True)
                    pq = q * e
                    a = jnp.sum(pq * ks, axis=-1, keepdims=True)
                    dq = dq + da * e * ks
                    dk = jnp.where(row == s, dk + _colsum(da * pq), dk)
                    dv = jnp.where(row == s, dv + _colsum(a * do), dv)
                dq_scr[hh, rows, :] = dq
                dk_scr[hh, rows, :] = dk
                dv_scr[hh, rows, :] = dv
                dcum_scr[hh, rows, :] = q * dq - k * dk + jnp.where(row == C - 1, dlast, 0.0)
            return carry

        lax.fori_loop(0, nch, chunk, 0)
        tri = _chunk_tri(tb, True)
        for hh in range(HB):
            cols = slice(hh * HGRN_DK, (hh + 1) * HGRN_DK)
            dlf = jnp.dot(tri, dcum_scr[hh], precision=HI, preferred_element_type=F32)
            dk = dk_scr[hh]
            lbh, sigh, nsigh, fh = lb[:, cols], sig[:, cols], nsig[:, cols], f[:, cols]
            common = (1.0 - lbh) * sigh * nsigh
            df_ref[:, cols] = (dlf * common / fh - dk * common).astype(BF)
            dq_ref[:, cols] = dq_scr[hh].astype(BF)
            dv_ref[:, cols] = dv_scr[hh].astype(BF)
            dl0 = _colsum(dlf * nsigh / fh - dk * nsigh) * lbh * (1.0 - lbh)
            dl_ref[:, cols] += jnp.where(lax.broadcasted_iota(jnp.int32, (2, HGRN_DK), 0) == 0, dl0, -dl0)

    def zs(part):
        return pl.BlockSpec((tb, HB * HGRN_DK), lambda hd, i: (nblk - 1 - i, part * (H // HB) + hd))

    blk = pl.BlockSpec((tb, HB * HGRN_DK), lambda hd, i: (nblk - 1 - i, hd))
    bft = jax.ShapeDtypeStruct((T, W), BF)
    scr = pltpu.VMEM((HB, tb, HGRN_DK), F32)
    return pl.pallas_call(
        body, name=name, grid=(H // HB, nblk),
        in_specs=[blk, zs(0), zs(1), zs(2), zs(3), blk,
                  pl.BlockSpec((nch, HB, HGRN_DK, HGRN_DK), lambda hd, i: (nblk - 1 - i, hd, 0, 0)),
                  pl.BlockSpec((2, HB * HGRN_DK), lambda hd, i: (0, hd)), pl.BlockSpec((1, HGRN_DK), lambda hd, i: (0, 0))],
        out_specs=[blk, blk, blk, blk, pl.BlockSpec((2, HB * HGRN_DK), lambda hd, i: (0, hd)),
                   pl.BlockSpec((1, HGRN_DK), lambda hd, i: (0, 0))],
        out_shape=[bft, bft, bft, bft, jax.ShapeDtypeStruct((2, W), F32), jax.ShapeDtypeStruct((1, HGRN_DK), F32)],
        scratch_shapes=[pltpu.VMEM((HB, HGRN_DK, HGRN_DK), F32), scr, scr, scr, scr, scr, scr, scr, scr, scr],
        compiler_params=_params(2),
    )(dx, z, z, z, z, o, states, lb_logits, out_norm)


def _log_sigmoid(x):
    return jnp.minimum(x, 0.0) - jnp.log(1.0 + jnp.exp(-jnp.abs(x)))


def _tri(n, upper):
    r = lax.broadcasted_iota(jnp.int32, (n, n), 0)
    c = lax.broadcasted_iota(jnp.int32, (n, n), 1)
    return ((r >= c) if upper else (c <= r)).astype(F32)


def fox_cum_fwd(kvf, fcol, b_row, name):
    T = kvf.shape[0]
    tb = _tile(T, 512)

    def body(x_ref, b_ref, ct_ref, carry_ref):
        @pl.when(pl.program_id(0) == 0)
        def _():
            carry_ref[...] = jnp.zeros_like(carry_ref)

        lf = _log_sigmoid(x_ref[...] + b_ref[...])
        cum = jnp.dot(_tri(tb, False), lf, precision=HI, preferred_element_type=F32) + carry_ref[...]
        carry_ref[...] += _colsum(lf)
        ct_ref[...] = cum.T

    return pl.pallas_call(
        body, name=name, grid=(T // tb,),
        in_specs=[pl.BlockSpec((tb, LANES), lambda i: (i, fcol)), pl.BlockSpec((1, LANES), lambda i: (0, 0))],
        out_specs=pl.BlockSpec((LANES, tb), lambda i: (0, i)), out_shape=jax.ShapeDtypeStruct((LANES, T), F32),
        scratch_shapes=[pltpu.VMEM((1, LANES), F32)], compiler_params=_params(1),
    )(kvf, b_row)


def fox_cum_bwd(dct, kvf, fcol, b_row, name):
    T = kvf.shape[0]
    tb = _tile(T, 512)
    nblk = T // tb

    def body(dc_ref, x_ref, b_ref, df_ref, db_ref, carry_ref):
        @pl.when(pl.program_id(0) == 0)
        def _():
            carry_ref[...] = jnp.zeros_like(carry_ref)
            db_ref[...] = jnp.zeros_like(db_ref)

        dc = dc_ref[...]
        dlf_t = jnp.dot(dc, _tri(tb, True), precision=HI, preferred_element_type=F32) + carry_ref[...]
        carry_ref[...] += jnp.sum(dc, axis=1, keepdims=True)
        x = x_ref[...] + b_ref[...]
        df = dlf_t.T * _sigmoid(-x)
        df_ref[...] = df.astype(BF)
        db_ref[...] += _colsum(df)

    return pl.pallas_call(
        body, name=name, grid=(nblk,),
        in_specs=[pl.BlockSpec((LANES, tb), lambda i: (0, nblk - 1 - i)),
                  pl.BlockSpec((tb, LANES), lambda i: (nblk - 1 - i, fcol)), pl.BlockSpec((1, LANES), lambda i: (0, 0))],
        out_specs=[pl.BlockSpec((tb, LANES), lambda i: (nblk - 1 - i, 0)), pl.BlockSpec((1, LANES), lambda i: (0, 0))],
        out_shape=[jax.ShapeDtypeStruct((T, LANES), BF), jax.ShapeDtypeStruct((1, LANES), F32)],
        scratch_shapes=[pltpu.VMEM((LANES, 1), F32)], compiler_params=_params(1),
    )(dct, kvf, b_row)


NAUG = 3


def fox_prep(kvf, ct, n_fox, tk):
    T = kvf.shape[0]
    W = n_fox * FOX_HEAD_DIM
    NP = n_fox // 2
    k = kvf[:, :W].astype(BF).reshape(T, NP, 2, FOX_HEAD_DIM)
    c = ct[:n_fox].T.reshape(T, NP, 2)
    hi = lax.reduce_precision(c, 8, 7)
    mid = lax.reduce_precision(c - hi, 8, 7)
    lo = c - hi - mid
    aug = jnp.stack([hi, mid, lo], axis=-1).astype(BF)
    pad = jnp.zeros((T, NP, FOX_HEAD_DIM - NAUG), BF)
    ka = jnp.concatenate([k[:, :, 0], aug[:, :, 0], pad], axis=-1).reshape(T, W)
    kb = jnp.concatenate([aug[:, :, 1], pad, k[:, :, 1]], axis=-1).reshape(T, W)
    v = kvf[:, W:2 * W].astype(BF)
    vt3 = v.reshape(T // tk, tk, NP, LANES).transpose(2, 0, 3, 1)
    return ka, kb, vt3


def fox_fwd(qg, ka, kb, vt3, name):
    T = qg.shape[0]
    W = qg.shape[1] // 2
    NP = W // LANES
    tq = tk = vt3.shape[3]
    scale = FOX_HEAD_DIM ** -0.5
    nk = T // tk
    HD = FOX_HEAD_DIM

    def body(q_ref, g_ref, ka_ref, kb_ref, vt_ref, x_ref, o_ref, lse_ref):
        i = pl.program_id(1)
        lane = lax.broadcasted_iota(jnp.int32, (tq, LANES), 1)
        q2 = q_ref[...] * scale
        qa = jnp.where(lane < HD, q2, jnp.where(lane < HD + NAUG, -1.0, 0.0))
        qb = jnp.where(lane >= HD, q2, jnp.where(lane < NAUG, -1.0, 0.0))
        qts = (qa.T.astype(BF), qb.T.astype(BF))
        krow = lax.broadcasted_iota(jnp.int32, (tk, tq), 0)
        qcol = lax.broadcasted_iota(jnp.int32, (tk, tq), 1)

        def step(j, carry, diag):
            rows = pl.ds(pl.multiple_of(j * tk, tk), tk)
            ks = (ka_ref[rows, :], kb_ref[rows, :])
            vt = vt_ref[j]
            sts = [jnp.dot(ks[a], qts[a], preferred_element_type=F32) for a in range(2)]
            pts, mls = [], []
            for a in range(2):
                m, l, _ = carry[a]
                st = sts[a]
                if diag:
                    st = jnp.where(krow + (j * tk - i * tq) <= qcol, st, -1e30)
                mn = jnp.maximum(m, jnp.max(st, axis=0, keepdims=True))
                alpha = jnp.exp(m - mn)
                pt = jnp.exp(st - mn)
                mls.append((mn, l * alpha + jnp.sum(pt, axis=0, keepdims=True), alpha))
                pts.append(pt.astype(BF))
            out = []
            for a in range(2):
                mn, l, alpha = mls[a]
                acc = carry[a][2] * alpha + jnp.dot(vt[a * HD:(a + 1) * HD, :], pts[a], preferred_element_type=F32)
                out.append((mn, l, acc))
            return tuple(out)

        init = (jnp.full((1, tq), -1e30, F32), jnp.zeros((1, tq), F32), jnp.zeros((HD, tq), F32))
        r = tq // tk
        carry = lax.fori_loop(0, i * r, lambda j, c: step(j, c, False), (init, init))
        for u in range(r):
            carry = step(i * r + u, carry, True)
        (ma, la, acca), (mb, lb, accb) = carry
        ot = jnp.concatenate([acca / la, accb / lb], axis=0)
        o = ot.T
        o_ref[...] = o
        lse_ref[0:1, :] = ma + jnp.log(la)
        lse_ref[1:2, :] = mb + jnp.log(lb)
        x_ref[...] = (o * _sigmoid(g_ref[...])).astype(BF)

    blk = pl.BlockSpec((tq, LANES), lambda hp, i: (i, hp))
    full = pl.BlockSpec((T, LANES), lambda hp, i: (0, hp))
    return pl.pallas_call(
        body, name=name, grid=(NP, T // tq),
        in_specs=[blk, pl.BlockSpec((tq, LANES), lambda hp, i: (i, NP + hp)), full, full,
                  pl.BlockSpec((None, nk, LANES, tk), lambda hp, i: (hp, 0, 0, 0))],
        out_specs=[blk, blk, pl.BlockSpec((None, None, 2, tq), lambda hp, i: (hp, i, 0, 0))],
        out_shape=[jax.ShapeDtypeStruct((T, W), BF), jax.ShapeDtypeStruct((T, W), F32),
                   jax.ShapeDtypeStruct((NP, T // tq, 2, tq), F32)],
        compiler_params=_params(2),
    )(qg, qg, ka, kb, vt3)


def fox_gate_bwd(dx, o, qg, name):
    T, W = o.shape
    tm = _tile(T, 512)

    def body(dx_ref, o_ref, g_ref, do_ref, dg_ref, ds_ref):
        dxv = dx_ref[...].astype(F32)
        sg = _sigmoid(g_ref[...])
        do = dxv * sg
        o = o_ref[...]
        do_ref[...] = do
        dg_ref[...] = (dxv * o * sg * (1.0 - sg)).astype(BF)
        head = jnp.right_shift(lax.broadcasted_iota(jnp.int32, (W, LANES), 0), FOX_HEAD_DIM.bit_length() - 1)
        sel = (head == lax.broadcasted_iota(jnp.int32, (W, LANES), 1)).astype(F32)
        ds_ref[...] = jnp.dot(do * o, sel, precision=HI, preferred_element_type=F32)

    tok = pl.BlockSpec((tm, W), lambda i: (i, 0))
    return pl.pallas_call(
        body, name=name, grid=(T // tm,),
        in_specs=[tok, tok, pl.BlockSpec((tm, W), lambda i: (i, 1))],
        out_specs=[tok, tok, pl.BlockSpec((tm, LANES), lambda i: (i, 0))],
        out_shape=[jax.ShapeDtypeStruct((T, W), F32), jax.ShapeDtypeStruct((T, W), BF), jax.ShapeDtypeStruct((T, LANES), F32)],
        compiler_params=_params(1),
    )(dx, o, qg)


def fox_bwd_prep(qg, kvf, do, tq):
    T = qg.shape[0]
    W = qg.shape[1] // 2
    NP = W // LANES
    scale = FOX_HEAD_DIM ** -0.5
    tr3 = lambda a: a.reshape(T // tq, tq, NP, LANES).transpose(2, 0, 3, 1)
    q = (qg[:, :W] * scale).astype(BF)
    dob = do.astype(BF)
    k = kvf[:, :W]
    return q, tr3(q), dob, tr3(dob), tr3((k * scale).astype(BF)), kvf[:, W:2 * W].astype(BF)


def fox_bwd(q, qt3, dob, dot3, kt3, v, ka, kb, lse4, dsum4, name):
    T, W = q.shape
    NP = W // LANES
    tq = tk = qt3.shape[3]
    nq = T // tq
    HD = FOX_HEAD_DIM

    def body(q_ref, qt_ref, do_ref, dot_ref, kt_ref, v_ref, ka_ref, kb_ref, lse_ref, dsum_ref,
             dqt_ref, dk_ref, dv_ref, dc_ref, dcq_ref, dk_scr, dv_scr, dcl_scr):
        j = pl.program_id(1)

        @pl.when(j == 0)
        def _():
            dqt_ref[...] = jnp.zeros_like(dqt_ref)
            dcq_ref[...] = jnp.zeros_like(dcq_ref)

        dk_scr[...] = jnp.zeros_like(dk_scr)
        dv_scr[...] = jnp.zeros_like(dv_scr)
        dcl_scr[...] = jnp.zeros_like(dcl_scr)
        lane = lax.broadcasted_iota(jnp.int32, (tk, LANES), 1)
        srow = lax.broadcasted_iota(jnp.int32, (LANES, tq), 0)
        lanes_of = (lane < HD, lane >= HD)
        rows_of = (srow < HD, srow >= HD)
        v2 = v_ref[...]
        kt2 = kt_ref[...]
        zero = jnp.zeros((), BF)
        vs = [jnp.where(lanes_of[a], v2, zero) for a in range(2)]
        kts = [jnp.where(rows_of[a], kt2, zero) for a in range(2)]
        kaug = (ka_ref[...], kb_ref[...])
        krow = lax.broadcasted_iota(jnp.int32, (tk, tq), 0)
        qcol = lax.broadcasted_iota(jnp.int32, (tk, tq), 1)
        neg1 = jnp.full((), -1.0, BF)

        def step(i, carry, diag):
            rows = pl.ds(pl.multiple_of(i * tq, tq), tq)
            qt2 = qt_ref[i]
            dot2 = dot_ref[i]
            q2 = q_ref[rows, :]
            do2 = do_ref[rows, :]
            qts = [jnp.where(srow < HD, qt2, jnp.where(srow < HD + NAUG, neg1, zero)),
                   jnp.where(srow >= HD, qt2, jnp.where(srow < NAUG, neg1, zero))]
            sts = [jnp.dot(kaug[a], qts[a], preferred_element_type=F32) for a in range(2)]
            dps = [jnp.dot(vs[a], dot2, preferred_element_type=F32) for a in range(2)]
            pbs, dsbs = [], []
            for a in range(2):
                pt = jnp.exp(sts[a] - lse_ref[i, a:a + 1, :])
                if diag:
                    pt = jnp.where(krow <= qcol, pt, 0.0)
                ds = pt * (dps[a] - dsum_ref[i, a:a + 1, :])
                dcq_ref[i, a:a + 1, :] += _colsum(ds)
                part = ds[:, 0:LANES]
                for u in range(1, tq // LANES):
                    part = part + ds[:, u * LANES:(u + 1) * LANES]
                dcl_scr[a] += part
                pbs.append(pt.astype(BF))
                dsbs.append(ds.astype(BF))
            qn = [jnp.where(lanes_of[a], q2, zero) for a in range(2)]
            don = [jnp.where(lanes_of[a], do2, zero) for a in range(2)]
            dv_scr[...] += (jnp.dot(pbs[0], don[0], preferred_element_type=F32) +
                            jnp.dot(pbs[1], don[1], preferred_element_type=F32))
            dk_scr[...] += (jnp.dot(dsbs[0], qn[0], preferred_element_type=F32) +
                            jnp.dot(dsbs[1], qn[1], preferred_element_type=F32))
            dqt_ref[i] += (jnp.dot(kts[0], dsbs[0], preferred_element_type=F32) +
                           jnp.dot(kts[1], dsbs[1], preferred_element_type=F32))
            return carry

        step(j, 0, True)
        lax.fori_loop(j + 1, nq, lambda i, c: step(i, c, False), 0)
        dk_ref[...] = dk_scr[...].astype(BF)
        dv_ref[...] = dv_scr[...].astype(BF)
        for a in range(2):
            dc_ref[a:a + 1, :] = -_colsum(dcl_scr[a].T)

    tile = pl.BlockSpec((tk, LANES), lambda hp, j: (j, hp))
    full = pl.BlockSpec((T, LANES), lambda hp, j: (0, hp))
    full3 = pl.BlockSpec((None, nq, LANES, tq), lambda hp, j: (hp, 0, 0, 0))
    rows4 = pl.BlockSpec((None, nq, 2, tq), lambda hp, j: (hp, 0, 0, 0))
    bft = jax.ShapeDtypeStruct((T, W), BF)
    r4 = jax.ShapeDtypeStruct((NP, nq, 2, tq), F32)
    return pl.pallas_call(
        body, name=name, grid=(NP, nq),
        in_specs=[full, full3, full, full3, pl.BlockSpec((None, None, LANES, tk), lambda hp, j: (hp, j, 0, 0)),
                  tile, tile, tile, rows4, rows4],
        out_specs=[full3, tile, tile, pl.BlockSpec((None, None, 2, tk), lambda hp, j: (hp, j, 0, 0)), rows4],
        out_shape=[jax.ShapeDtypeStruct((NP, nq, LANES, tq), F32), bft, bft, r4, r4],
        scratch_shapes=[pltpu.VMEM((tk, LANES), F32), pltpu.VMEM((tk, LANES), F32), pltpu.VMEM((2, tk, LANES), F32)],
        compiler_params=_params(2),
    )(q, qt3, dob, dot3, kt3, v, ka, kb, lse4, dsum4)


def loss_fwd_bwd(y, target, name):
    T, D = y.shape
    tm = _tile(T, 512)

    def body(y_ref, t_ref, dy_ref, l_ref):
        @pl.when(pl.program_id(0) == 0)
        def _():
            l_ref[...] = jnp.zeros_like(l_ref)

        d = y_ref[...] - t_ref[...]
        dy_ref[...] = d * (1.0 / D)
        l_ref[...] += 0.5 * jnp.sum(jnp.mean(d * d, axis=-1, keepdims=True), axis=0, keepdims=True)

    tok = pl.BlockSpec((tm, D), lambda i: (i, 0))
    return pl.pallas_call(
        body, name=name, grid=(T // tm,), in_specs=[tok, tok],
        out_specs=[tok, pl.BlockSpec((1, 1), lambda i: (0, 0))],
        out_shape=[jax.ShapeDtypeStruct((T, D), F32), jax.ShapeDtypeStruct((1, 1), F32)], compiler_params=_params(1),
    )(y, target)


def adamw(parts, w, m, v, layer, prev, name):
    L, R, C = w.shape
    tr = _tile(R, 256)
    c1 = 1.0 / (1.0 - ADAM_B1 ** ADAM_STEP)
    c2 = 1.0 / (1.0 - ADAM_B2 ** ADAM_STEP)

    def body(p_ref, w_ref, m_ref, v_ref, *rest):
        g_ref, d_ref, mo_ref, vo_ref = rest[-4:]
        g = p_ref[0].astype(F32)
        for d in range(1, N_DEV):
            g = g + p_ref[d].astype(F32)
        mn = ADAM_B1 * m_ref[...] + (1.0 - ADAM_B1) * g
        vn = ADAM_B2 * v_ref[...] + (1.0 - ADAM_B2) * (g * g)
        g_ref[...] = g
        mo_ref[...] = mn
        vo_ref[...] = vn
        d_ref[...] = -ADAM_LR * ((mn * c1) / (jnp.sqrt(vn * c2) + ADAM_EPS) + ADAM_WD * w_ref[...])

    blk = pl.BlockSpec((None, tr, C), lambda i: (layer, i, 0))
    shp = jax.ShapeDtypeStruct((L, R, C), F32)
    in_specs = [pl.BlockSpec((N_DEV, tr, C), lambda i: (0, i, 0)), blk, blk, blk]
    args = [parts, w, m, v]
    aliases = {}
    if prev is not None:
        in_specs += [pl.BlockSpec(memory_space=pl.ANY)] * 4
        args += list(prev)
        aliases = {4 + j: j for j in range(4)}
    return pl.pallas_call(
        body, name=name, grid=(R // tr,), in_specs=in_specs, out_specs=[blk, blk, blk, blk],
        out_shape=[shp, shp, shp, shp], input_output_aliases=aliases, compiler_params=_params(1),
    )(*args)


HBM_SPEC = pl.BlockSpec(memory_space=pltpu.HBM)
SEM_SPEC = pl.BlockSpec(memory_space=pltpu.SEMAPHORE)
EFFECT = pltpu.SideEffectType.DATAFLOW_SIDE_EFFECTING


def _mesh_pos():
    return lax.axis_index("x"), lax.axis_index("y"), lax.axis_index("c")


def _flip(v, bit):
    return v + bit - 2 * v * bit


def _peer(pos, delta):
    x, y, c = pos
    px, py, pc = _flip(x, (delta >> 2) & 1), _flip(y, (delta >> 1) & 1), _flip(c, delta & 1)
    return (px, py, pc), 4 * px + 2 * py + pc


def _me():
    x, y, c = _mesh_pos()
    return 4 * x + 2 * y + c


def _copies(src_refs, land_refs, whole, send, recv, incoming):
    pos = _mesh_pos()
    me = 4 * pos[0] + 2 * pos[1] + pos[2]
    out = []
    for k in range(len(src_refs)):
        for d in range(1, N_DEV):
            dev, idx = _peer(pos, d)
            j = k * (N_DEV - 1) + d - 1
            src = src_refs[k] if whole[k] else src_refs[k].at[idx]
            out.append(pltpu.make_async_remote_copy(
                src_ref=src, dst_ref=land_refs[k].at[idx if incoming else me], send_sem=send.at[j], recv_sem=recv.at[j],
                device_id=dev, device_id_type=pl.DeviceIdType.MESH))
    return out


def exchange_start(srcs, lands, whole, name):
    n = len(srcs)

    def body(*refs):
        for copy in _copies(refs[:n], refs[n:2 * n], whole, refs[2 * n], refs[2 * n + 1], False):
            copy.start()
        refs[-1][...] = jnp.zeros_like(refs[-1])

    sems = pltpu.SemaphoreType.DMA((n * (N_DEV - 1),))
    thru = [pltpu.HBM(a.shape, a.dtype) for a in list(srcs) + list(lands)]
    res = pl.pallas_call(
        body, name=name, in_specs=[HBM_SPEC] * (2 * n),
        out_specs=[SEM_SPEC, SEM_SPEC] + [HBM_SPEC] * (2 * n) + [pl.BlockSpec(memory_space=pltpu.VMEM)],
        out_shape=[sems, sems] + thru + [jax.ShapeDtypeStruct((8, LANES), F32)],
        input_output_aliases={j: 2 + j for j in range(2 * n)},
        compiler_params=pltpu.CompilerParams(has_side_effects=EFFECT),
    )(*[pltpu.with_memory_space_constraint(a, pltpu.HBM) for a in list(srcs) + list(lands)])
    return dict(send=res[0], recv=res[1], srcs=res[2:2 + n], lands=res[2 + n:2 + 2 * n], whole=whole, token=res[-1])


def exchange_wait(handle, after, name):
    n = len(handle["srcs"])
    whole = handle["whole"]

    def body(*refs):
        for copy in _copies(refs[:n], refs[n:2 * n], whole, refs[2 * n], refs[2 * n + 1], False):
            copy.wait_send()
        for copy in _copies(refs[:n], refs[n:2 * n], whole, refs[2 * n], refs[2 * n + 1], True):
            copy.wait_recv()

    bufs = list(handle["srcs"]) + list(handle["lands"])
    res = pl.pallas_call(
        body, name=name, in_specs=[HBM_SPEC] * (2 * n) + [SEM_SPEC, SEM_SPEC] + [pl.BlockSpec(memory_space=pl.ANY)] * len(after),
        out_specs=[HBM_SPEC] * (2 * n), out_shape=[pltpu.HBM(a.shape, a.dtype) for a in bufs],
        input_output_aliases={j: j for j in range(2 * n)},
        compiler_params=pltpu.CompilerParams(has_side_effects=EFFECT),
    )(*bufs, handle["send"], handle["recv"], *after)
    return list(res[n:])


def _landing(own, whole):
    me = _me()
    if not whole:
        own = lax.dynamic_index_in_dim(own, me, 0, keepdims=False)
    buf = lax.empty((N_DEV,) + own.shape, own.dtype)
    return lax.dynamic_update_slice(buf, own[None], (me,) + (0,) * own.ndim)


def _row(v):
    return v.reshape(1, -1)


def _pad_lanes(v, n):
    return jnp.pad(v, ((0, 0), (0, n - v.shape[1])))


GATHER_GROUPS = (
    ("ffn1_0", (("ffn1_w_in", 0), ("ffn1_w_out", 0))),
    ("hgrn", (("hgrn_w_in", 0), ("hgrn_w_out", 0))),
    ("rest_0", (("ffn2_w_in", 0), ("ffn2_w_out", 0), ("ple_w_gate", 0), ("ple_w_proj", 0), ("fox_w_kvf", 0))),
    ("ffn1_1", (("ffn1_w_in", 1), ("ffn1_w_out", 1))),
    ("fox", (("fox_w_qg", 0), ("fox_w_out", 0))),
    ("rest_1", (("ffn2_w_in", 1), ("ffn2_w_out", 1), ("ple_w_gate", 1), ("ple_w_proj", 1))),
)


def kernel(x, p, ffn1_norm_pre, ffn1_w_in, ffn1_w_out, ffn1_norm_post, mix_norm_pre, mix_norm_post, ffn2_norm_pre, ffn2_w_in, ffn2_w_out, ffn2_norm_post, hgrn_w_in, hgrn_lb_logits, hgrn_out_norm, hgrn_w_out, kv_norm, fox_w_kvf, fox_b_f, fox_w_qg, fox_w_out, ple_norm_pre, ple_w_gate, ple_w_proj, ple_norm_post, loss_target, m_ffn1_norm_pre, m_ffn1_w_in, m_ffn1_w_out, m_ffn1_norm_post, m_mix_norm_pre, m_mix_norm_post, m_ffn2_norm_pre, m_ffn2_w_in, m_ffn2_w_out, m_ffn2_norm_post, m_hgrn_w_in, m_hgrn_lb_logits, m_hgrn_out_norm, m_hgrn_w_out, m_kv_norm, m_fox_w_kvf, m_fox_b_f, m_fox_w_qg, m_fox_w_out, m_ple_norm_pre, m_ple_w_gate, m_ple_w_proj, m_ple_norm_post, v_ffn1_norm_pre, v_ffn1_w_in, v_ffn1_w_out, v_ffn1_norm_post, v_mix_norm_pre, v_mix_norm_post, v_ffn2_norm_pre, v_ffn2_w_in, v_ffn2_w_out, v_ffn2_norm_post, v_hgrn_w_in, v_hgrn_lb_logits, v_hgrn_out_norm, v_hgrn_w_out, v_kv_norm, v_fox_w_kvf, v_fox_b_f, v_fox_w_qg, v_fox_w_out, v_ple_norm_pre, v_ple_w_gate, v_ple_w_proj, v_ple_norm_post):
    weights = dict(ffn1_norm_pre=ffn1_norm_pre, ffn1_w_in=ffn1_w_in, ffn1_w_out=ffn1_w_out, ffn1_norm_post=ffn1_norm_post, mix_norm_pre=mix_norm_pre, mix_norm_post=mix_norm_post, ffn2_norm_pre=ffn2_norm_pre, ffn2_w_in=ffn2_w_in, ffn2_w_out=ffn2_w_out, ffn2_norm_post=ffn2_norm_post, hgrn_w_in=hgrn_w_in, hgrn_lb_logits=hgrn_lb_logits, hgrn_out_norm=hgrn_out_norm, hgrn_w_out=hgrn_w_out, kv_norm=kv_norm, fox_w_kvf=fox_w_kvf, fox_b_f=fox_b_f, fox_w_qg=fox_w_qg, fox_w_out=fox_w_out, ple_norm_pre=ple_norm_pre, ple_w_gate=ple_w_gate, ple_w_proj=ple_w_proj, ple_norm_post=ple_norm_post)
    mom1 = dict(ffn1_norm_pre=m_ffn1_norm_pre, ffn1_w_in=m_ffn1_w_in, ffn1_w_out=m_ffn1_w_out, ffn1_norm_post=m_ffn1_norm_post, mix_norm_pre=m_mix_norm_pre, mix_norm_post=m_mix_norm_post, ffn2_norm_pre=m_ffn2_norm_pre, ffn2_w_in=m_ffn2_w_in, ffn2_w_out=m_ffn2_w_out, ffn2_norm_post=m_ffn2_norm_post, hgrn_w_in=m_hgrn_w_in, hgrn_lb_logits=m_hgrn_lb_logits, hgrn_out_norm=m_hgrn_out_norm, hgrn_w_out=m_hgrn_w_out, kv_norm=m_kv_norm, fox_w_kvf=m_fox_w_kvf, fox_b_f=m_fox_b_f, fox_w_qg=m_fox_w_qg, fox_w_out=m_fox_w_out, ple_norm_pre=m_ple_norm_pre, ple_w_gate=m_ple_w_gate, ple_w_proj=m_ple_w_proj, ple_norm_post=m_ple_norm_post)
    mom2 = dict(ffn1_norm_pre=v_ffn1_norm_pre, ffn1_w_in=v_ffn1_w_in, ffn1_w_out=v_ffn1_w_out, ffn1_norm_post=v_ffn1_norm_post, mix_norm_pre=v_mix_norm_pre, mix_norm_post=v_mix_norm_post, ffn2_norm_pre=v_ffn2_norm_pre, ffn2_w_in=v_ffn2_w_in, ffn2_w_out=v_ffn2_w_out, ffn2_norm_post=v_ffn2_norm_post, hgrn_w_in=v_hgrn_w_in, hgrn_lb_logits=v_hgrn_lb_logits, hgrn_out_norm=v_hgrn_out_norm, hgrn_w_out=v_hgrn_w_out, kv_norm=v_kv_norm, fox_w_kvf=v_fox_w_kvf, fox_b_f=v_fox_b_f, fox_w_qg=v_fox_w_qg, fox_w_out=v_fox_w_out, ple_norm_pre=v_ple_norm_pre, ple_w_gate=v_ple_w_gate, ple_w_proj=v_ple_w_proj, ple_norm_post=v_ple_norm_post)
    names = list(weights)
    big = ["ffn1_w_in", "ffn1_w_out", "ffn2_w_in", "ffn2_w_out", "hgrn_w_in", "hgrn_w_out", "fox_w_kvf", "fox_w_qg",
           "fox_w_out", "ple_w_gate", "ple_w_proj"]
    small_names = [n for n in names if n not in big]

    T, D = x.shape[1], x.shape[2]
    depth = p.shape[0]
    h0 = x.reshape(T, D)
    target = loss_target.reshape(T, D)
    p3 = p.reshape(depth, T, p.shape[3])
    n_fox = fox_b_f.shape[0]
    fox_w = n_fox * FOX_HEAD_DIM
    fcol = 2 * fox_w // LANES
    b_row = _pad_lanes(_row(fox_b_f), LANES)

    tok = jnp.zeros((), F32)
    handles = {}
    for gname, keys in GATHER_GROUPS:
        shards = []
        for n, l in keys:
            w = weights[n]
            shards.append(((w[l] if w.ndim == 3 else w) + tok).astype(BF))
        handles[gname] = exchange_start(shards, [_landing(s, True) for s in shards], [True] * len(keys), f"gather_start_{gname}")
        tok = handles[gname]["token"][0, 0]
    W = {}

    def arrive(gname, after):
        lands = exchange_wait(handles[gname], after, f"gather_wait_{gname}")
        W.update(dict(zip(dict(GATHER_GROUPS)[gname], lands)))

    def w_rows(n, l):
        return W[n, l].reshape(-1, D)

    norm = lambda name, i: weights[name][i:i + 1]

    saved = []
    h = h0
    kvf = ct = kvf_w = None
    tq = _tile(T, 512)
    for i in range(depth):
        s = {}
        if i == 0:
            arrive("ffn1_0", [handles[GATHER_GROUPS[-1][0]]["token"]])
        for k in (1, 2):
            if k == 2:
                s["h_a"] = h
                if i == 0:
                    arrive("hgrn", [h])
                    z, xn = norm_mm(h, norm("mix_norm_pre", i), W["hgrn_w_in", 0], "hgrn_in")
                    xm, o, states = hgrn_fwd(z, hgrn_lb_logits, hgrn_out_norm, "hgrn_scan")
                    s.update(z=z, o=o, states=states)
                    wmix = w_rows("hgrn_w_out", 0)
                else:
                    arrive("fox", [h])
                    qg, xn = norm_mm(h, norm("mix_norm_pre", i), W["fox_w_qg", 0], "fox_qg")
                    ka, kb, vt3 = fox_prep(kvf, ct, n_fox, tq)
                    xm, o, lse = fox_fwd(qg, ka, kb, vt3, "fox_attn")
                    s.update(qg=qg, o=o, lse=lse, ka=ka, kb=kb)
                    wmix = w_rows("fox_w_out", 0)
                h, ym = mm_norm_res(xm, wmix, norm("mix_norm_post", i), h, 1.0, f"mix_out_{i}", kb=_tile(xm.shape[1], 512))
                s.update(xm=xm, ym=ym, xn_mix=xn)
                arrive(f"rest_{i}", [h])
            gate, up, a, xn = norm_mm_swiglu(h, norm(f"ffn{k}_norm_pre", i), W[f"ffn{k}_w_in", i], f"ffn{k}_in_{i}")
            hn, y = mm_norm_res(a, w_rows(f"ffn{k}_w_out", i), norm(f"ffn{k}_norm_post", i), h, 0.5, f"ffn{k}_out_{i}")
            s[f"ffn{k}"] = (h, gate, up, a, y, xn)
            h = hn
        s["h_c"] = h
        ple_proj = W["ple_w_proj", i].transpose(1, 0, 2).reshape(p.shape[3], D)
        h, pgate, pp = ple_fwd(h, norm("ple_norm_pre", i), w_rows("ple_w_gate", i), p3, ple_proj, norm("ple_norm_post", i),
                               i, f"ple_{i}")
        s.update(pgate=pgate, pp=pp)
        saved.append(s)
        if i == 0:
            kvf_nat = W["fox_w_kvf", 0].transpose(1, 0, 2).reshape(D, -1)
            kvf_cols = kvf_nat.shape[1]
            kvf_w = _pad_lanes(kvf_nat, 2 * fox_w + LANES)
            kvf, xn_kv = norm_mm(h, _row(kv_norm), kvf_w, "fox_kvf", tn=kvf_w.shape[1])
            ct = fox_cum_fwd(kvf, fcol, b_row, "fox_cum")
            h_kv = h
            arrive("ffn1_1", [h])

    dh, loss_part = loss_fwd_bwd(h, target, "loss")
    loss = lax.psum(loss_part[0, 0], ("x", "y", "c"))

    gsmall = {n: [None] * weights[n].shape[0] if weights[n].ndim == 2 else None for n in small_names}
    sent = []

    def send(gname, keys, srcs, whole):
        lands = [_landing(a, w) for a, w in zip(srcs, whole)]
        hd = exchange_start(srcs, lands, whole, f"scatter_start_{gname}")
        sent.append((gname, keys, hd))
        return hd["token"][0:1, 0:1]

    def send_grads(gname, grads):
        return send(gname, list(grads), list(grads.values()), [False] * len(grads))

    for i in reversed(range(depth)):
        s = saved[i]
        grads = {}
        dh, du, dpp, xn, dgpre, dgpost = ple_bwd(dh, s["h_c"], s["pgate"], s["pp"], norm("ple_norm_pre", i),
                                                 w_rows("ple_w_gate", i), norm("ple_norm_post", i), f"ple_bwd_{i}")
        gsmall["ple_norm_pre"][i], gsmall["ple_norm_post"][i] = dgpre, dgpost
        grads["ple_w_gate", i] = mm_tn(xn, du, f"ple_dgate_{i}", xb=_tile(D, 512)).reshape(N_DEV, -1, D)
        dproj = mm_tn(p3, dpp, f"ple_dproj_{i}", x_layer=i, yb=D)[0]
        grads["ple_w_proj", i] = dproj.reshape(dproj.shape[0], N_DEV, -1).transpose(1, 0, 2)
        for k in (2, 1):
            hin, gate, up, a, y, xn = s[f"ffn{k}"]
            ffn_cs = gate.shape[2]
            dy, dgpost, dz = nbwd_mm_nt(dh, y, norm(f"ffn{k}_norm_post", i), w_rows(f"ffn{k}_w_out", i), 0.5, ffn_cs,
                                        f"ffn{k}_bwd_out_{i}", gate=gate, up=up)
            dz = dz.reshape(-1, T, ffn_cs)
            grads[f"ffn{k}_w_out", i] = mm_tn(a, dy, f"ffn{k}_dwout_{i}").reshape(N_DEV, -1, D)
            grads[f"ffn{k}_w_in", i] = mm_tn(xn, dz, f"ffn{k}_dwin_{i}")
            tokv = send_grads(f"ffn{k}_{i}", grads)
            grads = {}
            dh, dgpre = mm_nt_nbwd(dz, W[f"ffn{k}_w_in", i], hin, norm(f"ffn{k}_norm_pre", i) + tokv, dh, f"ffn{k}_bwd_in_{i}")
            gsmall[f"ffn{k}_norm_pre"][i], gsmall[f"ffn{k}_norm_post"][i] = dgpre, dgpost
            if k == 2:
                nm = "hgrn" if i == 0 else "fox"
                wmix = w_rows(f"{nm}_w_out", 0)
                dy, dgpost, dxm = nbwd_mm_nt(dh, s["ym"], norm("mix_norm_post", i), wmix, 1.0, _tile(wmix.shape[0], 512),
                                             f"{nm}_bwd_out")
                grads[f"{nm}_w_out", 0] = mm_tn(s["xm"], dy, f"{nm}_dwout", xb=_tile(wmix.shape[0], 512)).reshape(N_DEV, -1, D)
                gsmall["mix_norm_post"][i] = dgpost
                if i == 0:
                    dq, df, dv, dg, dlog, don = hgrn_bwd(dxm, s["z"], s["o"], s["states"], hgrn_lb_logits, hgrn_out_norm,
                                                         "hgrn_scan_bwd")
                    gsmall["hgrn_lb_logits"] = [dlog[0:1], dlog[1:2]]
                    gsmall["hgrn_out_norm"] = [don]
                    dzm = jnp.concatenate([dq, df, dv, dg], axis=1)
                    nmin = "hgrn_w_in"
                else:
                    do, dg, dsum = fox_gate_bwd(dxm, s["o"], s["qg"], "fox_gate_bwd")
                    dsum4 = dsum[:, :n_fox].T.reshape(n_fox // 2, 2, T // tq, tq).transpose(0, 2, 1, 3)
                    dqt, dk_sh, dv_sh, dc4, dcq4 = fox_bwd(*fox_bwd_prep(s["qg"], kvf, do, tq), s["ka"], s["kb"], s["lse"], dsum4,
                                                           "fox_attn_bwd")
                    dq = dqt.transpose(1, 3, 0, 2).reshape(T, fox_w)
                    dzm = jnp.concatenate([dq.astype(BF), dg], axis=1)
                    nmin = "fox_w_qg"
                wmin = W[nmin, 0]
                grads[nmin, 0] = mm_tn(s["xn_mix"], dzm, f"{nm}_dwin", yb=wmin.shape[2])
                tokv = send_grads(f"mix_{i}", grads)
                grads = {}
                dh, dgpre = mm_nt_nbwd(dzm, wmin, s["h_a"], norm("mix_norm_pre", i) + tokv, dh, f"{nm}_bwd_in", tn=wmin.shape[2])
                gsmall["mix_norm_pre"][i] = dgpre
        if i == 1:
            dct = (dc4 + dcq4).transpose(0, 2, 1, 3).reshape(n_fox, T)
            dct = jnp.pad(dct, ((0, LANES - n_fox), (0, 0)))
            dflog, db = fox_cum_bwd(dct, kvf, fcol, b_row, "fox_cum_bwd")
            gsmall["fox_b_f"] = db[:, :n_fox]
            dkvf = jnp.concatenate([dk_sh, dv_sh, dflog], axis=1)
            dwk = mm_tn(xn_kv, dkvf, "fox_dwkvf", yb=kvf_w.shape[1])[0][:, :kvf_cols]
            tokv = send_grads("kvf", {("fox_w_kvf", 0): dwk.reshape(D, N_DEV, -1).transpose(1, 0, 2)})
            dh, dgkv = mm_nt_nbwd(dkvf, kvf_w, h_kv, _row(kv_norm) + tokv, dh, "fox_kvf_bwd", tn=kvf_w.shape[1])
            gsmall["kv_norm"] = dgkv
    grad_x = dh.reshape(x.shape)

    def small_rows(n):
        g = gsmall[n]
        rows = g if isinstance(g, list) else [g]
        return [_pad_lanes(r, D) for r in rows]

    counts = {n: len(small_rows(n)) for n in small_names}
    packed = jnp.concatenate([r for n in small_names for r in small_rows(n)], axis=0)
    n_rows = packed.shape[0]
    packed = jnp.pad(packed, ((0, -n_rows % 8), (0, 0)))
    send("small", ["small"], [packed], [True])

    res = {}
    after = [dh]
    small_parts = None
    for gname, keys, hd in sent:
        lands = exchange_wait(hd, after, f"scatter_wait_{gname}")
        after = []
        for key, parts in zip(keys, lands):
            if key == "small":
                small_parts = parts
                continue
            n, l = key
            w = weights[n]
            as3 = (lambda a: a.reshape((1,) + a.shape)) if w.ndim == 2 else (lambda a: a)
            res[n] = adamw(parts, as3(w), as3(mom1[n]), as3(mom2[n]), l, res.get(n), f"adamw_{n}_{l}")
            after.append(res[n][1])

    def pack(d):
        rows = []
        for n in small_names:
            a = d[n]
            rows.append(_pad_lanes(a.reshape(-1, a.shape[-1]), D))
        a = jnp.concatenate(rows, axis=0)
        return jnp.pad(a, ((0, -n_rows % 8), (0, 0)))[None]

    sm = adamw(small_parts, pack(weights), pack(mom1), pack(mom2), 0, None, "adamw_small")
    off = 0
    for n in small_names:
        w = weights[n]
        res[n] = [a[0, off:off + counts[n], :w.shape[-1]].reshape(w.shape) for a in sm]
        off += counts[n]

    out = [loss, grad_x]
    for j in range(4):
        out += [res[n][j].reshape(weights[n].shape) for n in names]
    return tuple(out)
```

```python
import functools

import jax
import jax.numpy as jnp
from jax import lax
from jax.experimental import pallas as pl
from jax.experimental.pallas import tpu as pltpu

F32 = jnp.float32
BF = jnp.bfloat16
NORM_EPS = 1e-6
N_DEV = 8
HGRN_DK = 128
HGRN_CHUNK = 16
HGRN_HEADS_PER_STEP = 4
HALF = HGRN_CHUNK // 2
FOX_HEAD_DIM = 64
LANES = 128
ADAM_LR, ADAM_B1, ADAM_B2, ADAM_EPS, ADAM_WD, ADAM_STEP = 0.001, 0.9, 0.999, 1e-08, 0.01, 10
VMEM_LIMIT = 56 * 1024 * 1024
HI = lax.Precision.HIGHEST
NT = (((1,), (1,)), ((), ()))
TN = (((0,), (0,)), ((), ()))


def _params(n_axes):
    return pltpu.CompilerParams(dimension_semantics=("arbitrary",) * n_axes, vmem_limit_bytes=VMEM_LIMIT)


def _tile(n, want):
    t = min(n, want)
    while n % t:
        t //= 2
    return t


def _sigmoid(x):
    return 1.0 / (1.0 + jnp.exp(-x))


def _rms(x):
    r = lax.rsqrt(jnp.mean(x * x, axis=-1, keepdims=True) + NORM_EPS)
    return x * r, r


def _norm_bwd(dy, xhat, r, g):
    dxh = dy * g
    return r * (dxh - xhat * jnp.mean(dxh * xhat, axis=-1, keepdims=True))


def _colsum(x):
    return jnp.sum(x, axis=0, keepdims=True)


def _w_spec(w, blk):
    if w.ndim == 3:
        return lambda off: pl.BlockSpec((None, w.shape[1], w.shape[2]), lambda i, j: (j + off, 0, 0))
    return lambda off: pl.BlockSpec((w.shape[0], blk), lambda i, j: (0, j + off))


def norm_mm_swiglu(h, g, w3, name):
    T, D = h.shape
    nb, _, cs = w3.shape
    nh = nb // 2
    tm = _tile(T, 1024)

    def body(h_ref, g_ref, wg_ref, wu_ref, gate_ref, up_ref, a_ref, xn_ref):
        @pl.when(pl.program_id(1) == 0)
        def _():
            xh, _ = _rms(h_ref[...])
            xn_ref[...] = (xh * g_ref[...]).astype(BF)

        xn = xn_ref[...]
        gt = jnp.dot(xn, wg_ref[...], preferred_element_type=F32)
        up = jnp.dot(xn, wu_ref[...], preferred_element_type=F32)
        gate_ref[...] = gt.astype(BF)
        up_ref[...] = up.astype(BF)
        a_ref[...] = (gt * _sigmoid(gt) * up).astype(BF)

    ws = _w_spec(w3, cs)
    blk = pl.BlockSpec((None, tm, cs), lambda i, j: (j, i, 0))
    shp = jax.ShapeDtypeStruct((nh, T, cs), BF)
    return pl.pallas_call(
        body, name=name, grid=(T // tm, nh),
        in_specs=[pl.BlockSpec((tm, D), lambda i, j: (i, 0)), pl.BlockSpec((1, D), lambda i, j: (0, 0)), ws(0), ws(nh)],
        out_specs=[blk, blk, blk, pl.BlockSpec((tm, D), lambda i, j: (i, 0))],
        out_shape=[shp, shp, shp, jax.ShapeDtypeStruct((T, D), BF)], compiler_params=_params(2),
    )(h, g, w3, w3)


def norm_mm(h, g, w, name, tn=None):
    T, D = h.shape
    if w.ndim == 3:
        nb, cs = w.shape[0], w.shape[2]
    else:
        cs = tn
        nb = w.shape[1] // cs
    tm = _tile(T, 1024)

    def body(h_ref, g_ref, w_ref, z_ref, xn_ref):
        @pl.when(pl.program_id(1) == 0)
        def _():
            xh, _ = _rms(h_ref[...])
            xn_ref[...] = (xh * g_ref[...]).astype(BF)

        z_ref[...] = jnp.dot(xn_ref[...], w_ref[...], preferred_element_type=F32)

    return pl.pallas_call(
        body, name=name, grid=(T // tm, nb),
        in_specs=[pl.BlockSpec((tm, D), lambda i, j: (i, 0)), pl.BlockSpec((1, D), lambda i, j: (0, 0)),
                  _w_spec(w, cs)(0)],
        out_specs=[pl.BlockSpec((tm, cs), lambda i, j: (i, j)), pl.BlockSpec((tm, D), lambda i, j: (i, 0))],
        out_shape=[jax.ShapeDtypeStruct((T, nb * cs), F32), jax.ShapeDtypeStruct((T, D), BF)], compiler_params=_params(2),
    )(h, g, w)


def _x_spec(x, tm, kb):
    if x.ndim == 3:
        return pl.BlockSpec((None, tm, x.shape[2]), lambda i, j: (j, i, 0))
    return pl.BlockSpec((tm, kb), lambda i, j: (i, j))


def mm_norm_res(x, w2, g, h, coef, name, kb=None):
    T, D = h.shape
    if x.ndim == 3:
        nb, kb = x.shape[0], x.shape[2]
    else:
        nb = x.shape[1] // kb
    tm = _tile(T, 1024)

    def body(x_ref, w_ref, h_ref, g_ref, hn_ref, y_ref, acc_ref):
        b = pl.program_id(1)

        @pl.when(b == 0)
        def _():
            acc_ref[...] = jnp.zeros_like(acc_ref)

        acc_ref[...] += jnp.dot(x_ref[...], w_ref[...], preferred_element_type=F32)

        @pl.when(b == nb - 1)
        def _():
            y = acc_ref[...]
            y_ref[...] = y
            yh, _ = _rms(y)
            hn_ref[...] = h_ref[...] + coef * (yh * g_ref[...])

    tok = pl.BlockSpec((tm, D), lambda i, j: (i, 0))
    shp = jax.ShapeDtypeStruct((T, D), F32)
    return pl.pallas_call(
        body, name=name, grid=(T // tm, nb),
        in_specs=[_x_spec(x, tm, kb), pl.BlockSpec((kb, D), lambda i, j: (j, 0)), tok,
                  pl.BlockSpec((1, D), lambda i, j: (0, 0))],
        out_specs=[tok, tok], out_shape=[shp, shp],
        scratch_shapes=[pltpu.VMEM((tm, D), F32)], compiler_params=_params(2),
    )(x, w2, h, g)


def nbwd_mm_nt(dout, y, g, w2, coef, kb, name, gate=None, up=None):
    T, D = dout.shape
    nb = w2.shape[0] // kb
    swiglu = gate is not None
    tm = _tile(T, 512)

    def body(*refs):
        if swiglu:
            dout_ref, y_ref, g_ref, w_ref, gate_ref, up_ref, dy_ref, dg_ref, da_ref, dys_ref = refs
        else:
            dout_ref, y_ref, g_ref, w_ref, dy_ref, dg_ref, da_ref, dys_ref = refs
        i, b = pl.program_id(0), pl.program_id(1)

        @pl.when((i == 0) & (b == 0))
        def _():
            dg_ref[...] = jnp.zeros_like(dg_ref)

        @pl.when(b == 0)
        def _():
            yh, r = _rms(y_ref[...])
            dyn = coef * dout_ref[...]
            dg_ref[...] += _colsum(dyn * yh)
            dy = _norm_bwd(dyn, yh, r, g_ref[...]).astype(BF)
            dys_ref[...] = dy
            dy_ref[...] = dy

        da = lax.dot_general(dys_ref[...], w_ref[...], NT, preferred_element_type=F32)
        if swiglu:
            gt = gate_ref[...].astype(F32)
            u = up_ref[...].astype(F32)
            sg = _sigmoid(gt)
            da_ref[0] = (da * u * (sg * (1.0 + gt * (1.0 - sg)))).astype(BF)
            da_ref[1] = (da * (gt * sg)).astype(BF)
        else:
            da_ref[...] = da.astype(BF)

    tok = pl.BlockSpec((tm, D), lambda i, j: (i, 0))
    vec = pl.BlockSpec((1, D), lambda i, j: (0, 0))
    in_specs = [tok, tok, vec, pl.BlockSpec((kb, D), lambda i, j: (j, 0))]
    args = [dout, y, g, w2]
    if swiglu:
        blk = pl.BlockSpec((None, tm, kb), lambda i, j: (j, i, 0))
        in_specs += [blk, blk]
        args += [gate, up]
        da_spec = pl.BlockSpec((2, None, tm, kb), lambda i, j: (0, j, i, 0))
        da_shape = jax.ShapeDtypeStruct((2, nb, T, kb), BF)
    else:
        da_spec = pl.BlockSpec((tm, kb), lambda i, j: (i, j))
        da_shape = jax.ShapeDtypeStruct((T, nb * kb), BF)
    return pl.pallas_call(
        body, name=name, grid=(T // tm, nb), in_specs=in_specs,
        out_specs=[tok, vec, da_spec],
        out_shape=[jax.ShapeDtypeStruct((T, D), BF), jax.ShapeDtypeStruct((1, D), F32), da_shape],
        scratch_shapes=[pltpu.VMEM((tm, D), BF)], compiler_params=_params(2),
    )(*args)


def mm_nt_nbwd(dz, w, h, g, dout, name, tn=None):
    T, D = h.shape
    if w.ndim == 3:
        nb, cs = w.shape[0], w.shape[2]
    else:
        cs = tn
        nb = w.shape[1] // cs
    tm = _tile(T, 1024)

    def body(dz_ref, w_ref, h_ref, g_ref, dout_ref, dh_ref, dg_ref, acc_ref):
        i, b = pl.program_id(0), pl.program_id(1)

        @pl.when((i == 0) & (b == 0))
        def _():
            dg_ref[...] = jnp.zeros_like(dg_ref)

        @pl.when(b == 0)
        def _():
            acc_ref[...] = jnp.zeros_like(acc_ref)

        acc_ref[...] += lax.dot_general(dz_ref[...], w_ref[...], NT, preferred_element_type=F32)

        @pl.when(b == nb - 1)
        def _():
            xh, r = _rms(h_ref[...])
            gg = g_ref[...]
            dxn = acc_ref[...]
            dg_ref[...] += _colsum(dxn * xh)
            dh_ref[...] = dout_ref[...] + _norm_bwd(dxn, xh, r, gg)

    tok = pl.BlockSpec((tm, D), lambda i, j: (i, 0))
    vec = pl.BlockSpec((1, D), lambda i, j: (0, 0))
    return pl.pallas_call(
        body, name=name, grid=(T // tm, nb),
        in_specs=[_x_spec(dz, tm, cs), _w_spec(w, cs)(0), tok, vec, tok],
        out_specs=[tok, vec],
        out_shape=[jax.ShapeDtypeStruct((T, D), F32), jax.ShapeDtypeStruct((1, D), F32)],
        scratch_shapes=[pltpu.VMEM((tm, D), F32)], compiler_params=_params(2),
    )(dz, w, h, g, dout)


def mm_tn(x, y, name, xb=None, yb=None, x_layer=None):
    T = y.shape[-2]
    wide = (yb if yb is not None else y.shape[-1]) > 1024
    tt = _tile(T, 1024 if wide else 2048)
    x_split = (x.ndim == 3 and x_layer is None) or xb is not None
    if x_layer is not None:
        xs = pl.BlockSpec((None, tt, x.shape[2]), lambda b, t: (x_layer, t, 0))
        kdim = x.shape[2]
    elif x.ndim == 3:
        xs = pl.BlockSpec((None, tt, x.shape[2]), lambda b, t: (b, t, 0))
        nb, kdim = x.shape[0], x.shape[2]
    elif xb is not None:
        xs = pl.BlockSpec((tt, xb), lambda b, t: (t, b))
        nb, kdim = x.shape[1] // xb, xb
    else:
        xs = pl.BlockSpec((tt, x.shape[1]), lambda b, t: (t, 0))
        kdim = x.shape[1]
    if x_split:
        ys = pl.BlockSpec((tt, y.shape[1]), lambda b, t: (t, 0))
        ndim = y.shape[1]
        out_spec = pl.BlockSpec((kdim, ndim), lambda b, t: (b, 0))
        out_shape = jax.ShapeDtypeStruct((nb * kdim, ndim), BF)
    else:
        if y.ndim == 3:
            ys = pl.BlockSpec((None, tt, y.shape[2]), lambda b, t: (b, t, 0))
            nb, ndim = y.shape[0], y.shape[2]
        else:
            ys = pl.BlockSpec((tt, yb), lambda b, t: (t, b))
            nb, ndim = y.shape[1] // yb, yb
        out_spec = pl.BlockSpec((None, kdim, ndim), lambda b, t: (b, 0, 0))
        out_shape = jax.ShapeDtypeStruct((nb, kdim, ndim), BF)
    nt = T // tt

    def body(x_ref, y_ref, o_ref, acc_ref):
        t = pl.program_id(1)

        @pl.when(t == 0)
        def _():
            acc_ref[...] = jnp.zeros_like(acc_ref)

        acc_ref[...] += lax.dot_general(x_ref[...].astype(BF), y_ref[...].astype(BF), TN, preferred_element_type=F32)

        @pl.when(t == nt - 1)
        def _():
            o_ref[...] = acc_ref[...].astype(BF)

    return pl.pallas_call(
        body, name=name, grid=(nb, nt), in_specs=[xs, ys], out_specs=out_spec, out_shape=out_shape,
        scratch_shapes=[pltpu.VMEM((kdim, ndim), F32)], compiler_params=_params(2),
    )(x, y)


def ple_fwd(h, gpre, wg, p3, wp, gpost, layer, name):
    T, D = h.shape
    pd = p3.shape[2]
    tm = _tile(T, 512)

    def body(h_ref, gpre_ref, wg_ref, p_ref, wp_ref, gpost_ref, hn_ref, gate_ref, pp_ref):
        x = h_ref[...]
        xh, _ = _rms(x)
        u = jnp.dot((xh * gpre_ref[...]).astype(BF), wg_ref[...], preferred_element_type=F32)
        gate = _sigmoid(u)
        pp = jnp.dot(p_ref[...].astype(BF), wp_ref[...], preferred_element_type=F32)
        yh, _ = _rms(gate * pp)
        hn_ref[...] = x + yh * gpost_ref[...]
        gate_ref[...] = gate.astype(BF)
        pp_ref[...] = pp.astype(BF)

    tok = pl.BlockSpec((tm, D), lambda i: (i, 0))
    vec = pl.BlockSpec((1, D), lambda i: (0, 0))
    return pl.pallas_call(
        body, name=name, grid=(T // tm,),
        in_specs=[tok, vec, pl.BlockSpec((D, D), lambda i: (0, 0)),
                  pl.BlockSpec((None, tm, pd), lambda i: (layer, i, 0)),
                  pl.BlockSpec((pd, D), lambda i: (0, 0)), vec],
        out_specs=[tok, tok, tok],
        out_shape=[jax.ShapeDtypeStruct((T, D), F32), jax.ShapeDtypeStruct((T, D), BF), jax.ShapeDtypeStruct((T, D), BF)],
        compiler_params=_params(1),
    )(h, gpre, wg, p3, wp, gpost)


def ple_bwd(dout, h, gate, pp, gpre, wg, gpost, name):
    T, D = h.shape
    tm = _tile(T, 512)

    def body(dout_ref, h_ref, gate_ref, pp_ref, gpre_ref, wg_ref, gpost_ref, dh_ref, du_ref, dpp_ref, xn_ref, dgpre_ref, dgpost_ref):
        @pl.when(pl.program_id(0) == 0)
        def _():
            dgpre_ref[...] = jnp.zeros_like(dgpre_ref)
            dgpost_ref[...] = jnp.zeros_like(dgpost_ref)

        dout = dout_ref[...]
        gate = gate_ref[...].astype(F32)
        pp = pp_ref[...].astype(F32)
        yh, ry = _rms(gate * pp)
        dgpost_ref[...] += _colsum(dout * yh)
        dy = _norm_bwd(dout, yh, ry, gpost_ref[...])
        dpp_ref[...] = (dy * gate).astype(BF)
        du = (dy * pp * gate * (1.0 - gate)).astype(BF)
        du_ref[...] = du
        dxn = lax.dot_general(du, wg_ref[...], NT, preferred_element_type=F32)
        xh, r = _rms(h_ref[...])
        gp = gpre_ref[...]
        dgpre_ref[...] += _colsum(dxn * xh)
        dh_ref[...] = dout + _norm_bwd(dxn, xh, r, gp)
        xn_ref[...] = (xh * gp).astype(BF)

    tok = pl.BlockSpec((tm, D), lambda i: (i, 0))
    vec = pl.BlockSpec((1, D), lambda i: (0, 0))
    bft = jax.ShapeDtypeStruct((T, D), BF)
    v32 = jax.ShapeDtypeStruct((1, D), F32)
    return pl.pallas_call(
        body, name=name, grid=(T // tm,),
        in_specs=[tok, tok, tok, tok, vec, pl.BlockSpec((D, D), lambda i: (0, 0)), vec],
        out_specs=[tok, tok, tok, tok, vec, vec],
        out_shape=[jax.ShapeDtypeStruct((T, D), F32), bft, bft, bft, v32, v32],
        compiler_params=_params(1),
    )(dout, h, gate, pp, gpre, wg, gpost)


def _chunk_tri(tb, upper):
    r = lax.broadcasted_iota(jnp.int32, (tb, tb), 0)
    c = lax.broadcasted_iota(jnp.int32, (tb, tb), 1)
    shift = HGRN_CHUNK.bit_length() - 1
    same = jnp.right_shift(r, shift) == jnp.right_shift(c, shift)
    return (same & ((c >= r) if upper else (c <= r))).astype(F32)


def _hgrn_gates(z, logits):
    lb = 1.0 / (1.0 + jnp.exp(logits[1:2, :] - logits[0:1, :]))
    e = jnp.exp(-jnp.abs(z))
    inv = 1.0 / (1.0 + e)
    sig = jnp.where(z >= 0, inv, e * inv)
    nsig = jnp.where(z >= 0, e * inv, inv)
    return lb, sig, nsig, lb + (1.0 - lb) * sig


def hgrn_fwd(z, lb_logits, out_norm, name):
    T = z.shape[0]
    W = z.shape[1] // 4
    H = W // HGRN_DK
    C = HGRN_CHUNK
    HB = _tile(H, HGRN_HEADS_PER_STEP)
    tb = _tile(T, 256)
    nch = tb // C

    def body(zq_ref, zf_ref, zv_ref, zg_ref, lbl_ref, on_ref, x_ref, o_ref, st_ref, s_scr, cum_scr, k_scr, v_scr, o_scr):
        @pl.when(pl.program_id(1) == 0)
        def _():
            s_scr[...] = jnp.zeros_like(s_scr)

        lb, sig, nsig, f = _hgrn_gates(zf_ref[...], lbl_ref[...])
        cum = jnp.dot(_chunk_tri(tb, False), jnp.log(f), precision=HI, preferred_element_type=F32)
        kk = (1.0 - lb) * nsig
        for hh in range(HB):
            cols = slice(hh * HGRN_DK, (hh + 1) * HGRN_DK)
            cum_scr[hh] = cum[:, cols]
            k_scr[hh] = kk[:, cols]
            v_scr[hh] = zv_ref[:, cols]
        row = lax.broadcasted_iota(jnp.int32, (C, HGRN_DK), 0)

        def chunk(c, carry):
            r0 = pl.multiple_of(c * C, C)
            rows = pl.ds(r0, C)
            last_row = pl.ds(r0 + C - 1, 1)
            heads = []
            for hh in range(HB):
                cols = slice(hh * HGRN_DK, (hh + 1) * HGRN_DK)
                q, cu = zq_ref[rows, cols], cum_scr[hh, rows, :]
                st = s_scr[hh]
                st_ref[c, hh] = st
                o = lax.dot_general((q * jnp.exp(cu)).astype(BF), st.astype(BF), NT, preferred_element_type=F32)
                last = cum_scr[hh, last_row, :]
                kg = (k_scr[hh, rows, :] * jnp.exp(last - cu)).astype(BF)
                s_scr[hh] = st * jnp.exp(last) + lax.dot_general(v_scr[hh, rows, :].astype(BF), kg, TN, preferred_element_type=F32)
                heads.append((hh, q, cu, o))
            for hh, q, cu, o in heads:
                qr = q.astype(BF).astype(F32)
                low = jnp.zeros((C - HALF, HGRN_DK), F32)
                for s in range(C):
                    one = pl.ds(r0 + s, 1)
                    sl = slice(0 if s < HALF else HALF, C)
                    e = jnp.exp(jnp.minimum(cu[sl] - cum_scr[hh, one, :], 0.0))
                    col = jnp.sum(qr[sl] * (e * k_scr[hh, one, :]).astype(BF).astype(F32), axis=-1, keepdims=True)
                    col = jnp.where(row[sl] >= s, col, 0.0).astype(BF).astype(F32)
                    term = col * v_scr[hh, one, :].astype(BF).astype(F32)
                    if s < HALF:
                        o = o + term
                    else:
                        low = low + term
                o_scr[hh, rows, :] = o
                o_scr[hh, pl.ds(r0 + HALF, C - HALF), :] += low
            return carry

        lax.fori_loop(0, nch, chunk, 0)
        for hh in range(HB):
            cols = slice(hh * HGRN_DK, (hh + 1) * HGRN_DK)
            o = o_scr[hh]
            o_ref[:, cols] = o
            oh, _ = _rms(o)
            g = zg_ref[:, cols]
            x_ref[:, cols] = (oh * on_ref[...] * (g * _sigmoid(g))).astype(BF)

    def zs(part):
        return pl.BlockSpec((tb, HB * HGRN_DK), lambda hd, i: (i, part * (H // HB) + hd))

    blk = pl.BlockSpec((tb, HB * HGRN_DK), lambda hd, i: (i, hd))
    wide = pltpu.VMEM((HB, tb, HGRN_DK), F32)
    return pl.pallas_call(
        body, name=name, grid=(H // HB, T // tb),
        in_specs=[zs(0), zs(1), zs(2), zs(3), pl.BlockSpec((2, HB * HGRN_DK), lambda hd, i: (0, hd)),
                  pl.BlockSpec((1, HGRN_DK), lambda hd, i: (0, 0))],
        out_specs=[blk, blk, pl.BlockSpec((nch, HB, HGRN_DK, HGRN_DK), lambda hd, i: (i, hd, 0, 0))],
        out_shape=[jax.ShapeDtypeStruct((T, W), BF), jax.ShapeDtypeStruct((T, W), F32),
                   jax.ShapeDtypeStruct((T // C, H, HGRN_DK, HGRN_DK), F32)],
        scratch_shapes=[pltpu.VMEM((HB, HGRN_DK, HGRN_DK), F32), wide, wide, wide, wide],
        compiler_params=_params(2),
    )(z, z, z, z, lb_logits, out_norm)


def hgrn_bwd(dx, z, o, states, lb_logits, out_norm, name):
    T = z.shape[0]
    W = z.shape[1] // 4
    H = W // HGRN_DK
    C = HGRN_CHUNK
    HB = _tile(H, HGRN_HEADS_PER_STEP)
    tb = _tile(T, 256)
    nch = tb // C
    nblk = T // tb

    def body(dx_ref, zq_ref, zf_ref, zv_ref, zg_ref, o_ref, st_ref, lbl_ref, on_ref,
             dq_ref, df_ref, dv_ref, dg_ref, dl_ref, don_ref,
             ds_scr, cum_scr, k_scr, v_scr, q_scr, do_scr, dq_scr, dk_scr, dv_scr, dcum_scr):
        hd, i = pl.program_id(0), pl.program_id(1)

        @pl.when(i == 0)
        def _():
            ds_scr[...] = jnp.zeros_like(ds_scr)
            dl_ref[...] = jnp.zeros_like(dl_ref)

        @pl.when((i == 0) & (hd == 0))
        def _():
            don_ref[...] = jnp.zeros_like(don_ref)

        lb, sig, nsig, f = _hgrn_gates(zf_ref[...], lbl_ref[...])
        cum = jnp.dot(_chunk_tri(tb, False), jnp.log(f), precision=HI, preferred_element_type=F32)
        kk = (1.0 - lb) * nsig
        w = on_ref[...]
        for hh in range(HB):
            cols = slice(hh * HGRN_DK, (hh + 1) * HGRN_DK)
            cum_scr[hh] = cum[:, cols]
            k_scr[hh] = kk[:, cols]
            v_scr[hh] = zv_ref[:, cols]
            q_scr[hh] = zq_ref[:, cols]
            oh, r = _rms(o_ref[:, cols])
            g = zg_ref[:, cols]
            sg = _sigmoid(g)
            dxv = dx_ref[:, cols].astype(F32)
            dg_ref[:, cols] = (dxv * (oh * w) * (sg * (1.0 + g * (1.0 - sg)))).astype(BF)
            don = dxv * (g * sg)
            don_ref[...] += _colsum(don * oh)
            do_scr[hh] = _norm_bwd(don, oh, r, w)
        row = lax.broadcasted_iota(jnp.int32, (C, HGRN_DK), 0)

        def chunk(cc, carry):
            c = nch - 1 - cc
            r0 = pl.multiple_of(c * C, C)
            rows = pl.ds(r0, C)
            last_row = pl.ds(r0 + C - 1, 1)
            heads = []
            for hh in range(HB):
                q, k, v, cu, do = q_scr[hh, rows, :], k_scr[hh, rows, :], v_scr[hh, rows, :], cum_scr[hh, rows, :], do_scr[hh, rows, :]
                st = st_ref[c, hh]
                dst = ds_scr[hh]
                last = cum_scr[hh, last_row, :]
                lam, gam, elast = jnp.exp(cu), jnp.exp(last - cu), jnp.exp(last)
                dob, dstb = do.astype(BF), dst.astype(BF)
                dq = jnp.dot(dob, st.astype(BF), preferred_element_type=F32) * lam
                dv = lax.dot_general((k * gam).astype(BF), dstb, NT, preferred_element_type=F32)
                dk = jnp.dot(v.astype(BF), dstb, preferred_element_type=F32) * gam
                dlast = elast * _colsum(dst * st) + _colsum(dk * k)
                ds_scr[hh] = dst * elast + lax.dot_general(dob, (q * lam).astype(BF), TN, preferred_element_type=F32)
                heads.append((hh, q, k, cu, do, dq, dk, dv, dlast))
            for hh, q, k, cu, do, dq, dk, dv, dlast in heads:
                for first in (True, False):
                    sl = slice(0 if first else HALF, C)
                    qs, cus, dos, rws = q[sl], cu[sl], do[sl], row[sl]
                    dqs, dks, dvs = dq[sl], dk[sl], dv[sl]
                    for s in (range(HALF) if first else range(HALF, C)):
                        one = pl.ds(r0 + s, 1)
                        e = jnp.where(rws >= s, jnp.exp(jnp.minimum(cus - cum_scr[hh, one, :], 0.0)), 0.0)
                        ks = k_scr[hh, one, :]
                        da = jnp.sum(dos * v_scr[hh, one, :], axis=-1, keepdims=True)
                        pq = qs * e
                        a = jnp.sum(pq * ks, axis=-1, keepdims=True)
                        dqs = dqs + da * e * ks
                        dks = jnp.where(rws == s, dks + _colsum(da * pq), dks)
                        dvs = jnp.where(rws == s, dvs + _colsum(a * dos), dvs)
                    if first:
                        dq, dk, dv = dqs, dks, dvs
                        top = pl.ds(r0, HALF)
                        dq_scr[hh, top, :] = dq[:HALF]
                        dk_scr[hh, top, :] = dk[:HALF]
                        dv_scr[hh, top, :] = dv[:HALF]
                        dcum_scr[hh, top, :] = q[:HALF] * dq[:HALF] - k[:HALF] * dk[:HALF]
                    else:
                        low = pl.ds(r0 + HALF, C - HALF)
                        dq_scr[hh, low, :] = dqs
                        dk_scr[hh, low, :] = dks
                        dv_scr[hh, low, :] = dvs
                        dcum_scr[hh, low, :] = qs * dqs - k[sl] * dks + jnp.where(rws == C - 1, dlast, 0.0)
            return carry

        lax.fori_loop(0, nch, chunk, 0)
        tri = _chunk_tri(tb, True)
        for hh in range(HB):
            cols = slice(hh * HGRN_DK, (hh + 1) * HGRN_DK)
            dlf = jnp.dot(tri, dcum_scr[hh], precision=HI, preferred_element_type=F32)
            dk = dk_scr[hh]
            lbh, sigh, nsigh, fh = lb[:, cols], sig[:, cols], nsig[:, cols], f[:, cols]
            common = (1.0 - lbh) * sigh * nsigh
            df_ref[:, cols] = (dlf * common / fh - dk * common).astype(BF)
            dq_ref[:, cols] = dq_scr[hh].astype(BF)
            dv_ref[:, cols] = dv_scr[hh].astype(BF)
            dl0 = _colsum(dlf * nsigh / fh - dk * nsigh) * lbh * (1.0 - lbh)
            dl_ref[:, cols] += jnp.where(lax.broadcasted_iota(jnp.int32, (2, HGRN_DK), 0) == 0, dl0, -dl0)

    def zs(part):
        return pl.BlockSpec((tb, HB * HGRN_DK), lambda hd, i: (nblk - 1 - i, part * (H // HB) + hd))

    blk = pl.BlockSpec((tb, HB * HGRN_DK), lambda hd, i: (nblk - 1 - i, hd))
    bft = jax.ShapeDtypeStruct((T, W), BF)
    scr = pltpu.VMEM((HB, tb, HGRN_DK), F32)
    return pl.pallas_call(
        body, name=name, grid=(H // HB, nblk),
        in_specs=[blk, zs(0), zs(1), zs(2), zs(3), blk,
                  pl.BlockSpec((nch, HB, HGRN_DK, HGRN_DK), lambda hd, i: (nblk - 1 - i, hd, 0, 0)),
                  pl.BlockSpec((2, HB * HGRN_DK), lambda hd, i: (0, hd)), pl.BlockSpec((1, HGRN_DK), lambda hd, i: (0, 0))],
        out_specs=[blk, blk, blk, blk, pl.BlockSpec((2, HB * HGRN_DK), lambda hd, i: (0, hd)),
                   pl.BlockSpec((1, HGRN_DK), lambda hd, i: (0, 0))],
        out_shape=[bft, bft, bft, bft, jax.ShapeDtypeStruct((2, W), F32), jax.ShapeDtypeStruct((1, HGRN_DK), F32)],
        scratch_shapes=[pltpu.VMEM((HB, HGRN_DK, HGRN_DK), F32), scr, scr, scr, scr, scr, scr, scr, scr, scr],
        compiler_params=_params(2),
    )(dx, z, z, z, z, o, states, lb_logits, out_norm)


def _log_sigmoid(x):
    return jnp.minimum(x, 0.0) - jnp.log(1.0 + jnp.exp(-jnp.abs(x)))


def _tri(n, upper):
    r = lax.broadcasted_iota(jnp.int32, (n, n), 0)
    c = lax.broadcasted_iota(jnp.int32, (n, n), 1)
    return ((r >= c) if upper else (c <= r)).astype(F32)


def fox_cum_fwd(kvf, fcol, b_row, name):
    T = kvf.shape[0]
    tb = _tile(T, 512)

    def body(x_ref, b_ref, ct_ref, carry_ref):
        @pl.when(pl.program_id(0) == 0)
        def _():
            carry_ref[...] = jnp.zeros_like(carry_ref)

        lf = _log_sigmoid(x_ref[...] + b_ref[...])
        cum = jnp.dot(_tri(tb, False), lf, precision=HI, preferred_element_type=F32) + carry_ref[...]
        carry_ref[...] += _colsum(lf)
        ct_ref[...] = cum.T

    return pl.pallas_call(
        body, name=name, grid=(T // tb,),
        in_specs=[pl.BlockSpec((tb, LANES), lambda i: (i, fcol)), pl.BlockSpec((1, LANES), lambda i: (0, 0))],
        out_specs=pl.BlockSpec((LANES, tb), lambda i: (0, i)), out_shape=jax.ShapeDtypeStruct((LANES, T), F32),
        scratch_shapes=[pltpu.VMEM((1, LANES), F32)], compiler_params=_params(1),
    )(kvf, b_row)


def fox_cum_bwd(dct, kvf, fcol, b_row, name):
    T = kvf.shape[0]
    tb = _tile(T, 512)
    nblk = T // tb

    def body(dc_ref, x_ref, b_ref, df_ref, db_ref, carry_ref):
        @pl.when(pl.program_id(0) == 0)
        def _():
            carry_ref[...] = jnp.zeros_like(carry_ref)
            db_ref[...] = jnp.zeros_like(db_ref)

        dc = dc_ref[...]
        dlf_t = jnp.dot(dc, _tri(tb, True), precision=HI, preferred_element_type=F32) + carry_ref[...]
        carry_ref[...] += jnp.sum(dc, axis=1, keepdims=True)
        x = x_ref[...] + b_ref[...]
        df = dlf_t.T * _sigmoid(-x)
        df_ref[...] = df.astype(BF)
        db_ref[...] += _colsum(df)

    return pl.pallas_call(
        body, name=name, grid=(nblk,),
        in_specs=[pl.BlockSpec((LANES, tb), lambda i: (0, nblk - 1 - i)),
                  pl.BlockSpec((tb, LANES), lambda i: (nblk - 1 - i, fcol)), pl.BlockSpec((1, LANES), lambda i: (0, 0))],
        out_specs=[pl.BlockSpec((tb, LANES), lambda i: (nblk - 1 - i, 0)), pl.BlockSpec((1, LANES), lambda i: (0, 0))],
        out_shape=[jax.ShapeDtypeStruct((T, LANES), BF), jax.ShapeDtypeStruct((1, LANES), F32)],
        scratch_shapes=[pltpu.VMEM((LANES, 1), F32)], compiler_params=_params(1),
    )(dct, kvf, b_row)


NAUG = 3


def fox_prep(kvf, ct, n_fox, tk):
    T = kvf.shape[0]
    W = n_fox * FOX_HEAD_DIM
    NP = n_fox // 2
    k = kvf[:, :W].astype(BF).reshape(T, NP, 2, FOX_HEAD_DIM)
    c = ct[:n_fox].T.reshape(T, NP, 2)
    hi = lax.reduce_precision(c, 8, 7)
    mid = lax.reduce_precision(c - hi, 8, 7)
    lo = c - hi - mid
    aug = jnp.stack([hi, mid, lo], axis=-1).astype(BF)
    pad = jnp.zeros((T, NP, FOX_HEAD_DIM - NAUG), BF)
    ka = jnp.concatenate([k[:, :, 0], aug[:, :, 0], pad], axis=-1).reshape(T, W)
    kb = jnp.concatenate([aug[:, :, 1], pad, k[:, :, 1]], axis=-1).reshape(T, W)
    v = kvf[:, W:2 * W].astype(BF)
    vt3 = v.reshape(T // tk, tk, NP, LANES).transpose(2, 0, 3, 1)
    return ka, kb, vt3


def fox_fwd(qg, ka, kb, vt3, name):
    T = qg.shape[0]
    W = qg.shape[1] // 2
    NP = W // LANES
    tq = tk = vt3.shape[3]
    scale = FOX_HEAD_DIM ** -0.5
    nk = T // tk
    HD = FOX_HEAD_DIM

    def body(q_ref, g_ref, ka_ref, kb_ref, vt_ref, x_ref, o_ref, lse_ref):
        i = pl.program_id(1)
        lane = lax.broadcasted_iota(jnp.int32, (tq, LANES), 1)
        q2 = q_ref[...] * scale
        qa = jnp.where(lane < HD, q2, jnp.where(lane < HD + NAUG, -1.0, 0.0))
        qb = jnp.where(lane >= HD, q2, jnp.where(lane < NAUG, -1.0, 0.0))
        qts = (qa.T.astype(BF), qb.T.astype(BF))
        krow = lax.broadcasted_iota(jnp.int32, (tk, tq), 0)
        qcol = lax.broadcasted_iota(jnp.int32, (tk, tq), 1)

        def step(j, carry, diag):
            rows = pl.ds(pl.multiple_of(j * tk, tk), tk)
            ks = (ka_ref[rows, :], kb_ref[rows, :])
            vt = vt_ref[j]
            sts = [jnp.dot(ks[a], qts[a], preferred_element_type=F32) for a in range(2)]
            pts, mls = [], []
            for a in range(2):
                m, l, _ = carry[a]
                st = sts[a]
                if diag:
                    st = jnp.where(krow + (j * tk - i * tq) <= qcol, st, -1e30)
                mn = jnp.maximum(m, jnp.max(st, axis=0, keepdims=True))
                alpha = jnp.exp(m - mn)
                pt = jnp.exp(st - mn)
                mls.append((mn, l * alpha + jnp.sum(pt, axis=0, keepdims=True), alpha))
                pts.append(pt.astype(BF))
            out = []
            for a in range(2):
                mn, l, alpha = mls[a]
                acc = carry[a][2] * alpha + jnp.dot(vt[a * HD:(a + 1) * HD, :], pts[a], preferred_element_type=F32)
                out.append((mn, l, acc))
            return tuple(out)

        init = (jnp.full((1, tq), -1e30, F32), jnp.zeros((1, tq), F32), jnp.zeros((HD, tq), F32))
        r = tq // tk
        carry = lax.fori_loop(0, i * r, lambda j, c: step(j, c, False), (init, init))
        for u in range(r):
            carry = step(i * r + u, carry, True)
        (ma, la, acca), (mb, lb, accb) = carry
        ot = jnp.concatenate([acca / la, accb / lb], axis=0)
        o = ot.T
        o_ref[...] = o
        lse_ref[0:1, :] = ma + jnp.log(la)
        lse_ref[1:2, :] = mb + jnp.log(lb)
        x_ref[...] = (o * _sigmoid(g_ref[...])).astype(BF)

    blk = pl.BlockSpec((tq, LANES), lambda hp, i: (i, hp))
    full = pl.BlockSpec((T, LANES), lambda hp, i: (0, hp))
    return pl.pallas_call(
        body, name=name, grid=(NP, T // tq),
        in_specs=[blk, pl.BlockSpec((tq, LANES), lambda hp, i: (i, NP + hp)), full, full,
                  pl.BlockSpec((None, nk, LANES, tk), lambda hp, i: (hp, 0, 0, 0))],
        out_specs=[blk, blk, pl.BlockSpec((None, None, 2, tq), lambda hp, i: (hp, i, 0, 0))],
        out_shape=[jax.ShapeDtypeStruct((T, W), BF), jax.ShapeDtypeStruct((T, W), F32),
                   jax.ShapeDtypeStruct((NP, T // tq, 2, tq), F32)],
        compiler_params=_params(2),
    )(qg, qg, ka, kb, vt3)


def fox_gate_bwd(dx, o, qg, name):
    T, W = o.shape
    tm = _tile(T, 512)

    def body(dx_ref, o_ref, g_ref, do_ref, dg_ref, ds_ref):
        dxv = dx_ref[...].astype(F32)
        sg = _sigmoid(g_ref[...])
        do = dxv * sg
        o = o_ref[...]
        do_ref[...] = do
        dg_ref[...] = (dxv * o * sg * (1.0 - sg)).astype(BF)
        head = jnp.right_shift(lax.broadcasted_iota(jnp.int32, (W, LANES), 0), FOX_HEAD_DIM.bit_length() - 1)
        sel = (head == lax.broadcasted_iota(jnp.int32, (W, LANES), 1)).astype(F32)
        ds_ref[...] = jnp.dot(do * o, sel, precision=HI, preferred_element_type=F32)

    tok = pl.BlockSpec((tm, W), lambda i: (i, 0))
    return pl.pallas_call(
        body, name=name, grid=(T // tm,),
        in_specs=[tok, tok, pl.BlockSpec((tm, W), lambda i: (i, 1))],
        out_specs=[tok, tok, pl.BlockSpec((tm, LANES), lambda i: (i, 0))],
        out_shape=[jax.ShapeDtypeStruct((T, W), F32), jax.ShapeDtypeStruct((T, W), BF), jax.ShapeDtypeStruct((T, LANES), F32)],
        compiler_params=_params(1),
    )(dx, o, qg)


def fox_bwd_prep(qg, kvf, do, tq):
    T = qg.shape[0]
    W = qg.shape[1] // 2
    NP = W // LANES
    scale = FOX_HEAD_DIM ** -0.5
    tr3 = lambda a: a.reshape(T // tq, tq, NP, LANES).transpose(2, 0, 3, 1)
    q = (qg[:, :W] * scale).astype(BF)
    dob = do.astype(BF)
    k = kvf[:, :W]
    return q, tr3(q), dob, tr3(dob), tr3((k * scale).astype(BF)), kvf[:, W:2 * W].astype(BF)


def fox_bwd(q, qt3, dob, dot3, kt3, v, ka, kb, lse4, dsum4, name):
    T, W = q.shape
    NP = W // LANES
    tq = tk = qt3.shape[3]
    nq = T // tq
    HD = FOX_HEAD_DIM

    def body(q_ref, qt_ref, do_ref, dot_ref, kt_ref, v_ref, ka_ref, kb_ref, lse_ref, dsum_ref,
             dqt_ref, dk_ref, dv_ref, dc_ref, dcq_ref, dk_scr, dv_scr, dcl_scr):
        j = pl.program_id(1)

        @pl.when(j == 0)
        def _():
            dqt_ref[...] = jnp.zeros_like(dqt_ref)
            dcq_ref[...] = jnp.zeros_like(dcq_ref)

        dk_scr[...] = jnp.zeros_like(dk_scr)
        dv_scr[...] = jnp.zeros_like(dv_scr)
        dcl_scr[...] = jnp.zeros_like(dcl_scr)
        lane = lax.broadcasted_iota(jnp.int32, (tk, LANES), 1)
        srow = lax.broadcasted_iota(jnp.int32, (LANES, tq), 0)
        lanes_of = (lane < HD, lane >= HD)
        rows_of = (srow < HD, srow >= HD)
        v2 = v_ref[...]
        kt2 = kt_ref[...]
        zero = jnp.zeros((), BF)
        vs = [jnp.where(lanes_of[a], v2, zero) for a in range(2)]
        kts = [jnp.where(rows_of[a], kt2, zero) for a in range(2)]
        kaug = (ka_ref[...], kb_ref[...])
        krow = lax.broadcasted_iota(jnp.int32, (tk, tq), 0)
        qcol = lax.broadcasted_iota(jnp.int32, (tk, tq), 1)
        neg1 = jnp.full((), -1.0, BF)

        def step(i, carry, diag):
            rows = pl.ds(pl.multiple_of(i * tq, tq), tq)
            qt2 = qt_ref[i]
            dot2 = dot_ref[i]
            q2 = q_ref[rows, :]
            do2 = do_ref[rows, :]
            qts = [jnp.where(srow < HD, qt2, jnp.where(srow < HD + NAUG, neg1, zero)),
                   jnp.where(srow >= HD, qt2, jnp.where(srow < NAUG, neg1, zero))]
            sts = [jnp.dot(kaug[a], qts[a], preferred_element_type=F32) for a in range(2)]
            dps = [jnp.dot(vs[a], dot2, preferred_element_type=F32) for a in range(2)]
            pbs, dsbs = [], []
            for a in range(2):
                pt = jnp.exp(sts[a] - lse_ref[i, a:a + 1, :])
                if diag:
                    pt = jnp.where(krow <= qcol, pt, 0.0)
                ds = pt * (dps[a] - dsum_ref[i, a:a + 1, :])
                dcq_ref[i, a:a + 1, :] += _colsum(ds)
                part = ds[:, 0:LANES]
                for u in range(1, tq // LANES):
                    part = part + ds[:, u * LANES:(u + 1) * LANES]
                dcl_scr[a] += part
                pbs.append(pt.astype(BF))
                dsbs.append(ds.astype(BF))
            qn = [jnp.where(lanes_of[a], q2, zero) for a in range(2)]
            don = [jnp.where(lanes_of[a], do2, zero) for a in range(2)]
            dv_scr[...] += (jnp.dot(pbs[0], don[0], preferred_element_type=F32) +
                            jnp.dot(pbs[1], don[1], preferred_element_type=F32))
            dk_scr[...] += (jnp.dot(dsbs[0], qn[0], preferred_element_type=F32) +
                            jnp.dot(dsbs[1], qn[1], preferred_element_type=F32))
            dqt_ref[i] += (jnp.dot(kts[0], dsbs[0], preferred_element_type=F32) +
                           jnp.dot(kts[1], dsbs[1], preferred_element_type=F32))
            return carry

        step(j, 0, True)
        lax.fori_loop(j + 1, nq, lambda i, c: step(i, c, False), 0)
        dk_ref[...] = dk_scr[...].astype(BF)
        dv_ref[...] = dv_scr[...].astype(BF)
        for a in range(2):
            dc_ref[a:a + 1, :] = -_colsum(dcl_scr[a].T)

    tile = pl.BlockSpec((tk, LANES), lambda hp, j: (j, hp))
    full = pl.BlockSpec((T, LANES), lambda hp, j: (0, hp))
    full3 = pl.BlockSpec((None, nq, LANES, tq), lambda hp, j: (hp, 0, 0, 0))
    rows4 = pl.BlockSpec((None, nq, 2, tq), lambda hp, j: (hp, 0, 0, 0))
    bft = jax.ShapeDtypeStruct((T, W), BF)
    r4 = jax.ShapeDtypeStruct((NP, nq, 2, tq), F32)
    return pl.pallas_call(
        body, name=name, grid=(NP, nq),
        in_specs=[full, full3, full, full3, pl.BlockSpec((None, None, LANES, tk), lambda hp, j: (hp, j, 0, 0)),
                  tile, tile, tile, rows4, rows4],
        out_specs=[full3, tile, tile, pl.BlockSpec((None, None, 2, tk), lambda hp, j: (hp, j, 0, 0)), rows4],
        out_shape=[jax.ShapeDtypeStruct((NP, nq, LANES, tq), F32), bft, bft, r4, r4],
        scratch_shapes=[pltpu.VMEM((tk, LANES), F32), pltpu.VMEM((tk, LANES), F32), pltpu.VMEM((2, tk, LANES), F32)],
        compiler_params=_params(2),
    )(q, qt3, dob, dot3, kt3, v, ka, kb, lse4, dsum4)


def loss_fwd_bwd(y, target, name):
    T, D = y.shape
    tm = _tile(T, 512)

    def body(y_ref, t_ref, dy_ref, l_ref):
        @pl.when(pl.program_id(0) == 0)
        def _():
            l_ref[...] = jnp.zeros_like(l_ref)

        d = y_ref[...] - t_ref[...]
        dy_ref[...] = d * (1.0 / D)
        l_ref[...] += 0.5 * jnp.sum(jnp.mean(d * d, axis=-1, keepdims=True), axis=0, keepdims=True)

    tok = pl.BlockSpec((tm, D), lambda i: (i, 0))
    return pl.pallas_call(
        body, name=name, grid=(T // tm,), in_specs=[tok, tok],
        out_specs=[tok, pl.BlockSpec((1, 1), lambda i: (0, 0))],
        out_shape=[jax.ShapeDtypeStruct((T, D), F32), jax.ShapeDtypeStruct((1, 1), F32)], compiler_params=_params(1),
    )(y, target)


def adamw(parts, w, m, v, layer, prev, name):
    L, R, C = w.shape
    tr = _tile(R, 256)
    c1 = 1.0 / (1.0 - ADAM_B1 ** ADAM_STEP)
    c2 = 1.0 / (1.0 - ADAM_B2 ** ADAM_STEP)

    def body(p_ref, w_ref, m_ref, v_ref, *rest):
        g_ref, d_ref, mo_ref, vo_ref = rest[-4:]
        g = p_ref[0].astype(F32)
        for d in range(1, N_DEV):
            g = g + p_ref[d].astype(F32)
        mn = ADAM_B1 * m_ref[...] + (1.0 - ADAM_B1) * g
        vn = ADAM_B2 * v_ref[...] + (1.0 - ADAM_B2) * (g * g)
        g_ref[...] = g
        mo_ref[...] = mn
        vo_ref[...] = vn
        d_ref[...] = -ADAM_LR * ((mn * c1) / (jnp.sqrt(vn * c2) + ADAM_EPS) + ADAM_WD * w_ref[...])

    blk = pl.BlockSpec((None, tr, C), lambda i: (layer, i, 0))
    shp = jax.ShapeDtypeStruct((L, R, C), F32)
    in_specs = [pl.BlockSpec((N_DEV, tr, C), lambda i: (0, i, 0)), blk, blk, blk]
    args = [parts, w, m, v]
    aliases = {}
    if prev is not None:
        in_specs += [pl.BlockSpec(memory_space=pl.ANY)] * 4
        args += list(prev)
        aliases = {4 + j: j for j in range(4)}
    return pl.pallas_call(
        body, name=name, grid=(R // tr,), in_specs=in_specs, out_specs=[blk, blk, blk, blk],
        out_shape=[shp, shp, shp, shp], input_output_aliases=aliases, compiler_params=_params(1),
    )(*args)


HBM_SPEC = pl.BlockSpec(memory_space=pltpu.HBM)
SEM_SPEC = pl.BlockSpec(memory_space=pltpu.SEMAPHORE)
EFFECT = pltpu.SideEffectType.DATAFLOW_SIDE_EFFECTING


def _mesh_pos():
    return lax.axis_index("x"), lax.axis_index("y"), lax.axis_index("c")


def _flip(v, bit):
    return v + bit - 2 * v * bit


def _peer(pos, delta):
    x, y, c = pos
    px, py, pc = _flip(x, (delta >> 2) & 1), _flip(y, (delta >> 1) & 1), _flip(c, delta & 1)
    return (px, py, pc), 4 * px + 2 * py + pc


def _me():
    x, y, c = _mesh_pos()
    return 4 * x + 2 * y + c


def _copies(src_refs, land_refs, whole, send, recv, incoming):
    pos = _mesh_pos()
    me = 4 * pos[0] + 2 * pos[1] + pos[2]
    out = []
    for k in range(len(src_refs)):
        for d in range(1, N_DEV):
            dev, idx = _peer(pos, d)
            j = k * (N_DEV - 1) + d - 1
            src = src_refs[k] if whole[k] else src_refs[k].at[idx]
            out.append(pltpu.make_async_remote_copy(
                src_ref=src, dst_ref=land_refs[k].at[idx if incoming else me], send_sem=send.at[j], recv_sem=recv.at[j],
                device_id=dev, device_id_type=pl.DeviceIdType.MESH))
    return out


def exchange_start(srcs, lands, whole, name):
    n = len(srcs)

    def body(*refs):
        for copy in _copies(refs[:n], refs[n:2 * n], whole, refs[2 * n], refs[2 * n + 1], False):
            copy.start()
        refs[-1][...] = jnp.zeros_like(refs[-1])

    sems = pltpu.SemaphoreType.DMA((n * (N_DEV - 1),))
    thru = [pltpu.HBM(a.shape, a.dtype) for a in list(srcs) + list(lands)]
    res = pl.pallas_call(
        body, name=name, in_specs=[HBM_SPEC] * (2 * n),
        out_specs=[SEM_SPEC, SEM_SPEC] + [HBM_SPEC] * (2 * n) + [pl.BlockSpec(memory_space=pltpu.VMEM)],
        out_shape=[sems, sems] + thru + [jax.ShapeDtypeStruct((8, LANES), F32)],
        input_output_aliases={j: 2 + j for j in range(2 * n)},
        compiler_params=pltpu.CompilerParams(has_side_effects=EFFECT),
    )(*[pltpu.with_memory_space_constraint(a, pltpu.HBM) for a in list(srcs) + list(lands)])
    return dict(send=res[0], recv=res[1], srcs=res[2:2 + n], lands=res[2 + n:2 + 2 * n], whole=whole, token=res[-1])


def exchange_wait(handle, after, name):
    n = len(handle["srcs"])
    whole = handle["whole"]

    def body(*refs):
        for copy in _copies(refs[:n], refs[n:2 * n], whole, refs[2 * n], refs[2 * n + 1], False):
            copy.wait_send()
        for copy in _copies(refs[:n], refs[n:2 * n], whole, refs[2 * n], refs[2 * n + 1], True):
            copy.wait_recv()

    bufs = list(handle["srcs"]) + list(handle["lands"])
    res = pl.pallas_call(
        body, name=name, in_specs=[HBM_SPEC] * (2 * n) + [SEM_SPEC, SEM_SPEC] + [pl.BlockSpec(memory_space=pl.ANY)] * len(after),
        out_specs=[HBM_SPEC] * (2 * n), out_shape=[pltpu.HBM(a.shape, a.dtype) for a in bufs],
        input_output_aliases={j: j for j in range(2 * n)},
        compiler_params=pltpu.CompilerParams(has_side_effects=EFFECT),
    )(*bufs, handle["send"], handle["recv"], *after)
    return list(res[n:])


def _landing(own, whole):
    me = _me()
    if not whole:
        own = lax.dynamic_index_in_dim(own, me, 0, keepdims=False)
    buf = lax.empty((N_DEV,) + own.shape, own.dtype)
    return lax.dynamic_update_slice(buf, own[None], (me,) + (0,) * own.ndim)


def _row(v):
    return v.reshape(1, -1)


def _pad_lanes(v, n):
    return jnp.pad(v, ((0, 0), (0, n - v.shape[1])))


GATHER_GROUPS = (
    ("ffn1_in_0", (("ffn1_w_in", 0),)),
    ("ffn1_out_0", (("ffn1_w_out", 0),)),
    ("hgrn", (("hgrn_w_in", 0), ("hgrn_w_out", 0))),
    ("rest_0", (("ffn2_w_in", 0), ("ffn2_w_out", 0), ("ple_w_gate", 0), ("ple_w_proj", 0), ("fox_w_kvf", 0))),
    ("ffn1_1", (("ffn1_w_in", 1), ("ffn1_w_out", 1))),
    ("fox", (("fox_w_qg", 0), ("fox_w_out", 0))),
    ("rest_1", (("ffn2_w_in", 1), ("ffn2_w_out", 1), ("ple_w_gate", 1), ("ple_w_proj", 1))),
)


def kernel(x, p, ffn1_norm_pre, ffn1_w_in, ffn1_w_out, ffn1_norm_post, mix_norm_pre, mix_norm_post, ffn2_norm_pre, ffn2_w_in, ffn2_w_out, ffn2_norm_post, hgrn_w_in, hgrn_lb_logits, hgrn_out_norm, hgrn_w_out, kv_norm, fox_w_kvf, fox_b_f, fox_w_qg, fox_w_out, ple_norm_pre, ple_w_gate, ple_w_proj, ple_norm_post, loss_target, m_ffn1_norm_pre, m_ffn1_w_in, m_ffn1_w_out, m_ffn1_norm_post, m_mix_norm_pre, m_mix_norm_post, m_ffn2_norm_pre, m_ffn2_w_in, m_ffn2_w_out, m_ffn2_norm_post, m_hgrn_w_in, m_hgrn_lb_logits, m_hgrn_out_norm, m_hgrn_w_out, m_kv_norm, m_fox_w_kvf, m_fox_b_f, m_fox_w_qg, m_fox_w_out, m_ple_norm_pre, m_ple_w_gate, m_ple_w_proj, m_ple_norm_post, v_ffn1_norm_pre, v_ffn1_w_in, v_ffn1_w_out, v_ffn1_norm_post, v_mix_norm_pre, v_mix_norm_post, v_ffn2_norm_pre, v_ffn2_w_in, v_ffn2_w_out, v_ffn2_norm_post, v_hgrn_w_in, v_hgrn_lb_logits, v_hgrn_out_norm, v_hgrn_w_out, v_kv_norm, v_fox_w_kvf, v_fox_b_f, v_fox_w_qg, v_fox_w_out, v_ple_norm_pre, v_ple_w_gate, v_ple_w_proj, v_ple_norm_post):
    weights = dict(ffn1_norm_pre=ffn1_norm_pre, ffn1_w_in=ffn1_w_in, ffn1_w_out=ffn1_w_out, ffn1_norm_post=ffn1_norm_post, mix_norm_pre=mix_norm_pre, mix_norm_post=mix_norm_post, ffn2_norm_pre=ffn2_norm_pre, ffn2_w_in=ffn2_w_in, ffn2_w_out=ffn2_w_out, ffn2_norm_post=ffn2_norm_post, hgrn_w_in=hgrn_w_in, hgrn_lb_logits=hgrn_lb_logits, hgrn_out_norm=hgrn_out_norm, hgrn_w_out=hgrn_w_out, kv_norm=kv_norm, fox_w_kvf=fox_w_kvf, fox_b_f=fox_b_f, fox_w_qg=fox_w_qg, fox_w_out=fox_w_out, ple_norm_pre=ple_norm_pre, ple_w_gate=ple_w_gate, ple_w_proj=ple_w_proj, ple_norm_post=ple_norm_post)
    mom1 = dict(ffn1_norm_pre=m_ffn1_norm_pre, ffn1_w_in=m_ffn1_w_in, ffn1_w_out=m_ffn1_w_out, ffn1_norm_post=m_ffn1_norm_post, mix_norm_pre=m_mix_norm_pre, mix_norm_post=m_mix_norm_post, ffn2_norm_pre=m_ffn2_norm_pre, ffn2_w_in=m_ffn2_w_in, ffn2_w_out=m_ffn2_w_out, ffn2_norm_post=m_ffn2_norm_post, hgrn_w_in=m_hgrn_w_in, hgrn_lb_logits=m_hgrn_lb_logits, hgrn_out_norm=m_hgrn_out_norm, hgrn_w_out=m_hgrn_w_out, kv_norm=m_kv_norm, fox_w_kvf=m_fox_w_kvf, fox_b_f=m_fox_b_f, fox_w_qg=m_fox_w_qg, fox_w_out=m_fox_w_out, ple_norm_pre=m_ple_norm_pre, ple_w_gate=m_ple_w_gate, ple_w_proj=m_ple_w_proj, ple_norm_post=m_ple_norm_post)
    mom2 = dict(ffn1_norm_pre=v_ffn1_norm_pre, ffn1_w_in=v_ffn1_w_in, ffn1_w_out=v_ffn1_w_out, ffn1_norm_post=v_ffn1_norm_post, mix_norm_pre=v_mix_norm_pre, mix_norm_post=v_mix_norm_post, ffn2_norm_pre=v_ffn2_norm_pre, ffn2_w_in=v_ffn2_w_in, ffn2_w_out=v_ffn2_w_out, ffn2_norm_post=v_ffn2_norm_post, hgrn_w_in=v_hgrn_w_in, hgrn_lb_logits=v_hgrn_lb_logits, hgrn_out_norm=v_hgrn_out_norm, hgrn_w_out=v_hgrn_w_out, kv_norm=v_kv_norm, fox_w_kvf=v_fox_w_kvf, fox_b_f=v_fox_b_f, fox_w_qg=v_fox_w_qg, fox_w_out=v_fox_w_out, ple_norm_pre=v_ple_norm_pre, ple_w_gate=v_ple_w_gate, ple_w_proj=v_ple_w_proj, ple_norm_post=v_ple_norm_post)
    names = list(weights)
    big = ["ffn1_w_in", "ffn1_w_out", "ffn2_w_in", "ffn2_w_out", "hgrn_w_in", "hgrn_w_out", "fox_w_kvf", "fox_w_qg",
           "fox_w_out", "ple_w_gate", "ple_w_proj"]
    small_names = [n for n in names if n not in big]

    T, D = x.shape[1], x.shape[2]
    depth = p.shape[0]
    h0 = x.reshape(T, D)
    target = loss_target.reshape(T, D)
    p3 = p.reshape(depth, T, p.shape[3])
    n_fox = fox_b_f.shape[0]
    fox_w = n_fox * FOX_HEAD_DIM
    fcol = 2 * fox_w // LANES
    b_row = _pad_lanes(_row(fox_b_f), LANES)

    tok = jnp.zeros((), F32)
    handles = {}
    for gname, keys in GATHER_GROUPS:
        shards = []
        for n, l in keys:
            w = weights[n]
            shards.append(((w[l] if w.ndim == 3 else w) + tok).astype(BF))
        handles[gname] = exchange_start(shards, [_landing(s, True) for s in shards], [True] * len(keys), f"gather_start_{gname}")
        tok = handles[gname]["token"][0, 0]
    W = {}

    def arrive(gname, after):
        lands = exchange_wait(handles[gname], after, f"gather_wait_{gname}")
        W.update(dict(zip(dict(GATHER_GROUPS)[gname], lands)))

    def w_rows(n, l):
        return W[n, l].reshape(-1, D)

    norm = lambda name, i: weights[name][i:i + 1]

    saved = []
    h = h0
    kvf = ct = kvf_w = None
    tq = _tile(T, 512)
    for i in range(depth):
        s = {}
        if i == 0:
            arrive("ffn1_in_0", [handles[GATHER_GROUPS[-1][0]]["token"]])
        for k in (1, 2):
            if k == 2:
                s["h_a"] = h
                if i == 0:
                    arrive("hgrn", [h])
                    z, xn = norm_mm(h, norm("mix_norm_pre", i), W["hgrn_w_in", 0], "hgrn_in")
                    xm, o, states = hgrn_fwd(z, hgrn_lb_logits, hgrn_out_norm, "hgrn_scan")
                    s.update(z=z, o=o, states=states)
                    wmix = w_rows("hgrn_w_out", 0)
                else:
                    arrive("fox", [h])
                    qg, xn = norm_mm(h, norm("mix_norm_pre", i), W["fox_w_qg", 0], "fox_qg")
                    ka, kb, vt3 = fox_prep(kvf, ct, n_fox, tq)
                    xm, o, lse = fox_fwd(qg, ka, kb, vt3, "fox_attn")
                    s.update(qg=qg, o=o, lse=lse, ka=ka, kb=kb)
                    wmix = w_rows("fox_w_out", 0)
                h, ym = mm_norm_res(xm, wmix, norm("mix_norm_post", i), h, 1.0, f"mix_out_{i}", kb=_tile(xm.shape[1], 512))
                s.update(xm=xm, ym=ym, xn_mix=xn)
                arrive(f"rest_{i}", [h])
            gate, up, a, xn = norm_mm_swiglu(h, norm(f"ffn{k}_norm_pre", i), W[f"ffn{k}_w_in", i], f"ffn{k}_in_{i}")
            if (i, k) == (0, 1):
                arrive("ffn1_out_0", [a])
            hn, y = mm_norm_res(a, w_rows(f"ffn{k}_w_out", i), norm(f"ffn{k}_norm_post", i), h, 0.5, f"ffn{k}_out_{i}")
            s[f"ffn{k}"] = (h, gate, up, a, y, xn)
            h = hn
        s["h_c"] = h
        ple_proj = W["ple_w_proj", i].transpose(1, 0, 2).reshape(p.shape[3], D)
        h, pgate, pp = ple_fwd(h, norm("ple_norm_pre", i), w_rows("ple_w_gate", i), p3, ple_proj, norm("ple_norm_post", i),
                               i, f"ple_{i}")
        s.update(pgate=pgate, pp=pp)
        saved.append(s)
        if i == 0:
            kvf_nat = W["fox_w_kvf", 0].transpose(1, 0, 2).reshape(D, -1)
            kvf_cols = kvf_nat.shape[1]
            kvf_w = _pad_lanes(kvf_nat, 2 * fox_w + LANES)
            kvf, xn_kv = norm_mm(h, _row(kv_norm), kvf_w, "fox_kvf", tn=kvf_w.shape[1])
            ct = fox_cum_fwd(kvf, fcol, b_row, "fox_cum")
            h_kv = h
            arrive("ffn1_1", [h])

    dh, loss_part = loss_fwd_bwd(h, target, "loss")
    loss = lax.psum(loss_part[0, 0], ("x", "y", "c"))

    gsmall = {n: [None] * weights[n].shape[0] if weights[n].ndim == 2 else None for n in small_names}
    sent = []

    def send(gname, keys, srcs, whole):
        lands = [_landing(a, w) for a, w in zip(srcs, whole)]
        hd = exchange_start(srcs, lands, whole, f"scatter_start_{gname}")
        sent.append((gname, keys, hd))
        return hd["token"][0:1, 0:1]

    def send_grads(gname, grads):
        return send(gname, list(grads), list(grads.values()), [False] * len(grads))

    for i in reversed(range(depth)):
        s = saved[i]
        grads = {}
        dh, du, dpp, xn, dgpre, dgpost = ple_bwd(dh, s["h_c"], s["pgate"], s["pp"], norm("ple_norm_pre", i),
                                                 w_rows("ple_w_gate", i), norm("ple_norm_post", i), f"ple_bwd_{i}")
        gsmall["ple_norm_pre"][i], gsmall["ple_norm_post"][i] = dgpre, dgpost
        grads["ple_w_gate", i] = mm_tn(xn, du, f"ple_dgate_{i}", xb=_tile(D, 512)).reshape(N_DEV, -1, D)
        dproj = mm_tn(p3, dpp, f"ple_dproj_{i}", x_layer=i, yb=D)[0]
        grads["ple_w_proj", i] = dproj.reshape(dproj.shape[0], N_DEV, -1).transpose(1, 0, 2)
        for k in (2, 1):
            hin, gate, up, a, y, xn = s[f"ffn{k}"]
            ffn_cs = gate.shape[2]
            dy, dgpost, dz = nbwd_mm_nt(dh, y, norm(f"ffn{k}_norm_post", i), w_rows(f"ffn{k}_w_out", i), 0.5, ffn_cs,
                                        f"ffn{k}_bwd_out_{i}", gate=gate, up=up)
            dz = dz.reshape(-1, T, ffn_cs)
            grads[f"ffn{k}_w_out", i] = mm_tn(a, dy, f"ffn{k}_dwout_{i}").reshape(N_DEV, -1, D)
            grads[f"ffn{k}_w_in", i] = mm_tn(xn, dz, f"ffn{k}_dwin_{i}")
            tokv = send_grads(f"ffn{k}_{i}", grads)
            grads = {}
            dh, dgpre = mm_nt_nbwd(dz, W[f"ffn{k}_w_in", i], hin, norm(f"ffn{k}_norm_pre", i) + tokv, dh, f"ffn{k}_bwd_in_{i}")
            gsmall[f"ffn{k}_norm_pre"][i], gsmall[f"ffn{k}_norm_post"][i] = dgpre, dgpost
            if k == 2:
                nm = "hgrn" if i == 0 else "fox"
                wmix = w_rows(f"{nm}_w_out", 0)
                dy, dgpost, dxm = nbwd_mm_nt(dh, s["ym"], norm("mix_norm_post", i), wmix, 1.0, _tile(wmix.shape[0], 512),
                                             f"{nm}_bwd_out")
                grads[f"{nm}_w_out", 0] = mm_tn(s["xm"], dy, f"{nm}_dwout", xb=_tile(wmix.shape[0], 512)).reshape(N_DEV, -1, D)
                gsmall["mix_norm_post"][i] = dgpost
                if i == 0:
                    dq, df, dv, dg, dlog, don = hgrn_bwd(dxm, s["z"], s["o"], s["states"], hgrn_lb_logits, hgrn_out_norm,
                                                         "hgrn_scan_bwd")
                    gsmall["hgrn_lb_logits"] = [dlog[0:1], dlog[1:2]]
                    gsmall["hgrn_out_norm"] = [don]
                    dzm = jnp.concatenate([dq, df, dv, dg], axis=1)
                    nmin = "hgrn_w_in"
                else:
                    do, dg, dsum = fox_gate_bwd(dxm, s["o"], s["qg"], "fox_gate_bwd")
                    dsum4 = dsum[:, :n_fox].T.reshape(n_fox // 2, 2, T // tq, tq).transpose(0, 2, 1, 3)
                    dqt, dk_sh, dv_sh, dc4, dcq4 = fox_bwd(*fox_bwd_prep(s["qg"], kvf, do, tq), s["ka"], s["kb"], s["lse"], dsum4,
                                                           "fox_attn_bwd")
                    dq = dqt.transpose(1, 3, 0, 2).reshape(T, fox_w)
                    dzm = jnp.concatenate([dq.astype(BF), dg], axis=1)
                    nmin = "fox_w_qg"
                wmin = W[nmin, 0]
                grads[nmin, 0] = mm_tn(s["xn_mix"], dzm, f"{nm}_dwin", yb=wmin.shape[2])
                tokv = send_grads(f"mix_{i}", grads)
                grads = {}
                dh, dgpre = mm_nt_nbwd(dzm, wmin, s["h_a"], norm("mix_norm_pre", i) + tokv, dh, f"{nm}_bwd_in", tn=wmin.shape[2])
                gsmall["mix_norm_pre"][i] = dgpre
        if i == 1:
            dct = (dc4 + dcq4).transpose(0, 2, 1, 3).reshape(n_fox, T)
            dct = jnp.pad(dct, ((0, LANES - n_fox), (0, 0)))
            dflog, db = fox_cum_bwd(dct, kvf, fcol, b_row, "fox_cum_bwd")
            gsmall["fox_b_f"] = db[:, :n_fox]
            dkvf = jnp.concatenate([dk_sh, dv_sh, dflog], axis=1)
            dwk = mm_tn(xn_kv, dkvf, "fox_dwkvf", yb=kvf_w.shape[1])[0][:, :kvf_cols]
            tokv = send_grads("kvf", {("fox_w_kvf", 0): dwk.reshape(D, N_DEV, -1).transpose(1, 0, 2)})
            dh, dgkv = mm_nt_nbwd(dkvf, kvf_w, h_kv, _row(kv_norm) + tokv, dh, "fox_kvf_bwd", tn=kvf_w.shape[1])
            gsmall["kv_norm"] = dgkv
    grad_x = dh.reshape(x.shape)

    def small_rows(n):
        g = gsmall[n]
        rows = g if isinstance(g, list) else [g]
        return [_pad_lanes(r, D) for r in rows]

    counts = {n: len(small_rows(n)) for n in small_names}
    packed = jnp.concatenate([r for n in small_names for r in small_rows(n)], axis=0)
    n_rows = packed.shape[0]
    packed = jnp.pad(packed, ((0, -n_rows % 8), (0, 0)))
    send("small", ["small"], [packed], [True])

    res = {}
    after = [dh]
    small_parts = None
    for gname, keys, hd in sent:
        lands = exchange_wait(hd, after, f"scatter_wait_{gname}")
        after = []
        for key, parts in zip(keys, lands):
            if key == "small":
                small_parts = parts
                continue
            n, l = key
            w = weights[n]
            as3 = (lambda a: a.reshape((1,) + a.shape)) if w.ndim == 2 else (lambda a: a)
            res[n] = adamw(parts, as3(w), as3(mom1[n]), as3(mom2[n]), l, res.get(n), f"adamw_{n}_{l}")
            after.append(res[n][1])

    def pack(d):
        rows = []
        for n in small_names:
            a = d[n]
            rows.append(_pad_lanes(a.reshape(-1, a.shape[-1]), D))
        a = jnp.concatenate(rows, axis=0)
        return jnp.pad(a, ((0, -n_rows % 8), (0, 0)))[None]

    sm = adamw(small_parts, pack(weights), pack(mom1), pack(mom2), 0, None, "adamw_small")
    off = 0
    for n in small_names:
        w = weights[n]
        res[n] = [a[0, off:off + counts[n], :w.shape[-1]].reshape(w.shape) for a in sm]
        off += counts[n]

    out = [loss, grad_x]
    for j in range(4):
        out += [res[n][j].reshape(weights[n].shape) for n in names]
    return tuple(out)
```

```python
import functools

import jax
import jax.numpy as jnp
from jax import lax
from jax.experimental import pallas as pl
from jax.experimental.pallas import tpu as pltpu

F32 = jnp.float32
BF = jnp.bfloat16
NORM_EPS = 1e-6
N_DEV = 8
HGRN_DK = 128
HGRN_CHUNK = 16
HGRN_HEADS_PER_STEP = 4
HALF = HGRN_CHUNK // 2
FOX_HEAD_DIM = 64
LANES = 128
ADAM_LR, ADAM_B1, ADAM_B2, ADAM_EPS, ADAM_WD, ADAM_STEP = 0.001, 0.9, 0.999, 1e-08, 0.01, 10
VMEM_LIMIT = 56 * 1024 * 1024
HI = lax.Precision.HIGHEST
NT = (((1,), (1,)), ((), ()))
TN = (((0,), (0,)), ((), ()))


def _params(n_axes):
    return pltpu.CompilerParams(dimension_semantics=("arbitrary",) * n_axes, vmem_limit_bytes=VMEM_LIMIT)


def _tile(n, want):
    t = min(n, want)
    while n % t:
        t //= 2
    return t


def _sigmoid(x):
    return 1.0 / (1.0 + jnp.exp(-x))


def _rms(x):
    r = lax.rsqrt(jnp.mean(x * x, axis=-1, keepdims=True) + NORM_EPS)
    return x * r, r


def _norm_bwd(dy, xhat, r, g):
    dxh = dy * g
    return r * (dxh - xhat * jnp.mean(dxh * xhat, axis=-1, keepdims=True))


def _colsum(x):
    return jnp.sum(x, axis=0, keepdims=True)


def _w_spec(w, blk):
    if w.ndim == 3:
        return lambda off: pl.BlockSpec((None, w.shape[1], w.shape[2]), lambda i, j: (j + off, 0, 0))
    return lambda off: pl.BlockSpec((w.shape[0], blk), lambda i, j: (0, j + off))


def norm_mm_swiglu(h, g, w3, name):
    T, D = h.shape
    nb, cs, _ = w3.shape
    nh = nb // 2
    tm = _tile(T, 1024)

    def body(h_ref, g_ref, wg_ref, wu_ref, gate_ref, up_ref, a_ref, xn_ref):
        @pl.when(pl.program_id(1) == 0)
        def _():
            xh, _ = _rms(h_ref[...])
            xn_ref[...] = (xh * g_ref[...]).astype(BF)

        xn = xn_ref[...]
        gt = lax.dot_general(xn, wg_ref[...], NT, preferred_element_type=F32)
        up = lax.dot_general(xn, wu_ref[...], NT, preferred_element_type=F32)
        gate_ref[...] = gt.astype(BF)
        up_ref[...] = up.astype(BF)
        a_ref[...] = (gt * _sigmoid(gt) * up).astype(BF)

    ws = _w_spec(w3, cs)
    blk = pl.BlockSpec((None, tm, cs), lambda i, j: (j, i, 0))
    shp = jax.ShapeDtypeStruct((nh, T, cs), BF)
    return pl.pallas_call(
        body, name=name, grid=(T // tm, nh),
        in_specs=[pl.BlockSpec((tm, D), lambda i, j: (i, 0)), pl.BlockSpec((1, D), lambda i, j: (0, 0)), ws(0), ws(nh)],
        out_specs=[blk, blk, blk, pl.BlockSpec((tm, D), lambda i, j: (i, 0))],
        out_shape=[shp, shp, shp, jax.ShapeDtypeStruct((T, D), BF)], compiler_params=_params(2),
    )(h, g, w3, w3)


def norm_mm(h, g, w, name, tn=None):
    T, D = h.shape
    if w.ndim == 3:
        nb, cs = w.shape[0], w.shape[2]
    else:
        cs = tn
        nb = w.shape[1] // cs
    tm = _tile(T, 1024)

    def body(h_ref, g_ref, w_ref, z_ref, xn_ref):
        @pl.when(pl.program_id(1) == 0)
        def _():
            xh, _ = _rms(h_ref[...])
            xn_ref[...] = (xh * g_ref[...]).astype(BF)

        z_ref[...] = jnp.dot(xn_ref[...], w_ref[...], preferred_element_type=F32)

    return pl.pallas_call(
        body, name=name, grid=(T // tm, nb),
        in_specs=[pl.BlockSpec((tm, D), lambda i, j: (i, 0)), pl.BlockSpec((1, D), lambda i, j: (0, 0)),
                  _w_spec(w, cs)(0)],
        out_specs=[pl.BlockSpec((tm, cs), lambda i, j: (i, j)), pl.BlockSpec((tm, D), lambda i, j: (i, 0))],
        out_shape=[jax.ShapeDtypeStruct((T, nb * cs), F32), jax.ShapeDtypeStruct((T, D), BF)], compiler_params=_params(2),
    )(h, g, w)


def _x_spec(x, tm, kb):
    if x.ndim == 3:
        return pl.BlockSpec((None, tm, x.shape[2]), lambda i, j: (j, i, 0))
    return pl.BlockSpec((tm, kb), lambda i, j: (i, j))


def mm_norm_res(x, w2, g, h, coef, name, kb=None):
    T, D = h.shape
    if x.ndim == 3:
        nb, kb = x.shape[0], x.shape[2]
    else:
        nb = x.shape[1] // kb
    tm = _tile(T, 1024)

    def body(x_ref, w_ref, h_ref, g_ref, hn_ref, y_ref, acc_ref):
        b = pl.program_id(1)

        @pl.when(b == 0)
        def _():
            acc_ref[...] = jnp.zeros_like(acc_ref)

        acc_ref[...] += jnp.dot(x_ref[...], w_ref[...], preferred_element_type=F32)

        @pl.when(b == nb - 1)
        def _():
            y = acc_ref[...]
            y_ref[...] = y
            yh, _ = _rms(y)
            hn_ref[...] = h_ref[...] + coef * (yh * g_ref[...])

    tok = pl.BlockSpec((tm, D), lambda i, j: (i, 0))
    shp = jax.ShapeDtypeStruct((T, D), F32)
    return pl.pallas_call(
        body, name=name, grid=(T // tm, nb),
        in_specs=[_x_spec(x, tm, kb), pl.BlockSpec((kb, D), lambda i, j: (j, 0)), tok,
                  pl.BlockSpec((1, D), lambda i, j: (0, 0))],
        out_specs=[tok, tok], out_shape=[shp, shp],
        scratch_shapes=[pltpu.VMEM((tm, D), F32)], compiler_params=_params(2),
    )(x, w2, h, g)


def nbwd_mm_nt(dout, y, g, w2, coef, kb, name, gate=None, up=None):
    T, D = dout.shape
    nb = w2.shape[0] // kb
    swiglu = gate is not None
    tm = _tile(T, 512)

    def body(*refs):
        if swiglu:
            dout_ref, y_ref, g_ref, w_ref, gate_ref, up_ref, dy_ref, dg_ref, da_ref, dys_ref = refs
        else:
            dout_ref, y_ref, g_ref, w_ref, dy_ref, dg_ref, da_ref, dys_ref = refs
        i, b = pl.program_id(0), pl.program_id(1)

        @pl.when((i == 0) & (b == 0))
        def _():
            dg_ref[...] = jnp.zeros_like(dg_ref)

        @pl.when(b == 0)
        def _():
            yh, r = _rms(y_ref[...])
            dyn = coef * dout_ref[...]
            dg_ref[...] += _colsum(dyn * yh)
            dy = _norm_bwd(dyn, yh, r, g_ref[...]).astype(BF)
            dys_ref[...] = dy
            dy_ref[...] = dy

        da = lax.dot_general(dys_ref[...], w_ref[...], NT, preferred_element_type=F32)
        if swiglu:
            gt = gate_ref[...].astype(F32)
            u = up_ref[...].astype(F32)
            sg = _sigmoid(gt)
            da_ref[0] = (da * u * (sg * (1.0 + gt * (1.0 - sg)))).astype(BF)
            da_ref[1] = (da * (gt * sg)).astype(BF)
        else:
            da_ref[...] = da.astype(BF)

    tok = pl.BlockSpec((tm, D), lambda i, j: (i, 0))
    vec = pl.BlockSpec((1, D), lambda i, j: (0, 0))
    in_specs = [tok, tok, vec, pl.BlockSpec((kb, D), lambda i, j: (j, 0))]
    args = [dout, y, g, w2]
    if swiglu:
        blk = pl.BlockSpec((None, tm, kb), lambda i, j: (j, i, 0))
        in_specs += [blk, blk]
        args += [gate, up]
        da_spec = pl.BlockSpec((2, None, tm, kb), lambda i, j: (0, j, i, 0))
        da_shape = jax.ShapeDtypeStruct((2, nb, T, kb), BF)
    else:
        da_spec = pl.BlockSpec((tm, kb), lambda i, j: (i, j))
        da_shape = jax.ShapeDtypeStruct((T, nb * kb), BF)
    return pl.pallas_call(
        body, name=name, grid=(T // tm, nb), in_specs=in_specs,
        out_specs=[tok, vec, da_spec],
        out_shape=[jax.ShapeDtypeStruct((T, D), BF), jax.ShapeDtypeStruct((1, D), F32), da_shape],
        scratch_shapes=[pltpu.VMEM((tm, D), BF)], compiler_params=_params(2),
    )(*args)


def mm_nt_nbwd(dz, w, h, g, dout, name, tn=None, transposed=False):
    T, D = h.shape
    if w.ndim == 3:
        nb, cs = w.shape[0], w.shape[1 if transposed else 2]
    else:
        cs = tn
        nb = w.shape[1] // cs
    tm = _tile(T, 1024)

    def body(dz_ref, w_ref, h_ref, g_ref, dout_ref, dh_ref, dg_ref, acc_ref):
        i, b = pl.program_id(0), pl.program_id(1)

        @pl.when((i == 0) & (b == 0))
        def _():
            dg_ref[...] = jnp.zeros_like(dg_ref)

        @pl.when(b == 0)
        def _():
            acc_ref[...] = jnp.zeros_like(acc_ref)

        if transposed:
            acc_ref[...] += jnp.dot(dz_ref[...], w_ref[...], preferred_element_type=F32)
        else:
            acc_ref[...] += lax.dot_general(dz_ref[...], w_ref[...], NT, preferred_element_type=F32)

        @pl.when(b == nb - 1)
        def _():
            xh, r = _rms(h_ref[...])
            gg = g_ref[...]
            dxn = acc_ref[...]
            dg_ref[...] += _colsum(dxn * xh)
            dh_ref[...] = dout_ref[...] + _norm_bwd(dxn, xh, r, gg)

    tok = pl.BlockSpec((tm, D), lambda i, j: (i, 0))
    vec = pl.BlockSpec((1, D), lambda i, j: (0, 0))
    return pl.pallas_call(
        body, name=name, grid=(T // tm, nb),
        in_specs=[_x_spec(dz, tm, cs), _w_spec(w, cs)(0), tok, vec, tok],
        out_specs=[tok, vec],
        out_shape=[jax.ShapeDtypeStruct((T, D), F32), jax.ShapeDtypeStruct((1, D), F32)],
        scratch_shapes=[pltpu.VMEM((tm, D), F32)], compiler_params=_params(2),
    )(dz, w, h, g, dout)


def mm_tn(x, y, name, xb=None, yb=None, x_layer=None):
    T = y.shape[-2]
    wide = (yb if yb is not None else y.shape[-1]) > 1024
    tt = _tile(T, 1024 if wide else 2048)
    x_split = (x.ndim == 3 and x_layer is None) or xb is not None
    if x_layer is not None:
        xs = pl.BlockSpec((None, tt, x.shape[2]), lambda b, t: (x_layer, t, 0))
        kdim = x.shape[2]
    elif x.ndim == 3:
        xs = pl.BlockSpec((None, tt, x.shape[2]), lambda b, t: (b, t, 0))
        nb, kdim = x.shape[0], x.shape[2]
    elif xb is not None:
        xs = pl.BlockSpec((tt, xb), lambda b, t: (t, b))
        nb, kdim = x.shape[1] // xb, xb
    else:
        xs = pl.BlockSpec((tt, x.shape[1]), lambda b, t: (t, 0))
        kdim = x.shape[1]
    if x_split:
        ys = pl.BlockSpec((tt, y.shape[1]), lambda b, t: (t, 0))
        ndim = y.shape[1]
        out_spec = pl.BlockSpec((kdim, ndim), lambda b, t: (b, 0))
        out_shape = jax.ShapeDtypeStruct((nb * kdim, ndim), BF)
    else:
        if y.ndim == 3:
            ys = pl.BlockSpec((None, tt, y.shape[2]), lambda b, t: (b, t, 0))
            nb, ndim = y.shape[0], y.shape[2]
        else:
            ys = pl.BlockSpec((tt, yb), lambda b, t: (t, b))
            nb, ndim = y.shape[1] // yb, yb
        out_spec = pl.BlockSpec((None, kdim, ndim), lambda b, t: (b, 0, 0))
        out_shape = jax.ShapeDtypeStruct((nb, kdim, ndim), BF)
    nt = T // tt

    def body(x_ref, y_ref, o_ref, acc_ref):
        t = pl.program_id(1)

        @pl.when(t == 0)
        def _():
            acc_ref[...] = jnp.zeros_like(acc_ref)

        acc_ref[...] += lax.dot_general(x_ref[...].astype(BF), y_ref[...].astype(BF), TN, preferred_element_type=F32)

        @pl.when(t == nt - 1)
        def _():
            o_ref[...] = acc_ref[...].astype(BF)

    return pl.pallas_call(
        body, name=name, grid=(nb, nt), in_specs=[xs, ys], out_specs=out_spec, out_shape=out_shape,
        scratch_shapes=[pltpu.VMEM((kdim, ndim), F32)], compiler_params=_params(2),
    )(x, y)


def ple_fwd(h, gpre, wg, p3, wp, gpost, layer, name):
    T, D = h.shape
    pd = p3.shape[2]
    tm = _tile(T, 512)

    def body(h_ref, gpre_ref, wg_ref, p_ref, wp_ref, gpost_ref, hn_ref, gate_ref, pp_ref):
        x = h_ref[...]
        xh, _ = _rms(x)
        u = jnp.dot((xh * gpre_ref[...]).astype(BF), wg_ref[...], preferred_element_type=F32)
        gate = _sigmoid(u)
        pp = jnp.dot(p_ref[...].astype(BF), wp_ref[...], preferred_element_type=F32)
        yh, _ = _rms(gate * pp)
        hn_ref[...] = x + yh * gpost_ref[...]
        gate_ref[...] = gate.astype(BF)
        pp_ref[...] = pp.astype(BF)

    tok = pl.BlockSpec((tm, D), lambda i: (i, 0))
    vec = pl.BlockSpec((1, D), lambda i: (0, 0))
    return pl.pallas_call(
        body, name=name, grid=(T // tm,),
        in_specs=[tok, vec, pl.BlockSpec((D, D), lambda i: (0, 0)),
                  pl.BlockSpec((None, tm, pd), lambda i: (layer, i, 0)),
                  pl.BlockSpec((pd, D), lambda i: (0, 0)), vec],
        out_specs=[tok, tok, tok],
        out_shape=[jax.ShapeDtypeStruct((T, D), F32), jax.ShapeDtypeStruct((T, D), BF), jax.ShapeDtypeStruct((T, D), BF)],
        compiler_params=_params(1),
    )(h, gpre, wg, p3, wp, gpost)


def ple_bwd(dout, h, gate, pp, gpre, wg, gpost, name):
    T, D = h.shape
    tm = _tile(T, 512)

    def body(dout_ref, h_ref, gate_ref, pp_ref, gpre_ref, wg_ref, gpost_ref, dh_ref, du_ref, dpp_ref, xn_ref, dgpre_ref, dgpost_ref):
        @pl.when(pl.program_id(0) == 0)
        def _():
            dgpre_ref[...] = jnp.zeros_like(dgpre_ref)
            dgpost_ref[...] = jnp.zeros_like(dgpost_ref)

        dout = dout_ref[...]
        gate = gate_ref[...].astype(F32)
        pp = pp_ref[...].astype(F32)
        yh, ry = _rms(gate * pp)
        dgpost_ref[...] += _colsum(dout * yh)
        dy = _norm_bwd(dout, yh, ry, gpost_ref[...])
        dpp_ref[...] = (dy * gate).astype(BF)
        du = (dy * pp * gate * (1.0 - gate)).astype(BF)
        du_ref[...] = du
        dxn = lax.dot_general(du, wg_ref[...], NT, preferred_element_type=F32)
        xh, r = _rms(h_ref[...])
        gp = gpre_ref[...]
        dgpre_ref[...] += _colsum(dxn * xh)
        dh_ref[...] = dout + _norm_bwd(dxn, xh, r, gp)
        xn_ref[...] = (xh * gp).astype(BF)

    tok = pl.BlockSpec((tm, D), lambda i: (i, 0))
    vec = pl.BlockSpec((1, D), lambda i: (0, 0))
    bft = jax.ShapeDtypeStruct((T, D), BF)
    v32 = jax.ShapeDtypeStruct((1, D), F32)
    return pl.pallas_call(
        body, name=name, grid=(T // tm,),
        in_specs=[tok, tok, tok, tok, vec, pl.BlockSpec((D, D), lambda i: (0, 0)), vec],
        out_specs=[tok, tok, tok, tok, vec, vec],
        out_shape=[jax.ShapeDtypeStruct((T, D), F32), bft, bft, bft, v32, v32],
        compiler_params=_params(1),
    )(dout, h, gate, pp, gpre, wg, gpost)


def _chunk_tri(tb, upper):
    r = lax.broadcasted_iota(jnp.int32, (tb, tb), 0)
    c = lax.broadcasted_iota(jnp.int32, (tb, tb), 1)
    shift = HGRN_CHUNK.bit_length() - 1
    same = jnp.right_shift(r, shift) == jnp.right_shift(c, shift)
    return (same & ((c >= r) if upper else (c <= r))).astype(F32)


def _hgrn_gates(z, logits):
    lb = 1.0 / (1.0 + jnp.exp(logits[1:2, :] - logits[0:1, :]))
    e = jnp.exp(-jnp.abs(z))
    inv = 1.0 / (1.0 + e)
    sig = jnp.where(z >= 0, inv, e * inv)
    nsig = jnp.where(z >= 0, e * inv, inv)
    return lb, sig, nsig, lb + (1.0 - lb) * sig


def hgrn_fwd(z, lb_logits, out_norm, name):
    T = z.shape[0]
    W = z.shape[1] // 4
    H = W // HGRN_DK
    C = HGRN_CHUNK
    HB = _tile(H, HGRN_HEADS_PER_STEP)
    tb = _tile(T, 256)
    nch = tb // C

    def body(zq_ref, zf_ref, zv_ref, zg_ref, lbl_ref, on_ref, x_ref, o_ref, st_ref, s_scr, cum_scr, k_scr, v_scr, o_scr):
        @pl.when(pl.program_id(1) == 0)
        def _():
            s_scr[...] = jnp.zeros_like(s_scr)

        lb, sig, nsig, f = _hgrn_gates(zf_ref[...], lbl_ref[...])
        cum = jnp.dot(_chunk_tri(tb, False), jnp.log(f), precision=HI, preferred_element_type=F32)
        kk = (1.0 - lb) * nsig
        for hh in range(HB):
            cols = slice(hh * HGRN_DK, (hh + 1) * HGRN_DK)
            cum_scr[hh] = cum[:, cols]
            k_scr[hh] = kk[:, cols]
            v_scr[hh] = zv_ref[:, cols]
        row = lax.broadcasted_iota(jnp.int32, (C, HGRN_DK), 0)

        def chunk(c, carry):
            r0 = pl.multiple_of(c * C, C)
            rows = pl.ds(r0, C)
            last_row = pl.ds(r0 + C - 1, 1)
            heads = []
            for hh in range(HB):
                cols = slice(hh * HGRN_DK, (hh + 1) * HGRN_DK)
                q, cu = zq_ref[rows, cols], cum_scr[hh, rows, :]
                st = s_scr[hh]
                st_ref[c, hh] = st
                o = lax.dot_general((q * jnp.exp(cu)).astype(BF), st.astype(BF), NT, preferred_element_type=F32)
                last = cum_scr[hh, last_row, :]
                kg = (k_scr[hh, rows, :] * jnp.exp(last - cu)).astype(BF)
                s_scr[hh] = st * jnp.exp(last) + lax.dot_general(v_scr[hh, rows, :].astype(BF), kg, TN, preferred_element_type=F32)
                heads.append((hh, q, cu, o))
            for hh, q, cu, o in heads:
                qr = q.astype(BF).astype(F32)
                low = jnp.zeros((C - HALF, HGRN_DK), F32)
                for s in range(C):
                    one = pl.ds(r0 + s, 1)
                    sl = slice(0 if s < HALF else HALF, C)
                    e = jnp.exp(jnp.minimum(cu[sl] - cum_scr[hh, one, :], 0.0))
                    col = jnp.sum(qr[sl] * (e * k_scr[hh, one, :]).astype(BF).astype(F32), axis=-1, keepdims=True)
                    col = jnp.where(row[sl] >= s, col, 0.0).astype(BF).astype(F32)
                    term = col * v_scr[hh, one, :].astype(BF).astype(F32)
                    if s < HALF:
                        o = o + term
                    else:
                        low = low + term
                o_scr[hh, rows, :] = o
                o_scr[hh, pl.ds(r0 + HALF, C - HALF), :] += low
            return carry

        lax.fori_loop(0, nch, chunk, 0)
        for hh in range(HB):
            cols = slice(hh * HGRN_DK, (hh + 1) * HGRN_DK)
            o = o_scr[hh]
            o_ref[:, cols] = o
            oh, _ = _rms(o)
            g = zg_ref[:, cols]
            x_ref[:, cols] = (oh * on_ref[...] * (g * _sigmoid(g))).astype(BF)

    def zs(part):
        return pl.BlockSpec((tb, HB * HGRN_DK), lambda hd, i: (i, part * (H // HB) + hd))

    blk = pl.BlockSpec((tb, HB * HGRN_DK), lambda hd, i: (i, hd))
    wide = pltpu.VMEM((HB, tb, HGRN_DK), F32)
    return pl.pallas_call(
        body, name=name, grid=(H // HB, T // tb),
        in_specs=[zs(0), zs(1), zs(2), zs(3), pl.BlockSpec((2, HB * HGRN_DK), lambda hd, i: (0, hd)),
                  pl.BlockSpec((1, HGRN_DK), lambda hd, i: (0, 0))],
        out_specs=[blk, blk, pl.BlockSpec((nch, HB, HGRN_DK, HGRN_DK), lambda hd, i: (i, hd, 0, 0))],
        out_shape=[jax.ShapeDtypeStruct((T, W), BF), jax.ShapeDtypeStruct((T, W), F32),
                   jax.ShapeDtypeStruct((T // C, H, HGRN_DK, HGRN_DK), F32)],
        scratch_shapes=[pltpu.VMEM((HB, HGRN_DK, HGRN_DK), F32), wide, wide, wide, wide],
        compiler_params=_params(2),
    )(z, z, z, z, lb_logits, out_norm)


def hgrn_bwd(dx, z, o, states, lb_logits, out_norm, name):
    T = z.shape[0]
    W = z.shape[1] // 4
    H = W // HGRN_DK
    C = HGRN_CHUNK
    HB = _tile(H, HGRN_HEADS_PER_STEP)
    tb = _tile(T, 256)
    nch = tb // C
    nblk = T // tb

    def body(dx_ref, zq_ref, zf_ref, zv_ref, zg_ref, o_ref, st_ref, lbl_ref, on_ref,
             dq_ref, df_ref, dv_ref, dg_ref, dl_ref, don_ref,
             ds_scr, cum_scr, k_scr, v_scr, q_scr, do_scr, dq_scr, dk_scr, dv_scr, dcum_scr):
        hd, i = pl.program_id(0), pl.program_id(1)

        @pl.when(i == 0)
        def _():
            ds_scr[...] = jnp.zeros_like(ds_scr)
            dl_ref[...] = jnp.zeros_like(dl_ref)

        @pl.when((i == 0) & (hd == 0))
        def _():
            don_ref[...] = jnp.zeros_like(don_ref)

        lb, sig, nsig, f = _hgrn_gates(zf_ref[...], lbl_ref[...])
        cum = jnp.dot(_chunk_tri(tb, False), jnp.log(f), precision=HI, preferred_element_type=F32)
        kk = (1.0 - lb) * nsig
        w = on_ref[...]
        for hh in range(HB):
            cols = slice(hh * HGRN_DK, (hh + 1) * HGRN_DK)
            cum_scr[hh] = cum[:, cols]
            k_scr[hh] = kk[:, cols]
            v_scr[hh] = zv_ref[:, cols]
            q_scr[hh] = zq_ref[:, cols]
            oh, r = _rms(o_ref[:, cols])
            g = zg_ref[:, cols]
            sg = _sigmoid(g)
            dxv = dx_ref[:, cols].astype(F32)
            dg_ref[:, cols] = (dxv * (oh * w) * (sg * (1.0 + g * (1.0 - sg)))).astype(BF)
            don = dxv * (g * sg)
            don_ref[...] += _colsum(don * oh)
            do_scr[hh] = _norm_bwd(don, oh, r, w)
        row = lax.broadcasted_iota(jnp.int32, (C, HGRN_DK), 0)

        def chunk(cc, carry):
            c = nch - 1 - cc
            r0 = pl.multiple_of(c * C, C)
            rows = pl.ds(r0, C)
            last_row = pl.ds(r0 + C - 1, 1)
            heads = []
            for hh in range(HB):
                q, k, v, cu, do = q_scr[hh, rows, :], k_scr[hh, rows, :], v_scr[hh, rows, :], cum_scr[hh, rows, :], do_scr[hh, rows, :]
                st = st_ref[c, hh]
                dst = ds_scr[hh]
                last = cum_scr[hh, last_row, :]
                lam, gam, elast = jnp.exp(cu), jnp.exp(last - cu), jnp.exp(last)
                dob, dstb = do.astype(BF), dst.astype(BF)
                dq = jnp.dot(dob, st.astype(BF), preferred_element_type=F32) * lam
                dv = lax.dot_general((k * gam).astype(BF), dstb, NT, preferred_element_type=F32)
                dk = jnp.dot(v.astype(BF), dstb, preferred_element_type=F32) * gam
                dlast = elast * _colsum(dst * st) + _colsum(dk * k)
                ds_scr[hh] = dst * elast + lax.dot_general(dob, (q * lam).astype(BF), TN, preferred_element_type=F32)
                heads.append((hh, q, k, cu, do, dq, dk, dv, dlast))
            for hh, q, k, cu, do, dq, dk, dv, dlast in heads:
                for first in (True, False):
                    sl = slice(0 if first else HALF, C)
                    qs, cus, dos, rws = q[sl], cu[sl], do[sl], row[sl]
                    dqs, dks, dvs = dq[sl], dk[sl], dv[sl]
                    for s in (range(HALF) if first else range(HALF, C)):
                        one = pl.ds(r0 + s, 1)
                        e = jnp.where(rws >= s, jnp.exp(jnp.minimum(cus - cum_scr[hh, one, :], 0.0)), 0.0)
                        ks = k_scr[hh, one, :]
                        da = jnp.sum(dos * v_scr[hh, one, :], axis=-1, keepdims=True)
                        pq = qs * e
                        a = jnp.sum(pq * ks, axis=-1, keepdims=True)
                        dqs = dqs + da * e * ks
                        dks = jnp.where(rws == s, dks + _colsum(da * pq), dks)
                        dvs = jnp.where(rws == s, dvs + _colsum(a * dos), dvs)
                    if first:
                        dq, dk, dv = dqs, dks, dvs
                        top = pl.ds(r0, HALF)
                        dq_scr[hh, top, :] = dq[:HALF]
                        dk_scr[hh, top, :] = dk[:HALF]
                        dv_scr[hh, top, :] = dv[:HALF]
                        dcum_scr[hh, top, :] = q[:HALF] * dq[:HALF] - k[:HALF] * dk[:HALF]
                    else:
                        low = pl.ds(r0 + HALF, C - HALF)
                        dq_scr[hh, low, :] = dqs
                        dk_scr[hh, low, :] = dks
                        dv_scr[hh, low, :] = dvs
                        dcum_scr[hh, low, :] = qs * dqs - k[sl] * dks + jnp.where(rws == C - 1, dlast, 0.0)
            return carry

        lax.fori_loop(0, nch, chunk, 0)
        tri = _chunk_tri(tb, True)
        for hh in range(HB):
            cols = slice(hh * HGRN_DK, (hh + 1) * HGRN_DK)
            dlf = jnp.dot(tri, dcum_scr[hh], precision=HI, preferred_element_type=F32)
            dk = dk_scr[hh]
            lbh, sigh, nsigh, fh = lb[:, cols], sig[:, cols], nsig[:, cols], f[:, cols]
            common = (1.0 - lbh) * sigh * nsigh
            df_ref[:, cols] = (dlf * common / fh - dk * common).astype(BF)
            dq_ref[:, cols] = dq_scr[hh].astype(BF)
            dv_ref[:, cols] = dv_scr[hh].astype(BF)
            dl0 = _colsum(dlf * nsigh / fh - dk * nsigh) * lbh * (1.0 - lbh)
            dl_ref[:, cols] += jnp.where(lax.broadcasted_iota(jnp.int32, (2, HGRN_DK), 0) == 0, dl0, -dl0)

    def zs(part):
        return pl.BlockSpec((tb, HB * HGRN_DK), lambda hd, i: (nblk - 1 - i, part * (H // HB) + hd))

    blk = pl.BlockSpec((tb, HB * HGRN_DK), lambda hd, i: (nblk - 1 - i, hd))
    bft = jax.ShapeDtypeStruct((T, W), BF)
    scr = pltpu.VMEM((HB, tb, HGRN_DK), F32)
    return pl.pallas_call(
        body, name=name, grid=(H // HB, nblk),
        in_specs=[blk, zs(0), zs(1), zs(2), zs(3), blk,
                  pl.BlockSpec((nch, HB, HGRN_DK, HGRN_DK), lambda hd, i: (nblk - 1 - i, hd, 0, 0)),
                  pl.BlockSpec((2, HB * HGRN_DK), lambda hd, i: (0, hd)), pl.BlockSpec((1, HGRN_DK), lambda hd, i: (0, 0))],
        out_specs=[blk, blk, blk, blk, pl.BlockSpec((2, HB * HGRN_DK), lambda hd, i: (0, hd)),
                   pl.BlockSpec((1, HGRN_DK), lambda hd, i: (0, 0))],
        out_shape=[bft, bft, bft, bft, jax.ShapeDtypeStruct((2, W), F32), jax.ShapeDtypeStruct((1, HGRN_DK), F32)],
        scratch_shapes=[pltpu.VMEM((HB, HGRN_DK, HGRN_DK), F32), scr, scr, scr, scr, scr, scr, scr, scr, scr],
        compiler_params=_params(2),
    )(dx, z, z, z, z, o, states, lb_logits, out_norm)


def _log_sigmoid(x):
    return jnp.minimum(x, 0.0) - jnp.log(1.0 + jnp.exp(-jnp.abs(x)))


def _tri(n, upper):
    r = lax.broadcasted_iota(jnp.int32, (n, n), 0)
    c = lax.broadcasted_iota(jnp.int32, (n, n), 1)
    return ((r >= c) if upper else (c <= r)).astype(F32)


def fox_cum_fwd(kvf, fcol, b_row, name):
    T = kvf.shape[0]
    tb = _tile(T, 512)

    def body(x_ref, b_ref, ct_ref, carry_ref):
        @pl.when(pl.program_id(0) == 0)
        def _():
            carry_ref[...] = jnp.zeros_like(carry_ref)

        lf = _log_sigmoid(x_ref[...] + b_ref[...])
        cum = jnp.dot(_tri(tb, False), lf, precision=HI, preferred_element_type=F32) + carry_ref[...]
        carry_ref[...] += _colsum(lf)
        ct_ref[...] = cum.T

    return pl.pallas_call(
        body, name=name, grid=(T // tb,),
        in_specs=[pl.BlockSpec((tb, LANES), lambda i: (i, fcol)), pl.BlockSpec((1, LANES), lambda i: (0, 0))],
        out_specs=pl.BlockSpec((LANES, tb), lambda i: (0, i)), out_shape=jax.ShapeDtypeStruct((LANES, T), F32),
        scratch_shapes=[pltpu.VMEM((1, LANES), F32)], compiler_params=_params(1),
    )(kvf, b_row)


def fox_cum_bwd(dct, kvf, fcol, b_row, name):
    T = kvf.shape[0]
    tb = _tile(T, 512)
    nblk = T // tb

    def body(dc_ref, x_ref, b_ref, df_ref, db_ref, carry_ref):
        @pl.when(pl.program_id(0) == 0)
        def _():
            carry_ref[...] = jnp.zeros_like(carry_ref)
            db_ref[...] = jnp.zeros_like(db_ref)

        dc = dc_ref[...]
        dlf_t = jnp.dot(dc, _tri(tb, True), precision=HI, preferred_element_type=F32) + carry_ref[...]
        carry_ref[...] += jnp.sum(dc, axis=1, keepdims=True)
        x = x_ref[...] + b_ref[...]
        df = dlf_t.T * _sigmoid(-x)
        df_ref[...] = df.astype(BF)
        db_ref[...] += _colsum(df)

    return pl.pallas_call(
        body, name=name, grid=(nblk,),
        in_specs=[pl.BlockSpec((LANES, tb), lambda i: (0, nblk - 1 - i)),
                  pl.BlockSpec((tb, LANES), lambda i: (nblk - 1 - i, fcol)), pl.BlockSpec((1, LANES), lambda i: (0, 0))],
        out_specs=[pl.BlockSpec((tb, LANES), lambda i: (nblk - 1 - i, 0)), pl.BlockSpec((1, LANES), lambda i: (0, 0))],
        out_shape=[jax.ShapeDtypeStruct((T, LANES), BF), jax.ShapeDtypeStruct((1, LANES), F32)],
        scratch_shapes=[pltpu.VMEM((LANES, 1), F32)], compiler_params=_params(1),
    )(dct, kvf, b_row)


NAUG = 3


def fox_prep(kvf, ct, n_fox, tk):
    T = kvf.shape[0]
    W = n_fox * FOX_HEAD_DIM
    NP = n_fox // 2
    k = kvf[:, :W].astype(BF).reshape(T, NP, 2, FOX_HEAD_DIM)
    c = ct[:n_fox].T.reshape(T, NP, 2)
    hi = lax.reduce_precision(c, 8, 7)
    mid = lax.reduce_precision(c - hi, 8, 7)
    lo = c - hi - mid
    aug = jnp.stack([hi, mid, lo], axis=-1).astype(BF)
    pad = jnp.zeros((T, NP, FOX_HEAD_DIM - NAUG), BF)
    ka = jnp.concatenate([k[:, :, 0], aug[:, :, 0], pad], axis=-1).reshape(T, W)
    kb = jnp.concatenate([aug[:, :, 1], pad, k[:, :, 1]], axis=-1).reshape(T, W)
    v = kvf[:, W:2 * W].astype(BF)
    vt3 = v.reshape(T // tk, tk, NP, LANES).transpose(2, 0, 3, 1)
    return ka, kb, vt3


def fox_fwd(qg, ka, kb, vt3, name):
    T = qg.shape[0]
    W = qg.shape[1] // 2
    NP = W // LANES
    tq = tk = vt3.shape[3]
    scale = FOX_HEAD_DIM ** -0.5
    nk = T // tk
    HD = FOX_HEAD_DIM

    def body(q_ref, g_ref, ka_ref, kb_ref, vt_ref, x_ref, o_ref, lse_ref):
        i = pl.program_id(1)
        lane = lax.broadcasted_iota(jnp.int32, (tq, LANES), 1)
        q2 = q_ref[...] * scale
        qa = jnp.where(lane < HD, q2, jnp.where(lane < HD + NAUG, -1.0, 0.0))
        qb = jnp.where(lane >= HD, q2, jnp.where(lane < NAUG, -1.0, 0.0))
        qts = (qa.T.astype(BF), qb.T.astype(BF))
        krow = lax.broadcasted_iota(jnp.int32, (tk, tq), 0)
        qcol = lax.broadcasted_iota(jnp.int32, (tk, tq), 1)

        def step(j, carry, diag):
            rows = pl.ds(pl.multiple_of(j * tk, tk), tk)
            ks = (ka_ref[rows, :], kb_ref[rows, :])
            vt = vt_ref[j]
            sts = [jnp.dot(ks[a], qts[a], preferred_element_type=F32) for a in range(2)]
            pts, mls = [], []
            for a in range(2):
                m, l, _ = carry[a]
                st = sts[a]
                if diag:
                    st = jnp.where(krow + (j * tk - i * tq) <= qcol, st, -1e30)
                mn = jnp.maximum(m, jnp.max(st, axis=0, keepdims=True))
                alpha = jnp.exp(m - mn)
                pt = jnp.exp(st - mn)
                mls.append((mn, l * alpha + jnp.sum(pt, axis=0, keepdims=True), alpha))
                pts.append(pt.astype(BF))
            out = []
            for a in range(2):
                mn, l, alpha = mls[a]
                acc = carry[a][2] * alpha + jnp.dot(vt[a * HD:(a + 1) * HD, :], pts[a], preferred_element_type=F32)
                out.append((mn, l, acc))
            return tuple(out)

        init = (jnp.full((1, tq), -1e30, F32), jnp.zeros((1, tq), F32), jnp.zeros((HD, tq), F32))
        r = tq // tk
        carry = lax.fori_loop(0, i * r, lambda j, c: step(j, c, False), (init, init))
        for u in range(r):
            carry = step(i * r + u, carry, True)
        (ma, la, acca), (mb, lb, accb) = carry
        ot = jnp.concatenate([acca / la, accb / lb], axis=0)
        o = ot.T
        o_ref[...] = o
        lse_ref[0:1, :] = ma + jnp.log(la)
        lse_ref[1:2, :] = mb + jnp.log(lb)
        x_ref[...] = (o * _sigmoid(g_ref[...])).astype(BF)

    blk = pl.BlockSpec((tq, LANES), lambda hp, i: (i, hp))
    full = pl.BlockSpec((T, LANES), lambda hp, i: (0, hp))
    return pl.pallas_call(
        body, name=name, grid=(NP, T // tq),
        in_specs=[blk, pl.BlockSpec((tq, LANES), lambda hp, i: (i, NP + hp)), full, full,
                  pl.BlockSpec((None, nk, LANES, tk), lambda hp, i: (hp, 0, 0, 0))],
        out_specs=[blk, blk, pl.BlockSpec((None, None, 2, tq), lambda hp, i: (hp, i, 0, 0))],
        out_shape=[jax.ShapeDtypeStruct((T, W), BF), jax.ShapeDtypeStruct((T, W), F32),
                   jax.ShapeDtypeStruct((NP, T // tq, 2, tq), F32)],
        compiler_params=_params(2),
    )(qg, qg, ka, kb, vt3)


def fox_gate_bwd(dx, o, qg, name):
    T, W = o.shape
    tm = _tile(T, 512)

    def body(dx_ref, o_ref, g_ref, do_ref, dg_ref, ds_ref):
        dxv = dx_ref[...].astype(F32)
        sg = _sigmoid(g_ref[...])
        do = dxv * sg
        o = o_ref[...]
        do_ref[...] = do
        dg_ref[...] = (dxv * o * sg * (1.0 - sg)).astype(BF)
        head = jnp.right_shift(lax.broadcasted_iota(jnp.int32, (W, LANES), 0), FOX_HEAD_DIM.bit_length() - 1)
        sel = (head == lax.broadcasted_iota(jnp.int32, (W, LANES), 1)).astype(F32)
        ds_ref[...] = jnp.dot(do * o, sel, precision=HI, preferred_element_type=F32)

    tok = pl.BlockSpec((tm, W), lambda i: (i, 0))
    return pl.pallas_call(
        body, name=name, grid=(T // tm,),
        in_specs=[tok, tok, pl.BlockSpec((tm, W), lambda i: (i, 1))],
        out_specs=[tok, tok, pl.BlockSpec((tm, LANES), lambda i: (i, 0))],
        out_shape=[jax.ShapeDtypeStruct((T, W), F32), jax.ShapeDtypeStruct((T, W), BF), jax.ShapeDtypeStruct((T, LANES), F32)],
        compiler_params=_params(1),
    )(dx, o, qg)


def fox_bwd_prep(qg, kvf, do, tq):
    T = qg.shape[0]
    W = qg.shape[1] // 2
    NP = W // LANES
    scale = FOX_HEAD_DIM ** -0.5
    tr3 = lambda a: a.reshape(T // tq, tq, NP, LANES).transpose(2, 0, 3, 1)
    q = (qg[:, :W] * scale).astype(BF)
    dob = do.astype(BF)
    k = kvf[:, :W]
    return q, tr3(q), dob, tr3(dob), tr3((k * scale).astype(BF)), kvf[:, W:2 * W].astype(BF)


def fox_bwd(q, qt3, dob, dot3, kt3, v, ka, kb, lse4, dsum4, name):
    T, W = q.shape
    NP = W // LANES
    tq = tk = qt3.shape[3]
    nq = T // tq
    HD = FOX_HEAD_DIM

    def body(q_ref, qt_ref, do_ref, dot_ref, kt_ref, v_ref, ka_ref, kb_ref, lse_ref, dsum_ref,
             dqt_ref, dk_ref, dv_ref, dc_ref, dcq_ref, dk_scr, dv_scr, dcl_scr):
        j = pl.program_id(1)

        @pl.when(j == 0)
        def _():
            dqt_ref[...] = jnp.zeros_like(dqt_ref)
            dcq_ref[...] = jnp.zeros_like(dcq_ref)

        dk_scr[...] = jnp.zeros_like(dk_scr)
        dv_scr[...] = jnp.zeros_like(dv_scr)
        dcl_scr[...] = jnp.zeros_like(dcl_scr)
        lane = lax.broadcasted_iota(jnp.int32, (tk, LANES), 1)
        srow = lax.broadcasted_iota(jnp.int32, (LANES, tq), 0)
        lanes_of = (lane < HD, lane >= HD)
        rows_of = (srow < HD, srow >= HD)
        v2 = v_ref[...]
        kt2 = kt_ref[...]
        zero = jnp.zeros((), BF)
        vs = [jnp.where(lanes_of[a], v2, zero) for a in range(2)]
        kts = [jnp.where(rows_of[a], kt2, zero) for a in range(2)]
        kaug = (ka_ref[...], kb_ref[...])
        krow = lax.broadcasted_iota(jnp.int32, (tk, tq), 0)
        qcol = lax.broadcasted_iota(jnp.int32, (tk, tq), 1)
        neg1 = jnp.full((), -1.0, BF)

        def step(i, carry, diag):
            rows = pl.ds(pl.multiple_of(i * tq, tq), tq)
            qt2 = qt_ref[i]
            dot2 = dot_ref[i]
            q2 = q_ref[rows, :]
            do2 = do_ref[rows, :]
            qts = [jnp.where(srow < HD, qt2, jnp.where(srow < HD + NAUG, neg1, zero)),
                   jnp.where(srow >= HD, qt2, jnp.where(srow < NAUG, neg1, zero))]
            sts = [jnp.dot(kaug[a], qts[a], preferred_element_type=F32) for a in range(2)]
            dps = [jnp.dot(vs[a], dot2, preferred_element_type=F32) for a in range(2)]
            pbs, dsbs = [], []
            for a in range(2):
                pt = jnp.exp(sts[a] - lse_ref[i, a:a + 1, :])
                if diag:
                    pt = jnp.where(krow <= qcol, pt, 0.0)
                ds = pt * (dps[a] - dsum_ref[i, a:a + 1, :])
                dcq_ref[i, a:a + 1, :] += _colsum(ds)
                part = ds[:, 0:LANES]
                for u in range(1, tq // LANES):
                    part = part + ds[:, u * LANES:(u + 1) * LANES]
                dcl_scr[a] += part
                pbs.append(pt.astype(BF))
                dsbs.append(ds.astype(BF))
            qn = [jnp.where(lanes_of[a], q2, zero) for a in range(2)]
            don = [jnp.where(lanes_of[a], do2, zero) for a in range(2)]
            dv_scr[...] += (jnp.dot(pbs[0], don[0], preferred_element_type=F32) +
                            jnp.dot(pbs[1], don[1], preferred_element_type=F32))
            dk_scr[...] += (jnp.dot(dsbs[0], qn[0], preferred_element_type=F32) +
                            jnp.dot(dsbs[1], qn[1], preferred_element_type=F32))
            dqt_ref[i] += (jnp.dot(kts[0], dsbs[0], preferred_element_type=F32) +
                           jnp.dot(kts[1], dsbs[1], preferred_element_type=F32))
            return carry

        step(j, 0, True)
        lax.fori_loop(j + 1, nq, lambda i, c: step(i, c, False), 0)
        dk_ref[...] = dk_scr[...].astype(BF)
        dv_ref[...] = dv_scr[...].astype(BF)
        for a in range(2):
            dc_ref[a:a + 1, :] = -_colsum(dcl_scr[a].T)

    tile = pl.BlockSpec((tk, LANES), lambda hp, j: (j, hp))
    full = pl.BlockSpec((T, LANES), lambda hp, j: (0, hp))
    full3 = pl.BlockSpec((None, nq, LANES, tq), lambda hp, j: (hp, 0, 0, 0))
    rows4 = pl.BlockSpec((None, nq, 2, tq), lambda hp, j: (hp, 0, 0, 0))
    bft = jax.ShapeDtypeStruct((T, W), BF)
    r4 = jax.ShapeDtypeStruct((NP, nq, 2, tq), F32)
    return pl.pallas_call(
        body, name=name, grid=(NP, nq),
        in_specs=[full, full3, full, full3, pl.BlockSpec((None, None, LANES, tk), lambda hp, j: (hp, j, 0, 0)),
                  tile, tile, tile, rows4, rows4],
        out_specs=[full3, tile, tile, pl.BlockSpec((None, None, 2, tk), lambda hp, j: (hp, j, 0, 0)), rows4],
        out_shape=[jax.ShapeDtypeStruct((NP, nq, LANES, tq), F32), bft, bft, r4, r4],
        scratch_shapes=[pltpu.VMEM((tk, LANES), F32), pltpu.VMEM((tk, LANES), F32), pltpu.VMEM((2, tk, LANES), F32)],
        compiler_params=_params(2),
    )(q, qt3, dob, dot3, kt3, v, ka, kb, lse4, dsum4)


def loss_fwd_bwd(y, target, name):
    T, D = y.shape
    tm = _tile(T, 512)

    def body(y_ref, t_ref, dy_ref, l_ref):
        @pl.when(pl.program_id(0) == 0)
        def _():
            l_ref[...] = jnp.zeros_like(l_ref)

        d = y_ref[...] - t_ref[...]
        dy_ref[...] = d * (1.0 / D)
        l_ref[...] += 0.5 * jnp.sum(jnp.mean(d * d, axis=-1, keepdims=True), axis=0, keepdims=True)

    tok = pl.BlockSpec((tm, D), lambda i: (i, 0))
    return pl.pallas_call(
        body, name=name, grid=(T // tm,), in_specs=[tok, tok],
        out_specs=[tok, pl.BlockSpec((1, 1), lambda i: (0, 0))],
        out_shape=[jax.ShapeDtypeStruct((T, D), F32), jax.ShapeDtypeStruct((1, 1), F32)], compiler_params=_params(1),
    )(y, target)


def adamw(parts, w, m, v, layer, prev, name):
    L, R, C = w.shape
    tr = _tile(R, 256)
    c1 = 1.0 / (1.0 - ADAM_B1 ** ADAM_STEP)
    c2 = 1.0 / (1.0 - ADAM_B2 ** ADAM_STEP)

    def body(p_ref, w_ref, m_ref, v_ref, *rest):
        g_ref, d_ref, mo_ref, vo_ref = rest[-4:]
        g = p_ref[0].astype(F32)
        for d in range(1, N_DEV):
            g = g + p_ref[d].astype(F32)
        mn = ADAM_B1 * m_ref[...] + (1.0 - ADAM_B1) * g
        vn = ADAM_B2 * v_ref[...] + (1.0 - ADAM_B2) * (g * g)
        g_ref[...] = g
        mo_ref[...] = mn
        vo_ref[...] = vn
        d_ref[...] = -ADAM_LR * ((mn * c1) / (jnp.sqrt(vn * c2) + ADAM_EPS) + ADAM_WD * w_ref[...])

    blk = pl.BlockSpec((None, tr, C), lambda i: (layer, i, 0))
    shp = jax.ShapeDtypeStruct((L, R, C), F32)
    in_specs = [pl.BlockSpec((N_DEV, tr, C), lambda i: (0, i, 0)), blk, blk, blk]
    args = [parts, w, m, v]
    aliases = {}
    if prev is not None:
        in_specs += [pl.BlockSpec(memory_space=pl.ANY)] * 4
        args += list(prev)
        aliases = {4 + j: j for j in range(4)}
    return pl.pallas_call(
        body, name=name, grid=(R // tr,), in_specs=in_specs, out_specs=[blk, blk, blk, blk],
        out_shape=[shp, shp, shp, shp], input_output_aliases=aliases, compiler_params=_params(1),
    )(*args)


HBM_SPEC = pl.BlockSpec(memory_space=pltpu.HBM)
SEM_SPEC = pl.BlockSpec(memory_space=pltpu.SEMAPHORE)
EFFECT = pltpu.SideEffectType.DATAFLOW_SIDE_EFFECTING


def _mesh_pos():
    return lax.axis_index("x"), lax.axis_index("y"), lax.axis_index("c")


def _flip(v, bit):
    return v + bit - 2 * v * bit


def _peer(pos, delta):
    x, y, c = pos
    px, py, pc = _flip(x, (delta >> 2) & 1), _flip(y, (delta >> 1) & 1), _flip(c, delta & 1)
    return (px, py, pc), 4 * px + 2 * py + pc


def _me():
    x, y, c = _mesh_pos()
    return 4 * x + 2 * y + c


def _copies(src_refs, land_refs, whole, send, recv, incoming):
    pos = _mesh_pos()
    me = 4 * pos[0] + 2 * pos[1] + pos[2]
    out = []
    for k in range(len(src_refs)):
        for d in range(1, N_DEV):
            dev, idx = _peer(pos, d)
            j = k * (N_DEV - 1) + d - 1
            src = src_refs[k] if whole[k] else src_refs[k].at[idx]
            out.append(pltpu.make_async_remote_copy(
                src_ref=src, dst_ref=land_refs[k].at[idx if incoming else me], send_sem=send.at[j], recv_sem=recv.at[j],
                device_id=dev, device_id_type=pl.DeviceIdType.MESH))
    return out


def exchange_start(srcs, lands, whole, name):
    n = len(srcs)

    def body(*refs):
        for copy in _copies(refs[:n], refs[n:2 * n], whole, refs[2 * n], refs[2 * n + 1], False):
            copy.start()
        refs[-1][...] = jnp.zeros_like(refs[-1])

    sems = pltpu.SemaphoreType.DMA((n * (N_DEV - 1),))
    thru = [pltpu.HBM(a.shape, a.dtype) for a in list(srcs) + list(lands)]
    res = pl.pallas_call(
        body, name=name, in_specs=[HBM_SPEC] * (2 * n),
        out_specs=[SEM_SPEC, SEM_SPEC] + [HBM_SPEC] * (2 * n) + [pl.BlockSpec(memory_space=pltpu.VMEM)],
        out_shape=[sems, sems] + thru + [jax.ShapeDtypeStruct((8, LANES), F32)],
        input_output_aliases={j: 2 + j for j in range(2 * n)},
        compiler_params=pltpu.CompilerParams(has_side_effects=EFFECT),
    )(*[pltpu.with_memory_space_constraint(a, pltpu.HBM) for a in list(srcs) + list(lands)])
    return dict(send=res[0], recv=res[1], srcs=res[2:2 + n], lands=res[2 + n:2 + 2 * n], whole=whole, token=res[-1])


def exchange_wait(handle, after, name):
    n = len(handle["srcs"])
    whole = handle["whole"]

    def body(*refs):
        for copy in _copies(refs[:n], refs[n:2 * n], whole, refs[2 * n], refs[2 * n + 1], False):
            copy.wait_send()
        for copy in _copies(refs[:n], refs[n:2 * n], whole, refs[2 * n], refs[2 * n + 1], True):
            copy.wait_recv()

    bufs = list(handle["srcs"]) + list(handle["lands"])
    res = pl.pallas_call(
        body, name=name, in_specs=[HBM_SPEC] * (2 * n) + [SEM_SPEC, SEM_SPEC] + [pl.BlockSpec(memory_space=pl.ANY)] * len(after),
        out_specs=[HBM_SPEC] * (2 * n), out_shape=[pltpu.HBM(a.shape, a.dtype) for a in bufs],
        input_output_aliases={j: j for j in range(2 * n)},
        compiler_params=pltpu.CompilerParams(has_side_effects=EFFECT),
    )(*bufs, handle["send"], handle["recv"], *after)
    return list(res[n:])


def _landing(own, whole):
    me = _me()
    if not whole:
        own = lax.dynamic_index_in_dim(own, me, 0, keepdims=False)
    buf = lax.empty((N_DEV,) + own.shape, own.dtype)
    return lax.dynamic_update_slice(buf, own[None], (me,) + (0,) * own.ndim)


def _row(v):
    return v.reshape(1, -1)


def _pad_lanes(v, n):
    return jnp.pad(v, ((0, 0), (0, n - v.shape[1])))


TRANSPOSED = ("ffn1_w_in", "ffn2_w_in")

GATHER_GROUPS = (
    ("ffn1_in_0", (("ffn1_w_in", 0),)),
    ("ffn1_out_0", (("ffn1_w_out", 0),)),
    ("hgrn", (("hgrn_w_in", 0), ("hgrn_w_out", 0))),
    ("rest_0", (("ffn2_w_in", 0), ("ffn2_w_out", 0), ("ple_w_gate", 0), ("ple_w_proj", 0), ("fox_w_kvf", 0))),
    ("ffn1_1", (("ffn1_w_in", 1), ("ffn1_w_out", 1))),
    ("fox", (("fox_w_qg", 0), ("fox_w_out", 0))),
    ("rest_1", (("ffn2_w_in", 1), ("ffn2_w_out", 1), ("ple_w_gate", 1), ("ple_w_proj", 1))),
)


def kernel(x, p, ffn1_norm_pre, ffn1_w_in, ffn1_w_out, ffn1_norm_post, mix_norm_pre, mix_norm_post, ffn2_norm_pre, ffn2_w_in, ffn2_w_out, ffn2_norm_post, hgrn_w_in, hgrn_lb_logits, hgrn_out_norm, hgrn_w_out, kv_norm, fox_w_kvf, fox_b_f, fox_w_qg, fox_w_out, ple_norm_pre, ple_w_gate, ple_w_proj, ple_norm_post, loss_target, m_ffn1_norm_pre, m_ffn1_w_in, m_ffn1_w_out, m_ffn1_norm_post, m_mix_norm_pre, m_mix_norm_post, m_ffn2_norm_pre, m_ffn2_w_in, m_ffn2_w_out, m_ffn2_norm_post, m_hgrn_w_in, m_hgrn_lb_logits, m_hgrn_out_norm, m_hgrn_w_out, m_kv_norm, m_fox_w_kvf, m_fox_b_f, m_fox_w_qg, m_fox_w_out, m_ple_norm_pre, m_ple_w_gate, m_ple_w_proj, m_ple_norm_post, v_ffn1_norm_pre, v_ffn1_w_in, v_ffn1_w_out, v_ffn1_norm_post, v_mix_norm_pre, v_mix_norm_post, v_ffn2_norm_pre, v_ffn2_w_in, v_ffn2_w_out, v_ffn2_norm_post, v_hgrn_w_in, v_hgrn_lb_logits, v_hgrn_out_norm, v_hgrn_w_out, v_kv_norm, v_fox_w_kvf, v_fox_b_f, v_fox_w_qg, v_fox_w_out, v_ple_norm_pre, v_ple_w_gate, v_ple_w_proj, v_ple_norm_post):
    weights = dict(ffn1_norm_pre=ffn1_norm_pre, ffn1_w_in=ffn1_w_in, ffn1_w_out=ffn1_w_out, ffn1_norm_post=ffn1_norm_post, mix_norm_pre=mix_norm_pre, mix_norm_post=mix_norm_post, ffn2_norm_pre=ffn2_norm_pre, ffn2_w_in=ffn2_w_in, ffn2_w_out=ffn2_w_out, ffn2_norm_post=ffn2_norm_post, hgrn_w_in=hgrn_w_in, hgrn_lb_logits=hgrn_lb_logits, hgrn_out_norm=hgrn_out_norm, hgrn_w_out=hgrn_w_out, kv_norm=kv_norm, fox_w_kvf=fox_w_kvf, fox_b_f=fox_b_f, fox_w_qg=fox_w_qg, fox_w_out=fox_w_out, ple_norm_pre=ple_norm_pre, ple_w_gate=ple_w_gate, ple_w_proj=ple_w_proj, ple_norm_post=ple_norm_post)
    mom1 = dict(ffn1_norm_pre=m_ffn1_norm_pre, ffn1_w_in=m_ffn1_w_in, ffn1_w_out=m_ffn1_w_out, ffn1_norm_post=m_ffn1_norm_post, mix_norm_pre=m_mix_norm_pre, mix_norm_post=m_mix_norm_post, ffn2_norm_pre=m_ffn2_norm_pre, ffn2_w_in=m_ffn2_w_in, ffn2_w_out=m_ffn2_w_out, ffn2_norm_post=m_ffn2_norm_post, hgrn_w_in=m_hgrn_w_in, hgrn_lb_logits=m_hgrn_lb_logits, hgrn_out_norm=m_hgrn_out_norm, hgrn_w_out=m_hgrn_w_out, kv_norm=m_kv_norm, fox_w_kvf=m_fox_w_kvf, fox_b_f=m_fox_b_f, fox_w_qg=m_fox_w_qg, fox_w_out=m_fox_w_out, ple_norm_pre=m_ple_norm_pre, ple_w_gate=m_ple_w_gate, ple_w_proj=m_ple_w_proj, ple_norm_post=m_ple_norm_post)
    mom2 = dict(ffn1_norm_pre=v_ffn1_norm_pre, ffn1_w_in=v_ffn1_w_in, ffn1_w_out=v_ffn1_w_out, ffn1_norm_post=v_ffn1_norm_post, mix_norm_pre=v_mix_norm_pre, mix_norm_post=v_mix_norm_post, ffn2_norm_pre=v_ffn2_norm_pre, ffn2_w_in=v_ffn2_w_in, ffn2_w_out=v_ffn2_w_out, ffn2_norm_post=v_ffn2_norm_post, hgrn_w_in=v_hgrn_w_in, hgrn_lb_logits=v_hgrn_lb_logits, hgrn_out_norm=v_hgrn_out_norm, hgrn_w_out=v_hgrn_w_out, kv_norm=v_kv_norm, fox_w_kvf=v_fox_w_kvf, fox_b_f=v_fox_b_f, fox_w_qg=v_fox_w_qg, fox_w_out=v_fox_w_out, ple_norm_pre=v_ple_norm_pre, ple_w_gate=v_ple_w_gate, ple_w_proj=v_ple_w_proj, ple_norm_post=v_ple_norm_post)
    names = list(weights)
    big = ["ffn1_w_in", "ffn1_w_out", "ffn2_w_in", "ffn2_w_out", "hgrn_w_in", "hgrn_w_out", "fox_w_kvf", "fox_w_qg",
           "fox_w_out", "ple_w_gate", "ple_w_proj"]
    small_names = [n for n in names if n not in big]

    T, D = x.shape[1], x.shape[2]
    depth = p.shape[0]
    h0 = x.reshape(T, D)
    target = loss_target.reshape(T, D)
    p3 = p.reshape(depth, T, p.shape[3])
    n_fox = fox_b_f.shape[0]
    fox_w = n_fox * FOX_HEAD_DIM
    fcol = 2 * fox_w // LANES
    b_row = _pad_lanes(_row(fox_b_f), LANES)

    tok = jnp.zeros((), F32)
    handles = {}
    for gname, keys in GATHER_GROUPS:
        shards = []
        for n, l in keys:
            w = weights[n]
            w = w[l] if w.ndim == 3 else w
            shards.append(((w.T if n in TRANSPOSED else w) + tok).astype(BF))
        handles[gname] = exchange_start(shards, [_landing(s, True) for s in shards], [True] * len(keys), f"gather_start_{gname}")
        tok = handles[gname]["token"][0, 0]
    W = {}

    def arrive(gname, after):
        lands = exchange_wait(handles[gname], after, f"gather_wait_{gname}")
        W.update(dict(zip(dict(GATHER_GROUPS)[gname], lands)))

    def w_rows(n, l):
        return W[n, l].reshape(-1, D)

    norm = lambda name, i: weights[name][i:i + 1]

    saved = []
    h = h0
    kvf = ct = kvf_w = None
    tq = _tile(T, 512)
    for i in range(depth):
        s = {}
        if i == 0:
            arrive("ffn1_in_0", [handles[GATHER_GROUPS[-1][0]]["token"]])
        for k in (1, 2):
            if k == 2:
                s["h_a"] = h
                if i == 0:
                    arrive("hgrn", [h])
                    z, xn = norm_mm(h, norm("mix_norm_pre", i), W["hgrn_w_in", 0], "hgrn_in")
                    xm, o, states = hgrn_fwd(z, hgrn_lb_logits, hgrn_out_norm, "hgrn_scan")
                    s.update(z=z, o=o, states=states)
                    wmix = w_rows("hgrn_w_out", 0)
                else:
                    arrive("fox", [h])
                    qg, xn = norm_mm(h, norm("mix_norm_pre", i), W["fox_w_qg", 0], "fox_qg")
                    ka, kb, vt3 = fox_prep(kvf, ct, n_fox, tq)
                    xm, o, lse = fox_fwd(qg, ka, kb, vt3, "fox_attn")
                    s.update(qg=qg, o=o, lse=lse, ka=ka, kb=kb)
                    wmix = w_rows("fox_w_out", 0)
                h, ym = mm_norm_res(xm, wmix, norm("mix_norm_post", i), h, 1.0, f"mix_out_{i}", kb=_tile(xm.shape[1], 512))
                s.update(xm=xm, ym=ym, xn_mix=xn)
                arrive(f"rest_{i}", [h])
            gate, up, a, xn = norm_mm_swiglu(h, norm(f"ffn{k}_norm_pre", i), W[f"ffn{k}_w_in", i], f"ffn{k}_in_{i}")
            if (i, k) == (0, 1):
                arrive("ffn1_out_0", [a])
            hn, y = mm_norm_res(a, w_rows(f"ffn{k}_w_out", i), norm(f"ffn{k}_norm_post", i), h, 0.5, f"ffn{k}_out_{i}")
            s[f"ffn{k}"] = (h, gate, up, a, y, xn)
            h = hn
        s["h_c"] = h
        ple_proj = W["ple_w_proj", i].transpose(1, 0, 2).reshape(p.shape[3], D)
        h, pgate, pp = ple_fwd(h, norm("ple_norm_pre", i), w_rows("ple_w_gate", i), p3, ple_proj, norm("ple_norm_post", i),
                               i, f"ple_{i}")
        s.update(pgate=pgate, pp=pp)
        saved.append(s)
        if i == 0:
            kvf_nat = W["fox_w_kvf", 0].transpose(1, 0, 2).reshape(D, -1)
            kvf_cols = kvf_nat.shape[1]
            kvf_w = _pad_lanes(kvf_nat, 2 * fox_w + LANES)
            kvf, xn_kv = norm_mm(h, _row(kv_norm), kvf_w, "fox_kvf", tn=kvf_w.shape[1])
            ct = fox_cum_fwd(kvf, fcol, b_row, "fox_cum")
            h_kv = h
            arrive("ffn1_1", [h])

    dh, loss_part = loss_fwd_bwd(h, target, "loss")
    loss = lax.psum(loss_part[0, 0], ("x", "y", "c"))

    gsmall = {n: [None] * weights[n].shape[0] if weights[n].ndim == 2 else None for n in small_names}
    sent = []

    def send(gname, keys, srcs, whole):
        lands = [_landing(a, w) for a, w in zip(srcs, whole)]
        hd = exchange_start(srcs, lands, whole, f"scatter_start_{gname}")
        sent.append((gname, keys, hd))
        return hd["token"][0:1, 0:1]

    def send_grads(gname, grads):
        return send(gname, list(grads), list(grads.values()), [False] * len(grads))

    for i in reversed(range(depth)):
        s = saved[i]
        grads = {}
        dh, du, dpp, xn, dgpre, dgpost = ple_bwd(dh, s["h_c"], s["pgate"], s["pp"], norm("ple_norm_pre", i),
                                                 w_rows("ple_w_gate", i), norm("ple_norm_post", i), f"ple_bwd_{i}")
        gsmall["ple_norm_pre"][i], gsmall["ple_norm_post"][i] = dgpre, dgpost
        grads["ple_w_gate", i] = mm_tn(xn, du, f"ple_dgate_{i}", xb=_tile(D, 512)).reshape(N_DEV, -1, D)
        dproj = mm_tn(p3, dpp, f"ple_dproj_{i}", x_layer=i, yb=D)[0]
        grads["ple_w_proj", i] = dproj.reshape(dproj.shape[0], N_DEV, -1).transpose(1, 0, 2)
        for k in (2, 1):
            hin, gate, up, a, y, xn = s[f"ffn{k}"]
            ffn_cs = gate.shape[2]
            dy, dgpost, dz = nbwd_mm_nt(dh, y, norm(f"ffn{k}_norm_post", i), w_rows(f"ffn{k}_w_out", i), 0.5, ffn_cs,
                                        f"ffn{k}_bwd_out_{i}", gate=gate, up=up)
            dz = dz.reshape(-1, T, ffn_cs)
            grads[f"ffn{k}_w_out", i] = mm_tn(a, dy, f"ffn{k}_dwout_{i}").reshape(N_DEV, -1, D)
            grads[f"ffn{k}_w_in", i] = mm_tn(dz, xn, f"ffn{k}_dwin_{i}").reshape(N_DEV, ffn_cs, D)
            tokv = send_grads(f"ffn{k}_{i}", grads)
            grads = {}
            dh, dgpre = mm_nt_nbwd(dz, W[f"ffn{k}_w_in", i], hin, norm(f"ffn{k}_norm_pre", i) + tokv, dh, f"ffn{k}_bwd_in_{i}",
                                   transposed=True)
            gsmall[f"ffn{k}_norm_pre"][i], gsmall[f"ffn{k}_norm_post"][i] = dgpre, dgpost
            if k == 2:
                nm = "hgrn" if i == 0 else "fox"
                wmix = w_rows(f"{nm}_w_out", 0)
                dy, dgpost, dxm = nbwd_mm_nt(dh, s["ym"], norm("mix_norm_post", i), wmix, 1.0, _tile(wmix.shape[0], 512),
                                             f"{nm}_bwd_out")
                grads[f"{nm}_w_out", 0] = mm_tn(s["xm"], dy, f"{nm}_dwout", xb=_tile(wmix.shape[0], 512)).reshape(N_DEV, -1, D)
                gsmall["mix_norm_post"][i] = dgpost
                if i == 0:
                    dq, df, dv, dg, dlog, don = hgrn_bwd(dxm, s["z"], s["o"], s["states"], hgrn_lb_logits, hgrn_out_norm,
                                                         "hgrn_scan_bwd")
                    gsmall["hgrn_lb_logits"] = [dlog[0:1], dlog[1:2]]
                    gsmall["hgrn_out_norm"] = [don]
                    dzm = jnp.concatenate([dq, df, dv, dg], axis=1)
                    nmin = "hgrn_w_in"
                else:
                    do, dg, dsum = fox_gate_bwd(dxm, s["o"], s["qg"], "fox_gate_bwd")
                    dsum4 = dsum[:, :n_fox].T.reshape(n_fox // 2, 2, T // tq, tq).transpose(0, 2, 1, 3)
                    dqt, dk_sh, dv_sh, dc4, dcq4 = fox_bwd(*fox_bwd_prep(s["qg"], kvf, do, tq), s["ka"], s["kb"], s["lse"], dsum4,
                                                           "fox_attn_bwd")
                    dq = dqt.transpose(1, 3, 0, 2).reshape(T, fox_w)
                    dzm = jnp.concatenate([dq.astype(BF), dg], axis=1)
                    nmin = "fox_w_qg"
                wmin = W[nmin, 0]
                grads[nmin, 0] = mm_tn(s["xn_mix"], dzm, f"{nm}_dwin", yb=wmin.shape[2])
                tokv = send_grads(f"mix_{i}", grads)
                grads = {}
                dh, dgpre = mm_nt_nbwd(dzm, wmin, s["h_a"], norm("mix_norm_pre", i) + tokv, dh, f"{nm}_bwd_in", tn=wmin.shape[2])
                gsmall["mix_norm_pre"][i] = dgpre
        if i == 1:
            dct = (dc4 + dcq4).transpose(0, 2, 1, 3).reshape(n_fox, T)
            dct = jnp.pad(dct, ((0, LANES - n_fox), (0, 0)))
            dflog, db = fox_cum_bwd(dct, kvf, fcol, b_row, "fox_cum_bwd")
            gsmall["fox_b_f"] = db[:, :n_fox]
            dkvf = jnp.concatenate([dk_sh, dv_sh, dflog], axis=1)
            dwk = mm_tn(xn_kv, dkvf, "fox_dwkvf", yb=kvf_w.shape[1])[0][:, :kvf_cols]
            tokv = send_grads("kvf", {("fox_w_kvf", 0): dwk.reshape(D, N_DEV, -1).transpose(1, 0, 2)})
            dh, dgkv = mm_nt_nbwd(dkvf, kvf_w, h_kv, _row(kv_norm) + tokv, dh, "fox_kvf_bwd", tn=kvf_w.shape[1])
            gsmall["kv_norm"] = dgkv
    grad_x = dh.reshape(x.shape)

    def small_rows(n):
        g = gsmall[n]
        rows = g if isinstance(g, list) else [g]
        return [_pad_lanes(r, D) for r in rows]

    counts = {n: len(small_rows(n)) for n in small_names}
    packed = jnp.concatenate([r for n in small_names for r in small_rows(n)], axis=0)
    n_rows = packed.shape[0]
    packed = jnp.pad(packed, ((0, -n_rows % 8), (0, 0)))
    send("small", ["small"], [packed], [True])

    res = {}
    after = [dh]
    small_parts = None
    for gname, keys, hd in sent:
        lands = exchange_wait(hd, after, f"scatter_wait_{gname}")
        after = []
        for key, parts in zip(keys, lands):
            if key == "small":
                small_parts = parts
                continue
            n, l = key
            w = weights[n]
            if w.ndim == 2:
                as3 = lambda a: a.reshape((1,) + a.shape)
            elif n in TRANSPOSED:
                as3 = lambda a: a.transpose(0, 2, 1)
            else:
                as3 = lambda a: a
            res[n] = adamw(parts, as3(w), as3(mom1[n]), as3(mom2[n]), l, res.get(n), f"adamw_{n}_{l}")
            after.append(res[n][1])
    for n in TRANSPOSED:
        res[n] = [a.transpose(0, 2, 1) for a in res[n]]

    def pack(d):
        rows = []
        for n in small_names:
            a = d[n]
            rows.append(_pad_lanes(a.reshape(-1, a.shape[-1]), D))
        a = jnp.concatenate(rows, axis=0)
        return jnp.pad(a, ((0, -n_rows % 8), (0, 0)))[None]

    sm = adamw(small_parts, pack(weights), pack(mom1), pack(mom2), 0, None, "adamw_small")
    off = 0
    for n in small_names:
        w = weights[n]
        res[n] = [a[0, off:off + counts[n], :w.shape[-1]].reshape(w.shape) for a in sm]
        off += counts[n]

    out = [loss, grad_x]
    for j in range(4):
        out += [res[n][j].reshape(weights[n].shape) for n in names]
    return tuple(out)
```

```python
import functools

import jax
import jax.numpy as jnp
from jax import lax
from jax.experimental import pallas as pl
from jax.experimental.pallas import tpu as pltpu

F32 = jnp.float32
BF = jnp.bfloat16
NORM_EPS = 1e-6
N_DEV = 8
HGRN_DK = 128
HGRN_CHUNK = 16
HGRN_HEADS_PER_STEP = 4
HALF = HGRN_CHUNK // 2
FOX_HEAD_DIM = 64
LANES = 128
ADAM_LR, ADAM_B1, ADAM_B2, ADAM_EPS, ADAM_WD, ADAM_STEP = 0.001, 0.9, 0.999, 1e-08, 0.01, 10
VMEM_LIMIT = 56 * 1024 * 1024
HI = lax.Precision.HIGHEST
NT = (((1,), (1,)), ((), ()))
TN = (((0,), (0,)), ((), ()))


def _params(n_axes):
    return pltpu.CompilerParams(dimension_semantics=("arbitrary",) * n_axes, vmem_limit_bytes=VMEM_LIMIT)


def _tile(n, want):
    t = min(n, want)
    while n % t:
        t //= 2
    return t


def _sigmoid(x):
    return 1.0 / (1.0 + jnp.exp(-x))


def _rms(x):
    r = lax.rsqrt(jnp.mean(x * x, axis=-1, keepdims=True) + NORM_EPS)
    return x * r, r


def _norm_bwd(dy, xhat, r, g):
    dxh = dy * g
    return r * (dxh - xhat * jnp.mean(dxh * xhat, axis=-1, keepdims=True))


def _colsum(x):
    return jnp.sum(x, axis=0, keepdims=True)


def _w_spec(w, blk):
    if w.ndim == 3:
        return lambda off: pl.BlockSpec((None, w.shape[1], w.shape[2]), lambda i, j: (j + off, 0, 0))
    return lambda off: pl.BlockSpec((w.shape[0], blk), lambda i, j: (0, j + off))


def norm_mm_swiglu(h, g, w3, name):
    T, D = h.shape
    nb, cs, _ = w3.shape
    nh = nb // 2
    tm = _tile(T, 1024)

    def body(h_ref, g_ref, wg_ref, wu_ref, gate_ref, up_ref, a_ref, xn_ref):
        @pl.when(pl.program_id(1) == 0)
        def _():
            xh, _ = _rms(h_ref[...])
            xn_ref[...] = (xh * g_ref[...]).astype(BF)

        xn = xn_ref[...]
        gt = lax.dot_general(xn, wg_ref[...], NT, preferred_element_type=F32)
        up = lax.dot_general(xn, wu_ref[...], NT, preferred_element_type=F32)
        sg = _sigmoid(gt)
        silu = gt * sg
        gate_ref[...] = (up * (sg * (1.0 + gt * (1.0 - sg)))).astype(BF)
        up_ref[...] = silu.astype(BF)
        a_ref[...] = (silu * up).astype(BF)

    ws = _w_spec(w3, cs)
    blk = pl.BlockSpec((None, tm, cs), lambda i, j: (j, i, 0))
    shp = jax.ShapeDtypeStruct((nh, T, cs), BF)
    return pl.pallas_call(
        body, name=name, grid=(T // tm, nh),
        in_specs=[pl.BlockSpec((tm, D), lambda i, j: (i, 0)), pl.BlockSpec((1, D), lambda i, j: (0, 0)), ws(0), ws(nh)],
        out_specs=[blk, blk, blk, pl.BlockSpec((tm, D), lambda i, j: (i, 0))],
        out_shape=[shp, shp, shp, jax.ShapeDtypeStruct((T, D), BF)], compiler_params=_params(2),
    )(h, g, w3, w3)


def norm_mm(h, g, w, name, tn=None):
    T, D = h.shape
    if w.ndim == 3:
        nb, cs = w.shape[0], w.shape[2]
    else:
        cs = tn
        nb = w.shape[1] // cs
    tm = _tile(T, 1024)

    def body(h_ref, g_ref, w_ref, z_ref, xn_ref):
        @pl.when(pl.program_id(1) == 0)
        def _():
            xh, _ = _rms(h_ref[...])
            xn_ref[...] = (xh * g_ref[...]).astype(BF)

        z_ref[...] = jnp.dot(xn_ref[...], w_ref[...], preferred_element_type=F32)

    return pl.pallas_call(
        body, name=name, grid=(T // tm, nb),
        in_specs=[pl.BlockSpec((tm, D), lambda i, j: (i, 0)), pl.BlockSpec((1, D), lambda i, j: (0, 0)),
                  _w_spec(w, cs)(0)],
        out_specs=[pl.BlockSpec((tm, cs), lambda i, j: (i, j)), pl.BlockSpec((tm, D), lambda i, j: (i, 0))],
        out_shape=[jax.ShapeDtypeStruct((T, nb * cs), F32), jax.ShapeDtypeStruct((T, D), BF)], compiler_params=_params(2),
    )(h, g, w)


def _x_spec(x, tm, kb):
    if x.ndim == 3:
        return pl.BlockSpec((None, tm, x.shape[2]), lambda i, j: (j, i, 0))
    return pl.BlockSpec((tm, kb), lambda i, j: (i, j))


def mm_norm_res(x, w2, g, h, coef, name, kb=None):
    T, D = h.shape
    if x.ndim == 3:
        nb, kb = x.shape[0], x.shape[2]
    else:
        nb = x.shape[1] // kb
    tm = _tile(T, 1024)

    def body(x_ref, w_ref, h_ref, g_ref, hn_ref, y_ref, acc_ref):
        b = pl.program_id(1)

        @pl.when(b == 0)
        def _():
            acc_ref[...] = jnp.zeros_like(acc_ref)

        acc_ref[...] += jnp.dot(x_ref[...], w_ref[...], preferred_element_type=F32)

        @pl.when(b == nb - 1)
        def _():
            y = acc_ref[...]
            y_ref[...] = y
            yh, _ = _rms(y)
            hn_ref[...] = h_ref[...] + coef * (yh * g_ref[...])

    tok = pl.BlockSpec((tm, D), lambda i, j: (i, 0))
    shp = jax.ShapeDtypeStruct((T, D), F32)
    return pl.pallas_call(
        body, name=name, grid=(T // tm, nb),
        in_specs=[_x_spec(x, tm, kb), pl.BlockSpec((kb, D), lambda i, j: (j, 0)), tok,
                  pl.BlockSpec((1, D), lambda i, j: (0, 0))],
        out_specs=[tok, tok], out_shape=[shp, shp],
        scratch_shapes=[pltpu.VMEM((tm, D), F32)], compiler_params=_params(2),
    )(x, w2, h, g)


def nbwd_mm_nt(dout, y, g, w2, coef, kb, name, gate=None, up=None):
    T, D = dout.shape
    nb = w2.shape[0] // kb
    swiglu = gate is not None
    tm = _tile(T, 1024 if swiglu else 512)

    def body(*refs):
        if swiglu:
            dout_ref, y_ref, g_ref, w_ref, gate_ref, up_ref, dy_ref, dg_ref, da_ref, dys_ref = refs
        else:
            dout_ref, y_ref, g_ref, w_ref, dy_ref, dg_ref, da_ref, dys_ref = refs
        i, b = pl.program_id(0), pl.program_id(1)

        @pl.when((i == 0) & (b == 0))
        def _():
            dg_ref[...] = jnp.zeros_like(dg_ref)

        @pl.when(b == 0)
        def _():
            yh, r = _rms(y_ref[...])
            dyn = coef * dout_ref[...]
            dg_ref[...] += _colsum(dyn * yh)
            dy = _norm_bwd(dyn, yh, r, g_ref[...]).astype(BF)
            dys_ref[...] = dy
            dy_ref[...] = dy

        da = lax.dot_general(dys_ref[...], w_ref[...], NT, preferred_element_type=F32)
        if swiglu:
            da_ref[0] = (da * gate_ref[...].astype(F32)).astype(BF)
            da_ref[1] = (da * up_ref[...].astype(F32)).astype(BF)
        else:
            da_ref[...] = da.astype(BF)

    tok = pl.BlockSpec((tm, D), lambda i, j: (i, 0))
    vec = pl.BlockSpec((1, D), lambda i, j: (0, 0))
    in_specs = [tok, tok, vec, pl.BlockSpec((kb, D), lambda i, j: (j, 0))]
    args = [dout, y, g, w2]
    if swiglu:
        blk = pl.BlockSpec((None, tm, kb), lambda i, j: (j, i, 0))
        in_specs += [blk, blk]
        args += [gate, up]
        da_spec = pl.BlockSpec((2, None, tm, kb), lambda i, j: (0, j, i, 0))
        da_shape = jax.ShapeDtypeStruct((2, nb, T, kb), BF)
    else:
        da_spec = pl.BlockSpec((tm, kb), lambda i, j: (i, j))
        da_shape = jax.ShapeDtypeStruct((T, nb * kb), BF)
    return pl.pallas_call(
        body, name=name, grid=(T // tm, nb), in_specs=in_specs,
        out_specs=[tok, vec, da_spec],
        out_shape=[jax.ShapeDtypeStruct((T, D), BF), jax.ShapeDtypeStruct((1, D), F32), da_shape],
        scratch_shapes=[pltpu.VMEM((tm, D), BF)], compiler_params=_params(2),
    )(*args)


def mm_nt_nbwd(dz, w, h, g, dout, name, tn=None, transposed=False):
    T, D = h.shape
    if w.ndim == 3:
        nb, cs = w.shape[0], w.shape[1 if transposed else 2]
    else:
        cs = tn
        nb = w.shape[1] // cs
    tm = _tile(T, 1024)

    def body(dz_ref, w_ref, h_ref, g_ref, dout_ref, dh_ref, dg_ref, acc_ref):
        i, b = pl.program_id(0), pl.program_id(1)

        @pl.when((i == 0) & (b == 0))
        def _():
            dg_ref[...] = jnp.zeros_like(dg_ref)

        @pl.when(b == 0)
        def _():
            acc_ref[...] = jnp.zeros_like(acc_ref)

        if transposed:
            acc_ref[...] += jnp.dot(dz_ref[...], w_ref[...], preferred_element_type=F32)
        else:
            acc_ref[...] += lax.dot_general(dz_ref[...], w_ref[...], NT, preferred_element_type=F32)

        @pl.when(b == nb - 1)
        def _():
            xh, r = _rms(h_ref[...])
            gg = g_ref[...]
            dxn = acc_ref[...]
            dg_ref[...] += _colsum(dxn * xh)
            dh_ref[...] = dout_ref[...] + _norm_bwd(dxn, xh, r, gg)

    tok = pl.BlockSpec((tm, D), lambda i, j: (i, 0))
    vec = pl.BlockSpec((1, D), lambda i, j: (0, 0))
    return pl.pallas_call(
        body, name=name, grid=(T // tm, nb),
        in_specs=[_x_spec(dz, tm, cs), _w_spec(w, cs)(0), tok, vec, tok],
        out_specs=[tok, vec],
        out_shape=[jax.ShapeDtypeStruct((T, D), F32), jax.ShapeDtypeStruct((1, D), F32)],
        scratch_shapes=[pltpu.VMEM((tm, D), F32)], compiler_params=_params(2),
    )(dz, w, h, g, dout)


def mm_tn(x, y, name, xb=None, yb=None, x_layer=None):
    T = y.shape[-2]
    wide = (yb if yb is not None else y.shape[-1]) > 1024
    tt = _tile(T, 1024 if wide else 2048)
    x_split = (x.ndim == 3 and x_layer is None) or xb is not None
    if x_layer is not None:
        xs = pl.BlockSpec((None, tt, x.shape[2]), lambda b, t: (x_layer, t, 0))
        kdim = x.shape[2]
    elif x.ndim == 3:
        xs = pl.BlockSpec((None, tt, x.shape[2]), lambda b, t: (b, t, 0))
        nb, kdim = x.shape[0], x.shape[2]
    elif xb is not None:
        xs = pl.BlockSpec((tt, xb), lambda b, t: (t, b))
        nb, kdim = x.shape[1] // xb, xb
    else:
        xs = pl.BlockSpec((tt, x.shape[1]), lambda b, t: (t, 0))
        kdim = x.shape[1]
    if x_split:
        ys = pl.BlockSpec((tt, y.shape[1]), lambda b, t: (t, 0))
        ndim = y.shape[1]
        out_spec = pl.BlockSpec((kdim, ndim), lambda b, t: (b, 0))
        out_shape = jax.ShapeDtypeStruct((nb * kdim, ndim), BF)
    else:
        if y.ndim == 3:
            ys = pl.BlockSpec((None, tt, y.shape[2]), lambda b, t: (b, t, 0))
            nb, ndim = y.shape[0], y.shape[2]
        else:
            ys = pl.BlockSpec((tt, yb), lambda b, t: (t, b))
            nb, ndim = y.shape[1] // yb, yb
        out_spec = pl.BlockSpec((None, kdim, ndim), lambda b, t: (b, 0, 0))
        out_shape = jax.ShapeDtypeStruct((nb, kdim, ndim), BF)
    nt = T // tt

    def body(x_ref, y_ref, o_ref, acc_ref):
        t = pl.program_id(1)

        @pl.when(t == 0)
        def _():
            acc_ref[...] = jnp.zeros_like(acc_ref)

        acc_ref[...] += lax.dot_general(x_ref[...].astype(BF), y_ref[...].astype(BF), TN, preferred_element_type=F32)

        @pl.when(t == nt - 1)
        def _():
            o_ref[...] = acc_ref[...].astype(BF)

    return pl.pallas_call(
        body, name=name, grid=(nb, nt), in_specs=[xs, ys], out_specs=out_spec, out_shape=out_shape,
        scratch_shapes=[pltpu.VMEM((kdim, ndim), F32)], compiler_params=_params(2),
    )(x, y)


def ple_fwd(h, gpre, wg, p3, wp, gpost, layer, name):
    T, D = h.shape
    pd = p3.shape[2]
    tm = _tile(T, 512)

    def body(h_ref, gpre_ref, wg_ref, p_ref, wp_ref, gpost_ref, hn_ref, gate_ref, pp_ref):
        x = h_ref[...]
        xh, _ = _rms(x)
        u = jnp.dot((xh * gpre_ref[...]).astype(BF), wg_ref[...], preferred_element_type=F32)
        gate = _sigmoid(u)
        pp = jnp.dot(p_ref[...].astype(BF), wp_ref[...], preferred_element_type=F32)
        yh, _ = _rms(gate * pp)
        hn_ref[...] = x + yh * gpost_ref[...]
        gate_ref[...] = gate.astype(BF)
        pp_ref[...] = pp.astype(BF)

    tok = pl.BlockSpec((tm, D), lambda i: (i, 0))
    vec = pl.BlockSpec((1, D), lambda i: (0, 0))
    return pl.pallas_call(
        body, name=name, grid=(T // tm,),
        in_specs=[tok, vec, pl.BlockSpec((D, D), lambda i: (0, 0)),
                  pl.BlockSpec((None, tm, pd), lambda i: (layer, i, 0)),
                  pl.BlockSpec((pd, D), lambda i: (0, 0)), vec],
        out_specs=[tok, tok, tok],
        out_shape=[jax.ShapeDtypeStruct((T, D), F32), jax.ShapeDtypeStruct((T, D), BF), jax.ShapeDtypeStruct((T, D), BF)],
        compiler_params=_params(1),
    )(h, gpre, wg, p3, wp, gpost)


def ple_bwd(dout, h, gate, pp, gpre, wg, gpost, name):
    T, D = h.shape
    tm = _tile(T, 512)

    def body(dout_ref, h_ref, gate_ref, pp_ref, gpre_ref, wg_ref, gpost_ref, dh_ref, du_ref, dpp_ref, xn_ref, dgpre_ref, dgpost_ref):
        @pl.when(pl.program_id(0) == 0)
        def _():
            dgpre_ref[...] = jnp.zeros_like(dgpre_ref)
            dgpost_ref[...] = jnp.zeros_like(dgpost_ref)

        dout = dout_ref[...]
        gate = gate_ref[...].astype(F32)
        pp = pp_ref[...].astype(F32)
        yh, ry = _rms(gate * pp)
        dgpost_ref[...] += _colsum(dout * yh)
        dy = _norm_bwd(dout, yh, ry, gpost_ref[...])
        dpp_ref[...] = (dy * gate).astype(BF)
        du = (dy * pp * gate * (1.0 - gate)).astype(BF)
        du_ref[...] = du
        dxn = lax.dot_general(du, wg_ref[...], NT, preferred_element_type=F32)
        xh, r = _rms(h_ref[...])
        gp = gpre_ref[...]
        dgpre_ref[...] += _colsum(dxn * xh)
        dh_ref[...] = dout + _norm_bwd(dxn, xh, r, gp)
        xn_ref[...] = (xh * gp).astype(BF)

    tok = pl.BlockSpec((tm, D), lambda i: (i, 0))
    vec = pl.BlockSpec((1, D), lambda i: (0, 0))
    bft = jax.ShapeDtypeStruct((T, D), BF)
    v32 = jax.ShapeDtypeStruct((1, D), F32)
    return pl.pallas_call(
        body, name=name, grid=(T // tm,),
        in_specs=[tok, tok, tok, tok, vec, pl.BlockSpec((D, D), lambda i: (0, 0)), vec],
        out_specs=[tok, tok, tok, tok, vec, vec],
        out_shape=[jax.ShapeDtypeStruct((T, D), F32), bft, bft, bft, v32, v32],
        compiler_params=_params(1),
    )(dout, h, gate, pp, gpre, wg, gpost)


def _chunk_tri(tb, upper):
    r = lax.broadcasted_iota(jnp.int32, (tb, tb), 0)
    c = lax.broadcasted_iota(jnp.int32, (tb, tb), 1)
    shift = HGRN_CHUNK.bit_length() - 1
    same = jnp.right_shift(r, shift) == jnp.right_shift(c, shift)
    return (same & ((c >= r) if upper else (c <= r))).astype(F32)


def _hgrn_gates(z, logits):
    lb = 1.0 / (1.0 + jnp.exp(logits[1:2, :] - logits[0:1, :]))
    e = jnp.exp(-jnp.abs(z))
    inv = 1.0 / (1.0 + e)
    sig = jnp.where(z >= 0, inv, e * inv)
    nsig = jnp.where(z >= 0, e * inv, inv)
    return lb, sig, nsig, lb + (1.0 - lb) * sig


def hgrn_fwd(z, lb_logits, out_norm, name):
    T = z.shape[0]
    W = z.shape[1] // 4
    H = W // HGRN_DK
    C = HGRN_CHUNK
    HB = _tile(H, HGRN_HEADS_PER_STEP)
    tb = _tile(T, 256)
    nch = tb // C

    def body(zq_ref, zf_ref, zv_ref, zg_ref, lbl_ref, on_ref, x_ref, o_ref, st_ref, s_scr, cum_scr, k_scr, v_scr, o_scr):
        @pl.when(pl.program_id(1) == 0)
        def _():
            s_scr[...] = jnp.zeros_like(s_scr)

        lb, sig, nsig, f = _hgrn_gates(zf_ref[...], lbl_ref[...])
        cum = jnp.dot(_chunk_tri(tb, False), jnp.log(f), precision=HI, preferred_element_type=F32)
        kk = (1.0 - lb) * nsig
        for hh in range(HB):
            cols = slice(hh * HGRN_DK, (hh + 1) * HGRN_DK)
            cum_scr[hh] = cum[:, cols]
            k_scr[hh] = kk[:, cols]
            v_scr[hh] = zv_ref[:, cols]
        row = lax.broadcasted_iota(jnp.int32, (C, HGRN_DK), 0)

        def chunk(c, carry):
            r0 = pl.multiple_of(c * C, C)
            rows = pl.ds(r0, C)
            last_row = pl.ds(r0 + C - 1, 1)
            heads = []
            for hh in range(HB):
                cols = slice(hh * HGRN_DK, (hh + 1) * HGRN_DK)
                q, cu = zq_ref[rows, cols], cum_scr[hh, rows, :]
                st = s_scr[hh]
                st_ref[c, hh] = st
                o = lax.dot_general((q * jnp.exp(cu)).astype(BF), st.astype(BF), NT, preferred_element_type=F32)
                last = cum_scr[hh, last_row, :]
                kg = (k_scr[hh, rows, :] * jnp.exp(last - cu)).astype(BF)
                s_scr[hh] = st * jnp.exp(last) + lax.dot_general(v_scr[hh, rows, :].astype(BF), kg, TN, preferred_element_type=F32)
                heads.append((hh, q, cu, o))
            for hh, q, cu, o in heads:
                qr = q.astype(BF).astype(F32)
                low = jnp.zeros((C - HALF, HGRN_DK), F32)
                for s in range(C):
                    one = pl.ds(r0 + s, 1)
                    sl = slice(0 if s < HALF else HALF, C)
                    e = jnp.exp(jnp.minimum(cu[sl] - cum_scr[hh, one, :], 0.0))
                    col = jnp.sum(qr[sl] * (e * k_scr[hh, one, :]).astype(BF).astype(F32), axis=-1, keepdims=True)
                    col = jnp.where(row[sl] >= s, col, 0.0).astype(BF).astype(F32)
                    term = col * v_scr[hh, one, :].astype(BF).astype(F32)
                    if s < HALF:
                        o = o + term
                    else:
                        low = low + term
                o_scr[hh, rows, :] = o
                o_scr[hh, pl.ds(r0 + HALF, C - HALF), :] += low
            return carry

        lax.fori_loop(0, nch, chunk, 0)
        for hh in range(HB):
            cols = slice(hh * HGRN_DK, (hh + 1) * HGRN_DK)
            o = o_scr[hh]
            o_ref[:, cols] = o
            oh, _ = _rms(o)
            g = zg_ref[:, cols]
            x_ref[:, cols] = (oh * on_ref[...] * (g * _sigmoid(g))).astype(BF)

    def zs(part):
        return pl.BlockSpec((tb, HB * HGRN_DK), lambda hd, i: (i, part * (H // HB) + hd))

    blk = pl.BlockSpec((tb, HB * HGRN_DK), lambda hd, i: (i, hd))
    wide = pltpu.VMEM((HB, tb, HGRN_DK), F32)
    return pl.pallas_call(
        body, name=name, grid=(H // HB, T // tb),
        in_specs=[zs(0), zs(1), zs(2), zs(3), pl.BlockSpec((2, HB * HGRN_DK), lambda hd, i: (0, hd)),
                  pl.BlockSpec((1, HGRN_DK), lambda hd, i: (0, 0))],
        out_specs=[blk, blk, pl.BlockSpec((nch, HB, HGRN_DK, HGRN_DK), lambda hd, i: (i, hd, 0, 0))],
        out_shape=[jax.ShapeDtypeStruct((T, W), BF), jax.ShapeDtypeStruct((T, W), F32),
                   jax.ShapeDtypeStruct((T // C, H, HGRN_DK, HGRN_DK), F32)],
        scratch_shapes=[pltpu.VMEM((HB, HGRN_DK, HGRN_DK), F32), wide, wide, wide, wide],
        compiler_params=_params(2),
    )(z, z, z, z, lb_logits, out_norm)


def hgrn_bwd(dx, z, o, states, lb_logits, out_norm, name):
    T = z.shape[0]
    W = z.shape[1] // 4
    H = W // HGRN_DK
    C = HGRN_CHUNK
    HB = _tile(H, HGRN_HEADS_PER_STEP)
    tb = _tile(T, 256)
    nch = tb // C
    nblk = T // tb

    def body(dx_ref, zq_ref, zf_ref, zv_ref, zg_ref, o_ref, st_ref, lbl_ref, on_ref,
             dq_ref, df_ref, dv_ref, dg_ref, dl_ref, don_ref,
             ds_scr, cum_scr, k_scr, v_scr, q_scr, do_scr, dq_scr, dk_scr, dv_scr, dcum_scr):
        hd, i = pl.program_id(0), pl.program_id(1)

        @pl.when(i == 0)
        def _():
            ds_scr[...] = jnp.zeros_like(ds_scr)
            dl_ref[...] = jnp.zeros_like(dl_ref)

        @pl.when((i == 0) & (hd == 0))
        def _():
            don_ref[...] = jnp.zeros_like(don_ref)

        lb, sig, nsig, f = _hgrn_gates(zf_ref[...], lbl_ref[...])
        cum = jnp.dot(_chunk_tri(tb, False), jnp.log(f), precision=HI, preferred_element_type=F32)
        kk = (1.0 - lb) * nsig
        w = on_ref[...]
        for hh in range(HB):
            cols = slice(hh * HGRN_DK, (hh + 1) * HGRN_DK)
            cum_scr[hh] = cum[:, cols]
            k_scr[hh] = kk[:, cols]
            v_scr[hh] = zv_ref[:, cols]
            q_scr[hh] = zq_ref[:, cols]
            oh, r = _rms(o_ref[:, cols])
            g = zg_ref[:, cols]
            sg = _sigmoid(g)
            dxv = dx_ref[:, cols].astype(F32)
            dg_ref[:, cols] = (dxv * (oh * w) * (sg * (1.0 + g * (1.0 - sg)))).astype(BF)
            don = dxv * (g * sg)
            don_ref[...] += _colsum(don * oh)
            do_scr[hh] = _norm_bwd(don, oh, r, w)
        row = lax.broadcasted_iota(jnp.int32, (C, HGRN_DK), 0)

        def chunk(cc, carry):
            c = nch - 1 - cc
            r0 = pl.multiple_of(c * C, C)
            rows = pl.ds(r0, C)
            last_row = pl.ds(r0 + C - 1, 1)
            heads = []
            for hh in range(HB):
                q, k, v, cu, do = q_scr[hh, rows, :], k_scr[hh, rows, :], v_scr[hh, rows, :], cum_scr[hh, rows, :], do_scr[hh, rows, :]
                st = st_ref[c, hh]
                dst = ds_scr[hh]
                last = cum_scr[hh, last_row, :]
                lam, gam, elast = jnp.exp(cu), jnp.exp(last - cu), jnp.exp(last)
                dob, dstb = do.astype(BF), dst.astype(BF)
                dq = jnp.dot(dob, st.astype(BF), preferred_element_type=F32) * lam
                dv = lax.dot_general((k * gam).astype(BF), dstb, NT, preferred_element_type=F32)
                dk = jnp.dot(v.astype(BF), dstb, preferred_element_type=F32) * gam
                dlast = elast * _colsum(dst * st) + _colsum(dk * k)
                ds_scr[hh] = dst * elast + lax.dot_general(dob, (q * lam).astype(BF), TN, preferred_element_type=F32)
                heads.append((hh, q, k, cu, do, dq, dk, dv, dlast))
            for hh, q, k, cu, do, dq, dk, dv, dlast in heads:
                for first in (True, False):
                    sl = slice(0 if first else HALF, C)
                    qs, cus, dos, rws = q[sl], cu[sl], do[sl], row[sl]
                    dqs, dks, dvs = dq[sl], dk[sl], dv[sl]
                    for s in (range(HALF) if first else range(HALF, C)):
                        one = pl.ds(r0 + s, 1)
                        e = jnp.where(rws >= s, jnp.exp(jnp.minimum(cus - cum_scr[hh, one, :], 0.0)), 0.0)
                        ks = k_scr[hh, one, :]
                        da = jnp.sum(dos * v_scr[hh, one, :], axis=-1, keepdims=True)
                        pq = qs * e
                        a = jnp.sum(pq * ks, axis=-1, keepdims=True)
                        dqs = dqs + da * e * ks
                        dks = jnp.where(rws == s, dks + _colsum(da * pq), dks)
                        dvs = jnp.where(rws == s, dvs + _colsum(a * dos), dvs)
                    if first:
                        dq, dk, dv = dqs, dks, dvs
                        top = pl.ds(r0, HALF)
                        dq_scr[hh, top, :] = dq[:HALF]
                        dk_scr[hh, top, :] = dk[:HALF]
                        dv_scr[hh, top, :] = dv[:HALF]
                        dcum_scr[hh, top, :] = q[:HALF] * dq[:HALF] - k[:HALF] * dk[:HALF]
                    else:
                        low = pl.ds(r0 + HALF, C - HALF)
                        dq_scr[hh, low, :] = dqs
                        dk_scr[hh, low, :] = dks
                        dv_scr[hh, low, :] = dvs
                        dcum_scr[hh, low, :] = qs * dqs - k[sl] * dks + jnp.where(rws == C - 1, dlast, 0.0)
            return carry

        lax.fori_loop(0, nch, chunk, 0)
        tri = _chunk_tri(tb, True)
        for hh in range(HB):
            cols = slice(hh * HGRN_DK, (hh + 1) * HGRN_DK)
            dlf = jnp.dot(tri, dcum_scr[hh], precision=HI, preferred_element_type=F32)
            dk = dk_scr[hh]
            lbh, sigh, nsigh, fh = lb[:, cols], sig[:, cols], nsig[:, cols], f[:, cols]
            common = (1.0 - lbh) * sigh * nsigh
            df_ref[:, cols] = (dlf * common / fh - dk * common).astype(BF)
            dq_ref[:, cols] = dq_scr[hh].astype(BF)
            dv_ref[:, cols] = dv_scr[hh].astype(BF)
            dl0 = _colsum(dlf * nsigh / fh - dk * nsigh) * lbh * (1.0 - lbh)
            dl_ref[:, cols] += jnp.where(lax.broadcasted_iota(jnp.int32, (2, HGRN_DK), 0) == 0, dl0, -dl0)

    def zs(part):
        return pl.BlockSpec((tb, HB * HGRN_DK), lambda hd, i: (nblk - 1 - i, part * (H // HB) + hd))

    blk = pl.BlockSpec((tb, HB * HGRN_DK), lambda hd, i: (nblk - 1 - i, hd))
    bft = jax.ShapeDtypeStruct((T, W), BF)
    scr = pltpu.VMEM((HB, tb, HGRN_DK), F32)
    return pl.pallas_call(
        body, name=name, grid=(H // HB, nblk),
        in_specs=[blk, zs(0), zs(1), zs(2), zs(3), blk,
                  pl.BlockSpec((nch, HB, HGRN_DK, HGRN_DK), lambda hd, i: (nblk - 1 - i, hd, 0, 0)),
                  pl.BlockSpec((2, HB * HGRN_DK), lambda hd, i: (0, hd)), pl.BlockSpec((1, HGRN_DK), lambda hd, i: (0, 0))],
        out_specs=[blk, blk, blk, blk, pl.BlockSpec((2, HB * HGRN_DK), lambda hd, i: (0, hd)),
                   pl.BlockSpec((1, HGRN_DK), lambda hd, i: (0, 0))],
        out_shape=[bft, bft, bft, bft, jax.ShapeDtypeStruct((2, W), F32), jax.ShapeDtypeStruct((1, HGRN_DK), F32)],
        scratch_shapes=[pltpu.VMEM((HB, HGRN_DK, HGRN_DK), F32), scr, scr, scr, scr, scr, scr, scr, scr, scr],
        compiler_params=_params(2),
    )(dx, z, z, z, z, o, states, lb_logits, out_norm)


def _log_sigmoid(x):
    return jnp.minimum(x, 0.0) - jnp.log(1.0 + jnp.exp(-jnp.abs(x)))


def _tri(n, upper):
    r = lax.broadcasted_iota(jnp.int32, (n, n), 0)
    c = lax.broadcasted_iota(jnp.int32, (n, n), 1)
    return ((r >= c) if upper else (c <= r)).astype(F32)


def fox_cum_fwd(kvf, fcol, b_row, name):
    T = kvf.shape[0]
    tb = _tile(T, 512)

    def body(x_ref, b_ref, ct_ref, carry_ref):
        @pl.when(pl.program_id(0) == 0)
        def _():
            carry_ref[...] = jnp.zeros_like(carry_ref)

        lf = _log_sigmoid(x_ref[...] + b_ref[...])
        cum = jnp.dot(_tri(tb, False), lf, precision=HI, preferred_element_type=F32) + carry_ref[...]
        carry_ref[...] += _colsum(lf)
        ct_ref[...] = cum.T

    return pl.pallas_call(
        body, name=name, grid=(T // tb,),
        in_specs=[pl.BlockSpec((tb, LANES), lambda i: (i, fcol)), pl.BlockSpec((1, LANES), lambda i: (0, 0))],
        out_specs=pl.BlockSpec((LANES, tb), lambda i: (0, i)), out_shape=jax.ShapeDtypeStruct((LANES, T), F32),
        scratch_shapes=[pltpu.VMEM((1, LANES), F32)], compiler_params=_params(1),
    )(kvf, b_row)


def fox_cum_bwd(dct, kvf, fcol, b_row, name):
    T = kvf.shape[0]
    tb = _tile(T, 512)
    nblk = T // tb

    def body(dc_ref, x_ref, b_ref, df_ref, db_ref, carry_ref):
        @pl.when(pl.program_id(0) == 0)
        def _():
            carry_ref[...] = jnp.zeros_like(carry_ref)
            db_ref[...] = jnp.zeros_like(db_ref)

        dc = dc_ref[...]
        dlf_t = jnp.dot(dc, _tri(tb, True), precision=HI, preferred_element_type=F32) + carry_ref[...]
        carry_ref[...] += jnp.sum(dc, axis=1, keepdims=True)
        x = x_ref[...] + b_ref[...]
        df = dlf_t.T * _sigmoid(-x)
        df_ref[...] = df.astype(BF)
        db_ref[...] += _colsum(df)

    return pl.pallas_call(
        body, name=name, grid=(nblk,),
        in_specs=[pl.BlockSpec((LANES, tb), lambda i: (0, nblk - 1 - i)),
                  pl.BlockSpec((tb, LANES), lambda i: (nblk - 1 - i, fcol)), pl.BlockSpec((1, LANES), lambda i: (0, 0))],
        out_specs=[pl.BlockSpec((tb, LANES), lambda i: (nblk - 1 - i, 0)), pl.BlockSpec((1, LANES), lambda i: (0, 0))],
        out_shape=[jax.ShapeDtypeStruct((T, LANES), BF), jax.ShapeDtypeStruct((1, LANES), F32)],
        scratch_shapes=[pltpu.VMEM((LANES, 1), F32)], compiler_params=_params(1),
    )(dct, kvf, b_row)


NAUG = 3


def fox_prep(kvf, ct, n_fox, tk):
    T = kvf.shape[0]
    W = n_fox * FOX_HEAD_DIM
    NP = n_fox // 2
    k = kvf[:, :W].astype(BF).reshape(T, NP, 2, FOX_HEAD_DIM)
    c = ct[:n_fox].T.reshape(T, NP, 2)
    hi = lax.reduce_precision(c, 8, 7)
    mid = lax.reduce_precision(c - hi, 8, 7)
    lo = c - hi - mid
    aug = jnp.stack([hi, mid, lo], axis=-1).astype(BF)
    pad = jnp.zeros((T, NP, FOX_HEAD_DIM - NAUG), BF)
    ka = jnp.concatenate([k[:, :, 0], aug[:, :, 0], pad], axis=-1).reshape(T, W)
    kb = jnp.concatenate([aug[:, :, 1], pad, k[:, :, 1]], axis=-1).reshape(T, W)
    v = kvf[:, W:2 * W].astype(BF)
    vt3 = v.reshape(T // tk, tk, NP, LANES).transpose(2, 0, 3, 1)
    return ka, kb, vt3


def fox_fwd(qg, ka, kb, vt3, name):
    T = qg.shape[0]
    W = qg.shape[1] // 2
    NP = W // LANES
    tq = tk = vt3.shape[3]
    scale = FOX_HEAD_DIM ** -0.5
    nk = T // tk
    HD = FOX_HEAD_DIM

    def body(q_ref, g_ref, ka_ref, kb_ref, vt_ref, x_ref, o_ref, lse_ref):
        i = pl.program_id(1)
        lane = lax.broadcasted_iota(jnp.int32, (tq, LANES), 1)
        q2 = q_ref[...] * scale
        qa = jnp.where(lane < HD, q2, jnp.where(lane < HD + NAUG, -1.0, 0.0))
        qb = jnp.where(lane >= HD, q2, jnp.where(lane < NAUG, -1.0, 0.0))
        qts = (qa.T.astype(BF), qb.T.astype(BF))
        krow = lax.broadcasted_iota(jnp.int32, (tk, tq), 0)
        qcol = lax.broadcasted_iota(jnp.int32, (tk, tq), 1)

        def step(j, carry, diag):
            rows = pl.ds(pl.multiple_of(j * tk, tk), tk)
            ks = (ka_ref[rows, :], kb_ref[rows, :])
            vt = vt_ref[j]
            sts = [jnp.dot(ks[a], qts[a], preferred_element_type=F32) for a in range(2)]
            pts, mls = [], []
            for a in range(2):
                m, l, _ = carry[a]
                st = sts[a]
                if diag:
                    st = jnp.where(krow + (j * tk - i * tq) <= qcol, st, -1e30)
                mn = jnp.maximum(m, jnp.max(st, axis=0, keepdims=True))
                alpha = jnp.exp(m - mn)
                pt = jnp.exp(st - mn)
                mls.append((mn, l * alpha + jnp.sum(pt, axis=0, keepdims=True), alpha))
                pts.append(pt.astype(BF))
            out = []
            for a in range(2):
                mn, l, alpha = mls[a]
                acc = carry[a][2] * alpha + jnp.dot(vt[a * HD:(a + 1) * HD, :], pts[a], preferred_element_type=F32)
                out.append((mn, l, acc))
            return tuple(out)

        init = (jnp.full((1, tq), -1e30, F32), jnp.zeros((1, tq), F32), jnp.zeros((HD, tq), F32))
        r = tq // tk
        carry = lax.fori_loop(0, i * r, lambda j, c: step(j, c, False), (init, init))
        for u in range(r):
            carry = step(i * r + u, carry, True)
        (ma, la, acca), (mb, lb, accb) = carry
        ot = jnp.concatenate([acca / la, accb / lb], axis=0)
        o = ot.T
        o_ref[...] = o
        lse_ref[0:1, :] = ma + jnp.log(la)
        lse_ref[1:2, :] = mb + jnp.log(lb)
        x_ref[...] = (o * _sigmoid(g_ref[...])).astype(BF)

    blk = pl.BlockSpec((tq, LANES), lambda hp, i: (i, hp))
    full = pl.BlockSpec((T, LANES), lambda hp, i: (0, hp))
    return pl.pallas_call(
        body, name=name, grid=(NP, T // tq),
        in_specs=[blk, pl.BlockSpec((tq, LANES), lambda hp, i: (i, NP + hp)), full, full,
                  pl.BlockSpec((None, nk, LANES, tk), lambda hp, i: (hp, 0, 0, 0))],
        out_specs=[blk, blk, pl.BlockSpec((None, None, 2, tq), lambda hp, i: (hp, i, 0, 0))],
        out_shape=[jax.ShapeDtypeStruct((T, W), BF), jax.ShapeDtypeStruct((T, W), F32),
                   jax.ShapeDtypeStruct((NP, T // tq, 2, tq), F32)],
        compiler_params=_params(2),
    )(qg, qg, ka, kb, vt3)


def fox_gate_bwd(dx, o, qg, name):
    T, W = o.shape
    tm = _tile(T, 512)

    def body(dx_ref, o_ref, g_ref, do_ref, dg_ref, ds_ref):
        dxv = dx_ref[...].astype(F32)
        sg = _sigmoid(g_ref[...])
        do = dxv * sg
        o = o_ref[...]
        do_ref[...] = do
        dg_ref[...] = (dxv * o * sg * (1.0 - sg)).astype(BF)
        head = jnp.right_shift(lax.broadcasted_iota(jnp.int32, (W, LANES), 0), FOX_HEAD_DIM.bit_length() - 1)
        sel = (head == lax.broadcasted_iota(jnp.int32, (W, LANES), 1)).astype(F32)
        ds_ref[...] = jnp.dot(do * o, sel, precision=HI, preferred_element_type=F32)

    tok = pl.BlockSpec((tm, W), lambda i: (i, 0))
    return pl.pallas_call(
        body, name=name, grid=(T // tm,),
        in_specs=[tok, tok, pl.BlockSpec((tm, W), lambda i: (i, 1))],
        out_specs=[tok, tok, pl.BlockSpec((tm, LANES), lambda i: (i, 0))],
        out_shape=[jax.ShapeDtypeStruct((T, W), F32), jax.ShapeDtypeStruct((T, W), BF), jax.ShapeDtypeStruct((T, LANES), F32)],
        compiler_params=_params(1),
    )(dx, o, qg)


def fox_bwd_prep(qg, kvf, do, tq):
    T = qg.shape[0]
    W = qg.shape[1] // 2
    NP = W // LANES
    scale = FOX_HEAD_DIM ** -0.5
    tr3 = lambda a: a.reshape(T // tq, tq, NP, LANES).transpose(2, 0, 3, 1)
    q = (qg[:, :W] * scale).astype(BF)
    dob = do.astype(BF)
    k = kvf[:, :W]
    return q, tr3(q), dob, tr3(dob), tr3((k * scale).astype(BF)), kvf[:, W:2 * W].astype(BF)


def fox_bwd(q, qt3, dob, dot3, kt3, v, ka, kb, lse4, dsum4, name):
    T, W = q.shape
    NP = W // LANES
    tq = tk = qt3.shape[3]
    nq = T // tq
    HD = FOX_HEAD_DIM

    def body(q_ref, qt_ref, do_ref, dot_ref, kt_ref, v_ref, ka_ref, kb_ref, lse_ref, dsum_ref,
             dqt_ref, dk_ref, dv_ref, dc_ref, dcq_ref, dk_scr, dv_scr, dcl_scr):
        j = pl.program_id(1)

        @pl.when(j == 0)
        def _():
            dqt_ref[...] = jnp.zeros_like(dqt_ref)
            dcq_ref[...] = jnp.zeros_like(dcq_ref)

        dk_scr[...] = jnp.zeros_like(dk_scr)
        dv_scr[...] = jnp.zeros_like(dv_scr)
        dcl_scr[...] = jnp.zeros_like(dcl_scr)
        lane = lax.broadcasted_iota(jnp.int32, (tk, LANES), 1)
        srow = lax.broadcasted_iota(jnp.int32, (LANES, tq), 0)
        lanes_of = (lane < HD, lane >= HD)
        rows_of = (srow < HD, srow >= HD)
        v2 = v_ref[...]
        kt2 = kt_ref[...]
        zero = jnp.zeros((), BF)
        vs = [jnp.where(lanes_of[a], v2, zero) for a in range(2)]
        kts = [jnp.where(rows_of[a], kt2, zero) for a in range(2)]
        kaug = (ka_ref[...], kb_ref[...])
        krow = lax.broadcasted_iota(jnp.int32, (tk, tq), 0)
        qcol = lax.broadcasted_iota(jnp.int32, (tk, tq), 1)
        neg1 = jnp.full((), -1.0, BF)

        def step(i, carry, diag):
            rows = pl.ds(pl.multiple_of(i * tq, tq), tq)
            qt2 = qt_ref[i]
            dot2 = dot_ref[i]
            q2 = q_ref[rows, :]
            do2 = do_ref[rows, :]
            qts = [jnp.where(srow < HD, qt2, jnp.where(srow < HD + NAUG, neg1, zero)),
                   jnp.where(srow >= HD, qt2, jnp.where(srow < NAUG, neg1, zero))]
            sts = [jnp.dot(kaug[a], qts[a], preferred_element_type=F32) for a in range(2)]
            dps = [jnp.dot(vs[a], dot2, preferred_element_type=F32) for a in range(2)]
            pbs, dsbs = [], []
            for a in range(2):
                pt = jnp.exp(sts[a] - lse_ref[i, a:a + 1, :])
                if diag:
                    pt = jnp.where(krow <= qcol, pt, 0.0)
                ds = pt * (dps[a] - dsum_ref[i, a:a + 1, :])
                dcq_ref[i, a:a + 1, :] += _colsum(ds)
                part = ds[:, 0:LANES]
                for u in range(1, tq // LANES):
                    part = part + ds[:, u * LANES:(u + 1) * LANES]
                dcl_scr[a] += part
                pbs.append(pt.astype(BF))
                dsbs.append(ds.astype(BF))
            qn = [jnp.where(lanes_of[a], q2, zero) for a in range(2)]
            don = [jnp.where(lanes_of[a], do2, zero) for a in range(2)]
            dv_scr[...] += (jnp.dot(pbs[0], don[0], preferred_element_type=F32) +
                            jnp.dot(pbs[1], don[1], preferred_element_type=F32))
            dk_scr[...] += (jnp.dot(dsbs[0], qn[0], preferred_element_type=F32) +
                            jnp.dot(dsbs[1], qn[1], preferred_element_type=F32))
            dqt_ref[i] += (jnp.dot(kts[0], dsbs[0], preferred_element_type=F32) +
                           jnp.dot(kts[1], dsbs[1], preferred_element_type=F32))
            return carry

        step(j, 0, True)
        lax.fori_loop(j + 1, nq, lambda i, c: step(i, c, False), 0)
        dk_ref[...] = dk_scr[...].astype(BF)
        dv_ref[...] = dv_scr[...].astype(BF)
        for a in range(2):
            dc_ref[a:a + 1, :] = -_colsum(dcl_scr[a].T)

    tile = pl.BlockSpec((tk, LANES), lambda hp, j: (j, hp))
    full = pl.BlockSpec((T, LANES), lambda hp, j: (0, hp))
    full3 = pl.BlockSpec((None, nq, LANES, tq), lambda hp, j: (hp, 0, 0, 0))
    rows4 = pl.BlockSpec((None, nq, 2, tq), lambda hp, j: (hp, 0, 0, 0))
    bft = jax.ShapeDtypeStruct((T, W), BF)
    r4 = jax.ShapeDtypeStruct((NP, nq, 2, tq), F32)
    return pl.pallas_call(
        body, name=name, grid=(NP, nq),
        in_specs=[full, full3, full, full3, pl.BlockSpec((None, None, LANES, tk), lambda hp, j: (hp, j, 0, 0)),
                  tile, tile, tile, rows4, rows4],
        out_specs=[full3, tile, tile, pl.BlockSpec((None, None, 2, tk), lambda hp, j: (hp, j, 0, 0)), rows4],
        out_shape=[jax.ShapeDtypeStruct((NP, nq, LANES, tq), F32), bft, bft, r4, r4],
        scratch_shapes=[pltpu.VMEM((tk, LANES), F32), pltpu.VMEM((tk, LANES), F32), pltpu.VMEM((2, tk, LANES), F32)],
        compiler_params=_params(2),
    )(q, qt3, dob, dot3, kt3, v, ka, kb, lse4, dsum4)


def loss_fwd_bwd(y, target, name):
    T, D = y.shape
    tm = _tile(T, 512)

    def body(y_ref, t_ref, dy_ref, l_ref):
        @pl.when(pl.program_id(0) == 0)
        def _():
            l_ref[...] = jnp.zeros_like(l_ref)

        d = y_ref[...] - t_ref[...]
        dy_ref[...] = d * (1.0 / D)
        l_ref[...] += 0.5 * jnp.sum(jnp.mean(d * d, axis=-1, keepdims=True), axis=0, keepdims=True)

    tok = pl.BlockSpec((tm, D), lambda i: (i, 0))
    return pl.pallas_call(
        body, name=name, grid=(T // tm,), in_specs=[tok, tok],
        out_specs=[tok, pl.BlockSpec((1, 1), lambda i: (0, 0))],
        out_shape=[jax.ShapeDtypeStruct((T, D), F32), jax.ShapeDtypeStruct((1, 1), F32)], compiler_params=_params(1),
    )(y, target)


def adamw(parts, w, m, v, layer, prev, name):
    L, R, C = w.shape
    tr = _tile(R, 256)
    c1 = 1.0 / (1.0 - ADAM_B1 ** ADAM_STEP)
    c2 = 1.0 / (1.0 - ADAM_B2 ** ADAM_STEP)

    def body(p_ref, w_ref, m_ref, v_ref, *rest):
        g_ref, d_ref, mo_ref, vo_ref = rest[-4:]
        g = p_ref[0].astype(F32)
        for d in range(1, N_DEV):
            g = g + p_ref[d].astype(F32)
        mn = ADAM_B1 * m_ref[...] + (1.0 - ADAM_B1) * g
        vn = ADAM_B2 * v_ref[...] + (1.0 - ADAM_B2) * (g * g)
        g_ref[...] = g
        mo_ref[...] = mn
        vo_ref[...] = vn
        d_ref[...] = -ADAM_LR * ((mn * c1) / (jnp.sqrt(vn * c2) + ADAM_EPS) + ADAM_WD * w_ref[...])

    blk = pl.BlockSpec((None, tr, C), lambda i: (layer, i, 0))
    shp = jax.ShapeDtypeStruct((L, R, C), F32)
    in_specs = [pl.BlockSpec((N_DEV, tr, C), lambda i: (0, i, 0)), blk, blk, blk]
    args = [parts, w, m, v]
    aliases = {}
    if prev is not None:
        in_specs += [pl.BlockSpec(memory_space=pl.ANY)] * 4
        args += list(prev)
        aliases = {4 + j: j for j in range(4)}
    return pl.pallas_call(
        body, name=name, grid=(R // tr,), in_specs=in_specs, out_specs=[blk, blk, blk, blk],
        out_shape=[shp, shp, shp, shp], input_output_aliases=aliases, compiler_params=_params(1),
    )(*args)


HBM_SPEC = pl.BlockSpec(memory_space=pltpu.HBM)
SEM_SPEC = pl.BlockSpec(memory_space=pltpu.SEMAPHORE)
EFFECT = pltpu.SideEffectType.DATAFLOW_SIDE_EFFECTING


def _mesh_pos():
    return lax.axis_index("x"), lax.axis_index("y"), lax.axis_index("c")


def _flip(v, bit):
    return v + bit - 2 * v * bit


def _peer(pos, delta):
    x, y, c = pos
    px, py, pc = _flip(x, (delta >> 2) & 1), _flip(y, (delta >> 1) & 1), _flip(c, delta & 1)
    return (px, py, pc), 4 * px + 2 * py + pc


def _me():
    x, y, c = _mesh_pos()
    return 4 * x + 2 * y + c


def _copies(src_refs, land_refs, whole, send, recv, incoming):
    pos = _mesh_pos()
    me = 4 * pos[0] + 2 * pos[1] + pos[2]
    out = []
    for k in range(len(src_refs)):
        for d in range(1, N_DEV):
            dev, idx = _peer(pos, d)
            j = k * (N_DEV - 1) + d - 1
            src = src_refs[k] if whole[k] else src_refs[k].at[idx]
            out.append(pltpu.make_async_remote_copy(
                src_ref=src, dst_ref=land_refs[k].at[idx if incoming else me], send_sem=send.at[j], recv_sem=recv.at[j],
                device_id=dev, device_id_type=pl.DeviceIdType.MESH))
    return out


def exchange_start(srcs, lands, whole, name):
    n = len(srcs)

    def body(*refs):
        for copy in _copies(refs[:n], refs[n:2 * n], whole, refs[2 * n], refs[2 * n + 1], False):
            copy.start()
        refs[-1][...] = jnp.zeros_like(refs[-1])

    sems = pltpu.SemaphoreType.DMA((n * (N_DEV - 1),))
    thru = [pltpu.HBM(a.shape, a.dtype) for a in list(srcs) + list(lands)]
    res = pl.pallas_call(
        body, name=name, in_specs=[HBM_SPEC] * (2 * n),
        out_specs=[SEM_SPEC, SEM_SPEC] + [HBM_SPEC] * (2 * n) + [pl.BlockSpec(memory_space=pltpu.VMEM)],
        out_shape=[sems, sems] + thru + [jax.ShapeDtypeStruct((8, LANES), F32)],
        input_output_aliases={j: 2 + j for j in range(2 * n)},
        compiler_params=pltpu.CompilerParams(has_side_effects=EFFECT),
    )(*[pltpu.with_memory_space_constraint(a, pltpu.HBM) for a in list(srcs) + list(lands)])
    return dict(send=res[0], recv=res[1], srcs=res[2:2 + n], lands=res[2 + n:2 + 2 * n], whole=whole, token=res[-1])


def exchange_wait(handle, after, name):
    n = len(handle["srcs"])
    whole = handle["whole"]

    def body(*refs):
        for copy in _copies(refs[:n], refs[n:2 * n], whole, refs[2 * n], refs[2 * n + 1], False):
            copy.wait_send()
        for copy in _copies(refs[:n], refs[n:2 * n], whole, refs[2 * n], refs[2 * n + 1], True):
            copy.wait_recv()

    bufs = list(handle["srcs"]) + list(handle["lands"])
    res = pl.pallas_call(
        body, name=name, in_specs=[HBM_SPEC] * (2 * n) + [SEM_SPEC, SEM_SPEC] + [pl.BlockSpec(memory_space=pl.ANY)] * len(after),
        out_specs=[HBM_SPEC] * (2 * n), out_shape=[pltpu.HBM(a.shape, a.dtype) for a in bufs],
        input_output_aliases={j: j for j in range(2 * n)},
        compiler_params=pltpu.CompilerParams(has_side_effects=EFFECT),
    )(*bufs, handle["send"], handle["recv"], *after)
    return list(res[n:])


def _landing(own, whole):
    me = _me()
    if not whole:
        own = lax.dynamic_index_in_dim(own, me, 0, keepdims=False)
    buf = lax.empty((N_DEV,) + own.shape, own.dtype)
    return lax.dynamic_update_slice(buf, own[None], (me,) + (0,) * own.ndim)


def _row(v):
    return v.reshape(1, -1)


def _pad_lanes(v, n):
    return jnp.pad(v, ((0, 0), (0, n - v.shape[1])))


TRANSPOSED = ("ffn1_w_in", "ffn2_w_in")

GATHER_GROUPS = (
    ("ffn1_in_0", (("ffn1_w_in", 0),)),
    ("ffn1_out_0", (("ffn1_w_out", 0),)),
    ("hgrn", (("hgrn_w_in", 0), ("hgrn_w_out", 0))),
    ("rest_0", (("ffn2_w_in", 0), ("ffn2_w_out", 0), ("ple_w_gate", 0), ("ple_w_proj", 0), ("fox_w_kvf", 0))),
    ("ffn1_1", (("ffn1_w_in", 1), ("ffn1_w_out", 1))),
    ("fox", (("fox_w_qg", 0), ("fox_w_out", 0))),
    ("rest_1", (("ffn2_w_in", 1), ("ffn2_w_out", 1), ("ple_w_gate", 1), ("ple_w_proj", 1))),
)


def kernel(x, p, ffn1_norm_pre, ffn1_w_in, ffn1_w_out, ffn1_norm_post, mix_norm_pre, mix_norm_post, ffn2_norm_pre, ffn2_w_in, ffn2_w_out, ffn2_norm_post, hgrn_w_in, hgrn_lb_logits, hgrn_out_norm, hgrn_w_out, kv_norm, fox_w_kvf, fox_b_f, fox_w_qg, fox_w_out, ple_norm_pre, ple_w_gate, ple_w_proj, ple_norm_post, loss_target, m_ffn1_norm_pre, m_ffn1_w_in, m_ffn1_w_out, m_ffn1_norm_post, m_mix_norm_pre, m_mix_norm_post, m_ffn2_norm_pre, m_ffn2_w_in, m_ffn2_w_out, m_ffn2_norm_post, m_hgrn_w_in, m_hgrn_lb_logits, m_hgrn_out_norm, m_hgrn_w_out, m_kv_norm, m_fox_w_kvf, m_fox_b_f, m_fox_w_qg, m_fox_w_out, m_ple_norm_pre, m_ple_w_gate, m_ple_w_proj, m_ple_norm_post, v_ffn1_norm_pre, v_ffn1_w_in, v_ffn1_w_out, v_ffn1_norm_post, v_mix_norm_pre, v_mix_norm_post, v_ffn2_norm_pre, v_ffn2_w_in, v_ffn2_w_out, v_ffn2_norm_post, v_hgrn_w_in, v_hgrn_lb_logits, v_hgrn_out_norm, v_hgrn_w_out, v_kv_norm, v_fox_w_kvf, v_fox_b_f, v_fox_w_qg, v_fox_w_out, v_ple_norm_pre, v_ple_w_gate, v_ple_w_proj, v_ple_norm_post):
    weights = dict(ffn1_norm_pre=ffn1_norm_pre, ffn1_w_in=ffn1_w_in, ffn1_w_out=ffn1_w_out, ffn1_norm_post=ffn1_norm_post, mix_norm_pre=mix_norm_pre, mix_norm_post=mix_norm_post, ffn2_norm_pre=ffn2_norm_pre, ffn2_w_in=ffn2_w_in, ffn2_w_out=ffn2_w_out, ffn2_norm_post=ffn2_norm_post, hgrn_w_in=hgrn_w_in, hgrn_lb_logits=hgrn_lb_logits, hgrn_out_norm=hgrn_out_norm, hgrn_w_out=hgrn_w_out, kv_norm=kv_norm, fox_w_kvf=fox_w_kvf, fox_b_f=fox_b_f, fox_w_qg=fox_w_qg, fox_w_out=fox_w_out, ple_norm_pre=ple_norm_pre, ple_w_gate=ple_w_gate, ple_w_proj=ple_w_proj, ple_norm_post=ple_norm_post)
    mom1 = dict(ffn1_norm_pre=m_ffn1_norm_pre, ffn1_w_in=m_ffn1_w_in, ffn1_w_out=m_ffn1_w_out, ffn1_norm_post=m_ffn1_norm_post, mix_norm_pre=m_mix_norm_pre, mix_norm_post=m_mix_norm_post, ffn2_norm_pre=m_ffn2_norm_pre, ffn2_w_in=m_ffn2_w_in, ffn2_w_out=m_ffn2_w_out, ffn2_norm_post=m_ffn2_norm_post, hgrn_w_in=m_hgrn_w_in, hgrn_lb_logits=m_hgrn_lb_logits, hgrn_out_norm=m_hgrn_out_norm, hgrn_w_out=m_hgrn_w_out, kv_norm=m_kv_norm, fox_w_kvf=m_fox_w_kvf, fox_b_f=m_fox_b_f, fox_w_qg=m_fox_w_qg, fox_w_out=m_fox_w_out, ple_norm_pre=m_ple_norm_pre, ple_w_gate=m_ple_w_gate, ple_w_proj=m_ple_w_proj, ple_norm_post=m_ple_norm_post)
    mom2 = dict(ffn1_norm_pre=v_ffn1_norm_pre, ffn1_w_in=v_ffn1_w_in, ffn1_w_out=v_ffn1_w_out, ffn1_norm_post=v_ffn1_norm_post, mix_norm_pre=v_mix_norm_pre, mix_norm_post=v_mix_norm_post, ffn2_norm_pre=v_ffn2_norm_pre, ffn2_w_in=v_ffn2_w_in, ffn2_w_out=v_ffn2_w_out, ffn2_norm_post=v_ffn2_norm_post, hgrn_w_in=v_hgrn_w_in, hgrn_lb_logits=v_hgrn_lb_logits, hgrn_out_norm=v_hgrn_out_norm, hgrn_w_out=v_hgrn_w_out, kv_norm=v_kv_norm, fox_w_kvf=v_fox_w_kvf, fox_b_f=v_fox_b_f, fox_w_qg=v_fox_w_qg, fox_w_out=v_fox_w_out, ple_norm_pre=v_ple_norm_pre, ple_w_gate=v_ple_w_gate, ple_w_proj=v_ple_w_proj, ple_norm_post=v_ple_norm_post)
    names = list(weights)
    big = ["ffn1_w_in", "ffn1_w_out", "ffn2_w_in", "ffn2_w_out", "hgrn_w_in", "hgrn_w_out", "fox_w_kvf", "fox_w_qg",
           "fox_w_out", "ple_w_gate", "ple_w_proj"]
    small_names = [n for n in names if n not in big]

    T, D = x.shape[1], x.shape[2]
    depth = p.shape[0]
    h0 = x.reshape(T, D)
    target = loss_target.reshape(T, D)
    p3 = p.reshape(depth, T, p.shape[3])
    n_fox = fox_b_f.shape[0]
    fox_w = n_fox * FOX_HEAD_DIM
    fcol = 2 * fox_w // LANES
    b_row = _pad_lanes(_row(fox_b_f), LANES)

    tok = jnp.zeros((), F32)
    handles = {}
    for gname, keys in GATHER_GROUPS:
        shards = []
        for n, l in keys:
            w = weights[n]
            w = w[l] if w.ndim == 3 else w
            shards.append(((w.T if n in TRANSPOSED else w) + tok).astype(BF))
        handles[gname] = exchange_start(shards, [_landing(s, True) for s in shards], [True] * len(keys), f"gather_start_{gname}")
        tok = handles[gname]["token"][0, 0]
    W = {}

    def arrive(gname, after):
        lands = exchange_wait(handles[gname], after, f"gather_wait_{gname}")
        W.update(dict(zip(dict(GATHER_GROUPS)[gname], lands)))

    def w_rows(n, l):
        return W[n, l].reshape(-1, D)

    norm = lambda name, i: weights[name][i:i + 1]

    saved = []
    h = h0
    kvf = ct = kvf_w = None
    tq = _tile(T, 512)
    for i in range(depth):
        s = {}
        if i == 0:
            arrive("ffn1_in_0", [handles[GATHER_GROUPS[-1][0]]["token"]])
        for k in (1, 2):
            if k == 2:
                s["h_a"] = h
                if i == 0:
                    arrive("hgrn", [h])
                    z, xn = norm_mm(h, norm("mix_norm_pre", i), W["hgrn_w_in", 0], "hgrn_in")
                    xm, o, states = hgrn_fwd(z, hgrn_lb_logits, hgrn_out_norm, "hgrn_scan")
                    s.update(z=z, o=o, states=states)
                    wmix = w_rows("hgrn_w_out", 0)
                else:
                    arrive("fox", [h])
                    qg, xn = norm_mm(h, norm("mix_norm_pre", i), W["fox_w_qg", 0], "fox_qg")
                    ka, kb, vt3 = fox_prep(kvf, ct, n_fox, tq)
                    xm, o, lse = fox_fwd(qg, ka, kb, vt3, "fox_attn")
                    s.update(qg=qg, o=o, lse=lse, ka=ka, kb=kb)
                    wmix = w_rows("fox_w_out", 0)
                h, ym = mm_norm_res(xm, wmix, norm("mix_norm_post", i), h, 1.0, f"mix_out_{i}", kb=_tile(xm.shape[1], 512))
                s.update(xm=xm, ym=ym, xn_mix=xn)
                arrive(f"rest_{i}", [h])
            gate, up, a, xn = norm_mm_swiglu(h, norm(f"ffn{k}_norm_pre", i), W[f"ffn{k}_w_in", i], f"ffn{k}_in_{i}")
            if (i, k) == (0, 1):
                arrive("ffn1_out_0", [a])
            hn, y = mm_norm_res(a, w_rows(f"ffn{k}_w_out", i), norm(f"ffn{k}_norm_post", i), h, 0.5, f"ffn{k}_out_{i}")
            s[f"ffn{k}"] = (h, gate, up, a, y, xn)
            h = hn
        s["h_c"] = h
        ple_proj = W["ple_w_proj", i].transpose(1, 0, 2).reshape(p.shape[3], D)
        h, pgate, pp = ple_fwd(h, norm("ple_norm_pre", i), w_rows("ple_w_gate", i), p3, ple_proj, norm("ple_norm_post", i),
                               i, f"ple_{i}")
        s.update(pgate=pgate, pp=pp)
        saved.append(s)
        if i == 0:
            kvf_nat = W["fox_w_kvf", 0].transpose(1, 0, 2).reshape(D, -1)
            kvf_cols = kvf_nat.shape[1]
            kvf_w = _pad_lanes(kvf_nat, 2 * fox_w + LANES)
            kvf, xn_kv = norm_mm(h, _row(kv_norm), kvf_w, "fox_kvf", tn=kvf_w.shape[1])
            ct = fox_cum_fwd(kvf, fcol, b_row, "fox_cum")
            h_kv = h
            arrive("ffn1_1", [h])

    dh, loss_part = loss_fwd_bwd(h, target, "loss")
    loss = lax.psum(loss_part[0, 0], ("x", "y", "c"))

    gsmall = {n: [None] * weights[n].shape[0] if weights[n].ndim == 2 else None for n in small_names}
    sent = []

    def send(gname, keys, srcs, whole):
        lands = [_landing(a, w) for a, w in zip(srcs, whole)]
        hd = exchange_start(srcs, lands, whole, f"scatter_start_{gname}")
        sent.append((gname, keys, hd))
        return hd["token"][0:1, 0:1]

    def send_grads(gname, grads):
        return send(gname, list(grads), list(grads.values()), [False] * len(grads))

    for i in reversed(range(depth)):
        s = saved[i]
        grads = {}
        dh, du, dpp, xn, dgpre, dgpost = ple_bwd(dh, s["h_c"], s["pgate"], s["pp"], norm("ple_norm_pre", i),
                                                 w_rows("ple_w_gate", i), norm("ple_norm_post", i), f"ple_bwd_{i}")
        gsmall["ple_norm_pre"][i], gsmall["ple_norm_post"][i] = dgpre, dgpost
        grads["ple_w_gate", i] = mm_tn(xn, du, f"ple_dgate_{i}", xb=_tile(D, 512)).reshape(N_DEV, -1, D)
        dproj = mm_tn(p3, dpp, f"ple_dproj_{i}", x_layer=i, yb=D)[0]
        grads["ple_w_proj", i] = dproj.reshape(dproj.shape[0], N_DEV, -1).transpose(1, 0, 2)
        for k in (2, 1):
            hin, gate, up, a, y, xn = s[f"ffn{k}"]
            ffn_cs = gate.shape[2]
            dy, dgpost, dz = nbwd_mm_nt(dh, y, norm(f"ffn{k}_norm_post", i), w_rows(f"ffn{k}_w_out", i), 0.5, ffn_cs,
                                        f"ffn{k}_bwd_out_{i}", gate=gate, up=up)
            dz = dz.reshape(-1, T, ffn_cs)
            grads[f"ffn{k}_w_out", i] = mm_tn(a, dy, f"ffn{k}_dwout_{i}").reshape(N_DEV, -1, D)
            grads[f"ffn{k}_w_in", i] = mm_tn(dz, xn, f"ffn{k}_dwin_{i}").reshape(N_DEV, ffn_cs, D)
            tokv = send_grads(f"ffn{k}_{i}", grads)
            grads = {}
            dh, dgpre = mm_nt_nbwd(dz, W[f"ffn{k}_w_in", i], hin, norm(f"ffn{k}_norm_pre", i) + tokv, dh, f"ffn{k}_bwd_in_{i}",
                                   transposed=True)
            gsmall[f"ffn{k}_norm_pre"][i], gsmall[f"ffn{k}_norm_post"][i] = dgpre, dgpost
            if k == 2:
                nm = "hgrn" if i == 0 else "fox"
                wmix = w_rows(f"{nm}_w_out", 0)
                dy, dgpost, dxm = nbwd_mm_nt(dh, s["ym"], norm("mix_norm_post", i), wmix, 1.0, _tile(wmix.shape[0], 512),
                                             f"{nm}_bwd_out")
                grads[f"{nm}_w_out", 0] = mm_tn(s["xm"], dy, f"{nm}_dwout", xb=_tile(wmix.shape[0], 512)).reshape(N_DEV, -1, D)
                gsmall["mix_norm_post"][i] = dgpost
                if i == 0:
                    dq, df, dv, dg, dlog, don = hgrn_bwd(dxm, s["z"], s["o"], s["states"], hgrn_lb_logits, hgrn_out_norm,
                                                         "hgrn_scan_bwd")
                    gsmall["hgrn_lb_logits"] = [dlog[0:1], dlog[1:2]]
                    gsmall["hgrn_out_norm"] = [don]
                    dzm = jnp.concatenate([dq, df, dv, dg], axis=1)
                    nmin = "hgrn_w_in"
                else:
                    do, dg, dsum = fox_gate_bwd(dxm, s["o"], s["qg"], "fox_gate_bwd")
                    dsum4 = dsum[:, :n_fox].T.reshape(n_fox // 2, 2, T // tq, tq).transpose(0, 2, 1, 3)
                    dqt, dk_sh, dv_sh, dc4, dcq4 = fox_bwd(*fox_bwd_prep(s["qg"], kvf, do, tq), s["ka"], s["kb"], s["lse"], dsum4,
                                                           "fox_attn_bwd")
                    dq = dqt.transpose(1, 3, 0, 2).reshape(T, fox_w)
                    dzm = jnp.concatenate([dq.astype(BF), dg], axis=1)
                    nmin = "fox_w_qg"
                wmin = W[nmin, 0]
                grads[nmin, 0] = mm_tn(s["xn_mix"], dzm, f"{nm}_dwin", yb=wmin.shape[2])
                tokv = send_grads(f"mix_{i}", grads)
                grads = {}
                dh, dgpre = mm_nt_nbwd(dzm, wmin, s["h_a"], norm("mix_norm_pre", i) + tokv, dh, f"{nm}_bwd_in", tn=wmin.shape[2])
                gsmall["mix_norm_pre"][i] = dgpre
        if i == 1:
            dct = (dc4 + dcq4).transpose(0, 2, 1, 3).reshape(n_fox, T)
            dct = jnp.pad(dct, ((0, LANES - n_fox), (0, 0)))
            dflog, db = fox_cum_bwd(dct, kvf, fcol, b_row, "fox_cum_bwd")
            gsmall["fox_b_f"] = db[:, :n_fox]
            dkvf = jnp.concatenate([dk_sh, dv_sh, dflog], axis=1)
            dwk = mm_tn(xn_kv, dkvf, "fox_dwkvf", yb=kvf_w.shape[1])[0][:, :kvf_cols]
            tokv = send_grads("kvf", {("fox_w_kvf", 0): dwk.reshape(D, N_DEV, -1).transpose(1, 0, 2)})
            dh, dgkv = mm_nt_nbwd(dkvf, kvf_w, h_kv, _row(kv_norm) + tokv, dh, "fox_kvf_bwd", tn=kvf_w.shape[1])
            gsmall["kv_norm"] = dgkv
    grad_x = dh.reshape(x.shape)

    def small_rows(n):
        g = gsmall[n]
        rows = g if isinstance(g, list) else [g]
        return [_pad_lanes(r, D) for r in rows]

    counts = {n: len(small_rows(n)) for n in small_names}
    packed = jnp.concatenate([r for n in small_names for r in small_rows(n)], axis=0)
    n_rows = packed.shape[0]
    packed = jnp.pad(packed, ((0, -n_rows % 8), (0, 0)))
    send("small", ["small"], [packed], [True])

    res = {}
    after = [dh]
    small_parts = None
    for gname, keys, hd in sent:
        lands = exchange_wait(hd, after, f"scatter_wait_{gname}")
        after = []
        for key, parts in zip(keys, lands):
            if key == "small":
                small_parts = parts
                continue
            n, l = key
            w = weights[n]
            if w.ndim == 2:
                as3 = lambda a: a.reshape((1,) + a.shape)
            elif n in TRANSPOSED:
                as3 = lambda a: a.transpose(0, 2, 1)
            else:
                as3 = lambda a: a
            res[n] = adamw(parts, as3(w), as3(mom1[n]), as3(mom2[n]), l, res.get(n), f"adamw_{n}_{l}")
            after.append(res[n][1])
    for n in TRANSPOSED:
        res[n] = [a.transpose(0, 2, 1) for a in res[n]]

    def pack(d):
        rows = []
        for n in small_names:
            a = d[n]
            rows.append(_pad_lanes(a.reshape(-1, a.shape[-1]), D))
        a = jnp.concatenate(rows, axis=0)
        return jnp.pad(a, ((0, -n_rows % 8), (0, 0)))[None]

    sm = adamw(small_parts, pack(weights), pack(mom1), pack(mom2), 0, None, "adamw_small")
    off = 0
    for n in small_names:
        w = weights[n]
        res[n] = [a[0, off:off + counts[n], :w.shape[-1]].reshape(w.shape) for a in sm]
        off += counts[n]

    out = [loss, grad_x]
    for j in range(4):
        out += [res[n][j].reshape(weights[n].shape) for n in names]
    return tuple(out)
```

```python
import functools

import jax
import jax.numpy as jnp
from jax import lax
from jax.experimental import pallas as pl
from jax.experimental.pallas import tpu as pltpu

F32 = jnp.float32
BF = jnp.bfloat16
NORM_EPS = 1e-6
N_DEV = 8
HGRN_DK = 128
HGRN_CHUNK = 16
HGRN_HEADS_PER_STEP = 4
HALF = HGRN_CHUNK // 2
FOX_HEAD_DIM = 64
LANES = 128
ADAM_LR, ADAM_B1, ADAM_B2, ADAM_EPS, ADAM_WD, ADAM_STEP = 0.001, 0.9, 0.999, 1e-08, 0.01, 10
VMEM_LIMIT = 56 * 1024 * 1024
HI = lax.Precision.HIGHEST
NT = (((1,), (1,)), ((), ()))
TN = (((0,), (0,)), ((), ()))


def _params(n_axes):
    return pltpu.CompilerParams(dimension_semantics=("arbitrary",) * n_axes, vmem_limit_bytes=VMEM_LIMIT)


def _tile(n, want):
    t = min(n, want)
    while n % t:
        t //= 2
    return t


def _sigmoid(x):
    return 1.0 / (1.0 + jnp.exp(-x))


def _rms(x):
    r = lax.rsqrt(jnp.mean(x * x, axis=-1, keepdims=True) + NORM_EPS)
    return x * r, r


def _norm_bwd(dy, xhat, r, g):
    dxh = dy * g
    return r * (dxh - xhat * jnp.mean(dxh * xhat, axis=-1, keepdims=True))


def _colsum(x):
    return jnp.sum(x, axis=0, keepdims=True)


def _w_spec(w, blk):
    if w.ndim == 3:
        return lambda off: pl.BlockSpec((None, w.shape[1], w.shape[2]), lambda i, j: (j + off, 0, 0))
    return lambda off: pl.BlockSpec((w.shape[0], blk), lambda i, j: (0, j + off))


def norm_mm_swiglu(h, g, w3, name):
    T, D = h.shape
    nb, cs, _ = w3.shape
    nh = nb // 2
    tm = _tile(T, 1024)

    def body(h_ref, g_ref, wg_ref, wu_ref, gate_ref, up_ref, a_ref, xn_ref):
        @pl.when(pl.program_id(1) == 0)
        def _():
            xh, _ = _rms(h_ref[...])
            xn_ref[...] = (xh * g_ref[...]).astype(BF)

        xn = xn_ref[...]
        gt = lax.dot_general(xn, wg_ref[...], NT, preferred_element_type=F32)
        up = lax.dot_general(xn, wu_ref[...], NT, preferred_element_type=F32)
        sg = _sigmoid(gt)
        silu = gt * sg
        gate_ref[...] = (up * (sg * (1.0 + gt * (1.0 - sg)))).astype(BF)
        up_ref[...] = silu.astype(BF)
        a_ref[...] = (silu * up).astype(BF)

    ws = _w_spec(w3, cs)
    blk = pl.BlockSpec((None, tm, cs), lambda i, j: (j, i, 0))
    shp = jax.ShapeDtypeStruct((nh, T, cs), BF)
    return pl.pallas_call(
        body, name=name, grid=(T // tm, nh),
        in_specs=[pl.BlockSpec((tm, D), lambda i, j: (i, 0)), pl.BlockSpec((1, D), lambda i, j: (0, 0)), ws(0), ws(nh)],
        out_specs=[blk, blk, blk, pl.BlockSpec((tm, D), lambda i, j: (i, 0))],
        out_shape=[shp, shp, shp, jax.ShapeDtypeStruct((T, D), BF)], compiler_params=_params(2),
    )(h, g, w3, w3)


def norm_mm(h, g, w, name, tn=None):
    T, D = h.shape
    if w.ndim == 3:
        nb, cs = w.shape[0], w.shape[2]
    else:
        cs = tn
        nb = w.shape[1] // cs
    tm = _tile(T, 1024)

    def body(h_ref, g_ref, w_ref, z_ref, xn_ref):
        @pl.when(pl.program_id(1) == 0)
        def _():
            xh, _ = _rms(h_ref[...])
            xn_ref[...] = (xh * g_ref[...]).astype(BF)

        z_ref[...] = jnp.dot(xn_ref[...], w_ref[...], preferred_element_type=F32)

    return pl.pallas_call(
        body, name=name, grid=(T // tm, nb),
        in_specs=[pl.BlockSpec((tm, D), lambda i, j: (i, 0)), pl.BlockSpec((1, D), lambda i, j: (0, 0)),
                  _w_spec(w, cs)(0)],
        out_specs=[pl.BlockSpec((tm, cs), lambda i, j: (i, j)), pl.BlockSpec((tm, D), lambda i, j: (i, 0))],
        out_shape=[jax.ShapeDtypeStruct((T, nb * cs), F32), jax.ShapeDtypeStruct((T, D), BF)], compiler_params=_params(2),
    )(h, g, w)


def _x_spec(x, tm, kb):
    if x.ndim == 3:
        return pl.BlockSpec((None, tm, x.shape[2]), lambda i, j: (j, i, 0))
    return pl.BlockSpec((tm, kb), lambda i, j: (i, j))


def mm_norm_res(x, w2, g, h, coef, name, kb=None):
    T, D = h.shape
    if x.ndim == 3:
        nb, kb = x.shape[0], x.shape[2]
    else:
        nb = x.shape[1] // kb
    tm = _tile(T, 1024)

    def body(x_ref, w_ref, h_ref, g_ref, hn_ref, y_ref, acc_ref):
        b = pl.program_id(1)

        @pl.when(b == 0)
        def _():
            acc_ref[...] = jnp.zeros_like(acc_ref)

        acc_ref[...] += jnp.dot(x_ref[...], w_ref[...], preferred_element_type=F32)

        @pl.when(b == nb - 1)
        def _():
            y = acc_ref[...]
            y_ref[...] = y
            yh, _ = _rms(y)
            hn_ref[...] = h_ref[...] + coef * (yh * g_ref[...])

    tok = pl.BlockSpec((tm, D), lambda i, j: (i, 0))
    shp = jax.ShapeDtypeStruct((T, D), F32)
    return pl.pallas_call(
        body, name=name, grid=(T // tm, nb),
        in_specs=[_x_spec(x, tm, kb), pl.BlockSpec((kb, D), lambda i, j: (j, 0)), tok,
                  pl.BlockSpec((1, D), lambda i, j: (0, 0))],
        out_specs=[tok, tok], out_shape=[shp, shp],
        scratch_shapes=[pltpu.VMEM((tm, D), F32)], compiler_params=_params(2),
    )(x, w2, h, g)


def nbwd_mm_nt(dout, y, g, w2, coef, kb, name, gate=None, up=None):
    T, D = dout.shape
    nb = w2.shape[0] // kb
    swiglu = gate is not None
    tm = _tile(T, 1024 if swiglu else 512)

    def body(*refs):
        if swiglu:
            dout_ref, y_ref, g_ref, w_ref, gate_ref, up_ref, dy_ref, dg_ref, da_ref, dys_ref = refs
        else:
            dout_ref, y_ref, g_ref, w_ref, dy_ref, dg_ref, da_ref, dys_ref = refs
        i, b = pl.program_id(0), pl.program_id(1)

        @pl.when((i == 0) & (b == 0))
        def _():
            dg_ref[...] = jnp.zeros_like(dg_ref)

        @pl.when(b == 0)
        def _():
            yh, r = _rms(y_ref[...])
            dyn = coef * dout_ref[...]
            dg_ref[...] += _colsum(dyn * yh)
            dy = _norm_bwd(dyn, yh, r, g_ref[...]).astype(BF)
            dys_ref[...] = dy
            dy_ref[...] = dy

        da = lax.dot_general(dys_ref[...], w_ref[...], NT, preferred_element_type=F32)
        if swiglu:
            da_ref[0] = (da * gate_ref[...].astype(F32)).astype(BF)
            da_ref[1] = (da * up_ref[...].astype(F32)).astype(BF)
        else:
            da_ref[...] = da.astype(BF)

    tok = pl.BlockSpec((tm, D), lambda i, j: (i, 0))
    vec = pl.BlockSpec((1, D), lambda i, j: (0, 0))
    in_specs = [tok, tok, vec, pl.BlockSpec((kb, D), lambda i, j: (j, 0))]
    args = [dout, y, g, w2]
    if swiglu:
        blk = pl.BlockSpec((None, tm, kb), lambda i, j: (j, i, 0))
        in_specs += [blk, blk]
        args += [gate, up]
        da_spec = pl.BlockSpec((2, None, tm, kb), lambda i, j: (0, j, i, 0))
        da_shape = jax.ShapeDtypeStruct((2, nb, T, kb), BF)
    else:
        da_spec = pl.BlockSpec((tm, kb), lambda i, j: (i, j))
        da_shape = jax.ShapeDtypeStruct((T, nb * kb), BF)
    return pl.pallas_call(
        body, name=name, grid=(T // tm, nb), in_specs=in_specs,
        out_specs=[tok, vec, da_spec],
        out_shape=[jax.ShapeDtypeStruct((T, D), BF), jax.ShapeDtypeStruct((1, D), F32), da_shape],
        scratch_shapes=[pltpu.VMEM((tm, D), BF)], compiler_params=_params(2),
    )(*args)


def mm_nt_nbwd(dz, w, h, g, dout, name, tn=None, transposed=False):
    T, D = h.shape
    if w.ndim == 3:
        nb, cs = w.shape[0], w.shape[1 if transposed else 2]
    else:
        cs = tn
        nb = w.shape[1] // cs
    tm = _tile(T, 1024)

    def body(dz_ref, w_ref, h_ref, g_ref, dout_ref, dh_ref, dg_ref, acc_ref):
        i, b = pl.program_id(0), pl.program_id(1)

        @pl.when((i == 0) & (b == 0))
        def _():
            dg_ref[...] = jnp.zeros_like(dg_ref)

        @pl.when(b == 0)
        def _():
            acc_ref[...] = jnp.zeros_like(acc_ref)

        if transposed:
            acc_ref[...] += jnp.dot(dz_ref[...], w_ref[...], preferred_element_type=F32)
        else:
            acc_ref[...] += lax.dot_general(dz_ref[...], w_ref[...], NT, preferred_element_type=F32)

        @pl.when(b == nb - 1)
        def _():
            xh, r = _rms(h_ref[...])
            gg = g_ref[...]
            dxn = acc_ref[...]
            dg_ref[...] += _colsum(dxn * xh)
            dh_ref[...] = dout_ref[...] + _norm_bwd(dxn, xh, r, gg)

    tok = pl.BlockSpec((tm, D), lambda i, j: (i, 0))
    vec = pl.BlockSpec((1, D), lambda i, j: (0, 0))
    return pl.pallas_call(
        body, name=name, grid=(T // tm, nb),
        in_specs=[_x_spec(dz, tm, cs), _w_spec(w, cs)(0), tok, vec, tok],
        out_specs=[tok, vec],
        out_shape=[jax.ShapeDtypeStruct((T, D), F32), jax.ShapeDtypeStruct((1, D), F32)],
        scratch_shapes=[pltpu.VMEM((tm, D), F32)], compiler_params=_params(2),
    )(dz, w, h, g, dout)


def mm_tn(x, y, name, xb=None, yb=None, x_layer=None):
    T = y.shape[-2]
    wide = (yb if yb is not None else y.shape[-1]) > 1024
    tt = _tile(T, 1024 if wide else 2048)
    x_split = (x.ndim == 3 and x_layer is None) or xb is not None
    if x_layer is not None:
        xs = pl.BlockSpec((None, tt, x.shape[2]), lambda b, t: (x_layer, t, 0))
        kdim = x.shape[2]
    elif x.ndim == 3:
        xs = pl.BlockSpec((None, tt, x.shape[2]), lambda b, t: (b, t, 0))
        nb, kdim = x.shape[0], x.shape[2]
    elif xb is not None:
        xs = pl.BlockSpec((tt, xb), lambda b, t: (t, b))
        nb, kdim = x.shape[1] // xb, xb
    else:
        xs = pl.BlockSpec((tt, x.shape[1]), lambda b, t: (t, 0))
        kdim = x.shape[1]
    if x_split:
        ys = pl.BlockSpec((tt, y.shape[1]), lambda b, t: (t, 0))
        ndim = y.shape[1]
        out_spec = pl.BlockSpec((kdim, ndim), lambda b, t: (b, 0))
        out_shape = jax.ShapeDtypeStruct((nb * kdim, ndim), BF)
    else:
        if y.ndim == 3:
            ys = pl.BlockSpec((None, tt, y.shape[2]), lambda b, t: (b, t, 0))
            nb, ndim = y.shape[0], y.shape[2]
        else:
            ys = pl.BlockSpec((tt, yb), lambda b, t: (t, b))
            nb, ndim = y.shape[1] // yb, yb
        out_spec = pl.BlockSpec((None, kdim, ndim), lambda b, t: (b, 0, 0))
        out_shape = jax.ShapeDtypeStruct((nb, kdim, ndim), BF)
    nt = T // tt

    def body(x_ref, y_ref, o_ref, acc_ref):
        t = pl.program_id(1)

        @pl.when(t == 0)
        def _():
            acc_ref[...] = jnp.zeros_like(acc_ref)

        acc_ref[...] += lax.dot_general(x_ref[...].astype(BF), y_ref[...].astype(BF), TN, preferred_element_type=F32)

        @pl.when(t == nt - 1)
        def _():
            o_ref[...] = acc_ref[...].astype(BF)

    return pl.pallas_call(
        body, name=name, grid=(nb, nt), in_specs=[xs, ys], out_specs=out_spec, out_shape=out_shape,
        scratch_shapes=[pltpu.VMEM((kdim, ndim), F32)], compiler_params=_params(2),
    )(x, y)


def ple_fwd(h, gpre, wg, p3, wp, gpost, layer, name):
    T, D = h.shape
    pd = p3.shape[2]
    tm = _tile(T, 512)

    def body(h_ref, gpre_ref, wg_ref, p_ref, wp_ref, gpost_ref, hn_ref, gate_ref, pp_ref):
        x = h_ref[...]
        xh, _ = _rms(x)
        u = jnp.dot((xh * gpre_ref[...]).astype(BF), wg_ref[...], preferred_element_type=F32)
        gate = _sigmoid(u)
        pp = jnp.dot(p_ref[...].astype(BF), wp_ref[...], preferred_element_type=F32)
        yh, _ = _rms(gate * pp)
        hn_ref[...] = x + yh * gpost_ref[...]
        gate_ref[...] = gate.astype(BF)
        pp_ref[...] = pp.astype(BF)

    tok = pl.BlockSpec((tm, D), lambda i: (i, 0))
    vec = pl.BlockSpec((1, D), lambda i: (0, 0))
    return pl.pallas_call(
        body, name=name, grid=(T // tm,),
        in_specs=[tok, vec, pl.BlockSpec((D, D), lambda i: (0, 0)),
                  pl.BlockSpec((None, tm, pd), lambda i: (layer, i, 0)),
                  pl.BlockSpec((pd, D), lambda i: (0, 0)), vec],
        out_specs=[tok, tok, tok],
        out_shape=[jax.ShapeDtypeStruct((T, D), F32), jax.ShapeDtypeStruct((T, D), BF), jax.ShapeDtypeStruct((T, D), BF)],
        compiler_params=_params(1),
    )(h, gpre, wg, p3, wp, gpost)


def ple_bwd(dout, h, gate, pp, gpre, wg, gpost, name):
    T, D = h.shape
    tm = _tile(T, 512)

    def body(dout_ref, h_ref, gate_ref, pp_ref, gpre_ref, wg_ref, gpost_ref, dh_ref, du_ref, dpp_ref, xn_ref, dgpre_ref, dgpost_ref):
        @pl.when(pl.program_id(0) == 0)
        def _():
            dgpre_ref[...] = jnp.zeros_like(dgpre_ref)
            dgpost_ref[...] = jnp.zeros_like(dgpost_ref)

        dout = dout_ref[...]
        gate = gate_ref[...].astype(F32)
        pp = pp_ref[...].astype(F32)
        yh, ry = _rms(gate * pp)
        dgpost_ref[...] += _colsum(dout * yh)
        dy = _norm_bwd(dout, yh, ry, gpost_ref[...])
        dpp_ref[...] = (dy * gate).astype(BF)
        du = (dy * pp * gate * (1.0 - gate)).astype(BF)
        du_ref[...] = du
        dxn = lax.dot_general(du, wg_ref[...], NT, preferred_element_type=F32)
        xh, r = _rms(h_ref[...])
        gp = gpre_ref[...]
        dgpre_ref[...] += _colsum(dxn * xh)
        dh_ref[...] = dout + _norm_bwd(dxn, xh, r, gp)
        xn_ref[...] = (xh * gp).astype(BF)

    tok = pl.BlockSpec((tm, D), lambda i: (i, 0))
    vec = pl.BlockSpec((1, D), lambda i: (0, 0))
    bft = jax.ShapeDtypeStruct((T, D), BF)
    v32 = jax.ShapeDtypeStruct((1, D), F32)
    return pl.pallas_call(
        body, name=name, grid=(T // tm,),
        in_specs=[tok, tok, tok, tok, vec, pl.BlockSpec((D, D), lambda i: (0, 0)), vec],
        out_specs=[tok, tok, tok, tok, vec, vec],
        out_shape=[jax.ShapeDtypeStruct((T, D), F32), bft, bft, bft, v32, v32],
        compiler_params=_params(1),
    )(dout, h, gate, pp, gpre, wg, gpost)


def _chunk_tri(tb, upper):
    r = lax.broadcasted_iota(jnp.int32, (tb, tb), 0)
    c = lax.broadcasted_iota(jnp.int32, (tb, tb), 1)
    shift = HGRN_CHUNK.bit_length() - 1
    same = jnp.right_shift(r, shift) == jnp.right_shift(c, shift)
    return (same & ((c >= r) if upper else (c <= r))).astype(F32)


def _hgrn_gates(z, logits):
    lb = 1.0 / (1.0 + jnp.exp(logits[1:2, :] - logits[0:1, :]))
    e = jnp.exp(-jnp.abs(z))
    inv = 1.0 / (1.0 + e)
    sig = jnp.where(z >= 0, inv, e * inv)
    nsig = jnp.where(z >= 0, e * inv, inv)
    return lb, sig, nsig, lb + (1.0 - lb) * sig


def hgrn_fwd(z, lb_logits, out_norm, name):
    T = z.shape[0]
    W = z.shape[1] // 4
    H = W // HGRN_DK
    C = HGRN_CHUNK
    HB = _tile(H, HGRN_HEADS_PER_STEP)
    tb = _tile(T, 256)
    nch = tb // C

    def body(zq_ref, zf_ref, zv_ref, zg_ref, lbl_ref, on_ref, x_ref, o_ref, st_ref, s_scr, cum_scr, k_scr, v_scr, o_scr):
        @pl.when(pl.program_id(1) == 0)
        def _():
            s_scr[...] = jnp.zeros_like(s_scr)

        lb, sig, nsig, f = _hgrn_gates(zf_ref[...], lbl_ref[...])
        cum = jnp.dot(_chunk_tri(tb, False), jnp.log(f), precision=HI, preferred_element_type=F32)
        kk = (1.0 - lb) * nsig
        for hh in range(HB):
            cols = slice(hh * HGRN_DK, (hh + 1) * HGRN_DK)
            cum_scr[hh] = cum[:, cols]
            k_scr[hh] = kk[:, cols]
            v_scr[hh] = zv_ref[:, cols]
        row = lax.broadcasted_iota(jnp.int32, (C, HGRN_DK), 0)

        def chunk(c, carry):
            r0 = pl.multiple_of(c * C, C)
            rows = pl.ds(r0, C)
            last_row = pl.ds(r0 + C - 1, 1)
            heads = []
            for hh in range(HB):
                cols = slice(hh * HGRN_DK, (hh + 1) * HGRN_DK)
                q, cu = zq_ref[rows, cols], cum_scr[hh, rows, :]
                st = s_scr[hh]
                st_ref[c, hh] = st
                o = lax.dot_general((q * jnp.exp(cu)).astype(BF), st.astype(BF), NT, preferred_element_type=F32)
                last = cum_scr[hh, last_row, :]
                kg = (k_scr[hh, rows, :] * jnp.exp(last - cu)).astype(BF)
                s_scr[hh] = st * jnp.exp(last) + lax.dot_general(v_scr[hh, rows, :].astype(BF), kg, TN, preferred_element_type=F32)
                heads.append((hh, q, cu, o))
            for hh, q, cu, o in heads:
                qr = q.astype(BF).astype(F32)
                low = jnp.zeros((C - HALF, HGRN_DK), F32)
                for s in range(C):
                    one = pl.ds(r0 + s, 1)
                    sl = slice(0 if s < HALF else HALF, C)
                    e = jnp.exp(jnp.minimum(cu[sl] - cum_scr[hh, one, :], 0.0))
                    col = jnp.sum(qr[sl] * (e * k_scr[hh, one, :]).astype(BF).astype(F32), axis=-1, keepdims=True)
                    col = jnp.where(row[sl] >= s, col, 0.0).astype(BF).astype(F32)
                    term = col * v_scr[hh, one, :].astype(BF).astype(F32)
                    if s < HALF:
                        o = o + term
                    else:
                        low = low + term
                o_scr[hh, rows, :] = o
                o_scr[hh, pl.ds(r0 + HALF, C - HALF), :] += low
            return carry

        lax.fori_loop(0, nch, chunk, 0)
        for hh in range(HB):
            cols = slice(hh * HGRN_DK, (hh + 1) * HGRN_DK)
            o = o_scr[hh]
            o_ref[:, cols] = o
            oh, _ = _rms(o)
            g = zg_ref[:, cols]
            x_ref[:, cols] = (oh * on_ref[...] * (g * _sigmoid(g))).astype(BF)

    def zs(part):
        return pl.BlockSpec((tb, HB * HGRN_DK), lambda hd, i: (i, part * (H // HB) + hd))

    blk = pl.BlockSpec((tb, HB * HGRN_DK), lambda hd, i: (i, hd))
    wide = pltpu.VMEM((HB, tb, HGRN_DK), F32)
    return pl.pallas_call(
        body, name=name, grid=(H // HB, T // tb),
        in_specs=[zs(0), zs(1), zs(2), zs(3), pl.BlockSpec((2, HB * HGRN_DK), lambda hd, i: (0, hd)),
                  pl.BlockSpec((1, HGRN_DK), lambda hd, i: (0, 0))],
        out_specs=[blk, blk, pl.BlockSpec((nch, HB, HGRN_DK, HGRN_DK), lambda hd, i: (i, hd, 0, 0))],
        out_shape=[jax.ShapeDtypeStruct((T, W), BF), jax.ShapeDtypeStruct((T, W), F32),
                   jax.ShapeDtypeStruct((T // C, H, HGRN_DK, HGRN_DK), F32)],
        scratch_shapes=[pltpu.VMEM((HB, HGRN_DK, HGRN_DK), F32), wide, wide, wide, wide],
        compiler_params=_params(2),
    )(z, z, z, z, lb_logits, out_norm)


def hgrn_bwd(dx, z, o, states, lb_logits, out_norm, name):
    T = z.shape[0]
    W = z.shape[1] // 4
    H = W // HGRN_DK
    C = HGRN_CHUNK
    HB = _tile(H, HGRN_HEADS_PER_STEP)
    tb = _tile(T, 256)
    nch = tb // C
    nblk = T // tb

    def body(dx_ref, zq_ref, zf_ref, zv_ref, zg_ref, o_ref, st_ref, lbl_ref, on_ref,
             dq_ref, df_ref, dv_ref, dg_ref, dl_ref, don_ref,
             ds_scr, cum_scr, k_scr, v_scr, q_scr, do_scr, dq_scr, dk_scr, dv_scr, dcum_scr):
        hd, i = pl.program_id(0), pl.program_id(1)

        @pl.when(i == 0)
        def _():
            ds_scr[...] = jnp.zeros_like(ds_scr)
            dl_ref[...] = jnp.zeros_like(dl_ref)

        @pl.when((i == 0) & (hd == 0))
        def _():
            don_ref[...] = jnp.zeros_like(don_ref)

        lb, sig, nsig, f = _hgrn_gates(zf_ref[...], lbl_ref[...])
        cum = jnp.dot(_chunk_tri(tb, False), jnp.log(f), precision=HI, preferred_element_type=F32)
        kk = (1.0 - lb) * nsig
        w = on_ref[...]
        for hh in range(HB):
            cols = slice(hh * HGRN_DK, (hh + 1) * HGRN_DK)
            cum_scr[hh] = cum[:, cols]
            k_scr[hh] = kk[:, cols]
            v_scr[hh] = zv_ref[:, cols]
            q_scr[hh] = zq_ref[:, cols]
            oh, r = _rms(o_ref[:, cols])
            g = zg_ref[:, cols]
            sg = _sigmoid(g)
            dxv = dx_ref[:, cols].astype(F32)
            dg_ref[:, cols] = (dxv * (oh * w) * (sg * (1.0 + g * (1.0 - sg)))).astype(BF)
            don = dxv * (g * sg)
            don_ref[...] += _colsum(don * oh)
            do_scr[hh] = _norm_bwd(don, oh, r, w)
        row = lax.broadcasted_iota(jnp.int32, (C, HGRN_DK), 0)

        def chunk(cc, carry):
            c = nch - 1 - cc
            r0 = pl.multiple_of(c * C, C)
            rows = pl.ds(r0, C)
            last_row = pl.ds(r0 + C - 1, 1)
            heads = []
            for hh in range(HB):
                q, k, v, cu, do = q_scr[hh, rows, :], k_scr[hh, rows, :], v_scr[hh, rows, :], cum_scr[hh, rows, :], do_scr[hh, rows, :]
                st = st_ref[c, hh]
                dst = ds_scr[hh]
                last = cum_scr[hh, last_row, :]
                lam, gam, elast = jnp.exp(cu), jnp.exp(last - cu), jnp.exp(last)
                dob, dstb = do.astype(BF), dst.astype(BF)
                dq = jnp.dot(dob, st.astype(BF), preferred_element_type=F32) * lam
                dv = lax.dot_general((k * gam).astype(BF), dstb, NT, preferred_element_type=F32)
                dk = jnp.dot(v.astype(BF), dstb, preferred_element_type=F32) * gam
                dlast = elast * _colsum(dst * st) + _colsum(dk * k)
                ds_scr[hh] = dst * elast + lax.dot_general(dob, (q * lam).astype(BF), TN, preferred_element_type=F32)
                heads.append((hh, q, k, cu, do, dq, dk, dv, dlast))
            for hh, q, k, cu, do, dq, dk, dv, dlast in heads:
                for first in (True, False):
                    sl = slice(0 if first else HALF, C)
                    qs, cus, dos, rws = q[sl], cu[sl], do[sl], row[sl]
                    dqs, dks, dvs = dq[sl], dk[sl], dv[sl]
                    for s in (range(HALF) if first else range(HALF, C)):
                        one = pl.ds(r0 + s, 1)
                        e = jnp.where(rws >= s, jnp.exp(jnp.minimum(cus - cum_scr[hh, one, :], 0.0)), 0.0)
                        ks = k_scr[hh, one, :]
                        da = jnp.sum(dos * v_scr[hh, one, :], axis=-1, keepdims=True)
                        pq = qs * e
                        a = jnp.sum(pq * ks, axis=-1, keepdims=True)
                        dqs = dqs + da * e * ks
                        dks = jnp.where(rws == s, dks + _colsum(da * pq), dks)
                        dvs = jnp.where(rws == s, dvs + _colsum(a * dos), dvs)
                    if first:
                        dq, dk, dv = dqs, dks, dvs
                        top = pl.ds(r0, HALF)
                        dq_scr[hh, top, :] = dq[:HALF]
                        dk_scr[hh, top, :] = dk[:HALF]
                        dv_scr[hh, top, :] = dv[:HALF]
                        dcum_scr[hh, top, :] = q[:HALF] * dq[:HALF] - k[:HALF] * dk[:HALF]
                    else:
                        low = pl.ds(r0 + HALF, C - HALF)
                        dq_scr[hh, low, :] = dqs
                        dk_scr[hh, low, :] = dks
                        dv_scr[hh, low, :] = dvs
                        dcum_scr[hh, low, :] = qs * dqs - k[sl] * dks + jnp.where(rws == C - 1, dlast, 0.0)
            return carry

        lax.fori_loop(0, nch, chunk, 0)
        tri = _chunk_tri(tb, True)
        for hh in range(HB):
            cols = slice(hh * HGRN_DK, (hh + 1) * HGRN_DK)
            dlf = jnp.dot(tri, dcum_scr[hh], precision=HI, preferred_element_type=F32)
            dk = dk_scr[hh]
            lbh, sigh, nsigh, fh = lb[:, cols], sig[:, cols], nsig[:, cols], f[:, cols]
            common = (1.0 - lbh) * sigh * nsigh
            df_ref[:, cols] = (dlf * common / fh - dk * common).astype(BF)
            dq_ref[:, cols] = dq_scr[hh].astype(BF)
            dv_ref[:, cols] = dv_scr[hh].astype(BF)
            dl0 = _colsum(dlf * nsigh / fh - dk * nsigh) * lbh * (1.0 - lbh)
            dl_ref[:, cols] += jnp.where(lax.broadcasted_iota(jnp.int32, (2, HGRN_DK), 0) == 0, dl0, -dl0)

    def zs(part):
        return pl.BlockSpec((tb, HB * HGRN_DK), lambda hd, i: (nblk - 1 - i, part * (H // HB) + hd))

    blk = pl.BlockSpec((tb, HB * HGRN_DK), lambda hd, i: (nblk - 1 - i, hd))
    bft = jax.ShapeDtypeStruct((T, W), BF)
    scr = pltpu.VMEM((HB, tb, HGRN_DK), F32)
    return pl.pallas_call(
        body, name=name, grid=(H // HB, nblk),
        in_specs=[blk, zs(0), zs(1), zs(2), zs(3), blk,
                  pl.BlockSpec((nch, HB, HGRN_DK, HGRN_DK), lambda hd, i: (nblk - 1 - i, hd, 0, 0)),
                  pl.BlockSpec((2, HB * HGRN_DK), lambda hd, i: (0, hd)), pl.BlockSpec((1, HGRN_DK), lambda hd, i: (0, 0))],
        out_specs=[blk, blk, blk, blk, pl.BlockSpec((2, HB * HGRN_DK), lambda hd, i: (0, hd)),
                   pl.BlockSpec((1, HGRN_DK), lambda hd, i: (0, 0))],
        out_shape=[bft, bft, bft, bft, jax.ShapeDtypeStruct((2, W), F32), jax.ShapeDtypeStruct((1, HGRN_DK), F32)],
        scratch_shapes=[pltpu.VMEM((HB, HGRN_DK, HGRN_DK), F32), scr, scr, scr, scr, scr, scr, scr, scr, scr],
        compiler_params=_params(2),
    )(dx, z, z, z, z, o, states, lb_logits, out_norm)


def _log_sigmoid(x):
    return jnp.minimum(x, 0.0) - jnp.log(1.0 + jnp.exp(-jnp.abs(x)))


def _tri(n, upper):
    r = lax.broadcasted_iota(jnp.int32, (n, n), 0)
    c = lax.broadcasted_iota(jnp.int32, (n, n), 1)
    return ((r >= c) if upper else (c <= r)).astype(F32)


def fox_cum_fwd(kvf, fcol, b_row, name):
    T = kvf.shape[0]
    tb = _tile(T, 512)

    def body(x_ref, b_ref, ct_ref, carry_ref):
        @pl.when(pl.program_id(0) == 0)
        def _():
            carry_ref[...] = jnp.zeros_like(carry_ref)

        lf = _log_sigmoid(x_ref[...] + b_ref[...])
        cum = jnp.dot(_tri(tb, False), lf, precision=HI, preferred_element_type=F32) + carry_ref[...]
        carry_ref[...] += _colsum(lf)
        ct_ref[...] = cum.T

    return pl.pallas_call(
        body, name=name, grid=(T // tb,),
        in_specs=[pl.BlockSpec((tb, LANES), lambda i: (i, fcol)), pl.BlockSpec((1, LANES), lambda i: (0, 0))],
        out_specs=pl.BlockSpec((LANES, tb), lambda i: (0, i)), out_shape=jax.ShapeDtypeStruct((LANES, T), F32),
        scratch_shapes=[pltpu.VMEM((1, LANES), F32)], compiler_params=_params(1),
    )(kvf, b_row)


def fox_cum_bwd(dct, kvf, fcol, b_row, name):
    T = kvf.shape[0]
    tb = _tile(T, 512)
    nblk = T // tb

    def body(dc_ref, x_ref, b_ref, df_ref, db_ref, carry_ref):
        @pl.when(pl.program_id(0) == 0)
        def _():
            carry_ref[...] = jnp.zeros_like(carry_ref)
            db_ref[...] = jnp.zeros_like(db_ref)

        dc = dc_ref[...]
        dlf_t = jnp.dot(dc, _tri(tb, True), precision=HI, preferred_element_type=F32) + carry_ref[...]
        carry_ref[...] += jnp.sum(dc, axis=1, keepdims=True)
        x = x_ref[...] + b_ref[...]
        df = dlf_t.T * _sigmoid(-x)
        df_ref[...] = df.astype(BF)
        db_ref[...] += _colsum(df)

    return pl.pallas_call(
        body, name=name, grid=(nblk,),
        in_specs=[pl.BlockSpec((LANES, tb), lambda i: (0, nblk - 1 - i)),
                  pl.BlockSpec((tb, LANES), lambda i: (nblk - 1 - i, fcol)), pl.BlockSpec((1, LANES), lambda i: (0, 0))],
        out_specs=[pl.BlockSpec((tb, LANES), lambda i: (nblk - 1 - i, 0)), pl.BlockSpec((1, LANES), lambda i: (0, 0))],
        out_shape=[jax.ShapeDtypeStruct((T, LANES), BF), jax.ShapeDtypeStruct((1, LANES), F32)],
        scratch_shapes=[pltpu.VMEM((LANES, 1), F32)], compiler_params=_params(1),
    )(dct, kvf, b_row)


NAUG = 3


def fox_prep(kvf, ct, n_fox, tk):
    T = kvf.shape[0]
    W = n_fox * FOX_HEAD_DIM
    NP = n_fox // 2
    k = kvf[:, :W].astype(BF).reshape(T, NP, 2, FOX_HEAD_DIM)
    c = ct[:n_fox].T.reshape(T, NP, 2)
    hi = lax.reduce_precision(c, 8, 7)
    mid = lax.reduce_precision(c - hi, 8, 7)
    lo = c - hi - mid
    aug = jnp.stack([hi, mid, lo], axis=-1).astype(BF)
    pad = jnp.zeros((T, NP, FOX_HEAD_DIM - NAUG), BF)
    ka = jnp.concatenate([k[:, :, 0], aug[:, :, 0], pad], axis=-1).reshape(T, W)
    kb = jnp.concatenate([aug[:, :, 1], pad, k[:, :, 1]], axis=-1).reshape(T, W)
    v = kvf[:, W:2 * W].astype(BF)
    vt3 = v.reshape(T // tk, tk, NP, LANES).transpose(2, 0, 3, 1)
    return ka, kb, vt3


def fox_fwd(qg, ka, kb, vt3, name):
    T = qg.shape[0]
    W = qg.shape[1] // 2
    NP = W // LANES
    tq = tk = vt3.shape[3]
    scale = FOX_HEAD_DIM ** -0.5
    nk = T // tk
    HD = FOX_HEAD_DIM

    def body(q_ref, g_ref, ka_ref, kb_ref, vt_ref, x_ref, o_ref, lse_ref):
        i = pl.program_id(1)
        lane = lax.broadcasted_iota(jnp.int32, (tq, LANES), 1)
        q2 = q_ref[...] * scale
        qa = jnp.where(lane < HD, q2, jnp.where(lane < HD + NAUG, -1.0, 0.0))
        qb = jnp.where(lane >= HD, q2, jnp.where(lane < NAUG, -1.0, 0.0))
        qts = (qa.T.astype(BF), qb.T.astype(BF))
        krow = lax.broadcasted_iota(jnp.int32, (tk, tq), 0)
        qcol = lax.broadcasted_iota(jnp.int32, (tk, tq), 1)

        def step(j, carry, diag):
            rows = pl.ds(pl.multiple_of(j * tk, tk), tk)
            ks = (ka_ref[rows, :], kb_ref[rows, :])
            vt = vt_ref[j]
            sts = [jnp.dot(ks[a], qts[a], preferred_element_type=F32) for a in range(2)]
            pts, mls = [], []
            for a in range(2):
                m, l, _ = carry[a]
                st = sts[a]
                if diag:
                    st = jnp.where(krow + (j * tk - i * tq) <= qcol, st, -1e30)
                mn = jnp.maximum(m, jnp.max(st, axis=0, keepdims=True))
                alpha = jnp.exp(m - mn)
                pt = jnp.exp(st - mn)
                mls.append((mn, l * alpha + jnp.sum(pt, axis=0, keepdims=True), alpha))
                pts.append(pt.astype(BF))
            out = []
            for a in range(2):
                mn, l, alpha = mls[a]
                acc = carry[a][2] * alpha + jnp.dot(vt[a * HD:(a + 1) * HD, :], pts[a], preferred_element_type=F32)
                out.append((mn, l, acc))
            return tuple(out)

        init = (jnp.full((1, tq), -1e30, F32), jnp.zeros((1, tq), F32), jnp.zeros((HD, tq), F32))
        r = tq // tk
        carry = lax.fori_loop(0, i * r, lambda j, c: step(j, c, False), (init, init))
        for u in range(r):
            carry = step(i * r + u, carry, True)
        (ma, la, acca), (mb, lb, accb) = carry
        ot = jnp.concatenate([acca / la, accb / lb], axis=0)
        o = ot.T
        o_ref[...] = o
        lse_ref[0:1, :] = ma + jnp.log(la)
        lse_ref[1:2, :] = mb + jnp.log(lb)
        x_ref[...] = (o * _sigmoid(g_ref[...])).astype(BF)

    blk = pl.BlockSpec((tq, LANES), lambda hp, i: (i, hp))
    full = pl.BlockSpec((T, LANES), lambda hp, i: (0, hp))
    return pl.pallas_call(
        body, name=name, grid=(NP, T // tq),
        in_specs=[blk, pl.BlockSpec((tq, LANES), lambda hp, i: (i, NP + hp)), full, full,
                  pl.BlockSpec((None, nk, LANES, tk), lambda hp, i: (hp, 0, 0, 0))],
        out_specs=[blk, blk, pl.BlockSpec((None, None, 2, tq), lambda hp, i: (hp, i, 0, 0))],
        out_shape=[jax.ShapeDtypeStruct((T, W), BF), jax.ShapeDtypeStruct((T, W), F32),
                   jax.ShapeDtypeStruct((NP, T // tq, 2, tq), F32)],
        compiler_params=_params(2),
    )(qg, qg, ka, kb, vt3)


def fox_gate_bwd(dx, o, qg, name):
    T, W = o.shape
    tm = _tile(T, 512)

    def body(dx_ref, o_ref, g_ref, do_ref, dg_ref, ds_ref):
        dxv = dx_ref[...].astype(F32)
        sg = _sigmoid(g_ref[...])
        do = dxv * sg
        o = o_ref[...]
        do_ref[...] = do
        dg_ref[...] = (dxv * o * sg * (1.0 - sg)).astype(BF)
        head = jnp.right_shift(lax.broadcasted_iota(jnp.int32, (W, LANES), 0), FOX_HEAD_DIM.bit_length() - 1)
        sel = (head == lax.broadcasted_iota(jnp.int32, (W, LANES), 1)).astype(F32)
        ds_ref[...] = jnp.dot(do * o, sel, precision=HI, preferred_element_type=F32)

    tok = pl.BlockSpec((tm, W), lambda i: (i, 0))
    return pl.pallas_call(
        body, name=name, grid=(T // tm,),
        in_specs=[tok, tok, pl.BlockSpec((tm, W), lambda i: (i, 1))],
        out_specs=[tok, tok, pl.BlockSpec((tm, LANES), lambda i: (i, 0))],
        out_shape=[jax.ShapeDtypeStruct((T, W), F32), jax.ShapeDtypeStruct((T, W), BF), jax.ShapeDtypeStruct((T, LANES), F32)],
        compiler_params=_params(1),
    )(dx, o, qg)


def fox_bwd_prep(qg, kvf, do, tq):
    T = qg.shape[0]
    W = qg.shape[1] // 2
    NP = W // LANES
    scale = FOX_HEAD_DIM ** -0.5
    tr3 = lambda a: a.reshape(T // tq, tq, NP, LANES).transpose(2, 0, 3, 1)
    q = (qg[:, :W] * scale).astype(BF)
    dob = do.astype(BF)
    k = kvf[:, :W]
    return q, tr3(q), dob, tr3(dob), tr3((k * scale).astype(BF)), kvf[:, W:2 * W].astype(BF)


def fox_bwd(q, qt3, dob, dot3, kt3, v, ka, kb, lse4, dsum4, name):
    T, W = q.shape
    NP = W // LANES
    tq = tk = qt3.shape[3]
    nq = T // tq
    HD = FOX_HEAD_DIM

    def body(q_ref, qt_ref, do_ref, dot_ref, kt_ref, v_ref, ka_ref, kb_ref, lse_ref, dsum_ref,
             dqt_ref, dk_ref, dv_ref, dc_ref, dcq_ref, dk_scr, dv_scr, dcl_scr):
        j = pl.program_id(1)

        @pl.when(j == 0)
        def _():
            dqt_ref[...] = jnp.zeros_like(dqt_ref)
            dcq_ref[...] = jnp.zeros_like(dcq_ref)

        dk_scr[...] = jnp.zeros_like(dk_scr)
        dv_scr[...] = jnp.zeros_like(dv_scr)
        dcl_scr[...] = jnp.zeros_like(dcl_scr)
        lane = lax.broadcasted_iota(jnp.int32, (tk, LANES), 1)
        srow = lax.broadcasted_iota(jnp.int32, (LANES, tq), 0)
        lanes_of = (lane < HD, lane >= HD)
        rows_of = (srow < HD, srow >= HD)
        v2 = v_ref[...]
        kt2 = kt_ref[...]
        zero = jnp.zeros((), BF)
        vs = [jnp.where(lanes_of[a], v2, zero) for a in range(2)]
        kts = [jnp.where(rows_of[a], kt2, zero) for a in range(2)]
        kaug = (ka_ref[...], kb_ref[...])
        krow = lax.broadcasted_iota(jnp.int32, (tk, tq), 0)
        qcol = lax.broadcasted_iota(jnp.int32, (tk, tq), 1)
        neg1 = jnp.full((), -1.0, BF)

        def step(i, carry, diag):
            rows = pl.ds(pl.multiple_of(i * tq, tq), tq)
            qt2 = qt_ref[i]
            dot2 = dot_ref[i]
            q2 = q_ref[rows, :]
            do2 = do_ref[rows, :]
            qts = [jnp.where(srow < HD, qt2, jnp.where(srow < HD + NAUG, neg1, zero)),
                   jnp.where(srow >= HD, qt2, jnp.where(srow < NAUG, neg1, zero))]
            sts = [jnp.dot(kaug[a], qts[a], preferred_element_type=F32) for a in range(2)]
            dps = [jnp.dot(vs[a], dot2, preferred_element_type=F32) for a in range(2)]
            pbs, dsbs = [], []
            for a in range(2):
                pt = jnp.exp(sts[a] - lse_ref[i, a:a + 1, :])
                if diag:
                    pt = jnp.where(krow <= qcol, pt, 0.0)
                ds = pt * (dps[a] - dsum_ref[i, a:a + 1, :])
                dcq_ref[i, a:a + 1, :] += _colsum(ds)
                part = ds[:, 0:LANES]
                for u in range(1, tq // LANES):
                    part = part + ds[:, u * LANES:(u + 1) * LANES]
                dcl_scr[a] += part
                pbs.append(pt.astype(BF))
                dsbs.append(ds.astype(BF))
            qn = [jnp.where(lanes_of[a], q2, zero) for a in range(2)]
            don = [jnp.where(lanes_of[a], do2, zero) for a in range(2)]
            dv_scr[...] += (jnp.dot(pbs[0], don[0], preferred_element_type=F32) +
                            jnp.dot(pbs[1], don[1], preferred_element_type=F32))
            dk_scr[...] += (jnp.dot(dsbs[0], qn[0], preferred_element_type=F32) +
                            jnp.dot(dsbs[1], qn[1], preferred_element_type=F32))
            dqt_ref[i] += (jnp.dot(kts[0], dsbs[0], preferred_element_type=F32) +
                           jnp.dot(kts[1], dsbs[1], preferred_element_type=F32))
            return carry

        step(j, 0, True)
        lax.fori_loop(j + 1, nq, lambda i, c: step(i, c, False), 0)
        dk_ref[...] = dk_scr[...].astype(BF)
        dv_ref[...] = dv_scr[...].astype(BF)
        for a in range(2):
            dc_ref[a:a + 1, :] = -_colsum(dcl_scr[a].T)

    tile = pl.BlockSpec((tk, LANES), lambda hp, j: (j, hp))
    full = pl.BlockSpec((T, LANES), lambda hp, j: (0, hp))
    full3 = pl.BlockSpec((None, nq, LANES, tq), lambda hp, j: (hp, 0, 0, 0))
    rows4 = pl.BlockSpec((None, nq, 2, tq), lambda hp, j: (hp, 0, 0, 0))
    bft = jax.ShapeDtypeStruct((T, W), BF)
    r4 = jax.ShapeDtypeStruct((NP, nq, 2, tq), F32)
    return pl.pallas_call(
        body, name=name, grid=(NP, nq),
        in_specs=[full, full3, full, full3, pl.BlockSpec((None, None, LANES, tk), lambda hp, j: (hp, j, 0, 0)),
                  tile, tile, tile, rows4, rows4],
        out_specs=[full3, tile, tile, pl.BlockSpec((None, None, 2, tk), lambda hp, j: (hp, j, 0, 0)), rows4],
        out_shape=[jax.ShapeDtypeStruct((NP, nq, LANES, tq), F32), bft, bft, r4, r4],
        scratch_shapes=[pltpu.VMEM((tk, LANES), F32), pltpu.VMEM((tk, LANES), F32), pltpu.VMEM((2, tk, LANES), F32)],
        compiler_params=_params(2),
    )(q, qt3, dob, dot3, kt3, v, ka, kb, lse4, dsum4)


def loss_fwd_bwd(y, target, name):
    T, D = y.shape
    tm = _tile(T, 512)

    def body(y_ref, t_ref, dy_ref, l_ref):
        @pl.when(pl.program_id(0) == 0)
        def _():
            l_ref[...] = jnp.zeros_like(l_ref)

        d = y_ref[...] - t_ref[...]
        dy_ref[...] = d * (1.0 / D)
        l_ref[...] += 0.5 * jnp.sum(jnp.mean(d * d, axis=-1, keepdims=True), axis=0, keepdims=True)

    tok = pl.BlockSpec((tm, D), lambda i: (i, 0))
    return pl.pallas_call(
        body, name=name, grid=(T // tm,), in_specs=[tok, tok],
        out_specs=[tok, pl.BlockSpec((1, 1), lambda i: (0, 0))],
        out_shape=[jax.ShapeDtypeStruct((T, D), F32), jax.ShapeDtypeStruct((1, 1), F32)], compiler_params=_params(1),
    )(y, target)


def adamw(parts, w, m, v, layer, prev, name):
    L, R, C = w.shape
    tr = _tile(R, 256)
    c1 = 1.0 / (1.0 - ADAM_B1 ** ADAM_STEP)
    c2 = 1.0 / (1.0 - ADAM_B2 ** ADAM_STEP)

    def body(p_ref, w_ref, m_ref, v_ref, *rest):
        g_ref, d_ref, mo_ref, vo_ref = rest[-4:]
        g = p_ref[0].astype(F32)
        for d in range(1, N_DEV):
            g = g + p_ref[d].astype(F32)
        mn = ADAM_B1 * m_ref[...] + (1.0 - ADAM_B1) * g
        vn = ADAM_B2 * v_ref[...] + (1.0 - ADAM_B2) * (g * g)
        g_ref[...] = g
        mo_ref[...] = mn
        vo_ref[...] = vn
        d_ref[...] = -ADAM_LR * ((mn * c1) / (jnp.sqrt(vn * c2) + ADAM_EPS) + ADAM_WD * w_ref[...])

    blk = pl.BlockSpec((None, tr, C), lambda i: (layer, i, 0))
    shp = jax.ShapeDtypeStruct((L, R, C), F32)
    in_specs = [pl.BlockSpec((N_DEV, tr, C), lambda i: (0, i, 0)), blk, blk, blk]
    args = [parts, w, m, v]
    aliases = {}
    if prev is not None:
        in_specs += [pl.BlockSpec(memory_space=pl.ANY)] * 4
        args += list(prev)
        aliases = {4 + j: j for j in range(4)}
    return pl.pallas_call(
        body, name=name, grid=(R // tr,), in_specs=in_specs, out_specs=[blk, blk, blk, blk],
        out_shape=[shp, shp, shp, shp], input_output_aliases=aliases, compiler_params=_params(1),
    )(*args)


HBM_SPEC = pl.BlockSpec(memory_space=pltpu.HBM)
SEM_SPEC = pl.BlockSpec(memory_space=pltpu.SEMAPHORE)
EFFECT = pltpu.SideEffectType.DATAFLOW_SIDE_EFFECTING


def _mesh_pos():
    return lax.axis_index("x"), lax.axis_index("y"), lax.axis_index("c")


def _flip(v, bit):
    return v + bit - 2 * v * bit


def _peer(pos, delta):
    x, y, c = pos
    px, py, pc = _flip(x, (delta >> 2) & 1), _flip(y, (delta >> 1) & 1), _flip(c, delta & 1)
    return (px, py, pc), 4 * px + 2 * py + pc


def _me():
    x, y, c = _mesh_pos()
    return 4 * x + 2 * y + c


def _copies(src_refs, land_refs, whole, send, recv, incoming):
    pos = _mesh_pos()
    me = 4 * pos[0] + 2 * pos[1] + pos[2]
    out = []
    for k in range(len(src_refs)):
        for d in range(1, N_DEV):
            dev, idx = _peer(pos, d)
            j = k * (N_DEV - 1) + d - 1
            src = src_refs[k] if whole[k] else src_refs[k].at[idx]
            out.append(pltpu.make_async_remote_copy(
                src_ref=src, dst_ref=land_refs[k].at[idx if incoming else me], send_sem=send.at[j], recv_sem=recv.at[j],
                device_id=dev, device_id_type=pl.DeviceIdType.MESH))
    return out


def exchange_start(srcs, lands, whole, name):
    n = len(srcs)

    def body(*refs):
        for copy in _copies(refs[:n], refs[n:2 * n], whole, refs[2 * n], refs[2 * n + 1], False):
            copy.start()
        refs[-1][...] = jnp.zeros_like(refs[-1])

    sems = pltpu.SemaphoreType.DMA((n * (N_DEV - 1),))
    thru = [pltpu.HBM(a.shape, a.dtype) for a in list(srcs) + list(lands)]
    res = pl.pallas_call(
        body, name=name, in_specs=[HBM_SPEC] * (2 * n),
        out_specs=[SEM_SPEC, SEM_SPEC] + [HBM_SPEC] * (2 * n) + [pl.BlockSpec(memory_space=pltpu.VMEM)],
        out_shape=[sems, sems] + thru + [jax.ShapeDtypeStruct((8, LANES), F32)],
        input_output_aliases={j: 2 + j for j in range(2 * n)},
        compiler_params=pltpu.CompilerParams(has_side_effects=EFFECT),
    )(*[pltpu.with_memory_space_constraint(a, pltpu.HBM) for a in list(srcs) + list(lands)])
    return dict(send=res[0], recv=res[1], srcs=res[2:2 + n], lands=res[2 + n:2 + 2 * n], whole=whole, token=res[-1])


def exchange_wait(handle, after, name):
    n = len(handle["srcs"])
    whole = handle["whole"]

    def body(*refs):
        for copy in _copies(refs[:n], refs[n:2 * n], whole, refs[2 * n], refs[2 * n + 1], False):
            copy.wait_send()
        for copy in _copies(refs[:n], refs[n:2 * n], whole, refs[2 * n], refs[2 * n + 1], True):
            copy.wait_recv()

    bufs = list(handle["srcs"]) + list(handle["lands"])
    res = pl.pallas_call(
        body, name=name, in_specs=[HBM_SPEC] * (2 * n) + [SEM_SPEC, SEM_SPEC] + [pl.BlockSpec(memory_space=pl.ANY)] * len(after),
        out_specs=[HBM_SPEC] * (2 * n), out_shape=[pltpu.HBM(a.shape, a.dtype) for a in bufs],
        input_output_aliases={j: j for j in range(2 * n)},
        compiler_params=pltpu.CompilerParams(has_side_effects=EFFECT),
    )(*bufs, handle["send"], handle["recv"], *after)
    return list(res[n:])


def _landing(own, whole):
    me = _me()
    if not whole:
        own = lax.dynamic_index_in_dim(own, me, 0, keepdims=False)
    buf = lax.empty((N_DEV,) + own.shape, own.dtype)
    return lax.dynamic_update_slice(buf, own[None], (me,) + (0,) * own.ndim)


def _row(v):
    return v.reshape(1, -1)


def _pad_lanes(v, n):
    return jnp.pad(v, ((0, 0), (0, n - v.shape[1])))


TRANSPOSED = ("ffn1_w_in", "ffn2_w_in")

GATHER_GROUPS = (
    ("ffn1_in_0", (("ffn1_w_in", 0),)),
    ("ffn1_out_0", (("ffn1_w_out", 0),)),
    ("hgrn", (("hgrn_w_in", 0), ("hgrn_w_out", 0))),
    ("rest_0", (("ffn2_w_in", 0), ("ffn2_w_out", 0), ("ple_w_gate", 0), ("ple_w_proj", 0), ("fox_w_kvf", 0))),
    ("ffn1_1", (("ffn1_w_in", 1), ("ffn1_w_out", 1))),
    ("fox", (("fox_w_qg", 0), ("fox_w_out", 0))),
    ("rest_1", (("ffn2_w_in", 1), ("ffn2_w_out", 1), ("ple_w_gate", 1), ("ple_w_proj", 1))),
)


def kernel(x, p, ffn1_norm_pre, ffn1_w_in, ffn1_w_out, ffn1_norm_post, mix_norm_pre, mix_norm_post, ffn2_norm_pre, ffn2_w_in, ffn2_w_out, ffn2_norm_post, hgrn_w_in, hgrn_lb_logits, hgrn_out_norm, hgrn_w_out, kv_norm, fox_w_kvf, fox_b_f, fox_w_qg, fox_w_out, ple_norm_pre, ple_w_gate, ple_w_proj, ple_norm_post, loss_target, m_ffn1_norm_pre, m_ffn1_w_in, m_ffn1_w_out, m_ffn1_norm_post, m_mix_norm_pre, m_mix_norm_post, m_ffn2_norm_pre, m_ffn2_w_in, m_ffn2_w_out, m_ffn2_norm_post, m_hgrn_w_in, m_hgrn_lb_logits, m_hgrn_out_norm, m_hgrn_w_out, m_kv_norm, m_fox_w_kvf, m_fox_b_f, m_fox_w_qg, m_fox_w_out, m_ple_norm_pre, m_ple_w_gate, m_ple_w_proj, m_ple_norm_post, v_ffn1_norm_pre, v_ffn1_w_in, v_ffn1_w_out, v_ffn1_norm_post, v_mix_norm_pre, v_mix_norm_post, v_ffn2_norm_pre, v_ffn2_w_in, v_ffn2_w_out, v_ffn2_norm_post, v_hgrn_w_in, v_hgrn_lb_logits, v_hgrn_out_norm, v_hgrn_w_out, v_kv_norm, v_fox_w_kvf, v_fox_b_f, v_fox_w_qg, v_fox_w_out, v_ple_norm_pre, v_ple_w_gate, v_ple_w_proj, v_ple_norm_post):
    weights = dict(ffn1_norm_pre=ffn1_norm_pre, ffn1_w_in=ffn1_w_in, ffn1_w_out=ffn1_w_out, ffn1_norm_post=ffn1_norm_post, mix_norm_pre=mix_norm_pre, mix_norm_post=mix_norm_post, ffn2_norm_pre=ffn2_norm_pre, ffn2_w_in=ffn2_w_in, ffn2_w_out=ffn2_w_out, ffn2_norm_post=ffn2_norm_post, hgrn_w_in=hgrn_w_in, hgrn_lb_logits=hgrn_lb_logits, hgrn_out_norm=hgrn_out_norm, hgrn_w_out=hgrn_w_out, kv_norm=kv_norm, fox_w_kvf=fox_w_kvf, fox_b_f=fox_b_f, fox_w_qg=fox_w_qg, fox_w_out=fox_w_out, ple_norm_pre=ple_norm_pre, ple_w_gate=ple_w_gate, ple_w_proj=ple_w_proj, ple_norm_post=ple_norm_post)
    mom1 = dict(ffn1_norm_pre=m_ffn1_norm_pre, ffn1_w_in=m_ffn1_w_in, ffn1_w_out=m_ffn1_w_out, ffn1_norm_post=m_ffn1_norm_post, mix_norm_pre=m_mix_norm_pre, mix_norm_post=m_mix_norm_post, ffn2_norm_pre=m_ffn2_norm_pre, ffn2_w_in=m_ffn2_w_in, ffn2_w_out=m_ffn2_w_out, ffn2_norm_post=m_ffn2_norm_post, hgrn_w_in=m_hgrn_w_in, hgrn_lb_logits=m_hgrn_lb_logits, hgrn_out_norm=m_hgrn_out_norm, hgrn_w_out=m_hgrn_w_out, kv_norm=m_kv_norm, fox_w_kvf=m_fox_w_kvf, fox_b_f=m_fox_b_f, fox_w_qg=m_fox_w_qg, fox_w_out=m_fox_w_out, ple_norm_pre=m_ple_norm_pre, ple_w_gate=m_ple_w_gate, ple_w_proj=m_ple_w_proj, ple_norm_post=m_ple_norm_post)
    mom2 = dict(ffn1_norm_pre=v_ffn1_norm_pre, ffn1_w_in=v_ffn1_w_in, ffn1_w_out=v_ffn1_w_out, ffn1_norm_post=v_ffn1_norm_post, mix_norm_pre=v_mix_norm_pre, mix_norm_post=v_mix_norm_post, ffn2_norm_pre=v_ffn2_norm_pre, ffn2_w_in=v_ffn2_w_in, ffn2_w_out=v_ffn2_w_out, ffn2_norm_post=v_ffn2_norm_post, hgrn_w_in=v_hgrn_w_in, hgrn_lb_logits=v_hgrn_lb_logits, hgrn_out_norm=v_hgrn_out_norm, hgrn_w_out=v_hgrn_w_out, kv_norm=v_kv_norm, fox_w_kvf=v_fox_w_kvf, fox_b_f=v_fox_b_f, fox_w_qg=v_fox_w_qg, fox_w_out=v_fox_w_out, ple_norm_pre=v_ple_norm_pre, ple_w_gate=v_ple_w_gate, ple_w_proj=v_ple_w_proj, ple_norm_post=v_ple_norm_post)
    names = list(weights)
    big = ["ffn1_w_in", "ffn1_w_out", "ffn2_w_in", "ffn2_w_out", "hgrn_w_in", "hgrn_w_out", "fox_w_kvf", "fox_w_qg",
           "fox_w_out", "ple_w_gate", "ple_w_proj"]
    small_names = [n for n in names if n not in big]

    T, D = x.shape[1], x.shape[2]
    depth = p.shape[0]
    h0 = x.reshape(T, D)
    target = loss_target.reshape(T, D)
    p3 = p.reshape(depth, T, p.shape[3])
    n_fox = fox_b_f.shape[0]
    fox_w = n_fox * FOX_HEAD_DIM
    fcol = 2 * fox_w // LANES
    b_row = _pad_lanes(_row(fox_b_f), LANES)

    tok = jnp.zeros((), F32)
    handles = {}
    for gname, keys in GATHER_GROUPS:
        shards = []
        for n, l in keys:
            w = weights[n]
            w = w[l] if w.ndim == 3 else w
            shards.append(((w.T if n in TRANSPOSED else w) + tok).astype(BF))
        handles[gname] = exchange_start(shards, [_landing(s, True) for s in shards], [True] * len(keys), f"gather_start_{gname}")
        tok = handles[gname]["token"][0, 0]
    W = {}

    def arrive(gname, after):
        lands = exchange_wait(handles[gname], after, f"gather_wait_{gname}")
        W.update(dict(zip(dict(GATHER_GROUPS)[gname], lands)))

    def w_rows(n, l):
        return W[n, l].reshape(-1, D)

    norm = lambda name, i: weights[name][i:i + 1]

    saved = []
    h = h0
    kvf = ct = kvf_w = None
    tq = _tile(T, 512)
    for i in range(depth):
        s = {}
        if i == 0:
            arrive("ffn1_in_0", [handles[GATHER_GROUPS[-1][0]]["token"]])
        for k in (1, 2):
            if k == 2:
                s["h_a"] = h
                if i == 0:
                    arrive("hgrn", [h])
                    z, xn = norm_mm(h, norm("mix_norm_pre", i), W["hgrn_w_in", 0], "hgrn_in")
                    xm, o, states = hgrn_fwd(z, hgrn_lb_logits, hgrn_out_norm, "hgrn_scan")
                    s.update(z=z, o=o, states=states)
                    wmix = w_rows("hgrn_w_out", 0)
                else:
                    arrive("fox", [h])
                    qg, xn = norm_mm(h, norm("mix_norm_pre", i), W["fox_w_qg", 0], "fox_qg")
                    tqf = _tile(T, 1024)
                    ka, kb, vt3 = fox_prep(kvf, ct, n_fox, tqf)
                    xm, o, lse = fox_fwd(qg, ka, kb, vt3, "fox_attn")
                    lse = lse.reshape(-1, T // tqf, 2, tqf // tq, tq).transpose(0, 1, 3, 2, 4).reshape(-1, T // tq, 2, tq)
                    s.update(qg=qg, o=o, lse=lse, ka=ka, kb=kb)
                    wmix = w_rows("fox_w_out", 0)
                h, ym = mm_norm_res(xm, wmix, norm("mix_norm_post", i), h, 1.0, f"mix_out_{i}", kb=_tile(xm.shape[1], 512))
                s.update(xm=xm, ym=ym, xn_mix=xn)
                arrive(f"rest_{i}", [h])
            gate, up, a, xn = norm_mm_swiglu(h, norm(f"ffn{k}_norm_pre", i), W[f"ffn{k}_w_in", i], f"ffn{k}_in_{i}")
            if (i, k) == (0, 1):
                arrive("ffn1_out_0", [a])
            hn, y = mm_norm_res(a, w_rows(f"ffn{k}_w_out", i), norm(f"ffn{k}_norm_post", i), h, 0.5, f"ffn{k}_out_{i}")
            s[f"ffn{k}"] = (h, gate, up, a, y, xn)
            h = hn
        s["h_c"] = h
        ple_proj = W["ple_w_proj", i].transpose(1, 0, 2).reshape(p.shape[3], D)
        h, pgate, pp = ple_fwd(h, norm("ple_norm_pre", i), w_rows("ple_w_gate", i), p3, ple_proj, norm("ple_norm_post", i),
                               i, f"ple_{i}")
        s.update(pgate=pgate, pp=pp)
        saved.append(s)
        if i == 0:
            kvf_nat = W["fox_w_kvf", 0].transpose(1, 0, 2).reshape(D, -1)
            kvf_cols = kvf_nat.shape[1]
            kvf_w = _pad_lanes(kvf_nat, 2 * fox_w + LANES)
            kvf, xn_kv = norm_mm(h, _row(kv_norm), kvf_w, "fox_kvf", tn=kvf_w.shape[1])
            ct = fox_cum_fwd(kvf, fcol, b_row, "fox_cum")
            h_kv = h
            arrive("ffn1_1", [h])

    dh, loss_part = loss_fwd_bwd(h, target, "loss")
    loss = lax.psum(loss_part[0, 0], ("x", "y", "c"))

    gsmall = {n: [None] * weights[n].shape[0] if weights[n].ndim == 2 else None for n in small_names}
    sent = []

    def send(gname, keys, srcs, whole):
        lands = [_landing(a, w) for a, w in zip(srcs, whole)]
        hd = exchange_start(srcs, lands, whole, f"scatter_start_{gname}")
        sent.append((gname, keys, hd))
        return hd["token"][0:1, 0:1]

    def send_grads(gname, grads):
        return send(gname, list(grads), list(grads.values()), [False] * len(grads))

    for i in reversed(range(depth)):
        s = saved[i]
        grads = {}
        dh, du, dpp, xn, dgpre, dgpost = ple_bwd(dh, s["h_c"], s["pgate"], s["pp"], norm("ple_norm_pre", i),
                                                 w_rows("ple_w_gate", i), norm("ple_norm_post", i), f"ple_bwd_{i}")
        gsmall["ple_norm_pre"][i], gsmall["ple_norm_post"][i] = dgpre, dgpost
        grads["ple_w_gate", i] = mm_tn(xn, du, f"ple_dgate_{i}", xb=_tile(D, 512)).reshape(N_DEV, -1, D)
        dproj = mm_tn(p3, dpp, f"ple_dproj_{i}", x_layer=i, yb=D)[0]
        grads["ple_w_proj", i] = dproj.reshape(dproj.shape[0], N_DEV, -1).transpose(1, 0, 2)
        for k in (2, 1):
            hin, gate, up, a, y, xn = s[f"ffn{k}"]
            ffn_cs = gate.shape[2]
            dy, dgpost, dz = nbwd_mm_nt(dh, y, norm(f"ffn{k}_norm_post", i), w_rows(f"ffn{k}_w_out", i), 0.5, ffn_cs,
                                        f"ffn{k}_bwd_out_{i}", gate=gate, up=up)
            dz = dz.reshape(-1, T, ffn_cs)
            grads[f"ffn{k}_w_out", i] = mm_tn(a, dy, f"ffn{k}_dwout_{i}").reshape(N_DEV, -1, D)
            grads[f"ffn{k}_w_in", i] = mm_tn(dz, xn, f"ffn{k}_dwin_{i}").reshape(N_DEV, ffn_cs, D)
            tokv = send_grads(f"ffn{k}_{i}", grads)
            grads = {}
            dh, dgpre = mm_nt_nbwd(dz, W[f"ffn{k}_w_in", i], hin, norm(f"ffn{k}_norm_pre", i) + tokv, dh, f"ffn{k}_bwd_in_{i}",
                                   transposed=True)
            gsmall[f"ffn{k}_norm_pre"][i], gsmall[f"ffn{k}_norm_post"][i] = dgpre, dgpost
            if k == 2:
                nm = "hgrn" if i == 0 else "fox"
                wmix = w_rows(f"{nm}_w_out", 0)
                dy, dgpost, dxm = nbwd_mm_nt(dh, s["ym"], norm("mix_norm_post", i), wmix, 1.0, _tile(wmix.shape[0], 512),
                                             f"{nm}_bwd_out")
                grads[f"{nm}_w_out", 0] = mm_tn(s["xm"], dy, f"{nm}_dwout", xb=_tile(wmix.shape[0], 512)).reshape(N_DEV, -1, D)
                gsmall["mix_norm_post"][i] = dgpost
                if i == 0:
                    dq, df, dv, dg, dlog, don = hgrn_bwd(dxm, s["z"], s["o"], s["states"], hgrn_lb_logits, hgrn_out_norm,
                                                         "hgrn_scan_bwd")
                    gsmall["hgrn_lb_logits"] = [dlog[0:1], dlog[1:2]]
                    gsmall["hgrn_out_norm"] = [don]
                    dzm = jnp.concatenate([dq, df, dv, dg], axis=1)
                    nmin = "hgrn_w_in"
                else:
                    do, dg, dsum = fox_gate_bwd(dxm, s["o"], s["qg"], "fox_gate_bwd")
                    dsum4 = dsum[:, :n_fox].T.reshape(n_fox // 2, 2, T // tq, tq).transpose(0, 2, 1, 3)
                    dqt, dk_sh, dv_sh, dc4, dcq4 = fox_bwd(*fox_bwd_prep(s["qg"], kvf, do, tq), s["ka"], s["kb"], s["lse"], dsum4,
                                                           "fox_attn_bwd")
                    dq = dqt.transpose(1, 3, 0, 2).reshape(T, fox_w)
                    dzm = jnp.concatenate([dq.astype(BF), dg], axis=1)
                    nmin = "fox_w_qg"
                wmin = W[nmin, 0]
                grads[nmin, 0] = mm_tn(s["xn_mix"], dzm, f"{nm}_dwin", yb=wmin.shape[2])
                tokv = send_grads(f"mix_{i}", grads)
                grads = {}
                dh, dgpre = mm_nt_nbwd(dzm, wmin, s["h_a"], norm("mix_norm_pre", i) + tokv, dh, f"{nm}_bwd_in", tn=wmin.shape[2])
                gsmall["mix_norm_pre"][i] = dgpre
        if i == 1:
            dct = (dc4 + dcq4).transpose(0, 2, 1, 3).reshape(n_fox, T)
            dct = jnp.pad(dct, ((0, LANES - n_fox), (0, 0)))
            dflog, db = fox_cum_bwd(dct, kvf, fcol, b_row, "fox_cum_bwd")
            gsmall["fox_b_f"] = db[:, :n_fox]
            dkvf = jnp.concatenate([dk_sh, dv_sh, dflog], axis=1)
            dwk = mm_tn(xn_kv, dkvf, "fox_dwkvf", yb=kvf_w.shape[1])[0][:, :kvf_cols]
            tokv = send_grads("kvf", {("fox_w_kvf", 0): dwk.reshape(D, N_DEV, -1).transpose(1, 0, 2)})
            dh, dgkv = mm_nt_nbwd(dkvf, kvf_w, h_kv, _row(kv_norm) + tokv, dh, "fox_kvf_bwd", tn=kvf_w.shape[1])
            gsmall["kv_norm"] = dgkv
    grad_x = dh.reshape(x.shape)

    def small_rows(n):
        g = gsmall[n]
        rows = g if isinstance(g, list) else [g]
        return [_pad_lanes(r, D) for r in rows]

    counts = {n: len(small_rows(n)) for n in small_names}
    packed = jnp.concatenate([r for n in small_names for r in small_rows(n)], axis=0)
    n_rows = packed.shape[0]
    packed = jnp.pad(packed, ((0, -n_rows % 8), (0, 0)))
    send("small", ["small"], [packed], [True])

    res = {}
    after = [dh]
    small_parts = None
    for gname, keys, hd in sent:
        lands = exchange_wait(hd, after, f"scatter_wait_{gname}")
        after = []
        for key, parts in zip(keys, lands):
            if key == "small":
                small_parts = parts
                continue
            n, l = key
            w = weights[n]
            if w.ndim == 2:
                as3 = lambda a: a.reshape((1,) + a.shape)
            elif n in TRANSPOSED:
                as3 = lambda a: a.transpose(0, 2, 1)
            else:
                as3 = lambda a: a
            res[n] = adamw(parts, as3(w), as3(mom1[n]), as3(mom2[n]), l, res.get(n), f"adamw_{n}_{l}")
            after.append(res[n][1])
    for n in TRANSPOSED:
        res[n] = [a.transpose(0, 2, 1) for a in res[n]]

    def pack(d):
        rows = []
        for n in small_names:
            a = d[n]
            rows.append(_pad_lanes(a.reshape(-1, a.shape[-1]), D))
        a = jnp.concatenate(rows, axis=0)
        return jnp.pad(a, ((0, -n_rows % 8), (0, 0)))[None]

    sm = adamw(small_parts, pack(weights), pack(mom1), pack(mom2), 0, None, "adamw_small")
    off = 0
    for n in small_names:
        w = weights[n]
        res[n] = [a[0, off:off + counts[n], :w.shape[-1]].reshape(w.shape) for a in sm]
        off += counts[n]

    out = [loss, grad_x]
    for j in range(4):
        out += [res[n][j].reshape(weights[n].shape) for n in names]
    return tuple(out)
```

```python
import functools

import jax
import jax.numpy as jnp
from jax import lax
from jax.experimental import pallas as pl
from jax.experimental.pallas import tpu as pltpu

F32 = jnp.float32
BF = jnp.bfloat16
NORM_EPS = 1e-6
N_DEV = 8
HGRN_DK = 128
HGRN_CHUNK = 16
HGRN_HEADS_PER_STEP = 8
HALF = HGRN_CHUNK // 2
FOX_HEAD_DIM = 64
LANES = 128
ADAM_LR, ADAM_B1, ADAM_B2, ADAM_EPS, ADAM_WD, ADAM_STEP = 0.001, 0.9, 0.999, 1e-08, 0.01, 10
VMEM_LIMIT = 56 * 1024 * 1024
HI = lax.Precision.HIGHEST
NT = (((1,), (1,)), ((), ()))
TN = (((0,), (0,)), ((), ()))


def _params(n_axes):
    return pltpu.CompilerParams(dimension_semantics=("arbitrary",) * n_axes, vmem_limit_bytes=VMEM_LIMIT)


def _tile(n, want):
    t = min(n, want)
    while n % t:
        t //= 2
    return t


def _sigmoid(x):
    return 1.0 / (1.0 + jnp.exp(-x))


def _rms(x):
    r = lax.rsqrt(jnp.mean(x * x, axis=-1, keepdims=True) + NORM_EPS)
    return x * r, r


def _norm_bwd(dy, xhat, r, g):
    dxh = dy * g
    return r * (dxh - xhat * jnp.mean(dxh * xhat, axis=-1, keepdims=True))


def _colsum(x):
    return jnp.sum(x, axis=0, keepdims=True)


def _w_spec(w, blk):
    if w.ndim == 3:
        return lambda off: pl.BlockSpec((None, w.shape[1], w.shape[2]), lambda i, j: (j + off, 0, 0))
    return lambda off: pl.BlockSpec((w.shape[0], blk), lambda i, j: (0, j + off))


def norm_mm_swiglu(h, g, w3, name):
    T, D = h.shape
    nb, cs, _ = w3.shape
    nh = nb // 2
    tm = _tile(T, 1024)

    def body(h_ref, g_ref, wg_ref, wu_ref, gate_ref, up_ref, a_ref, xn_ref):
        @pl.when(pl.program_id(1) == 0)
        def _():
            xh, _ = _rms(h_ref[...])
            xn_ref[...] = (xh * g_ref[...]).astype(BF)

        xn = xn_ref[...]
        gt = lax.dot_general(xn, wg_ref[...], NT, preferred_element_type=F32)
        up = lax.dot_general(xn, wu_ref[...], NT, preferred_element_type=F32)
        sg = _sigmoid(gt)
        silu = gt * sg
        gate_ref[...] = (up * (sg * (1.0 + gt * (1.0 - sg)))).astype(BF)
        up_ref[...] = silu.astype(BF)
        a_ref[...] = (silu * up).astype(BF)

    ws = _w_spec(w3, cs)
    blk = pl.BlockSpec((None, tm, cs), lambda i, j: (j, i, 0))
    shp = jax.ShapeDtypeStruct((nh, T, cs), BF)
    return pl.pallas_call(
        body, name=name, grid=(T // tm, nh),
        in_specs=[pl.BlockSpec((tm, D), lambda i, j: (i, 0)), pl.BlockSpec((1, D), lambda i, j: (0, 0)), ws(0), ws(nh)],
        out_specs=[blk, blk, blk, pl.BlockSpec((tm, D), lambda i, j: (i, 0))],
        out_shape=[shp, shp, shp, jax.ShapeDtypeStruct((T, D), BF)], compiler_params=_params(2),
    )(h, g, w3, w3)


def norm_mm(h, g, w, name, tn=None):
    T, D = h.shape
    if w.ndim == 3:
        nb, cs = w.shape[0], w.shape[2]
    else:
        cs = tn
        nb = w.shape[1] // cs
    tm = _tile(T, 1024)

    def body(h_ref, g_ref, w_ref, z_ref, xn_ref):
        @pl.when(pl.program_id(1) == 0)
        def _():
            xh, _ = _rms(h_ref[...])
            xn_ref[...] = (xh * g_ref[...]).astype(BF)

        z_ref[...] = jnp.dot(xn_ref[...], w_ref[...], preferred_element_type=F32)

    return pl.pallas_call(
        body, name=name, grid=(T // tm, nb),
        in_specs=[pl.BlockSpec((tm, D), lambda i, j: (i, 0)), pl.BlockSpec((1, D), lambda i, j: (0, 0)),
                  _w_spec(w, cs)(0)],
        out_specs=[pl.BlockSpec((tm, cs), lambda i, j: (i, j)), pl.BlockSpec((tm, D), lambda i, j: (i, 0))],
        out_shape=[jax.ShapeDtypeStruct((T, nb * cs), F32), jax.ShapeDtypeStruct((T, D), BF)], compiler_params=_params(2),
    )(h, g, w)


def _x_spec(x, tm, kb):
    if x.ndim == 3:
        return pl.BlockSpec((None, tm, x.shape[2]), lambda i, j: (j, i, 0))
    return pl.BlockSpec((tm, kb), lambda i, j: (i, j))


def mm_norm_res(x, w2, g, h, coef, name, kb=None):
    T, D = h.shape
    if x.ndim == 3:
        nb, kb = x.shape[0], x.shape[2]
    else:
        nb = x.shape[1] // kb
    tm = _tile(T, 1024)

    def body(x_ref, w_ref, h_ref, g_ref, hn_ref, y_ref, acc_ref):
        b = pl.program_id(1)

        @pl.when(b == 0)
        def _():
            acc_ref[...] = jnp.zeros_like(acc_ref)

        acc_ref[...] += jnp.dot(x_ref[...], w_ref[...], preferred_element_type=F32)

        @pl.when(b == nb - 1)
        def _():
            y = acc_ref[...]
            y_ref[...] = y
            yh, _ = _rms(y)
            hn_ref[...] = h_ref[...] + coef * (yh * g_ref[...])

    tok = pl.BlockSpec((tm, D), lambda i, j: (i, 0))
    shp = jax.ShapeDtypeStruct((T, D), F32)
    return pl.pallas_call(
        body, name=name, grid=(T // tm, nb),
        in_specs=[_x_spec(x, tm, kb), pl.BlockSpec((kb, D), lambda i, j: (j, 0)), tok,
                  pl.BlockSpec((1, D), lambda i, j: (0, 0))],
        out_specs=[tok, tok], out_shape=[shp, shp],
        scratch_shapes=[pltpu.VMEM((tm, D), F32)], compiler_params=_params(2),
    )(x, w2, h, g)


def nbwd_mm_nt(dout, y, g, w2, coef, kb, name, gate=None, up=None):
    T, D = dout.shape
    nb = w2.shape[0] // kb
    swiglu = gate is not None
    tm = _tile(T, 1024 if swiglu else 512)

    def body(*refs):
        if swiglu:
            dout_ref, y_ref, g_ref, w_ref, gate_ref, up_ref, dy_ref, dg_ref, da_ref, dys_ref = refs
        else:
            dout_ref, y_ref, g_ref, w_ref, dy_ref, dg_ref, da_ref, dys_ref = refs
        i, b = pl.program_id(0), pl.program_id(1)

        @pl.when((i == 0) & (b == 0))
        def _():
            dg_ref[...] = jnp.zeros_like(dg_ref)

        @pl.when(b == 0)
        def _():
            yh, r = _rms(y_ref[...])
            dyn = coef * dout_ref[...]
            dg_ref[...] += _colsum(dyn * yh)
            dy = _norm_bwd(dyn, yh, r, g_ref[...]).astype(BF)
            dys_ref[...] = dy
            dy_ref[...] = dy

        da = lax.dot_general(dys_ref[...], w_ref[...], NT, preferred_element_type=F32)
        if swiglu:
            da_ref[0] = (da * gate_ref[...].astype(F32)).astype(BF)
            da_ref[1] = (da * up_ref[...].astype(F32)).astype(BF)
        else:
            da_ref[...] = da.astype(BF)

    tok = pl.BlockSpec((tm, D), lambda i, j: (i, 0))
    vec = pl.BlockSpec((1, D), lambda i, j: (0, 0))
    in_specs = [tok, tok, vec, pl.BlockSpec((kb, D), lambda i, j: (j, 0))]
    args = [dout, y, g, w2]
    if swiglu:
        blk = pl.BlockSpec((None, tm, kb), lambda i, j: (j, i, 0))
        in_specs += [blk, blk]
        args += [gate, up]
        da_spec = pl.BlockSpec((2, None, tm, kb), lambda i, j: (0, j, i, 0))
        da_shape = jax.ShapeDtypeStruct((2, nb, T, kb), BF)
    else:
        da_spec = pl.BlockSpec((tm, kb), lambda i, j: (i, j))
        da_shape = jax.ShapeDtypeStruct((T, nb * kb), BF)
    return pl.pallas_call(
        body, name=name, grid=(T // tm, nb), in_specs=in_specs,
        out_specs=[tok, vec, da_spec],
        out_shape=[jax.ShapeDtypeStruct((T, D), BF), jax.ShapeDtypeStruct((1, D), F32), da_shape],
        scratch_shapes=[pltpu.VMEM((tm, D), BF)], compiler_params=_params(2),
    )(*args)


def mm_nt_nbwd(dz, w, h, g, dout, name, tn=None, transposed=False):
    T, D = h.shape
    if w.ndim == 3:
        nb, cs = w.shape[0], w.shape[1 if transposed else 2]
    else:
        cs = tn
        nb = w.shape[1] // cs
    tm = _tile(T, 1024)

    def body(dz_ref, w_ref, h_ref, g_ref, dout_ref, dh_ref, dg_ref, acc_ref):
        i, b = pl.program_id(0), pl.program_id(1)

        @pl.when((i == 0) & (b == 0))
        def _():
            dg_ref[...] = jnp.zeros_like(dg_ref)

        @pl.when(b == 0)
        def _():
            acc_ref[...] = jnp.zeros_like(acc_ref)

        if transposed:
            acc_ref[...] += jnp.dot(dz_ref[...], w_ref[...], preferred_element_type=F32)
        else:
            acc_ref[...] += lax.dot_general(dz_ref[...], w_ref[...], NT, preferred_element_type=F32)

        @pl.when(b == nb - 1)
        def _():
            xh, r = _rms(h_ref[...])
            gg = g_ref[...]
            dxn = acc_ref[...]
            dg_ref[...] += _colsum(dxn * xh)
            dh_ref[...] = dout_ref[...] + _norm_bwd(dxn, xh, r, gg)

    tok = pl.BlockSpec((tm, D), lambda i, j: (i, 0))
    vec = pl.BlockSpec((1, D), lambda i, j: (0, 0))
    return pl.pallas_call(
        body, name=name, grid=(T // tm, nb),
        in_specs=[_x_spec(dz, tm, cs), _w_spec(w, cs)(0), tok, vec, tok],
        out_specs=[tok, vec],
        out_shape=[jax.ShapeDtypeStruct((T, D), F32), jax.ShapeDtypeStruct((1, D), F32)],
        scratch_shapes=[pltpu.VMEM((tm, D), F32)], compiler_params=_params(2),
    )(dz, w, h, g, dout)


def mm_tn(x, y, name, xb=None, yb=None, x_layer=None):
    T = y.shape[-2]
    wide = (yb if yb is not None else y.shape[-1]) > 1024
    tt = _tile(T, 1024 if wide else 2048)
    x_split = (x.ndim == 3 and x_layer is None) or xb is not None
    if x_layer is not None:
        xs = pl.BlockSpec((None, tt, x.shape[2]), lambda b, t: (x_layer, t, 0))
        kdim = x.shape[2]
    elif x.ndim == 3:
        xs = pl.BlockSpec((None, tt, x.shape[2]), lambda b, t: (b, t, 0))
        nb, kdim = x.shape[0], x.shape[2]
    elif xb is not None:
        xs = pl.BlockSpec((tt, xb), lambda b, t: (t, b))
        nb, kdim = x.shape[1] // xb, xb
    else:
        xs = pl.BlockSpec((tt, x.shape[1]), lambda b, t: (t, 0))
        kdim = x.shape[1]
    if x_split:
        ys = pl.BlockSpec((tt, y.shape[1]), lambda b, t: (t, 0))
        ndim = y.shape[1]
        out_spec = pl.BlockSpec((kdim, ndim), lambda b, t: (b, 0))
        out_shape = jax.ShapeDtypeStruct((nb * kdim, ndim), BF)
    else:
        if y.ndim == 3:
            ys = pl.BlockSpec((None, tt, y.shape[2]), lambda b, t: (b, t, 0))
            nb, ndim = y.shape[0], y.shape[2]
        else:
            ys = pl.BlockSpec((tt, yb), lambda b, t: (t, b))
            nb, ndim = y.shape[1] // yb, yb
        out_spec = pl.BlockSpec((None, kdim, ndim), lambda b, t: (b, 0, 0))
        out_shape = jax.ShapeDtypeStruct((nb, kdim, ndim), BF)
    nt = T // tt

    def body(x_ref, y_ref, o_ref, acc_ref):
        t = pl.program_id(1)

        @pl.when(t == 0)
        def _():
            acc_ref[...] = jnp.zeros_like(acc_ref)

        acc_ref[...] += lax.dot_general(x_ref[...].astype(BF), y_ref[...].astype(BF), TN, preferred_element_type=F32)

        @pl.when(t == nt - 1)
        def _():
            o_ref[...] = acc_ref[...].astype(BF)

    return pl.pallas_call(
        body, name=name, grid=(nb, nt), in_specs=[xs, ys], out_specs=out_spec, out_shape=out_shape,
        scratch_shapes=[pltpu.VMEM((kdim, ndim), F32)], compiler_params=_params(2),
    )(x, y)


def ple_fwd(h, gpre, wg, p3, wp, gpost, layer, name):
    T, D = h.shape
    pd = p3.shape[2]
    tm = _tile(T, 512)

    def body(h_ref, gpre_ref, wg_ref, p_ref, wp_ref, gpost_ref, hn_ref, gate_ref, pp_ref):
        x = h_ref[...]
        xh, _ = _rms(x)
        u = jnp.dot((xh * gpre_ref[...]).astype(BF), wg_ref[...], preferred_element_type=F32)
        gate = _sigmoid(u)
        pp = jnp.dot(p_ref[...].astype(BF), wp_ref[...], preferred_element_type=F32)
        yh, _ = _rms(gate * pp)
        hn_ref[...] = x + yh * gpost_ref[...]
        gate_ref[...] = gate.astype(BF)
        pp_ref[...] = pp.astype(BF)

    tok = pl.BlockSpec((tm, D), lambda i: (i, 0))
    vec = pl.BlockSpec((1, D), lambda i: (0, 0))
    return pl.pallas_call(
        body, name=name, grid=(T // tm,),
        in_specs=[tok, vec, pl.BlockSpec((D, D), lambda i: (0, 0)),
                  pl.BlockSpec((None, tm, pd), lambda i: (layer, i, 0)),
                  pl.BlockSpec((pd, D), lambda i: (0, 0)), vec],
        out_specs=[tok, tok, tok],
        out_shape=[jax.ShapeDtypeStruct((T, D), F32), jax.ShapeDtypeStruct((T, D), BF), jax.ShapeDtypeStruct((T, D), BF)],
        compiler_params=_params(1),
    )(h, gpre, wg, p3, wp, gpost)


def ple_bwd(dout, h, gate, pp, gpre, wg, gpost, name):
    T, D = h.shape
    tm = _tile(T, 512)

    def body(dout_ref, h_ref, gate_ref, pp_ref, gpre_ref, wg_ref, gpost_ref, dh_ref, du_ref, dpp_ref, xn_ref, dgpre_ref, dgpost_ref):
        @pl.when(pl.program_id(0) == 0)
        def _():
            dgpre_ref[...] = jnp.zeros_like(dgpre_ref)
            dgpost_ref[...] = jnp.zeros_like(dgpost_ref)

        dout = dout_ref[...]
        gate = gate_ref[...].astype(F32)
        pp = pp_ref[...].astype(F32)
        yh, ry = _rms(gate * pp)
        dgpost_ref[...] += _colsum(dout * yh)
        dy = _norm_bwd(dout, yh, ry, gpost_ref[...])
        dpp_ref[...] = (dy * gate).astype(BF)
        du = (dy * pp * gate * (1.0 - gate)).astype(BF)
        du_ref[...] = du
        dxn = lax.dot_general(du, wg_ref[...], NT, preferred_element_type=F32)
        xh, r = _rms(h_ref[...])
        gp = gpre_ref[...]
        dgpre_ref[...] += _colsum(dxn * xh)
        dh_ref[...] = dout + _norm_bwd(dxn, xh, r, gp)
        xn_ref[...] = (xh * gp).astype(BF)

    tok = pl.BlockSpec((tm, D), lambda i: (i, 0))
    vec = pl.BlockSpec((1, D), lambda i: (0, 0))
    bft = jax.ShapeDtypeStruct((T, D), BF)
    v32 = jax.ShapeDtypeStruct((1, D), F32)
    return pl.pallas_call(
        body, name=name, grid=(T // tm,),
        in_specs=[tok, tok, tok, tok, vec, pl.BlockSpec((D, D), lambda i: (0, 0)), vec],
        out_specs=[tok, tok, tok, tok, vec, vec],
        out_shape=[jax.ShapeDtypeStruct((T, D), F32), bft, bft, bft, v32, v32],
        compiler_params=_params(1),
    )(dout, h, gate, pp, gpre, wg, gpost)


def _chunk_tri(tb, upper):
    r = lax.broadcasted_iota(jnp.int32, (tb, tb), 0)
    c = lax.broadcasted_iota(jnp.int32, (tb, tb), 1)
    shift = HGRN_CHUNK.bit_length() - 1
    same = jnp.right_shift(r, shift) == jnp.right_shift(c, shift)
    return (same & ((c >= r) if upper else (c <= r))).astype(F32)


def _hgrn_gates(z, logits):
    lb = 1.0 / (1.0 + jnp.exp(logits[1:2, :] - logits[0:1, :]))
    e = jnp.exp(-jnp.abs(z))
    inv = 1.0 / (1.0 + e)
    sig = jnp.where(z >= 0, inv, e * inv)
    nsig = jnp.where(z >= 0, e * inv, inv)
    return lb, sig, nsig, lb + (1.0 - lb) * sig


def hgrn_fwd(z, lb_logits, out_norm, name):
    T = z.shape[0]
    W = z.shape[1] // 4
    H = W // HGRN_DK
    C = HGRN_CHUNK
    HB = _tile(H, HGRN_HEADS_PER_STEP)
    tb = _tile(T, 256)
    nch = tb // C

    def body(zq_ref, zf_ref, zv_ref, zg_ref, lbl_ref, on_ref, x_ref, o_ref, st_ref, s_scr, cum_scr, k_scr, v_scr, o_scr):
        @pl.when(pl.program_id(1) == 0)
        def _():
            s_scr[...] = jnp.zeros_like(s_scr)

        lb, sig, nsig, f = _hgrn_gates(zf_ref[...], lbl_ref[...])
        cum = jnp.dot(_chunk_tri(tb, False), jnp.log(f), precision=HI, preferred_element_type=F32)
        kk = (1.0 - lb) * nsig
        for hh in range(HB):
            cols = slice(hh * HGRN_DK, (hh + 1) * HGRN_DK)
            cum_scr[hh] = cum[:, cols]
            k_scr[hh] = kk[:, cols]
            v_scr[hh] = zv_ref[:, cols]
        row = lax.broadcasted_iota(jnp.int32, (C, HGRN_DK), 0)

        def chunk(c, carry):
            r0 = pl.multiple_of(c * C, C)
            rows = pl.ds(r0, C)
            last_row = pl.ds(r0 + C - 1, 1)
            heads = []
            for hh in range(HB):
                cols = slice(hh * HGRN_DK, (hh + 1) * HGRN_DK)
                q, cu = zq_ref[rows, cols], cum_scr[hh, rows, :]
                st = s_scr[hh]
                st_ref[c, hh] = st
                o = lax.dot_general((q * jnp.exp(cu)).astype(BF), st.astype(BF), NT, preferred_element_type=F32)
                last = cum_scr[hh, last_row, :]
                kg = (k_scr[hh, rows, :] * jnp.exp(last - cu)).astype(BF)
                s_scr[hh] = st * jnp.exp(last) + lax.dot_general(v_scr[hh, rows, :].astype(BF), kg, TN, preferred_element_type=F32)
                heads.append((hh, q, cu, o))
            for hh, q, cu, o in heads:
                qr = q.astype(BF).astype(F32)
                low = jnp.zeros((C - HALF, HGRN_DK), F32)
                for s in range(C):
                    one = pl.ds(r0 + s, 1)
                    sl = slice(0 if s < HALF else HALF, C)
                    e = jnp.exp(jnp.minimum(cu[sl] - cum_scr[hh, one, :], 0.0))
                    col = jnp.sum(qr[sl] * (e * k_scr[hh, one, :]).astype(BF).astype(F32), axis=-1, keepdims=True)
                    col = jnp.where(row[sl] >= s, col, 0.0).astype(BF).astype(F32)
                    term = col * v_scr[hh, one, :].astype(BF).astype(F32)
                    if s < HALF:
                        o = o + term
                    else:
                        low = low + term
                o_scr[hh, rows, :] = o
                o_scr[hh, pl.ds(r0 + HALF, C - HALF), :] += low
            return carry

        lax.fori_loop(0, nch, chunk, 0)
        for hh in range(HB):
            cols = slice(hh * HGRN_DK, (hh + 1) * HGRN_DK)
            o = o_scr[hh]
            o_ref[:, cols] = o
            oh, _ = _rms(o)
            g = zg_ref[:, cols]
            x_ref[:, cols] = (oh * on_ref[...] * (g * _sigmoid(g))).astype(BF)

    def zs(part):
        return pl.BlockSpec((tb, HB * HGRN_DK), lambda hd, i: (i, part * (H // HB) + hd))

    blk = pl.BlockSpec((tb, HB * HGRN_DK), lambda hd, i: (i, hd))
    wide = pltpu.VMEM((HB, tb, HGRN_DK), F32)
    return pl.pallas_call(
        body, name=name, grid=(H // HB, T // tb),
        in_specs=[zs(0), zs(1), zs(2), zs(3), pl.BlockSpec((2, HB * HGRN_DK), lambda hd, i: (0, hd)),
                  pl.BlockSpec((1, HGRN_DK), lambda hd, i: (0, 0))],
        out_specs=[blk, blk, pl.BlockSpec((nch, HB, HGRN_DK, HGRN_DK), lambda hd, i: (i, hd, 0, 0))],
        out_shape=[jax.ShapeDtypeStruct((T, W), BF), jax.ShapeDtypeStruct((T, W), F32),
                   jax.ShapeDtypeStruct((T // C, H, HGRN_DK, HGRN_DK), F32)],
        scratch_shapes=[pltpu.VMEM((HB, HGRN_DK, HGRN_DK), F32), wide, wide, wide, wide],
        compiler_params=_params(2),
    )(z, z, z, z, lb_logits, out_norm)


def hgrn_bwd(dx, z, o, states, lb_logits, out_norm, name):
    T = z.shape[0]
    W = z.shape[1] // 4
    H = W // HGRN_DK
    C = HGRN_CHUNK
    HB = _tile(H, HGRN_HEADS_PER_STEP)
    tb = _tile(T, 256)
    nch = tb // C
    nblk = T // tb

    def body(dx_ref, zq_ref, zf_ref, zv_ref, zg_ref, o_ref, st_ref, lbl_ref, on_ref,
             dq_ref, df_ref, dv_ref, dg_ref, dl_ref, don_ref,
             ds_scr, cum_scr, k_scr, v_scr, q_scr, do_scr, dq_scr, dk_scr, dv_scr, dcum_scr):
        hd, i = pl.program_id(0), pl.program_id(1)

        @pl.when(i == 0)
        def _():
            ds_scr[...] = jnp.zeros_like(ds_scr)
            dl_ref[...] = jnp.zeros_like(dl_ref)

        @pl.when((i == 0) & (hd == 0))
        def _():
            don_ref[...] = jnp.zeros_like(don_ref)

        lb, sig, nsig, f = _hgrn_gates(zf_ref[...], lbl_ref[...])
        cum = jnp.dot(_chunk_tri(tb, False), jnp.log(f), precision=HI, preferred_element_type=F32)
        kk = (1.0 - lb) * nsig
        w = on_ref[...]
        for hh in range(HB):
            cols = slice(hh * HGRN_DK, (hh + 1) * HGRN_DK)
            cum_scr[hh] = cum[:, cols]
            k_scr[hh] = kk[:, cols]
            v_scr[hh] = zv_ref[:, cols]
            q_scr[hh] = zq_ref[:, cols]
            oh, r = _rms(o_ref[:, cols])
            g = zg_ref[:, cols]
            sg = _sigmoid(g)
            dxv = dx_ref[:, cols].astype(F32)
            dg_ref[:, cols] = (dxv * (oh * w) * (sg * (1.0 + g * (1.0 - sg)))).astype(BF)
            don = dxv * (g * sg)
            don_ref[...] += _colsum(don * oh)
            do_scr[hh] = _norm_bwd(don, oh, r, w)
        row = lax.broadcasted_iota(jnp.int32, (C, HGRN_DK), 0)

        def chunk(cc, carry):
            c = nch - 1 - cc
            r0 = pl.multiple_of(c * C, C)
            rows = pl.ds(r0, C)
            last_row = pl.ds(r0 + C - 1, 1)
            heads = []
            for hh in range(HB):
                q, k, v, cu, do = q_scr[hh, rows, :], k_scr[hh, rows, :], v_scr[hh, rows, :], cum_scr[hh, rows, :], do_scr[hh, rows, :]
                st = st_ref[c, hh]
                dst = ds_scr[hh]
                last = cum_scr[hh, last_row, :]
                lam, gam, elast = jnp.exp(cu), jnp.exp(last - cu), jnp.exp(last)
                dob, dstb = do.astype(BF), dst.astype(BF)
                dq = jnp.dot(dob, st.astype(BF), preferred_element_type=F32) * lam
                dv = lax.dot_general((k * gam).astype(BF), dstb, NT, preferred_element_type=F32)
                dk = jnp.dot(v.astype(BF), dstb, preferred_element_type=F32) * gam
                dlast = elast * _colsum(dst * st) + _colsum(dk * k)
                ds_scr[hh] = dst * elast + lax.dot_general(dob, (q * lam).astype(BF), TN, preferred_element_type=F32)
                heads.append((hh, q, k, cu, do, dq, dk, dv, dlast))
            for hh, q, k, cu, do, dq, dk, dv, dlast in heads:
                for first in (True, False):
                    sl = slice(0 if first else HALF, C)
                    qs, cus, dos, rws = q[sl], cu[sl], do[sl], row[sl]
                    dqs, dks, dvs = dq[sl], dk[sl], dv[sl]
                    for s in (range(HALF) if first else range(HALF, C)):
                        one = pl.ds(r0 + s, 1)
                        e = jnp.where(rws >= s, jnp.exp(jnp.minimum(cus - cum_scr[hh, one, :], 0.0)), 0.0)
                        ks = k_scr[hh, one, :]
                        da = jnp.sum(dos * v_scr[hh, one, :], axis=-1, keepdims=True)
                        pq = qs * e
                        a = jnp.sum(pq * ks, axis=-1, keepdims=True)
                        dqs = dqs + da * e * ks
                        dks = jnp.where(rws == s, dks + _colsum(da * pq), dks)
                        dvs = jnp.where(rws == s, dvs + _colsum(a * dos), dvs)
                    if first:
                        dq, dk, dv = dqs, dks, dvs
                        top = pl.ds(r0, HALF)
                        dq_scr[hh, top, :] = dq[:HALF]
                        dk_scr[hh, top, :] = dk[:HALF]
                        dv_scr[hh, top, :] = dv[:HALF]
                        dcum_scr[hh, top, :] = q[:HALF] * dq[:HALF] - k[:HALF] * dk[:HALF]
                    else:
                        low = pl.ds(r0 + HALF, C - HALF)
                        dq_scr[hh, low, :] = dqs
                        dk_scr[hh, low, :] = dks
                        dv_scr[hh, low, :] = dvs
                        dcum_scr[hh, low, :] = qs * dqs - k[sl] * dks + jnp.where(rws == C - 1, dlast, 0.0)
            return carry

        lax.fori_loop(0, nch, chunk, 0)
        tri = _chunk_tri(tb, True)
        for hh in range(HB):
            cols = slice(hh * HGRN_DK, (hh + 1) * HGRN_DK)
            dlf = jnp.dot(tri, dcum_scr[hh], precision=HI, preferred_element_type=F32)
            dk = dk_scr[hh]
            lbh, sigh, nsigh, fh = lb[:, cols], sig[:, cols], nsig[:, cols], f[:, cols]
            common = (1.0 - lbh) * sigh * nsigh
            df_ref[:, cols] = (dlf * common / fh - dk * common).astype(BF)
            dq_ref[:, cols] = dq_scr[hh].astype(BF)
            dv_ref[:, cols] = dv_scr[hh].astype(BF)
            dl0 = _colsum(dlf * nsigh / fh - dk * nsigh) * lbh * (1.0 - lbh)
            dl_ref[:, cols] += jnp.where(lax.broadcasted_iota(jnp.int32, (2, HGRN_DK), 0) == 0, dl0, -dl0)

    def zs(part):
        return pl.BlockSpec((tb, HB * HGRN_DK), lambda hd, i: (nblk - 1 - i, part * (H // HB) + hd))

    blk = pl.BlockSpec((tb, HB * HGRN_DK), lambda hd, i: (nblk - 1 - i, hd))
    bft = jax.ShapeDtypeStruct((T, W), BF)
    scr = pltpu.VMEM((HB, tb, HGRN_DK), F32)
    return pl.pallas_call(
        body, name=name, grid=(H // HB, nblk),
        in_specs=[blk, zs(0), zs(1), zs(2), zs(3), blk,
                  pl.BlockSpec((nch, HB, HGRN_DK, HGRN_DK), lambda hd, i: (nblk - 1 - i, hd, 0, 0)),
                  pl.BlockSpec((2, HB * HGRN_DK), lambda hd, i: (0, hd)), pl.BlockSpec((1, HGRN_DK), lambda hd, i: (0, 0))],
        out_specs=[blk, blk, blk, blk, pl.BlockSpec((2, HB * HGRN_DK), lambda hd, i: (0, hd)),
                   pl.BlockSpec((1, HGRN_DK), lambda hd, i: (0, 0))],
        out_shape=[bft, bft, bft, bft, jax.ShapeDtypeStruct((2, W), F32), jax.ShapeDtypeStruct((1, HGRN_DK), F32)],
        scratch_shapes=[pltpu.VMEM((HB, HGRN_DK, HGRN_DK), F32), scr, scr, scr, scr, scr, scr, scr, scr, scr],
        compiler_params=_params(2),
    )(dx, z, z, z, z, o, states, lb_logits, out_norm)


def _log_sigmoid(x):
    return jnp.minimum(x, 0.0) - jnp.log(1.0 + jnp.exp(-jnp.abs(x)))


def _tri(n, upper):
    r = lax.broadcasted_iota(jnp.int32, (n, n), 0)
    c = lax.broadcasted_iota(jnp.int32, (n, n), 1)
    return ((r >= c) if upper else (c <= r)).astype(F32)


def fox_cum_fwd(kvf, fcol, b_row, name):
    T = kvf.shape[0]
    tb = _tile(T, 512)

    def body(x_ref, b_ref, ct_ref, carry_ref):
        @pl.when(pl.program_id(0) == 0)
        def _():
            carry_ref[...] = jnp.zeros_like(carry_ref)

        lf = _log_sigmoid(x_ref[...] + b_ref[...])
        cum = jnp.dot(_tri(tb, False), lf, precision=HI, preferred_element_type=F32) + carry_ref[...]
        carry_ref[...] += _colsum(lf)
        ct_ref[...] = cum.T

    return pl.pallas_call(
        body, name=name, grid=(T // tb,),
        in_specs=[pl.BlockSpec((tb, LANES), lambda i: (i, fcol)), pl.BlockSpec((1, LANES), lambda i: (0, 0))],
        out_specs=pl.BlockSpec((LANES, tb), lambda i: (0, i)), out_shape=jax.ShapeDtypeStruct((LANES, T), F32),
        scratch_shapes=[pltpu.VMEM((1, LANES), F32)], compiler_params=_params(1),
    )(kvf, b_row)


def fox_cum_bwd(dct, kvf, fcol, b_row, name):
    T = kvf.shape[0]
    tb = _tile(T, 512)
    nblk = T // tb

    def body(dc_ref, x_ref, b_ref, df_ref, db_ref, carry_ref):
        @pl.when(pl.program_id(0) == 0)
        def _():
            carry_ref[...] = jnp.zeros_like(carry_ref)
            db_ref[...] = jnp.zeros_like(db_ref)

        dc = dc_ref[...]
        dlf_t = jnp.dot(dc, _tri(tb, True), precision=HI, preferred_element_type=F32) + carry_ref[...]
        carry_ref[...] += jnp.sum(dc, axis=1, keepdims=True)
        x = x_ref[...] + b_ref[...]
        df = dlf_t.T * _sigmoid(-x)
        df_ref[...] = df.astype(BF)
        db_ref[...] += _colsum(df)

    return pl.pallas_call(
        body, name=name, grid=(nblk,),
        in_specs=[pl.BlockSpec((LANES, tb), lambda i: (0, nblk - 1 - i)),
                  pl.BlockSpec((tb, LANES), lambda i: (nblk - 1 - i, fcol)), pl.BlockSpec((1, LANES), lambda i: (0, 0))],
        out_specs=[pl.BlockSpec((tb, LANES), lambda i: (nblk - 1 - i, 0)), pl.BlockSpec((1, LANES), lambda i: (0, 0))],
        out_shape=[jax.ShapeDtypeStruct((T, LANES), BF), jax.ShapeDtypeStruct((1, LANES), F32)],
        scratch_shapes=[pltpu.VMEM((LANES, 1), F32)], compiler_params=_params(1),
    )(dct, kvf, b_row)


NAUG = 3


def fox_prep(kvf, ct, n_fox, tk):
    T = kvf.shape[0]
    W = n_fox * FOX_HEAD_DIM
    NP = n_fox // 2
    k = kvf[:, :W].astype(BF).reshape(T, NP, 2, FOX_HEAD_DIM)
    c = ct[:n_fox].T.reshape(T, NP, 2)
    hi = lax.reduce_precision(c, 8, 7)
    mid = lax.reduce_precision(c - hi, 8, 7)
    lo = c - hi - mid
    aug = jnp.stack([hi, mid, lo], axis=-1).astype(BF)
    pad = jnp.zeros((T, NP, FOX_HEAD_DIM - NAUG), BF)
    ka = jnp.concatenate([k[:, :, 0], aug[:, :, 0], pad], axis=-1).reshape(T, W)
    kb = jnp.concatenate([aug[:, :, 1], pad, k[:, :, 1]], axis=-1).reshape(T, W)
    v = kvf[:, W:2 * W].astype(BF)
    vt3 = v.reshape(T // tk, tk, NP, LANES).transpose(2, 0, 3, 1)
    return ka, kb, vt3


def fox_fwd(qg, ka, kb, vt3, name):
    T = qg.shape[0]
    W = qg.shape[1] // 2
    NP = W // LANES
    tq = tk = vt3.shape[3]
    scale = FOX_HEAD_DIM ** -0.5
    nk = T // tk
    HD = FOX_HEAD_DIM

    def body(q_ref, g_ref, ka_ref, kb_ref, vt_ref, x_ref, o_ref, lse_ref):
        i = pl.program_id(1)
        lane = lax.broadcasted_iota(jnp.int32, (tq, LANES), 1)
        q2 = q_ref[...] * scale
        qa = jnp.where(lane < HD, q2, jnp.where(lane < HD + NAUG, -1.0, 0.0))
        qb = jnp.where(lane >= HD, q2, jnp.where(lane < NAUG, -1.0, 0.0))
        qts = (qa.T.astype(BF), qb.T.astype(BF))
        krow = lax.broadcasted_iota(jnp.int32, (tk, tq), 0)
        qcol = lax.broadcasted_iota(jnp.int32, (tk, tq), 1)

        def step(j, carry, diag):
            rows = pl.ds(pl.multiple_of(j * tk, tk), tk)
            ks = (ka_ref[rows, :], kb_ref[rows, :])
            vt = vt_ref[j]
            sts = [jnp.dot(ks[a], qts[a], preferred_element_type=F32) for a in range(2)]
            pts, mls = [], []
            for a in range(2):
                m, l, _ = carry[a]
                st = sts[a]
                if diag:
                    st = jnp.where(krow + (j * tk - i * tq) <= qcol, st, -1e30)
                mn = jnp.maximum(m, jnp.max(st, axis=0, keepdims=True))
                alpha = jnp.exp(m - mn)
                pt = jnp.exp(st - mn)
                mls.append((mn, l * alpha + jnp.sum(pt, axis=0, keepdims=True), alpha))
                pts.append(pt.astype(BF))
            out = []
            for a in range(2):
                mn, l, alpha = mls[a]
                acc = carry[a][2] * alpha + jnp.dot(vt[a * HD:(a + 1) * HD, :], pts[a], preferred_element_type=F32)
                out.append((mn, l, acc))
            return tuple(out)

        init = (jnp.full((1, tq), -1e30, F32), jnp.zeros((1, tq), F32), jnp.zeros((HD, tq), F32))
        r = tq // tk
        carry = lax.fori_loop(0, i * r, lambda j, c: step(j, c, False), (init, init))
        for u in range(r):
            carry = step(i * r + u, carry, True)
        (ma, la, acca), (mb, lb, accb) = carry
        ot = jnp.concatenate([acca / la, accb / lb], axis=0)
        o = ot.T
        o_ref[...] = o
        lse_ref[0:1, :] = ma + jnp.log(la)
        lse_ref[1:2, :] = mb + jnp.log(lb)
        x_ref[...] = (o * _sigmoid(g_ref[...])).astype(BF)

    blk = pl.BlockSpec((tq, LANES), lambda hp, i: (i, hp))
    full = pl.BlockSpec((T, LANES), lambda hp, i: (0, hp))
    return pl.pallas_call(
        body, name=name, grid=(NP, T // tq),
        in_specs=[blk, pl.BlockSpec((tq, LANES), lambda hp, i: (i, NP + hp)), full, full,
                  pl.BlockSpec((None, nk, LANES, tk), lambda hp, i: (hp, 0, 0, 0))],
        out_specs=[blk, blk, pl.BlockSpec((None, None, 2, tq), lambda hp, i: (hp, i, 0, 0))],
        out_shape=[jax.ShapeDtypeStruct((T, W), BF), jax.ShapeDtypeStruct((T, W), F32),
                   jax.ShapeDtypeStruct((NP, T // tq, 2, tq), F32)],
        compiler_params=_params(2),
    )(qg, qg, ka, kb, vt3)


def fox_gate_bwd(dx, o, qg, name):
    T, W = o.shape
    tm = _tile(T, 512)

    def body(dx_ref, o_ref, g_ref, do_ref, dg_ref, ds_ref):
        dxv = dx_ref[...].astype(F32)
        sg = _sigmoid(g_ref[...])
        do = dxv * sg
        o = o_ref[...]
        do_ref[...] = do
        dg_ref[...] = (dxv * o * sg * (1.0 - sg)).astype(BF)
        head = jnp.right_shift(lax.broadcasted_iota(jnp.int32, (W, LANES), 0), FOX_HEAD_DIM.bit_length() - 1)
        sel = (head == lax.broadcasted_iota(jnp.int32, (W, LANES), 1)).astype(F32)
        ds_ref[...] = jnp.dot(do * o, sel, precision=HI, preferred_element_type=F32)

    tok = pl.BlockSpec((tm, W), lambda i: (i, 0))
    return pl.pallas_call(
        body, name=name, grid=(T // tm,),
        in_specs=[tok, tok, pl.BlockSpec((tm, W), lambda i: (i, 1))],
        out_specs=[tok, tok, pl.BlockSpec((tm, LANES), lambda i: (i, 0))],
        out_shape=[jax.ShapeDtypeStruct((T, W), F32), jax.ShapeDtypeStruct((T, W), BF), jax.ShapeDtypeStruct((T, LANES), F32)],
        compiler_params=_params(1),
    )(dx, o, qg)


def fox_bwd_prep(qg, kvf, do, tq):
    T = qg.shape[0]
    W = qg.shape[1] // 2
    NP = W // LANES
    scale = FOX_HEAD_DIM ** -0.5
    tr3 = lambda a: a.reshape(T // tq, tq, NP, LANES).transpose(2, 0, 3, 1)
    q = (qg[:, :W] * scale).astype(BF)
    dob = do.astype(BF)
    k = kvf[:, :W]
    return q, tr3(q), dob, tr3(dob), tr3((k * scale).astype(BF)), kvf[:, W:2 * W].astype(BF)


def fox_bwd(q, qt3, dob, dot3, kt3, v, ka, kb, lse4, dsum4, name):
    T, W = q.shape
    NP = W // LANES
    tq = tk = qt3.shape[3]
    nq = T // tq
    HD = FOX_HEAD_DIM

    def body(q_ref, qt_ref, do_ref, dot_ref, kt_ref, v_ref, ka_ref, kb_ref, lse_ref, dsum_ref,
             dqt_ref, dk_ref, dv_ref, dc_ref, dcq_ref, dk_scr, dv_scr, dcl_scr):
        j = pl.program_id(1)

        @pl.when(j == 0)
        def _():
            dqt_ref[...] = jnp.zeros_like(dqt_ref)
            dcq_ref[...] = jnp.zeros_like(dcq_ref)

        dk_scr[...] = jnp.zeros_like(dk_scr)
        dv_scr[...] = jnp.zeros_like(dv_scr)
        dcl_scr[...] = jnp.zeros_like(dcl_scr)
        lane = lax.broadcasted_iota(jnp.int32, (tk, LANES), 1)
        srow = lax.broadcasted_iota(jnp.int32, (LANES, tq), 0)
        lanes_of = (lane < HD, lane >= HD)
        v2 = v_ref[...]
        kt2 = kt_ref[...]
        zero = jnp.zeros((), BF)
        vs = [jnp.where(lanes_of[a], v2, zero) for a in range(2)]
        kts = [kt2[a * HD:(a + 1) * HD, :] for a in range(2)]
        kaug = (ka_ref[...], kb_ref[...])
        krow = lax.broadcasted_iota(jnp.int32, (tk, tq), 0)
        qcol = lax.broadcasted_iota(jnp.int32, (tk, tq), 1)
        neg1 = jnp.full((), -1.0, BF)

        def step(i, carry, diag):
            rows = pl.ds(pl.multiple_of(i * tq, tq), tq)
            qt2 = qt_ref[i]
            dot2 = dot_ref[i]
            q2 = q_ref[rows, :]
            do2 = do_ref[rows, :]
            qts = [jnp.where(srow < HD, qt2, jnp.where(srow < HD + NAUG, neg1, zero)),
                   jnp.where(srow >= HD, qt2, jnp.where(srow < NAUG, neg1, zero))]
            sts = [jnp.dot(kaug[a], qts[a], preferred_element_type=F32) for a in range(2)]
            dps = [jnp.dot(vs[a], dot2, preferred_element_type=F32) for a in range(2)]
            pbs, dsbs = [], []
            for a in range(2):
                pt = jnp.exp(sts[a] - lse_ref[i, a:a + 1, :])
                if diag:
                    pt = jnp.where(krow <= qcol, pt, 0.0)
                ds = pt * (dps[a] - dsum_ref[i, a:a + 1, :])
                dcq_ref[i, a:a + 1, :] += _colsum(ds)
                part = ds[:, 0:LANES]
                for u in range(1, tq // LANES):
                    part = part + ds[:, u * LANES:(u + 1) * LANES]
                dcl_scr[a] += part
                pbs.append(pt.astype(BF))
                dsbs.append(ds.astype(BF))
            qn = [jnp.where(lanes_of[a], q2, zero) for a in range(2)]
            don = [jnp.where(lanes_of[a], do2, zero) for a in range(2)]
            dv_scr[...] += (jnp.dot(pbs[0], don[0], preferred_element_type=F32) +
                            jnp.dot(pbs[1], don[1], preferred_element_type=F32))
            dk_scr[...] += (jnp.dot(dsbs[0], qn[0], preferred_element_type=F32) +
                            jnp.dot(dsbs[1], qn[1], preferred_element_type=F32))
            for a in range(2):
                dqt_ref[i, a * HD:(a + 1) * HD, :] += jnp.dot(kts[a], dsbs[a], preferred_element_type=F32)
            return carry

        step(j, 0, True)
        lax.fori_loop(j + 1, nq, lambda i, c: step(i, c, False), 0)
        dk_ref[...] = dk_scr[...].astype(BF)
        dv_ref[...] = dv_scr[...].astype(BF)
        for a in range(2):
            dc_ref[a:a + 1, :] = -_colsum(dcl_scr[a].T)

    tile = pl.BlockSpec((tk, LANES), lambda hp, j: (j, hp))
    full = pl.BlockSpec((T, LANES), lambda hp, j: (0, hp))
    full3 = pl.BlockSpec((None, nq, LANES, tq), lambda hp, j: (hp, 0, 0, 0))
    rows4 = pl.BlockSpec((None, nq, 2, tq), lambda hp, j: (hp, 0, 0, 0))
    bft = jax.ShapeDtypeStruct((T, W), BF)
    r4 = jax.ShapeDtypeStruct((NP, nq, 2, tq), F32)
    return pl.pallas_call(
        body, name=name, grid=(NP, nq),
        in_specs=[full, full3, full, full3, pl.BlockSpec((None, None, LANES, tk), lambda hp, j: (hp, j, 0, 0)),
                  tile, tile, tile, rows4, rows4],
        out_specs=[full3, tile, tile, pl.BlockSpec((None, None, 2, tk), lambda hp, j: (hp, j, 0, 0)), rows4],
        out_shape=[jax.ShapeDtypeStruct((NP, nq, LANES, tq), F32), bft, bft, r4, r4],
        scratch_shapes=[pltpu.VMEM((tk, LANES), F32), pltpu.VMEM((tk, LANES), F32), pltpu.VMEM((2, tk, LANES), F32)],
        compiler_params=_params(2),
    )(q, qt3, dob, dot3, kt3, v, ka, kb, lse4, dsum4)


def loss_fwd_bwd(y, target, name):
    T, D = y.shape
    tm = _tile(T, 512)

    def body(y_ref, t_ref, dy_ref, l_ref):
        @pl.when(pl.program_id(0) == 0)
        def _():
            l_ref[...] = jnp.zeros_like(l_ref)

        d = y_ref[...] - t_ref[...]
        dy_ref[...] = d * (1.0 / D)
        l_ref[...] += 0.5 * jnp.sum(jnp.mean(d * d, axis=-1, keepdims=True), axis=0, keepdims=True)

    tok = pl.BlockSpec((tm, D), lambda i: (i, 0))
    return pl.pallas_call(
        body, name=name, grid=(T // tm,), in_specs=[tok, tok],
        out_specs=[tok, pl.BlockSpec((1, 1), lambda i: (0, 0))],
        out_shape=[jax.ShapeDtypeStruct((T, D), F32), jax.ShapeDtypeStruct((1, 1), F32)], compiler_params=_params(1),
    )(y, target)


def adamw(parts, w, m, v, layer, prev, name):
    L, R, C = w.shape
    tr = _tile(R, 256)
    c1 = 1.0 / (1.0 - ADAM_B1 ** ADAM_STEP)
    c2 = 1.0 / (1.0 - ADAM_B2 ** ADAM_STEP)

    def body(p_ref, w_ref, m_ref, v_ref, *rest):
        g_ref, d_ref, mo_ref, vo_ref = rest[-4:]
        g = p_ref[0].astype(F32)
        for d in range(1, N_DEV):
            g = g + p_ref[d].astype(F32)
        mn = ADAM_B1 * m_ref[...] + (1.0 - ADAM_B1) * g
        vn = ADAM_B2 * v_ref[...] + (1.0 - ADAM_B2) * (g * g)
        g_ref[...] = g
        mo_ref[...] = mn
        vo_ref[...] = vn
        d_ref[...] = -ADAM_LR * ((mn * c1) / (jnp.sqrt(vn * c2) + ADAM_EPS) + ADAM_WD * w_ref[...])

    blk = pl.BlockSpec((None, tr, C), lambda i: (layer, i, 0))
    shp = jax.ShapeDtypeStruct((L, R, C), F32)
    in_specs = [pl.BlockSpec((N_DEV, tr, C), lambda i: (0, i, 0)), blk, blk, blk]
    args = [parts, w, m, v]
    aliases = {}
    if prev is not None:
        in_specs += [pl.BlockSpec(memory_space=pl.ANY)] * 4
        args += list(prev)
        aliases = {4 + j: j for j in range(4)}
    return pl.pallas_call(
        body, name=name, grid=(R // tr,), in_specs=in_specs, out_specs=[blk, blk, blk, blk],
        out_shape=[shp, shp, shp, shp], input_output_aliases=aliases, compiler_params=_params(1),
    )(*args)


HBM_SPEC = pl.BlockSpec(memory_space=pltpu.HBM)
SEM_SPEC = pl.BlockSpec(memory_space=pltpu.SEMAPHORE)
EFFECT = pltpu.SideEffectType.DATAFLOW_SIDE_EFFECTING


def _mesh_pos():
    return lax.axis_index("x"), lax.axis_index("y"), lax.axis_index("c")


def _flip(v, bit):
    return v + bit - 2 * v * bit


def _peer(pos, delta):
    x, y, c = pos
    px, py, pc = _flip(x, (delta >> 2) & 1), _flip(y, (delta >> 1) & 1), _flip(c, delta & 1)
    return (px, py, pc), 4 * px + 2 * py + pc


def _me():
    x, y, c = _mesh_pos()
    return 4 * x + 2 * y + c


def _copies(src_refs, land_refs, whole, send, recv, incoming):
    pos = _mesh_pos()
    me = 4 * pos[0] + 2 * pos[1] + pos[2]
    out = []
    for k in range(len(src_refs)):
        for d in range(1, N_DEV):
            dev, idx = _peer(pos, d)
            j = k * (N_DEV - 1) + d - 1
            src = src_refs[k] if whole[k] else src_refs[k].at[idx]
            out.append(pltpu.make_async_remote_copy(
                src_ref=src, dst_ref=land_refs[k].at[idx if incoming else me], send_sem=send.at[j], recv_sem=recv.at[j],
                device_id=dev, device_id_type=pl.DeviceIdType.MESH))
    return out


def exchange_start(srcs, lands, whole, name):
    n = len(srcs)

    def body(*refs):
        for copy in _copies(refs[:n], refs[n:2 * n], whole, refs[2 * n], refs[2 * n + 1], False):
            copy.start()
        refs[-1][...] = jnp.zeros_like(refs[-1])

    sems = pltpu.SemaphoreType.DMA((n * (N_DEV - 1),))
    thru = [pltpu.HBM(a.shape, a.dtype) for a in list(srcs) + list(lands)]
    res = pl.pallas_call(
        body, name=name, in_specs=[HBM_SPEC] * (2 * n),
        out_specs=[SEM_SPEC, SEM_SPEC] + [HBM_SPEC] * (2 * n) + [pl.BlockSpec(memory_space=pltpu.VMEM)],
        out_shape=[sems, sems] + thru + [jax.ShapeDtypeStruct((8, LANES), F32)],
        input_output_aliases={j: 2 + j for j in range(2 * n)},
        compiler_params=pltpu.CompilerParams(has_side_effects=EFFECT),
    )(*[pltpu.with_memory_space_constraint(a, pltpu.HBM) for a in list(srcs) + list(lands)])
    return dict(send=res[0], recv=res[1], srcs=res[2:2 + n], lands=res[2 + n:2 + 2 * n], whole=whole, token=res[-1])


def exchange_wait(handle, after, name):
    n = len(handle["srcs"])
    whole = handle["whole"]

    def body(*refs):
        for copy in _copies(refs[:n], refs[n:2 * n], whole, refs[2 * n], refs[2 * n + 1], False):
            copy.wait_send()
        for copy in _copies(refs[:n], refs[n:2 * n], whole, refs[2 * n], refs[2 * n + 1], True):
            copy.wait_recv()

    bufs = list(handle["srcs"]) + list(handle["lands"])
    res = pl.pallas_call(
        body, name=name, in_specs=[HBM_SPEC] * (2 * n) + [SEM_SPEC, SEM_SPEC] + [pl.BlockSpec(memory_space=pl.ANY)] * len(after),
        out_specs=[HBM_SPEC] * (2 * n), out_shape=[pltpu.HBM(a.shape, a.dtype) for a in bufs],
        input_output_aliases={j: j for j in range(2 * n)},
        compiler_params=pltpu.CompilerParams(has_side_effects=EFFECT),
    )(*bufs, handle["send"], handle["recv"], *after)
    return list(res[n:])


def _landing(own, whole):
    me = _me()
    if not whole:
        own = lax.dynamic_index_in_dim(own, me, 0, keepdims=False)
    buf = lax.empty((N_DEV,) + own.shape, own.dtype)
    return lax.dynamic_update_slice(buf, own[None], (me,) + (0,) * own.ndim)


def _row(v):
    return v.reshape(1, -1)


def _pad_lanes(v, n):
    return jnp.pad(v, ((0, 0), (0, n - v.shape[1])))


TRANSPOSED = ("ffn1_w_in", "ffn2_w_in")

GATHER_GROUPS = (
    ("ffn1_in_0", (("ffn1_w_in", 0),)),
    ("ffn1_out_0", (("ffn1_w_out", 0),)),
    ("hgrn", (("hgrn_w_in", 0), ("hgrn_w_out", 0))),
    ("rest_0", (("ffn2_w_in", 0), ("ffn2_w_out", 0), ("ple_w_gate", 0), ("ple_w_proj", 0), ("fox_w_kvf", 0))),
    ("ffn1_1", (("ffn1_w_in", 1), ("ffn1_w_out", 1))),
    ("fox", (("fox_w_qg", 0), ("fox_w_out", 0))),
    ("rest_1", (("ffn2_w_in", 1), ("ffn2_w_out", 1), ("ple_w_gate", 1), ("ple_w_proj", 1))),
)


def kernel(x, p, ffn1_norm_pre, ffn1_w_in, ffn1_w_out, ffn1_norm_post, mix_norm_pre, mix_norm_post, ffn2_norm_pre, ffn2_w_in, ffn2_w_out, ffn2_norm_post, hgrn_w_in, hgrn_lb_logits, hgrn_out_norm, hgrn_w_out, kv_norm, fox_w_kvf, fox_b_f, fox_w_qg, fox_w_out, ple_norm_pre, ple_w_gate, ple_w_proj, ple_norm_post, loss_target, m_ffn1_norm_pre, m_ffn1_w_in, m_ffn1_w_out, m_ffn1_norm_post, m_mix_norm_pre, m_mix_norm_post, m_ffn2_norm_pre, m_ffn2_w_in, m_ffn2_w_out, m_ffn2_norm_post, m_hgrn_w_in, m_hgrn_lb_logits, m_hgrn_out_norm, m_hgrn_w_out, m_kv_norm, m_fox_w_kvf, m_fox_b_f, m_fox_w_qg, m_fox_w_out, m_ple_norm_pre, m_ple_w_gate, m_ple_w_proj, m_ple_norm_post, v_ffn1_norm_pre, v_ffn1_w_in, v_ffn1_w_out, v_ffn1_norm_post, v_mix_norm_pre, v_mix_norm_post, v_ffn2_norm_pre, v_ffn2_w_in, v_ffn2_w_out, v_ffn2_norm_post, v_hgrn_w_in, v_hgrn_lb_logits, v_hgrn_out_norm, v_hgrn_w_out, v_kv_norm, v_fox_w_kvf, v_fox_b_f, v_fox_w_qg, v_fox_w_out, v_ple_norm_pre, v_ple_w_gate, v_ple_w_proj, v_ple_norm_post):
    weights = dict(ffn1_norm_pre=ffn1_norm_pre, ffn1_w_in=ffn1_w_in, ffn1_w_out=ffn1_w_out, ffn1_norm_post=ffn1_norm_post, mix_norm_pre=mix_norm_pre, mix_norm_post=mix_norm_post, ffn2_norm_pre=ffn2_norm_pre, ffn2_w_in=ffn2_w_in, ffn2_w_out=ffn2_w_out, ffn2_norm_post=ffn2_norm_post, hgrn_w_in=hgrn_w_in, hgrn_lb_logits=hgrn_lb_logits, hgrn_out_norm=hgrn_out_norm, hgrn_w_out=hgrn_w_out, kv_norm=kv_norm, fox_w_kvf=fox_w_kvf, fox_b_f=fox_b_f, fox_w_qg=fox_w_qg, fox_w_out=fox_w_out, ple_norm_pre=ple_norm_pre, ple_w_gate=ple_w_gate, ple_w_proj=ple_w_proj, ple_norm_post=ple_norm_post)
    mom1 = dict(ffn1_norm_pre=m_ffn1_norm_pre, ffn1_w_in=m_ffn1_w_in, ffn1_w_out=m_ffn1_w_out, ffn1_norm_post=m_ffn1_norm_post, mix_norm_pre=m_mix_norm_pre, mix_norm_post=m_mix_norm_post, ffn2_norm_pre=m_ffn2_norm_pre, ffn2_w_in=m_ffn2_w_in, ffn2_w_out=m_ffn2_w_out, ffn2_norm_post=m_ffn2_norm_post, hgrn_w_in=m_hgrn_w_in, hgrn_lb_logits=m_hgrn_lb_logits, hgrn_out_norm=m_hgrn_out_norm, hgrn_w_out=m_hgrn_w_out, kv_norm=m_kv_norm, fox_w_kvf=m_fox_w_kvf, fox_b_f=m_fox_b_f, fox_w_qg=m_fox_w_qg, fox_w_out=m_fox_w_out, ple_norm_pre=m_ple_norm_pre, ple_w_gate=m_ple_w_gate, ple_w_proj=m_ple_w_proj, ple_norm_post=m_ple_norm_post)
    mom2 = dict(ffn1_norm_pre=v_ffn1_norm_pre, ffn1_w_in=v_ffn1_w_in, ffn1_w_out=v_ffn1_w_out, ffn1_norm_post=v_ffn1_norm_post, mix_norm_pre=v_mix_norm_pre, mix_norm_post=v_mix_norm_post, ffn2_norm_pre=v_ffn2_norm_pre, ffn2_w_in=v_ffn2_w_in, ffn2_w_out=v_ffn2_w_out, ffn2_norm_post=v_ffn2_norm_post, hgrn_w_in=v_hgrn_w_in, hgrn_lb_logits=v_hgrn_lb_logits, hgrn_out_norm=v_hgrn_out_norm, hgrn_w_out=v_hgrn_w_out, kv_norm=v_kv_norm, fox_w_kvf=v_fox_w_kvf, fox_b_f=v_fox_b_f, fox_w_qg=v_fox_w_qg, fox_w_out=v_fox_w_out, ple_norm_pre=v_ple_norm_pre, ple_w_gate=v_ple_w_gate, ple_w_proj=v_ple_w_proj, ple_norm_post=v_ple_norm_post)
    names = list(weights)
    big = ["ffn1_w_in", "ffn1_w_out", "ffn2_w_in", "ffn2_w_out", "hgrn_w_in", "hgrn_w_out", "fox_w_kvf", "fox_w_qg",
           "fox_w_out", "ple_w_gate", "ple_w_proj"]
    small_names = [n for n in names if n not in big]

    T, D = x.shape[1], x.shape[2]
    depth = p.shape[0]
    h0 = x.reshape(T, D)
    target = loss_target.reshape(T, D)
    p3 = p.reshape(depth, T, p.shape[3])
    n_fox = fox_b_f.shape[0]
    fox_w = n_fox * FOX_HEAD_DIM
    fcol = 2 * fox_w // LANES
    b_row = _pad_lanes(_row(fox_b_f), LANES)

    tok = jnp.zeros((), F32)
    handles = {}
    for gname, keys in GATHER_GROUPS:
        shards = []
        for n, l in keys:
            w = weights[n]
            w = w[l] if w.ndim == 3 else w
            shards.append(((w.T if n in TRANSPOSED else w) + tok).astype(BF))
        handles[gname] = exchange_start(shards, [_landing(s, True) for s in shards], [True] * len(keys), f"gather_start_{gname}")
        tok = handles[gname]["token"][0, 0]
    W = {}

    def arrive(gname, after):
        lands = exchange_wait(handles[gname], after, f"gather_wait_{gname}")
        W.update(dict(zip(dict(GATHER_GROUPS)[gname], lands)))

    def w_rows(n, l):
        return W[n, l].reshape(-1, D)

    norm = lambda name, i: weights[name][i:i + 1]

    saved = []
    h = h0
    kvf = ct = kvf_w = None
    tq = _tile(T, 512)
    for i in range(depth):
        s = {}
        if i == 0:
            arrive("ffn1_in_0", [handles[GATHER_GROUPS[-1][0]]["token"]])
        for k in (1, 2):
            if k == 2:
                s["h_a"] = h
                if i == 0:
                    arrive("hgrn", [h])
                    z, xn = norm_mm(h, norm("mix_norm_pre", i), W["hgrn_w_in", 0], "hgrn_in")
                    xm, o, states = hgrn_fwd(z, hgrn_lb_logits, hgrn_out_norm, "hgrn_scan")
                    s.update(z=z, o=o, states=states)
                    wmix = w_rows("hgrn_w_out", 0)
                else:
                    arrive("fox", [h])
                    qg, xn = norm_mm(h, norm("mix_norm_pre", i), W["fox_w_qg", 0], "fox_qg")
                    tqf = _tile(T, 1024)
                    ka, kb, vt3 = fox_prep(kvf, ct, n_fox, tqf)
                    xm, o, lse = fox_fwd(qg, ka, kb, vt3, "fox_attn")
                    lse = lse.reshape(-1, T // tqf, 2, tqf // tq, tq).transpose(0, 1, 3, 2, 4).reshape(-1, T // tq, 2, tq)
                    s.update(qg=qg, o=o, lse=lse, ka=ka, kb=kb)
                    wmix = w_rows("fox_w_out", 0)
                h, ym = mm_norm_res(xm, wmix, norm("mix_norm_post", i), h, 1.0, f"mix_out_{i}", kb=_tile(xm.shape[1], 512))
                s.update(xm=xm, ym=ym, xn_mix=xn)
                arrive(f"rest_{i}", [h])
            gate, up, a, xn = norm_mm_swiglu(h, norm(f"ffn{k}_norm_pre", i), W[f"ffn{k}_w_in", i], f"ffn{k}_in_{i}")
            if (i, k) == (0, 1):
                arrive("ffn1_out_0", [a])
            hn, y = mm_norm_res(a, w_rows(f"ffn{k}_w_out", i), norm(f"ffn{k}_norm_post", i), h, 0.5, f"ffn{k}_out_{i}")
            s[f"ffn{k}"] = (h, gate, up, a, y, xn)
            h = hn
        s["h_c"] = h
        ple_proj = W["ple_w_proj", i].transpose(1, 0, 2).reshape(p.shape[3], D)
        h, pgate, pp = ple_fwd(h, norm("ple_norm_pre", i), w_rows("ple_w_gate", i), p3, ple_proj, norm("ple_norm_post", i),
                               i, f"ple_{i}")
        s.update(pgate=pgate, pp=pp)
        saved.append(s)
        if i == 0:
            kvf_nat = W["fox_w_kvf", 0].transpose(1, 0, 2).reshape(D, -1)
            kvf_cols = kvf_nat.shape[1]
            kvf_w = _pad_lanes(kvf_nat, 2 * fox_w + LANES)
            kvf, xn_kv = norm_mm(h, _row(kv_norm), kvf_w, "fox_kvf", tn=kvf_w.shape[1])
            ct = fox_cum_fwd(kvf, fcol, b_row, "fox_cum")
            h_kv = h
            arrive("ffn1_1", [h])

    dh, loss_part = loss_fwd_bwd(h, target, "loss")
    loss = lax.psum(loss_part[0, 0], ("x", "y", "c"))

    gsmall = {n: [None] * weights[n].shape[0] if weights[n].ndim == 2 else None for n in small_names}
    sent = []

    def send(gname, keys, srcs, whole):
        lands = [_landing(a, w) for a, w in zip(srcs, whole)]
        hd = exchange_start(srcs, lands, whole, f"scatter_start_{gname}")
        sent.append((gname, keys, hd))
        return hd["token"][0:1, 0:1]

    def send_grads(gname, grads):
        return send(gname, list(grads), list(grads.values()), [False] * len(grads))

    for i in reversed(range(depth)):
        s = saved[i]
        grads = {}
        dh, du, dpp, xn, dgpre, dgpost = ple_bwd(dh, s["h_c"], s["pgate"], s["pp"], norm("ple_norm_pre", i),
                                                 w_rows("ple_w_gate", i), norm("ple_norm_post", i), f"ple_bwd_{i}")
        gsmall["ple_norm_pre"][i], gsmall["ple_norm_post"][i] = dgpre, dgpost
        grads["ple_w_gate", i] = mm_tn(xn, du, f"ple_dgate_{i}", xb=_tile(D, 512)).reshape(N_DEV, -1, D)
        dproj = mm_tn(p3, dpp, f"ple_dproj_{i}", x_layer=i, yb=D)[0]
        grads["ple_w_proj", i] = dproj.reshape(dproj.shape[0], N_DEV, -1).transpose(1, 0, 2)
        for k in (2, 1):
            hin, gate, up, a, y, xn = s[f"ffn{k}"]
            ffn_cs = gate.shape[2]
            dy, dgpost, dz = nbwd_mm_nt(dh, y, norm(f"ffn{k}_norm_post", i), w_rows(f"ffn{k}_w_out", i), 0.5, ffn_cs,
                                        f"ffn{k}_bwd_out_{i}", gate=gate, up=up)
            dz = dz.reshape(-1, T, ffn_cs)
            grads[f"ffn{k}_w_out", i] = mm_tn(a, dy, f"ffn{k}_dwout_{i}").reshape(N_DEV, -1, D)
            grads[f"ffn{k}_w_in", i] = mm_tn(dz, xn, f"ffn{k}_dwin_{i}").reshape(N_DEV, ffn_cs, D)
            tokv = send_grads(f"ffn{k}_{i}", grads)
            grads = {}
            dh, dgpre = mm_nt_nbwd(dz, W[f"ffn{k}_w_in", i], hin, norm(f"ffn{k}_norm_pre", i) + tokv, dh, f"ffn{k}_bwd_in_{i}",
                                   transposed=True)
            gsmall[f"ffn{k}_norm_pre"][i], gsmall[f"ffn{k}_norm_post"][i] = dgpre, dgpost
            if k == 2:
                nm = "hgrn" if i == 0 else "fox"
                wmix = w_rows(f"{nm}_w_out", 0)
                dy, dgpost, dxm = nbwd_mm_nt(dh, s["ym"], norm("mix_norm_post", i), wmix, 1.0, _tile(wmix.shape[0], 512),
                                             f"{nm}_bwd_out")
                grads[f"{nm}_w_out", 0] = mm_tn(s["xm"], dy, f"{nm}_dwout", xb=_tile(wmix.shape[0], 512)).reshape(N_DEV, -1, D)
                gsmall["mix_norm_post"][i] = dgpost
                if i == 0:
                    dq, df, dv, dg, dlog, don = hgrn_bwd(dxm, s["z"], s["o"], s["states"], hgrn_lb_logits, hgrn_out_norm,
                                                         "hgrn_scan_bwd")
                    gsmall["hgrn_lb_logits"] = [dlog[0:1], dlog[1:2]]
                    gsmall["hgrn_out_norm"] = [don]
                    dzm = jnp.concatenate([dq, df, dv, dg], axis=1)
                    nmin = "hgrn_w_in"
                else:
                    do, dg, dsum = fox_gate_bwd(dxm, s["o"], s["qg"], "fox_gate_bwd")
                    dsum4 = dsum[:, :n_fox].T.reshape(n_fox // 2, 2, T // tq, tq).transpose(0, 2, 1, 3)
                    dqt, dk_sh, dv_sh, dc4, dcq4 = fox_bwd(*fox_bwd_prep(s["qg"], kvf, do, tq), s["ka"], s["kb"], s["lse"], dsum4,
                                                           "fox_attn_bwd")
                    dq = dqt.transpose(1, 3, 0, 2).reshape(T, fox_w)
                    dzm = jnp.concatenate([dq.astype(BF), dg], axis=1)
                    nmin = "fox_w_qg"
                wmin = W[nmin, 0]
                grads[nmin, 0] = mm_tn(s["xn_mix"], dzm, f"{nm}_dwin", yb=wmin.shape[2])
                tokv = send_grads(f"mix_{i}", grads)
                grads = {}
                dh, dgpre = mm_nt_nbwd(dzm, wmin, s["h_a"], norm("mix_norm_pre", i) + tokv, dh, f"{nm}_bwd_in", tn=wmin.shape[2])
                gsmall["mix_norm_pre"][i] = dgpre
        if i == 1:
            dct = (dc4 + dcq4).transpose(0, 2, 1, 3).reshape(n_fox, T)
            dct = jnp.pad(dct, ((0, LANES - n_fox), (0, 0)))
            dflog, db = fox_cum_bwd(dct, kvf, fcol, b_row, "fox_cum_bwd")
            gsmall["fox_b_f"] = db[:, :n_fox]
            dkvf = jnp.concatenate([dk_sh, dv_sh, dflog], axis=1)
            dwk = mm_tn(xn_kv, dkvf, "fox_dwkvf", yb=kvf_w.shape[1])[0][:, :kvf_cols]
            tokv = send_grads("kvf", {("fox_w_kvf", 0): dwk.reshape(D, N_DEV, -1).transpose(1, 0, 2)})
            dh, dgkv = mm_nt_nbwd(dkvf, kvf_w, h_kv, _row(kv_norm) + tokv, dh, "fox_kvf_bwd", tn=kvf_w.shape[1])
            gsmall["kv_norm"] = dgkv
    grad_x = dh.reshape(x.shape)

    def small_rows(n):
        g = gsmall[n]
        rows = g if isinstance(g, list) else [g]
        return [_pad_lanes(r, D) for r in rows]

    counts = {n: len(small_rows(n)) for n in small_names}
    packed = jnp.concatenate([r for n in small_names for r in small_rows(n)], axis=0)
    n_rows = packed.shape[0]
    packed = jnp.pad(packed, ((0, -n_rows % 8), (0, 0)))
    send("small", ["small"], [packed], [True])

    res = {}
    after = [dh]
    small_parts = None
    for gname, keys, hd in sent:
        lands = exchange_wait(hd, after, f"scatter_wait_{gname}")
        after = []
        for key, parts in zip(keys, lands):
            if key == "small":
                small_parts = parts
                continue
            n, l = key
            w = weights[n]
            if w.ndim == 2:
                as3 = lambda a: a.reshape((1,) + a.shape)
            elif n in TRANSPOSED:
                as3 = lambda a: a.transpose(0, 2, 1)
            else:
                as3 = lambda a: a
            res[n] = adamw(parts, as3(w), as3(mom1[n]), as3(mom2[n]), l, res.get(n), f"adamw_{n}_{l}")
            after.append(res[n][1])
    for n in TRANSPOSED:
        res[n] = [a.transpose(0, 2, 1) for a in res[n]]

    def pack(d):
        rows = []
        for n in small_names:
            a = d[n]
            rows.append(_pad_lanes(a.reshape(-1, a.shape[-1]), D))
        a = jnp.concatenate(rows, axis=0)
        return jnp.pad(a, ((0, -n_rows % 8), (0, 0)))[None]

    sm = adamw(small_parts, pack(weights), pack(mom1), pack(mom2), 0, None, "adamw_small")
    off = 0
    for n in small_names:
        w = weights[n]
        res[n] = [a[0, off:off + counts[n], :w.shape[-1]].reshape(w.shape) for a in sm]
        off += counts[n]

    out = [loss, grad_x]
    for j in range(4):
        out += [res[n][j].reshape(weights[n].shape) for n in names]
    return tuple(out)
```

```python
import jax
import jax.numpy as jnp
from jax import lax
from jax.experimental import pallas as pl
from jax.experimental.pallas import tpu as pltpu

F32 = jnp.float32
BF = jnp.bfloat16
NORM_EPS = 1e-6
N_DEV = 8
HGRN_DK = 128
HGRN_CHUNK = 16
HGRN_HEADS_PER_STEP = 8
HALF = HGRN_CHUNK // 2
FOX_HEAD_DIM = 64
LANES = 128
ADAM_LR, ADAM_B1, ADAM_B2, ADAM_EPS, ADAM_WD, ADAM_STEP = 0.001, 0.9, 0.999, 1e-08, 0.01, 10
VMEM_LIMIT = 56 * 1024 * 1024
HI = lax.Precision.HIGHEST
NT = (((1,), (1,)), ((), ()))
TN = (((0,), (0,)), ((), ()))


def _params(n_axes):
    return pltpu.CompilerParams(dimension_semantics=("arbitrary",) * n_axes, vmem_limit_bytes=VMEM_LIMIT)


def _tile(n, want):
    t = min(n, want)
    while n % t:
        t //= 2
    return t


def _sigmoid(x):
    return 1.0 / (1.0 + jnp.exp(-x))


def _rms(x):
    r = lax.rsqrt(jnp.mean(x * x, axis=-1, keepdims=True) + NORM_EPS)
    return x * r, r


def _norm_bwd(dy, xhat, r, g):
    dxh = dy * g
    return r * (dxh - xhat * jnp.mean(dxh * xhat, axis=-1, keepdims=True))


def _colsum(x):
    return jnp.sum(x, axis=0, keepdims=True)


def _w_spec(w, blk):
    if w.ndim == 3:
        return lambda off: pl.BlockSpec((None, w.shape[1], w.shape[2]), lambda i, j: (j + off, 0, 0))
    return lambda off: pl.BlockSpec((w.shape[0], blk), lambda i, j: (0, j + off))


def norm_mm_swiglu(h, g, w3, name):
    T, D = h.shape
    nb, cs, _ = w3.shape
    nh = nb // 2
    tm = _tile(T, 1024)

    def body(h_ref, g_ref, wg_ref, wu_ref, gate_ref, up_ref, a_ref, xn_ref):
        @pl.when(pl.program_id(1) == 0)
        def _():
            xh, _ = _rms(h_ref[...])
            xn_ref[...] = (xh * g_ref[...]).astype(BF)

        xn = xn_ref[...]
        gt = lax.dot_general(xn, wg_ref[...], NT, preferred_element_type=F32)
        up = lax.dot_general(xn, wu_ref[...], NT, preferred_element_type=F32)
        sg = _sigmoid(gt)
        silu = gt * sg
        gate_ref[...] = (up * (sg * (1.0 + gt * (1.0 - sg)))).astype(BF)
        up_ref[...] = silu.astype(BF)
        a_ref[...] = (silu * up).astype(BF)

    ws = _w_spec(w3, cs)
    blk = pl.BlockSpec((None, tm, cs), lambda i, j: (j, i, 0))
    shp = jax.ShapeDtypeStruct((nh, T, cs), BF)
    return pl.pallas_call(
        body, name=name, grid=(T // tm, nh),
        in_specs=[pl.BlockSpec((tm, D), lambda i, j: (i, 0)), pl.BlockSpec((1, D), lambda i, j: (0, 0)), ws(0), ws(nh)],
        out_specs=[blk, blk, blk, pl.BlockSpec((tm, D), lambda i, j: (i, 0))],
        out_shape=[shp, shp, shp, jax.ShapeDtypeStruct((T, D), BF)], compiler_params=_params(2),
    )(h, g, w3, w3)


def norm_mm(h, g, w, name, tn=None):
    T, D = h.shape
    if w.ndim == 3:
        nb, cs = w.shape[0], w.shape[2]
    else:
        cs = tn
        nb = w.shape[1] // cs
    tm = _tile(T, 1024)

    def body(h_ref, g_ref, w_ref, z_ref, xn_ref):
        @pl.when(pl.program_id(1) == 0)
        def _():
            xh, _ = _rms(h_ref[...])
            xn_ref[...] = (xh * g_ref[...]).astype(BF)

        z_ref[...] = jnp.dot(xn_ref[...], w_ref[...], preferred_element_type=F32)

    return pl.pallas_call(
        body, name=name, grid=(T // tm, nb),
        in_specs=[pl.BlockSpec((tm, D), lambda i, j: (i, 0)), pl.BlockSpec((1, D), lambda i, j: (0, 0)),
                  _w_spec(w, cs)(0)],
        out_specs=[pl.BlockSpec((tm, cs), lambda i, j: (i, j)), pl.BlockSpec((tm, D), lambda i, j: (i, 0))],
        out_shape=[jax.ShapeDtypeStruct((T, nb * cs), F32), jax.ShapeDtypeStruct((T, D), BF)], compiler_params=_params(2),
    )(h, g, w)


def _x_spec(x, tm, kb):
    if x.ndim == 3:
        return pl.BlockSpec((None, tm, x.shape[2]), lambda i, j: (j, i, 0))
    return pl.BlockSpec((tm, kb), lambda i, j: (i, j))


def mm_norm_res(x, w2, g, h, coef, name, kb=None):
    T, D = h.shape
    if x.ndim == 3:
        nb, kb = x.shape[0], x.shape[2]
    else:
        nb = x.shape[1] // kb
    tm = _tile(T, 1024)

    def body(x_ref, w_ref, h_ref, g_ref, hn_ref, y_ref, acc_ref):
        b = pl.program_id(1)

        @pl.when(b == 0)
        def _():
            acc_ref[...] = jnp.zeros_like(acc_ref)

        acc_ref[...] += jnp.dot(x_ref[...], w_ref[...], preferred_element_type=F32)

        @pl.when(b == nb - 1)
        def _():
            y = acc_ref[...]
            y_ref[...] = y
            yh, _ = _rms(y)
            hn_ref[...] = h_ref[...] + coef * (yh * g_ref[...])

    tok = pl.BlockSpec((tm, D), lambda i, j: (i, 0))
    shp = jax.ShapeDtypeStruct((T, D), F32)
    return pl.pallas_call(
        body, name=name, grid=(T // tm, nb),
        in_specs=[_x_spec(x, tm, kb), pl.BlockSpec((kb, D), lambda i, j: (j, 0)), tok,
                  pl.BlockSpec((1, D), lambda i, j: (0, 0))],
        out_specs=[tok, tok], out_shape=[shp, shp],
        scratch_shapes=[pltpu.VMEM((tm, D), F32)], compiler_params=_params(2),
    )(x, w2, h, g)


def nbwd_mm_nt(dout, y, g, w2, coef, kb, name, gate=None, up=None):
    T, D = dout.shape
    nb = w2.shape[0] // kb
    swiglu = gate is not None
    tm = _tile(T, 1024)

    def body(*refs):
        if swiglu:
            dout_ref, y_ref, g_ref, w_ref, gate_ref, up_ref, dy_ref, dg_ref, da_ref, dys_ref = refs
        else:
            dout_ref, y_ref, g_ref, w_ref, dy_ref, dg_ref, da_ref, dys_ref = refs
        i, b = pl.program_id(0), pl.program_id(1)

        @pl.when((i == 0) & (b == 0))
        def _():
            dg_ref[...] = jnp.zeros_like(dg_ref)

        @pl.when(b == 0)
        def _():
            yh, r = _rms(y_ref[...])
            dyn = coef * dout_ref[...]
            dg_ref[...] += _colsum(dyn * yh)
            dy = _norm_bwd(dyn, yh, r, g_ref[...]).astype(BF)
            dys_ref[...] = dy
            dy_ref[...] = dy

        da = lax.dot_general(dys_ref[...], w_ref[...], NT, preferred_element_type=F32)
        if swiglu:
            da_ref[0] = (da * gate_ref[...].astype(F32)).astype(BF)
            da_ref[1] = (da * up_ref[...].astype(F32)).astype(BF)
        else:
            da_ref[...] = da.astype(BF)

    tok = pl.BlockSpec((tm, D), lambda i, j: (i, 0))
    vec = pl.BlockSpec((1, D), lambda i, j: (0, 0))
    in_specs = [tok, tok, vec, pl.BlockSpec((kb, D), lambda i, j: (j, 0))]
    args = [dout, y, g, w2]
    if swiglu:
        blk = pl.BlockSpec((None, tm, kb), lambda i, j: (j, i, 0))
        in_specs += [blk, blk]
        args += [gate, up]
        da_spec = pl.BlockSpec((2, None, tm, kb), lambda i, j: (0, j, i, 0))
        da_shape = jax.ShapeDtypeStruct((2, nb, T, kb), BF)
    else:
        da_spec = pl.BlockSpec((tm, kb), lambda i, j: (i, j))
        da_shape = jax.ShapeDtypeStruct((T, nb * kb), BF)
    return pl.pallas_call(
        body, name=name, grid=(T // tm, nb), in_specs=in_specs,
        out_specs=[tok, vec, da_spec],
        out_shape=[jax.ShapeDtypeStruct((T, D), BF), jax.ShapeDtypeStruct((1, D), F32), da_shape],
        scratch_shapes=[pltpu.VMEM((tm, D), BF)], compiler_params=_params(2),
    )(*args)


def mm_nt_nbwd(dz, w, h, g, dout, name, tn=None, transposed=False):
    T, D = h.shape
    if w.ndim == 3:
        nb, cs = w.shape[0], w.shape[1 if transposed else 2]
    else:
        cs = tn
        nb = w.shape[1] // cs
    tm = _tile(T, 1024)

    def body(dz_ref, w_ref, h_ref, g_ref, dout_ref, dh_ref, dg_ref, acc_ref):
        i, b = pl.program_id(0), pl.program_id(1)

        @pl.when((i == 0) & (b == 0))
        def _():
            dg_ref[...] = jnp.zeros_like(dg_ref)

        @pl.when(b == 0)
        def _():
            acc_ref[...] = jnp.zeros_like(acc_ref)

        if transposed:
            acc_ref[...] += jnp.dot(dz_ref[...], w_ref[...], preferred_element_type=F32)
        else:
            acc_ref[...] += lax.dot_general(dz_ref[...], w_ref[...], NT, preferred_element_type=F32)

        @pl.when(b == nb - 1)
        def _():
            xh, r = _rms(h_ref[...])
            gg = g_ref[...]
            dxn = acc_ref[...]
            dg_ref[...] += _colsum(dxn * xh)
            dh_ref[...] = dout_ref[...] + _norm_bwd(dxn, xh, r, gg)

    tok = pl.BlockSpec((tm, D), lambda i, j: (i, 0))
    vec = pl.BlockSpec((1, D), lambda i, j: (0, 0))
    return pl.pallas_call(
        body, name=name, grid=(T // tm, nb),
        in_specs=[_x_spec(dz, tm, cs), _w_spec(w, cs)(0), tok, vec, tok],
        out_specs=[tok, vec],
        out_shape=[jax.ShapeDtypeStruct((T, D), F32), jax.ShapeDtypeStruct((1, D), F32)],
        scratch_shapes=[pltpu.VMEM((tm, D), F32)], compiler_params=_params(2),
    )(dz, w, h, g, dout)


def mm_tn(x, y, name, xb=None, yb=None, x_layer=None):
    T = y.shape[-2]
    wide = (yb if yb is not None else y.shape[-1]) > 1024
    tt = _tile(T, 1024 if wide else 2048)
    x_split = (x.ndim == 3 and x_layer is None) or xb is not None
    if x_layer is not None:
        xs = pl.BlockSpec((None, tt, x.shape[2]), lambda b, t: (x_layer, t, 0))
        kdim = x.shape[2]
    elif x.ndim == 3:
        xs = pl.BlockSpec((None, tt, x.shape[2]), lambda b, t: (b, t, 0))
        nb, kdim = x.shape[0], x.shape[2]
    elif xb is not None:
        xs = pl.BlockSpec((tt, xb), lambda b, t: (t, b))
        nb, kdim = x.shape[1] // xb, xb
    else:
        xs = pl.BlockSpec((tt, x.shape[1]), lambda b, t: (t, 0))
        kdim = x.shape[1]
    if x_split:
        ys = pl.BlockSpec((tt, y.shape[1]), lambda b, t: (t, 0))
        ndim = y.shape[1]
        out_spec = pl.BlockSpec((kdim, ndim), lambda b, t: (b, 0))
        out_shape = jax.ShapeDtypeStruct((nb * kdim, ndim), BF)
    else:
        if y.ndim == 3:
            ys = pl.BlockSpec((None, tt, y.shape[2]), lambda b, t: (b, t, 0))
            nb, ndim = y.shape[0], y.shape[2]
        else:
            ys = pl.BlockSpec((tt, yb), lambda b, t: (t, b))
            nb, ndim = y.shape[1] // yb, yb
        out_spec = pl.BlockSpec((None, kdim, ndim), lambda b, t: (b, 0, 0))
        out_shape = jax.ShapeDtypeStruct((nb, kdim, ndim), BF)
    nt = T // tt

    def body(x_ref, y_ref, o_ref, acc_ref):
        t = pl.program_id(1)

        @pl.when(t == 0)
        def _():
            acc_ref[...] = jnp.zeros_like(acc_ref)

        acc_ref[...] += lax.dot_general(x_ref[...].astype(BF), y_ref[...].astype(BF), TN, preferred_element_type=F32)

        @pl.when(t == nt - 1)
        def _():
            o_ref[...] = acc_ref[...].astype(BF)

    return pl.pallas_call(
        body, name=name, grid=(nb, nt), in_specs=[xs, ys], out_specs=out_spec, out_shape=out_shape,
        scratch_shapes=[pltpu.VMEM((kdim, ndim), F32)], compiler_params=_params(2),
    )(x, y)


def ple_fwd(h, gpre, wg, p3, wp, gpost, layer, name):
    T, D = h.shape
    pd = p3.shape[2]
    tm = _tile(T, 512)

    def body(h_ref, gpre_ref, wg_ref, p_ref, wp_ref, gpost_ref, hn_ref, gate_ref, pp_ref):
        x = h_ref[...]
        xh, _ = _rms(x)
        u = jnp.dot((xh * gpre_ref[...]).astype(BF), wg_ref[...], preferred_element_type=F32)
        gate = _sigmoid(u)
        pp = jnp.dot(p_ref[...].astype(BF), wp_ref[...], preferred_element_type=F32)
        yh, _ = _rms(gate * pp)
        hn_ref[...] = x + yh * gpost_ref[...]
        gate_ref[...] = gate.astype(BF)
        pp_ref[...] = pp.astype(BF)

    tok = pl.BlockSpec((tm, D), lambda i: (i, 0))
    vec = pl.BlockSpec((1, D), lambda i: (0, 0))
    return pl.pallas_call(
        body, name=name, grid=(T // tm,),
        in_specs=[tok, vec, pl.BlockSpec((D, D), lambda i: (0, 0)),
                  pl.BlockSpec((None, tm, pd), lambda i: (layer, i, 0)),
                  pl.BlockSpec((pd, D), lambda i: (0, 0)), vec],
        out_specs=[tok, tok, tok],
        out_shape=[jax.ShapeDtypeStruct((T, D), F32), jax.ShapeDtypeStruct((T, D), BF), jax.ShapeDtypeStruct((T, D), BF)],
        compiler_params=_params(1),
    )(h, gpre, wg, p3, wp, gpost)


def ple_bwd(dout, h, gate, pp, gpre, wg, gpost, name):
    T, D = h.shape
    tm = _tile(T, 512)

    def body(dout_ref, h_ref, gate_ref, pp_ref, gpre_ref, wg_ref, gpost_ref, dh_ref, du_ref, dpp_ref, xn_ref, dgpre_ref, dgpost_ref):
        @pl.when(pl.program_id(0) == 0)
        def _():
            dgpre_ref[...] = jnp.zeros_like(dgpre_ref)
            dgpost_ref[...] = jnp.zeros_like(dgpost_ref)

        dout = dout_ref[...]
        gate = gate_ref[...].astype(F32)
        pp = pp_ref[...].astype(F32)
        yh, ry = _rms(gate * pp)
        dgpost_ref[...] += _colsum(dout * yh)
        dy = _norm_bwd(dout, yh, ry, gpost_ref[...])
        dpp_ref[...] = (dy * gate).astype(BF)
        du = (dy * pp * gate * (1.0 - gate)).astype(BF)
        du_ref[...] = du
        dxn = lax.dot_general(du, wg_ref[...], NT, preferred_element_type=F32)
        xh, r = _rms(h_ref[...])
        gp = gpre_ref[...]
        dgpre_ref[...] += _colsum(dxn * xh)
        dh_ref[...] = dout + _norm_bwd(dxn, xh, r, gp)
        xn_ref[...] = (xh * gp).astype(BF)

    tok = pl.BlockSpec((tm, D), lambda i: (i, 0))
    vec = pl.BlockSpec((1, D), lambda i: (0, 0))
    bft = jax.ShapeDtypeStruct((T, D), BF)
    v32 = jax.ShapeDtypeStruct((1, D), F32)
    return pl.pallas_call(
        body, name=name, grid=(T // tm,),
        in_specs=[tok, tok, tok, tok, vec, pl.BlockSpec((D, D), lambda i: (0, 0)), vec],
        out_specs=[tok, tok, tok, tok, vec, vec],
        out_shape=[jax.ShapeDtypeStruct((T, D), F32), bft, bft, bft, v32, v32],
        compiler_params=_params(1),
    )(dout, h, gate, pp, gpre, wg, gpost)


def _chunk_tri(tb, upper):
    r = lax.broadcasted_iota(jnp.int32, (tb, tb), 0)
    c = lax.broadcasted_iota(jnp.int32, (tb, tb), 1)
    shift = HGRN_CHUNK.bit_length() - 1
    same = jnp.right_shift(r, shift) == jnp.right_shift(c, shift)
    return (same & ((c >= r) if upper else (c <= r))).astype(F32)


def _hgrn_gates(z, logits):
    lb = 1.0 / (1.0 + jnp.exp(logits[1:2, :] - logits[0:1, :]))
    e = jnp.exp(-jnp.abs(z))
    inv = 1.0 / (1.0 + e)
    sig = jnp.where(z >= 0, inv, e * inv)
    nsig = jnp.where(z >= 0, e * inv, inv)
    return lb, sig, nsig, lb + (1.0 - lb) * sig


def hgrn_fwd(z, lb_logits, out_norm, name):
    T = z.shape[0]
    W = z.shape[1] // 4
    H = W // HGRN_DK
    C = HGRN_CHUNK
    HB = _tile(H, HGRN_HEADS_PER_STEP)
    tb = _tile(T, 256)
    nch = tb // C

    def body(zq_ref, zf_ref, zv_ref, zg_ref, lbl_ref, on_ref, x_ref, o_ref, st_ref, s_scr, cum_scr, k_scr, v_scr, o_scr):
        @pl.when(pl.program_id(1) == 0)
        def _():
            s_scr[...] = jnp.zeros_like(s_scr)

        lb, sig, nsig, f = _hgrn_gates(zf_ref[...], lbl_ref[...])
        cum = jnp.dot(_chunk_tri(tb, False), jnp.log(f), precision=HI, preferred_element_type=F32)
        kk = (1.0 - lb) * nsig
        for hh in range(HB):
            cols = slice(hh * HGRN_DK, (hh + 1) * HGRN_DK)
            cum_scr[hh] = cum[:, cols]
            k_scr[hh] = kk[:, cols]
            v_scr[hh] = zv_ref[:, cols]
        row = lax.broadcasted_iota(jnp.int32, (C, HGRN_DK), 0)

        def chunk(c, carry):
            r0 = pl.multiple_of(c * C, C)
            rows = pl.ds(r0, C)
            last_row = pl.ds(r0 + C - 1, 1)
            heads = []
            for hh in range(HB):
                cols = slice(hh * HGRN_DK, (hh + 1) * HGRN_DK)
                q, cu = zq_ref[rows, cols], cum_scr[hh, rows, :]
                st = s_scr[hh]
                st_ref[c, hh] = st
                o = lax.dot_general((q * jnp.exp(cu)).astype(BF), st.astype(BF), NT, preferred_element_type=F32)
                last = cum_scr[hh, last_row, :]
                kg = (k_scr[hh, rows, :] * jnp.exp(last - cu)).astype(BF)
                s_scr[hh] = st * jnp.exp(last) + lax.dot_general(v_scr[hh, rows, :].astype(BF), kg, TN, preferred_element_type=F32)
                heads.append((hh, q, cu, o))
            for hh, q, cu, o in heads:
                qr = q.astype(BF).astype(F32)
                low = jnp.zeros((C - HALF, HGRN_DK), F32)
                for s in range(C):
                    one = pl.ds(r0 + s, 1)
                    sl = slice(0 if s < HALF else HALF, C)
                    e = jnp.exp(jnp.minimum(cu[sl] - cum_scr[hh, one, :], 0.0))
                    col = jnp.sum(qr[sl] * (e * k_scr[hh, one, :]).astype(BF).astype(F32), axis=-1, keepdims=True)
                    col = jnp.where(row[sl] >= s, col, 0.0).astype(BF).astype(F32)
                    term = col * v_scr[hh, one, :].astype(BF).astype(F32)
                    if s < HALF:
                        o = o + term
                    else:
                        low = low + term
                o_scr[hh, rows, :] = o
                o_scr[hh, pl.ds(r0 + HALF, C - HALF), :] += low
            return carry

        lax.fori_loop(0, nch, chunk, 0)
        for hh in range(HB):
            cols = slice(hh * HGRN_DK, (hh + 1) * HGRN_DK)
            o = o_scr[hh]
            o_ref[:, cols] = o
            oh, _ = _rms(o)
            g = zg_ref[:, cols]
            x_ref[:, cols] = (oh * on_ref[...] * (g * _sigmoid(g))).astype(BF)

    def zs(part):
        return pl.BlockSpec((tb, HB * HGRN_DK), lambda hd, i: (i, part * (H // HB) + hd))

    blk = pl.BlockSpec((tb, HB * HGRN_DK), lambda hd, i: (i, hd))
    wide = pltpu.VMEM((HB, tb, HGRN_DK), F32)
    return pl.pallas_call(
        body, name=name, grid=(H // HB, T // tb),
        in_specs=[zs(0), zs(1), zs(2), zs(3), pl.BlockSpec((2, HB * HGRN_DK), lambda hd, i: (0, hd)),
                  pl.BlockSpec((1, HGRN_DK), lambda hd, i: (0, 0))],
        out_specs=[blk, blk, pl.BlockSpec((nch, HB, HGRN_DK, HGRN_DK), lambda hd, i: (i, hd, 0, 0))],
        out_shape=[jax.ShapeDtypeStruct((T, W), BF), jax.ShapeDtypeStruct((T, W), F32),
                   jax.ShapeDtypeStruct((T // C, H, HGRN_DK, HGRN_DK), F32)],
        scratch_shapes=[pltpu.VMEM((HB, HGRN_DK, HGRN_DK), F32), wide, wide, wide, wide],
        compiler_params=_params(2),
    )(z, z, z, z, lb_logits, out_norm)


def hgrn_bwd(dx, z, o, states, lb_logits, out_norm, name):
    T = z.shape[0]
    W = z.shape[1] // 4
    H = W // HGRN_DK
    C = HGRN_CHUNK
    HB = _tile(H, HGRN_HEADS_PER_STEP)
    tb = _tile(T, 256)
    nch = tb // C
    nblk = T // tb

    def body(dx_ref, zq_ref, zf_ref, zv_ref, zg_ref, o_ref, st_ref, lbl_ref, on_ref,
             dq_ref, df_ref, dv_ref, dg_ref, dl_ref, don_ref,
             ds_scr, cum_scr, k_scr, v_scr, q_scr, do_scr, dq_scr, dk_scr, dv_scr, dcum_scr):
        hd, i = pl.program_id(0), pl.program_id(1)

        @pl.when(i == 0)
        def _():
            ds_scr[...] = jnp.zeros_like(ds_scr)
            dl_ref[...] = jnp.zeros_like(dl_ref)

        @pl.when((i == 0) & (hd == 0))
        def _():
            don_ref[...] = jnp.zeros_like(don_ref)

        lb, sig, nsig, f = _hgrn_gates(zf_ref[...], lbl_ref[...])
        cum = jnp.dot(_chunk_tri(tb, False), jnp.log(f), precision=HI, preferred_element_type=F32)
        kk = (1.0 - lb) * nsig
        w = on_ref[...]
        for hh in range(HB):
            cols = slice(hh * HGRN_DK, (hh + 1) * HGRN_DK)
            cum_scr[hh] = cum[:, cols]
            k_scr[hh] = kk[:, cols]
            v_scr[hh] = zv_ref[:, cols]
            q_scr[hh] = zq_ref[:, cols]
            oh, r = _rms(o_ref[:, cols])
            g = zg_ref[:, cols]
            sg = _sigmoid(g)
            dxv = dx_ref[:, cols].astype(F32)
            dg_ref[:, cols] = (dxv * (oh * w) * (sg * (1.0 + g * (1.0 - sg)))).astype(BF)
            don = dxv * (g * sg)
            don_ref[...] += _colsum(don * oh)
            do_scr[hh] = _norm_bwd(don, oh, r, w)
        row = lax.broadcasted_iota(jnp.int32, (C, HGRN_DK), 0)

        def chunk(cc, carry):
            c = nch - 1 - cc
            r0 = pl.multiple_of(c * C, C)
            rows = pl.ds(r0, C)
            last_row = pl.ds(r0 + C - 1, 1)
            heads = []
            for hh in range(HB):
                q, k, v, cu, do = q_scr[hh, rows, :], k_scr[hh, rows, :], v_scr[hh, rows, :], cum_scr[hh, rows, :], do_scr[hh, rows, :]
                st = st_ref[c, hh]
                dst = ds_scr[hh]
                last = cum_scr[hh, last_row, :]
                lam, gam, elast = jnp.exp(cu), jnp.exp(last - cu), jnp.exp(last)
                dob, dstb = do.astype(BF), dst.astype(BF)
                dq = jnp.dot(dob, st.astype(BF), preferred_element_type=F32) * lam
                dv = lax.dot_general((k * gam).astype(BF), dstb, NT, preferred_element_type=F32)
                dk = jnp.dot(v.astype(BF), dstb, preferred_element_type=F32) * gam
                dlast = elast * _colsum(dst * st) + _colsum(dk * k)
                ds_scr[hh] = dst * elast + lax.dot_general(dob, (q * lam).astype(BF), TN, preferred_element_type=F32)
                heads.append((hh, q, k, cu, do, dq, dk, dv, dlast))
            for hh, q, k, cu, do, dq, dk, dv, dlast in heads:
                for first in (True, False):
                    sl = slice(0 if first else HALF, C)
                    qs, cus, dos, rws = q[sl], cu[sl], do[sl], row[sl]
                    dqs, dks, dvs = dq[sl], dk[sl], dv[sl]
                    for s in (range(HALF) if first else range(HALF, C)):
                        one = pl.ds(r0 + s, 1)
                        e = jnp.where(rws >= s, jnp.exp(jnp.minimum(cus - cum_scr[hh, one, :], 0.0)), 0.0)
                        ks = k_scr[hh, one, :]
                        da = jnp.sum(dos * v_scr[hh, one, :], axis=-1, keepdims=True)
                        pq = qs * e
                        a = jnp.sum(pq * ks, axis=-1, keepdims=True)
                        dqs = dqs + da * e * ks
                        dks = jnp.where(rws == s, dks + _colsum(da * pq), dks)
                        dvs = jnp.where(rws == s, dvs + _colsum(a * dos), dvs)
                    if first:
                        dq, dk, dv = dqs, dks, dvs
                        top = pl.ds(r0, HALF)
                        dq_scr[hh, top, :] = dq[:HALF]
                        dk_scr[hh, top, :] = dk[:HALF]
                        dv_scr[hh, top, :] = dv[:HALF]
                        dcum_scr[hh, top, :] = q[:HALF] * dq[:HALF] - k[:HALF] * dk[:HALF]
                    else:
                        low = pl.ds(r0 + HALF, C - HALF)
                        dq_scr[hh, low, :] = dqs
                        dk_scr[hh, low, :] = dks
                        dv_scr[hh, low, :] = dvs
                        dcum_scr[hh, low, :] = qs * dqs - k[sl] * dks + jnp.where(rws == C - 1, dlast, 0.0)
            return carry

        lax.fori_loop(0, nch, chunk, 0)
        tri = _chunk_tri(tb, True)
        for hh in range(HB):
            cols = slice(hh * HGRN_DK, (hh + 1) * HGRN_DK)
            dlf = jnp.dot(tri, dcum_scr[hh], precision=HI, preferred_element_type=F32)
            dk = dk_scr[hh]
            lbh, sigh, nsigh, fh = lb[:, cols], sig[:, cols], nsig[:, cols], f[:, cols]
            common = (1.0 - lbh) * sigh * nsigh
            df_ref[:, cols] = (dlf * common / fh - dk * common).astype(BF)
            dq_ref[:, cols] = dq_scr[hh].astype(BF)
            dv_ref[:, cols] = dv_scr[hh].astype(BF)
            dl0 = _colsum(dlf * nsigh / fh - dk * nsigh) * lbh * (1.0 - lbh)
            dl_ref[:, cols] += jnp.where(lax.broadcasted_iota(jnp.int32, (2, HGRN_DK), 0) == 0, dl0, -dl0)

    def zs(part):
        return pl.BlockSpec((tb, HB * HGRN_DK), lambda hd, i: (nblk - 1 - i, part * (H // HB) + hd))

    blk = pl.BlockSpec((tb, HB * HGRN_DK), lambda hd, i: (nblk - 1 - i, hd))
    bft = jax.ShapeDtypeStruct((T, W), BF)
    scr = pltpu.VMEM((HB, tb, HGRN_DK), F32)
    return pl.pallas_call(
        body, name=name, grid=(H // HB, nblk),
        in_specs=[blk, zs(0), zs(1), zs(2), zs(3), blk,
                  pl.BlockSpec((nch, HB, HGRN_DK, HGRN_DK), lambda hd, i: (nblk - 1 - i, hd, 0, 0)),
                  pl.BlockSpec((2, HB * HGRN_DK), lambda hd, i: (0, hd)), pl.BlockSpec((1, HGRN_DK), lambda hd, i: (0, 0))],
        out_specs=[blk, blk, blk, blk, pl.BlockSpec((2, HB * HGRN_DK), lambda hd, i: (0, hd)),
                   pl.BlockSpec((1, HGRN_DK), lambda hd, i: (0, 0))],
        out_shape=[bft, bft, bft, bft, jax.ShapeDtypeStruct((2, W), F32), jax.ShapeDtypeStruct((1, HGRN_DK), F32)],
        scratch_shapes=[pltpu.VMEM((HB, HGRN_DK, HGRN_DK), F32), scr, scr, scr, scr, scr, scr, scr, scr, scr],
        compiler_params=_params(2),
    )(dx, z, z, z, z, o, states, lb_logits, out_norm)


def _log_sigmoid(x):
    return jnp.minimum(x, 0.0) - jnp.log(1.0 + jnp.exp(-jnp.abs(x)))


def _tri(n, upper):
    r = lax.broadcasted_iota(jnp.int32, (n, n), 0)
    c = lax.broadcasted_iota(jnp.int32, (n, n), 1)
    return ((r >= c) if upper else (c <= r)).astype(F32)


def fox_cum_fwd(kvf, fcol, b_row, name):
    T = kvf.shape[0]
    tb = _tile(T, 512)

    def body(x_ref, b_ref, ct_ref, carry_ref):
        @pl.when(pl.program_id(0) == 0)
        def _():
            carry_ref[...] = jnp.zeros_like(carry_ref)

        lf = _log_sigmoid(x_ref[...] + b_ref[...])
        cum = jnp.dot(_tri(tb, False), lf, precision=HI, preferred_element_type=F32) + carry_ref[...]
        carry_ref[...] += _colsum(lf)
        ct_ref[...] = cum.T

    return pl.pallas_call(
        body, name=name, grid=(T // tb,),
        in_specs=[pl.BlockSpec((tb, LANES), lambda i: (i, fcol)), pl.BlockSpec((1, LANES), lambda i: (0, 0))],
        out_specs=pl.BlockSpec((LANES, tb), lambda i: (0, i)), out_shape=jax.ShapeDtypeStruct((LANES, T), F32),
        scratch_shapes=[pltpu.VMEM((1, LANES), F32)], compiler_params=_params(1),
    )(kvf, b_row)


def fox_cum_bwd(dct, kvf, fcol, b_row, name):
    T = kvf.shape[0]
    tb = _tile(T, 512)
    nblk = T // tb

    def body(dc_ref, x_ref, b_ref, df_ref, db_ref, carry_ref):
        @pl.when(pl.program_id(0) == 0)
        def _():
            carry_ref[...] = jnp.zeros_like(carry_ref)
            db_ref[...] = jnp.zeros_like(db_ref)

        dc = dc_ref[...]
        dlf_t = jnp.dot(dc, _tri(tb, True), precision=HI, preferred_element_type=F32) + carry_ref[...]
        carry_ref[...] += jnp.sum(dc, axis=1, keepdims=True)
        x = x_ref[...] + b_ref[...]
        df = dlf_t.T * _sigmoid(-x)
        df_ref[...] = df.astype(BF)
        db_ref[...] += _colsum(df)

    return pl.pallas_call(
        body, name=name, grid=(nblk,),
        in_specs=[pl.BlockSpec((LANES, tb), lambda i: (0, nblk - 1 - i)),
                  pl.BlockSpec((tb, LANES), lambda i: (nblk - 1 - i, fcol)), pl.BlockSpec((1, LANES), lambda i: (0, 0))],
        out_specs=[pl.BlockSpec((tb, LANES), lambda i: (nblk - 1 - i, 0)), pl.BlockSpec((1, LANES), lambda i: (0, 0))],
        out_shape=[jax.ShapeDtypeStruct((T, LANES), BF), jax.ShapeDtypeStruct((1, LANES), F32)],
        scratch_shapes=[pltpu.VMEM((LANES, 1), F32)], compiler_params=_params(1),
    )(dct, kvf, b_row)


NAUG = 3


def fox_prep(kvf, ct, n_fox, tk):
    T = kvf.shape[0]
    W = n_fox * FOX_HEAD_DIM
    NP = n_fox // 2
    k = kvf[:, :W].astype(BF).reshape(T, NP, 2, FOX_HEAD_DIM)
    c = ct[:n_fox].T.reshape(T, NP, 2)
    hi = lax.reduce_precision(c, 8, 7)
    mid = lax.reduce_precision(c - hi, 8, 7)
    lo = c - hi - mid
    aug = jnp.stack([hi, mid, lo], axis=-1).astype(BF)
    pad = jnp.zeros((T, NP, FOX_HEAD_DIM - NAUG), BF)
    ka = jnp.concatenate([k[:, :, 0], aug[:, :, 0], pad], axis=-1).reshape(T, W)
    kb = jnp.concatenate([aug[:, :, 1], pad, k[:, :, 1]], axis=-1).reshape(T, W)
    v = kvf[:, W:2 * W].astype(BF)
    vt3 = v.reshape(T // tk, tk, NP, LANES).transpose(2, 0, 3, 1)
    return ka, kb, vt3


def fox_fwd(qg, ka, kb, vt3, name):
    T = qg.shape[0]
    W = qg.shape[1] // 2
    NP = W // LANES
    tq = tk = vt3.shape[3]
    scale = FOX_HEAD_DIM ** -0.5
    nk = T // tk
    HD = FOX_HEAD_DIM

    def body(q_ref, g_ref, ka_ref, kb_ref, vt_ref, x_ref, o_ref, lse_ref):
        i = pl.program_id(1)
        lane = lax.broadcasted_iota(jnp.int32, (tq, LANES), 1)
        q2 = q_ref[...] * scale
        qa = jnp.where(lane < HD, q2, jnp.where(lane < HD + NAUG, -1.0, 0.0))
        qb = jnp.where(lane >= HD, q2, jnp.where(lane < NAUG, -1.0, 0.0))
        qts = (qa.T.astype(BF), qb.T.astype(BF))
        krow = lax.broadcasted_iota(jnp.int32, (tk, tq), 0)
        qcol = lax.broadcasted_iota(jnp.int32, (tk, tq), 1)

        def step(j, carry, diag):
            rows = pl.ds(pl.multiple_of(j * tk, tk), tk)
            ks = (ka_ref[rows, :], kb_ref[rows, :])
            vt = vt_ref[j]
            sts = [jnp.dot(ks[a], qts[a], preferred_element_type=F32) for a in range(2)]
            pts, mls = [], []
            for a in range(2):
                m, l, _ = carry[a]
                st = sts[a]
                if diag:
                    st = jnp.where(krow + (j * tk - i * tq) <= qcol, st, -1e30)
                mn = jnp.maximum(m, jnp.max(st, axis=0, keepdims=True))
                alpha = jnp.exp(m - mn)
                pt = jnp.exp(st - mn)
                mls.append((mn, l * alpha + jnp.sum(pt, axis=0, keepdims=True), alpha))
                pts.append(pt.astype(BF))
            out = []
            for a in range(2):
                mn, l, alpha = mls[a]
                acc = carry[a][2] * alpha + jnp.dot(vt[a * HD:(a + 1) * HD, :], pts[a], preferred_element_type=F32)
                out.append((mn, l, acc))
            return tuple(out)

        init = (jnp.full((1, tq), -1e30, F32), jnp.zeros((1, tq), F32), jnp.zeros((HD, tq), F32))
        r = tq // tk
        carry = lax.fori_loop(0, i * r, lambda j, c: step(j, c, False), (init, init))
        for u in range(r):
            carry = step(i * r + u, carry, True)
        (ma, la, acca), (mb, lb, accb) = carry
        ot = jnp.concatenate([acca / la, accb / lb], axis=0)
        o = ot.T
        o_ref[...] = o
        lse_ref[0:1, :] = ma + jnp.log(la)
        lse_ref[1:2, :] = mb + jnp.log(lb)
        x_ref[...] = (o * _sigmoid(g_ref[...])).astype(BF)

    blk = pl.BlockSpec((tq, LANES), lambda hp, i: (i, hp))
    full = pl.BlockSpec((T, LANES), lambda hp, i: (0, hp))
    return pl.pallas_call(
        body, name=name, grid=(NP, T // tq),
        in_specs=[blk, pl.BlockSpec((tq, LANES), lambda hp, i: (i, NP + hp)), full, full,
                  pl.BlockSpec((None, nk, LANES, tk), lambda hp, i: (hp, 0, 0, 0))],
        out_specs=[blk, blk, pl.BlockSpec((None, None, 2, tq), lambda hp, i: (hp, i, 0, 0))],
        out_shape=[jax.ShapeDtypeStruct((T, W), BF), jax.ShapeDtypeStruct((T, W), F32),
                   jax.ShapeDtypeStruct((NP, T // tq, 2, tq), F32)],
        compiler_params=_params(2),
    )(qg, qg, ka, kb, vt3)


def fox_gate_bwd(dx, o, qg, name):
    T, W = o.shape
    tm = _tile(T, 512)

    def body(dx_ref, o_ref, g_ref, do_ref, dg_ref, ds_ref):
        dxv = dx_ref[...].astype(F32)
        sg = _sigmoid(g_ref[...])
        do = dxv * sg
        o = o_ref[...]
        do_ref[...] = do
        dg_ref[...] = (dxv * o * sg * (1.0 - sg)).astype(BF)
        head = jnp.right_shift(lax.broadcasted_iota(jnp.int32, (W, LANES), 0), FOX_HEAD_DIM.bit_length() - 1)
        sel = (head == lax.broadcasted_iota(jnp.int32, (W, LANES), 1)).astype(F32)
        ds_ref[...] = jnp.dot(do * o, sel, precision=HI, preferred_element_type=F32)

    tok = pl.BlockSpec((tm, W), lambda i: (i, 0))
    return pl.pallas_call(
        body, name=name, grid=(T // tm,),
        in_specs=[tok, tok, pl.BlockSpec((tm, W), lambda i: (i, 1))],
        out_specs=[tok, tok, pl.BlockSpec((tm, LANES), lambda i: (i, 0))],
        out_shape=[jax.ShapeDtypeStruct((T, W), F32), jax.ShapeDtypeStruct((T, W), BF), jax.ShapeDtypeStruct((T, LANES), F32)],
        compiler_params=_params(1),
    )(dx, o, qg)


def fox_bwd_prep(qg, kvf, do, tq):
    T = qg.shape[0]
    W = qg.shape[1] // 2
    NP = W // LANES
    scale = FOX_HEAD_DIM ** -0.5
    tr3 = lambda a: a.reshape(T // tq, tq, NP, LANES).transpose(2, 0, 3, 1)
    q = (qg[:, :W] * scale).astype(BF)
    dob = do.astype(BF)
    k = kvf[:, :W]
    return q, tr3(q), dob, tr3(dob), tr3((k * scale).astype(BF)), kvf[:, W:2 * W].astype(BF)


def fox_bwd(q, qt3, dob, dot3, kt3, v, ka, kb, lse4, dsum4, name):
    T, W = q.shape
    NP = W // LANES
    tq = tk = qt3.shape[3]
    nq = T // tq
    HD = FOX_HEAD_DIM

    def body(q_ref, qt_ref, do_ref, dot_ref, kt_ref, v_ref, ka_ref, kb_ref, lse_ref, dsum_ref,
             dqt_ref, dk_ref, dv_ref, dc_ref, dcq_ref, dk_scr, dv_scr, dcl_scr):
        j = pl.program_id(1)

        @pl.when(j == 0)
        def _():
            dqt_ref[...] = jnp.zeros_like(dqt_ref)
            dcq_ref[...] = jnp.zeros_like(dcq_ref)

        dk_scr[...] = jnp.zeros_like(dk_scr)
        dv_scr[...] = jnp.zeros_like(dv_scr)
        dcl_scr[...] = jnp.zeros_like(dcl_scr)
        lane = lax.broadcasted_iota(jnp.int32, (tk, LANES), 1)
        srow = lax.broadcasted_iota(jnp.int32, (LANES, tq), 0)
        lanes_of = (lane < HD, lane >= HD)
        v2 = v_ref[...]
        kt2 = kt_ref[...]
        zero = jnp.zeros((), BF)
        vs = [jnp.where(lanes_of[a], v2, zero) for a in range(2)]
        kts = [kt2[a * HD:(a + 1) * HD, :] for a in range(2)]
        kaug = (ka_ref[...], kb_ref[...])
        krow = lax.broadcasted_iota(jnp.int32, (tk, tq), 0)
        qcol = lax.broadcasted_iota(jnp.int32, (tk, tq), 1)
        neg1 = jnp.full((), -1.0, BF)

        def step(i, carry, diag):
            rows = pl.ds(pl.multiple_of(i * tq, tq), tq)
            qt2 = qt_ref[i]
            dot2 = dot_ref[i]
            q2 = q_ref[rows, :]
            do2 = do_ref[rows, :]
            qts = [jnp.where(srow < HD, qt2, jnp.where(srow < HD + NAUG, neg1, zero)),
                   jnp.where(srow >= HD, qt2, jnp.where(srow < NAUG, neg1, zero))]
            sts = [jnp.dot(kaug[a], qts[a], preferred_element_type=F32) for a in range(2)]
            dps = [jnp.dot(vs[a], dot2, preferred_element_type=F32) for a in range(2)]
            pbs, dsbs = [], []
            for a in range(2):
                pt = jnp.exp(sts[a] - lse_ref[i, a:a + 1, :])
                if diag:
                    pt = jnp.where(krow <= qcol, pt, 0.0)
                ds = pt * (dps[a] - dsum_ref[i, a:a + 1, :])
                dcq_ref[i, a:a + 1, :] += _colsum(ds)
                part = ds[:, 0:LANES]
                for u in range(1, tq // LANES):
                    part = part + ds[:, u * LANES:(u + 1) * LANES]
                dcl_scr[a] += part
                pbs.append(pt.astype(BF))
                dsbs.append(ds.astype(BF))
            qn = [jnp.where(lanes_of[a], q2, zero) for a in range(2)]
            don = [jnp.where(lanes_of[a], do2, zero) for a in range(2)]
            dv_scr[...] += (jnp.dot(pbs[0], don[0], preferred_element_type=F32) +
                            jnp.dot(pbs[1], don[1], preferred_element_type=F32))
            dk_scr[...] += (jnp.dot(dsbs[0], qn[0], preferred_element_type=F32) +
                            jnp.dot(dsbs[1], qn[1], preferred_element_type=F32))
            for a in range(2):
                dqt_ref[i, a * HD:(a + 1) * HD, :] += jnp.dot(kts[a], dsbs[a], preferred_element_type=F32)
            return carry

        step(j, 0, True)
        lax.fori_loop(j + 1, nq, lambda i, c: step(i, c, False), 0)
        dk_ref[...] = dk_scr[...].astype(BF)
        dv_ref[...] = dv_scr[...].astype(BF)
        for a in range(2):
            dc_ref[a:a + 1, :] = -_colsum(dcl_scr[a].T)

    tile = pl.BlockSpec((tk, LANES), lambda hp, j: (j, hp))
    full = pl.BlockSpec((T, LANES), lambda hp, j: (0, hp))
    full3 = pl.BlockSpec((None, nq, LANES, tq), lambda hp, j: (hp, 0, 0, 0))
    rows4 = pl.BlockSpec((None, nq, 2, tq), lambda hp, j: (hp, 0, 0, 0))
    bft = jax.ShapeDtypeStruct((T, W), BF)
    r4 = jax.ShapeDtypeStruct((NP, nq, 2, tq), F32)
    return pl.pallas_call(
        body, name=name, grid=(NP, nq),
        in_specs=[full, full3, full, full3, pl.BlockSpec((None, None, LANES, tk), lambda hp, j: (hp, j, 0, 0)),
                  tile, tile, tile, rows4, rows4],
        out_specs=[full3, tile, tile, pl.BlockSpec((None, None, 2, tk), lambda hp, j: (hp, j, 0, 0)), rows4],
        out_shape=[jax.ShapeDtypeStruct((NP, nq, LANES, tq), F32), bft, bft, r4, r4],
        scratch_shapes=[pltpu.VMEM((tk, LANES), F32), pltpu.VMEM((tk, LANES), F32), pltpu.VMEM((2, tk, LANES), F32)],
        compiler_params=_params(2),
    )(q, qt3, dob, dot3, kt3, v, ka, kb, lse4, dsum4)


def loss_fwd_bwd(y, target, name):
    T, D = y.shape
    tm = _tile(T, 512)

    def body(y_ref, t_ref, dy_ref, l_ref):
        @pl.when(pl.program_id(0) == 0)
        def _():
            l_ref[...] = jnp.zeros_like(l_ref)

        d = y_ref[...] - t_ref[...]
        dy_ref[...] = d * (1.0 / D)
        l_ref[...] += 0.5 * jnp.sum(jnp.mean(d * d, axis=-1, keepdims=True), axis=0, keepdims=True)

    tok = pl.BlockSpec((tm, D), lambda i: (i, 0))
    return pl.pallas_call(
        body, name=name, grid=(T // tm,), in_specs=[tok, tok],
        out_specs=[tok, pl.BlockSpec((1, 1), lambda i: (0, 0))],
        out_shape=[jax.ShapeDtypeStruct((T, D), F32), jax.ShapeDtypeStruct((1, 1), F32)], compiler_params=_params(1),
    )(y, target)


def adamw(parts, w, m, v, layer, prev, name):
    L, R, C = w.shape
    tr = _tile(R, 256)
    c1 = 1.0 / (1.0 - ADAM_B1 ** ADAM_STEP)
    c2 = 1.0 / (1.0 - ADAM_B2 ** ADAM_STEP)

    def body(p_ref, w_ref, m_ref, v_ref, *rest):
        g_ref, d_ref, mo_ref, vo_ref = rest[-4:]
        g = p_ref[0].astype(F32)
        for d in range(1, N_DEV):
            g = g + p_ref[d].astype(F32)
        mn = ADAM_B1 * m_ref[...] + (1.0 - ADAM_B1) * g
        vn = ADAM_B2 * v_ref[...] + (1.0 - ADAM_B2) * (g * g)
        g_ref[...] = g
        mo_ref[...] = mn
        vo_ref[...] = vn
        d_ref[...] = -ADAM_LR * ((mn * c1) / (jnp.sqrt(vn * c2) + ADAM_EPS) + ADAM_WD * w_ref[...])

    blk = pl.BlockSpec((None, tr, C), lambda i: (layer, i, 0))
    shp = jax.ShapeDtypeStruct((L, R, C), F32)
    in_specs = [pl.BlockSpec((N_DEV, tr, C), lambda i: (0, i, 0)), blk, blk, blk]
    args = [parts, w, m, v]
    aliases = {}
    if prev is not None:
        in_specs += [pl.BlockSpec(memory_space=pl.ANY)] * 4
        args += list(prev)
        aliases = {4 + j: j for j in range(4)}
    return pl.pallas_call(
        body, name=name, grid=(R // tr,), in_specs=in_specs, out_specs=[blk, blk, blk, blk],
        out_shape=[shp, shp, shp, shp], input_output_aliases=aliases, compiler_params=_params(1),
    )(*args)


HBM_SPEC = pl.BlockSpec(memory_space=pltpu.HBM)
SEM_SPEC = pl.BlockSpec(memory_space=pltpu.SEMAPHORE)
EFFECT = pltpu.SideEffectType.DATAFLOW_SIDE_EFFECTING


def _mesh_pos():
    return lax.axis_index("x"), lax.axis_index("y"), lax.axis_index("c")


def _flip(v, bit):
    return v + bit - 2 * v * bit


def _peer(pos, delta):
    x, y, c = pos
    px, py, pc = _flip(x, (delta >> 2) & 1), _flip(y, (delta >> 1) & 1), _flip(c, delta & 1)
    return (px, py, pc), 4 * px + 2 * py + pc


def _me():
    x, y, c = _mesh_pos()
    return 4 * x + 2 * y + c


def _copies(src_refs, land_refs, whole, send, recv, incoming):
    pos = _mesh_pos()
    me = 4 * pos[0] + 2 * pos[1] + pos[2]
    out = []
    for k in range(len(src_refs)):
        for d in range(1, N_DEV):
            dev, idx = _peer(pos, d)
            j = k * (N_DEV - 1) + d - 1
            src = src_refs[k] if whole[k] else src_refs[k].at[idx]
            out.append(pltpu.make_async_remote_copy(
                src_ref=src, dst_ref=land_refs[k].at[idx if incoming else me], send_sem=send.at[j], recv_sem=recv.at[j],
                device_id=dev, device_id_type=pl.DeviceIdType.MESH))
    return out


def exchange_start(srcs, lands, whole, name):
    n = len(srcs)

    def body(*refs):
        for copy in _copies(refs[:n], refs[n:2 * n], whole, refs[2 * n], refs[2 * n + 1], False):
            copy.start()
        refs[-1][...] = jnp.zeros_like(refs[-1])

    sems = pltpu.SemaphoreType.DMA((n * (N_DEV - 1),))
    thru = [pltpu.HBM(a.shape, a.dtype) for a in list(srcs) + list(lands)]
    res = pl.pallas_call(
        body, name=name, in_specs=[HBM_SPEC] * (2 * n),
        out_specs=[SEM_SPEC, SEM_SPEC] + [HBM_SPEC] * (2 * n) + [pl.BlockSpec(memory_space=pltpu.VMEM)],
        out_shape=[sems, sems] + thru + [jax.ShapeDtypeStruct((8, LANES), F32)],
        input_output_aliases={j: 2 + j for j in range(2 * n)},
        compiler_params=pltpu.CompilerParams(has_side_effects=EFFECT),
    )(*[pltpu.with_memory_space_constraint(a, pltpu.HBM) for a in list(srcs) + list(lands)])
    return dict(send=res[0], recv=res[1], srcs=res[2:2 + n], lands=res[2 + n:2 + 2 * n], whole=whole, token=res[-1])


def exchange_wait(handle, after, name):
    n = len(handle["srcs"])
    whole = handle["whole"]

    def body(*refs):
        for copy in _copies(refs[:n], refs[n:2 * n], whole, refs[2 * n], refs[2 * n + 1], False):
            copy.wait_send()
        for copy in _copies(refs[:n], refs[n:2 * n], whole, refs[2 * n], refs[2 * n + 1], True):
            copy.wait_recv()

    bufs = list(handle["srcs"]) + list(handle["lands"])
    res = pl.pallas_call(
        body, name=name, in_specs=[HBM_SPEC] * (2 * n) + [SEM_SPEC, SEM_SPEC] + [pl.BlockSpec(memory_space=pl.ANY)] * len(after),
        out_specs=[HBM_SPEC] * (2 * n), out_shape=[pltpu.HBM(a.shape, a.dtype) for a in bufs],
        input_output_aliases={j: j for j in range(2 * n)},
        compiler_params=pltpu.CompilerParams(has_side_effects=EFFECT),
    )(*bufs, handle["send"], handle["recv"], *after)
    return list(res[n:])


def _landing(own, whole):
    me = _me()
    if not whole:
        own = lax.dynamic_index_in_dim(own, me, 0, keepdims=False)
    buf = lax.empty((N_DEV,) + own.shape, own.dtype)
    return lax.dynamic_update_slice(buf, own[None], (me,) + (0,) * own.ndim)


def _row(v):
    return v.reshape(1, -1)


def _pad_lanes(v, n):
    return jnp.pad(v, ((0, 0), (0, n - v.shape[1])))


TRANSPOSED = ("ffn1_w_in", "ffn2_w_in")

GATHER_GROUPS = (
    ("ffn1_in_0", (("ffn1_w_in", 0),)),
    ("ffn1_out_0", (("ffn1_w_out", 0),)),
    ("hgrn", (("hgrn_w_in", 0), ("hgrn_w_out", 0))),
    ("rest_0", (("ffn2_w_in", 0), ("ffn2_w_out", 0), ("ple_w_gate", 0), ("ple_w_proj", 0), ("fox_w_kvf", 0))),
    ("ffn1_1", (("ffn1_w_in", 1), ("ffn1_w_out", 1))),
    ("fox", (("fox_w_qg", 0), ("fox_w_out", 0))),
    ("rest_1", (("ffn2_w_in", 1), ("ffn2_w_out", 1), ("ple_w_gate", 1), ("ple_w_proj", 1))),
)


def kernel(x, p, ffn1_norm_pre, ffn1_w_in, ffn1_w_out, ffn1_norm_post, mix_norm_pre, mix_norm_post, ffn2_norm_pre, ffn2_w_in, ffn2_w_out, ffn2_norm_post, hgrn_w_in, hgrn_lb_logits, hgrn_out_norm, hgrn_w_out, kv_norm, fox_w_kvf, fox_b_f, fox_w_qg, fox_w_out, ple_norm_pre, ple_w_gate, ple_w_proj, ple_norm_post, loss_target, m_ffn1_norm_pre, m_ffn1_w_in, m_ffn1_w_out, m_ffn1_norm_post, m_mix_norm_pre, m_mix_norm_post, m_ffn2_norm_pre, m_ffn2_w_in, m_ffn2_w_out, m_ffn2_norm_post, m_hgrn_w_in, m_hgrn_lb_logits, m_hgrn_out_norm, m_hgrn_w_out, m_kv_norm, m_fox_w_kvf, m_fox_b_f, m_fox_w_qg, m_fox_w_out, m_ple_norm_pre, m_ple_w_gate, m_ple_w_proj, m_ple_norm_post, v_ffn1_norm_pre, v_ffn1_w_in, v_ffn1_w_out, v_ffn1_norm_post, v_mix_norm_pre, v_mix_norm_post, v_ffn2_norm_pre, v_ffn2_w_in, v_ffn2_w_out, v_ffn2_norm_post, v_hgrn_w_in, v_hgrn_lb_logits, v_hgrn_out_norm, v_hgrn_w_out, v_kv_norm, v_fox_w_kvf, v_fox_b_f, v_fox_w_qg, v_fox_w_out, v_ple_norm_pre, v_ple_w_gate, v_ple_w_proj, v_ple_norm_post):
    weights = dict(ffn1_norm_pre=ffn1_norm_pre, ffn1_w_in=ffn1_w_in, ffn1_w_out=ffn1_w_out, ffn1_norm_post=ffn1_norm_post, mix_norm_pre=mix_norm_pre, mix_norm_post=mix_norm_post, ffn2_norm_pre=ffn2_norm_pre, ffn2_w_in=ffn2_w_in, ffn2_w_out=ffn2_w_out, ffn2_norm_post=ffn2_norm_post, hgrn_w_in=hgrn_w_in, hgrn_lb_logits=hgrn_lb_logits, hgrn_out_norm=hgrn_out_norm, hgrn_w_out=hgrn_w_out, kv_norm=kv_norm, fox_w_kvf=fox_w_kvf, fox_b_f=fox_b_f, fox_w_qg=fox_w_qg, fox_w_out=fox_w_out, ple_norm_pre=ple_norm_pre, ple_w_gate=ple_w_gate, ple_w_proj=ple_w_proj, ple_norm_post=ple_norm_post)
    mom1 = dict(ffn1_norm_pre=m_ffn1_norm_pre, ffn1_w_in=m_ffn1_w_in, ffn1_w_out=m_ffn1_w_out, ffn1_norm_post=m_ffn1_norm_post, mix_norm_pre=m_mix_norm_pre, mix_norm_post=m_mix_norm_post, ffn2_norm_pre=m_ffn2_norm_pre, ffn2_w_in=m_ffn2_w_in, ffn2_w_out=m_ffn2_w_out, ffn2_norm_post=m_ffn2_norm_post, hgrn_w_in=m_hgrn_w_in, hgrn_lb_logits=m_hgrn_lb_logits, hgrn_out_norm=m_hgrn_out_norm, hgrn_w_out=m_hgrn_w_out, kv_norm=m_kv_norm, fox_w_kvf=m_fox_w_kvf, fox_b_f=m_fox_b_f, fox_w_qg=m_fox_w_qg, fox_w_out=m_fox_w_out, ple_norm_pre=m_ple_norm_pre, ple_w_gate=m_ple_w_gate, ple_w_proj=m_ple_w_proj, ple_norm_post=m_ple_norm_post)
    mom2 = dict(ffn1_norm_pre=v_ffn1_norm_pre, ffn1_w_in=v_ffn1_w_in, ffn1_w_out=v_ffn1_w_out, ffn1_norm_post=v_ffn1_norm_post, mix_norm_pre=v_mix_norm_pre, mix_norm_post=v_mix_norm_post, ffn2_norm_pre=v_ffn2_norm_pre, ffn2_w_in=v_ffn2_w_in, ffn2_w_out=v_ffn2_w_out, ffn2_norm_post=v_ffn2_norm_post, hgrn_w_in=v_hgrn_w_in, hgrn_lb_logits=v_hgrn_lb_logits, hgrn_out_norm=v_hgrn_out_norm, hgrn_w_out=v_hgrn_w_out, kv_norm=v_kv_norm, fox_w_kvf=v_fox_w_kvf, fox_b_f=v_fox_b_f, fox_w_qg=v_fox_w_qg, fox_w_out=v_fox_w_out, ple_norm_pre=v_ple_norm_pre, ple_w_gate=v_ple_w_gate, ple_w_proj=v_ple_w_proj, ple_norm_post=v_ple_norm_post)
    names = list(weights)
    big = ["ffn1_w_in", "ffn1_w_out", "ffn2_w_in", "ffn2_w_out", "hgrn_w_in", "hgrn_w_out", "fox_w_kvf", "fox_w_qg",
           "fox_w_out", "ple_w_gate", "ple_w_proj"]
    small_names = [n for n in names if n not in big]

    T, D = x.shape[1], x.shape[2]
    depth = p.shape[0]
    h0 = x.reshape(T, D)
    target = loss_target.reshape(T, D)
    p3 = p.reshape(depth, T, p.shape[3])
    n_fox = fox_b_f.shape[0]
    fox_w = n_fox * FOX_HEAD_DIM
    fcol = 2 * fox_w // LANES
    b_row = _pad_lanes(_row(fox_b_f), LANES)

    tok = jnp.zeros((), F32)
    handles = {}
    for gname, keys in GATHER_GROUPS:
        shards = []
        for n, l in keys:
            w = weights[n]
            w = w[l] if w.ndim == 3 else w
            shards.append(((w.T if n in TRANSPOSED else w) + tok).astype(BF))
        handles[gname] = exchange_start(shards, [_landing(s, True) for s in shards], [True] * len(keys), f"gather_start_{gname}")
        tok = handles[gname]["token"][0, 0]
    W = {}

    def arrive(gname, after):
        lands = exchange_wait(handles[gname], after, f"gather_wait_{gname}")
        W.update(dict(zip(dict(GATHER_GROUPS)[gname], lands)))

    def w_rows(n, l):
        return W[n, l].reshape(-1, D)

    norm = lambda name, i: weights[name][i:i + 1]

    saved = []
    h = h0
    kvf = ct = kvf_w = None
    tq = _tile(T, 512)
    for i in range(depth):
        s = {}
        if i == 0:
            arrive("ffn1_in_0", [handles[GATHER_GROUPS[-1][0]]["token"]])
        for k in (1, 2):
            if k == 2:
                s["h_a"] = h
                if i == 0:
                    arrive("hgrn", [h])
                    z, xn = norm_mm(h, norm("mix_norm_pre", i), W["hgrn_w_in", 0], "hgrn_in")
                    xm, o, states = hgrn_fwd(z, hgrn_lb_logits, hgrn_out_norm, "hgrn_scan")
                    s.update(z=z, o=o, states=states)
                    wmix = w_rows("hgrn_w_out", 0)
                else:
                    arrive("fox", [h])
                    qg, xn = norm_mm(h, norm("mix_norm_pre", i), W["fox_w_qg", 0], "fox_qg")
                    tqf = _tile(T, 1024)
                    ka, kb, vt3 = fox_prep(kvf, ct, n_fox, tqf)
                    xm, o, lse = fox_fwd(qg, ka, kb, vt3, "fox_attn")
                    lse = lse.reshape(-1, T // tqf, 2, tqf // tq, tq).transpose(0, 1, 3, 2, 4).reshape(-1, T // tq, 2, tq)
                    s.update(qg=qg, o=o, lse=lse, ka=ka, kb=kb)
                    wmix = w_rows("fox_w_out", 0)
                h, ym = mm_norm_res(xm, wmix, norm("mix_norm_post", i), h, 1.0, f"mix_out_{i}", kb=_tile(xm.shape[1], 512))
                s.update(xm=xm, ym=ym, xn_mix=xn)
                arrive(f"rest_{i}", [h])
            gate, up, a, xn = norm_mm_swiglu(h, norm(f"ffn{k}_norm_pre", i), W[f"ffn{k}_w_in", i], f"ffn{k}_in_{i}")
            if (i, k) == (0, 1):
                arrive("ffn1_out_0", [a])
            hn, y = mm_norm_res(a, w_rows(f"ffn{k}_w_out", i), norm(f"ffn{k}_norm_post", i), h, 0.5, f"ffn{k}_out_{i}")
            s[f"ffn{k}"] = (h, gate, up, a, y, xn)
            h = hn
        s["h_c"] = h
        ple_proj = W["ple_w_proj", i].transpose(1, 0, 2).reshape(p.shape[3], D)
        h, pgate, pp = ple_fwd(h, norm("ple_norm_pre", i), w_rows("ple_w_gate", i), p3, ple_proj, norm("ple_norm_post", i),
                               i, f"ple_{i}")
        s.update(pgate=pgate, pp=pp)
        saved.append(s)
        if i == 0:
            kvf_nat = W["fox_w_kvf", 0].transpose(1, 0, 2).reshape(D, -1)
            kvf_cols = kvf_nat.shape[1]
            kvf_w = _pad_lanes(kvf_nat, 2 * fox_w + LANES)
            kvf, xn_kv = norm_mm(h, _row(kv_norm), kvf_w, "fox_kvf", tn=kvf_w.shape[1])
            ct = fox_cum_fwd(kvf, fcol, b_row, "fox_cum")
            h_kv = h
            arrive("ffn1_1", [h])

    dh, loss_part = loss_fwd_bwd(h, target, "loss")
    loss = lax.psum(loss_part[0, 0], ("x", "y", "c"))

    gsmall = {n: [None] * weights[n].shape[0] if weights[n].ndim == 2 else None for n in small_names}
    sent = []

    def send(gname, keys, srcs, whole):
        lands = [_landing(a, w) for a, w in zip(srcs, whole)]
        hd = exchange_start(srcs, lands, whole, f"scatter_start_{gname}")
        sent.append((gname, keys, hd))
        return hd["token"][0:1, 0:1]

    def send_grads(gname, grads):
        return send(gname, list(grads), list(grads.values()), [False] * len(grads))

    for i in reversed(range(depth)):
        s = saved[i]
        grads = {}
        dh, du, dpp, xn, dgpre, dgpost = ple_bwd(dh, s["h_c"], s["pgate"], s["pp"], norm("ple_norm_pre", i),
                                                 w_rows("ple_w_gate", i), norm("ple_norm_post", i), f"ple_bwd_{i}")
        gsmall["ple_norm_pre"][i], gsmall["ple_norm_post"][i] = dgpre, dgpost
        grads["ple_w_gate", i] = mm_tn(xn, du, f"ple_dgate_{i}", xb=_tile(D, 512)).reshape(N_DEV, -1, D)
        dproj = mm_tn(p3, dpp, f"ple_dproj_{i}", x_layer=i, yb=D)[0]
        grads["ple_w_proj", i] = dproj.reshape(dproj.shape[0], N_DEV, -1).transpose(1, 0, 2)
        for k in (2, 1):
            hin, gate, up, a, y, xn = s[f"ffn{k}"]
            ffn_cs = gate.shape[2]
            dy, dgpost, dz = nbwd_mm_nt(dh, y, norm(f"ffn{k}_norm_post", i), w_rows(f"ffn{k}_w_out", i), 0.5, ffn_cs,
                                        f"ffn{k}_bwd_out_{i}", gate=gate, up=up)
            dz = dz.reshape(-1, T, ffn_cs)
            grads[f"ffn{k}_w_out", i] = mm_tn(a, dy, f"ffn{k}_dwout_{i}").reshape(N_DEV, -1, D)
            grads[f"ffn{k}_w_in", i] = mm_tn(dz, xn, f"ffn{k}_dwin_{i}").reshape(N_DEV, ffn_cs, D)
            tokv = send_grads(f"ffn{k}_{i}", grads)
            grads = {}
            dh, dgpre = mm_nt_nbwd(dz, W[f"ffn{k}_w_in", i], hin, norm(f"ffn{k}_norm_pre", i) + tokv, dh, f"ffn{k}_bwd_in_{i}",
                                   transposed=True)
            gsmall[f"ffn{k}_norm_pre"][i], gsmall[f"ffn{k}_norm_post"][i] = dgpre, dgpost
            if k == 2:
                nm = "hgrn" if i == 0 else "fox"
                wmix = w_rows(f"{nm}_w_out", 0)
                dy, dgpost, dxm = nbwd_mm_nt(dh, s["ym"], norm("mix_norm_post", i), wmix, 1.0, _tile(wmix.shape[0], 512),
                                             f"{nm}_bwd_out")
                grads[f"{nm}_w_out", 0] = mm_tn(s["xm"], dy, f"{nm}_dwout", xb=_tile(wmix.shape[0], 512)).reshape(N_DEV, -1, D)
                gsmall["mix_norm_post"][i] = dgpost
                if i == 0:
                    dq, df, dv, dg, dlog, don = hgrn_bwd(dxm, s["z"], s["o"], s["states"], hgrn_lb_logits, hgrn_out_norm,
                                                         "hgrn_scan_bwd")
                    gsmall["hgrn_lb_logits"] = [dlog[0:1], dlog[1:2]]
                    gsmall["hgrn_out_norm"] = [don]
                    dzm = jnp.concatenate([dq, df, dv, dg], axis=1)
                    nmin = "hgrn_w_in"
                else:
                    do, dg, dsum = fox_gate_bwd(dxm, s["o"], s["qg"], "fox_gate_bwd")
                    dsum4 = dsum[:, :n_fox].T.reshape(n_fox // 2, 2, T // tq, tq).transpose(0, 2, 1, 3)
                    dqt, dk_sh, dv_sh, dc4, dcq4 = fox_bwd(*fox_bwd_prep(s["qg"], kvf, do, tq), s["ka"], s["kb"], s["lse"], dsum4,
                                                           "fox_attn_bwd")
                    dq = dqt.transpose(1, 3, 0, 2).reshape(T, fox_w)
                    dzm = jnp.concatenate([dq.astype(BF), dg], axis=1)
                    nmin = "fox_w_qg"
                wmin = W[nmin, 0]
                grads[nmin, 0] = mm_tn(s["xn_mix"], dzm, f"{nm}_dwin", yb=wmin.shape[2])
                tokv = send_grads(f"mix_{i}", grads)
                grads = {}
                dh, dgpre = mm_nt_nbwd(dzm, wmin, s["h_a"], norm("mix_norm_pre", i) + tokv, dh, f"{nm}_bwd_in", tn=wmin.shape[2])
                gsmall["mix_norm_pre"][i] = dgpre
        if i == 1:
            dct = (dc4 + dcq4).transpose(0, 2, 1, 3).reshape(n_fox, T)
            dct = jnp.pad(dct, ((0, LANES - n_fox), (0, 0)))
            dflog, db = fox_cum_bwd(dct, kvf, fcol, b_row, "fox_cum_bwd")
            gsmall["fox_b_f"] = db[:, :n_fox]
            dkvf = jnp.concatenate([dk_sh, dv_sh, dflog], axis=1)
            dwk = mm_tn(xn_kv, dkvf, "fox_dwkvf", yb=kvf_w.shape[1])[0][:, :kvf_cols]
            tokv = send_grads("kvf", {("fox_w_kvf", 0): dwk.reshape(D, N_DEV, -1).transpose(1, 0, 2)})
            dh, dgkv = mm_nt_nbwd(dkvf, kvf_w, h_kv, _row(kv_norm) + tokv, dh, "fox_kvf_bwd", tn=kvf_w.shape[1])
            gsmall["kv_norm"] = dgkv
    grad_x = dh.reshape(x.shape)

    def small_rows(n):
        g = gsmall[n]
        rows = g if isinstance(g, list) else [g]
        return [_pad_lanes(r, D) for r in rows]

    counts = {n: len(small_rows(n)) for n in small_names}
    packed = jnp.concatenate([r for n in small_names for r in small_rows(n)], axis=0)
    n_rows = packed.shape[0]
    packed = jnp.pad(packed, ((0, -n_rows % 8), (0, 0)))
    send("small", ["small"], [packed], [True])

    res = {}
    after = [dh]
    small_parts = None
    for gname, keys, hd in sent:
        lands = exchange_wait(hd, after, f"scatter_wait_{gname}")
        after = []
        for key, parts in zip(keys, lands):
            if key == "small":
                small_parts = parts
                continue
            n, l = key
            w = weights[n]
            if w.ndim == 2:
                as3 = lambda a: a.reshape((1,) + a.shape)
            elif n in TRANSPOSED:
                as3 = lambda a: a.transpose(0, 2, 1)
            else:
                as3 = lambda a: a
            res[n] = adamw(parts, as3(w), as3(mom1[n]), as3(mom2[n]), l, res.get(n), f"adamw_{n}_{l}")
            after.append(res[n][1])
    for n in TRANSPOSED:
        res[n] = [a.transpose(0, 2, 1) for a in res[n]]

    def pack(d):
        rows = []
        for n in small_names:
            a = d[n]
            rows.append(_pad_lanes(a.reshape(-1, a.shape[-1]), D))
        a = jnp.concatenate(rows, axis=0)
        return jnp.pad(a, ((0, -n_rows % 8), (0, 0)))[None]

    sm = adamw(small_parts, pack(weights), pack(mom1), pack(mom2), 0, None, "adamw_small")
    off = 0
    for n in small_names:
        w = weights[n]
        res[n] = [a[0, off:off + counts[n], :w.shape[-1]].reshape(w.shape) for a in sm]
        off += counts[n]

    out = [loss, grad_x]
    for j in range(4):
        out += [res[n][j].reshape(weights[n].shape) for n in names]
    return tuple(out)
```

```python
import functools

import jax
import jax.numpy as jnp
from jax import lax
from jax.experimental import pallas as pl
from jax.experimental.pallas import tpu as pltpu

F32 = jnp.float32
BF = jnp.bfloat16
NORM_EPS = 1e-6
N_DEV = 8
HGRN_DK = 128
HGRN_CHUNK = 16
HGRN_HEADS_PER_STEP = 8
HALF = HGRN_CHUNK // 2
FOX_HEAD_DIM = 64
LANES = 128
ADAM_LR, ADAM_B1, ADAM_B2, ADAM_EPS, ADAM_WD, ADAM_STEP = 0.001, 0.9, 0.999, 1e-08, 0.01, 10
VMEM_LIMIT = 56 * 1024 * 1024
HI = lax.Precision.HIGHEST
NT = (((1,), (1,)), ((), ()))
TN = (((0,), (0,)), ((), ()))


def _params(n_axes):
    return pltpu.CompilerParams(dimension_semantics=("arbitrary",) * n_axes, vmem_limit_bytes=VMEM_LIMIT)


def _tile(n, want):
    t = min(n, want)
    while n % t:
        t //= 2
    return t


def _sigmoid(x):
    return 1.0 / (1.0 + jnp.exp(-x))


def _rms(x):
    r = lax.rsqrt(jnp.mean(x * x, axis=-1, keepdims=True) + NORM_EPS)
    return x * r, r


def _norm_bwd(dy, xhat, r, g):
    dxh = dy * g
    return r * (dxh - xhat * jnp.mean(dxh * xhat, axis=-1, keepdims=True))


def _colsum(x):
    return jnp.sum(x, axis=0, keepdims=True)


def _w_spec(w, blk):
    if w.ndim == 3:
        return lambda off: pl.BlockSpec((None, w.shape[1], w.shape[2]), lambda i, j: (j + off, 0, 0))
    return lambda off: pl.BlockSpec((w.shape[0], blk), lambda i, j: (0, j + off))


def norm_mm_swiglu(h, g, w3, name):
    T, D = h.shape
    nb, cs, _ = w3.shape
    nh = nb // 2
    tm = _tile(T, 1024)

    def body(h_ref, g_ref, wg_ref, wu_ref, gate_ref, up_ref, a_ref, xn_ref):
        @pl.when(pl.program_id(1) == 0)
        def _():
            xh, _ = _rms(h_ref[...])
            xn_ref[...] = (xh * g_ref[...]).astype(BF)

        xn = xn_ref[...]
        gt = lax.dot_general(xn, wg_ref[...], NT, preferred_element_type=F32)
        up = lax.dot_general(xn, wu_ref[...], NT, preferred_element_type=F32)
        sg = _sigmoid(gt)
        silu = gt * sg
        gate_ref[...] = (up * (sg * (1.0 + gt * (1.0 - sg)))).astype(BF)
        up_ref[...] = silu.astype(BF)
        a_ref[...] = (silu * up).astype(BF)

    ws = _w_spec(w3, cs)
    blk = pl.BlockSpec((None, tm, cs), lambda i, j: (j, i, 0))
    shp = jax.ShapeDtypeStruct((nh, T, cs), BF)
    return pl.pallas_call(
        body, name=name, grid=(T // tm, nh),
        in_specs=[pl.BlockSpec((tm, D), lambda i, j: (i, 0)), pl.BlockSpec((1, D), lambda i, j: (0, 0)), ws(0), ws(nh)],
        out_specs=[blk, blk, blk, pl.BlockSpec((tm, D), lambda i, j: (i, 0))],
        out_shape=[shp, shp, shp, jax.ShapeDtypeStruct((T, D), BF)], compiler_params=_params(2),
    )(h, g, w3, w3)


def norm_mm(h, g, w, name, tn=None):
    T, D = h.shape
    if w.ndim == 3:
        nb, cs = w.shape[0], w.shape[2]
    else:
        cs = tn
        nb = w.shape[1] // cs
    tm = _tile(T, 1024)

    def body(h_ref, g_ref, w_ref, z_ref, xn_ref):
        @pl.when(pl.program_id(1) == 0)
        def _():
            xh, _ = _rms(h_ref[...])
            xn_ref[...] = (xh * g_ref[...]).astype(BF)

        z_ref[...] = jnp.dot(xn_ref[...], w_ref[...], preferred_element_type=F32)

    return pl.pallas_call(
        body, name=name, grid=(T // tm, nb),
        in_specs=[pl.BlockSpec((tm, D), lambda i, j: (i, 0)), pl.BlockSpec((1, D), lambda i, j: (0, 0)),
                  _w_spec(w, cs)(0)],
        out_specs=[pl.BlockSpec((tm, cs), lambda i, j: (i, j)), pl.BlockSpec((tm, D), lambda i, j: (i, 0))],
        out_shape=[jax.ShapeDtypeStruct((T, nb * cs), F32), jax.ShapeDtypeStruct((T, D), BF)], compiler_params=_params(2),
    )(h, g, w)


def _x_spec(x, tm, kb):
    if x.ndim == 3:
        return pl.BlockSpec((None, tm, x.shape[2]), lambda i, j: (j, i, 0))
    return pl.BlockSpec((tm, kb), lambda i, j: (i, j))


def mm_norm_res(x, w2, g, h, coef, name, kb=None):
    T, D = h.shape
    if x.ndim == 3:
        nb, kb = x.shape[0], x.shape[2]
    else:
        nb = x.shape[1] // kb
    tm = _tile(T, 1024)

    def body(x_ref, w_ref, h_ref, g_ref, hn_ref, y_ref, acc_ref):
        b = pl.program_id(1)

        @pl.when(b == 0)
        def _():
            acc_ref[...] = jnp.zeros_like(acc_ref)

        acc_ref[...] += jnp.dot(x_ref[...], w_ref[...], preferred_element_type=F32)

        @pl.when(b == nb - 1)
        def _():
            y = acc_ref[...]
            y_ref[...] = y
            yh, _ = _rms(y)
            hn_ref[...] = h_ref[...] + coef * (yh * g_ref[...])

    tok = pl.BlockSpec((tm, D), lambda i, j: (i, 0))
    shp = jax.ShapeDtypeStruct((T, D), F32)
    return pl.pallas_call(
        body, name=name, grid=(T // tm, nb),
        in_specs=[_x_spec(x, tm, kb), pl.BlockSpec((kb, D), lambda i, j: (j, 0)), tok,
                  pl.BlockSpec((1, D), lambda i, j: (0, 0))],
        out_specs=[tok, tok], out_shape=[shp, shp],
        scratch_shapes=[pltpu.VMEM((tm, D), F32)], compiler_params=_params(2),
    )(x, w2, h, g)


def nbwd_mm_nt(dout, y, g, w2, coef, kb, name, gate=None, up=None):
    T, D = dout.shape
    nb = w2.shape[0] // kb
    swiglu = gate is not None
    tm = _tile(T, 1024 if swiglu else 512)

    def body(*refs):
        if swiglu:
            dout_ref, y_ref, g_ref, w_ref, gate_ref, up_ref, dy_ref, dg_ref, da_ref, dys_ref = refs
        else:
            dout_ref, y_ref, g_ref, w_ref, dy_ref, dg_ref, da_ref, dys_ref = refs
        i, b = pl.program_id(0), pl.program_id(1)

        @pl.when((i == 0) & (b == 0))
        def _():
            dg_ref[...] = jnp.zeros_like(dg_ref)

        @pl.when(b == 0)
        def _():
            yh, r = _rms(y_ref[...])
            dyn = coef * dout_ref[...]
            dg_ref[...] += _colsum(dyn * yh)
            dy = _norm_bwd(dyn, yh, r, g_ref[...]).astype(BF)
            dys_ref[...] = dy
            dy_ref[...] = dy

        da = lax.dot_general(dys_ref[...], w_ref[...], NT, preferred_element_type=F32)
        if swiglu:
            da_ref[0] = (da * gate_ref[...].astype(F32)).astype(BF)
            da_ref[1] = (da * up_ref[...].astype(F32)).astype(BF)
        else:
            da_ref[...] = da.astype(BF)

    tok = pl.BlockSpec((tm, D), lambda i, j: (i, 0))
    vec = pl.BlockSpec((1, D), lambda i, j: (0, 0))
    in_specs = [tok, tok, vec, pl.BlockSpec((kb, D), lambda i, j: (j, 0))]
    args = [dout, y, g, w2]
    if swiglu:
        blk = pl.BlockSpec((None, tm, kb), lambda i, j: (j, i, 0))
        in_specs += [blk, blk]
        args += [gate, up]
        da_spec = pl.BlockSpec((2, None, tm, kb), lambda i, j: (0, j, i, 0))
        da_shape = jax.ShapeDtypeStruct((2, nb, T, kb), BF)
    else:
        da_spec = pl.BlockSpec((tm, kb), lambda i, j: (i, j))
        da_shape = jax.ShapeDtypeStruct((T, nb * kb), BF)
    return pl.pallas_call(
        body, name=name, grid=(T // tm, nb), in_specs=in_specs,
        out_specs=[tok, vec, da_spec],
        out_shape=[jax.ShapeDtypeStruct((T, D), BF), jax.ShapeDtypeStruct((1, D), F32), da_shape],
        scratch_shapes=[pltpu.VMEM((tm, D), BF)], compiler_params=_params(2),
    )(*args)


def mm_nt_nbwd(dz, w, h, g, dout, name, tn=None, transposed=False):
    T, D = h.shape
    if w.ndim == 3:
        nb, cs = w.shape[0], w.shape[1 if transposed else 2]
    else:
        cs = tn
        nb = w.shape[1] // cs
    tm = _tile(T, 1024)

    def body(dz_ref, w_ref, h_ref, g_ref, dout_ref, dh_ref, dg_ref, acc_ref):
        i, b = pl.program_id(0), pl.program_id(1)

        @pl.when((i == 0) & (b == 0))
        def _():
            dg_ref[...] = jnp.zeros_like(dg_ref)

        @pl.when(b == 0)
        def _():
            acc_ref[...] = jnp.zeros_like(acc_ref)

        if transposed:
            acc_ref[...] += jnp.dot(dz_ref[...], w_ref[...], preferred_element_type=F32)
        else:
            acc_ref[...] += lax.dot_general(dz_ref[...], w_ref[...], NT, preferred_element_type=F32)

        @pl.when(b == nb - 1)
        def _():
            xh, r = _rms(h_ref[...])
            gg = g_ref[...]
            dxn = acc_ref[...]
            dg_ref[...] += _colsum(dxn * xh)
            dh_ref[...] = dout_ref[...] + _norm_bwd(dxn, xh, r, gg)

    tok = pl.BlockSpec((tm, D), lambda i, j: (i, 0))
    vec = pl.BlockSpec((1, D), lambda i, j: (0, 0))
    return pl.pallas_call(
        body, name=name, grid=(T // tm, nb),
        in_specs=[_x_spec(dz, tm, cs), _w_spec(w, cs)(0), tok, vec, tok],
        out_specs=[tok, vec],
        out_shape=[jax.ShapeDtypeStruct((T, D), F32), jax.ShapeDtypeStruct((1, D), F32)],
        scratch_shapes=[pltpu.VMEM((tm, D), F32)], compiler_params=_params(2),
    )(dz, w, h, g, dout)


def mm_tn(x, y, name, xb=None, yb=None, x_layer=None):
    T = y.shape[-2]
    wide = (yb if yb is not None else y.shape[-1]) > 1024
    tt = _tile(T, 1024 if wide else 2048)
    x_split = (x.ndim == 3 and x_layer is None) or xb is not None
    if x_layer is not None:
        xs = pl.BlockSpec((None, tt, x.shape[2]), lambda b, t: (x_layer, t, 0))
        kdim = x.shape[2]
    elif x.ndim == 3:
        xs = pl.BlockSpec((None, tt, x.shape[2]), lambda b, t: (b, t, 0))
        nb, kdim = x.shape[0], x.shape[2]
    elif xb is not None:
        xs = pl.BlockSpec((tt, xb), lambda b, t: (t, b))
        nb, kdim = x.shape[1] // xb, xb
    else:
        xs = pl.BlockSpec((tt, x.shape[1]), lambda b, t: (t, 0))
        kdim = x.shape[1]
    if x_split:
        ys = pl.BlockSpec((tt, y.shape[1]), lambda b, t: (t, 0))
        ndim = y.shape[1]
        out_spec = pl.BlockSpec((kdim, ndim), lambda b, t: (b, 0))
        out_shape = jax.ShapeDtypeStruct((nb * kdim, ndim), BF)
    else:
        if y.ndim == 3:
            ys = pl.BlockSpec((None, tt, y.shape[2]), lambda b, t: (b, t, 0))
            nb, ndim = y.shape[0], y.shape[2]
        else:
            ys = pl.BlockSpec((tt, yb), lambda b, t: (t, b))
            nb, ndim = y.shape[1] // yb, yb
        out_spec = pl.BlockSpec((None, kdim, ndim), lambda b, t: (b, 0, 0))
        out_shape = jax.ShapeDtypeStruct((nb, kdim, ndim), BF)
    nt = T // tt

    def body(x_ref, y_ref, o_ref, acc_ref):
        t = pl.program_id(1)

        @pl.when(t == 0)
        def _():
            acc_ref[...] = jnp.zeros_like(acc_ref)

        acc_ref[...] += lax.dot_general(x_ref[...].astype(BF), y_ref[...].astype(BF), TN, preferred_element_type=F32)

        @pl.when(t == nt - 1)
        def _():
            o_ref[...] = acc_ref[...].astype(BF)

    return pl.pallas_call(
        body, name=name, grid=(nb, nt), in_specs=[xs, ys], out_specs=out_spec, out_shape=out_shape,
        scratch_shapes=[pltpu.VMEM((kdim, ndim), F32)], compiler_params=_params(2),
    )(x, y)


def ple_fwd(h, gpre, wg, p3, wp, gpost, layer, name):
    T, D = h.shape
    pd = p3.shape[2]
    tm = _tile(T, 512)

    def body(h_ref, gpre_ref, wg_ref, p_ref, wp_ref, gpost_ref, hn_ref, gate_ref, pp_ref):
        x = h_ref[...]
        xh, _ = _rms(x)
        u = jnp.dot((xh * gpre_ref[...]).astype(BF), wg_ref[...], preferred_element_type=F32)
        gate = _sigmoid(u)
        pp = jnp.dot(p_ref[...].astype(BF), wp_ref[...], preferred_element_type=F32)
        yh, _ = _rms(gate * pp)
        hn_ref[...] = x + yh * gpost_ref[...]
        gate_ref[...] = gate.astype(BF)
        pp_ref[...] = pp.astype(BF)

    tok = pl.BlockSpec((tm, D), lambda i: (i, 0))
    vec = pl.BlockSpec((1, D), lambda i: (0, 0))
    return pl.pallas_call(
        body, name=name, grid=(T // tm,),
        in_specs=[tok, vec, pl.BlockSpec((D, D), lambda i: (0, 0)),
                  pl.BlockSpec((None, tm, pd), lambda i: (layer, i, 0)),
                  pl.BlockSpec((pd, D), lambda i: (0, 0)), vec],
        out_specs=[tok, tok, tok],
        out_shape=[jax.ShapeDtypeStruct((T, D), F32), jax.ShapeDtypeStruct((T, D), BF), jax.ShapeDtypeStruct((T, D), BF)],
        compiler_params=_params(1),
    )(h, gpre, wg, p3, wp, gpost)


def ple_bwd(dout, h, gate, pp, gpre, wg, gpost, name):
    T, D = h.shape
    tm = _tile(T, 512)

    def body(dout_ref, h_ref, gate_ref, pp_ref, gpre_ref, wg_ref, gpost_ref, dh_ref, du_ref, dpp_ref, xn_ref, dgpre_ref, dgpost_ref):
        @pl.when(pl.program_id(0) == 0)
        def _():
            dgpre_ref[...] = jnp.zeros_like(dgpre_ref)
            dgpost_ref[...] = jnp.zeros_like(dgpost_ref)

        dout = dout_ref[...]
        gate = gate_ref[...].astype(F32)
        pp = pp_ref[...].astype(F32)
        yh, ry = _rms(gate * pp)
        dgpost_ref[...] += _colsum(dout * yh)
        dy = _norm_bwd(dout, yh, ry, gpost_ref[...])
        dpp_ref[...] = (dy * gate).astype(BF)
        du = (dy * pp * gate * (1.0 - gate)).astype(BF)
        du_ref[...] = du
        dxn = lax.dot_general(du, wg_ref[...], NT, preferred_element_type=F32)
        xh, r = _rms(h_ref[...])
        gp = gpre_ref[...]
        dgpre_ref[...] += _colsum(dxn * xh)
        dh_ref[...] = dout + _norm_bwd(dxn, xh, r, gp)
        xn_ref[...] = (xh * gp).astype(BF)

    tok = pl.BlockSpec((tm, D), lambda i: (i, 0))
    vec = pl.BlockSpec((1, D), lambda i: (0, 0))
    bft = jax.ShapeDtypeStruct((T, D), BF)
    v32 = jax.ShapeDtypeStruct((1, D), F32)
    return pl.pallas_call(
        body, name=name, grid=(T // tm,),
        in_specs=[tok, tok, tok, tok, vec, pl.BlockSpec((D, D), lambda i: (0, 0)), vec],
        out_specs=[tok, tok, tok, tok, vec, vec],
        out_shape=[jax.ShapeDtypeStruct((T, D), F32), bft, bft, bft, v32, v32],
        compiler_params=_params(1),
    )(dout, h, gate, pp, gpre, wg, gpost)


def _chunk_tri(tb, upper):
    r = lax.broadcasted_iota(jnp.int32, (tb, tb), 0)
    c = lax.broadcasted_iota(jnp.int32, (tb, tb), 1)
    shift = HGRN_CHUNK.bit_length() - 1
    same = jnp.right_shift(r, shift) == jnp.right_shift(c, shift)
    return (same & ((c >= r) if upper else (c <= r))).astype(F32)


def _hgrn_gates(z, logits):
    lb = 1.0 / (1.0 + jnp.exp(logits[1:2, :] - logits[0:1, :]))
    e = jnp.exp(-jnp.abs(z))
    inv = 1.0 / (1.0 + e)
    sig = jnp.where(z >= 0, inv, e * inv)
    nsig = jnp.where(z >= 0, e * inv, inv)
    return lb, sig, nsig, lb + (1.0 - lb) * sig


def hgrn_fwd(z, lb_logits, out_norm, name):
    T = z.shape[0]
    W = z.shape[1] // 4
    H = W // HGRN_DK
    C = HGRN_CHUNK
    HB = _tile(H, HGRN_HEADS_PER_STEP)
    tb = _tile(T, 256)
    nch = tb // C

    def body(zq_ref, zf_ref, zv_ref, zg_ref, lbl_ref, on_ref, x_ref, o_ref, st_ref, s_scr, cum_scr, k_scr, v_scr, o_scr):
        @pl.when(pl.program_id(1) == 0)
        def _():
            s_scr[...] = jnp.zeros_like(s_scr)

        lb, sig, nsig, f = _hgrn_gates(zf_ref[...], lbl_ref[...])
        cum = jnp.dot(_chunk_tri(tb, False), jnp.log(f), precision=HI, preferred_element_type=F32)
        kk = (1.0 - lb) * nsig
        for hh in range(HB):
            cols = slice(hh * HGRN_DK, (hh + 1) * HGRN_DK)
            cum_scr[hh] = cum[:, cols]
            k_scr[hh] = kk[:, cols]
            v_scr[hh] = zv_ref[:, cols]
        row = lax.broadcasted_iota(jnp.int32, (C, HGRN_DK), 0)

        def chunk(c, carry):
            r0 = pl.multiple_of(c * C, C)
            rows = pl.ds(r0, C)
            last_row = pl.ds(r0 + C - 1, 1)
            heads = []
            for hh in range(HB):
                cols = slice(hh * HGRN_DK, (hh + 1) * HGRN_DK)
                q, cu = zq_ref[rows, cols], cum_scr[hh, rows, :]
                st = s_scr[hh]
                st_ref[c, hh] = st
                o = lax.dot_general((q * jnp.exp(cu)).astype(BF), st.astype(BF), NT, preferred_element_type=F32)
                last = cum_scr[hh, last_row, :]
                kg = (k_scr[hh, rows, :] * jnp.exp(last - cu)).astype(BF)
                s_scr[hh] = st * jnp.exp(last) + lax.dot_general(v_scr[hh, rows, :].astype(BF), kg, TN, preferred_element_type=F32)
                heads.append((hh, q, cu, o))
            for hh, q, cu, o in heads:
                qr = q.astype(BF).astype(F32)
                low = jnp.zeros((C - HALF, HGRN_DK), F32)
                for s in range(C):
                    one = pl.ds(r0 + s, 1)
                    sl = slice(0 if s < HALF else HALF, C)
                    e = jnp.exp(jnp.minimum(cu[sl] - cum_scr[hh, one, :], 0.0))
                    col = jnp.sum(qr[sl] * (e * k_scr[hh, one, :]).astype(BF).astype(F32), axis=-1, keepdims=True)
                    col = jnp.where(row[sl] >= s, col, 0.0).astype(BF).astype(F32)
                    term = col * v_scr[hh, one, :].astype(BF).astype(F32)
                    if s < HALF:
                        o = o + term
                    else:
                        low = low + term
                o_scr[hh, rows, :] = o
                o_scr[hh, pl.ds(r0 + HALF, C - HALF), :] += low
            return carry

        lax.fori_loop(0, nch, chunk, 0)
        for hh in range(HB):
            cols = slice(hh * HGRN_DK, (hh + 1) * HGRN_DK)
            o = o_scr[hh]
            o_ref[:, cols] = o
            oh, _ = _rms(o)
            g = zg_ref[:, cols]
            x_ref[:, cols] = (oh * on_ref[...] * (g * _sigmoid(g))).astype(BF)

    def zs(part):
        return pl.BlockSpec((tb, HB * HGRN_DK), lambda hd, i: (i, part * (H // HB) + hd))

    blk = pl.BlockSpec((tb, HB * HGRN_DK), lambda hd, i: (i, hd))
    wide = pltpu.VMEM((HB, tb, HGRN_DK), F32)
    return pl.pallas_call(
        body, name=name, grid=(H // HB, T // tb),
        in_specs=[zs(0), zs(1), zs(2), zs(3), pl.BlockSpec((2, HB * HGRN_DK), lambda hd, i: (0, hd)),
                  pl.BlockSpec((1, HGRN_DK), lambda hd, i: (0, 0))],
        out_specs=[blk, blk, pl.BlockSpec((nch, HB, HGRN_DK, HGRN_DK), lambda hd, i: (i, hd, 0, 0))],
        out_shape=[jax.ShapeDtypeStruct((T, W), BF), jax.ShapeDtypeStruct((T, W), F32),
                   jax.ShapeDtypeStruct((T // C, H, HGRN_DK, HGRN_DK), F32)],
        scratch_shapes=[pltpu.VMEM((HB, HGRN_DK, HGRN_DK), F32), wide, wide, wide, wide],
        compiler_params=_params(2),
    )(z, z, z, z, lb_logits, out_norm)


def hgrn_bwd(dx, z, o, states, lb_logits, out_norm, name):
    T = z.shape[0]
    W = z.shape[1] // 4
    H = W // HGRN_DK
    C = HGRN_CHUNK
    HB = _tile(H, HGRN_HEADS_PER_STEP)
    tb = _tile(T, 256)
    nch = tb // C
    nblk = T // tb

    def body(dx_ref, zq_ref, zf_ref, zv_ref, zg_ref, o_ref, st_ref, lbl_ref, on_ref,
             dq_ref, df_ref, dv_ref, dg_ref, dl_ref, don_ref,
             ds_scr, cum_scr, k_scr, v_scr, q_scr, do_scr, dq_scr, dk_scr, dv_scr, dcum_scr):
        hd, i = pl.program_id(0), pl.program_id(1)

        @pl.when(i == 0)
        def _():
            ds_scr[...] = jnp.zeros_like(ds_scr)
            dl_ref[...] = jnp.zeros_like(dl_ref)

        @pl.when((i == 0) & (hd == 0))
        def _():
            don_ref[...] = jnp.zeros_like(don_ref)

        lb, sig, nsig, f = _hgrn_gates(zf_ref[...], lbl_ref[...])
        cum = jnp.dot(_chunk_tri(tb, False), jnp.log(f), precision=HI, preferred_element_type=F32)
        kk = (1.0 - lb) * nsig
        w = on_ref[...]
        for hh in range(HB):
            cols = slice(hh * HGRN_DK, (hh + 1) * HGRN_DK)
            cum_scr[hh] = cum[:, cols]
            k_scr[hh] = kk[:, cols]
            v_scr[hh] = zv_ref[:, cols]
            q_scr[hh] = zq_ref[:, cols]
            oh, r = _rms(o_ref[:, cols])
            g = zg_ref[:, cols]
            sg = _sigmoid(g)
            dxv = dx_ref[:, cols].astype(F32)
            dg_ref[:, cols] = (dxv * (oh * w) * (sg * (1.0 + g * (1.0 - sg)))).astype(BF)
            don = dxv * (g * sg)
            don_ref[...] += _colsum(don * oh)
            do_scr[hh] = _norm_bwd(don, oh, r, w)
        row = lax.broadcasted_iota(jnp.int32, (C, HGRN_DK), 0)

        def chunk(cc, carry):
            c = nch - 1 - cc
            r0 = pl.multiple_of(c * C, C)
            rows = pl.ds(r0, C)
            last_row = pl.ds(r0 + C - 1, 1)
            heads = []
            for hh in range(HB):
                q, k, v, cu, do = q_scr[hh, rows, :], k_scr[hh, rows, :], v_scr[hh, rows, :], cum_scr[hh, rows, :], do_scr[hh, rows, :]
                st = st_ref[c, hh]
                dst = ds_scr[hh]
                last = cum_scr[hh, last_row, :]
                lam, gam, elast = jnp.exp(cu), jnp.exp(last - cu), jnp.exp(last)
                dob, dstb = do.astype(BF), dst.astype(BF)
                dq = jnp.dot(dob, st.astype(BF), preferred_element_type=F32) * lam
                dv = lax.dot_general((k * gam).astype(BF), dstb, NT, preferred_element_type=F32)
                dk = jnp.dot(v.astype(BF), dstb, preferred_element_type=F32) * gam
                dlast = elast * _colsum(dst * st) + _colsum(dk * k)
                ds_scr[hh] = dst * elast + lax.dot_general(dob, (q * lam).astype(BF), TN, preferred_element_type=F32)
                heads.append((hh, q, k, cu, do, dq, dk, dv, dlast))
            for hh, q, k, cu, do, dq, dk, dv, dlast in heads:
                for first in (True, False):
                    sl = slice(0 if first else HALF, C)
                    qs, cus, dos, rws = q[sl], cu[sl], do[sl], row[sl]
                    dqs, dks, dvs = dq[sl], dk[sl], dv[sl]
                    for s in (range(HALF) if first else range(HALF, C)):
                        one = pl.ds(r0 + s, 1)
                        e = jnp.where(rws >= s, jnp.exp(jnp.minimum(cus - cum_scr[hh, one, :], 0.0)), 0.0)
                        ks = k_scr[hh, one, :]
                        da = jnp.sum(dos * v_scr[hh, one, :], axis=-1, keepdims=True)
                        pq = qs * e
                        a = jnp.sum(pq * ks, axis=-1, keepdims=True)
                        dqs = dqs + da * e * ks
                        dks = jnp.where(rws == s, dks + _colsum(da * pq), dks)
                        dvs = jnp.where(rws == s, dvs + _colsum(a * dos), dvs)
                    if first:
                        dq, dk, dv = dqs, dks, dvs
                        top = pl.ds(r0, HALF)
                        dq_scr[hh, top, :] = dq[:HALF]
                        dk_scr[hh, top, :] = dk[:HALF]
                        dv_scr[hh, top, :] = dv[:HALF]
                        dcum_scr[hh, top, :] = q[:HALF] * dq[:HALF] - k[:HALF] * dk[:HALF]
                    else:
                        low = pl.ds(r0 + HALF, C - HALF)
                        dq_scr[hh, low, :] = dqs
                        dk_scr[hh, low, :] = dks
                        dv_scr[hh, low, :] = dvs
                        dcum_scr[hh, low, :] = qs * dqs - k[sl] * dks + jnp.where(rws == C - 1, dlast, 0.0)
            return carry

        lax.fori_loop(0, nch, chunk, 0)
        tri = _chunk_tri(tb, True)
        for hh in range(HB):
            cols = slice(hh * HGRN_DK, (hh + 1) * HGRN_DK)
            dlf = jnp.dot(tri, dcum_scr[hh], precision=HI, preferred_element_type=F32)
            dk = dk_scr[hh]
            lbh, sigh, nsigh, fh = lb[:, cols], sig[:, cols], nsig[:, cols], f[:, cols]
            common = (1.0 - lbh) * sigh * nsigh
            df_ref[:, cols] = (dlf * common / fh - dk * common).astype(BF)
            dq_ref[:, cols] = dq_scr[hh].astype(BF)
            dv_ref[:, cols] = dv_scr[hh].astype(BF)
            dl0 = _colsum(dlf * nsigh / fh - dk * nsigh) * lbh * (1.0 - lbh)
            dl_ref[:, cols] += jnp.where(lax.broadcasted_iota(jnp.int32, (2, HGRN_DK), 0) == 0, dl0, -dl0)

    def zs(part):
        return pl.BlockSpec((tb, HB * HGRN_DK), lambda hd, i: (nblk - 1 - i, part * (H // HB) + hd))

    blk = pl.BlockSpec((tb, HB * HGRN_DK), lambda hd, i: (nblk - 1 - i, hd))
    bft = jax.ShapeDtypeStruct((T, W), BF)
    scr = pltpu.VMEM((HB, tb, HGRN_DK), F32)
    return pl.pallas_call(
        body, name=name, grid=(H // HB, nblk),
        in_specs=[blk, zs(0), zs(1), zs(2), zs(3), blk,
                  pl.BlockSpec((nch, HB, HGRN_DK, HGRN_DK), lambda hd, i: (nblk - 1 - i, hd, 0, 0)),
                  pl.BlockSpec((2, HB * HGRN_DK), lambda hd, i: (0, hd)), pl.BlockSpec((1, HGRN_DK), lambda hd, i: (0, 0))],
        out_specs=[blk, blk, blk, blk, pl.BlockSpec((2, HB * HGRN_DK), lambda hd, i: (0, hd)),
                   pl.BlockSpec((1, HGRN_DK), lambda hd, i: (0, 0))],
        out_shape=[bft, bft, bft, bft, jax.ShapeDtypeStruct((2, W), F32), jax.ShapeDtypeStruct((1, HGRN_DK), F32)],
        scratch_shapes=[pltpu.VMEM((HB, HGRN_DK, HGRN_DK), F32), scr, scr, scr, scr, scr, scr, scr, scr, scr],
        compiler_params=_params(2),
    )(dx, z, z, z, z, o, states, lb_logits, out_norm)


def _log_sigmoid(x):
    return jnp.minimum(x, 0.0) - jnp.log(1.0 + jnp.exp(-jnp.abs(x)))


def _tri(n, upper):
    r = lax.broadcasted_iota(jnp.int32, (n, n), 0)
    c = lax.broadcasted_iota(jnp.int32, (n, n), 1)
    return ((r >= c) if upper else (c <= r)).astype(F32)


def fox_cum_fwd(kvf, fcol, b_row, name):
    T = kvf.shape[0]
    tb = _tile(T, 512)

    def body(x_ref, b_ref, ct_ref, carry_ref):
        @pl.when(pl.program_id(0) == 0)
        def _():
            carry_ref[...] = jnp.zeros_like(carry_ref)

        lf = _log_sigmoid(x_ref[...] + b_ref[...])
        cum = jnp.dot(_tri(tb, False), lf, precision=HI, preferred_element_type=F32) + carry_ref[...]
        carry_ref[...] += _colsum(lf)
        ct_ref[...] = cum.T

    return pl.pallas_call(
        body, name=name, grid=(T // tb,),
        in_specs=[pl.BlockSpec((tb, LANES), lambda i: (i, fcol)), pl.BlockSpec((1, LANES), lambda i: (0, 0))],
        out_specs=pl.BlockSpec((LANES, tb), lambda i: (0, i)), out_shape=jax.ShapeDtypeStruct((LANES, T), F32),
        scratch_shapes=[pltpu.VMEM((1, LANES), F32)], compiler_params=_params(1),
    )(kvf, b_row)


def fox_cum_bwd(dct, kvf, fcol, b_row, name):
    T = kvf.shape[0]
    tb = _tile(T, 512)
    nblk = T // tb

    def body(dc_ref, x_ref, b_ref, df_ref, db_ref, carry_ref):
        @pl.when(pl.program_id(0) == 0)
        def _():
            carry_ref[...] = jnp.zeros_like(carry_ref)
            db_ref[...] = jnp.zeros_like(db_ref)

        dc = dc_ref[...]
        dlf_t = jnp.dot(dc, _tri(tb, True), precision=HI, preferred_element_type=F32) + carry_ref[...]
        carry_ref[...] += jnp.sum(dc, axis=1, keepdims=True)
        x = x_ref[...] + b_ref[...]
        df = dlf_t.T * _sigmoid(-x)
        df_ref[...] = df.astype(BF)
        db_ref[...] += _colsum(df)

    return pl.pallas_call(
        body, name=name, grid=(nblk,),
        in_specs=[pl.BlockSpec((LANES, tb), lambda i: (0, nblk - 1 - i)),
                  pl.BlockSpec((tb, LANES), lambda i: (nblk - 1 - i, fcol)), pl.BlockSpec((1, LANES), lambda i: (0, 0))],
        out_specs=[pl.BlockSpec((tb, LANES), lambda i: (nblk - 1 - i, 0)), pl.BlockSpec((1, LANES), lambda i: (0, 0))],
        out_shape=[jax.ShapeDtypeStruct((T, LANES), BF), jax.ShapeDtypeStruct((1, LANES), F32)],
        scratch_shapes=[pltpu.VMEM((LANES, 1), F32)], compiler_params=_params(1),
    )(dct, kvf, b_row)


NAUG = 3


def fox_prep(kvf, ct, n_fox, tk):
    T = kvf.shape[0]
    W = n_fox * FOX_HEAD_DIM
    NP = n_fox // 2
    k = kvf[:, :W].astype(BF).reshape(T, NP, 2, FOX_HEAD_DIM)
    c = ct[:n_fox].T.reshape(T, NP, 2)
    hi = lax.reduce_precision(c, 8, 7)
    mid = lax.reduce_precision(c - hi, 8, 7)
    lo = c - hi - mid
    aug = jnp.stack([hi, mid, lo], axis=-1).astype(BF)
    pad = jnp.zeros((T, NP, FOX_HEAD_DIM - NAUG), BF)
    ka = jnp.concatenate([k[:, :, 0], aug[:, :, 0], pad], axis=-1).reshape(T, W)
    kb = jnp.concatenate([aug[:, :, 1], pad, k[:, :, 1]], axis=-1).reshape(T, W)
    v = kvf[:, W:2 * W].astype(BF)
    vt3 = v.reshape(T // tk, tk, NP, LANES).transpose(2, 0, 3, 1)
    return ka, kb, vt3


def fox_fwd(qg, ka, kb, vt3, name):
    T = qg.shape[0]
    W = qg.shape[1] // 2
    NP = W // LANES
    tq = tk = vt3.shape[3]
    scale = FOX_HEAD_DIM ** -0.5
    nk = T // tk
    HD = FOX_HEAD_DIM

    def body(q_ref, g_ref, ka_ref, kb_ref, vt_ref, x_ref, o_ref, lse_ref):
        i = pl.program_id(1)
        lane = lax.broadcasted_iota(jnp.int32, (tq, LANES), 1)
        q2 = q_ref[...] * scale
        qa = jnp.where(lane < HD, q2, jnp.where(lane < HD + NAUG, -1.0, 0.0))
        qb = jnp.where(lane >= HD, q2, jnp.where(lane < NAUG, -1.0, 0.0))
        qts = (qa.T.astype(BF), qb.T.astype(BF))
        krow = lax.broadcasted_iota(jnp.int32, (tk, tq), 0)
        qcol = lax.broadcasted_iota(jnp.int32, (tk, tq), 1)

        def step(j, carry, diag):
            rows = pl.ds(pl.multiple_of(j * tk, tk), tk)
            ks = (ka_ref[rows, :], kb_ref[rows, :])
            vt = vt_ref[j]
            sts = [jnp.dot(ks[a], qts[a], preferred_element_type=F32) for a in range(2)]
            pts, mls = [], []
            for a in range(2):
                m, l, _ = carry[a]
                st = sts[a]
                if diag:
                    st = jnp.where(krow + (j * tk - i * tq) <= qcol, st, -1e30)
                mn = jnp.maximum(m, jnp.max(st, axis=0, keepdims=True))
                alpha = jnp.exp(m - mn)
                pt = jnp.exp(st - mn)
                mls.append((mn, l * alpha + jnp.sum(pt, axis=0, keepdims=True), alpha))
                pts.append(pt.astype(BF))
            out = []
            for a in range(2):
                mn, l, alpha = mls[a]
                acc = carry[a][2] * alpha + jnp.dot(vt[a * HD:(a + 1) * HD, :], pts[a], preferred_element_type=F32)
                out.append((mn, l, acc))
            return tuple(out)

        init = (jnp.full((1, tq), -1e30, F32), jnp.zeros((1, tq), F32), jnp.zeros((HD, tq), F32))
        r = tq // tk
        carry = lax.fori_loop(0, i * r, lambda j, c: step(j, c, False), (init, init))
        for u in range(r):
            carry = step(i * r + u, carry, True)
        (ma, la, acca), (mb, lb, accb) = carry
        ot = jnp.concatenate([acca / la, accb / lb], axis=0)
        o = ot.T
        o_ref[...] = o
        lse_ref[0:1, :] = ma + jnp.log(la)
        lse_ref[1:2, :] = mb + jnp.log(lb)
        x_ref[...] = (o * _sigmoid(g_ref[...])).astype(BF)

    blk = pl.BlockSpec((tq, LANES), lambda hp, i: (i, hp))
    full = pl.BlockSpec((T, LANES), lambda hp, i: (0, hp))
    return pl.pallas_call(
        body, name=name, grid=(NP, T // tq),
        in_specs=[blk, pl.BlockSpec((tq, LANES), lambda hp, i: (i, NP + hp)), full, full,
                  pl.BlockSpec((None, nk, LANES, tk), lambda hp, i: (hp, 0, 0, 0))],
        out_specs=[blk, blk, pl.BlockSpec((None, None, 2, tq), lambda hp, i: (hp, i, 0, 0))],
        out_shape=[jax.ShapeDtypeStruct((T, W), BF), jax.ShapeDtypeStruct((T, W), F32),
                   jax.ShapeDtypeStruct((NP, T // tq, 2, tq), F32)],
        compiler_params=_params(2),
    )(qg, qg, ka, kb, vt3)


def fox_gate_bwd(dx, o, qg, name):
    T, W = o.shape
    tm = _tile(T, 512)

    def body(dx_ref, o_ref, g_ref, do_ref, dg_ref, ds_ref):
        dxv = dx_ref[...].astype(F32)
        sg = _sigmoid(g_ref[...])
        do = dxv * sg
        o = o_ref[...]
        do_ref[...] = do
        dg_ref[...] = (dxv * o * sg * (1.0 - sg)).astype(BF)
        head = jnp.right_shift(lax.broadcasted_iota(jnp.int32, (W, LANES), 0), FOX_HEAD_DIM.bit_length() - 1)
        sel = (head == lax.broadcasted_iota(jnp.int32, (W, LANES), 1)).astype(F32)
        ds_ref[...] = jnp.dot(do * o, sel, precision=HI, preferred_element_type=F32)

    tok = pl.BlockSpec((tm, W), lambda i: (i, 0))
    return pl.pallas_call(
        body, name=name, grid=(T // tm,),
        in_specs=[tok, tok, pl.BlockSpec((tm, W), lambda i: (i, 1))],
        out_specs=[tok, tok, pl.BlockSpec((tm, LANES), lambda i: (i, 0))],
        out_shape=[jax.ShapeDtypeStruct((T, W), F32), jax.ShapeDtypeStruct((T, W), BF), jax.ShapeDtypeStruct((T, LANES), F32)],
        compiler_params=_params(1),
    )(dx, o, qg)


def fox_bwd_prep(qg, kvf, do, tq):
    T = qg.shape[0]
    W = qg.shape[1] // 2
    NP = W // LANES
    scale = FOX_HEAD_DIM ** -0.5
    tr3 = lambda a: a.reshape(T // tq, tq, NP, LANES).transpose(2, 0, 3, 1)
    q = (qg[:, :W] * scale).astype(BF)
    dob = do.astype(BF)
    k = kvf[:, :W]
    return q, tr3(q), dob, tr3(dob), tr3((k * scale).astype(BF)), kvf[:, W:2 * W].astype(BF)


def fox_bwd(q, qt3, dob, dot3, kt3, v, ka, kb, lse4, dsum4, name):
    T, W = q.shape
    NP = W // LANES
    tq = tk = qt3.shape[3]
    nq = T // tq
    HD = FOX_HEAD_DIM

    def body(q_ref, qt_ref, do_ref, dot_ref, kt_ref, v_ref, ka_ref, kb_ref, lse_ref, dsum_ref,
             dqt_ref, dk_ref, dv_ref, dc_ref, dcq_ref, dk_scr, dv_scr, dcl_scr):
        j = pl.program_id(1)

        @pl.when(j == 0)
        def _():
            dqt_ref[...] = jnp.zeros_like(dqt_ref)
            dcq_ref[...] = jnp.zeros_like(dcq_ref)

        dk_scr[...] = jnp.zeros_like(dk_scr)
        dv_scr[...] = jnp.zeros_like(dv_scr)
        dcl_scr[...] = jnp.zeros_like(dcl_scr)
        lane = lax.broadcasted_iota(jnp.int32, (tk, LANES), 1)
        srow = lax.broadcasted_iota(jnp.int32, (LANES, tq), 0)
        lanes_of = (lane < HD, lane >= HD)
        v2 = v_ref[...]
        kt2 = kt_ref[...]
        zero = jnp.zeros((), BF)
        vs = [jnp.where(lanes_of[a], v2, zero) for a in range(2)]
        kts = [kt2[a * HD:(a + 1) * HD, :] for a in range(2)]
        kaug = (ka_ref[...], kb_ref[...])
        krow = lax.broadcasted_iota(jnp.int32, (tk, tq), 0)
        qcol = lax.broadcasted_iota(jnp.int32, (tk, tq), 1)
        neg1 = jnp.full((), -1.0, BF)

        def step(i, carry, diag):
            rows = pl.ds(pl.multiple_of(i * tq, tq), tq)
            qt2 = qt_ref[i]
            dot2 = dot_ref[i]
            q2 = q_ref[rows, :]
            do2 = do_ref[rows, :]
            qts = [jnp.where(srow < HD, qt2, jnp.where(srow < HD + NAUG, neg1, zero)),
                   jnp.where(srow >= HD, qt2, jnp.where(srow < NAUG, neg1, zero))]
            sts = [jnp.dot(kaug[a], qts[a], preferred_element_type=F32) for a in range(2)]
            dps = [jnp.dot(vs[a], dot2, preferred_element_type=F32) for a in range(2)]
            pbs, dsbs = [], []
            for a in range(2):
                pt = jnp.exp(sts[a] - lse_ref[i, a:a + 1, :])
                if diag:
                    pt = jnp.where(krow <= qcol, pt, 0.0)
                ds = pt * (dps[a] - dsum_ref[i, a:a + 1, :])
                dcq_ref[i, a:a + 1, :] += _colsum(ds)
                part = ds[:, 0:LANES]
                for u in range(1, tq // LANES):
                    part = part + ds[:, u * LANES:(u + 1) * LANES]
                dcl_scr[a] += part
                pbs.append(pt.astype(BF))
                dsbs.append(ds.astype(BF))
            qn = [jnp.where(lanes_of[a], q2, zero) for a in range(2)]
            don = [jnp.where(lanes_of[a], do2, zero) for a in range(2)]
            dv_scr[...] += (jnp.dot(pbs[0], don[0], preferred_element_type=F32) +
                            jnp.dot(pbs[1], don[1], preferred_element_type=F32))
            dk_scr[...] += (jnp.dot(dsbs[0], qn[0], preferred_element_type=F32) +
                            jnp.dot(dsbs[1], qn[1], preferred_element_type=F32))
            for a in range(2):
                dqt_ref[i, a * HD:(a + 1) * HD, :] += jnp.dot(kts[a], dsbs[a], preferred_element_type=F32)
            return carry

        step(j, 0, True)
        lax.fori_loop(j + 1, nq, lambda i, c: step(i, c, False), 0)
        dk_ref[...] = dk_scr[...].astype(BF)
        dv_ref[...] = dv_scr[...].astype(BF)
        for a in range(2):
            dc_ref[a:a + 1, :] = -_colsum(dcl_scr[a].T)

    tile = pl.BlockSpec((tk, LANES), lambda hp, j: (j, hp))
    full = pl.BlockSpec((T, LANES), lambda hp, j: (0, hp))
    full3 = pl.BlockSpec((None, nq, LANES, tq), lambda hp, j: (hp, 0, 0, 0))
    rows4 = pl.BlockSpec((None, nq, 2, tq), lambda hp, j: (hp, 0, 0, 0))
    bft = jax.ShapeDtypeStruct((T, W), BF)
    r4 = jax.ShapeDtypeStruct((NP, nq, 2, tq), F32)
    return pl.pallas_call(
        body, name=name, grid=(NP, nq),
        in_specs=[full, full3, full, full3, pl.BlockSpec((None, None, LANES, tk), lambda hp, j: (hp, j, 0, 0)),
                  tile, tile, tile, rows4, rows4],
        out_specs=[full3, tile, tile, pl.BlockSpec((None, None, 2, tk), lambda hp, j: (hp, j, 0, 0)), rows4],
        out_shape=[jax.ShapeDtypeStruct((NP, nq, LANES, tq), F32), bft, bft, r4, r4],
        scratch_shapes=[pltpu.VMEM((tk, LANES), F32), pltpu.VMEM((tk, LANES), F32), pltpu.VMEM((2, tk, LANES), F32)],
        compiler_params=_params(2),
    )(q, qt3, dob, dot3, kt3, v, ka, kb, lse4, dsum4)


def loss_fwd_bwd(y, target, name):
    T, D = y.shape
    tm = _tile(T, 512)

    def body(y_ref, t_ref, dy_ref, l_ref):
        @pl.when(pl.program_id(0) == 0)
        def _():
            l_ref[...] = jnp.zeros_like(l_ref)

        d = y_ref[...] - t_ref[...]
        dy_ref[...] = d * (1.0 / D)
        l_ref[...] += 0.5 * jnp.sum(jnp.mean(d * d, axis=-1, keepdims=True), axis=0, keepdims=True)

    tok = pl.BlockSpec((tm, D), lambda i: (i, 0))
    return pl.pallas_call(
        body, name=name, grid=(T // tm,), in_specs=[tok, tok],
        out_specs=[tok, pl.BlockSpec((1, 1), lambda i: (0, 0))],
        out_shape=[jax.ShapeDtypeStruct((T, D), F32), jax.ShapeDtypeStruct((1, 1), F32)], compiler_params=_params(1),
    )(y, target)


def adamw(parts, w, m, v, layer, prev, name):
    L, R, C = w.shape
    tr = _tile(R, 256)
    c1 = 1.0 / (1.0 - ADAM_B1 ** ADAM_STEP)
    c2 = 1.0 / (1.0 - ADAM_B2 ** ADAM_STEP)

    def body(p_ref, w_ref, m_ref, v_ref, *rest):
        g_ref, d_ref, mo_ref, vo_ref = rest[-4:]
        g = p_ref[0].astype(F32)
        for d in range(1, N_DEV):
            g = g + p_ref[d].astype(F32)
        mn = ADAM_B1 * m_ref[...] + (1.0 - ADAM_B1) * g
        vn = ADAM_B2 * v_ref[...] + (1.0 - ADAM_B2) * (g * g)
        g_ref[...] = g
        mo_ref[...] = mn
        vo_ref[...] = vn
        d_ref[...] = -ADAM_LR * ((mn * c1) / (jnp.sqrt(vn * c2) + ADAM_EPS) + ADAM_WD * w_ref[...])

    blk = pl.BlockSpec((None, tr, C), lambda i: (layer, i, 0))
    shp = jax.ShapeDtypeStruct((L, R, C), F32)
    in_specs = [pl.BlockSpec((N_DEV, tr, C), lambda i: (0, i, 0)), blk, blk, blk]
    args = [parts, w, m, v]
    aliases = {}
    if prev is not None:
        in_specs += [pl.BlockSpec(memory_space=pl.ANY)] * 4
        args += list(prev)
        aliases = {4 + j: j for j in range(4)}
    return pl.pallas_call(
        body, name=name, grid=(R // tr,), in_specs=in_specs, out_specs=[blk, blk, blk, blk],
        out_shape=[shp, shp, shp, shp], input_output_aliases=aliases, compiler_params=_params(1),
    )(*args)


HBM_SPEC = pl.BlockSpec(memory_space=pltpu.HBM)
SEM_SPEC = pl.BlockSpec(memory_space=pltpu.SEMAPHORE)
EFFECT = pltpu.SideEffectType.DATAFLOW_SIDE_EFFECTING


def _mesh_pos():
    return lax.axis_index("x"), lax.axis_index("y"), lax.axis_index("c")


def _flip(v, bit):
    return v + bit - 2 * v * bit


def _peer(pos, delta):
    x, y, c = pos
    px, py, pc = _flip(x, (delta >> 2) & 1), _flip(y, (delta >> 1) & 1), _flip(c, delta & 1)
    return (px, py, pc), 4 * px + 2 * py + pc


def _copies(src_refs, land_refs, whole, send, recv, incoming):
    pos = _mesh_pos()
    me = 4 * pos[0] + 2 * pos[1] + pos[2]
    out = []
    for k in range(len(src_refs)):
        for d in range(N_DEV):
            dev, idx = _peer(pos, d)
            j = k * N_DEV + d
            src = src_refs[k] if whole[k] else src_refs[k].at[idx]
            out.append(pltpu.make_async_remote_copy(
                src_ref=src, dst_ref=land_refs[k].at[idx if incoming else me], send_sem=send.at[j], recv_sem=recv.at[j],
                device_id=dev, device_id_type=pl.DeviceIdType.MESH))
    return out


def exchange_start(srcs, lands, whole, name):
    n = len(srcs)

    def body(*refs):
        for copy in _copies(refs[:n], refs[n:2 * n], whole, refs[2 * n], refs[2 * n + 1], False):
            copy.start()
        refs[-1][...] = jnp.zeros_like(refs[-1])

    sems = pltpu.SemaphoreType.DMA((n * N_DEV,))
    thru = [pltpu.HBM(a.shape, a.dtype) for a in list(srcs) + list(lands)]
    res = pl.pallas_call(
        body, name=name, in_specs=[HBM_SPEC] * (2 * n),
        out_specs=[SEM_SPEC, SEM_SPEC] + [HBM_SPEC] * (2 * n) + [pl.BlockSpec(memory_space=pltpu.VMEM)],
        out_shape=[sems, sems] + thru + [jax.ShapeDtypeStruct((8, LANES), F32)],
        input_output_aliases={j: 2 + j for j in range(2 * n)},
        compiler_params=pltpu.CompilerParams(has_side_effects=EFFECT),
    )(*[pltpu.with_memory_space_constraint(a, pltpu.HBM) for a in list(srcs) + list(lands)])
    return dict(send=res[0], recv=res[1], srcs=res[2:2 + n], lands=res[2 + n:2 + 2 * n], whole=whole, token=res[-1])


def exchange_wait(handle, after, name):
    n = len(handle["srcs"])
    whole = handle["whole"]

    def body(*refs):
        for copy in _copies(refs[:n], refs[n:2 * n], whole, refs[2 * n], refs[2 * n + 1], False):
            copy.wait_send()
        for copy in _copies(refs[:n], refs[n:2 * n], whole, refs[2 * n], refs[2 * n + 1], True):
            copy.wait_recv()

    bufs = list(handle["srcs"]) + list(handle["lands"])
    res = pl.pallas_call(
        body, name=name, in_specs=[HBM_SPEC] * (2 * n) + [SEM_SPEC, SEM_SPEC] + [pl.BlockSpec(memory_space=pl.ANY)] * len(after),
        out_specs=[HBM_SPEC] * (2 * n), out_shape=[pltpu.HBM(a.shape, a.dtype) for a in bufs],
        input_output_aliases={j: j for j in range(2 * n)},
        compiler_params=pltpu.CompilerParams(has_side_effects=EFFECT),
    )(*bufs, handle["send"], handle["recv"], *after)
    return list(res[n:])


def _landing(own, whole):
    return lax.empty((N_DEV,) + (own.shape if whole else own.shape[1:]), own.dtype)


def _row(v):
    return v.reshape(1, -1)


def _pad_lanes(v, n):
    return jnp.pad(v, ((0, 0), (0, n - v.shape[1])))


TRANSPOSED = ("ffn1_w_in", "ffn2_w_in")

GATHER_GROUPS = (
    ("ffn1_in_0", (("ffn1_w_in", 0),)),
    ("ffn1_out_0", (("ffn1_w_out", 0),)),
    ("hgrn", (("hgrn_w_in", 0), ("hgrn_w_out", 0))),
    ("rest_0", (("ffn2_w_in", 0), ("ffn2_w_out", 0), ("ple_w_gate", 0), ("ple_w_proj", 0), ("fox_w_kvf", 0))),
    ("ffn1_1", (("ffn1_w_in", 1), ("ffn1_w_out", 1))),
    ("fox", (("fox_w_qg", 0), ("fox_w_out", 0))),
    ("rest_1", (("ffn2_w_in", 1), ("ffn2_w_out", 1), ("ple_w_gate", 1), ("ple_w_proj", 1))),
)


def kernel(x, p, ffn1_norm_pre, ffn1_w_in, ffn1_w_out, ffn1_norm_post, mix_norm_pre, mix_norm_post, ffn2_norm_pre, ffn2_w_in, ffn2_w_out, ffn2_norm_post, hgrn_w_in, hgrn_lb_logits, hgrn_out_norm, hgrn_w_out, kv_norm, fox_w_kvf, fox_b_f, fox_w_qg, fox_w_out, ple_norm_pre, ple_w_gate, ple_w_proj, ple_norm_post, loss_target, m_ffn1_norm_pre, m_ffn1_w_in, m_ffn1_w_out, m_ffn1_norm_post, m_mix_norm_pre, m_mix_norm_post, m_ffn2_norm_pre, m_ffn2_w_in, m_ffn2_w_out, m_ffn2_norm_post, m_hgrn_w_in, m_hgrn_lb_logits, m_hgrn_out_norm, m_hgrn_w_out, m_kv_norm, m_fox_w_kvf, m_fox_b_f, m_fox_w_qg, m_fox_w_out, m_ple_norm_pre, m_ple_w_gate, m_ple_w_proj, m_ple_norm_post, v_ffn1_norm_pre, v_ffn1_w_in, v_ffn1_w_out, v_ffn1_norm_post, v_mix_norm_pre, v_mix_norm_post, v_ffn2_norm_pre, v_ffn2_w_in, v_ffn2_w_out, v_ffn2_norm_post, v_hgrn_w_in, v_hgrn_lb_logits, v_hgrn_out_norm, v_hgrn_w_out, v_kv_norm, v_fox_w_kvf, v_fox_b_f, v_fox_w_qg, v_fox_w_out, v_ple_norm_pre, v_ple_w_gate, v_ple_w_proj, v_ple_norm_post):
    weights = dict(ffn1_norm_pre=ffn1_norm_pre, ffn1_w_in=ffn1_w_in, ffn1_w_out=ffn1_w_out, ffn1_norm_post=ffn1_norm_post, mix_norm_pre=mix_norm_pre, mix_norm_post=mix_norm_post, ffn2_norm_pre=ffn2_norm_pre, ffn2_w_in=ffn2_w_in, ffn2_w_out=ffn2_w_out, ffn2_norm_post=ffn2_norm_post, hgrn_w_in=hgrn_w_in, hgrn_lb_logits=hgrn_lb_logits, hgrn_out_norm=hgrn_out_norm, hgrn_w_out=hgrn_w_out, kv_norm=kv_norm, fox_w_kvf=fox_w_kvf, fox_b_f=fox_b_f, fox_w_qg=fox_w_qg, fox_w_out=fox_w_out, ple_norm_pre=ple_norm_pre, ple_w_gate=ple_w_gate, ple_w_proj=ple_w_proj, ple_norm_post=ple_norm_post)
    mom1 = dict(ffn1_norm_pre=m_ffn1_norm_pre, ffn1_w_in=m_ffn1_w_in, ffn1_w_out=m_ffn1_w_out, ffn1_norm_post=m_ffn1_norm_post, mix_norm_pre=m_mix_norm_pre, mix_norm_post=m_mix_norm_post, ffn2_norm_pre=m_ffn2_norm_pre, ffn2_w_in=m_ffn2_w_in, ffn2_w_out=m_ffn2_w_out, ffn2_norm_post=m_ffn2_norm_post, hgrn_w_in=m_hgrn_w_in, hgrn_lb_logits=m_hgrn_lb_logits, hgrn_out_norm=m_hgrn_out_norm, hgrn_w_out=m_hgrn_w_out, kv_norm=m_kv_norm, fox_w_kvf=m_fox_w_kvf, fox_b_f=m_fox_b_f, fox_w_qg=m_fox_w_qg, fox_w_out=m_fox_w_out, ple_norm_pre=m_ple_norm_pre, ple_w_gate=m_ple_w_gate, ple_w_proj=m_ple_w_proj, ple_norm_post=m_ple_norm_post)
    mom2 = dict(ffn1_norm_pre=v_ffn1_norm_pre, ffn1_w_in=v_ffn1_w_in, ffn1_w_out=v_ffn1_w_out, ffn1_norm_post=v_ffn1_norm_post, mix_norm_pre=v_mix_norm_pre, mix_norm_post=v_mix_norm_post, ffn2_norm_pre=v_ffn2_norm_pre, ffn2_w_in=v_ffn2_w_in, ffn2_w_out=v_ffn2_w_out, ffn2_norm_post=v_ffn2_norm_post, hgrn_w_in=v_hgrn_w_in, hgrn_lb_logits=v_hgrn_lb_logits, hgrn_out_norm=v_hgrn_out_norm, hgrn_w_out=v_hgrn_w_out, kv_norm=v_kv_norm, fox_w_kvf=v_fox_w_kvf, fox_b_f=v_fox_b_f, fox_w_qg=v_fox_w_qg, fox_w_out=v_fox_w_out, ple_norm_pre=v_ple_norm_pre, ple_w_gate=v_ple_w_gate, ple_w_proj=v_ple_w_proj, ple_norm_post=v_ple_norm_post)
    names = list(weights)
    big = ["ffn1_w_in", "ffn1_w_out", "ffn2_w_in", "ffn2_w_out", "hgrn_w_in", "hgrn_w_out", "fox_w_kvf", "fox_w_qg",
           "fox_w_out", "ple_w_gate", "ple_w_proj"]
    small_names = [n for n in names if n not in big]

    T, D = x.shape[1], x.shape[2]
    depth = p.shape[0]
    h0 = x.reshape(T, D)
    target = loss_target.reshape(T, D)
    p3 = p.reshape(depth, T, p.shape[3])
    n_fox = fox_b_f.shape[0]
    fox_w = n_fox * FOX_HEAD_DIM
    fcol = 2 * fox_w // LANES
    b_row = _pad_lanes(_row(fox_b_f), LANES)

    tok = jnp.zeros((), F32)
    handles = {}
    for gname, keys in GATHER_GROUPS:
        shards = []
        for n, l in keys:
            w = weights[n]
            w = w[l] if w.ndim == 3 else w
            shards.append(((w.T if n in TRANSPOSED else w) + tok).astype(BF))
        handles[gname] = exchange_start(shards, [_landing(s, True) for s in shards], [True] * len(keys), f"gather_start_{gname}")
        tok = handles[gname]["token"][0, 0]
    W = {}

    def arrive(gname, after):
        lands = exchange_wait(handles[gname], after, f"gather_wait_{gname}")
        W.update(dict(zip(dict(GATHER_GROUPS)[gname], lands)))

    def w_rows(n, l):
        return W[n, l].reshape(-1, D)

    norm = lambda name, i: weights[name][i:i + 1]

    saved = []
    h = h0
    kvf = ct = kvf_w = None
    tq = _tile(T, 512)
    for i in range(depth):
        s = {}
        if i == 0:
            arrive("ffn1_in_0", [handles[GATHER_GROUPS[-1][0]]["token"]])
        for k in (1, 2):
            if k == 2:
                s["h_a"] = h
                if i == 0:
                    arrive("hgrn", [h])
                    z, xn = norm_mm(h, norm("mix_norm_pre", i), W["hgrn_w_in", 0], "hgrn_in")
                    xm, o, states = hgrn_fwd(z, hgrn_lb_logits, hgrn_out_norm, "hgrn_scan")
                    s.update(z=z, o=o, states=states)
                    wmix = w_rows("hgrn_w_out", 0)
                else:
                    arrive("fox", [h])
                    qg, xn = norm_mm(h, norm("mix_norm_pre", i), W["fox_w_qg", 0], "fox_qg")
                    tqf = _tile(T, 1024)
                    ka, kb, vt3 = fox_prep(kvf, ct, n_fox, tqf)
                    xm, o, lse = fox_fwd(qg, ka, kb, vt3, "fox_attn")
                    lse = lse.reshape(-1, T // tqf, 2, tqf // tq, tq).transpose(0, 1, 3, 2, 4).reshape(-1, T // tq, 2, tq)
                    s.update(qg=qg, o=o, lse=lse, ka=ka, kb=kb)
                    wmix = w_rows("fox_w_out", 0)
                h, ym = mm_norm_res(xm, wmix, norm("mix_norm_post", i), h, 1.0, f"mix_out_{i}", kb=_tile(xm.shape[1], 512))
                s.update(xm=xm, ym=ym, xn_mix=xn)
                arrive(f"rest_{i}", [h])
            gate, up, a, xn = norm_mm_swiglu(h, norm(f"ffn{k}_norm_pre", i), W[f"ffn{k}_w_in", i], f"ffn{k}_in_{i}")
            if (i, k) == (0, 1):
                arrive("ffn1_out_0", [a])
            hn, y = mm_norm_res(a, w_rows(f"ffn{k}_w_out", i), norm(f"ffn{k}_norm_post", i), h, 0.5, f"ffn{k}_out_{i}")
            s[f"ffn{k}"] = (h, gate, up, a, y, xn)
            h = hn
        s["h_c"] = h
        ple_proj = W["ple_w_proj", i].transpose(1, 0, 2).reshape(p.shape[3], D)
        h, pgate, pp = ple_fwd(h, norm("ple_norm_pre", i), w_rows("ple_w_gate", i), p3, ple_proj, norm("ple_norm_post", i),
                               i, f"ple_{i}")
        s.update(pgate=pgate, pp=pp)
        saved.append(s)
        if i == 0:
            kvf_nat = W["fox_w_kvf", 0].transpose(1, 0, 2).reshape(D, -1)
            kvf_cols = kvf_nat.shape[1]
            kvf_w = _pad_lanes(kvf_nat, 2 * fox_w + LANES)
            kvf, xn_kv = norm_mm(h, _row(kv_norm), kvf_w, "fox_kvf", tn=kvf_w.shape[1])
            ct = fox_cum_fwd(kvf, fcol, b_row, "fox_cum")
            h_kv = h
            arrive("ffn1_1", [h])

    dh, loss_part = loss_fwd_bwd(h, target, "loss")
    loss = lax.psum(loss_part[0, 0], ("x", "y", "c"))

    gsmall = {n: [None] * weights[n].shape[0] if weights[n].ndim == 2 else None for n in small_names}
    sent = []

    def send(gname, keys, srcs, whole):
        lands = [_landing(a, w) for a, w in zip(srcs, whole)]
        hd = exchange_start(srcs, lands, whole, f"scatter_start_{gname}")
        sent.append((gname, keys, hd))
        return hd["token"][0:1, 0:1]

    def send_grads(gname, grads):
        return send(gname, list(grads), list(grads.values()), [False] * len(grads))

    for i in reversed(range(depth)):
        s = saved[i]
        grads = {}
        dh, du, dpp, xn, dgpre, dgpost = ple_bwd(dh, s["h_c"], s["pgate"], s["pp"], norm("ple_norm_pre", i),
                                                 w_rows("ple_w_gate", i), norm("ple_norm_post", i), f"ple_bwd_{i}")
        gsmall["ple_norm_pre"][i], gsmall["ple_norm_post"][i] = dgpre, dgpost
        grads["ple_w_gate", i] = mm_tn(xn, du, f"ple_dgate_{i}", xb=_tile(D, 512)).reshape(N_DEV, -1, D)
        dproj = mm_tn(p3, dpp, f"ple_dproj_{i}", x_layer=i, yb=D)[0]
        grads["ple_w_proj", i] = dproj.reshape(dproj.shape[0], N_DEV, -1).transpose(1, 0, 2)
        for k in (2, 1):
            hin, gate, up, a, y, xn = s[f"ffn{k}"]
            ffn_cs = gate.shape[2]
            dy, dgpost, dz = nbwd_mm_nt(dh, y, norm(f"ffn{k}_norm_post", i), w_rows(f"ffn{k}_w_out", i), 0.5, ffn_cs,
                                        f"ffn{k}_bwd_out_{i}", gate=gate, up=up)
            dz = dz.reshape(-1, T, ffn_cs)
            grads[f"ffn{k}_w_out", i] = mm_tn(a, dy, f"ffn{k}_dwout_{i}").reshape(N_DEV, -1, D)
            grads[f"ffn{k}_w_in", i] = mm_tn(dz, xn, f"ffn{k}_dwin_{i}").reshape(N_DEV, ffn_cs, D)
            tokv = send_grads(f"ffn{k}_{i}", grads)
            grads = {}
            dh, dgpre = mm_nt_nbwd(dz, W[f"ffn{k}_w_in", i], hin, norm(f"ffn{k}_norm_pre", i) + tokv, dh, f"ffn{k}_bwd_in_{i}",
                                   transposed=True)
            gsmall[f"ffn{k}_norm_pre"][i], gsmall[f"ffn{k}_norm_post"][i] = dgpre, dgpost
            if k == 2:
                nm = "hgrn" if i == 0 else "fox"
                wmix = w_rows(f"{nm}_w_out", 0)
                dy, dgpost, dxm = nbwd_mm_nt(dh, s["ym"], norm("mix_norm_post", i), wmix, 1.0, _tile(wmix.shape[0], 512),
                                             f"{nm}_bwd_out")
                grads[f"{nm}_w_out", 0] = mm_tn(s["xm"], dy, f"{nm}_dwout", xb=_tile(wmix.shape[0], 512)).reshape(N_DEV, -1, D)
                gsmall["mix_norm_post"][i] = dgpost
                if i == 0:
                    dq, df, dv, dg, dlog, don = hgrn_bwd(dxm, s["z"], s["o"], s["states"], hgrn_lb_logits, hgrn_out_norm,
                                                         "hgrn_scan_bwd")
                    gsmall["hgrn_lb_logits"] = [dlog[0:1], dlog[1:2]]
                    gsmall["hgrn_out_norm"] = [don]
                    dzm = jnp.concatenate([dq, df, dv, dg], axis=1)
                    nmin = "hgrn_w_in"
                else:
                    do, dg, dsum = fox_gate_bwd(dxm, s["o"], s["qg"], "fox_gate_bwd")
                    dsum4 = dsum[:, :n_fox].T.reshape(n_fox // 2, 2, T // tq, tq).transpose(0, 2, 1, 3)
                    dqt, dk_sh, dv_sh, dc4, dcq4 = fox_bwd(*fox_bwd_prep(s["qg"], kvf, do, tq), s["ka"], s["kb"], s["lse"], dsum4,
                                                           "fox_attn_bwd")
                    dq = dqt.transpose(1, 3, 0, 2).reshape(T, fox_w)
                    dzm = jnp.concatenate([dq.astype(BF), dg], axis=1)
                    nmin = "fox_w_qg"
                wmin = W[nmin, 0]
                grads[nmin, 0] = mm_tn(s["xn_mix"], dzm, f"{nm}_dwin", yb=wmin.shape[2])
                tokv = send_grads(f"mix_{i}", grads)
                grads = {}
                dh, dgpre = mm_nt_nbwd(dzm, wmin, s["h_a"], norm("mix_norm_pre", i) + tokv, dh, f"{nm}_bwd_in", tn=wmin.shape[2])
                gsmall["mix_norm_pre"][i] = dgpre
        if i == 1:
            dct = (dc4 + dcq4).transpose(0, 2, 1, 3).reshape(n_fox, T)
            dct = jnp.pad(dct, ((0, LANES - n_fox), (0, 0)))
            dflog, db = fox_cum_bwd(dct, kvf, fcol, b_row, "fox_cum_bwd")
            gsmall["fox_b_f"] = db[:, :n_fox]
            dkvf = jnp.concatenate([dk_sh, dv_sh, dflog], axis=1)
            dwk = mm_tn(xn_kv, dkvf, "fox_dwkvf", yb=kvf_w.shape[1])[0][:, :kvf_cols]
            tokv = send_grads("kvf", {("fox_w_kvf", 0): dwk.reshape(D, N_DEV, -1).transpose(1, 0, 2)})
            dh, dgkv = mm_nt_nbwd(dkvf, kvf_w, h_kv, _row(kv_norm) + tokv, dh, "fox_kvf_bwd", tn=kvf_w.shape[1])
            gsmall["kv_norm"] = dgkv
    grad_x = dh.reshape(x.shape)

    def small_rows(n):
        g = gsmall[n]
        rows = g if isinstance(g, list) else [g]
        return [_pad_lanes(r, D) for r in rows]

    counts = {n: len(small_rows(n)) for n in small_names}
    packed = jnp.concatenate([r for n in small_names for r in small_rows(n)], axis=0)
    n_rows = packed.shape[0]
    packed = jnp.pad(packed, ((0, -n_rows % 8), (0, 0)))
    send("small", ["small"], [packed], [True])

    res = {}
    after = [dh]
    small_parts = None
    for gname, keys, hd in sent:
        lands = exchange_wait(hd, after, f"scatter_wait_{gname}")
        after = []
        for key, parts in zip(keys, lands):
            if key == "small":
                small_parts = parts
                continue
            n, l = key
            w = weights[n]
            if w.ndim == 2:
                as3 = lambda a: a.reshape((1,) + a.shape)
            elif n in TRANSPOSED:
                as3 = lambda a: a.transpose(0, 2, 1)
            else:
                as3 = lambda a: a
            res[n] = adamw(parts, as3(w), as3(mom1[n]), as3(mom2[n]), l, res.get(n), f"adamw_{n}_{l}")
            after.append(res[n][1])
    for n in TRANSPOSED:
        res[n] = [a.transpose(0, 2, 1) for a in res[n]]

    def pack(d):
        rows = []
        for n in small_names:
            a = d[n]
            rows.append(_pad_lanes(a.reshape(-1, a.shape[-1]), D))
        a = jnp.concatenate(rows, axis=0)
        return jnp.pad(a, ((0, -n_rows % 8), (0, 0)))[None]

    sm = adamw(small_parts, pack(weights), pack(mom1), pack(mom2), 0, None, "adamw_small")
    off = 0
    for n in small_names:
        w = weights[n]
        res[n] = [a[0, off:off + counts[n], :w.shape[-1]].reshape(w.shape) for a in sm]
        off += counts[n]

    out = [loss, grad_x]
    for j in range(4):
        out += [res[n][j].reshape(weights[n].shape) for n in names]
    return tuple(out)
```

```python
import functools

import jax
import jax.numpy as jnp
from jax import lax
from jax.experimental import pallas as pl
from jax.experimental.pallas import tpu as pltpu

F32 = jnp.float32
BF = jnp.bfloat16
NORM_EPS = 1e-6
N_DEV = 8
HGRN_DK = 128
HGRN_CHUNK = 16
HGRN_HEADS_PER_STEP = 8
HALF = HGRN_CHUNK // 2
FOX_HEAD_DIM = 64
LANES = 128
ADAM_LR, ADAM_B1, ADAM_B2, ADAM_EPS, ADAM_WD, ADAM_STEP = 0.001, 0.9, 0.999, 1e-08, 0.01, 10
VMEM_LIMIT = 56 * 1024 * 1024
HI = lax.Precision.HIGHEST
NT = (((1,), (1,)), ((), ()))
TN = (((0,), (0,)), ((), ()))


def _params(n_axes):
    return pltpu.CompilerParams(dimension_semantics=("arbitrary",) * n_axes, vmem_limit_bytes=VMEM_LIMIT)


def _tile(n, want):
    t = min(n, want)
    while n % t:
        t //= 2
    return t


def _sigmoid(x):
    return 1.0 / (1.0 + jnp.exp(-x))


def _rms(x):
    r = lax.rsqrt(jnp.mean(x * x, axis=-1, keepdims=True) + NORM_EPS)
    return x * r, r


def _norm_bwd(dy, xhat, r, g):
    dxh = dy * g
    return r * (dxh - xhat * jnp.mean(dxh * xhat, axis=-1, keepdims=True))


def _colsum(x):
    return jnp.sum(x, axis=0, keepdims=True)


def _w_spec(w, blk):
    if w.ndim == 3:
        return lambda off: pl.BlockSpec((None, w.shape[1], w.shape[2]), lambda i, j: (j + off, 0, 0))
    return lambda off: pl.BlockSpec((w.shape[0], blk), lambda i, j: (0, j + off))


def norm_mm_swiglu(h, g, w3, name):
    T, D = h.shape
    nb, cs, _ = w3.shape
    nh = nb // 2
    tm = _tile(T, 1024)

    def body(h_ref, g_ref, wg_ref, wu_ref, gate_ref, up_ref, a_ref, xn_ref):
        @pl.when(pl.program_id(1) == 0)
        def _():
            xh, _ = _rms(h_ref[...])
            xn_ref[...] = (xh * g_ref[...]).astype(BF)

        xn = xn_ref[...]
        gt = lax.dot_general(xn, wg_ref[...], NT, preferred_element_type=F32)
        up = lax.dot_general(xn, wu_ref[...], NT, preferred_element_type=F32)
        sg = _sigmoid(gt)
        silu = gt * sg
        gate_ref[...] = (up * (sg * (1.0 + gt * (1.0 - sg)))).astype(BF)
        up_ref[...] = silu.astype(BF)
        a_ref[...] = (silu * up).astype(BF)

    ws = _w_spec(w3, cs)
    blk = pl.BlockSpec((None, tm, cs), lambda i, j: (j, i, 0))
    shp = jax.ShapeDtypeStruct((nh, T, cs), BF)
    return pl.pallas_call(
        body, name=name, grid=(T // tm, nh),
        in_specs=[pl.BlockSpec((tm, D), lambda i, j: (i, 0)), pl.BlockSpec((1, D), lambda i, j: (0, 0)), ws(0), ws(nh)],
        out_specs=[blk, blk, blk, pl.BlockSpec((tm, D), lambda i, j: (i, 0))],
        out_shape=[shp, shp, shp, jax.ShapeDtypeStruct((T, D), BF)], compiler_params=_params(2),
    )(h, g, w3, w3)


def norm_mm(h, g, w, name, tn=None):
    T, D = h.shape
    if w.ndim == 3:
        nb, cs = w.shape[0], w.shape[2]
    else:
        cs = tn
        nb = w.shape[1] // cs
    tm = _tile(T, 1024)

    def body(h_ref, g_ref, w_ref, z_ref, xn_ref):
        @pl.when(pl.program_id(1) == 0)
        def _():
            xh, _ = _rms(h_ref[...])
            xn_ref[...] = (xh * g_ref[...]).astype(BF)

        z_ref[...] = jnp.dot(xn_ref[...], w_ref[...], preferred_element_type=F32)

    return pl.pallas_call(
        body, name=name, grid=(T // tm, nb),
        in_specs=[pl.BlockSpec((tm, D), lambda i, j: (i, 0)), pl.BlockSpec((1, D), lambda i, j: (0, 0)),
                  _w_spec(w, cs)(0)],
        out_specs=[pl.BlockSpec((tm, cs), lambda i, j: (i, j)), pl.BlockSpec((tm, D), lambda i, j: (i, 0))],
        out_shape=[jax.ShapeDtypeStruct((T, nb * cs), F32), jax.ShapeDtypeStruct((T, D), BF)], compiler_params=_params(2),
    )(h, g, w)


def _x_spec(x, tm, kb):
    if x.ndim == 3:
        return pl.BlockSpec((None, tm, x.shape[2]), lambda i, j: (j, i, 0))
    return pl.BlockSpec((tm, kb), lambda i, j: (i, j))


def mm_norm_res(x, w2, g, h, coef, name, kb=None):
    T, D = h.shape
    if x.ndim == 3:
        nb, kb = x.shape[0], x.shape[2]
    else:
        nb = x.shape[1] // kb
    tm = _tile(T, 1024)

    def body(x_ref, w_ref, h_ref, g_ref, hn_ref, y_ref, acc_ref):
        b = pl.program_id(1)

        @pl.when(b == 0)
        def _():
            acc_ref[...] = jnp.zeros_like(acc_ref)

        acc_ref[...] += jnp.dot(x_ref[...], w_ref[...], preferred_element_type=F32)

        @pl.when(b == nb - 1)
        def _():
            y = acc_ref[...]
            y_ref[...] = y
            yh, _ = _rms(y)
            hn_ref[...] = h_ref[...] + coef * (yh * g_ref[...])

    tok = pl.BlockSpec((tm, D), lambda i, j: (i, 0))
    shp = jax.ShapeDtypeStruct((T, D), F32)
    return pl.pallas_call(
        body, name=name, grid=(T // tm, nb),
        in_specs=[_x_spec(x, tm, kb), pl.BlockSpec((kb, D), lambda i, j: (j, 0)), tok,
                  pl.BlockSpec((1, D), lambda i, j: (0, 0))],
        out_specs=[tok, tok], out_shape=[shp, shp],
        scratch_shapes=[pltpu.VMEM((tm, D), F32)], compiler_params=_params(2),
    )(x, w2, h, g)


def nbwd_mm_nt(dout, y, g, w2, coef, kb, name, gate=None, up=None):
    T, D = dout.shape
    nb = w2.shape[0] // kb
    swiglu = gate is not None
    tm = _tile(T, 1024 if swiglu else 512)

    def body(*refs):
        if swiglu:
            dout_ref, y_ref, g_ref, w_ref, gate_ref, up_ref, dy_ref, dg_ref, da_ref, dys_ref = refs
        else:
            dout_ref, y_ref, g_ref, w_ref, dy_ref, dg_ref, da_ref, dys_ref = refs
        i, b = pl.program_id(0), pl.program_id(1)

        @pl.when((i == 0) & (b == 0))
        def _():
            dg_ref[...] = jnp.zeros_like(dg_ref)

        @pl.when(b == 0)
        def _():
            yh, r = _rms(y_ref[...])
            dyn = coef * dout_ref[...]
            dg_ref[...] += _colsum(dyn * yh)
            dy = _norm_bwd(dyn, yh, r, g_ref[...]).astype(BF)
            dys_ref[...] = dy
            dy_ref[...] = dy

        da = lax.dot_general(dys_ref[...], w_ref[...], NT, preferred_element_type=F32)
        if swiglu:
            da_ref[0] = (da * gate_ref[...].astype(F32)).astype(BF)
            da_ref[1] = (da * up_ref[...].astype(F32)).astype(BF)
        else:
            da_ref[...] = da.astype(BF)

    tok = pl.BlockSpec((tm, D), lambda i, j: (i, 0))
    vec = pl.BlockSpec((1, D), lambda i, j: (0, 0))
    in_specs = [tok, tok, vec, pl.BlockSpec((kb, D), lambda i, j: (j, 0))]
    args = [dout, y, g, w2]
    if swiglu:
        blk = pl.BlockSpec((None, tm, kb), lambda i, j: (j, i, 0))
        in_specs += [blk, blk]
        args += [gate, up]
        da_spec = pl.BlockSpec((2, None, tm, kb), lambda i, j: (0, j, i, 0))
        da_shape = jax.ShapeDtypeStruct((2, nb, T, kb), BF)
    else:
        da_spec = pl.BlockSpec((tm, kb), lambda i, j: (i, j))
        da_shape = jax.ShapeDtypeStruct((T, nb * kb), BF)
    return pl.pallas_call(
        body, name=name, grid=(T // tm, nb), in_specs=in_specs,
        out_specs=[tok, vec, da_spec],
        out_shape=[jax.ShapeDtypeStruct((T, D), BF), jax.ShapeDtypeStruct((1, D), F32), da_shape],
        scratch_shapes=[pltpu.VMEM((tm, D), BF)], compiler_params=_params(2),
    )(*args)


def mm_nt_nbwd(dz, w, h, g, dout, name, tn=None, transposed=False):
    T, D = h.shape
    if w.ndim == 3:
        nb, cs = w.shape[0], w.shape[1 if transposed else 2]
    else:
        cs = tn
        nb = w.shape[1] // cs
    tm = _tile(T, 1024)

    def body(dz_ref, w_ref, h_ref, g_ref, dout_ref, dh_ref, dg_ref, acc_ref):
        i, b = pl.program_id(0), pl.program_id(1)

        @pl.when((i == 0) & (b == 0))
        def _():
            dg_ref[...] = jnp.zeros_like(dg_ref)

        @pl.when(b == 0)
        def _():
            acc_ref[...] = jnp.zeros_like(acc_ref)

        if transposed:
            acc_ref[...] += jnp.dot(dz_ref[...], w_ref[...], preferred_element_type=F32)
        else:
            acc_ref[...] += lax.dot_general(dz_ref[...], w_ref[...], NT, preferred_element_type=F32)

        @pl.when(b == nb - 1)
        def _():
            xh, r = _rms(h_ref[...])
            gg = g_ref[...]
            dxn = acc_ref[...]
            dg_ref[...] += _colsum(dxn * xh)
            dh_ref[...] = dout_ref[...] + _norm_bwd(dxn, xh, r, gg)

    tok = pl.BlockSpec((tm, D), lambda i, j: (i, 0))
    vec = pl.BlockSpec((1, D), lambda i, j: (0, 0))
    return pl.pallas_call(
        body, name=name, grid=(T // tm, nb),
        in_specs=[_x_spec(dz, tm, cs), _w_spec(w, cs)(0), tok, vec, tok],
        out_specs=[tok, vec],
        out_shape=[jax.ShapeDtypeStruct((T, D), F32), jax.ShapeDtypeStruct((1, D), F32)],
        scratch_shapes=[pltpu.VMEM((tm, D), F32)], compiler_params=_params(2),
    )(dz, w, h, g, dout)


def mm_tn(x, y, name, xb=None, yb=None, x_layer=None):
    T = y.shape[-2]
    wide = (yb if yb is not None else y.shape[-1]) > 1024
    tt = _tile(T, 1024 if wide else 2048)
    x_split = (x.ndim == 3 and x_layer is None) or xb is not None
    if x_layer is not None:
        xs = pl.BlockSpec((None, tt, x.shape[2]), lambda b, t: (x_layer, t, 0))
        kdim = x.shape[2]
    elif x.ndim == 3:
        xs = pl.BlockSpec((None, tt, x.shape[2]), lambda b, t: (b, t, 0))
        nb, kdim = x.shape[0], x.shape[2]
    elif xb is not None:
        xs = pl.BlockSpec((tt, xb), lambda b, t: (t, b))
        nb, kdim = x.shape[1] // xb, xb
    else:
        xs = pl.BlockSpec((tt, x.shape[1]), lambda b, t: (t, 0))
        kdim = x.shape[1]
    if x_split:
        ys = pl.BlockSpec((tt, y.shape[1]), lambda b, t: (t, 0))
        ndim = y.shape[1]
        out_spec = pl.BlockSpec((kdim, ndim), lambda b, t: (b, 0))
        out_shape = jax.ShapeDtypeStruct((nb * kdim, ndim), BF)
    else:
        if y.ndim == 3:
            ys = pl.BlockSpec((None, tt, y.shape[2]), lambda b, t: (b, t, 0))
            nb, ndim = y.shape[0], y.shape[2]
        else:
            ys = pl.BlockSpec((tt, yb), lambda b, t: (t, b))
            nb, ndim = y.shape[1] // yb, yb
        out_spec = pl.BlockSpec((None, kdim, ndim), lambda b, t: (b, 0, 0))
        out_shape = jax.ShapeDtypeStruct((nb, kdim, ndim), BF)
    nt = T // tt

    def body(x_ref, y_ref, o_ref, acc_ref):
        t = pl.program_id(1)

        @pl.when(t == 0)
        def _():
            acc_ref[...] = jnp.zeros_like(acc_ref)

        acc_ref[...] += lax.dot_general(x_ref[...].astype(BF), y_ref[...].astype(BF), TN, preferred_element_type=F32)

        @pl.when(t == nt - 1)
        def _():
            o_ref[...] = acc_ref[...].astype(BF)

    return pl.pallas_call(
        body, name=name, grid=(nb, nt), in_specs=[xs, ys], out_specs=out_spec, out_shape=out_shape,
        scratch_shapes=[pltpu.VMEM((kdim, ndim), F32)], compiler_params=_params(2),
    )(x, y)


def ple_fwd(h, gpre, wg, p3, wp, gpost, layer, name):
    T, D = h.shape
    pd = p3.shape[2]
    tm = _tile(T, 512)

    def body(h_ref, gpre_ref, wg_ref, p_ref, wp_ref, gpost_ref, hn_ref, gate_ref, pp_ref):
        x = h_ref[...]
        xh, _ = _rms(x)
        u = jnp.dot((xh * gpre_ref[...]).astype(BF), wg_ref[...], preferred_element_type=F32)
        gate = _sigmoid(u)
        pp = jnp.dot(p_ref[...].astype(BF), wp_ref[...], preferred_element_type=F32)
        yh, _ = _rms(gate * pp)
        hn_ref[...] = x + yh * gpost_ref[...]
        gate_ref[...] = gate.astype(BF)
        pp_ref[...] = pp.astype(BF)

    tok = pl.BlockSpec((tm, D), lambda i: (i, 0))
    vec = pl.BlockSpec((1, D), lambda i: (0, 0))
    return pl.pallas_call(
        body, name=name, grid=(T // tm,),
        in_specs=[tok, vec, pl.BlockSpec((D, D), lambda i: (0, 0)),
                  pl.BlockSpec((None, tm, pd), lambda i: (layer, i, 0)),
                  pl.BlockSpec((pd, D), lambda i: (0, 0)), vec],
        out_specs=[tok, tok, tok],
        out_shape=[jax.ShapeDtypeStruct((T, D), F32), jax.ShapeDtypeStruct((T, D), BF), jax.ShapeDtypeStruct((T, D), BF)],
        compiler_params=_params(1),
    )(h, gpre, wg, p3, wp, gpost)


def ple_bwd(dout, h, gate, pp, gpre, wg, gpost, name):
    T, D = h.shape
    tm = _tile(T, 512)

    def body(dout_ref, h_ref, gate_ref, pp_ref, gpre_ref, wg_ref, gpost_ref, dh_ref, du_ref, dpp_ref, xn_ref, dgpre_ref, dgpost_ref):
        @pl.when(pl.program_id(0) == 0)
        def _():
            dgpre_ref[...] = jnp.zeros_like(dgpre_ref)
            dgpost_ref[...] = jnp.zeros_like(dgpost_ref)

        dout = dout_ref[...]
        gate = gate_ref[...].astype(F32)
        pp = pp_ref[...].astype(F32)
        yh, ry = _rms(gate * pp)
        dgpost_ref[...] += _colsum(dout * yh)
        dy = _norm_bwd(dout, yh, ry, gpost_ref[...])
        dpp_ref[...] = (dy * gate).astype(BF)
        du = (dy * pp * gate * (1.0 - gate)).astype(BF)
        du_ref[...] = du
        dxn = lax.dot_general(du, wg_ref[...], NT, preferred_element_type=F32)
        xh, r = _rms(h_ref[...])
        gp = gpre_ref[...]
        dgpre_ref[...] += _colsum(dxn * xh)
        dh_ref[...] = dout + _norm_bwd(dxn, xh, r, gp)
        xn_ref[...] = (xh * gp).astype(BF)

    tok = pl.BlockSpec((tm, D), lambda i: (i, 0))
    vec = pl.BlockSpec((1, D), lambda i: (0, 0))
    bft = jax.ShapeDtypeStruct((T, D), BF)
    v32 = jax.ShapeDtypeStruct((1, D), F32)
    return pl.pallas_call(
        body, name=name, grid=(T // tm,),
        in_specs=[tok, tok, tok, tok, vec, pl.BlockSpec((D, D), lambda i: (0, 0)), vec],
        out_specs=[tok, tok, tok, tok, vec, vec],
        out_shape=[jax.ShapeDtypeStruct((T, D), F32), bft, bft, bft, v32, v32],
        compiler_params=_params(1),
    )(dout, h, gate, pp, gpre, wg, gpost)


def _chunk_tri(tb, upper):
    r = lax.broadcasted_iota(jnp.int32, (tb, tb), 0)
    c = lax.broadcasted_iota(jnp.int32, (tb, tb), 1)
    shift = HGRN_CHUNK.bit_length() - 1
    same = jnp.right_shift(r, shift) == jnp.right_shift(c, shift)
    return (same & ((c >= r) if upper else (c <= r))).astype(F32)


def _hgrn_gates(z, logits):
    lb = 1.0 / (1.0 + jnp.exp(logits[1:2, :] - logits[0:1, :]))
    e = jnp.exp(-jnp.abs(z))
    inv = 1.0 / (1.0 + e)
    sig = jnp.where(z >= 0, inv, e * inv)
    nsig = jnp.where(z >= 0, e * inv, inv)
    return lb, sig, nsig, lb + (1.0 - lb) * sig


def hgrn_fwd(z, lb_logits, out_norm, name):
    T = z.shape[0]
    W = z.shape[1] // 4
    H = W // HGRN_DK
    C = HGRN_CHUNK
    HB = _tile(H, HGRN_HEADS_PER_STEP)
    tb = _tile(T, 256)
    nch = tb // C

    def body(zq_ref, zf_ref, zv_ref, zg_ref, lbl_ref, on_ref, x_ref, o_ref, st_ref, s_scr, cum_scr, k_scr, v_scr, o_scr):
        @pl.when(pl.program_id(1) == 0)
        def _():
            s_scr[...] = jnp.zeros_like(s_scr)

        lb, sig, nsig, f = _hgrn_gates(zf_ref[...], lbl_ref[...])
        cum = jnp.dot(_chunk_tri(tb, False), jnp.log(f), precision=HI, preferred_element_type=F32)
        kk = (1.0 - lb) * nsig
        for hh in range(HB):
            cols = slice(hh * HGRN_DK, (hh + 1) * HGRN_DK)
            cum_scr[hh] = cum[:, cols]
            k_scr[hh] = kk[:, cols]
            v_scr[hh] = zv_ref[:, cols]
        row = lax.broadcasted_iota(jnp.int32, (C, HGRN_DK), 0)

        def chunk(c, carry):
            r0 = pl.multiple_of(c * C, C)
            rows = pl.ds(r0, C)
            last_row = pl.ds(r0 + C - 1, 1)
            heads = []
            for hh in range(HB):
                cols = slice(hh * HGRN_DK, (hh + 1) * HGRN_DK)
                q, cu = zq_ref[rows, cols], cum_scr[hh, rows, :]
                st = s_scr[hh]
                st_ref[c, hh] = st
                o = lax.dot_general((q * jnp.exp(cu)).astype(BF), st.astype(BF), NT, preferred_element_type=F32)
                last = cum_scr[hh, last_row, :]
                kg = (k_scr[hh, rows, :] * jnp.exp(last - cu)).astype(BF)
                s_scr[hh] = st * jnp.exp(last) + lax.dot_general(v_scr[hh, rows, :].astype(BF), kg, TN, preferred_element_type=F32)
                heads.append((hh, q, cu, o))
            for hh, q, cu, o in heads:
                qr = q.astype(BF).astype(F32)
                low = jnp.zeros((C - HALF, HGRN_DK), F32)
                for s in range(C):
                    one = pl.ds(r0 + s, 1)
                    sl = slice(0 if s < HALF else HALF, C)
                    e = jnp.exp(jnp.minimum(cu[sl] - cum_scr[hh, one, :], 0.0))
                    col = jnp.sum(qr[sl] * (e * k_scr[hh, one, :]).astype(BF).astype(F32), axis=-1, keepdims=True)
                    col = jnp.where(row[sl] >= s, col, 0.0).astype(BF).astype(F32)
                    term = col * v_scr[hh, one, :].astype(BF).astype(F32)
                    if s < HALF:
                        o = o + term
                    else:
                        low = low + term
                o_scr[hh, rows, :] = o
                o_scr[hh, pl.ds(r0 + HALF, C - HALF), :] += low
            return carry

        lax.fori_loop(0, nch, chunk, 0)
        for hh in range(HB):
            cols = slice(hh * HGRN_DK, (hh + 1) * HGRN_DK)
            o = o_scr[hh]
            o_ref[:, cols] = o
            oh, _ = _rms(o)
            g = zg_ref[:, cols]
            x_ref[:, cols] = (oh * on_ref[...] * (g * _sigmoid(g))).astype(BF)

    def zs(part):
        return pl.BlockSpec((tb, HB * HGRN_DK), lambda hd, i: (i, part * (H // HB) + hd))

    blk = pl.BlockSpec((tb, HB * HGRN_DK), lambda hd, i: (i, hd))
    wide = pltpu.VMEM((HB, tb, HGRN_DK), F32)
    return pl.pallas_call(
        body, name=name, grid=(H // HB, T // tb),
        in_specs=[zs(0), zs(1), zs(2), zs(3), pl.BlockSpec((2, HB * HGRN_DK), lambda hd, i: (0, hd)),
                  pl.BlockSpec((1, HGRN_DK), lambda hd, i: (0, 0))],
        out_specs=[blk, blk, pl.BlockSpec((nch, HB, HGRN_DK, HGRN_DK), lambda hd, i: (i, hd, 0, 0))],
        out_shape=[jax.ShapeDtypeStruct((T, W), BF), jax.ShapeDtypeStruct((T, W), F32),
                   jax.ShapeDtypeStruct((T // C, H, HGRN_DK, HGRN_DK), F32)],
        scratch_shapes=[pltpu.VMEM((HB, HGRN_DK, HGRN_DK), F32), wide, wide, wide, wide],
        compiler_params=_params(2),
    )(z, z, z, z, lb_logits, out_norm)


def hgrn_bwd(dx, z, o, states, lb_logits, out_norm, name):
    T = z.shape[0]
    W = z.shape[1] // 4
    H = W // HGRN_DK
    C = HGRN_CHUNK
    HB = _tile(H, HGRN_HEADS_PER_STEP)
    tb = _tile(T, 256)
    nch = tb // C
    nblk = T // tb

    def body(dx_ref, zq_ref, zf_ref, zv_ref, zg_ref, o_ref, st_ref, lbl_ref, on_ref,
             dq_ref, df_ref, dv_ref, dg_ref, dl_ref, don_ref,
             ds_scr, cum_scr, k_scr, v_scr, q_scr, do_scr, dq_scr, dk_scr, dv_scr, dcum_scr):
        hd, i = pl.program_id(0), pl.program_id(1)

        @pl.when(i == 0)
        def _():
            ds_scr[...] = jnp.zeros_like(ds_scr)
            dl_ref[...] = jnp.zeros_like(dl_ref)

        @pl.when((i == 0) & (hd == 0))
        def _():
            don_ref[...] = jnp.zeros_like(don_ref)

        lb, sig, nsig, f = _hgrn_gates(zf_ref[...], lbl_ref[...])
        cum = jnp.dot(_chunk_tri(tb, False), jnp.log(f), precision=HI, preferred_element_type=F32)
        kk = (1.0 - lb) * nsig
        w = on_ref[...]
        for hh in range(HB):
            cols = slice(hh * HGRN_DK, (hh + 1) * HGRN_DK)
            cum_scr[hh] = cum[:, cols]
            k_scr[hh] = kk[:, cols]
            v_scr[hh] = zv_ref[:, cols]
            q_scr[hh] = zq_ref[:, cols]
            oh, r = _rms(o_ref[:, cols])
            g = zg_ref[:, cols]
            sg = _sigmoid(g)
            dxv = dx_ref[:, cols].astype(F32)
            dg_ref[:, cols] = (dxv * (oh * w) * (sg * (1.0 + g * (1.0 - sg)))).astype(BF)
            don = dxv * (g * sg)
            don_ref[...] += _colsum(don * oh)
            do_scr[hh] = _norm_bwd(don, oh, r, w)
        row = lax.broadcasted_iota(jnp.int32, (C, HGRN_DK), 0)

        def chunk(cc, carry):
            c = nch - 1 - cc
            r0 = pl.multiple_of(c * C, C)
            rows = pl.ds(r0, C)
            last_row = pl.ds(r0 + C - 1, 1)
            heads = []
            for hh in range(HB):
                q, k, v, cu, do = q_scr[hh, rows, :], k_scr[hh, rows, :], v_scr[hh, rows, :], cum_scr[hh, rows, :], do_scr[hh, rows, :]
                st = st_ref[c, hh]
                dst = ds_scr[hh]
                last = cum_scr[hh, last_row, :]
                lam, gam, elast = jnp.exp(cu), jnp.exp(last - cu), jnp.exp(last)
                dob, dstb = do.astype(BF), dst.astype(BF)
                dq = jnp.dot(dob, st.astype(BF), preferred_element_type=F32) * lam
                dv = lax.dot_general((k * gam).astype(BF), dstb, NT, preferred_element_type=F32)
                dk = jnp.dot(v.astype(BF), dstb, preferred_element_type=F32) * gam
                dlast = elast * _colsum(dst * st) + _colsum(dk * k)
                ds_scr[hh] = dst * elast + lax.dot_general(dob, (q * lam).astype(BF), TN, preferred_element_type=F32)
                heads.append((hh, q, k, cu, do, dq, dk, dv, dlast))
            for hh, q, k, cu, do, dq, dk, dv, dlast in heads:
                for first in (True, False):
                    sl = slice(0 if first else HALF, C)
                    qs, cus, dos, rws = q[sl], cu[sl], do[sl], row[sl]
                    dqs, dks, dvs = dq[sl], dk[sl], dv[sl]
                    for s in (range(HALF) if first else range(HALF, C)):
                        one = pl.ds(r0 + s, 1)
                        e = jnp.where(rws >= s, jnp.exp(jnp.minimum(cus - cum_scr[hh, one, :], 0.0)), 0.0)
                        ks = k_scr[hh, one, :]
                        da = jnp.sum(dos * v_scr[hh, one, :], axis=-1, keepdims=True)
                        pq = qs * e
                        a = jnp.sum(pq * ks, axis=-1, keepdims=True)
                        dqs = dqs + da * e * ks
                        dks = jnp.where(rws == s, dks + _colsum(da * pq), dks)
                        dvs = jnp.where(rws == s, dvs + _colsum(a * dos), dvs)
                    if first:
                        dq, dk, dv = dqs, dks, dvs
                        top = pl.ds(r0, HALF)
                        dq_scr[hh, top, :] = dq[:HALF]
                        dk_scr[hh, top, :] = dk[:HALF]
                        dv_scr[hh, top, :] = dv[:HALF]
                        dcum_scr[hh, top, :] = q[:HALF] * dq[:HALF] - k[:HALF] * dk[:HALF]
                    else:
                        low = pl.ds(r0 + HALF, C - HALF)
                        dq_scr[hh, low, :] = dqs
                        dk_scr[hh, low, :] = dks
                        dv_scr[hh, low, :] = dvs
                        dcum_scr[hh, low, :] = qs * dqs - k[sl] * dks + jnp.where(rws == C - 1, dlast, 0.0)
            return carry

        lax.fori_loop(0, nch, chunk, 0)
        tri = _chunk_tri(tb, True)
        for hh in range(HB):
            cols = slice(hh * HGRN_DK, (hh + 1) * HGRN_DK)
            dlf = jnp.dot(tri, dcum_scr[hh], precision=HI, preferred_element_type=F32)
            dk = dk_scr[hh]
            lbh, sigh, nsigh, fh = lb[:, cols], sig[:, cols], nsig[:, cols], f[:, cols]
            common = (1.0 - lbh) * sigh * nsigh
            df_ref[:, cols] = (dlf * common / fh - dk * common).astype(BF)
            dq_ref[:, cols] = dq_scr[hh].astype(BF)
            dv_ref[:, cols] = dv_scr[hh].astype(BF)
            dl0 = _colsum(dlf * nsigh / fh - dk * nsigh) * lbh * (1.0 - lbh)
            dl_ref[:, cols] += jnp.where(lax.broadcasted_iota(jnp.int32, (2, HGRN_DK), 0) == 0, dl0, -dl0)

    def zs(part):
        return pl.BlockSpec((tb, HB * HGRN_DK), lambda hd, i: (nblk - 1 - i, part * (H // HB) + hd))

    blk = pl.BlockSpec((tb, HB * HGRN_DK), lambda hd, i: (nblk - 1 - i, hd))
    bft = jax.ShapeDtypeStruct((T, W), BF)
    scr = pltpu.VMEM((HB, tb, HGRN_DK), F32)
    return pl.pallas_call(
        body, name=name, grid=(H // HB, nblk),
        in_specs=[blk, zs(0), zs(1), zs(2), zs(3), blk,
                  pl.BlockSpec((nch, HB, HGRN_DK, HGRN_DK), lambda hd, i: (nblk - 1 - i, hd, 0, 0)),
                  pl.BlockSpec((2, HB * HGRN_DK), lambda hd, i: (0, hd)), pl.BlockSpec((1, HGRN_DK), lambda hd, i: (0, 0))],
        out_specs=[blk, blk, blk, blk, pl.BlockSpec((2, HB * HGRN_DK), lambda hd, i: (0, hd)),
                   pl.BlockSpec((1, HGRN_DK), lambda hd, i: (0, 0))],
        out_shape=[bft, bft, bft, bft, jax.ShapeDtypeStruct((2, W), F32), jax.ShapeDtypeStruct((1, HGRN_DK), F32)],
        scratch_shapes=[pltpu.VMEM((HB, HGRN_DK, HGRN_DK), F32), scr, scr, scr, scr, scr, scr, scr, scr, scr],
        compiler_params=_params(2),
    )(dx, z, z, z, z, o, states, lb_logits, out_norm)


def _log_sigmoid(x):
    return jnp.minimum(x, 0.0) - jnp.log(1.0 + jnp.exp(-jnp.abs(x)))


def _tri(n, upper):
    r = lax.broadcasted_iota(jnp.int32, (n, n), 0)
    c = lax.broadcasted_iota(jnp.int32, (n, n), 1)
    return ((r >= c) if upper else (c <= r)).astype(F32)


def fox_cum_fwd(kvf, fcol, b_row, name):
    T = kvf.shape[0]
    tb = _tile(T, 512)

    def body(x_ref, b_ref, ct_ref, carry_ref):
        @pl.when(pl.program_id(0) == 0)
        def _():
            carry_ref[...] = jnp.zeros_like(carry_ref)

        lf = _log_sigmoid(x_ref[...] + b_ref[...])
        cum = jnp.dot(_tri(tb, False), lf, precision=HI, preferred_element_type=F32) + carry_ref[...]
        carry_ref[...] += _colsum(lf)
        ct_ref[...] = cum.T

    return pl.pallas_call(
        body, name=name, grid=(T // tb,),
        in_specs=[pl.BlockSpec((tb, LANES), lambda i: (i, fcol)), pl.BlockSpec((1, LANES), lambda i: (0, 0))],
        out_specs=pl.BlockSpec((LANES, tb), lambda i: (0, i)), out_shape=jax.ShapeDtypeStruct((LANES, T), F32),
        scratch_shapes=[pltpu.VMEM((1, LANES), F32)], compiler_params=_params(1),
    )(kvf, b_row)


def fox_cum_bwd(dct, kvf, fcol, b_row, name):
    T = kvf.shape[0]
    tb = _tile(T, 512)
    nblk = T // tb

    def body(dc_ref, x_ref, b_ref, df_ref, db_ref, carry_ref):
        @pl.when(pl.program_id(0) == 0)
        def _():
            carry_ref[...] = jnp.zeros_like(carry_ref)
            db_ref[...] = jnp.zeros_like(db_ref)

        dc = dc_ref[...]
        dlf_t = jnp.dot(dc, _tri(tb, True), precision=HI, preferred_element_type=F32) + carry_ref[...]
        carry_ref[...] += jnp.sum(dc, axis=1, keepdims=True)
        x = x_ref[...] + b_ref[...]
        df = dlf_t.T * _sigmoid(-x)
        df_ref[...] = df.astype(BF)
        db_ref[...] += _colsum(df)

    return pl.pallas_call(
        body, name=name, grid=(nblk,),
        in_specs=[pl.BlockSpec((LANES, tb), lambda i: (0, nblk - 1 - i)),
                  pl.BlockSpec((tb, LANES), lambda i: (nblk - 1 - i, fcol)), pl.BlockSpec((1, LANES), lambda i: (0, 0))],
        out_specs=[pl.BlockSpec((tb, LANES), lambda i: (nblk - 1 - i, 0)), pl.BlockSpec((1, LANES), lambda i: (0, 0))],
        out_shape=[jax.ShapeDtypeStruct((T, LANES), BF), jax.ShapeDtypeStruct((1, LANES), F32)],
        scratch_shapes=[pltpu.VMEM((LANES, 1), F32)], compiler_params=_params(1),
    )(dct, kvf, b_row)


NAUG = 3


def fox_prep(kvf, ct, n_fox, tk):
    T = kvf.shape[0]
    W = n_fox * FOX_HEAD_DIM
    NP = n_fox // 2
    k = kvf[:, :W].astype(BF).reshape(T, NP, 2, FOX_HEAD_DIM)
    c = ct[:n_fox].T.reshape(T, NP, 2)
    hi = lax.reduce_precision(c, 8, 7)
    mid = lax.reduce_precision(c - hi, 8, 7)
    lo = c - hi - mid
    aug = jnp.stack([hi, mid, lo], axis=-1).astype(BF)
    pad = jnp.zeros((T, NP, FOX_HEAD_DIM - NAUG), BF)
    ka = jnp.concatenate([k[:, :, 0], aug[:, :, 0], pad], axis=-1).reshape(T, W)
    kb = jnp.concatenate([aug[:, :, 1], pad, k[:, :, 1]], axis=-1).reshape(T, W)
    v = kvf[:, W:2 * W].astype(BF)
    vt3 = v.reshape(T // tk, tk, NP, LANES).transpose(2, 0, 3, 1)
    return ka, kb, vt3


def fox_fwd(qg, ka, kb, vt3, name):
    T = qg.shape[0]
    W = qg.shape[1] // 2
    NP = W // LANES
    tq = tk = vt3.shape[3]
    scale = FOX_HEAD_DIM ** -0.5
    nk = T // tk
    HD = FOX_HEAD_DIM

    def body(q_ref, g_ref, ka_ref, kb_ref, vt_ref, x_ref, o_ref, lse_ref):
        i = pl.program_id(1)
        lane = lax.broadcasted_iota(jnp.int32, (tq, LANES), 1)
        q2 = q_ref[...] * scale
        qa = jnp.where(lane < HD, q2, jnp.where(lane < HD + NAUG, -1.0, 0.0))
        qb = jnp.where(lane >= HD, q2, jnp.where(lane < NAUG, -1.0, 0.0))
        qts = (qa.T.astype(BF), qb.T.astype(BF))
        krow = lax.broadcasted_iota(jnp.int32, (tk, tq), 0)
        qcol = lax.broadcasted_iota(jnp.int32, (tk, tq), 1)

        def step(j, carry, diag):
            rows = pl.ds(pl.multiple_of(j * tk, tk), tk)
            ks = (ka_ref[rows, :], kb_ref[rows, :])
            vt = vt_ref[j]
            sts = [jnp.dot(ks[a], qts[a], preferred_element_type=F32) for a in range(2)]
            pts, mls = [], []
            for a in range(2):
                m, l, _ = carry[a]
                st = sts[a]
                if diag:
                    st = jnp.where(krow + (j * tk - i * tq) <= qcol, st, -1e30)
                mn = jnp.maximum(m, jnp.max(st, axis=0, keepdims=True))
                alpha = jnp.exp(m - mn)
                pt = jnp.exp(st - mn)
                mls.append((mn, l * alpha + jnp.sum(pt, axis=0, keepdims=True), alpha))
                pts.append(pt.astype(BF))
            out = []
            for a in range(2):
                mn, l, alpha = mls[a]
                acc = carry[a][2] * alpha + jnp.dot(vt[a * HD:(a + 1) * HD, :], pts[a], preferred_element_type=F32)
                out.append((mn, l, acc))
            return tuple(out)

        init = (jnp.full((1, tq), -1e30, F32), jnp.zeros((1, tq), F32), jnp.zeros((HD, tq), F32))
        r = tq // tk
        carry = lax.fori_loop(0, i * r, lambda j, c: step(j, c, False), (init, init))
        for u in range(r):
            carry = step(i * r + u, carry, True)
        (ma, la, acca), (mb, lb, accb) = carry
        ot = jnp.concatenate([acca / la, accb / lb], axis=0)
        o = ot.T
        o_ref[...] = o
        lse_ref[0:1, :] = ma + jnp.log(la)
        lse_ref[1:2, :] = mb + jnp.log(lb)
        x_ref[...] = (o * _sigmoid(g_ref[...])).astype(BF)

    blk = pl.BlockSpec((tq, LANES), lambda hp, i: (i, hp))
    full = pl.BlockSpec((T, LANES), lambda hp, i: (0, hp))
    return pl.pallas_call(
        body, name=name, grid=(NP, T // tq),
        in_specs=[blk, pl.BlockSpec((tq, LANES), lambda hp, i: (i, NP + hp)), full, full,
                  pl.BlockSpec((None, nk, LANES, tk), lambda hp, i: (hp, 0, 0, 0))],
        out_specs=[blk, blk, pl.BlockSpec((None, None, 2, tq), lambda hp, i: (hp, i, 0, 0))],
        out_shape=[jax.ShapeDtypeStruct((T, W), BF), jax.ShapeDtypeStruct((T, W), F32),
                   jax.ShapeDtypeStruct((NP, T // tq, 2, tq), F32)],
        compiler_params=_params(2),
    )(qg, qg, ka, kb, vt3)


def fox_gate_bwd(dx, o, qg, name):
    T, W = o.shape
    tm = _tile(T, 512)

    def body(dx_ref, o_ref, g_ref, do_ref, dg_ref, ds_ref):
        dxv = dx_ref[...].astype(F32)
        sg = _sigmoid(g_ref[...])
        do = dxv * sg
        o = o_ref[...]
        do_ref[...] = do
        dg_ref[...] = (dxv * o * sg * (1.0 - sg)).astype(BF)
        head = jnp.right_shift(lax.broadcasted_iota(jnp.int32, (W, LANES), 0), FOX_HEAD_DIM.bit_length() - 1)
        sel = (head == lax.broadcasted_iota(jnp.int32, (W, LANES), 1)).astype(F32)
        ds_ref[...] = jnp.dot(do * o, sel, precision=HI, preferred_element_type=F32)

    tok = pl.BlockSpec((tm, W), lambda i: (i, 0))
    return pl.pallas_call(
        body, name=name, grid=(T // tm,),
        in_specs=[tok, tok, pl.BlockSpec((tm, W), lambda i: (i, 1))],
        out_specs=[tok, tok, pl.BlockSpec((tm, LANES), lambda i: (i, 0))],
        out_shape=[jax.ShapeDtypeStruct((T, W), F32), jax.ShapeDtypeStruct((T, W), BF), jax.ShapeDtypeStruct((T, LANES), F32)],
        compiler_params=_params(1),
    )(dx, o, qg)


def fox_bwd_prep(qg, kvf, do, tq):
    T = qg.shape[0]
    W = qg.shape[1] // 2
    NP = W // LANES
    scale = FOX_HEAD_DIM ** -0.5
    tr3 = lambda a: a.reshape(T // tq, tq, NP, LANES).transpose(2, 0, 3, 1)
    q = (qg[:, :W] * scale).astype(BF)
    dob = do.astype(BF)
    k = kvf[:, :W]
    return q, tr3(q), dob, tr3(dob), tr3((k * scale).astype(BF)), kvf[:, W:2 * W].astype(BF)


def fox_bwd(q, qt3, dob, dot3, kt3, v, ka, kb, lse4, dsum4, name):
    T, W = q.shape
    NP = W // LANES
    tq = tk = qt3.shape[3]
    nq = T // tq
    HD = FOX_HEAD_DIM

    def body(q_ref, qt_ref, do_ref, dot_ref, kt_ref, v_ref, ka_ref, kb_ref, lse_ref, dsum_ref,
             dqt_ref, dk_ref, dv_ref, dc_ref, dcq_ref, dk_scr, dv_scr, dcl_scr):
        j = pl.program_id(1)

        @pl.when(j == 0)
        def _():
            dqt_ref[...] = jnp.zeros_like(dqt_ref)
            dcq_ref[...] = jnp.zeros_like(dcq_ref)

        dk_scr[...] = jnp.zeros_like(dk_scr)
        dv_scr[...] = jnp.zeros_like(dv_scr)
        dcl_scr[...] = jnp.zeros_like(dcl_scr)
        lane = lax.broadcasted_iota(jnp.int32, (tk, LANES), 1)
        srow = lax.broadcasted_iota(jnp.int32, (LANES, tq), 0)
        lanes_of = (lane < HD, lane >= HD)
        v2 = v_ref[...]
        kt2 = kt_ref[...]
        zero = jnp.zeros((), BF)
        vs = [jnp.where(lanes_of[a], v2, zero) for a in range(2)]
        kts = [kt2[a * HD:(a + 1) * HD, :] for a in range(2)]
        kaug = (ka_ref[...], kb_ref[...])
        krow = lax.broadcasted_iota(jnp.int32, (tk, tq), 0)
        qcol = lax.broadcasted_iota(jnp.int32, (tk, tq), 1)
        neg1 = jnp.full((), -1.0, BF)

        def step(i, carry, diag):
            rows = pl.ds(pl.multiple_of(i * tq, tq), tq)
            qt2 = qt_ref[i]
            dot2 = dot_ref[i]
            q2 = q_ref[rows, :]
            do2 = do_ref[rows, :]
            qts = [jnp.where(srow < HD, qt2, jnp.where(srow < HD + NAUG, neg1, zero)),
                   jnp.where(srow >= HD, qt2, jnp.where(srow < NAUG, neg1, zero))]
            sts = [jnp.dot(kaug[a], qts[a], preferred_element_type=F32) for a in range(2)]
            dps = [jnp.dot(vs[a], dot2, preferred_element_type=F32) for a in range(2)]
            pbs, dsbs = [], []
            for a in range(2):
                pt = jnp.exp(sts[a] - lse_ref[i, a:a + 1, :])
                if diag:
                    pt = jnp.where(krow <= qcol, pt, 0.0)
                ds = pt * (dps[a] - dsum_ref[i, a:a + 1, :])
                dcq_ref[i, a:a + 1, :] += _colsum(ds)
                part = ds[:, 0:LANES]
                for u in range(1, tq // LANES):
                    part = part + ds[:, u * LANES:(u + 1) * LANES]
                dcl_scr[a] += part
                pbs.append(pt.astype(BF))
                dsbs.append(ds.astype(BF))
            qn = [jnp.where(lanes_of[a], q2, zero) for a in range(2)]
            don = [jnp.where(lanes_of[a], do2, zero) for a in range(2)]
            dv_scr[...] += (jnp.dot(pbs[0], don[0], preferred_element_type=F32) +
                            jnp.dot(pbs[1], don[1], preferred_element_type=F32))
            dk_scr[...] += (jnp.dot(dsbs[0], qn[0], preferred_element_type=F32) +
                            jnp.dot(dsbs[1], qn[1], preferred_element_type=F32))
            for a in range(2):
                dqt_ref[i, a * HD:(a + 1) * HD, :] += jnp.dot(kts[a], dsbs[a], preferred_element_type=F32)
            return carry

        step(j, 0, True)
        lax.fori_loop(j + 1, nq, lambda i, c: step(i, c, False), 0)
        dk_ref[...] = dk_scr[...].astype(BF)
        dv_ref[...] = dv_scr[...].astype(BF)
        for a in range(2):
            dc_ref[a:a + 1, :] = -_colsum(dcl_scr[a].T)

    tile = pl.BlockSpec((tk, LANES), lambda hp, j: (j, hp))
    full = pl.BlockSpec((T, LANES), lambda hp, j: (0, hp))
    full3 = pl.BlockSpec((None, nq, LANES, tq), lambda hp, j: (hp, 0, 0, 0))
    rows4 = pl.BlockSpec((None, nq, 2, tq), lambda hp, j: (hp, 0, 0, 0))
    bft = jax.ShapeDtypeStruct((T, W), BF)
    r4 = jax.ShapeDtypeStruct((NP, nq, 2, tq), F32)
    return pl.pallas_call(
        body, name=name, grid=(NP, nq),
        in_specs=[full, full3, full, full3, pl.BlockSpec((None, None, LANES, tk), lambda hp, j: (hp, j, 0, 0)),
                  tile, tile, tile, rows4, rows4],
        out_specs=[full3, tile, tile, pl.BlockSpec((None, None, 2, tk), lambda hp, j: (hp, j, 0, 0)), rows4],
        out_shape=[jax.ShapeDtypeStruct((NP, nq, LANES, tq), F32), bft, bft, r4, r4],
        scratch_shapes=[pltpu.VMEM((tk, LANES), F32), pltpu.VMEM((tk, LANES), F32), pltpu.VMEM((2, tk, LANES), F32)],
        compiler_params=_params(2),
    )(q, qt3, dob, dot3, kt3, v, ka, kb, lse4, dsum4)


def loss_fwd_bwd(y, target, name):
    T, D = y.shape
    tm = _tile(T, 512)

    def body(y_ref, t_ref, dy_ref, l_ref):
        @pl.when(pl.program_id(0) == 0)
        def _():
            l_ref[...] = jnp.zeros_like(l_ref)

        d = y_ref[...] - t_ref[...]
        dy_ref[...] = d * (1.0 / D)
        l_ref[...] += 0.5 * jnp.sum(jnp.mean(d * d, axis=-1, keepdims=True), axis=0, keepdims=True)

    tok = pl.BlockSpec((tm, D), lambda i: (i, 0))
    return pl.pallas_call(
        body, name=name, grid=(T // tm,), in_specs=[tok, tok],
        out_specs=[tok, pl.BlockSpec((1, 1), lambda i: (0, 0))],
        out_shape=[jax.ShapeDtypeStruct((T, D), F32), jax.ShapeDtypeStruct((1, 1), F32)], compiler_params=_params(1),
    )(y, target)


def adamw(parts, w, m, v, layer, prev, name):
    L, R, C = w.shape
    tr = _tile(R, 256)
    c1 = 1.0 / (1.0 - ADAM_B1 ** ADAM_STEP)
    c2 = 1.0 / (1.0 - ADAM_B2 ** ADAM_STEP)

    def body(p_ref, w_ref, m_ref, v_ref, *rest):
        g_ref, d_ref, mo_ref, vo_ref = rest[-4:]
        g = p_ref[0].astype(F32)
        for d in range(1, N_DEV):
            g = g + p_ref[d].astype(F32)
        mn = ADAM_B1 * m_ref[...] + (1.0 - ADAM_B1) * g
        vn = ADAM_B2 * v_ref[...] + (1.0 - ADAM_B2) * (g * g)
        g_ref[...] = g
        mo_ref[...] = mn
        vo_ref[...] = vn
        d_ref[...] = -ADAM_LR * ((mn * c1) / (jnp.sqrt(vn * c2) + ADAM_EPS) + ADAM_WD * w_ref[...])

    blk = pl.BlockSpec((None, tr, C), lambda i: (layer, i, 0))
    shp = jax.ShapeDtypeStruct((L, R, C), F32)
    in_specs = [pl.BlockSpec((N_DEV, tr, C), lambda i: (0, i, 0)), blk, blk, blk]
    args = [parts, w, m, v]
    aliases = {}
    if prev is not None:
        in_specs += [pl.BlockSpec(memory_space=pl.ANY)] * 4
        args += list(prev)
        aliases = {4 + j: j for j in range(4)}
    return pl.pallas_call(
        body, name=name, grid=(R // tr,), in_specs=in_specs, out_specs=[blk, blk, blk, blk],
        out_shape=[shp, shp, shp, shp], input_output_aliases=aliases, compiler_params=_params(1),
    )(*args)


HBM_SPEC = pl.BlockSpec(memory_space=pltpu.HBM)
SEM_SPEC = pl.BlockSpec(memory_space=pltpu.SEMAPHORE)
EFFECT = pltpu.SideEffectType.DATAFLOW_SIDE_EFFECTING


def _mesh_pos():
    return lax.axis_index("x"), lax.axis_index("y"), lax.axis_index("c")


def _flip(v, bit):
    return v + bit - 2 * v * bit


def _peer(pos, delta):
    x, y, c = pos
    px, py, pc = _flip(x, (delta >> 2) & 1), _flip(y, (delta >> 1) & 1), _flip(c, delta & 1)
    return (px, py, pc), 4 * px + 2 * py + pc


def _copies(src_refs, land_refs, whole, send, recv, incoming):
    pos = _mesh_pos()
    me = 4 * pos[0] + 2 * pos[1] + pos[2]
    out = []
    for k in range(len(src_refs)):
        for d in range(N_DEV):
            dev, idx = _peer(pos, d)
            j = k * N_DEV + d
            src = src_refs[k] if whole[k] else src_refs[k].at[idx]
            out.append(pltpu.make_async_remote_copy(
                src_ref=src, dst_ref=land_refs[k].at[idx if incoming else me], send_sem=send.at[j], recv_sem=recv.at[j],
                device_id=dev, device_id_type=pl.DeviceIdType.MESH))
    return out


def exchange_start(srcs, lands, whole, name):
    n = len(srcs)

    def body(*refs):
        for copy in _copies(refs[:n], refs[n:2 * n], whole, refs[2 * n], refs[2 * n + 1], False):
            copy.start()
        refs[-1][...] = jnp.zeros_like(refs[-1])

    sems = pltpu.SemaphoreType.DMA((n * N_DEV,))
    thru = [pltpu.HBM(a.shape, a.dtype) for a in list(srcs) + list(lands)]
    res = pl.pallas_call(
        body, name=name, in_specs=[HBM_SPEC] * (2 * n),
        out_specs=[SEM_SPEC, SEM_SPEC] + [HBM_SPEC] * (2 * n) + [pl.BlockSpec(memory_space=pltpu.VMEM)],
        out_shape=[sems, sems] + thru + [jax.ShapeDtypeStruct((8, LANES), F32)],
        input_output_aliases={j: 2 + j for j in range(2 * n)},
        compiler_params=pltpu.CompilerParams(has_side_effects=EFFECT),
    )(*[pltpu.with_memory_space_constraint(a, pltpu.HBM) for a in list(srcs) + list(lands)])
    return dict(send=res[0], recv=res[1], srcs=res[2:2 + n], lands=res[2 + n:2 + 2 * n], whole=whole, token=res[-1])


def exchange_wait(handle, after, name):
    n = len(handle["srcs"])
    whole = handle["whole"]

    def body(*refs):
        for copy in _copies(refs[:n], refs[n:2 * n], whole, refs[2 * n], refs[2 * n + 1], False):
            copy.wait_send()
        for copy in _copies(refs[:n], refs[n:2 * n], whole, refs[2 * n], refs[2 * n + 1], True):
            copy.wait_recv()

    bufs = list(handle["srcs"]) + list(handle["lands"])
    res = pl.pallas_call(
        body, name=name, in_specs=[HBM_SPEC] * (2 * n) + [SEM_SPEC, SEM_SPEC] + [pl.BlockSpec(memory_space=pl.ANY)] * len(after),
        out_specs=[HBM_SPEC] * (2 * n), out_shape=[pltpu.HBM(a.shape, a.dtype) for a in bufs],
        input_output_aliases={j: j for j in range(2 * n)},
        compiler_params=pltpu.CompilerParams(has_side_effects=EFFECT),
    )(*bufs, handle["send"], handle["recv"], *after)
    return list(res[n:])


def _landing(own, whole):
    return lax.empty((N_DEV,) + (own.shape if whole else own.shape[1:]), own.dtype)


def _row(v):
    return v.reshape(1, -1)


def _pad_lanes(v, n):
    return jnp.pad(v, ((0, 0), (0, n - v.shape[1])))


TRANSPOSED = ("ffn1_w_in", "ffn2_w_in")

GATHER_GROUPS = (
    ("ffn1_in_0", (("ffn1_w_in", 0),)),
    ("ffn1_out_0", (("ffn1_w_out", 0),)),
    ("hgrn", (("hgrn_w_in", 0), ("hgrn_w_out", 0))),
    ("rest_0", (("ffn2_w_in", 0), ("ffn2_w_out", 0), ("ple_w_gate", 0), ("ple_w_proj", 0), ("fox_w_kvf", 0))),
    ("ffn1_1", (("ffn1_w_in", 1), ("ffn1_w_out", 1))),
    ("fox", (("fox_w_qg", 0), ("fox_w_out", 0))),
    ("rest_1", (("ffn2_w_in", 1), ("ffn2_w_out", 1), ("ple_w_gate", 1), ("ple_w_proj", 1))),
)


def kernel(x, p, ffn1_norm_pre, ffn1_w_in, ffn1_w_out, ffn1_norm_post, mix_norm_pre, mix_norm_post, ffn2_norm_pre, ffn2_w_in, ffn2_w_out, ffn2_norm_post, hgrn_w_in, hgrn_lb_logits, hgrn_out_norm, hgrn_w_out, kv_norm, fox_w_kvf, fox_b_f, fox_w_qg, fox_w_out, ple_norm_pre, ple_w_gate, ple_w_proj, ple_norm_post, loss_target, m_ffn1_norm_pre, m_ffn1_w_in, m_ffn1_w_out, m_ffn1_norm_post, m_mix_norm_pre, m_mix_norm_post, m_ffn2_norm_pre, m_ffn2_w_in, m_ffn2_w_out, m_ffn2_norm_post, m_hgrn_w_in, m_hgrn_lb_logits, m_hgrn_out_norm, m_hgrn_w_out, m_kv_norm, m_fox_w_kvf, m_fox_b_f, m_fox_w_qg, m_fox_w_out, m_ple_norm_pre, m_ple_w_gate, m_ple_w_proj, m_ple_norm_post, v_ffn1_norm_pre, v_ffn1_w_in, v_ffn1_w_out, v_ffn1_norm_post, v_mix_norm_pre, v_mix_norm_post, v_ffn2_norm_pre, v_ffn2_w_in, v_ffn2_w_out, v_ffn2_norm_post, v_hgrn_w_in, v_hgrn_lb_logits, v_hgrn_out_norm, v_hgrn_w_out, v_kv_norm, v_fox_w_kvf, v_fox_b_f, v_fox_w_qg, v_fox_w_out, v_ple_norm_pre, v_ple_w_gate, v_ple_w_proj, v_ple_norm_post):
    weights = dict(ffn1_norm_pre=ffn1_norm_pre, ffn1_w_in=ffn1_w_in, ffn1_w_out=ffn1_w_out, ffn1_norm_post=ffn1_norm_post, mix_norm_pre=mix_norm_pre, mix_norm_post=mix_norm_post, ffn2_norm_pre=ffn2_norm_pre, ffn2_w_in=ffn2_w_in, ffn2_w_out=ffn2_w_out, ffn2_norm_post=ffn2_norm_post, hgrn_w_in=hgrn_w_in, hgrn_lb_logits=hgrn_lb_logits, hgrn_out_norm=hgrn_out_norm, hgrn_w_out=hgrn_w_out, kv_norm=kv_norm, fox_w_kvf=fox_w_kvf, fox_b_f=fox_b_f, fox_w_qg=fox_w_qg, fox_w_out=fox_w_out, ple_norm_pre=ple_norm_pre, ple_w_gate=ple_w_gate, ple_w_proj=ple_w_proj, ple_norm_post=ple_norm_post)
    mom1 = dict(ffn1_norm_pre=m_ffn1_norm_pre, ffn1_w_in=m_ffn1_w_in, ffn1_w_out=m_ffn1_w_out, ffn1_norm_post=m_ffn1_norm_post, mix_norm_pre=m_mix_norm_pre, mix_norm_post=m_mix_norm_post, ffn2_norm_pre=m_ffn2_norm_pre, ffn2_w_in=m_ffn2_w_in, ffn2_w_out=m_ffn2_w_out, ffn2_norm_post=m_ffn2_norm_post, hgrn_w_in=m_hgrn_w_in, hgrn_lb_logits=m_hgrn_lb_logits, hgrn_out_norm=m_hgrn_out_norm, hgrn_w_out=m_hgrn_w_out, kv_norm=m_kv_norm, fox_w_kvf=m_fox_w_kvf, fox_b_f=m_fox_b_f, fox_w_qg=m_fox_w_qg, fox_w_out=m_fox_w_out, ple_norm_pre=m_ple_norm_pre, ple_w_gate=m_ple_w_gate, ple_w_proj=m_ple_w_proj, ple_norm_post=m_ple_norm_post)
    mom2 = dict(ffn1_norm_pre=v_ffn1_norm_pre, ffn1_w_in=v_ffn1_w_in, ffn1_w_out=v_ffn1_w_out, ffn1_norm_post=v_ffn1_norm_post, mix_norm_pre=v_mix_norm_pre, mix_norm_post=v_mix_norm_post, ffn2_norm_pre=v_ffn2_norm_pre, ffn2_w_in=v_ffn2_w_in, ffn2_w_out=v_ffn2_w_out, ffn2_norm_post=v_ffn2_norm_post, hgrn_w_in=v_hgrn_w_in, hgrn_lb_logits=v_hgrn_lb_logits, hgrn_out_norm=v_hgrn_out_norm, hgrn_w_out=v_hgrn_w_out, kv_norm=v_kv_norm, fox_w_kvf=v_fox_w_kvf, fox_b_f=v_fox_b_f, fox_w_qg=v_fox_w_qg, fox_w_out=v_fox_w_out, ple_norm_pre=v_ple_norm_pre, ple_w_gate=v_ple_w_gate, ple_w_proj=v_ple_w_proj, ple_norm_post=v_ple_norm_post)
    names = list(weights)
    big = ["ffn1_w_in", "ffn1_w_out", "ffn2_w_in", "ffn2_w_out", "hgrn_w_in", "hgrn_w_out", "fox_w_kvf", "fox_w_qg",
           "fox_w_out", "ple_w_gate", "ple_w_proj"]
    small_names = [n for n in names if n not in big]

    T, D = x.shape[1], x.shape[2]
    depth = p.shape[0]
    h0 = x.reshape(T, D)
    target = loss_target.reshape(T, D)
    p3 = p.reshape(depth, T, p.shape[3])
    n_fox = fox_b_f.shape[0]
    fox_w = n_fox * FOX_HEAD_DIM
    fcol = 2 * fox_w // LANES
    b_row = _pad_lanes(_row(fox_b_f), LANES)

    tok = jnp.zeros((), F32)
    handles = {}
    for gname, keys in GATHER_GROUPS:
        shards = []
        for n, l in keys:
            w = weights[n]
            w = w[l] if w.ndim == 3 else w
            shards.append(((w.T if n in TRANSPOSED else w) + tok).astype(BF))
        handles[gname] = exchange_start(shards, [_landing(s, True) for s in shards], [True] * len(keys), f"gather_start_{gname}")
        tok = handles[gname]["token"][0, 0]
    W = {}

    def arrive(gname, after):
        lands = exchange_wait(handles[gname], after, f"gather_wait_{gname}")
        W.update(dict(zip(dict(GATHER_GROUPS)[gname], lands)))

    def w_rows(n, l):
        return W[n, l].reshape(-1, D)

    norm = lambda name, i: weights[name][i:i + 1]

    saved = []
    h = h0
    kvf = ct = kvf_w = None
    tq = _tile(T, 512)
    for i in range(depth):
        s = {}
        if i == 0:
            arrive("ffn1_in_0", [handles[GATHER_GROUPS[-1][0]]["token"]])
        for k in (1, 2):
            if k == 2:
                s["h_a"] = h
                if i == 0:
                    arrive("hgrn", [h])
                    z, xn = norm_mm(h, norm("mix_norm_pre", i), W["hgrn_w_in", 0], "hgrn_in")
                    xm, o, states = hgrn_fwd(z, hgrn_lb_logits, hgrn_out_norm, "hgrn_scan")
                    s.update(z=z, o=o, states=states)
                    wmix = w_rows("hgrn_w_out", 0)
                else:
                    arrive("fox", [h])
                    qg, xn = norm_mm(h, norm("mix_norm_pre", i), W["fox_w_qg", 0], "fox_qg")
                    tqf = _tile(T, 1024)
                    ka, kb, vt3 = fox_prep(kvf, ct, n_fox, tqf)
                    xm, o, lse = fox_fwd(qg, ka, kb, vt3, "fox_attn")
                    lse = lse.reshape(-1, T // tqf, 2, tqf // tq, tq).transpose(0, 1, 3, 2, 4).reshape(-1, T // tq, 2, tq)
                    s.update(qg=qg, o=o, lse=lse, ka=ka, kb=kb)
                    wmix = w_rows("fox_w_out", 0)
                h, ym = mm_norm_res(xm, wmix, norm("mix_norm_post", i), h, 1.0, f"mix_out_{i}", kb=_tile(xm.shape[1], 512))
                s.update(xm=xm, ym=ym, xn_mix=xn)
                arrive(f"rest_{i}", [h])
            gate, up, a, xn = norm_mm_swiglu(h, norm(f"ffn{k}_norm_pre", i), W[f"ffn{k}_w_in", i], f"ffn{k}_in_{i}")
            if (i, k) == (0, 1):
                arrive("ffn1_out_0", [a])
            hn, y = mm_norm_res(a, w_rows(f"ffn{k}_w_out", i), norm(f"ffn{k}_norm_post", i), h, 0.5, f"ffn{k}_out_{i}")
            s[f"ffn{k}"] = (h, gate, up, a, y, xn)
            h = hn
        s["h_c"] = h
        ple_proj = W["ple_w_proj", i].transpose(1, 0, 2).reshape(p.shape[3], D)
        h, pgate, pp = ple_fwd(h, norm("ple_norm_pre", i), w_rows("ple_w_gate", i), p3, ple_proj, norm("ple_norm_post", i),
                               i, f"ple_{i}")
        s.update(pgate=pgate, pp=pp)
        saved.append(s)
        if i == 0:
            kvf_nat = W["fox_w_kvf", 0].transpose(1, 0, 2).reshape(D, -1)
            kvf_cols = kvf_nat.shape[1]
            kvf_w = _pad_lanes(kvf_nat, 2 * fox_w + LANES)
            kvf, xn_kv = norm_mm(h, _row(kv_norm), kvf_w, "fox_kvf", tn=kvf_w.shape[1])
            ct = fox_cum_fwd(kvf, fcol, b_row, "fox_cum")
            h_kv = h
            arrive("ffn1_1", [h])

    dh, loss_part = loss_fwd_bwd(h, target, "loss")

    gsmall = {n: [None] * weights[n].shape[0] if weights[n].ndim == 2 else None for n in small_names}
    sent = []

    def send(gname, keys, srcs, whole):
        lands = [_landing(a, w) for a, w in zip(srcs, whole)]
        hd = exchange_start(srcs, lands, whole, f"scatter_start_{gname}")
        sent.append((gname, keys, hd))
        return hd["token"][0:1, 0:1]

    def send_grads(gname, grads):
        return send(gname, list(grads), list(grads.values()), [False] * len(grads))

    for i in reversed(range(depth)):
        s = saved[i]
        grads = {}
        dh, du, dpp, xn, dgpre, dgpost = ple_bwd(dh, s["h_c"], s["pgate"], s["pp"], norm("ple_norm_pre", i),
                                                 w_rows("ple_w_gate", i), norm("ple_norm_post", i), f"ple_bwd_{i}")
        gsmall["ple_norm_pre"][i], gsmall["ple_norm_post"][i] = dgpre, dgpost
        grads["ple_w_gate", i] = mm_tn(xn, du, f"ple_dgate_{i}", xb=_tile(D, 512)).reshape(N_DEV, -1, D)
        dproj = mm_tn(p3, dpp, f"ple_dproj_{i}", x_layer=i, yb=D)[0]
        grads["ple_w_proj", i] = dproj.reshape(dproj.shape[0], N_DEV, -1).transpose(1, 0, 2)
        for k in (2, 1):
            hin, gate, up, a, y, xn = s[f"ffn{k}"]
            ffn_cs = gate.shape[2]
            dy, dgpost, dz = nbwd_mm_nt(dh, y, norm(f"ffn{k}_norm_post", i), w_rows(f"ffn{k}_w_out", i), 0.5, ffn_cs,
                                        f"ffn{k}_bwd_out_{i}", gate=gate, up=up)
            dz = dz.reshape(-1, T, ffn_cs)
            grads[f"ffn{k}_w_out", i] = mm_tn(a, dy, f"ffn{k}_dwout_{i}").reshape(N_DEV, -1, D)
            grads[f"ffn{k}_w_in", i] = mm_tn(dz, xn, f"ffn{k}_dwin_{i}").reshape(N_DEV, ffn_cs, D)
            tokv = send_grads(f"ffn{k}_{i}", grads)
            grads = {}
            dh, dgpre = mm_nt_nbwd(dz, W[f"ffn{k}_w_in", i], hin, norm(f"ffn{k}_norm_pre", i) + tokv, dh, f"ffn{k}_bwd_in_{i}",
                                   transposed=True)
            gsmall[f"ffn{k}_norm_pre"][i], gsmall[f"ffn{k}_norm_post"][i] = dgpre, dgpost
            if k == 2:
                nm = "hgrn" if i == 0 else "fox"
                wmix = w_rows(f"{nm}_w_out", 0)
                dy, dgpost, dxm = nbwd_mm_nt(dh, s["ym"], norm("mix_norm_post", i), wmix, 1.0, _tile(wmix.shape[0], 512),
                                             f"{nm}_bwd_out")
                grads[f"{nm}_w_out", 0] = mm_tn(s["xm"], dy, f"{nm}_dwout", xb=_tile(wmix.shape[0], 512)).reshape(N_DEV, -1, D)
                gsmall["mix_norm_post"][i] = dgpost
                if i == 0:
                    dq, df, dv, dg, dlog, don = hgrn_bwd(dxm, s["z"], s["o"], s["states"], hgrn_lb_logits, hgrn_out_norm,
                                                         "hgrn_scan_bwd")
                    gsmall["hgrn_lb_logits"] = [dlog[0:1], dlog[1:2]]
                    gsmall["hgrn_out_norm"] = [don]
                    dzm = jnp.concatenate([dq, df, dv, dg], axis=1)
                    nmin = "hgrn_w_in"
                else:
                    do, dg, dsum = fox_gate_bwd(dxm, s["o"], s["qg"], "fox_gate_bwd")
                    dsum4 = dsum[:, :n_fox].T.reshape(n_fox // 2, 2, T // tq, tq).transpose(0, 2, 1, 3)
                    dqt, dk_sh, dv_sh, dc4, dcq4 = fox_bwd(*fox_bwd_prep(s["qg"], kvf, do, tq), s["ka"], s["kb"], s["lse"], dsum4,
                                                           "fox_attn_bwd")
                    dq = dqt.transpose(1, 3, 0, 2).reshape(T, fox_w)
                    dzm = jnp.concatenate([dq.astype(BF), dg], axis=1)
                    nmin = "fox_w_qg"
                wmin = W[nmin, 0]
                grads[nmin, 0] = mm_tn(s["xn_mix"], dzm, f"{nm}_dwin", yb=wmin.shape[2])
                tokv = send_grads(f"mix_{i}", grads)
                grads = {}
                dh, dgpre = mm_nt_nbwd(dzm, wmin, s["h_a"], norm("mix_norm_pre", i) + tokv, dh, f"{nm}_bwd_in", tn=wmin.shape[2])
                gsmall["mix_norm_pre"][i] = dgpre
        if i == 1:
            dct = (dc4 + dcq4).transpose(0, 2, 1, 3).reshape(n_fox, T)
            dct = jnp.pad(dct, ((0, LANES - n_fox), (0, 0)))
            dflog, db = fox_cum_bwd(dct, kvf, fcol, b_row, "fox_cum_bwd")
            gsmall["fox_b_f"] = db[:, :n_fox]
            dkvf = jnp.concatenate([dk_sh, dv_sh, dflog], axis=1)
            dwk = mm_tn(xn_kv, dkvf, "fox_dwkvf", yb=kvf_w.shape[1])[0][:, :kvf_cols]
            tokv = send_grads("kvf", {("fox_w_kvf", 0): dwk.reshape(D, N_DEV, -1).transpose(1, 0, 2)})
            dh, dgkv = mm_nt_nbwd(dkvf, kvf_w, h_kv, _row(kv_norm) + tokv, dh, "fox_kvf_bwd", tn=kvf_w.shape[1])
            gsmall["kv_norm"] = dgkv
    grad_x = dh.reshape(x.shape)

    def small_rows(n):
        g = gsmall[n]
        rows = g if isinstance(g, list) else [g]
        return [_pad_lanes(r, D) for r in rows]

    counts = {n: len(small_rows(n)) for n in small_names}
    packed = jnp.concatenate([r for n in small_names for r in small_rows(n)] + [_pad_lanes(loss_part, D)], axis=0)
    loss_row = packed.shape[0] - 1
    packed = jnp.pad(packed, ((0, -packed.shape[0] % 8), (0, 0)))
    n_rows = packed.shape[0]
    send("small", ["small"], [packed], [True])

    res = {}
    after = [dh]
    small_parts = None
    for gname, keys, hd in sent:
        lands = exchange_wait(hd, after, f"scatter_wait_{gname}")
        after = []
        for key, parts in zip(keys, lands):
            if key == "small":
                small_parts = parts
                continue
            n, l = key
            w = weights[n]
            if w.ndim == 2:
                as3 = lambda a: a.reshape((1,) + a.shape)
            elif n in TRANSPOSED:
                as3 = lambda a: a.transpose(0, 2, 1)
            else:
                as3 = lambda a: a
            res[n] = adamw(parts, as3(w), as3(mom1[n]), as3(mom2[n]), l, res.get(n), f"adamw_{n}_{l}")
            after.append(res[n][1])
    for n in TRANSPOSED:
        res[n] = [a.transpose(0, 2, 1) for a in res[n]]

    def pack(d):
        rows = []
        for n in small_names:
            a = d[n]
            rows.append(_pad_lanes(a.reshape(-1, a.shape[-1]), D))
        a = jnp.concatenate(rows, axis=0)
        return jnp.pad(a, ((0, n_rows - a.shape[0]), (0, 0)))[None]

    sm = adamw(small_parts, pack(weights), pack(mom1), pack(mom2), 0, None, "adamw_small")
    loss = sm[0][0, loss_row, 0]
    off = 0
    for n in small_names:
        w = weights[n]
        res[n] = [a[0, off:off + counts[n], :w.shape[-1]].reshape(w.shape) for a in sm]
        off += counts[n]

    out = [loss, grad_x]
    for j in range(4):
        out += [res[n][j].reshape(weights[n].shape) for n in names]
    return tuple(out)
```
